```python
import math
import jax, jax.numpy as jnp
from jax import lax
import numpy as np

D_MODEL = 1024
BATCH = 8
SEQ = 8192
DEPTH = 2

N_MIXERS = 2
N_LRU_LAYERS = (DEPTH + 1) // 2
N_FOX_LAYERS = DEPTH // 2
EPS = 1e-6
LRU_WIDTH = 1536
LRU_BLOCKS = 12
LRU_BLOCK_W = LRU_WIDTH // LRU_BLOCKS
CONV_WIDTH = 4
LRU_C = 8.0
FOX_HEADS = 16
FOX_HEAD_DIM = 64
FOX_WIDTH = FOX_HEADS * FOX_HEAD_DIM
Q_BLOCK = 128
NEG_INF = -1e30

kernel_name = "hybrid_rglru_fox_interleaved"


def rms_norm(x, g):
    xf = x.astype(jnp.float32)
    y = xf * lax.rsqrt(jnp.mean(xf * xf, axis=-1, keepdims=True) + EPS)
    return (y * g.astype(jnp.float32)).astype(x.dtype)


def causal_depthwise_conv(x, w, b):
    c = x.shape[-1]
    y = lax.conv_general_dilated(
        x, w[:, None, :].astype(x.dtype), window_strides=(1,),
        padding=[(CONV_WIDTH - 1, 0)],
        dimension_numbers=("NWC", "WIO", "NWC"),
        feature_group_count=c)
    return y + b.astype(x.dtype)


def block_diag_linear(x, w, b):
    bsz, s, _ = x.shape
    xb = x.reshape(bsz, s, LRU_BLOCKS, LRU_BLOCK_W)
    y = jnp.einsum("bsnc,ncd->bsnd", xb, w.astype(x.dtype))
    return y.reshape(bsz, s, LRU_WIDTH) + b.astype(x.dtype)


def lru_mixer(h, w_in, conv_w, conv_b, wa, ba, wx, bx, a_param, w_out):
    u = h @ w_in.astype(h.dtype)
    xb, gate = u[..., :LRU_WIDTH], u[..., LRU_WIDTH:]
    xc = causal_depthwise_conv(xb, conv_w, conv_b)
    r = jax.nn.sigmoid(block_diag_linear(xc, wa, ba).astype(jnp.float32))
    i = jax.nn.sigmoid(block_diag_linear(xc, wx, bx).astype(jnp.float32))
    log_a = -LRU_C * r * jax.nn.softplus(-a_param.astype(jnp.float32))
    a = jnp.exp(log_a)
    mult = jnp.sqrt(-jnp.expm1(2.0 * log_a))
    bterm = mult * (i * xc.astype(jnp.float32))

    def combine(lhs, rhs):
        a1, b1 = lhs
        a2, b2 = rhs
        return a1 * a2, a2 * b1 + b2

    _, hs = lax.associative_scan(combine, (a, bterm), axis=1)
    y = hs.astype(h.dtype) * jax.nn.silu(gate)
    return y @ w_out.astype(h.dtype)


def fox_mixer(h, w_in, b_f, w_out):
    bsz, s, _ = h.shape
    u = h @ w_in.astype(h.dtype)
    q = u[..., 0 * FOX_WIDTH:1 * FOX_WIDTH].reshape(bsz, s, FOX_HEADS, FOX_HEAD_DIM)
    k = u[..., 1 * FOX_WIDTH:2 * FOX_WIDTH].reshape(bsz, s, FOX_HEADS, FOX_HEAD_DIM)
    v = u[..., 2 * FOX_WIDTH:3 * FOX_WIDTH].reshape(bsz, s, FOX_HEADS, FOX_HEAD_DIM)
    gate = u[..., 3 * FOX_WIDTH:4 * FOX_WIDTH]
    f_logit = u[..., 4 * FOX_WIDTH:].astype(jnp.float32) + b_f.astype(jnp.float32)
    cum = jnp.cumsum(jax.nn.log_sigmoid(f_logit), axis=1)
    ck = jnp.transpose(cum, (0, 2, 1))
    scale = 1.0 / math.sqrt(FOX_HEAD_DIM)
    n_blocks = s // Q_BLOCK
    qb = jnp.transpose(q.reshape(bsz, n_blocks, Q_BLOCK, FOX_HEADS, FOX_HEAD_DIM), (1, 0, 2, 3, 4))
    cqb = jnp.transpose(cum.reshape(bsz, n_blocks, Q_BLOCK, FOX_HEADS), (1, 0, 3, 2))
    starts = jnp.arange(n_blocks, dtype=jnp.int32) * Q_BLOCK
    kpos = jnp.arange(s, dtype=jnp.int32)
    kf = k.astype(jnp.float32)
    vf = v.astype(jnp.float32)

    def one_block(args):
        q_blk, cq_blk, start = args
        qpos = start + jnp.arange(Q_BLOCK, dtype=jnp.int32)
        logits = jnp.einsum("bqhd,bkhd->bhqk", q_blk.astype(jnp.float32), kf) * scale
        logits = logits + (cq_blk[..., :, None] - ck[:, :, None, :])
        mask = kpos[None, :] <= qpos[:, None]
        logits = jnp.where(mask[None, None], logits, NEG_INF)
        p = jax.nn.softmax(logits, axis=-1)
        return jnp.einsum("bhqk,bkhd->bqhd", p, vf)

    o = lax.map(one_block, (qb, cqb, starts))
    o = jnp.transpose(o, (1, 0, 2, 3, 4)).reshape(bsz, s, FOX_WIDTH).astype(h.dtype)
    y = o * jax.nn.silu(gate)
    return y @ w_out.astype(h.dtype)


def _fwd_setup_inputs(seed: int = 0) -> dict:
    key = jax.random.key(seed)
    ks = jax.random.split(key, 16)
    f32 = jnp.float32
    nl, nf = N_LRU_LAYERS, N_FOX_LAYERS
    x = jax.random.normal(ks[0], (BATCH, SEQ, D_MODEL), f32)
    norm_g = 1.0 + 0.05 * jax.random.normal(ks[1], (DEPTH, D_MODEL), f32)
    final_g = 1.0 + 0.05 * jax.random.normal(ks[2], (D_MODEL,), f32)
    lru_w_in = jax.random.normal(ks[3], (nl, D_MODEL, 2 * LRU_WIDTH), f32) * D_MODEL ** -0.5
    lru_conv_w = jax.random.normal(ks[4], (nl, CONV_WIDTH, LRU_WIDTH), f32) * CONV_WIDTH ** -0.5
    lru_conv_b = 0.02 * jax.random.normal(ks[5], (nl, LRU_WIDTH), f32)
    lru_wa = jax.random.normal(ks[6], (nl, LRU_BLOCKS, LRU_BLOCK_W, LRU_BLOCK_W), f32) * LRU_BLOCK_W ** -0.5
    lru_ba = 0.02 * jax.random.normal(ks[7], (nl, LRU_WIDTH), f32)
    lru_wx = jax.random.normal(ks[8], (nl, LRU_BLOCKS, LRU_BLOCK_W, LRU_BLOCK_W), f32) * LRU_BLOCK_W ** -0.5
    lru_bx = 0.02 * jax.random.normal(ks[9], (nl, LRU_WIDTH), f32)
    a_c = jax.random.uniform(ks[10], (nl, LRU_WIDTH), f32, minval=0.9, maxval=0.999)
    a0 = a_c ** (1.0 / LRU_C)
    lru_a_param = jnp.log(a0) - jnp.log1p(-a0)
    lru_w_out = jax.random.normal(ks[11], (nl, LRU_WIDTH, D_MODEL), f32) * LRU_WIDTH ** -0.5
    fox_w_in = jax.random.normal(ks[12], (nf, D_MODEL, 4 * FOX_WIDTH + FOX_HEADS), f32) * D_MODEL ** -0.5
    fox_b_f = 3.0 + 0.5 * jax.random.normal(ks[13], (nf, FOX_HEADS), f32)
    fox_w_out = jax.random.normal(ks[14], (nf, FOX_WIDTH, D_MODEL), f32) * FOX_WIDTH ** -0.5
    return {"x": x, "norm_g": norm_g, "final_g": final_g,
            "lru_w_in": lru_w_in, "lru_conv_w": lru_conv_w, "lru_conv_b": lru_conv_b,
            "lru_wa": lru_wa, "lru_ba": lru_ba, "lru_wx": lru_wx, "lru_bx": lru_bx,
            "lru_a_param": lru_a_param, "lru_w_out": lru_w_out,
            "fox_w_in": fox_w_in, "fox_b_f": fox_b_f, "fox_w_out": fox_w_out}


def _fwd_reference(x, norm_g, final_g, lru_w_in, lru_conv_w, lru_conv_b, lru_wa, lru_ba,
              lru_wx, lru_bx, lru_a_param, lru_w_out, fox_w_in, fox_b_f, fox_w_out):
    for i in range(DEPTH):
        h = rms_norm(x, norm_g[i])
        j = i // N_MIXERS
        if i % N_MIXERS == 0:
            x = x + lru_mixer(h, lru_w_in[j], lru_conv_w[j], lru_conv_b[j], lru_wa[j], lru_ba[j],
                              lru_wx[j], lru_bx[j], lru_a_param[j], lru_w_out[j])
        else:
            x = x + fox_mixer(h, fox_w_in[j], fox_b_f[j], fox_w_out[j])
    return rms_norm(x, final_g)


import jax as _jax
import jax.numpy as _jnp

TWIN_FORMAT = 'train_step'
FWD_PARAMS = ['x', 'norm_g', 'final_g', 'lru_w_in', 'lru_conv_w', 'lru_conv_b', 'lru_wa', 'lru_ba', 'lru_wx', 'lru_bx', 'lru_a_param', 'lru_w_out', 'fox_w_in', 'fox_b_f', 'fox_w_out']
TWIN_WEIGHTS = ['norm_g', 'final_g', 'lru_w_in', 'lru_conv_w', 'lru_conv_b', 'lru_wa', 'lru_ba', 'lru_wx', 'lru_bx', 'lru_a_param', 'lru_w_out', 'fox_w_in', 'fox_b_f', 'fox_w_out']
TWIN_DIFF_INPUT = 'x'
TWIN_INPUTS = ['x', 'norm_g', 'final_g', 'lru_w_in', 'lru_conv_w', 'lru_conv_b', 'lru_wa', 'lru_ba', 'lru_wx', 'lru_bx', 'lru_a_param', 'lru_w_out', 'fox_w_in', 'fox_b_f', 'fox_w_out', 'loss_target', 'm_norm_g', 'm_final_g', 'm_lru_w_in', 'm_lru_conv_w', 'm_lru_conv_b', 'm_lru_wa', 'm_lru_ba', 'm_lru_wx', 'm_lru_bx', 'm_lru_a_param', 'm_lru_w_out', 'm_fox_w_in', 'm_fox_b_f', 'm_fox_w_out', 'v_norm_g', 'v_final_g', 'v_lru_w_in', 'v_lru_conv_w', 'v_lru_conv_b', 'v_lru_wa', 'v_lru_ba', 'v_lru_wx', 'v_lru_bx', 'v_lru_a_param', 'v_lru_w_out', 'v_fox_w_in', 'v_fox_b_f', 'v_fox_w_out']
TWIN_OUTPUTS = ['loss', 'grad_x', 'grad_norm_g', 'grad_final_g', 'grad_lru_w_in', 'grad_lru_conv_w', 'grad_lru_conv_b', 'grad_lru_wa', 'grad_lru_ba', 'grad_lru_wx', 'grad_lru_bx', 'grad_lru_a_param', 'grad_lru_w_out', 'grad_fox_w_in', 'grad_fox_b_f', 'grad_fox_w_out', 'delta_norm_g', 'delta_final_g', 'delta_lru_w_in', 'delta_lru_conv_w', 'delta_lru_conv_b', 'delta_lru_wa', 'delta_lru_ba', 'delta_lru_wx', 'delta_lru_bx', 'delta_lru_a_param', 'delta_lru_w_out', 'delta_fox_w_in', 'delta_fox_b_f', 'delta_fox_w_out', 'new_m_norm_g', 'new_m_final_g', 'new_m_lru_w_in', 'new_m_lru_conv_w', 'new_m_lru_conv_b', 'new_m_lru_wa', 'new_m_lru_ba', 'new_m_lru_wx', 'new_m_lru_bx', 'new_m_lru_a_param', 'new_m_lru_w_out', 'new_m_fox_w_in', 'new_m_fox_b_f', 'new_m_fox_w_out', 'new_v_norm_g', 'new_v_final_g', 'new_v_lru_w_in', 'new_v_lru_conv_w', 'new_v_lru_conv_b', 'new_v_lru_wa', 'new_v_lru_ba', 'new_v_lru_wx', 'new_v_lru_bx', 'new_v_lru_a_param', 'new_v_lru_w_out', 'new_v_fox_w_in', 'new_v_fox_b_f', 'new_v_fox_w_out']
TWIN_LEAF_KINDS = {'loss': 'loss', 'grad_x': 'grad_x', 'grad_norm_g': 'grad_w', 'grad_final_g': 'grad_w', 'grad_lru_w_in': 'grad_w', 'grad_lru_conv_w': 'grad_w', 'grad_lru_conv_b': 'grad_w', 'grad_lru_wa': 'grad_w', 'grad_lru_ba': 'grad_w', 'grad_lru_wx': 'grad_w', 'grad_lru_bx': 'grad_w', 'grad_lru_a_param': 'grad_w', 'grad_lru_w_out': 'grad_w', 'grad_fox_w_in': 'grad_w', 'grad_fox_b_f': 'grad_w', 'grad_fox_w_out': 'grad_w', 'delta_norm_g': 'delta_w', 'delta_final_g': 'delta_w', 'delta_lru_w_in': 'delta_w', 'delta_lru_conv_w': 'delta_w', 'delta_lru_conv_b': 'delta_w', 'delta_lru_wa': 'delta_w', 'delta_lru_ba': 'delta_w', 'delta_lru_wx': 'delta_w', 'delta_lru_bx': 'delta_w', 'delta_lru_a_param': 'delta_w', 'delta_lru_w_out': 'delta_w', 'delta_fox_w_in': 'delta_w', 'delta_fox_b_f': 'delta_w', 'delta_fox_w_out': 'delta_w', 'new_m_norm_g': 'new_m', 'new_m_final_g': 'new_m', 'new_m_lru_w_in': 'new_m', 'new_m_lru_conv_w': 'new_m', 'new_m_lru_conv_b': 'new_m', 'new_m_lru_wa': 'new_m', 'new_m_lru_ba': 'new_m', 'new_m_lru_wx': 'new_m', 'new_m_lru_bx': 'new_m', 'new_m_lru_a_param': 'new_m', 'new_m_lru_w_out': 'new_m', 'new_m_fox_w_in': 'new_m', 'new_m_fox_b_f': 'new_m', 'new_m_fox_w_out': 'new_m', 'new_v_norm_g': 'new_v', 'new_v_final_g': 'new_v', 'new_v_lru_w_in': 'new_v', 'new_v_lru_conv_w': 'new_v', 'new_v_lru_conv_b': 'new_v', 'new_v_lru_wa': 'new_v', 'new_v_lru_ba': 'new_v', 'new_v_lru_wx': 'new_v', 'new_v_lru_bx': 'new_v', 'new_v_lru_a_param': 'new_v', 'new_v_lru_w_out': 'new_v', 'new_v_fox_w_in': 'new_v', 'new_v_fox_b_f': 'new_v', 'new_v_fox_w_out': 'new_v'}


def _forward(args):
    return _fwd_reference(*[args[k] for k in FWD_PARAMS])


def _output_shape():
    def fwd():
        inp = _fwd_setup_inputs(0)
        return _fwd_reference(*[inp[k] for k in FWD_PARAMS])
    out = _jax.eval_shape(fwd)
    return out.shape, out.dtype

N_MICROBATCH = 1
ADAM_LR = 0.001
ADAM_B1 = 0.9
ADAM_B2 = 0.999
ADAM_EPS = 1e-08
ADAM_WD = 0.01
ADAM_STEP = 10
PER_EXAMPLE_BATCH_AXIS = {'x': 0, 'loss_target': 0}
SHARED_INPUTS = []
_WEIGHT_DTYPES = {'norm_g': _jnp.float32, 'final_g': _jnp.float32, 'lru_w_in': _jnp.float32, 'lru_conv_w': _jnp.float32, 'lru_conv_b': _jnp.float32, 'lru_wa': _jnp.float32, 'lru_ba': _jnp.float32, 'lru_wx': _jnp.float32, 'lru_bx': _jnp.float32, 'lru_a_param': _jnp.float32, 'lru_w_out': _jnp.float32, 'fox_w_in': _jnp.float32, 'fox_b_f': _jnp.float32, 'fox_w_out': _jnp.float32}
MOMENT_SCALE = {'norm_g': 1.118936e-01, 'final_g': 6.401414e+01, 'lru_w_in': 8.107638e-02, 'lru_conv_w': 8.360149e-02, 'lru_conv_b': 1.113975e+00, 'lru_wa': 2.692599e-02, 'lru_ba': 2.198582e-02, 'lru_wx': 4.910580e-02, 'lru_bx': 3.165079e-02, 'lru_a_param': 4.304362e-02, 'lru_w_out': 1.076595e-01, 'fox_w_in': 4.047800e-02, 'fox_b_f': 2.320114e-01, 'fox_w_out': 4.444398e-02}


def _to_microbatches(a, axis):
    t = _jnp.moveaxis(a, axis, 0)
    t = t.reshape((N_MICROBATCH, t.shape[0] // N_MICROBATCH) + t.shape[1:])
    return _jnp.moveaxis(t, 1, axis + 1)


def setup_inputs(seed: int = 0) -> dict:
    inp = _fwd_setup_inputs(seed)
    key = _jax.random.fold_in(_jax.random.key(seed), 7919)
    shape, _ = _output_shape()
    out = dict(inp)
    out["loss_target"] = _jax.random.normal(_jax.random.fold_in(key, 0), shape, _jnp.float32)
    for i, name in enumerate(TWIN_WEIGHTS):
        w = inp[name].astype(_jnp.float32)
        if MOMENT_SCALE is None:
            s = _jnp.sqrt(_jnp.mean(_jnp.square(w)) + 1e-30)
        else:
            s = MOMENT_SCALE[name]
        km, kv = _jax.random.split(_jax.random.fold_in(key, i + 1))
        out[name] = w
        out["m_" + name] = s * _jax.random.normal(km, w.shape, _jnp.float32)
        out["v_" + name] = (s * s) * _jax.random.uniform(kv, w.shape, _jnp.float32, 0.5, 1.5)
    if N_MICROBATCH > 1:
        for name, axis in PER_EXAMPLE_BATCH_AXIS.items():
            out[name] = _to_microbatches(out[name], axis)
    return {'x': out['x'], 'norm_g': out['norm_g'], 'final_g': out['final_g'], 'lru_w_in': out['lru_w_in'], 'lru_conv_w': out['lru_conv_w'], 'lru_conv_b': out['lru_conv_b'], 'lru_wa': out['lru_wa'], 'lru_ba': out['lru_ba'], 'lru_wx': out['lru_wx'], 'lru_bx': out['lru_bx'], 'lru_a_param': out['lru_a_param'], 'lru_w_out': out['lru_w_out'], 'fox_w_in': out['fox_w_in'], 'fox_b_f': out['fox_b_f'], 'fox_w_out': out['fox_w_out'], 'loss_target': out['loss_target'], 'm_norm_g': out['m_norm_g'], 'm_final_g': out['m_final_g'], 'm_lru_w_in': out['m_lru_w_in'], 'm_lru_conv_w': out['m_lru_conv_w'], 'm_lru_conv_b': out['m_lru_conv_b'], 'm_lru_wa': out['m_lru_wa'], 'm_lru_ba': out['m_lru_ba'], 'm_lru_wx': out['m_lru_wx'], 'm_lru_bx': out['m_lru_bx'], 'm_lru_a_param': out['m_lru_a_param'], 'm_lru_w_out': out['m_lru_w_out'], 'm_fox_w_in': out['m_fox_w_in'], 'm_fox_b_f': out['m_fox_b_f'], 'm_fox_w_out': out['m_fox_w_out'], 'v_norm_g': out['v_norm_g'], 'v_final_g': out['v_final_g'], 'v_lru_w_in': out['v_lru_w_in'], 'v_lru_conv_w': out['v_lru_conv_w'], 'v_lru_conv_b': out['v_lru_conv_b'], 'v_lru_wa': out['v_lru_wa'], 'v_lru_ba': out['v_lru_ba'], 'v_lru_wx': out['v_lru_wx'], 'v_lru_bx': out['v_lru_bx'], 'v_lru_a_param': out['v_lru_a_param'], 'v_lru_w_out': out['v_lru_w_out'], 'v_fox_w_in': out['v_fox_w_in'], 'v_fox_b_f': out['v_fox_b_f'], 'v_fox_w_out': out['v_fox_w_out']}


def _loss(weights, diff, rest, loss_target):
    with _jax.named_scope("forward"):
        args = {**rest, TWIN_DIFF_INPUT: diff, **{k: w.astype(_WEIGHT_DTYPES[k]) for k, w in weights.items()}}
        y = _forward(args)
    with _jax.named_scope("loss_head"):
        err = _jnp.square(y.astype(_jnp.float32) - loss_target)
        return 0.5 * _jnp.sum(_jnp.mean(err, axis=-1)) if err.ndim else 0.5 * err


def _adamw(w, g, m, v):
    m = ADAM_B1 * m + (1.0 - ADAM_B1) * g
    v = ADAM_B2 * v + (1.0 - ADAM_B2) * _jnp.square(g)
    m_hat = m / (1.0 - ADAM_B1 ** ADAM_STEP)
    v_hat = v / (1.0 - ADAM_B2 ** ADAM_STEP)
    delta = -ADAM_LR * (m_hat / (_jnp.sqrt(v_hat) + ADAM_EPS) + ADAM_WD * w)
    return delta, m, v


def reference(x, norm_g, final_g, lru_w_in, lru_conv_w, lru_conv_b, lru_wa, lru_ba, lru_wx, lru_bx, lru_a_param, lru_w_out, fox_w_in, fox_b_f, fox_w_out, loss_target, m_norm_g, m_final_g, m_lru_w_in, m_lru_conv_w, m_lru_conv_b, m_lru_wa, m_lru_ba, m_lru_wx, m_lru_bx, m_lru_a_param, m_lru_w_out, m_fox_w_in, m_fox_b_f, m_fox_w_out, v_norm_g, v_final_g, v_lru_w_in, v_lru_conv_w, v_lru_conv_b, v_lru_wa, v_lru_ba, v_lru_wx, v_lru_bx, v_lru_a_param, v_lru_w_out, v_fox_w_in, v_fox_b_f, v_fox_w_out):
    given = dict(x=x, norm_g=norm_g, final_g=final_g, lru_w_in=lru_w_in, lru_conv_w=lru_conv_w, lru_conv_b=lru_conv_b, lru_wa=lru_wa, lru_ba=lru_ba, lru_wx=lru_wx, lru_bx=lru_bx, lru_a_param=lru_a_param, lru_w_out=lru_w_out, fox_w_in=fox_w_in, fox_b_f=fox_b_f, fox_w_out=fox_w_out, loss_target=loss_target, m_norm_g=m_norm_g, m_final_g=m_final_g, m_lru_w_in=m_lru_w_in, m_lru_conv_w=m_lru_conv_w, m_lru_conv_b=m_lru_conv_b, m_lru_wa=m_lru_wa, m_lru_ba=m_lru_ba, m_lru_wx=m_lru_wx, m_lru_bx=m_lru_bx, m_lru_a_param=m_lru_a_param, m_lru_w_out=m_lru_w_out, m_fox_w_in=m_fox_w_in, m_fox_b_f=m_fox_b_f, m_fox_w_out=m_fox_w_out, v_norm_g=v_norm_g, v_final_g=v_final_g, v_lru_w_in=v_lru_w_in, v_lru_conv_w=v_lru_conv_w, v_lru_conv_b=v_lru_conv_b, v_lru_wa=v_lru_wa, v_lru_ba=v_lru_ba, v_lru_wx=v_lru_wx, v_lru_bx=v_lru_bx, v_lru_a_param=v_lru_a_param, v_lru_w_out=v_lru_w_out, v_fox_w_in=v_fox_w_in, v_fox_b_f=v_fox_b_f, v_fox_w_out=v_fox_w_out)
    weights = {n: given[n] for n in TWIN_WEIGHTS}
    shared = {n: given[n] for n in SHARED_INPUTS}
    per_example = {n: given[n] for n in ['x']}
    grad_fn = _jax.value_and_grad(_loss, argnums=(0, 1))

    def one_microbatch(ex, loss_target):
        ex = dict(ex)
        diff = ex.pop(TWIN_DIFF_INPUT)
        return grad_fn(weights, diff, {**shared, **ex}, loss_target)

    if N_MICROBATCH == 1:
        loss, (grad_w, grad_x) = one_microbatch(per_example, given["loss_target"])
    else:
        def body(carry, xs):
            loss_sum, grad_sum = carry
            l_k, (gw_k, gx_k) = one_microbatch(xs[0], xs[1])
            with _jax.named_scope("update"):
                return (loss_sum + l_k, _jax.tree.map(_jnp.add, grad_sum, gw_k)), gx_k

        init = (_jnp.zeros((), _jnp.float32), _jax.tree.map(_jnp.zeros_like, weights))
        (loss, grad_w), grad_x = _jax.lax.scan(body, init, (per_example, given["loss_target"]))
    with _jax.named_scope("update"):
        delta_w, new_m, new_v = {}, {}, {}
        for n in TWIN_WEIGHTS:
            delta_w[n], new_m[n], new_v[n] = _adamw(weights[n], grad_w[n], given["m_" + n], given["v_" + n])
    return (loss, grad_x, *[grad_w[n] for n in TWIN_WEIGHTS], *[delta_w[n] for n in TWIN_WEIGHTS],
            *[new_m[n] for n in TWIN_WEIGHTS], *[new_v[n] for n in TWIN_WEIGHTS])
```

```python
import functools
import math

import jax
import jax.numpy as jnp
from jax import lax
from jax.experimental import pallas as pl
from jax.experimental.pallas import tpu as pltpu

F32 = jnp.float32
BF16 = jnp.bfloat16

EPS = 1e-6
LRU_C = 8.0
LRU_BLOCK_W = 128
CONV_WIDTH = 4
FOX_HEADS = 16
FOX_HEAD_DIM = 64
NEG_INF = -1e30
ADAM_LR = 0.001
ADAM_B1 = 0.9
ADAM_B2 = 0.999
ADAM_EPS = 1e-08
ADAM_WD = 0.01
ADAM_STEP = 10

LANES = 128
SUBLANES = 8
VMEM_LIMIT = 56 * 1024 * 1024
N_CHIPS = 4
N_DEV = 8
MESH = pl.DeviceIdType.MESH
HIGHEST = lax.Precision.HIGHEST


def _tile(n, pref):
    t = min(n, pref)
    while n % t:
        t //= 2
    return t


def _cparams(dims=None):
    return pltpu.CompilerParams(dimension_semantics=dims, vmem_limit_bytes=VMEM_LIMIT)


def _sigmoid(x):
    return jax.nn.sigmoid(x)


def _log1p(x):
    u = 1.0 + x
    return jnp.where(u == 1.0, x, jnp.log(u) * x / (u - 1.0))


def _softplus(x):
    return jnp.maximum(x, 0.0) + _log1p(jnp.exp(-jnp.abs(x)))


def _matmul(a, b, *, name, ta=False, tb=False, out_dtype=F32, add=None, tm=512, tn=512, tk=512):
    if ta:
        kdim, m = a.shape
    else:
        m, kdim = a.shape
    if tb:
        n, kb = b.shape
    else:
        kb, n = b.shape
    assert kdim == kb, (a.shape, b.shape, ta, tb)
    tm, tn, tk = _tile(m, tm), _tile(n, tn), _tile(kdim, tk)
    nk = kdim // tk
    dn = (((0 if ta else 1,), (1 if tb else 0,)), ((), ()))
    has_add = add is not None

    def body(*refs):
        if has_add:
            a_ref, b_ref, add_ref, o_ref, acc_ref = refs
        else:
            a_ref, b_ref, o_ref, acc_ref = refs
        k = pl.program_id(2)

        @pl.when(k == 0)
        def _():
            acc_ref[...] = jnp.zeros_like(acc_ref)

        acc_ref[...] += lax.dot_general(a_ref[...].astype(BF16), b_ref[...].astype(BF16), dn,
                                        preferred_element_type=F32)

        @pl.when(k == nk - 1)
        def _():
            r = acc_ref[...]
            if has_add:
                r = r + add_ref[...].astype(F32)
            o_ref[...] = r.astype(o_ref.dtype)

    a_spec = pl.BlockSpec((tk, tm), lambda i, j, k: (k, i)) if ta else pl.BlockSpec((tm, tk), lambda i, j, k: (i, k))
    b_spec = pl.BlockSpec((tn, tk), lambda i, j, k: (j, k)) if tb else pl.BlockSpec((tk, tn), lambda i, j, k: (k, j))
    o_spec = pl.BlockSpec((tm, tn), lambda i, j, k: (i, j))
    in_specs = [a_spec, b_spec] + ([o_spec] if has_add else [])
    args = (a, b) + ((add,) if has_add else ())
    return pl.pallas_call(
        body, name=name, grid=(m // tm, n // tn, nk), in_specs=in_specs, out_specs=o_spec,
        out_shape=jax.ShapeDtypeStruct((m, n), out_dtype),
        scratch_shapes=[pltpu.VMEM((tm, tn), F32)],
        compiler_params=_cparams(("parallel", "parallel", "arbitrary")),
    )(*args)


def _rmsnorm(x, g, *, name):
    t, d = x.shape
    tt = _tile(t, 512)

    def body(x_ref, g_ref, o_ref):
        xf = x_ref[...]
        rstd = lax.rsqrt(jnp.mean(xf * xf, axis=-1, keepdims=True) + EPS)
        o_ref[...] = (xf * rstd * g_ref[...]).astype(o_ref.dtype)

    return pl.pallas_call(
        body, name=name, grid=(t // tt,),
        in_specs=[pl.BlockSpec((tt, d), lambda i: (i, 0)), pl.BlockSpec((1, d), lambda i: (0, 0))],
        out_specs=pl.BlockSpec((tt, d), lambda i: (i, 0)),
        out_shape=jax.ShapeDtypeStruct((t, d), BF16),
        compiler_params=_cparams(("parallel",)),
    )(x, g.reshape(1, d))


def _rmsnorm_bwd(dh, x, g, dres, *, name):
    t, d = x.shape
    tt = _tile(t, 512)

    def body(dh_ref, x_ref, g_ref, dres_ref, dx_ref, dg_ref):
        i = pl.program_id(0)

        @pl.when(i == 0)
        def _():
            dg_ref[...] = jnp.zeros_like(dg_ref)

        xf = x_ref[...]
        rstd = lax.rsqrt(jnp.mean(xf * xf, axis=-1, keepdims=True) + EPS)
        xhat = xf * rstd
        dhf = dh_ref[...].astype(F32)
        dxhat = dhf * g_ref[...]
        mt = jnp.mean(dxhat * xhat, axis=-1, keepdims=True)
        dx_ref[...] = dres_ref[...] + rstd * (dxhat - xhat * mt)
        dg_ref[...] += jnp.sum(dhf * xhat, axis=0, keepdims=True)

    blk = pl.BlockSpec((tt, d), lambda i: (i, 0))
    vec = pl.BlockSpec((1, d), lambda i: (0, 0))
    return pl.pallas_call(
        body, name=name, grid=(t // tt,),
        in_specs=[blk, blk, vec, blk], out_specs=[blk, vec],
        out_shape=[jax.ShapeDtypeStruct((t, d), F32), jax.ShapeDtypeStruct((1, d), F32)],
        compiler_params=_cparams(("arbitrary",)),
    )(dh, x, g.reshape(1, d), dres)


def _final_loss(x2, tgt, g, *, name):
    t, d = x2.shape
    tt = _tile(t, 512)

    def body(x_ref, t_ref, g_ref, l_ref, dx_ref, dg_ref):
        i = pl.program_id(0)

        @pl.when(i == 0)
        def _():
            dg_ref[...] = jnp.zeros_like(dg_ref)
            l_ref[...] = jnp.zeros_like(l_ref)

        xf = x_ref[...]
        gg = g_ref[...]
        rstd = lax.rsqrt(jnp.mean(xf * xf, axis=-1, keepdims=True) + EPS)
        xhat = xf * rstd
        err = xhat * gg - t_ref[...]
        l_ref[...] += jnp.sum(err * err, axis=0, keepdims=True)
        dy = err * (1.0 / d)
        dxhat = dy * gg
        mt = jnp.mean(dxhat * xhat, axis=-1, keepdims=True)
        dx_ref[...] = rstd * (dxhat - xhat * mt)
        dg_ref[...] += jnp.sum(dy * xhat, axis=0, keepdims=True)

    blk = pl.BlockSpec((tt, d), lambda i: (i, 0))
    vec = pl.BlockSpec((1, d), lambda i: (0, 0))
    return pl.pallas_call(
        body, name=name, grid=(t // tt,),
        in_specs=[blk, blk, vec], out_specs=[vec, blk, vec],
        out_shape=[jax.ShapeDtypeStruct((1, d), F32), jax.ShapeDtypeStruct((t, d), F32),
                   jax.ShapeDtypeStruct((1, d), F32)],
        compiler_params=_cparams(("arbitrary",)),
    )(x2, tgt, g.reshape(1, d))


def _shift_down(prev8, cur, s):
    ext = jnp.concatenate([prev8, cur], axis=0)
    if s == 0:
        return cur
    return pltpu.roll(ext, s, 0)[SUBLANES:, :]


def _shift_up(cur, next8, s):
    if s == 0:
        return cur
    n = cur.shape[0]
    ext = jnp.concatenate([cur, next8], axis=0)
    return pltpu.roll(ext, n + SUBLANES - s, 0)[:n, :]


def _lru_gates(xc, wa, ba, wx, bx, sp):
    xcb = xc.astype(BF16)
    r = _sigmoid(jnp.dot(xcb, wa, preferred_element_type=F32) + ba)
    ig = _sigmoid(jnp.dot(xcb, wx, preferred_element_type=F32) + bx)
    log_a = -LRU_C * r * sp
    a = jnp.exp(log_a)
    mult = jnp.sqrt(-jnp.tanh(log_a) * (a * a + 1.0))
    return r, ig, a, mult


def _lru_specs(tt, cg, n_groups, nt, reverse):
    ncol = cg // LANES
    if reverse:
        ti = lambda i: nt - 1 - i
    else:
        ti = lambda i: i
    hb = tt // SUBLANES
    cur = lambda col: pl.BlockSpec((tt, cg), lambda g, i: (ti(i), 2 * g + col))
    prev = lambda col: pl.BlockSpec((SUBLANES, cg), lambda g, i: (jnp.maximum(ti(i) * hb - 1, 0), 2 * g + col))
    chan = lambda rows: pl.BlockSpec((rows, cg), lambda g, i: (0, g))
    wblk = pl.BlockSpec((ncol, LRU_BLOCK_W, LRU_BLOCK_W), lambda g, i: (g, 0, 0))
    plain = pl.BlockSpec((tt, cg), lambda g, i: (ti(i), g))
    plain_prev = pl.BlockSpec((SUBLANES, cg), lambda g, i: (jnp.maximum(ti(i) * hb - 1, 0), g))
    return cur, prev, chan, wblk, plain, plain_prev


def _lru_fwd(u, conv_w, conv_b, wa, ba, wx, bx, a_param, *, cg, name):
    t, w2 = u.shape
    w = w2 // 2
    n_groups = w // cg
    ncol = cg // LANES
    tt = _tile(t, 256)
    nt = t // tt
    cur, prev, chan, wblk, plain, _ = _lru_specs(tt, cg, n_groups, nt, False)

    def body(xb_ref, xp_ref, gate_ref, cw_ref, cb_ref, wa_ref, ba_ref, wx_ref, bx_ref, ap_ref,
             y_ref, hs_ref, h_ref, a_s, b_s):
        i = pl.program_id(1)

        @pl.when(i == 0)
        def _():
            h_ref[...] = jnp.zeros_like(h_ref)

        keep = (i > 0).astype(F32)
        for n in range(ncol):
            sl = slice(n * LANES, (n + 1) * LANES)
            xb = xb_ref[:, sl]
            xp = xp_ref[:, sl] * keep
            xc = cb_ref[:, sl] + cw_ref[3:4, sl] * xb
            for s in range(1, CONV_WIDTH):
                xc = xc + cw_ref[3 - s:4 - s, sl] * _shift_down(xp, xb, s)
            sp = _softplus(-ap_ref[:, sl])
            _, ig, a, mult = _lru_gates(xc, wa_ref[n].astype(BF16), ba_ref[:, sl],
                                        wx_ref[n].astype(BF16), bx_ref[:, sl], sp)
            a_s[:, sl] = a
            b_s[:, sl] = mult * (ig * xc)

        def step(tr, h):
            h = a_s[pl.ds(tr, 1), :] * h + b_s[pl.ds(tr, 1), :]
            hs_ref[pl.ds(tr, 1), :] = h
            return h

        h = lax.fori_loop(0, tt, step, h_ref[0:1, :], unroll=8)
        h_ref[0:1, :] = h
        gate = gate_ref[...]
        y_ref[...] = (hs_ref[...] * (gate * _sigmoid(gate))).astype(y_ref.dtype)

    return pl.pallas_call(
        body, name=name, grid=(n_groups, nt),
        in_specs=[cur(0), prev(0), cur(1), chan(CONV_WIDTH), chan(1), wblk, chan(1), wblk, chan(1), chan(1)],
        out_specs=[plain, plain],
        out_shape=[jax.ShapeDtypeStruct((t, w), BF16), jax.ShapeDtypeStruct((t, w), F32)],
        scratch_shapes=[pltpu.VMEM((SUBLANES, cg), F32), pltpu.VMEM((tt, cg), F32), pltpu.VMEM((tt, cg), F32)],
        compiler_params=_cparams(("parallel", "arbitrary")),
    )(u, u, u, conv_w, conv_b, wa, ba, wx, bx, a_param)


def _lru_bwd(u, hs, dy, conv_w, conv_b, wa, ba, wx, bx, a_param, *, cg, name):
    t, w2 = u.shape
    w = w2 // 2
    n_groups = w // cg
    ncol = cg // LANES
    tt = _tile(t, 256)
    nt = t // tt
    cur, prev, chan, wblk, plain, plain_prev = _lru_specs(tt, cg, n_groups, nt, True)
    tn_dims = (((0,), (0,)), ((), ()))
    nt_dims = (((1,), (1,)), ((), ()))

    def body(xb_ref, xp_ref, gate_ref, hs_ref, hp_ref, dy_ref, cw_ref, cb_ref, wa_ref, ba_ref, wx_ref, bx_ref,
             ap_ref, dxb_ref, dgate_ref, dcw_ref, dcb_ref, dwa_ref, dba_ref, dwx_ref, dbx_ref, dsp_ref,
             c_ref, nx_ref, a_s, dhs_s, lam_s):
        i = pl.program_id(1)
        first_time_block = i == nt - 1

        @pl.when(i == 0)
        def _():
            c_ref[...] = jnp.zeros_like(c_ref)
            nx_ref[...] = jnp.zeros_like(nx_ref)
            for r in (dcw_ref, dcb_ref, dwa_ref, dba_ref, dwx_ref, dbx_ref, dsp_ref):
                r[...] = jnp.zeros_like(r)

        keep = jnp.where(first_time_block, 0.0, 1.0).astype(F32)
        gate = gate_ref[...]
        sg = _sigmoid(gate)
        dyv = dy_ref[...]
        hsv = hs_ref[...]
        dhs_s[...] = dyv * (gate * sg)
        dgate_ref[...] = (dyv * hsv * (sg * (1.0 + gate * (1.0 - sg)))).astype(dgate_ref.dtype)

        saved = []
        for n in range(ncol):
            sl = slice(n * LANES, (n + 1) * LANES)
            xb = xb_ref[:, sl]
            xp = xp_ref[:, sl] * keep
            shifted = [xb] + [_shift_down(xp, xb, s) for s in range(1, CONV_WIDTH)]
            xc = cb_ref[:, sl] + cw_ref[3:4, sl] * xb
            for s in range(1, CONV_WIDTH):
                xc = xc + cw_ref[3 - s:4 - s, sl] * shifted[s]
            sp = _softplus(-ap_ref[:, sl])
            wab = wa_ref[n].astype(BF16)
            wxb = wx_ref[n].astype(BF16)
            r, ig, a, mult = _lru_gates(xc, wab, ba_ref[:, sl], wxb, bx_ref[:, sl], sp)
            a_s[:, sl] = a
            saved.append((sl, shifted, xc, sp, wab, wxb, r, ig, a, mult))

        def step(k, c):
            tr = tt - 1 - k
            lam = dhs_s[pl.ds(tr, 1), :] + c
            lam_s[pl.ds(tr, 1), :] = lam
            return a_s[pl.ds(tr, 1), :] * lam

        c_ref[0:1, :] = lax.fori_loop(0, tt, step, c_ref[0:1, :], unroll=8)

        for n in range(ncol):
            sl, shifted, xc, sp, wab, wxb, r, ig, a, mult = saved[n]
            lam = lam_s[:, sl]
            hprev = _shift_down(hp_ref[:, sl] * keep, hs_ref[:, sl], 1)
            da = lam * hprev
            dmult = lam * (ig * xc)
            dlog_a = da * a - dmult * (a * a / mult)
            di = lam * (mult * xc)
            dxc = lam * (mult * ig)
            dr = dlog_a * (-LRU_C * sp)
            dsp_ref[:, sl] += jnp.sum(dlog_a * (-LRU_C * r), axis=0, keepdims=True)
            dza = dr * (r * (1.0 - r))
            dzx = di * (ig * (1.0 - ig))
            dba_ref[:, sl] += jnp.sum(dza, axis=0, keepdims=True)
            dbx_ref[:, sl] += jnp.sum(dzx, axis=0, keepdims=True)
            xcb = xc.astype(BF16)
            dzab = dza.astype(BF16)
            dzxb = dzx.astype(BF16)
            dwa_ref[n] += lax.dot_general(xcb, dzab, tn_dims, preferred_element_type=F32)
            dwx_ref[n] += lax.dot_general(xcb, dzxb, tn_dims, preferred_element_type=F32)
            dxc = dxc + lax.dot_general(dzab, wab, nt_dims, preferred_element_type=F32)
            dxc = dxc + lax.dot_general(dzxb, wxb, nt_dims, preferred_element_type=F32)
            dcb_ref[:, sl] += jnp.sum(dxc, axis=0, keepdims=True)
            for s in range(CONV_WIDTH):
                dcw_ref[3 - s:4 - s, sl] += jnp.sum(dxc * shifted[s], axis=0, keepdims=True)
            nx = nx_ref[:, sl]
            dxb = cw_ref[3:4, sl] * dxc
            for s in range(1, CONV_WIDTH):
                dxb = dxb + cw_ref[3 - s:4 - s, sl] * _shift_up(dxc, nx, s)
            dxb_ref[:, sl] = dxb.astype(dxb_ref.dtype)
            nx_ref[:, sl] = dxc[0:SUBLANES, :]

        @pl.when(first_time_block)
        def _():
            dsp_ref[...] = dsp_ref[...] * (-_sigmoid(-ap_ref[...]))

    dxb_spec = pl.BlockSpec((tt, cg), lambda g, i: (nt - 1 - i, g))
    outs = pl.pallas_call(
        body, name=name, grid=(n_groups, nt),
        in_specs=[cur(0), prev(0), cur(1), plain, plain_prev, plain, chan(CONV_WIDTH), chan(1), wblk, chan(1), wblk,
                  chan(1), chan(1)],
        out_specs=[dxb_spec, dxb_spec, chan(CONV_WIDTH), chan(1), wblk, chan(1), wblk, chan(1), chan(1)],
        out_shape=[jax.ShapeDtypeStruct((t, w), BF16), jax.ShapeDtypeStruct((t, w), BF16),
                   jax.ShapeDtypeStruct(conv_w.shape, F32), jax.ShapeDtypeStruct(conv_b.shape, F32),
                   jax.ShapeDtypeStruct(wa.shape, F32), jax.ShapeDtypeStruct(ba.shape, F32),
                   jax.ShapeDtypeStruct(wx.shape, F32), jax.ShapeDtypeStruct(bx.shape, F32),
                   jax.ShapeDtypeStruct(a_param.shape, F32)],
        scratch_shapes=[pltpu.VMEM((SUBLANES, cg), F32), pltpu.VMEM((SUBLANES, cg), F32),
                        pltpu.VMEM((tt, cg), F32), pltpu.VMEM((tt, cg), F32), pltpu.VMEM((tt, cg), F32)],
        compiler_params=_cparams(("parallel", "arbitrary")),
    )(u, u, u, hs, hs, dy, conv_w, conv_b, wa, ba, wx, bx, a_param)
    return outs


def _fgate_fwd(f, b_f, *, name):
    t, n = f.shape
    tt = _tile(t, 256)

    def body(f_ref, b_ref, cum_ref, carry_ref):
        i = pl.program_id(0)

        @pl.when(i == 0)
        def _():
            carry_ref[...] = jnp.zeros_like(carry_ref)

        z = f_ref[...] + b_ref[...]
        lf = jnp.minimum(z, 0.0) - _log1p(jnp.exp(-jnp.abs(z)))
        row = lax.broadcasted_iota(jnp.int32, (tt, tt), 0)
        col = lax.broadcasted_iota(jnp.int32, (tt, tt), 1)
        tri = (col <= row).astype(F32)
        cum = jnp.dot(tri, lf, precision=HIGHEST, preferred_element_type=F32) + carry_ref[0:1, :]
        cum_ref[...] = cum
        carry_ref[0:1, :] = cum[tt - 1:tt, :]

    return pl.pallas_call(
        body, name=name, grid=(t // tt,),
        in_specs=[pl.BlockSpec((tt, n), lambda i: (i, 0)), pl.BlockSpec((1, n), lambda i: (0, 0))],
        out_specs=pl.BlockSpec((tt, n), lambda i: (i, 0)),
        out_shape=jax.ShapeDtypeStruct((t, n), F32),
        scratch_shapes=[pltpu.VMEM((SUBLANES, n), F32)],
        compiler_params=_cparams(("arbitrary",)),
    )(f, b_f)


def _fgate_bwd(dcum, f, b_f, *, name):
    t, n = f.shape
    tt = _tile(t, 256)
    nt = t // tt

    def body(dc_ref, f_ref, b_ref, df_ref, db_ref, carry_ref):
        i = pl.program_id(0)

        @pl.when(i == 0)
        def _():
            carry_ref[...] = jnp.zeros_like(carry_ref)
            db_ref[...] = jnp.zeros_like(db_ref)

        row = lax.broadcasted_iota(jnp.int32, (tt, tt), 0)
        col = lax.broadcasted_iota(jnp.int32, (tt, tt), 1)
        triu = (col >= row).astype(F32)
        dlf = jnp.dot(triu, dc_ref[...], precision=HIGHEST, preferred_element_type=F32) + carry_ref[0:1, :]
        carry_ref[0:1, :] = dlf[0:1, :]
        z = f_ref[...] + b_ref[...]
        df = dlf * _sigmoid(-z)
        df_ref[...] = df
        db_ref[...] += jnp.sum(df, axis=0, keepdims=True)

    blk = pl.BlockSpec((tt, n), lambda i: (nt - 1 - i, 0))
    vec = pl.BlockSpec((1, n), lambda i: (0, 0))
    return pl.pallas_call(
        body, name=name, grid=(nt,),
        in_specs=[blk, blk, vec], out_specs=[blk, vec],
        out_shape=[jax.ShapeDtypeStruct((t, n), F32), jax.ShapeDtypeStruct((1, n), F32)],
        scratch_shapes=[pltpu.VMEM((SUBLANES, n), F32)],
        compiler_params=_cparams(("arbitrary",)),
    )(dcum, f, b_f)


def _attn_fwd(qkv, ckt, gate, *, name):
    t = qkv.shape[0]
    f = gate.shape[1]
    npair = f // LANES
    tq = _tile(t, 512)
    nq = t // tq
    scale = 1.0 / math.sqrt(FOX_HEAD_DIM)
    nt_dims = (((1,), (1,)), ((), ()))

    def body(q_ref, k_ref, v_ref, ck_ref, g_ref, o_ref, y_ref, l_ref):
        i = pl.program_id(1)
        lane = lax.broadcasted_iota(jnp.int32, (tq, LANES), 1)
        lo = lane < FOX_HEAD_DIM
        q2 = q_ref[...] * scale
        qs = (jnp.where(lo, q2, 0).astype(BF16), jnp.where(lo, 0, q2).astype(BF16))
        row = lax.broadcasted_iota(jnp.int32, (tq, tq), 0)
        col = lax.broadcasted_iota(jnp.int32, (tq, tq), 1)
        causal = col <= row

        def kv_step(j, carry, masked):
            off = pl.multiple_of(j * tq, tq)
            kj = k_ref[pl.ds(off, tq), :]
            vj = v_ref[pl.ds(off, tq), :]
            ck = ck_ref[:, pl.ds(off, tq)]
            new = []
            for h in range(2):
                m, l, acc = carry[h]
                s = lax.dot_general(qs[h], kj, nt_dims, preferred_element_type=F32) - ck[h:h + 1, :]
                if masked:
                    s = jnp.where(causal, s, NEG_INF)
                m_new = jnp.maximum(m, jnp.max(s, axis=-1, keepdims=True))
                alpha = jnp.exp(m - m_new)
                p = jnp.exp(s - m_new)
                l = alpha * l + jnp.sum(p, axis=-1, keepdims=True)
                acc = alpha * acc + jnp.dot(p.astype(BF16), vj, preferred_element_type=F32)
                new.append((m_new, l, acc))
            return tuple(new)

        init = tuple((jnp.full((tq, 1), NEG_INF, F32), jnp.zeros((tq, 1), F32), jnp.zeros((tq, LANES), F32))
                     for _ in range(2))
        carry = lax.fori_loop(0, i, lambda j, c: kv_step(j, c, False), init)
        (m0, l0, a0), (m1, l1, a1) = kv_step(i, carry, True)
        o = jnp.where(lo, a0 / l0, a1 / l1)
        o_ref[...] = o
        gate_v = g_ref[...]
        y_ref[...] = (o * (gate_v * _sigmoid(gate_v))).astype(y_ref.dtype)
        l_ref[...] = jnp.where(lo, m0 + jnp.log(l0), m1 + jnp.log(l1))

    blk = lambda base: pl.BlockSpec((tq, LANES), lambda p, i: (i, base + p))
    full = lambda base: pl.BlockSpec((t, LANES), lambda p, i: (0, base + p))
    return pl.pallas_call(
        body, name=name, grid=(npair, nq),
        in_specs=[blk(0), full(npair), full(2 * npair), pl.BlockSpec((None, 2, t), lambda p, i: (p, 0, 0)), blk(0)],
        out_specs=[blk(0), blk(0), blk(0)],
        out_shape=[jax.ShapeDtypeStruct((t, f), F32), jax.ShapeDtypeStruct((t, f), BF16),
                   jax.ShapeDtypeStruct((t, f), F32)],
        compiler_params=_cparams(("parallel", "arbitrary")),
    )(qkv, qkv, qkv, ckt, gate)


def _attn_bwd(qkv, do, lt, dt, cke, *, name):
    t, f = do.shape
    npair = f // LANES
    tk = _tile(t, 512)
    nk = t // tk
    scale = 1.0 / math.sqrt(FOX_HEAD_DIM)
    nt_dims = (((1,), (1,)), ((), ()))
    tn_dims = (((0,), (0,)), ((), ()))

    def body(k_ref, v_ref, q_ref, do_ref, l_ref, d_ref, ck_ref, dq_ref, dk_ref, dv_ref, dck_ref, dcq_ref):
        j = pl.program_id(1)

        @pl.when(j == 0)
        def _():
            dq_ref[...] = jnp.zeros_like(dq_ref)
            dcq_ref[...] = jnp.zeros_like(dcq_ref)

        lane = lax.broadcasted_iota(jnp.int32, (tk, LANES), 1)
        lo = lane < FOX_HEAD_DIM
        sel = (lo, jnp.logical_not(lo))
        kj = k_ref[...]
        vj = v_ref[...]
        km = tuple(jnp.where(sel[h], kj, 0).astype(BF16) for h in range(2))
        ckv = ck_ref[...]
        ckh = (ckv[:, 0:1], ckv[:, FOX_HEAD_DIM:FOX_HEAD_DIM + 1])
        row = lax.broadcasted_iota(jnp.int32, (tk, tk), 0)
        col = lax.broadcasted_iota(jnp.int32, (tk, tk), 1)
        causal = row <= col

        def q_step(i, carry, masked):
            dk_acc, dv_acc, dck = carry
            off = pl.multiple_of(i * tk, tk)
            qi = q_ref[pl.ds(off, tk), :]
            doi = do_ref[pl.ds(off, tk), :]
            lrow = l_ref[:, pl.ds(off, tk)]
            drow = d_ref[:, pl.ds(off, tk)]
            dq_add = jnp.zeros((tk, LANES), F32)
            new_dck = []
            for h in range(2):
                qm = jnp.where(sel[h], qi, 0).astype(BF16)
                dom = jnp.where(sel[h], doi, 0).astype(BF16)
                st = lax.dot_general(kj, qm, nt_dims, preferred_element_type=F32) * scale
                st = st - ckh[h] - lrow[h:h + 1, :]
                if masked:
                    st = jnp.where(causal, st, NEG_INF)
                pt = jnp.exp(st)
                dpt = lax.dot_general(vj, dom, nt_dims, preferred_element_type=F32)
                dst = pt * (dpt - drow[h:h + 1, :])
                ptb = pt.astype(BF16)
                dstb = dst.astype(BF16)
                dv_acc = dv_acc + jnp.dot(ptb, dom, preferred_element_type=F32)
                dk_acc = dk_acc + jnp.dot(dstb, qm, preferred_element_type=F32)
                dq_add = dq_add + lax.dot_general(dstb, km[h], tn_dims, preferred_element_type=F32)
                new_dck.append(dck[h] - jnp.sum(dst, axis=-1, keepdims=True))
                dcq_ref[h:h + 1, pl.ds(off, tk)] += jnp.sum(dst, axis=0, keepdims=True)
            dq_ref[pl.ds(off, tk), :] += dq_add * scale
            return dk_acc, dv_acc, tuple(new_dck)

        zero = jnp.zeros((tk, LANES), F32)
        carry = (zero, zero, (jnp.zeros((tk, 1), F32), jnp.zeros((tk, 1), F32)))
        carry = q_step(j, carry, True)
        dk_acc, dv_acc, dck = lax.fori_loop(j + 1, nk, lambda i, c: q_step(i, c, False), carry)
        dk_ref[...] = (dk_acc * scale).astype(dk_ref.dtype)
        dv_ref[...] = dv_acc.astype(dv_ref.dtype)
        dck_ref[...] = jnp.where(lo, dck[0], dck[1])

    blk = lambda base: pl.BlockSpec((tk, LANES), lambda p, j: (j, base + p))
    full = lambda base: pl.BlockSpec((t, LANES), lambda p, j: (0, base + p))
    rows = pl.BlockSpec((None, 2, t), lambda p, j: (p, 0, 0))
    return pl.pallas_call(
        body, name=name, grid=(npair, nk),
        in_specs=[blk(npair), blk(2 * npair), full(0), full(0), rows, rows, blk(0)],
        out_specs=[full(0), blk(0), blk(0), blk(0), rows],
        out_shape=[jax.ShapeDtypeStruct((t, f), F32), jax.ShapeDtypeStruct((t, f), BF16),
                   jax.ShapeDtypeStruct((t, f), BF16), jax.ShapeDtypeStruct((t, f), F32),
                   jax.ShapeDtypeStruct((npair, 2, t), F32)],
        compiler_params=_cparams(("parallel", "arbitrary")),
    )(qkv, qkv, qkv, do, lt, dt, cke)


def _fox_post_bwd(dy, o, gate, *, name):
    t, f = dy.shape
    tt = _tile(t, 512)

    def body(dy_ref, o_ref, g_ref, do_ref, dg_ref, dl_ref):
        g = g_ref[...]
        sg = _sigmoid(g)
        dyv = dy_ref[...]
        ov = o_ref[...]
        do = dyv * (g * sg)
        do_ref[...] = do.astype(do_ref.dtype)
        dg_ref[...] = (dyv * ov * (sg * (1.0 + g * (1.0 - sg)))).astype(dg_ref.dtype)
        chan = lax.broadcasted_iota(jnp.int32, (f, LANES), 0)
        head = lax.broadcasted_iota(jnp.int32, (f, LANES), 1)
        pick = (chan // FOX_HEAD_DIM == head).astype(F32)
        dl_ref[...] = jnp.dot(do * ov, pick, precision=HIGHEST, preferred_element_type=F32)

    blk = pl.BlockSpec((tt, f), lambda i: (i, 0))
    return pl.pallas_call(
        body, name=name, grid=(t // tt,),
        in_specs=[blk, blk, blk], out_specs=[blk, blk, pl.BlockSpec((tt, LANES), lambda i: (i, 0))],
        out_shape=[jax.ShapeDtypeStruct((t, f), BF16), jax.ShapeDtypeStruct((t, f), BF16),
                   jax.ShapeDtypeStruct((t, LANES), F32)],
        compiler_params=_cparams(("parallel",)),
    )(dy, o, gate)


def _adamw(w, g, m, v, *, name):
    r, c = w.shape
    tr = _tile(r, 256) if r % SUBLANES == 0 else r
    c1 = 1.0 - ADAM_B1 ** ADAM_STEP
    c2 = 1.0 - ADAM_B2 ** ADAM_STEP

    def body(w_ref, g_ref, m_ref, v_ref, d_ref, mo_ref, vo_ref):
        gv = g_ref[...]
        mn = ADAM_B1 * m_ref[...] + (1.0 - ADAM_B1) * gv
        vn = ADAM_B2 * v_ref[...] + (1.0 - ADAM_B2) * (gv * gv)
        mo_ref[...] = mn
        vo_ref[...] = vn
        d_ref[...] = -ADAM_LR * ((mn / c1) / (jnp.sqrt(vn / c2) + ADAM_EPS) + ADAM_WD * w_ref[...])

    blk = pl.BlockSpec((tr, c), lambda i: (i, 0))
    return pl.pallas_call(
        body, name=name, grid=(r // tr,), in_specs=[blk] * 4, out_specs=[blk] * 3,
        out_shape=[jax.ShapeDtypeStruct((r, c), F32)] * 3,
        compiler_params=_cparams(("parallel",)),
    )(w, g, m, v)


def _sum_slots(land, *, name):
    ns, r, c = land.shape
    tr = _tile(r, 64) if r % SUBLANES == 0 else r

    def body(l_ref, o_ref):
        acc = l_ref[0].astype(F32)
        for s in range(1, ns):
            acc = acc + l_ref[s].astype(F32)
        o_ref[...] = acc

    return pl.pallas_call(
        body, name=name, grid=(r // tr,),
        in_specs=[pl.BlockSpec((ns, tr, c), lambda i: (0, i, 0))],
        out_specs=pl.BlockSpec((tr, c), lambda i: (i, 0)),
        out_shape=jax.ShapeDtypeStruct((r, c), F32),
        compiler_params=_cparams(("parallel",)),
    )(land)


ANY = pl.BlockSpec(memory_space=pl.ANY)


def _flip(v, bit):
    return 1 - v if bit else v


def _gather_chips(shards, *, name):
    n = len(shards)
    rels = ((1, 0), (0, 1), (1, 1))

    def body(*refs):
        ins, outs = refs[:n], refs[n:2 * n]
        send, recv, loc = refs[2 * n:]
        x, y, c = lax.axis_index("x"), lax.axis_index("y"), lax.axis_index("c")
        me = 2 * x + y
        copies = []
        for k in range(n):
            cp = pltpu.make_async_copy(ins[k], outs[k].at[me], loc.at[k])
            cp.start()
            copies.append(cp)
        for r, (rx, ry) in enumerate(rels):
            for k in range(n):
                cp = pltpu.make_async_remote_copy(
                    src_ref=ins[k], dst_ref=outs[k].at[me], send_sem=send.at[r * n + k], recv_sem=recv.at[r * n + k],
                    device_id=(_flip(x, rx), _flip(y, ry), c), device_id_type=MESH)
                cp.start()
                copies.append(cp)
        for cp in copies:
            cp.wait()

    return pl.pallas_call(
        body, name=name, in_specs=[ANY] * n, out_specs=[ANY] * n,
        out_shape=[jax.ShapeDtypeStruct((N_CHIPS,) + s.shape, s.dtype) for s in shards],
        scratch_shapes=[pltpu.SemaphoreType.DMA((3 * n,)), pltpu.SemaphoreType.DMA((3 * n,)),
                        pltpu.SemaphoreType.DMA((n,))],
        compiler_params=pltpu.CompilerParams(has_side_effects=True),
    )(*shards)


_RELS7 = tuple((r >> 2 & 1, r >> 1 & 1, r & 1) for r in range(1, N_DEV))


def _scatter_pieces(grads, *, name):
    n = len(grads)

    def body(*refs):
        ins, outs = refs[:n], refs[n:2 * n]
        send, recv, loc = refs[2 * n:]
        x, y, c = lax.axis_index("x"), lax.axis_index("y"), lax.axis_index("c")
        me = 4 * x + 2 * y + c
        copies = []

        def piece(k, px, py, pc):
            half = ins[k].shape[1] // 2
            return ins[k].at[2 * px + py, pl.ds(pc * half, half), :]

        for k in range(n):
            cp = pltpu.make_async_copy(piece(k, x, y, c), outs[k].at[me], loc.at[k])
            cp.start()
            copies.append(cp)
        for r, (rx, ry, rc) in enumerate(_RELS7):
            tx, ty, tc = _flip(x, rx), _flip(y, ry), _flip(c, rc)
            for k in range(n):
                cp = pltpu.make_async_remote_copy(
                    src_ref=piece(k, tx, ty, tc), dst_ref=outs[k].at[me], send_sem=send.at[r * n + k],
                    recv_sem=recv.at[r * n + k], device_id=(tx, ty, tc), device_id_type=MESH)
                cp.start()
                copies.append(cp)
        for cp in copies:
            cp.wait()

    return pl.pallas_call(
        body, name=name, in_specs=[ANY] * n, out_specs=[ANY] * n,
        out_shape=[jax.ShapeDtypeStruct((N_DEV, g.shape[1] // 2, g.shape[2]), g.dtype) for g in grads],
        scratch_shapes=[pltpu.SemaphoreType.DMA((7 * n,)), pltpu.SemaphoreType.DMA((7 * n,)),
                        pltpu.SemaphoreType.DMA((n,))],
        compiler_params=pltpu.CompilerParams(has_side_effects=True),
    )(*grads)


def _join_cores(halves, *, name):
    n = len(halves)

    def body(*refs):
        ins, outs = refs[:n], refs[n:2 * n]
        send, recv, loc = refs[2 * n:]
        x, y, c = lax.axis_index("x"), lax.axis_index("y"), lax.axis_index("c")
        copies = []
        for k in range(n):
            cp = pltpu.make_async_copy(ins[k], outs[k].at[c], loc.at[k])
            cp.start()
            copies.append(cp)
            cp = pltpu.make_async_remote_copy(
                src_ref=ins[k], dst_ref=outs[k].at[c], send_sem=send.at[k], recv_sem=recv.at[k],
                device_id=(x, y, 1 - c), device_id_type=MESH)
            cp.start()
            copies.append(cp)
        for cp in copies:
            cp.wait()

    return pl.pallas_call(
        body, name=name, in_specs=[ANY] * n, out_specs=[ANY] * n,
        out_shape=[jax.ShapeDtypeStruct((2,) + h.shape, h.dtype) for h in halves],
        scratch_shapes=[pltpu.SemaphoreType.DMA((n,)), pltpu.SemaphoreType.DMA((n,)), pltpu.SemaphoreType.DMA((n,))],
        compiler_params=pltpu.CompilerParams(has_side_effects=True),
    )(*halves)


def _gather_all(buf, *, name):
    def body(in_ref, out_ref, send, recv, loc):
        x, y, c = lax.axis_index("x"), lax.axis_index("y"), lax.axis_index("c")
        me = 4 * x + 2 * y + c
        copies = [pltpu.make_async_copy(in_ref, out_ref.at[me], loc.at[0])]
        copies[0].start()
        for r, (rx, ry, rc) in enumerate(_RELS7):
            cp = pltpu.make_async_remote_copy(
                src_ref=in_ref, dst_ref=out_ref.at[me], send_sem=send.at[r], recv_sem=recv.at[r],
                device_id=(_flip(x, rx), _flip(y, ry), _flip(c, rc)), device_id_type=MESH)
            cp.start()
            copies.append(cp)
        for cp in copies:
            cp.wait()

    return pl.pallas_call(
        body, name=name, in_specs=[ANY], out_specs=ANY,
        out_shape=jax.ShapeDtypeStruct((N_DEV,) + buf.shape, buf.dtype),
        scratch_shapes=[pltpu.SemaphoreType.DMA((7,)), pltpu.SemaphoreType.DMA((7,)), pltpu.SemaphoreType.DMA((1,))],
        compiler_params=pltpu.CompilerParams(has_side_effects=True),
    )(buf)


def _pack(arrs):
    flat = []
    for a in arrs:
        v = a.reshape(-1)
        pad = (-v.shape[0]) % LANES
        if pad:
            v = jnp.pad(v, (0, pad))
        flat.append(v)
    v = jnp.concatenate(flat)
    pad = (-v.shape[0]) % (LANES * SUBLANES)
    if pad:
        v = jnp.pad(v, (0, pad))
    return v.reshape(-1, LANES)


def _unpack(buf, shapes):
    v = buf.reshape(-1)
    out, off = [], 0
    for s in shapes:
        n = math.prod(s)
        out.append(v[off:off + n].reshape(s))
        off += n + (-n) % LANES
    return out


def kernel(x, norm_g, final_g, lru_w_in, lru_conv_w, lru_conv_b, lru_wa, lru_ba, lru_wx, lru_bx, lru_a_param, lru_w_out, fox_w_in, fox_b_f, fox_w_out, loss_target, m_norm_g, m_final_g, m_lru_w_in, m_lru_conv_w, m_lru_conv_b, m_lru_wa, m_lru_ba, m_lru_wx, m_lru_bx, m_lru_a_param, m_lru_w_out, m_fox_w_in, m_fox_b_f, m_fox_w_out, v_norm_g, v_final_g, v_lru_w_in, v_lru_conv_w, v_lru_conv_b, v_lru_wa, v_lru_ba, v_lru_wx, v_lru_bx, v_lru_a_param, v_lru_w_out, v_fox_w_in, v_fox_b_f, v_fox_w_out):
    t, d = x.shape[1], x.shape[2]
    w = lru_wa.shape[1] * LRU_BLOCK_W
    f = FOX_HEADS * FOX_HEAD_DIM
    npair = f // LANES
    x0 = x.reshape(t, d)
    tgt = loss_target.reshape(t, d)
    chip = 2 * lax.axis_index("x") + lax.axis_index("y")

    g_lwi, g_lwo, g_fwi, g_fwo, g_cw = _gather_chips(
        [lru_w_in[0].astype(BF16), lru_w_out[0].astype(BF16), fox_w_in[0].astype(BF16), fox_w_out[0].astype(BF16),
         lru_conv_w[0]], name="gather_weights")
    cg = w // 2
    lwi = jnp.concatenate([g_lwi[0], g_lwi[2], g_lwi[1], g_lwi[3]], axis=1)
    lwo = g_lwo.reshape(w, d)
    fwi = jnp.concatenate([g_fwi[s] for s in range(N_CHIPS)], axis=1)
    w_qkv, w_g2 = fwi[:, :3 * f], fwi[:, 3 * f:4 * f]
    w_f = jnp.pad(fwi[:, 4 * f:], ((0, 0), (0, LANES - FOX_HEADS)))
    fwo = g_fwo.reshape(f, d)
    conv_w = jnp.concatenate([g_cw[s] for s in range(N_CHIPS)], axis=1)
    conv_b, ba, bx, a_param = lru_conv_b, lru_ba, lru_bx, lru_a_param
    wa, wx = lru_wa[0], lru_wx[0]
    b_f = jnp.pad(fox_b_f, ((0, 0), (0, LANES - FOX_HEADS)))

    h0 = _rmsnorm(x0, norm_g[0], name="norm0")
    u = _matmul(h0, lwi, name="lru_in")
    y1, hs = _lru_fwd(u, conv_w, conv_b, wa, ba, wx, bx, a_param, cg=cg, name="lru_fwd")
    x1 = _matmul(y1, lwo, add=x0, name="lru_out")
    h1 = _rmsnorm(x1, norm_g[1], name="norm1")
    qkv = _matmul(h1, w_qkv, out_dtype=BF16, name="fox_qkv")
    gate2 = _matmul(h1, w_g2, name="fox_gate")
    flog = _matmul(h1, w_f, name="fox_f")
    cum = _fgate_fwd(flog, b_f, name="fgate_fwd")
    cum16 = cum[:, :FOX_HEADS]
    ckt = cum16.T.reshape(npair, 2, t)
    cke = jnp.repeat(cum16, FOX_HEAD_DIM, axis=1)
    o, y2, lse = _attn_fwd(qkv, ckt, gate2, name="attn_fwd")
    x2 = _matmul(y2, fwo, add=x1, name="fox_out")
    lsum, dx2, dgf = _final_loss(x2, tgt, final_g, name="final_loss")
    loss = lax.psum(0.5 * jnp.sum(lsum) / d, ("x", "y", "c"))

    d_fwo = _matmul(y2, dx2, ta=True, name="d_fox_w_out")
    dy2 = _matmul(dx2, fwo, tb=True, name="d_y2")
    do, dgate2, dl = _fox_post_bwd(dy2, o, gate2, name="fox_post_bwd")
    lt = lse[:, ::FOX_HEAD_DIM].T.reshape(npair, 2, t)
    dt = dl[:, :FOX_HEADS].T.reshape(npair, 2, t)
    dq, dk, dv, dck, dcq = _attn_bwd(qkv, do, lt, dt, cke, name="attn_bwd")
    dcum = jnp.pad(dck[:, ::FOX_HEAD_DIM] + dcq.reshape(FOX_HEADS, t).T, ((0, 0), (0, LANES - FOX_HEADS)))
    dflog, db_f = _fgate_bwd(dcum, flog, b_f, name="fgate_bwd")
    du2 = jnp.concatenate([dq.astype(BF16), dk, dv, dgate2], axis=1)
    dflog_b = dflog.astype(BF16)
    dh1 = _matmul(du2, fwi[:, :4 * f], tb=True, name="d_h1_a")
    dh1 = _matmul(dflog_b, w_f, tb=True, add=dh1, name="d_h1_b")
    d_fwi_a = _matmul(h1, du2, ta=True, name="d_fox_w_in_a")
    d_fwi_b = _matmul(h1, dflog_b, ta=True, name="d_fox_w_in_b")
    d_fwi = jnp.concatenate([d_fwi_a, d_fwi_b[:, :FOX_HEADS]], axis=1)
    dx1, dg1 = _rmsnorm_bwd(dh1, x1, norm_g[1], dx2, name="norm1_bwd")

    d_lwo = _matmul(y1, dx1, ta=True, name="d_lru_w_out")
    dy1 = _matmul(dx1, lwo, tb=True, name="d_y1")
    dxb, dgate, d_cw, d_cb, d_wa, d_ba, d_wx, d_bx, d_ap = _lru_bwd(
        u, hs, dy1, conv_w, conv_b, wa, ba, wx, bx, a_param, cg=cg, name="lru_bwd")
    du = jnp.concatenate([dxb[:, :cg], dgate[:, :cg], dxb[:, cg:], dgate[:, cg:]], axis=1)
    dh0 = _matmul(du, lwi, tb=True, name="d_h0")
    d_lwi_p = _matmul(h0, du, ta=True, name="d_lru_w_in")
    dx0, dg0 = _rmsnorm_bwd(dh0, x0, norm_g[0], dx1, name="norm0_bwd")

    csz = cg
    g_lwi4 = jnp.stack([d_lwi_p[:, 0:csz], d_lwi_p[:, 2 * csz:3 * csz], d_lwi_p[:, csz:2 * csz],
                        d_lwi_p[:, 3 * csz:]])
    n_fwi = fox_w_in.shape[2]
    g_fwi4 = jnp.stack([d_fwi[:, s * n_fwi:(s + 1) * n_fwi] for s in range(N_CHIPS)])
    g_lwo4 = d_lwo.reshape(N_CHIPS, w // N_CHIPS, d)
    g_fwo4 = d_fwo.reshape(N_CHIPS, f // N_CHIPS, d)
    lands = _scatter_pieces([g_lwi4, g_lwo4, g_fwi4, g_fwo4], name="scatter_grads")
    halves = [_sum_slots(l, name="sum_" + nm) for l, nm in zip(lands, ("lru_w_in", "lru_w_out", "fox_w_in", "fox_w_out"))]
    joined = _join_cores(halves, name="join_cores")
    big_g = [j.reshape(2 * j.shape[1], j.shape[2]) for j in joined]

    small_g = [jnp.concatenate([dg0, dg1], axis=0), dgf.reshape(d), d_cw, d_cb, d_wa, d_ba, d_wx, d_bx, d_ap,
               db_f[:, :FOX_HEADS]]
    gsum = _sum_slots(_gather_all(_pack(small_g), name="gather_small"), name="sum_small")
    zc = jnp.zeros((CONV_WIDTH, w), F32)
    pk_w = _pack([norm_g, final_g, zc, lru_conv_b, lru_wa, lru_ba, lru_wx, lru_bx, lru_a_param, fox_b_f])
    pk_m = _pack([m_norm_g, m_final_g, zc, m_lru_conv_b, m_lru_wa, m_lru_ba, m_lru_wx, m_lru_bx, m_lru_a_param,
                  m_fox_b_f])
    pk_v = _pack([v_norm_g, v_final_g, zc + 1.0, v_lru_conv_b, v_lru_wa, v_lru_ba, v_lru_wx, v_lru_bx,
                  v_lru_a_param, v_fox_b_f])
    s_delta, s_m, s_v = _adamw(pk_w, gsum, pk_m, pk_v, name="adamw_small")
    out_shapes = [norm_g.shape, final_g.shape, (CONV_WIDTH, w), lru_conv_b.shape, lru_wa.shape, lru_ba.shape,
                  lru_wx.shape, lru_bx.shape, lru_a_param.shape, fox_b_f.shape]
    sg = _unpack(gsum, out_shapes)
    sd = _unpack(s_delta, out_shapes)
    sm = _unpack(s_m, out_shapes)
    sv = _unpack(s_v, out_shapes)

    ncw = lru_conv_w.shape[2]
    g_cw_loc = lax.dynamic_slice_in_dim(sg[2], chip * ncw, ncw, axis=1)
    cw_d, cw_m, cw_v = _adamw(lru_conv_w[0], g_cw_loc, m_lru_conv_w[0], v_lru_conv_w[0], name="adamw_conv_w")

    big = []
    for nm, wt, g, mm, vv in (("lru_w_in", lru_w_in, big_g[0], m_lru_w_in, v_lru_w_in),
                              ("lru_w_out", lru_w_out, big_g[1], m_lru_w_out, v_lru_w_out),
                              ("fox_w_in", fox_w_in, big_g[2], m_fox_w_in, v_fox_w_in),
                              ("fox_w_out", fox_w_out, big_g[3], m_fox_w_out, v_fox_w_out)):
        dl_, mn_, vn_ = _adamw(wt[0], g, mm[0], vv[0], name="adamw_" + nm)
        big.append((g[None], dl_[None], mn_[None], vn_[None]))

    def assemble(idx):
        small = (sg, sd, sm, sv)[idx]
        cw = (g_cw_loc, cw_d, cw_m, cw_v)[idx][None]
        return [small[0], small[1], big[0][idx], cw, small[3], small[4], small[5], small[6], small[7], small[8],
                big[1][idx], big[2][idx], small[9], big[3][idx]]

    grad_x = dx0.reshape(1, t, d)
    return (loss, grad_x, *assemble(0), *assemble(1), *assemble(2), *assemble(3))
```

```python
import functools
import math

import jax
import jax.numpy as jnp
from jax import lax
from jax.experimental import pallas as pl
from jax.experimental.pallas import tpu as pltpu

F32 = jnp.float32
BF16 = jnp.bfloat16

EPS = 1e-6
LRU_C = 8.0
LRU_BLOCK_W = 128
CONV_WIDTH = 4
FOX_HEADS = 16
FOX_HEAD_DIM = 64
NEG_INF = -1e30
ADAM_LR = 0.001
ADAM_B1 = 0.9
ADAM_B2 = 0.999
ADAM_EPS = 1e-08
ADAM_WD = 0.01
ADAM_STEP = 10

LANES = 128
SUBLANES = 8
VMEM_LIMIT = 56 * 1024 * 1024
N_CHIPS = 4
N_DEV = 8
MESH = pl.DeviceIdType.MESH
HIGHEST = lax.Precision.HIGHEST


def _tile(n, pref):
    t = min(n, pref)
    while n % t:
        t //= 2
    return t


def _cparams(dims=None):
    return pltpu.CompilerParams(dimension_semantics=dims, vmem_limit_bytes=VMEM_LIMIT)


def _sigmoid(x):
    return jax.nn.sigmoid(x)


def _log1p(x):
    u = 1.0 + x
    return jnp.where(u == 1.0, x, jnp.log(u) * x / (u - 1.0))


def _softplus(x):
    return jnp.maximum(x, 0.0) + _log1p(jnp.exp(-jnp.abs(x)))


MM_TILE = 1024
MM_FULL_K = 1536


def _matmul(a, b, *, name, ta=False, tb=False, out_dtype=F32, add=None, tm=MM_TILE, tn=MM_TILE, tk=None):
    if ta:
        kdim, m = a.shape
    else:
        m, kdim = a.shape
    if tb:
        n, kb = b.shape
    else:
        kb, n = b.shape
    assert kdim == kb, (a.shape, b.shape, ta, tb)
    if tk is None:
        tk = kdim if kdim <= MM_FULL_K else MM_TILE
    tm, tn, tk = _tile(m, tm), _tile(n, tn), _tile(kdim, tk)
    nk = kdim // tk
    dn = (((0 if ta else 1,), (1 if tb else 0,)), ((), ()))
    has_add = add is not None

    def body(*refs):
        if has_add:
            a_ref, b_ref, add_ref, o_ref = refs[:4]
        else:
            a_ref, b_ref, o_ref = refs[:3]
        part = lax.dot_general(a_ref[...].astype(BF16), b_ref[...].astype(BF16), dn, preferred_element_type=F32)

        def finish(r):
            if has_add:
                r = r + add_ref[...].astype(F32)
            o_ref[...] = r.astype(o_ref.dtype)

        if nk == 1:
            finish(part)
            return
        acc_ref = refs[-1]
        k = pl.program_id(2)

        @pl.when(k == 0)
        def _():
            acc_ref[...] = part

        @pl.when(k > 0)
        def _():
            acc_ref[...] += part

        @pl.when(k == nk - 1)
        def _():
            finish(acc_ref[...])

    a_spec = pl.BlockSpec((tk, tm), lambda i, j, k: (k, i)) if ta else pl.BlockSpec((tm, tk), lambda i, j, k: (i, k))
    b_spec = pl.BlockSpec((tn, tk), lambda i, j, k: (j, k)) if tb else pl.BlockSpec((tk, tn), lambda i, j, k: (k, j))
    o_spec = pl.BlockSpec((tm, tn), lambda i, j, k: (i, j))
    in_specs = [a_spec, b_spec] + ([o_spec] if has_add else [])
    args = (a, b) + ((add,) if has_add else ())
    return pl.pallas_call(
        body, name=name, grid=(m // tm, n // tn, nk), in_specs=in_specs, out_specs=o_spec,
        out_shape=jax.ShapeDtypeStruct((m, n), out_dtype),
        scratch_shapes=[pltpu.VMEM((tm, tn), F32)] if nk > 1 else [],
        compiler_params=_cparams(("parallel", "parallel", "arbitrary")),
    )(*args)


def _rmsnorm(x, g, *, name):
    t, d = x.shape
    tt = _tile(t, 512)

    def body(x_ref, g_ref, o_ref):
        xf = x_ref[...]
        rstd = lax.rsqrt(jnp.mean(xf * xf, axis=-1, keepdims=True) + EPS)
        o_ref[...] = (xf * rstd * g_ref[...]).astype(o_ref.dtype)

    return pl.pallas_call(
        body, name=name, grid=(t // tt,),
        in_specs=[pl.BlockSpec((tt, d), lambda i: (i, 0)), pl.BlockSpec((1, d), lambda i: (0, 0))],
        out_specs=pl.BlockSpec((tt, d), lambda i: (i, 0)),
        out_shape=jax.ShapeDtypeStruct((t, d), BF16),
        compiler_params=_cparams(("parallel",)),
    )(x, g.reshape(1, d))


def _rmsnorm_bwd(dh, x, g, dres, *, name):
    t, d = x.shape
    tt = _tile(t, 512)

    def body(dh_ref, x_ref, g_ref, dres_ref, dx_ref, dg_ref):
        i = pl.program_id(0)

        @pl.when(i == 0)
        def _():
            dg_ref[...] = jnp.zeros_like(dg_ref)

        xf = x_ref[...]
        rstd = lax.rsqrt(jnp.mean(xf * xf, axis=-1, keepdims=True) + EPS)
        xhat = xf * rstd
        dhf = dh_ref[...].astype(F32)
        dxhat = dhf * g_ref[...]
        mt = jnp.mean(dxhat * xhat, axis=-1, keepdims=True)
        dx_ref[...] = dres_ref[...] + rstd * (dxhat - xhat * mt)
        dg_ref[...] += jnp.sum(dhf * xhat, axis=0, keepdims=True)

    blk = pl.BlockSpec((tt, d), lambda i: (i, 0))
    vec = pl.BlockSpec((1, d), lambda i: (0, 0))
    return pl.pallas_call(
        body, name=name, grid=(t // tt,),
        in_specs=[blk, blk, vec, blk], out_specs=[blk, vec],
        out_shape=[jax.ShapeDtypeStruct((t, d), F32), jax.ShapeDtypeStruct((1, d), F32)],
        compiler_params=_cparams(("arbitrary",)),
    )(dh, x, g.reshape(1, d), dres)


def _final_loss(x2, tgt, g, *, name):
    t, d = x2.shape
    tt = _tile(t, 512)

    def body(x_ref, t_ref, g_ref, l_ref, dx_ref, dg_ref):
        i = pl.program_id(0)

        @pl.when(i == 0)
        def _():
            dg_ref[...] = jnp.zeros_like(dg_ref)
            l_ref[...] = jnp.zeros_like(l_ref)

        xf = x_ref[...]
        gg = g_ref[...]
        rstd = lax.rsqrt(jnp.mean(xf * xf, axis=-1, keepdims=True) + EPS)
        xhat = xf * rstd
        err = xhat * gg - t_ref[...]
        l_ref[...] += jnp.sum(err * err, axis=0, keepdims=True)
        dy = err * (1.0 / d)
        dxhat = dy * gg
        mt = jnp.mean(dxhat * xhat, axis=-1, keepdims=True)
        dx_ref[...] = rstd * (dxhat - xhat * mt)
        dg_ref[...] += jnp.sum(dy * xhat, axis=0, keepdims=True)

    blk = pl.BlockSpec((tt, d), lambda i: (i, 0))
    vec = pl.BlockSpec((1, d), lambda i: (0, 0))
    return pl.pallas_call(
        body, name=name, grid=(t // tt,),
        in_specs=[blk, blk, vec], out_specs=[vec, blk, vec],
        out_shape=[jax.ShapeDtypeStruct((1, d), F32), jax.ShapeDtypeStruct((t, d), F32),
                   jax.ShapeDtypeStruct((1, d), F32)],
        compiler_params=_cparams(("arbitrary",)),
    )(x2, tgt, g.reshape(1, d))


def _shift_down(prev8, cur, s):
    ext = jnp.concatenate([prev8, cur], axis=0)
    if s == 0:
        return cur
    return pltpu.roll(ext, s, 0)[SUBLANES:, :]


def _shift_up(cur, next8, s):
    if s == 0:
        return cur
    n = cur.shape[0]
    ext = jnp.concatenate([cur, next8], axis=0)
    return pltpu.roll(ext, n + SUBLANES - s, 0)[:n, :]


def _lru_gates(xc, wa, ba, wx, bx, sp):
    xcb = xc.astype(BF16)
    r = _sigmoid(jnp.dot(xcb, wa, preferred_element_type=F32) + ba)
    ig = _sigmoid(jnp.dot(xcb, wx, preferred_element_type=F32) + bx)
    log_a = -LRU_C * r * sp
    a = jnp.exp(log_a)
    mult = jnp.sqrt(-jnp.tanh(log_a) * (a * a + 1.0))
    return r, ig, a, mult


def _lru_specs(tt, cg, n_groups, nt, reverse):
    ncol = cg // LANES
    if reverse:
        ti = lambda i: nt - 1 - i
    else:
        ti = lambda i: i
    hb = tt // SUBLANES
    cur = lambda col: pl.BlockSpec((tt, cg), lambda g, i: (ti(i), 2 * g + col))
    prev = lambda col: pl.BlockSpec((SUBLANES, cg), lambda g, i: (jnp.maximum(ti(i) * hb - 1, 0), 2 * g + col))
    chan = lambda rows: pl.BlockSpec((rows, cg), lambda g, i: (0, g))
    wblk = pl.BlockSpec((ncol, LRU_BLOCK_W, LRU_BLOCK_W), lambda g, i: (g, 0, 0))
    plain = pl.BlockSpec((tt, cg), lambda g, i: (ti(i), g))
    plain_prev = pl.BlockSpec((SUBLANES, cg), lambda g, i: (jnp.maximum(ti(i) * hb - 1, 0), g))
    return cur, prev, chan, wblk, plain, plain_prev


def _lru_fwd(u, conv_w, conv_b, wa, ba, wx, bx, a_param, *, cg, name):
    t, w2 = u.shape
    w = w2 // 2
    n_groups = w // cg
    ncol = cg // LANES
    tt = _tile(t, 256)
    nt = t // tt
    cur, prev, chan, wblk, plain, _ = _lru_specs(tt, cg, n_groups, nt, False)

    def body(xb_ref, xp_ref, gate_ref, cw_ref, cb_ref, wa_ref, ba_ref, wx_ref, bx_ref, ap_ref,
             y_ref, hs_ref, h_ref, a_s, b_s):
        i = pl.program_id(1)

        @pl.when(i == 0)
        def _():
            h_ref[...] = jnp.zeros_like(h_ref)

        keep = (i > 0).astype(F32)
        for n in range(ncol):
            sl = slice(n * LANES, (n + 1) * LANES)
            xb = xb_ref[:, sl]
            xp = xp_ref[:, sl] * keep
            xc = cb_ref[:, sl] + cw_ref[3:4, sl] * xb
            for s in range(1, CONV_WIDTH):
                xc = xc + cw_ref[3 - s:4 - s, sl] * _shift_down(xp, xb, s)
            sp = _softplus(-ap_ref[:, sl])
            _, ig, a, mult = _lru_gates(xc, wa_ref[n].astype(BF16), ba_ref[:, sl],
                                        wx_ref[n].astype(BF16), bx_ref[:, sl], sp)
            a_s[:, sl] = a
            b_s[:, sl] = mult * (ig * xc)

        def step(tr, h):
            h = a_s[pl.ds(tr, 1), :] * h + b_s[pl.ds(tr, 1), :]
            hs_ref[pl.ds(tr, 1), :] = h
            return h

        h = lax.fori_loop(0, tt, step, h_ref[0:1, :], unroll=8)
        h_ref[0:1, :] = h
        gate = gate_ref[...]
        y_ref[...] = (hs_ref[...] * (gate * _sigmoid(gate))).astype(y_ref.dtype)

    return pl.pallas_call(
        body, name=name, grid=(n_groups, nt),
        in_specs=[cur(0), prev(0), cur(1), chan(CONV_WIDTH), chan(1), wblk, chan(1), wblk, chan(1), chan(1)],
        out_specs=[plain, plain],
        out_shape=[jax.ShapeDtypeStruct((t, w), BF16), jax.ShapeDtypeStruct((t, w), F32)],
        scratch_shapes=[pltpu.VMEM((SUBLANES, cg), F32), pltpu.VMEM((tt, cg), F32), pltpu.VMEM((tt, cg), F32)],
        compiler_params=_cparams(("parallel", "arbitrary")),
    )(u, u, u, conv_w, conv_b, wa, ba, wx, bx, a_param)


def _lru_bwd(u, hs, dy, conv_w, conv_b, wa, ba, wx, bx, a_param, *, cg, name):
    t, w2 = u.shape
    w = w2 // 2
    n_groups = w // cg
    ncol = cg // LANES
    tt = _tile(t, 256)
    nt = t // tt
    cur, prev, chan, wblk, plain, plain_prev = _lru_specs(tt, cg, n_groups, nt, True)
    tn_dims = (((0,), (0,)), ((), ()))
    nt_dims = (((1,), (1,)), ((), ()))

    def body(xb_ref, xp_ref, gate_ref, hs_ref, hp_ref, dy_ref, cw_ref, cb_ref, wa_ref, ba_ref, wx_ref, bx_ref,
             ap_ref, dxb_ref, dgate_ref, dcw_ref, dcb_ref, dwa_ref, dba_ref, dwx_ref, dbx_ref, dsp_ref,
             c_ref, nx_ref, a_s, dhs_s, lam_s):
        i = pl.program_id(1)
        first_time_block = i == nt - 1

        @pl.when(i == 0)
        def _():
            c_ref[...] = jnp.zeros_like(c_ref)
            nx_ref[...] = jnp.zeros_like(nx_ref)
            for r in (dcw_ref, dcb_ref, dwa_ref, dba_ref, dwx_ref, dbx_ref, dsp_ref):
                r[...] = jnp.zeros_like(r)

        keep = jnp.where(first_time_block, 0.0, 1.0).astype(F32)
        gate = gate_ref[...]
        sg = _sigmoid(gate)
        dyv = dy_ref[...]
        hsv = hs_ref[...]
        dhs_s[...] = dyv * (gate * sg)
        dgate_ref[...] = (dyv * hsv * (sg * (1.0 + gate * (1.0 - sg)))).astype(dgate_ref.dtype)

        saved = []
        for n in range(ncol):
            sl = slice(n * LANES, (n + 1) * LANES)
            xb = xb_ref[:, sl]
            xp = xp_ref[:, sl] * keep
            shifted = [xb] + [_shift_down(xp, xb, s) for s in range(1, CONV_WIDTH)]
            xc = cb_ref[:, sl] + cw_ref[3:4, sl] * xb
            for s in range(1, CONV_WIDTH):
                xc = xc + cw_ref[3 - s:4 - s, sl] * shifted[s]
            sp = _softplus(-ap_ref[:, sl])
            wab = wa_ref[n].astype(BF16)
            wxb = wx_ref[n].astype(BF16)
            r, ig, a, mult = _lru_gates(xc, wab, ba_ref[:, sl], wxb, bx_ref[:, sl], sp)
            a_s[:, sl] = a
            saved.append((sl, shifted, xc, sp, wab, wxb, r, ig, a, mult))

        def step(k, c):
            tr = tt - 1 - k
            lam = dhs_s[pl.ds(tr, 1), :] + c
            lam_s[pl.ds(tr, 1), :] = lam
            return a_s[pl.ds(tr, 1), :] * lam

        c_ref[0:1, :] = lax.fori_loop(0, tt, step, c_ref[0:1, :], unroll=8)

        for n in range(ncol):
            sl, shifted, xc, sp, wab, wxb, r, ig, a, mult = saved[n]
            lam = lam_s[:, sl]
            hprev = _shift_down(hp_ref[:, sl] * keep, hs_ref[:, sl], 1)
            da = lam * hprev
            dmult = lam * (ig * xc)
            dlog_a = da * a - dmult * (a * a / mult)
            di = lam * (mult * xc)
            dxc = lam * (mult * ig)
            dr = dlog_a * (-LRU_C * sp)
            dsp_ref[:, sl] += jnp.sum(dlog_a * (-LRU_C * r), axis=0, keepdims=True)
            dza = dr * (r * (1.0 - r))
            dzx = di * (ig * (1.0 - ig))
            dba_ref[:, sl] += jnp.sum(dza, axis=0, keepdims=True)
            dbx_ref[:, sl] += jnp.sum(dzx, axis=0, keepdims=True)
            xcb = xc.astype(BF16)
            dzab = dza.astype(BF16)
            dzxb = dzx.astype(BF16)
            dwa_ref[n] += lax.dot_general(xcb, dzab, tn_dims, preferred_element_type=F32)
            dwx_ref[n] += lax.dot_general(xcb, dzxb, tn_dims, preferred_element_type=F32)
            dxc = dxc + lax.dot_general(dzab, wab, nt_dims, preferred_element_type=F32)
            dxc = dxc + lax.dot_general(dzxb, wxb, nt_dims, preferred_element_type=F32)
            dcb_ref[:, sl] += jnp.sum(dxc, axis=0, keepdims=True)
            for s in range(CONV_WIDTH):
                dcw_ref[3 - s:4 - s, sl] += jnp.sum(dxc * shifted[s], axis=0, keepdims=True)
            nx = nx_ref[:, sl]
            dxb = cw_ref[3:4, sl] * dxc
            for s in range(1, CONV_WIDTH):
                dxb = dxb + cw_ref[3 - s:4 - s, sl] * _shift_up(dxc, nx, s)
            dxb_ref[:, sl] = dxb.astype(dxb_ref.dtype)
            nx_ref[:, sl] = dxc[0:SUBLANES, :]

        @pl.when(first_time_block)
        def _():
            dsp_ref[...] = dsp_ref[...] * (-_sigmoid(-ap_ref[...]))

    dxb_spec = pl.BlockSpec((tt, cg), lambda g, i: (nt - 1 - i, g))
    outs = pl.pallas_call(
        body, name=name, grid=(n_groups, nt),
        in_specs=[cur(0), prev(0), cur(1), plain, plain_prev, plain, chan(CONV_WIDTH), chan(1), wblk, chan(1), wblk,
                  chan(1), chan(1)],
        out_specs=[dxb_spec, dxb_spec, chan(CONV_WIDTH), chan(1), wblk, chan(1), wblk, chan(1), chan(1)],
        out_shape=[jax.ShapeDtypeStruct((t, w), BF16), jax.ShapeDtypeStruct((t, w), BF16),
                   jax.ShapeDtypeStruct(conv_w.shape, F32), jax.ShapeDtypeStruct(conv_b.shape, F32),
                   jax.ShapeDtypeStruct(wa.shape, F32), jax.ShapeDtypeStruct(ba.shape, F32),
                   jax.ShapeDtypeStruct(wx.shape, F32), jax.ShapeDtypeStruct(bx.shape, F32),
                   jax.ShapeDtypeStruct(a_param.shape, F32)],
        scratch_shapes=[pltpu.VMEM((SUBLANES, cg), F32), pltpu.VMEM((SUBLANES, cg), F32),
                        pltpu.VMEM((tt, cg), F32), pltpu.VMEM((tt, cg), F32), pltpu.VMEM((tt, cg), F32)],
        compiler_params=_cparams(("parallel", "arbitrary")),
    )(u, u, u, hs, hs, dy, conv_w, conv_b, wa, ba, wx, bx, a_param)
    return outs


def _fgate_fwd(f, b_f, *, name):
    t, n = f.shape
    tt = _tile(t, 256)

    def body(f_ref, b_ref, cum_ref, carry_ref):
        i = pl.program_id(0)

        @pl.when(i == 0)
        def _():
            carry_ref[...] = jnp.zeros_like(carry_ref)

        z = f_ref[...] + b_ref[...]
        lf = jnp.minimum(z, 0.0) - _log1p(jnp.exp(-jnp.abs(z)))
        row = lax.broadcasted_iota(jnp.int32, (tt, tt), 0)
        col = lax.broadcasted_iota(jnp.int32, (tt, tt), 1)
        tri = (col <= row).astype(F32)
        cum = jnp.dot(tri, lf, precision=HIGHEST, preferred_element_type=F32) + carry_ref[0:1, :]
        cum_ref[...] = cum
        carry_ref[0:1, :] = cum[tt - 1:tt, :]

    return pl.pallas_call(
        body, name=name, grid=(t // tt,),
        in_specs=[pl.BlockSpec((tt, n), lambda i: (i, 0)), pl.BlockSpec((1, n), lambda i: (0, 0))],
        out_specs=pl.BlockSpec((tt, n), lambda i: (i, 0)),
        out_shape=jax.ShapeDtypeStruct((t, n), F32),
        scratch_shapes=[pltpu.VMEM((SUBLANES, n), F32)],
        compiler_params=_cparams(("arbitrary",)),
    )(f, b_f)


def _fgate_bwd(dcum, f, b_f, *, name):
    t, n = f.shape
    tt = _tile(t, 256)
    nt = t // tt

    def body(dc_ref, f_ref, b_ref, df_ref, db_ref, carry_ref):
        i = pl.program_id(0)

        @pl.when(i == 0)
        def _():
            carry_ref[...] = jnp.zeros_like(carry_ref)
            db_ref[...] = jnp.zeros_like(db_ref)

        row = lax.broadcasted_iota(jnp.int32, (tt, tt), 0)
        col = lax.broadcasted_iota(jnp.int32, (tt, tt), 1)
        triu = (col >= row).astype(F32)
        dlf = jnp.dot(triu, dc_ref[...], precision=HIGHEST, preferred_element_type=F32) + carry_ref[0:1, :]
        carry_ref[0:1, :] = dlf[0:1, :]
        z = f_ref[...] + b_ref[...]
        df = dlf * _sigmoid(-z)
        df_ref[...] = df
        db_ref[...] += jnp.sum(df, axis=0, keepdims=True)

    blk = pl.BlockSpec((tt, n), lambda i: (nt - 1 - i, 0))
    vec = pl.BlockSpec((1, n), lambda i: (0, 0))
    return pl.pallas_call(
        body, name=name, grid=(nt,),
        in_specs=[blk, blk, vec], out_specs=[blk, vec],
        out_shape=[jax.ShapeDtypeStruct((t, n), F32), jax.ShapeDtypeStruct((1, n), F32)],
        scratch_shapes=[pltpu.VMEM((SUBLANES, n), F32)],
        compiler_params=_cparams(("arbitrary",)),
    )(dcum, f, b_f)


def _attn_fwd(qkv, ckt, gate, *, name):
    t = qkv.shape[0]
    f = gate.shape[1]
    npair = f // LANES
    tq = _tile(t, 512)
    nq = t // tq
    scale = 1.0 / math.sqrt(FOX_HEAD_DIM)
    nt_dims = (((1,), (1,)), ((), ()))

    def body(q_ref, k_ref, v_ref, ck_ref, g_ref, o_ref, y_ref, l_ref):
        i = pl.program_id(1)
        lane = lax.broadcasted_iota(jnp.int32, (tq, LANES), 1)
        lo = lane < FOX_HEAD_DIM
        q2 = q_ref[...] * scale
        qs = (jnp.where(lo, q2, 0).astype(BF16), jnp.where(lo, 0, q2).astype(BF16))
        row = lax.broadcasted_iota(jnp.int32, (tq, tq), 0)
        col = lax.broadcasted_iota(jnp.int32, (tq, tq), 1)
        causal = col <= row

        def kv_step(j, carry, masked):
            off = pl.multiple_of(j * tq, tq)
            kj = k_ref[pl.ds(off, tq), :]
            vj = v_ref[pl.ds(off, tq), :]
            ck = ck_ref[:, pl.ds(off, tq)]
            new = []
            for h in range(2):
                m, l, acc = carry[h]
                s = lax.dot_general(qs[h], kj, nt_dims, preferred_element_type=F32) - ck[h:h + 1, :]
                if masked:
                    s = jnp.where(causal, s, NEG_INF)
                m_new = jnp.maximum(m, jnp.max(s, axis=-1, keepdims=True))
                alpha = jnp.exp(m - m_new)
                p = jnp.exp(s - m_new)
                l = alpha * l + jnp.sum(p, axis=-1, keepdims=True)
                acc = alpha * acc + jnp.dot(p.astype(BF16), vj, preferred_element_type=F32)
                new.append((m_new, l, acc))
            return tuple(new)

        init = tuple((jnp.full((tq, 1), NEG_INF, F32), jnp.zeros((tq, 1), F32), jnp.zeros((tq, LANES), F32))
                     for _ in range(2))
        carry = lax.fori_loop(0, i, lambda j, c: kv_step(j, c, False), init)
        (m0, l0, a0), (m1, l1, a1) = kv_step(i, carry, True)
        o = jnp.where(lo, a0 / l0, a1 / l1)
        o_ref[...] = o
        gate_v = g_ref[...]
        y_ref[...] = (o * (gate_v * _sigmoid(gate_v))).astype(y_ref.dtype)
        lse_t = jnp.transpose(jnp.where(lo, m0 + jnp.log(l0), m1 + jnp.log(l1)))
        l_ref[0:1, :] = lse_t[0:1, :]
        l_ref[1:2, :] = lse_t[FOX_HEAD_DIM:FOX_HEAD_DIM + 1, :]

    blk = lambda base: pl.BlockSpec((tq, LANES), lambda p, i: (i, base + p))
    full = lambda base: pl.BlockSpec((t, LANES), lambda p, i: (0, base + p))
    return pl.pallas_call(
        body, name=name, grid=(npair, nq),
        in_specs=[blk(0), full(npair), full(2 * npair), pl.BlockSpec((None, 2, t), lambda p, i: (p, 0, 0)), blk(0)],
        out_specs=[blk(0), blk(0), pl.BlockSpec((None, 2, tq), lambda p, i: (p, 0, i))],
        out_shape=[jax.ShapeDtypeStruct((t, f), F32), jax.ShapeDtypeStruct((t, f), BF16),
                   jax.ShapeDtypeStruct((npair, 2, t), F32)],
        compiler_params=_cparams(("parallel", "arbitrary")),
    )(qkv, qkv, qkv, ckt, gate)


def _attn_bwd(qkv, do, lt, dt, cke, *, name):
    t, f = do.shape
    npair = f // LANES
    tk = _tile(t, 512)
    nk = t // tk
    scale = 1.0 / math.sqrt(FOX_HEAD_DIM)
    nt_dims = (((1,), (1,)), ((), ()))
    tn_dims = (((0,), (0,)), ((), ()))

    def body(k_ref, v_ref, q_ref, do_ref, l_ref, d_ref, ck_ref, dq_ref, dk_ref, dv_ref, dck_ref, dcq_ref):
        j = pl.program_id(1)

        @pl.when(j == 0)
        def _():
            dq_ref[...] = jnp.zeros_like(dq_ref)
            dcq_ref[...] = jnp.zeros_like(dcq_ref)

        lane = lax.broadcasted_iota(jnp.int32, (tk, LANES), 1)
        lo = lane < FOX_HEAD_DIM
        sel = (lo, jnp.logical_not(lo))
        kj = k_ref[...]
        vj = v_ref[...]
        km = tuple(jnp.where(sel[h], kj, 0).astype(BF16) for h in range(2))
        ckv = ck_ref[...]
        ckh = (ckv[:, 0:1], ckv[:, FOX_HEAD_DIM:FOX_HEAD_DIM + 1])
        row = lax.broadcasted_iota(jnp.int32, (tk, tk), 0)
        col = lax.broadcasted_iota(jnp.int32, (tk, tk), 1)
        causal = row <= col

        def q_step(i, carry, masked):
            dk_acc, dv_acc, dck = carry
            off = pl.multiple_of(i * tk, tk)
            qi = q_ref[pl.ds(off, tk), :]
            doi = do_ref[pl.ds(off, tk), :]
            lrow = l_ref[:, pl.ds(off, tk)]
            drow = d_ref[:, pl.ds(off, tk)]
            dq_add = jnp.zeros((tk, LANES), F32)
            new_dck = []
            for h in range(2):
                qm = jnp.where(sel[h], qi, 0).astype(BF16)
                dom = jnp.where(sel[h], doi, 0).astype(BF16)
                st = lax.dot_general(kj, qm, nt_dims, preferred_element_type=F32) * scale
                st = st - ckh[h] - lrow[h:h + 1, :]
                if masked:
                    st = jnp.where(causal, st, NEG_INF)
                pt = jnp.exp(st)
                dpt = lax.dot_general(vj, dom, nt_dims, preferred_element_type=F32)
                dst = pt * (dpt - drow[h:h + 1, :])
                ptb = pt.astype(BF16)
                dstb = dst.astype(BF16)
                dv_acc = dv_acc + jnp.dot(ptb, dom, preferred_element_type=F32)
                dk_acc = dk_acc + jnp.dot(dstb, qm, preferred_element_type=F32)
                dq_add = dq_add + lax.dot_general(dstb, km[h], tn_dims, preferred_element_type=F32)
                new_dck.append(dck[h] - jnp.sum(dst, axis=-1, keepdims=True))
                dcq_ref[h:h + 1, pl.ds(off, tk)] += jnp.sum(dst, axis=0, keepdims=True)
            dq_ref[pl.ds(off, tk), :] += dq_add * scale
            return dk_acc, dv_acc, tuple(new_dck)

        zero = jnp.zeros((tk, LANES), F32)
        carry = (zero, zero, (jnp.zeros((tk, 1), F32), jnp.zeros((tk, 1), F32)))
        carry = q_step(j, carry, True)
        dk_acc, dv_acc, dck = lax.fori_loop(j + 1, nk, lambda i, c: q_step(i, c, False), carry)
        dk_ref[...] = (dk_acc * scale).astype(dk_ref.dtype)
        dv_ref[...] = dv_acc.astype(dv_ref.dtype)
        dck_t = jnp.transpose(jnp.where(lo, dck[0], dck[1]))
        dck_ref[0:1, :] = dck_t[0:1, :]
        dck_ref[1:2, :] = dck_t[FOX_HEAD_DIM:FOX_HEAD_DIM + 1, :]

    blk = lambda base: pl.BlockSpec((tk, LANES), lambda p, j: (j, base + p))
    full = lambda base: pl.BlockSpec((t, LANES), lambda p, j: (0, base + p))
    rows = pl.BlockSpec((None, 2, t), lambda p, j: (p, 0, 0))
    return pl.pallas_call(
        body, name=name, grid=(npair, nk),
        in_specs=[blk(npair), blk(2 * npair), full(0), full(0), rows, rows, blk(0)],
        out_specs=[full(0), blk(0), blk(0), pl.BlockSpec((None, 2, tk), lambda p, j: (p, 0, j)), rows],
        out_shape=[jax.ShapeDtypeStruct((t, f), F32), jax.ShapeDtypeStruct((t, f), BF16),
                   jax.ShapeDtypeStruct((t, f), BF16), jax.ShapeDtypeStruct((npair, 2, t), F32),
                   jax.ShapeDtypeStruct((npair, 2, t), F32)],
        compiler_params=_cparams(("parallel", "arbitrary")),
    )(qkv, qkv, qkv, do, lt, dt, cke)


def _fox_post_bwd(dy, o, gate, *, name):
    t, f = dy.shape
    tt = _tile(t, 512)

    def body(dy_ref, o_ref, g_ref, do_ref, dg_ref, dl_ref):
        g = g_ref[...]
        sg = _sigmoid(g)
        dyv = dy_ref[...]
        ov = o_ref[...]
        do = dyv * (g * sg)
        do_ref[...] = do.astype(do_ref.dtype)
        dg_ref[...] = (dyv * ov * (sg * (1.0 + g * (1.0 - sg)))).astype(dg_ref.dtype)
        chan = lax.broadcasted_iota(jnp.int32, (f, LANES), 0)
        head = lax.broadcasted_iota(jnp.int32, (f, LANES), 1)
        pick = (chan // FOX_HEAD_DIM == head).astype(F32)
        dl_ref[...] = jnp.dot(do * ov, pick, precision=HIGHEST, preferred_element_type=F32)

    blk = pl.BlockSpec((tt, f), lambda i: (i, 0))
    return pl.pallas_call(
        body, name=name, grid=(t // tt,),
        in_specs=[blk, blk, blk], out_specs=[blk, blk, pl.BlockSpec((tt, LANES), lambda i: (i, 0))],
        out_shape=[jax.ShapeDtypeStruct((t, f), BF16), jax.ShapeDtypeStruct((t, f), BF16),
                   jax.ShapeDtypeStruct((t, LANES), F32)],
        compiler_params=_cparams(("parallel",)),
    )(dy, o, gate)


def _adamw(w, g, m, v, *, name):
    _, r, c = w.shape
    tr = _tile(r, 256) if r % SUBLANES == 0 else r
    c1 = 1.0 - ADAM_B1 ** ADAM_STEP
    c2 = 1.0 - ADAM_B2 ** ADAM_STEP

    def body(w_ref, g_ref, m_ref, v_ref, d_ref, mo_ref, vo_ref):
        gv = g_ref[...]
        mn = ADAM_B1 * m_ref[...] + (1.0 - ADAM_B1) * gv
        vn = ADAM_B2 * v_ref[...] + (1.0 - ADAM_B2) * (gv * gv)
        mo_ref[...] = mn
        vo_ref[...] = vn
        d_ref[...] = -ADAM_LR * ((mn / c1) / (jnp.sqrt(vn / c2) + ADAM_EPS) + ADAM_WD * w_ref[...])

    blk = pl.BlockSpec((None, tr, c), lambda i: (0, i, 0))
    return pl.pallas_call(
        body, name=name, grid=(r // tr,), in_specs=[blk] * 4, out_specs=[blk] * 3,
        out_shape=[jax.ShapeDtypeStruct((1, r, c), F32)] * 3,
        compiler_params=_cparams(("parallel",)),
    )(w, g, m, v)


def _sum_slots(land, *, name):
    ns, r, c = land.shape
    tr = _tile(r, 64) if r % SUBLANES == 0 else r

    def body(l_ref, o_ref):
        acc = l_ref[0].astype(F32)
        for s in range(1, ns):
            acc = acc + l_ref[s].astype(F32)
        o_ref[...] = acc

    return pl.pallas_call(
        body, name=name, grid=(r // tr,),
        in_specs=[pl.BlockSpec((ns, tr, c), lambda i: (0, i, 0))],
        out_specs=pl.BlockSpec((tr, c), lambda i: (i, 0)),
        out_shape=jax.ShapeDtypeStruct((r, c), F32),
        compiler_params=_cparams(("parallel",)),
    )(land)


ANY = pl.BlockSpec(memory_space=pl.ANY)


def _flip(v, bit):
    return 1 - v if bit else v


def _gather_chips(shards, *, name):
    n = len(shards)
    rels = ((1, 0), (0, 1), (1, 1))

    def body(*refs):
        ins, outs = refs[:n], refs[n:2 * n]
        send, recv, loc = refs[2 * n:]
        x, y, c = lax.axis_index("x"), lax.axis_index("y"), lax.axis_index("c")
        me = 2 * x + y
        copies = []
        for k in range(n):
            cp = pltpu.make_async_copy(ins[k], outs[k].at[me], loc.at[k])
            cp.start()
            copies.append(cp)
        for r, (rx, ry) in enumerate(rels):
            for k in range(n):
                cp = pltpu.make_async_remote_copy(
                    src_ref=ins[k], dst_ref=outs[k].at[me], send_sem=send.at[r * n + k], recv_sem=recv.at[r * n + k],
                    device_id=(_flip(x, rx), _flip(y, ry), c), device_id_type=MESH)
                cp.start()
                copies.append(cp)
        for cp in copies:
            cp.wait()

    return pl.pallas_call(
        body, name=name, in_specs=[ANY] * n, out_specs=[ANY] * n,
        out_shape=[jax.ShapeDtypeStruct((N_CHIPS,) + s.shape, s.dtype) for s in shards],
        scratch_shapes=[pltpu.SemaphoreType.DMA((3 * n,)), pltpu.SemaphoreType.DMA((3 * n,)),
                        pltpu.SemaphoreType.DMA((n,))],
        compiler_params=pltpu.CompilerParams(has_side_effects=True),
    )(*shards)


_RELS7 = tuple((r >> 2 & 1, r >> 1 & 1, r & 1) for r in range(1, N_DEV))


def _scatter_pieces(grads, *, name):
    n = len(grads)

    def body(*refs):
        ins, outs = refs[:n], refs[n:2 * n]
        send, recv, loc = refs[2 * n:]
        x, y, c = lax.axis_index("x"), lax.axis_index("y"), lax.axis_index("c")
        me = 4 * x + 2 * y + c
        copies = []

        def piece(k, px, py, pc):
            half = ins[k].shape[1] // 2
            return ins[k].at[2 * px + py, pl.ds(pc * half, half), :]

        for k in range(n):
            cp = pltpu.make_async_copy(piece(k, x, y, c), outs[k].at[me], loc.at[k])
            cp.start()
            copies.append(cp)
        for r, (rx, ry, rc) in enumerate(_RELS7):
            tx, ty, tc = _flip(x, rx), _flip(y, ry), _flip(c, rc)
            for k in range(n):
                cp = pltpu.make_async_remote_copy(
                    src_ref=piece(k, tx, ty, tc), dst_ref=outs[k].at[me], send_sem=send.at[r * n + k],
                    recv_sem=recv.at[r * n + k], device_id=(tx, ty, tc), device_id_type=MESH)
                cp.start()
                copies.append(cp)
        for cp in copies:
            cp.wait()

    return pl.pallas_call(
        body, name=name, in_specs=[ANY] * n, out_specs=[ANY] * n,
        out_shape=[jax.ShapeDtypeStruct((N_DEV, g.shape[1] // 2, g.shape[2]), g.dtype) for g in grads],
        scratch_shapes=[pltpu.SemaphoreType.DMA((7 * n,)), pltpu.SemaphoreType.DMA((7 * n,)),
                        pltpu.SemaphoreType.DMA((n,))],
        compiler_params=pltpu.CompilerParams(has_side_effects=True),
    )(*grads)


def _join_cores(halves, *, name):
    n = len(halves)

    def body(*refs):
        ins, outs = refs[:n], refs[n:2 * n]
        send, recv, loc = refs[2 * n:]
        x, y, c = lax.axis_index("x"), lax.axis_index("y"), lax.axis_index("c")
        copies = []
        for k in range(n):
            half = ins[k].shape[0]
            mine = outs[k].at[0, pl.ds(c * half, half), :]
            cp = pltpu.make_async_copy(ins[k], mine, loc.at[k])
            cp.start()
            copies.append(cp)
            cp = pltpu.make_async_remote_copy(
                src_ref=ins[k], dst_ref=mine, send_sem=send.at[k], recv_sem=recv.at[k],
                device_id=(x, y, 1 - c), device_id_type=MESH)
            cp.start()
            copies.append(cp)
        for cp in copies:
            cp.wait()

    return pl.pallas_call(
        body, name=name, in_specs=[ANY] * n, out_specs=[ANY] * n,
        out_shape=[jax.ShapeDtypeStruct((1, 2 * h.shape[0], h.shape[1]), h.dtype) for h in halves],
        scratch_shapes=[pltpu.SemaphoreType.DMA((n,)), pltpu.SemaphoreType.DMA((n,)), pltpu.SemaphoreType.DMA((n,))],
        compiler_params=pltpu.CompilerParams(has_side_effects=True),
    )(*halves)


def _gather_all(buf, *, name):
    def body(in_ref, out_ref, send, recv, loc):
        x, y, c = lax.axis_index("x"), lax.axis_index("y"), lax.axis_index("c")
        me = 4 * x + 2 * y + c
        copies = [pltpu.make_async_copy(in_ref, out_ref.at[me], loc.at[0])]
        copies[0].start()
        for r, (rx, ry, rc) in enumerate(_RELS7):
            cp = pltpu.make_async_remote_copy(
                src_ref=in_ref, dst_ref=out_ref.at[me], send_sem=send.at[r], recv_sem=recv.at[r],
                device_id=(_flip(x, rx), _flip(y, ry), _flip(c, rc)), device_id_type=MESH)
            cp.start()
            copies.append(cp)
        for cp in copies:
            cp.wait()

    return pl.pallas_call(
        body, name=name, in_specs=[ANY], out_specs=ANY,
        out_shape=jax.ShapeDtypeStruct((N_DEV,) + buf.shape, buf.dtype),
        scratch_shapes=[pltpu.SemaphoreType.DMA((7,)), pltpu.SemaphoreType.DMA((7,)), pltpu.SemaphoreType.DMA((1,))],
        compiler_params=pltpu.CompilerParams(has_side_effects=True),
    )(buf)


def _pack(arrs):
    flat = []
    for a in arrs:
        v = a.reshape(-1)
        pad = (-v.shape[0]) % LANES
        if pad:
            v = jnp.pad(v, (0, pad))
        flat.append(v)
    v = jnp.concatenate(flat)
    pad = (-v.shape[0]) % (LANES * SUBLANES)
    if pad:
        v = jnp.pad(v, (0, pad))
    return v.reshape(-1, LANES)


def _unpack(buf, shapes):
    v = buf.reshape(-1)
    out, off = [], 0
    for s in shapes:
        n = math.prod(s)
        out.append(v[off:off + n].reshape(s))
        off += n + (-n) % LANES
    return out


def kernel(x, norm_g, final_g, lru_w_in, lru_conv_w, lru_conv_b, lru_wa, lru_ba, lru_wx, lru_bx, lru_a_param, lru_w_out, fox_w_in, fox_b_f, fox_w_out, loss_target, m_norm_g, m_final_g, m_lru_w_in, m_lru_conv_w, m_lru_conv_b, m_lru_wa, m_lru_ba, m_lru_wx, m_lru_bx, m_lru_a_param, m_lru_w_out, m_fox_w_in, m_fox_b_f, m_fox_w_out, v_norm_g, v_final_g, v_lru_w_in, v_lru_conv_w, v_lru_conv_b, v_lru_wa, v_lru_ba, v_lru_wx, v_lru_bx, v_lru_a_param, v_lru_w_out, v_fox_w_in, v_fox_b_f, v_fox_w_out):
    t, d = x.shape[1], x.shape[2]
    w = lru_wa.shape[1] * LRU_BLOCK_W
    f = FOX_HEADS * FOX_HEAD_DIM
    npair = f // LANES
    x0 = x.reshape(t, d)
    tgt = loss_target.reshape(t, d)
    chip = 2 * lax.axis_index("x") + lax.axis_index("y")

    g_lwi, g_lwo, g_fwi, g_fwo, g_cw = _gather_chips(
        [lru_w_in[0].astype(BF16), lru_w_out[0].astype(BF16), fox_w_in[0].astype(BF16), fox_w_out[0].astype(BF16),
         lru_conv_w[0]], name="gather_weights")
    cg = w // 2
    lwi = jnp.concatenate([g_lwi[0], g_lwi[2], g_lwi[1], g_lwi[3]], axis=1)
    lwo = g_lwo.reshape(w, d)
    fwi = jnp.concatenate([g_fwi[s] for s in range(N_CHIPS)], axis=1)
    w_qkv, w_g2 = fwi[:, :3 * f], fwi[:, 3 * f:4 * f]
    w_f = jnp.pad(fwi[:, 4 * f:], ((0, 0), (0, LANES - FOX_HEADS)))
    fwo = g_fwo.reshape(f, d)
    conv_w = jnp.concatenate([g_cw[s] for s in range(N_CHIPS)], axis=1)
    conv_b, ba, bx, a_param = lru_conv_b, lru_ba, lru_bx, lru_a_param
    wa, wx = lru_wa[0], lru_wx[0]
    b_f = jnp.pad(fox_b_f, ((0, 0), (0, LANES - FOX_HEADS)))

    h0 = _rmsnorm(x0, norm_g[0], name="norm0")
    u = _matmul(h0, lwi, name="lru_in")
    y1, hs = _lru_fwd(u, conv_w, conv_b, wa, ba, wx, bx, a_param, cg=cg, name="lru_fwd")
    x1 = _matmul(y1, lwo, add=x0, name="lru_out")
    h1 = _rmsnorm(x1, norm_g[1], name="norm1")
    qkv = _matmul(h1, w_qkv, out_dtype=BF16, name="fox_qkv")
    gate2 = _matmul(h1, w_g2, name="fox_gate")
    flog = _matmul(h1, w_f, name="fox_f")
    cum = _fgate_fwd(flog, b_f, name="fgate_fwd")
    cum16 = cum[:, :FOX_HEADS]
    ckt = cum16.T.reshape(npair, 2, t)
    cke = jnp.repeat(cum16, FOX_HEAD_DIM, axis=1)
    o, y2, lse = _attn_fwd(qkv, ckt, gate2, name="attn_fwd")
    x2 = _matmul(y2, fwo, add=x1, name="fox_out")
    lsum, dx2, dgf = _final_loss(x2, tgt, final_g, name="final_loss")
    loss = lax.psum(0.5 * jnp.sum(lsum) / d, ("x", "y", "c"))

    d_fwo = _matmul(y2, dx2, ta=True, name="d_fox_w_out")
    dy2 = _matmul(dx2, fwo, tb=True, name="d_y2")
    do, dgate2, dl = _fox_post_bwd(dy2, o, gate2, name="fox_post_bwd")
    lt = lse
    dt = dl[:, :FOX_HEADS].T.reshape(npair, 2, t)
    dq, dk, dv, dck, dcq = _attn_bwd(qkv, do, lt, dt, cke, name="attn_bwd")
    dcum = jnp.pad((dck + dcq).reshape(FOX_HEADS, t).T, ((0, 0), (0, LANES - FOX_HEADS)))
    dflog, db_f = _fgate_bwd(dcum, flog, b_f, name="fgate_bwd")
    du2 = jnp.concatenate([dq.astype(BF16), dk, dv, dgate2], axis=1)
    dflog_b = dflog.astype(BF16)
    dh1 = _matmul(du2, fwi[:, :4 * f], tb=True, name="d_h1_a")
    dh1 = _matmul(dflog_b, w_f, tb=True, add=dh1, name="d_h1_b")
    d_fwi_a = _matmul(h1, du2, ta=True, name="d_fox_w_in_a")
    d_fwi_b = _matmul(h1, dflog_b, ta=True, name="d_fox_w_in_b")
    d_fwi = jnp.concatenate([d_fwi_a, d_fwi_b[:, :FOX_HEADS]], axis=1)
    dx1, dg1 = _rmsnorm_bwd(dh1, x1, norm_g[1], dx2, name="norm1_bwd")

    d_lwo = _matmul(y1, dx1, ta=True, name="d_lru_w_out")
    dy1 = _matmul(dx1, lwo, tb=True, name="d_y1")
    dxb, dgate, d_cw, d_cb, d_wa, d_ba, d_wx, d_bx, d_ap = _lru_bwd(
        u, hs, dy1, conv_w, conv_b, wa, ba, wx, bx, a_param, cg=cg, name="lru_bwd")
    du = jnp.concatenate([dxb[:, :cg], dgate[:, :cg], dxb[:, cg:], dgate[:, cg:]], axis=1)
    dh0 = _matmul(du, lwi, tb=True, name="d_h0")
    d_lwi_p = _matmul(h0, du, ta=True, name="d_lru_w_in")
    dx0, dg0 = _rmsnorm_bwd(dh0, x0, norm_g[0], dx1, name="norm0_bwd")

    csz = cg
    g_lwi4 = jnp.stack([d_lwi_p[:, 0:csz], d_lwi_p[:, 2 * csz:3 * csz], d_lwi_p[:, csz:2 * csz],
                        d_lwi_p[:, 3 * csz:]])
    n_fwi = fox_w_in.shape[2]
    g_fwi4 = jnp.stack([d_fwi[:, s * n_fwi:(s + 1) * n_fwi] for s in range(N_CHIPS)])
    g_lwo4 = d_lwo.reshape(N_CHIPS, w // N_CHIPS, d)
    g_fwo4 = d_fwo.reshape(N_CHIPS, f // N_CHIPS, d)
    lands = _scatter_pieces([g_lwi4, g_lwo4, g_fwi4, g_fwo4], name="scatter_grads")
    halves = [_sum_slots(l, name="sum_" + nm) for l, nm in zip(lands, ("lru_w_in", "lru_w_out", "fox_w_in", "fox_w_out"))]
    big_g = _join_cores(halves, name="join_cores")

    small_g = [jnp.concatenate([dg0, dg1], axis=0), dgf.reshape(d), d_cw, d_cb, d_wa, d_ba, d_wx, d_bx, d_ap,
               db_f[:, :FOX_HEADS]]
    gsum = _sum_slots(_gather_all(_pack(small_g), name="gather_small"), name="sum_small")
    zc = jnp.zeros((CONV_WIDTH, w), F32)
    pk_w = _pack([norm_g, final_g, zc, lru_conv_b, lru_wa, lru_ba, lru_wx, lru_bx, lru_a_param, fox_b_f])
    pk_m = _pack([m_norm_g, m_final_g, zc, m_lru_conv_b, m_lru_wa, m_lru_ba, m_lru_wx, m_lru_bx, m_lru_a_param,
                  m_fox_b_f])
    pk_v = _pack([v_norm_g, v_final_g, zc + 1.0, v_lru_conv_b, v_lru_wa, v_lru_ba, v_lru_wx, v_lru_bx,
                  v_lru_a_param, v_fox_b_f])
    s_delta, s_m, s_v = _adamw(pk_w[None], gsum[None], pk_m[None], pk_v[None], name="adamw_small")
    out_shapes = [norm_g.shape, final_g.shape, (CONV_WIDTH, w), lru_conv_b.shape, lru_wa.shape, lru_ba.shape,
                  lru_wx.shape, lru_bx.shape, lru_a_param.shape, fox_b_f.shape]
    sg = _unpack(gsum, out_shapes)
    sd = _unpack(s_delta, out_shapes)
    sm = _unpack(s_m, out_shapes)
    sv = _unpack(s_v, out_shapes)

    ncw = lru_conv_w.shape[2]
    g_cw_loc = lax.dynamic_slice_in_dim(sg[2], chip * ncw, ncw, axis=1)
    g_cw_loc = g_cw_loc[None]
    cw_d, cw_m, cw_v = _adamw(lru_conv_w, g_cw_loc, m_lru_conv_w, v_lru_conv_w, name="adamw_conv_w")

    big = []
    for nm, wt, g, mm, vv in (("lru_w_in", lru_w_in, big_g[0], m_lru_w_in, v_lru_w_in),
                              ("lru_w_out", lru_w_out, big_g[1], m_lru_w_out, v_lru_w_out),
                              ("fox_w_in", fox_w_in, big_g[2], m_fox_w_in, v_fox_w_in),
                              ("fox_w_out", fox_w_out, big_g[3], m_fox_w_out, v_fox_w_out)):
        big.append((g,) + tuple(_adamw(wt, g, mm, vv, name="adamw_" + nm)))

    def assemble(idx):
        small = (sg, sd, sm, sv)[idx]
        cw = (g_cw_loc, cw_d, cw_m, cw_v)[idx]
        return [small[0], small[1], big[0][idx], cw, small[3], small[4], small[5], small[6], small[7], small[8],
                big[1][idx], big[2][idx], small[9], big[3][idx]]

    grad_x = dx0.reshape(1, t, d)
    return (loss, grad_x, *assemble(0), *assemble(1), *assemble(2), *assemble(3))
```

```python
import functools
import math

import jax
import jax.numpy as jnp
from jax import lax
from jax.experimental import pallas as pl
from jax.experimental.pallas import tpu as pltpu

F32 = jnp.float32
BF16 = jnp.bfloat16

EPS = 1e-6
LRU_C = 8.0
LRU_BLOCK_W = 128
CONV_WIDTH = 4
FOX_HEADS = 16
FOX_HEAD_DIM = 64
NEG_INF = -1e30
ADAM_LR = 0.001
ADAM_B1 = 0.9
ADAM_B2 = 0.999
ADAM_EPS = 1e-08
ADAM_WD = 0.01
ADAM_STEP = 10

LANES = 128
SUBLANES = 8
VMEM_LIMIT = 56 * 1024 * 1024
N_CHIPS = 4
N_DEV = 8
MESH = pl.DeviceIdType.MESH
HIGHEST = lax.Precision.HIGHEST


def _tile(n, pref):
    t = min(n, pref)
    while n % t:
        t //= 2
    return t


def _cparams(dims=None):
    return pltpu.CompilerParams(dimension_semantics=dims, vmem_limit_bytes=VMEM_LIMIT)


def _sigmoid(x):
    return jax.nn.sigmoid(x)


def _log1p(x):
    u = 1.0 + x
    return jnp.where(u == 1.0, x, jnp.log(u) * x / (u - 1.0))


def _softplus(x):
    return jnp.maximum(x, 0.0) + _log1p(jnp.exp(-jnp.abs(x)))


MM_TILE = 1024
MM_FULL_K = 1536


def _matmul(a, b, *, name, ta=False, tb=False, out_dtype=F32, add=None, tm=MM_TILE, tn=MM_TILE, tk=None):
    if ta:
        kdim, m = a.shape
    else:
        m, kdim = a.shape
    if tb:
        n, kb = b.shape
    else:
        kb, n = b.shape
    assert kdim == kb, (a.shape, b.shape, ta, tb)
    if tk is None:
        tk = kdim if kdim <= MM_FULL_K else MM_TILE
    tm, tn, tk = _tile(m, tm), _tile(n, tn), _tile(kdim, tk)
    nk = kdim // tk
    dn = (((0 if ta else 1,), (1 if tb else 0,)), ((), ()))
    has_add = add is not None

    def body(*refs):
        if has_add:
            a_ref, b_ref, add_ref, o_ref = refs[:4]
        else:
            a_ref, b_ref, o_ref = refs[:3]
        part = lax.dot_general(a_ref[...].astype(BF16), b_ref[...].astype(BF16), dn, preferred_element_type=F32)

        def finish(r):
            if has_add:
                r = r + add_ref[...].astype(F32)
            o_ref[...] = r.astype(o_ref.dtype)

        if nk == 1:
            finish(part)
            return
        acc_ref = refs[-1]
        k = pl.program_id(2)

        @pl.when(k == 0)
        def _():
            acc_ref[...] = part

        @pl.when(k > 0)
        def _():
            acc_ref[...] += part

        @pl.when(k == nk - 1)
        def _():
            finish(acc_ref[...])

    a_spec = pl.BlockSpec((tk, tm), lambda i, j, k: (k, i)) if ta else pl.BlockSpec((tm, tk), lambda i, j, k: (i, k))
    b_spec = pl.BlockSpec((tn, tk), lambda i, j, k: (j, k)) if tb else pl.BlockSpec((tk, tn), lambda i, j, k: (k, j))
    o_spec = pl.BlockSpec((tm, tn), lambda i, j, k: (i, j))
    in_specs = [a_spec, b_spec] + ([o_spec] if has_add else [])
    args = (a, b) + ((add,) if has_add else ())
    return pl.pallas_call(
        body, name=name, grid=(m // tm, n // tn, nk), in_specs=in_specs, out_specs=o_spec,
        out_shape=jax.ShapeDtypeStruct((m, n), out_dtype),
        scratch_shapes=[pltpu.VMEM((tm, tn), F32)] if nk > 1 else [],
        compiler_params=_cparams(("parallel", "parallel", "arbitrary")),
    )(*args)


def _rmsnorm(x, g, *, name):
    t, d = x.shape
    tt = _tile(t, 512)

    def body(x_ref, g_ref, o_ref):
        xf = x_ref[...]
        rstd = lax.rsqrt(jnp.mean(xf * xf, axis=-1, keepdims=True) + EPS)
        o_ref[...] = (xf * rstd * g_ref[...]).astype(o_ref.dtype)

    return pl.pallas_call(
        body, name=name, grid=(t // tt,),
        in_specs=[pl.BlockSpec((tt, d), lambda i: (i, 0)), pl.BlockSpec((1, d), lambda i: (0, 0))],
        out_specs=pl.BlockSpec((tt, d), lambda i: (i, 0)),
        out_shape=jax.ShapeDtypeStruct((t, d), BF16),
        compiler_params=_cparams(("parallel",)),
    )(x, g.reshape(1, d))


def _rmsnorm_bwd(dh, x, g, dres, *, name):
    t, d = x.shape
    tt = _tile(t, 512)

    def body(dh_ref, x_ref, g_ref, dres_ref, dx_ref, dg_ref):
        i = pl.program_id(0)

        @pl.when(i == 0)
        def _():
            dg_ref[...] = jnp.zeros_like(dg_ref)

        xf = x_ref[...]
        rstd = lax.rsqrt(jnp.mean(xf * xf, axis=-1, keepdims=True) + EPS)
        xhat = xf * rstd
        dhf = dh_ref[...].astype(F32)
        dxhat = dhf * g_ref[...]
        mt = jnp.mean(dxhat * xhat, axis=-1, keepdims=True)
        dx_ref[...] = dres_ref[...] + rstd * (dxhat - xhat * mt)
        dg_ref[...] += jnp.sum(dhf * xhat, axis=0, keepdims=True)

    blk = pl.BlockSpec((tt, d), lambda i: (i, 0))
    vec = pl.BlockSpec((1, d), lambda i: (0, 0))
    return pl.pallas_call(
        body, name=name, grid=(t // tt,),
        in_specs=[blk, blk, vec, blk], out_specs=[blk, vec],
        out_shape=[jax.ShapeDtypeStruct((t, d), F32), jax.ShapeDtypeStruct((1, d), F32)],
        compiler_params=_cparams(("arbitrary",)),
    )(dh, x, g.reshape(1, d), dres)


def _final_loss(x2, tgt, g, *, name):
    t, d = x2.shape
    tt = _tile(t, 512)

    def body(x_ref, t_ref, g_ref, l_ref, dx_ref, dg_ref):
        i = pl.program_id(0)

        @pl.when(i == 0)
        def _():
            dg_ref[...] = jnp.zeros_like(dg_ref)
            l_ref[...] = jnp.zeros_like(l_ref)

        xf = x_ref[...]
        gg = g_ref[...]
        rstd = lax.rsqrt(jnp.mean(xf * xf, axis=-1, keepdims=True) + EPS)
        xhat = xf * rstd
        err = xhat * gg - t_ref[...]
        l_ref[...] += jnp.sum(err * err, axis=0, keepdims=True)
        dy = err * (1.0 / d)
        dxhat = dy * gg
        mt = jnp.mean(dxhat * xhat, axis=-1, keepdims=True)
        dx_ref[...] = rstd * (dxhat - xhat * mt)
        dg_ref[...] += jnp.sum(dy * xhat, axis=0, keepdims=True)

    blk = pl.BlockSpec((tt, d), lambda i: (i, 0))
    vec = pl.BlockSpec((1, d), lambda i: (0, 0))
    return pl.pallas_call(
        body, name=name, grid=(t // tt,),
        in_specs=[blk, blk, vec], out_specs=[vec, blk, vec],
        out_shape=[jax.ShapeDtypeStruct((1, d), F32), jax.ShapeDtypeStruct((t, d), F32),
                   jax.ShapeDtypeStruct((1, d), F32)],
        compiler_params=_cparams(("arbitrary",)),
    )(x2, tgt, g.reshape(1, d))


def _shift_down(prev8, cur, s):
    ext = jnp.concatenate([prev8, cur], axis=0)
    if s == 0:
        return cur
    return pltpu.roll(ext, s, 0)[SUBLANES:, :]


def _shift_up(cur, next8, s):
    if s == 0:
        return cur
    n = cur.shape[0]
    ext = jnp.concatenate([cur, next8], axis=0)
    return pltpu.roll(ext, n + SUBLANES - s, 0)[:n, :]


def _lru_gates(xc, wa, ba, wx, bx, sp):
    xcb = xc.astype(BF16)
    r = _sigmoid(jnp.dot(xcb, wa, preferred_element_type=F32) + ba)
    ig = _sigmoid(jnp.dot(xcb, wx, preferred_element_type=F32) + bx)
    log_a = -LRU_C * r * sp
    a = jnp.exp(log_a)
    mult = jnp.sqrt(-jnp.tanh(log_a) * (a * a + 1.0))
    return r, ig, a, mult


def _lru_specs(tt, cg, n_groups, nt, reverse):
    ncol = cg // LANES
    if reverse:
        ti = lambda i: nt - 1 - i
    else:
        ti = lambda i: i
    hb = tt // SUBLANES
    cur = lambda col: pl.BlockSpec((tt, cg), lambda g, i: (ti(i), 2 * g + col))
    prev = lambda col: pl.BlockSpec((SUBLANES, cg), lambda g, i: (jnp.maximum(ti(i) * hb - 1, 0), 2 * g + col))
    chan = lambda rows: pl.BlockSpec((rows, cg), lambda g, i: (0, g))
    wblk = pl.BlockSpec((ncol, LRU_BLOCK_W, LRU_BLOCK_W), lambda g, i: (g, 0, 0))
    plain = pl.BlockSpec((tt, cg), lambda g, i: (ti(i), g))
    plain_prev = pl.BlockSpec((SUBLANES, cg), lambda g, i: (jnp.maximum(ti(i) * hb - 1, 0), g))
    return cur, prev, chan, wblk, plain, plain_prev


def _lru_fwd(u, conv_w, conv_b, wa, ba, wx, bx, a_param, *, cg, name):
    t, w2 = u.shape
    w = w2 // 2
    n_groups = w // cg
    ncol = cg // LANES
    tt = _tile(t, 256)
    nt = t // tt
    cur, prev, chan, wblk, plain, _ = _lru_specs(tt, cg, n_groups, nt, False)

    def body(xb_ref, xp_ref, gate_ref, cw_ref, cb_ref, wa_ref, ba_ref, wx_ref, bx_ref, ap_ref,
             y_ref, hs_ref, h_ref, a_s, b_s):
        i = pl.program_id(1)

        @pl.when(i == 0)
        def _():
            h_ref[...] = jnp.zeros_like(h_ref)

        keep = (i > 0).astype(F32)
        for n in range(ncol):
            sl = slice(n * LANES, (n + 1) * LANES)
            xb = xb_ref[:, sl]
            xp = xp_ref[:, sl] * keep
            xc = cb_ref[:, sl] + cw_ref[3:4, sl] * xb
            for s in range(1, CONV_WIDTH):
                xc = xc + cw_ref[3 - s:4 - s, sl] * _shift_down(xp, xb, s)
            sp = _softplus(-ap_ref[:, sl])
            _, ig, a, mult = _lru_gates(xc, wa_ref[n].astype(BF16), ba_ref[:, sl],
                                        wx_ref[n].astype(BF16), bx_ref[:, sl], sp)
            a_s[:, sl] = a
            b_s[:, sl] = mult * (ig * xc)

        def step(tr, h):
            h = a_s[pl.ds(tr, 1), :] * h + b_s[pl.ds(tr, 1), :]
            hs_ref[pl.ds(tr, 1), :] = h
            return h

        h = lax.fori_loop(0, tt, step, h_ref[0:1, :], unroll=8)
        h_ref[0:1, :] = h
        gate = gate_ref[...]
        y_ref[...] = (hs_ref[...] * (gate * _sigmoid(gate))).astype(y_ref.dtype)

    return pl.pallas_call(
        body, name=name, grid=(n_groups, nt),
        in_specs=[cur(0), prev(0), cur(1), chan(CONV_WIDTH), chan(1), wblk, chan(1), wblk, chan(1), chan(1)],
        out_specs=[plain, plain],
        out_shape=[jax.ShapeDtypeStruct((t, w), BF16), jax.ShapeDtypeStruct((t, w), F32)],
        scratch_shapes=[pltpu.VMEM((SUBLANES, cg), F32), pltpu.VMEM((tt, cg), F32), pltpu.VMEM((tt, cg), F32)],
        compiler_params=_cparams(("parallel", "arbitrary")),
    )(u, u, u, conv_w, conv_b, wa, ba, wx, bx, a_param)


def _lru_bwd(u, hs, dy, conv_w, conv_b, wa, ba, wx, bx, a_param, *, cg, name):
    t, w2 = u.shape
    w = w2 // 2
    n_groups = w // cg
    ncol = cg // LANES
    tt = _tile(t, 256)
    nt = t // tt
    cur, prev, chan, wblk, plain, plain_prev = _lru_specs(tt, cg, n_groups, nt, True)
    tn_dims = (((0,), (0,)), ((), ()))
    nt_dims = (((1,), (1,)), ((), ()))

    def body(xb_ref, xp_ref, gate_ref, hs_ref, hp_ref, dy_ref, cw_ref, cb_ref, wa_ref, ba_ref, wx_ref, bx_ref,
             ap_ref, dxb_ref, dgate_ref, dcw_ref, dcb_ref, dwa_ref, dba_ref, dwx_ref, dbx_ref, dsp_ref,
             c_ref, nx_ref, a_s, dhs_s, lam_s):
        i = pl.program_id(1)
        first_time_block = i == nt - 1

        @pl.when(i == 0)
        def _():
            c_ref[...] = jnp.zeros_like(c_ref)
            nx_ref[...] = jnp.zeros_like(nx_ref)
            for r in (dcw_ref, dcb_ref, dwa_ref, dba_ref, dwx_ref, dbx_ref, dsp_ref):
                r[...] = jnp.zeros_like(r)

        keep = jnp.where(first_time_block, 0.0, 1.0).astype(F32)
        gate = gate_ref[...]
        sg = _sigmoid(gate)
        dyv = dy_ref[...]
        hsv = hs_ref[...]
        dhs_s[...] = dyv * (gate * sg)
        dgate_ref[...] = (dyv * hsv * (sg * (1.0 + gate * (1.0 - sg)))).astype(dgate_ref.dtype)

        saved = []
        for n in range(ncol):
            sl = slice(n * LANES, (n + 1) * LANES)
            xb = xb_ref[:, sl]
            xp = xp_ref[:, sl] * keep
            shifted = [xb] + [_shift_down(xp, xb, s) for s in range(1, CONV_WIDTH)]
            xc = cb_ref[:, sl] + cw_ref[3:4, sl] * xb
            for s in range(1, CONV_WIDTH):
                xc = xc + cw_ref[3 - s:4 - s, sl] * shifted[s]
            sp = _softplus(-ap_ref[:, sl])
            wab = wa_ref[n].astype(BF16)
            wxb = wx_ref[n].astype(BF16)
            r, ig, a, mult = _lru_gates(xc, wab, ba_ref[:, sl], wxb, bx_ref[:, sl], sp)
            a_s[:, sl] = a
            saved.append((sl, shifted, xc, sp, wab, wxb, r, ig, a, mult))

        def step(k, c):
            tr = tt - 1 - k
            lam = dhs_s[pl.ds(tr, 1), :] + c
            lam_s[pl.ds(tr, 1), :] = lam
            return a_s[pl.ds(tr, 1), :] * lam

        c_ref[0:1, :] = lax.fori_loop(0, tt, step, c_ref[0:1, :], unroll=8)

        for n in range(ncol):
            sl, shifted, xc, sp, wab, wxb, r, ig, a, mult = saved[n]
            lam = lam_s[:, sl]
            hprev = _shift_down(hp_ref[:, sl] * keep, hs_ref[:, sl], 1)
            da = lam * hprev
            dmult = lam * (ig * xc)
            dlog_a = da * a - dmult * (a * a / mult)
            di = lam * (mult * xc)
            dxc = lam * (mult * ig)
            dr = dlog_a * (-LRU_C * sp)
            dsp_ref[:, sl] += jnp.sum(dlog_a * (-LRU_C * r), axis=0, keepdims=True)
            dza = dr * (r * (1.0 - r))
            dzx = di * (ig * (1.0 - ig))
            dba_ref[:, sl] += jnp.sum(dza, axis=0, keepdims=True)
            dbx_ref[:, sl] += jnp.sum(dzx, axis=0, keepdims=True)
            xcb = xc.astype(BF16)
            dzab = dza.astype(BF16)
            dzxb = dzx.astype(BF16)
            dwa_ref[n] += lax.dot_general(xcb, dzab, tn_dims, preferred_element_type=F32)
            dwx_ref[n] += lax.dot_general(xcb, dzxb, tn_dims, preferred_element_type=F32)
            dxc = dxc + lax.dot_general(dzab, wab, nt_dims, preferred_element_type=F32)
            dxc = dxc + lax.dot_general(dzxb, wxb, nt_dims, preferred_element_type=F32)
            dcb_ref[:, sl] += jnp.sum(dxc, axis=0, keepdims=True)
            for s in range(CONV_WIDTH):
                dcw_ref[3 - s:4 - s, sl] += jnp.sum(dxc * shifted[s], axis=0, keepdims=True)
            nx = nx_ref[:, sl]
            dxb = cw_ref[3:4, sl] * dxc
            for s in range(1, CONV_WIDTH):
                dxb = dxb + cw_ref[3 - s:4 - s, sl] * _shift_up(dxc, nx, s)
            dxb_ref[:, sl] = dxb.astype(dxb_ref.dtype)
            nx_ref[:, sl] = dxc[0:SUBLANES, :]

        @pl.when(first_time_block)
        def _():
            dsp_ref[...] = dsp_ref[...] * (-_sigmoid(-ap_ref[...]))

    dxb_spec = pl.BlockSpec((tt, cg), lambda g, i: (nt - 1 - i, g))
    outs = pl.pallas_call(
        body, name=name, grid=(n_groups, nt),
        in_specs=[cur(0), prev(0), cur(1), plain, plain_prev, plain, chan(CONV_WIDTH), chan(1), wblk, chan(1), wblk,
                  chan(1), chan(1)],
        out_specs=[dxb_spec, dxb_spec, chan(CONV_WIDTH), chan(1), wblk, chan(1), wblk, chan(1), chan(1)],
        out_shape=[jax.ShapeDtypeStruct((t, w), BF16), jax.ShapeDtypeStruct((t, w), BF16),
                   jax.ShapeDtypeStruct(conv_w.shape, F32), jax.ShapeDtypeStruct(conv_b.shape, F32),
                   jax.ShapeDtypeStruct(wa.shape, F32), jax.ShapeDtypeStruct(ba.shape, F32),
                   jax.ShapeDtypeStruct(wx.shape, F32), jax.ShapeDtypeStruct(bx.shape, F32),
                   jax.ShapeDtypeStruct(a_param.shape, F32)],
        scratch_shapes=[pltpu.VMEM((SUBLANES, cg), F32), pltpu.VMEM((SUBLANES, cg), F32),
                        pltpu.VMEM((tt, cg), F32), pltpu.VMEM((tt, cg), F32), pltpu.VMEM((tt, cg), F32)],
        compiler_params=_cparams(("parallel", "arbitrary")),
    )(u, u, u, hs, hs, dy, conv_w, conv_b, wa, ba, wx, bx, a_param)
    return outs


def _fgate_fwd(f, b_f, *, name):
    t, n = f.shape
    tt = _tile(t, 256)

    def body(f_ref, b_ref, cum_ref, carry_ref):
        i = pl.program_id(0)

        @pl.when(i == 0)
        def _():
            carry_ref[...] = jnp.zeros_like(carry_ref)

        z = f_ref[...] + b_ref[...]
        lf = jnp.minimum(z, 0.0) - _log1p(jnp.exp(-jnp.abs(z)))
        row = lax.broadcasted_iota(jnp.int32, (tt, tt), 0)
        col = lax.broadcasted_iota(jnp.int32, (tt, tt), 1)
        tri = (col <= row).astype(F32)
        cum = jnp.dot(tri, lf, precision=HIGHEST, preferred_element_type=F32) + carry_ref[0:1, :]
        cum_ref[...] = cum
        carry_ref[0:1, :] = cum[tt - 1:tt, :]

    return pl.pallas_call(
        body, name=name, grid=(t // tt,),
        in_specs=[pl.BlockSpec((tt, n), lambda i: (i, 0)), pl.BlockSpec((1, n), lambda i: (0, 0))],
        out_specs=pl.BlockSpec((tt, n), lambda i: (i, 0)),
        out_shape=jax.ShapeDtypeStruct((t, n), F32),
        scratch_shapes=[pltpu.VMEM((SUBLANES, n), F32)],
        compiler_params=_cparams(("arbitrary",)),
    )(f, b_f)


def _fgate_bwd(dcum, f, b_f, *, name):
    t, n = f.shape
    tt = _tile(t, 256)
    nt = t // tt

    def body(dc_ref, f_ref, b_ref, df_ref, db_ref, carry_ref):
        i = pl.program_id(0)

        @pl.when(i == 0)
        def _():
            carry_ref[...] = jnp.zeros_like(carry_ref)
            db_ref[...] = jnp.zeros_like(db_ref)

        row = lax.broadcasted_iota(jnp.int32, (tt, tt), 0)
        col = lax.broadcasted_iota(jnp.int32, (tt, tt), 1)
        triu = (col >= row).astype(F32)
        dlf = jnp.dot(triu, dc_ref[...], precision=HIGHEST, preferred_element_type=F32) + carry_ref[0:1, :]
        carry_ref[0:1, :] = dlf[0:1, :]
        z = f_ref[...] + b_ref[...]
        df = dlf * _sigmoid(-z)
        df_ref[...] = df
        db_ref[...] += jnp.sum(df, axis=0, keepdims=True)

    blk = pl.BlockSpec((tt, n), lambda i: (nt - 1 - i, 0))
    vec = pl.BlockSpec((1, n), lambda i: (0, 0))
    return pl.pallas_call(
        body, name=name, grid=(nt,),
        in_specs=[blk, blk, vec], out_specs=[blk, vec],
        out_shape=[jax.ShapeDtypeStruct((t, n), F32), jax.ShapeDtypeStruct((1, n), F32)],
        scratch_shapes=[pltpu.VMEM((SUBLANES, n), F32)],
        compiler_params=_cparams(("arbitrary",)),
    )(dcum, f, b_f)


def _attn_fwd(qkv, ckt, gate, *, name):
    t = qkv.shape[0]
    f = gate.shape[1]
    npair = f // LANES
    tq = _tile(t, 512)
    nq = t // tq
    scale = 1.0 / math.sqrt(FOX_HEAD_DIM)
    nt_dims = (((1,), (1,)), ((), ()))

    def body(q_ref, k_ref, v_ref, ck_ref, g_ref, o_ref, y_ref, l_ref):
        i = pl.program_id(1)
        lane = lax.broadcasted_iota(jnp.int32, (tq, LANES), 1)
        lo = lane < FOX_HEAD_DIM
        q2 = q_ref[...] * scale
        qs = (jnp.where(lo, q2, 0).astype(BF16), jnp.where(lo, 0, q2).astype(BF16))
        row = lax.broadcasted_iota(jnp.int32, (tq, tq), 0)
        col = lax.broadcasted_iota(jnp.int32, (tq, tq), 1)
        causal = col <= row

        def kv_step(j, carry, masked):
            off = pl.multiple_of(j * tq, tq)
            kj = k_ref[pl.ds(off, tq), :]
            vj = v_ref[pl.ds(off, tq), :]
            ck = ck_ref[:, pl.ds(off, tq)]
            new = []
            for h in range(2):
                m, l, acc = carry[h]
                s = lax.dot_general(qs[h], kj, nt_dims, preferred_element_type=F32) - ck[h:h + 1, :]
                if masked:
                    s = jnp.where(causal, s, NEG_INF)
                m_new = jnp.maximum(m, jnp.max(s, axis=-1, keepdims=True))
                alpha = jnp.exp(m - m_new)
                p = jnp.exp(s - m_new)
                l = alpha * l + jnp.sum(p, axis=-1, keepdims=True)
                acc = alpha * acc + jnp.dot(p.astype(BF16), vj, preferred_element_type=F32)
                new.append((m_new, l, acc))
            return tuple(new)

        init = tuple((jnp.full((tq, 1), NEG_INF, F32), jnp.zeros((tq, 1), F32), jnp.zeros((tq, LANES), F32))
                     for _ in range(2))
        carry = lax.fori_loop(0, i, lambda j, c: kv_step(j, c, False), init)
        (m0, l0, a0), (m1, l1, a1) = kv_step(i, carry, True)
        o = jnp.where(lo, a0 / l0, a1 / l1)
        o_ref[...] = o
        gate_v = g_ref[...]
        y_ref[...] = (o * (gate_v * _sigmoid(gate_v))).astype(y_ref.dtype)
        lse_t = jnp.transpose(jnp.where(lo, m0 + jnp.log(l0), m1 + jnp.log(l1)))
        l_ref[0:1, :] = lse_t[0:1, :]
        l_ref[1:2, :] = lse_t[FOX_HEAD_DIM:FOX_HEAD_DIM + 1, :]

    blk = lambda base: pl.BlockSpec((tq, LANES), lambda p, i: (i, base + p))
    full = lambda base: pl.BlockSpec((t, LANES), lambda p, i: (0, base + p))
    return pl.pallas_call(
        body, name=name, grid=(npair, nq),
        in_specs=[blk(0), full(npair), full(2 * npair), pl.BlockSpec((None, 2, t), lambda p, i: (p, 0, 0)), blk(0)],
        out_specs=[blk(0), blk(0), pl.BlockSpec((None, 2, tq), lambda p, i: (p, 0, i))],
        out_shape=[jax.ShapeDtypeStruct((t, f), F32), jax.ShapeDtypeStruct((t, f), BF16),
                   jax.ShapeDtypeStruct((npair, 2, t), F32)],
        compiler_params=_cparams(("parallel", "arbitrary")),
    )(qkv, qkv, qkv, ckt, gate)


def _attn_bwd(qkv, do, lt, dt, cke, *, name):
    t, f = do.shape
    npair = f // LANES
    tk = _tile(t, 512)
    nk = t // tk
    scale = 1.0 / math.sqrt(FOX_HEAD_DIM)
    nt_dims = (((1,), (1,)), ((), ()))
    tn_dims = (((0,), (0,)), ((), ()))

    def body(k_ref, v_ref, q_ref, do_ref, l_ref, d_ref, ck_ref, dq_ref, dk_ref, dv_ref, dck_ref, dcq_ref):
        j = pl.program_id(1)

        @pl.when(j == 0)
        def _():
            dq_ref[...] = jnp.zeros_like(dq_ref)
            dcq_ref[...] = jnp.zeros_like(dcq_ref)

        lane = lax.broadcasted_iota(jnp.int32, (tk, LANES), 1)
        lo = lane < FOX_HEAD_DIM
        sel = (lo, jnp.logical_not(lo))
        kj = k_ref[...]
        vj = v_ref[...]
        km = tuple(jnp.where(sel[h], kj, 0).astype(BF16) for h in range(2))
        ckv = ck_ref[...]
        ckh = (ckv[:, 0:1], ckv[:, FOX_HEAD_DIM:FOX_HEAD_DIM + 1])
        row = lax.broadcasted_iota(jnp.int32, (tk, tk), 0)
        col = lax.broadcasted_iota(jnp.int32, (tk, tk), 1)
        causal = row <= col

        def q_step(i, carry, masked):
            dk_acc, dv_acc, dck = carry
            off = pl.multiple_of(i * tk, tk)
            qi = q_ref[pl.ds(off, tk), :]
            doi = do_ref[pl.ds(off, tk), :]
            lrow = l_ref[:, pl.ds(off, tk)]
            drow = d_ref[:, pl.ds(off, tk)]
            dq_add = jnp.zeros((tk, LANES), F32)
            new_dck = []
            for h in range(2):
                qm = jnp.where(sel[h], qi, 0).astype(BF16)
                dom = jnp.where(sel[h], doi, 0).astype(BF16)
                st = lax.dot_general(kj, qm, nt_dims, preferred_element_type=F32) * scale
                st = st - ckh[h] - lrow[h:h + 1, :]
                if masked:
                    st = jnp.where(causal, st, NEG_INF)
                pt = jnp.exp(st)
                dpt = lax.dot_general(vj, dom, nt_dims, preferred_element_type=F32)
                dst = pt * (dpt - drow[h:h + 1, :])
                ptb = pt.astype(BF16)
                dstb = dst.astype(BF16)
                dv_acc = dv_acc + jnp.dot(ptb, dom, preferred_element_type=F32)
                dk_acc = dk_acc + jnp.dot(dstb, qm, preferred_element_type=F32)
                dq_add = dq_add + lax.dot_general(dstb, km[h], tn_dims, preferred_element_type=F32)
                new_dck.append(dck[h] - jnp.sum(dst, axis=-1, keepdims=True))
                dcq_ref[h:h + 1, pl.ds(off, tk)] += jnp.sum(dst, axis=0, keepdims=True)
            dq_ref[pl.ds(off, tk), :] += dq_add * scale
            return dk_acc, dv_acc, tuple(new_dck)

        zero = jnp.zeros((tk, LANES), F32)
        carry = (zero, zero, (jnp.zeros((tk, 1), F32), jnp.zeros((tk, 1), F32)))
        carry = q_step(j, carry, True)
        dk_acc, dv_acc, dck = lax.fori_loop(j + 1, nk, lambda i, c: q_step(i, c, False), carry)
        dk_ref[...] = (dk_acc * scale).astype(dk_ref.dtype)
        dv_ref[...] = dv_acc.astype(dv_ref.dtype)
        dck_t = jnp.transpose(jnp.where(lo, dck[0], dck[1]))
        dck_ref[0:1, :] = dck_t[0:1, :]
        dck_ref[1:2, :] = dck_t[FOX_HEAD_DIM:FOX_HEAD_DIM + 1, :]

    blk = lambda base: pl.BlockSpec((tk, LANES), lambda p, j: (j, base + p))
    full = lambda base: pl.BlockSpec((t, LANES), lambda p, j: (0, base + p))
    rows = pl.BlockSpec((None, 2, t), lambda p, j: (p, 0, 0))
    return pl.pallas_call(
        body, name=name, grid=(npair, nk),
        in_specs=[blk(npair), blk(2 * npair), full(0), full(0), rows, rows, blk(0)],
        out_specs=[full(0), blk(0), blk(0), pl.BlockSpec((None, 2, tk), lambda p, j: (p, 0, j)), rows],
        out_shape=[jax.ShapeDtypeStruct((t, f), F32), jax.ShapeDtypeStruct((t, f), BF16),
                   jax.ShapeDtypeStruct((t, f), BF16), jax.ShapeDtypeStruct((npair, 2, t), F32),
                   jax.ShapeDtypeStruct((npair, 2, t), F32)],
        compiler_params=_cparams(("parallel", "arbitrary")),
    )(qkv, qkv, qkv, do, lt, dt, cke)


def _fox_post_bwd(dy, o, gate, *, name):
    t, f = dy.shape
    tt = _tile(t, 512)

    def body(dy_ref, o_ref, g_ref, do_ref, dg_ref, dl_ref):
        g = g_ref[...]
        sg = _sigmoid(g)
        dyv = dy_ref[...]
        ov = o_ref[...]
        do = dyv * (g * sg)
        do_ref[...] = do.astype(do_ref.dtype)
        dg_ref[...] = (dyv * ov * (sg * (1.0 + g * (1.0 - sg)))).astype(dg_ref.dtype)
        chan = lax.broadcasted_iota(jnp.int32, (f, LANES), 0)
        head = lax.broadcasted_iota(jnp.int32, (f, LANES), 1)
        pick = (chan // FOX_HEAD_DIM == head).astype(F32)
        dl_ref[...] = jnp.dot(do * ov, pick, precision=HIGHEST, preferred_element_type=F32)

    blk = pl.BlockSpec((tt, f), lambda i: (i, 0))
    return pl.pallas_call(
        body, name=name, grid=(t // tt,),
        in_specs=[blk, blk, blk], out_specs=[blk, blk, pl.BlockSpec((tt, LANES), lambda i: (i, 0))],
        out_shape=[jax.ShapeDtypeStruct((t, f), BF16), jax.ShapeDtypeStruct((t, f), BF16),
                   jax.ShapeDtypeStruct((t, LANES), F32)],
        compiler_params=_cparams(("parallel",)),
    )(dy, o, gate)


def _adamw(w, g, m, v, *, name):
    _, r, c = w.shape
    tr = _tile(r, 256) if r % SUBLANES == 0 else r
    c1 = 1.0 - ADAM_B1 ** ADAM_STEP
    c2 = 1.0 - ADAM_B2 ** ADAM_STEP

    def body(w_ref, g_ref, m_ref, v_ref, d_ref, mo_ref, vo_ref):
        gv = g_ref[...]
        mn = ADAM_B1 * m_ref[...] + (1.0 - ADAM_B1) * gv
        vn = ADAM_B2 * v_ref[...] + (1.0 - ADAM_B2) * (gv * gv)
        mo_ref[...] = mn
        vo_ref[...] = vn
        d_ref[...] = -ADAM_LR * ((mn / c1) / (jnp.sqrt(vn / c2) + ADAM_EPS) + ADAM_WD * w_ref[...])

    blk = pl.BlockSpec((None, tr, c), lambda i: (0, i, 0))
    return pl.pallas_call(
        body, name=name, grid=(r // tr,), in_specs=[blk] * 4, out_specs=[blk] * 3,
        out_shape=[jax.ShapeDtypeStruct((1, r, c), F32)] * 3,
        compiler_params=_cparams(("parallel",)),
    )(w, g, m, v)


def _sum_slots(land, *, name):
    ns, r, c = land.shape
    tr = _tile(r, 64) if r % SUBLANES == 0 else r

    def body(l_ref, o_ref):
        acc = l_ref[0].astype(F32)
        for s in range(1, ns):
            acc = acc + l_ref[s].astype(F32)
        o_ref[...] = acc

    return pl.pallas_call(
        body, name=name, grid=(r // tr,),
        in_specs=[pl.BlockSpec((ns, tr, c), lambda i: (0, i, 0))],
        out_specs=pl.BlockSpec((tr, c), lambda i: (i, 0)),
        out_shape=jax.ShapeDtypeStruct((r, c), F32),
        compiler_params=_cparams(("parallel",)),
    )(land)


ANY = pl.BlockSpec(memory_space=pl.ANY)


def _flip(v, bit):
    return 1 - v if bit else v


def _gather_chips(shards, *, name):
    n = len(shards)
    rels = ((1, 0), (0, 1), (1, 1))

    def body(*refs):
        ins, outs = refs[:n], refs[n:2 * n]
        send, recv, loc = refs[2 * n:]
        x, y, c = lax.axis_index("x"), lax.axis_index("y"), lax.axis_index("c")
        me = 2 * x + y
        copies = []
        for k in range(n):
            cp = pltpu.make_async_copy(ins[k], outs[k].at[me], loc.at[k])
            cp.start()
            copies.append(cp)
        for r, (rx, ry) in enumerate(rels):
            for k in range(n):
                cp = pltpu.make_async_remote_copy(
                    src_ref=ins[k], dst_ref=outs[k].at[me], send_sem=send.at[r * n + k], recv_sem=recv.at[r * n + k],
                    device_id=(_flip(x, rx), _flip(y, ry), c), device_id_type=MESH)
                cp.start()
                copies.append(cp)
        for cp in copies:
            cp.wait()

    return pl.pallas_call(
        body, name=name, in_specs=[ANY] * n, out_specs=[ANY] * n,
        out_shape=[jax.ShapeDtypeStruct((N_CHIPS,) + s.shape, s.dtype) for s in shards],
        scratch_shapes=[pltpu.SemaphoreType.DMA((3 * n,)), pltpu.SemaphoreType.DMA((3 * n,)),
                        pltpu.SemaphoreType.DMA((n,))],
        compiler_params=pltpu.CompilerParams(has_side_effects=True),
    )(*shards)


_RELS7 = tuple((r >> 2 & 1, r >> 1 & 1, r & 1) for r in range(1, N_DEV))


def _scatter_pieces(grads, *, name):
    n = len(grads)

    def body(*refs):
        ins, outs = refs[:n], refs[n:2 * n]
        send, recv, loc = refs[2 * n:]
        x, y, c = lax.axis_index("x"), lax.axis_index("y"), lax.axis_index("c")
        me = 4 * x + 2 * y + c
        copies = []

        def piece(k, px, py, pc):
            half = ins[k].shape[1] // 2
            return ins[k].at[2 * px + py, pl.ds(pc * half, half), :]

        for k in range(n):
            cp = pltpu.make_async_copy(piece(k, x, y, c), outs[k].at[me], loc.at[k])
            cp.start()
            copies.append(cp)
        for r, (rx, ry, rc) in enumerate(_RELS7):
            tx, ty, tc = _flip(x, rx), _flip(y, ry), _flip(c, rc)
            for k in range(n):
                cp = pltpu.make_async_remote_copy(
                    src_ref=piece(k, tx, ty, tc), dst_ref=outs[k].at[me], send_sem=send.at[r * n + k],
                    recv_sem=recv.at[r * n + k], device_id=(tx, ty, tc), device_id_type=MESH)
                cp.start()
                copies.append(cp)
        for cp in copies:
            cp.wait()

    return pl.pallas_call(
        body, name=name, in_specs=[ANY] * n, out_specs=[ANY] * n,
        out_shape=[jax.ShapeDtypeStruct((N_DEV, g.shape[1] // 2, g.shape[2]), g.dtype) for g in grads],
        scratch_shapes=[pltpu.SemaphoreType.DMA((7 * n,)), pltpu.SemaphoreType.DMA((7 * n,)),
                        pltpu.SemaphoreType.DMA((n,))],
        compiler_params=pltpu.CompilerParams(has_side_effects=True),
    )(*grads)


def _join_cores(halves, *, name):
    n = len(halves)

    def body(*refs):
        ins, outs = refs[:n], refs[n:2 * n]
        send, recv, loc = refs[2 * n:]
        x, y, c = lax.axis_index("x"), lax.axis_index("y"), lax.axis_index("c")
        copies = []
        for k in range(n):
            half = ins[k].shape[0]
            mine = outs[k].at[0, pl.ds(c * half, half), :]
            cp = pltpu.make_async_copy(ins[k], mine, loc.at[k])
            cp.start()
            copies.append(cp)
            cp = pltpu.make_async_remote_copy(
                src_ref=ins[k], dst_ref=mine, send_sem=send.at[k], recv_sem=recv.at[k],
                device_id=(x, y, 1 - c), device_id_type=MESH)
            cp.start()
            copies.append(cp)
        for cp in copies:
            cp.wait()

    in_vmem = pl.BlockSpec(memory_space=pltpu.VMEM)
    return pl.pallas_call(
        body, name=name, in_specs=[in_vmem] * n, out_specs=[in_vmem] * n,
        out_shape=[jax.ShapeDtypeStruct((1, 2 * h.shape[0], h.shape[1]), h.dtype) for h in halves],
        scratch_shapes=[pltpu.SemaphoreType.DMA((n,)), pltpu.SemaphoreType.DMA((n,)), pltpu.SemaphoreType.DMA((n,))],
        compiler_params=pltpu.CompilerParams(has_side_effects=True, vmem_limit_bytes=VMEM_LIMIT),
    )(*halves)


def _allreduce_small(buf, *, name):
    r, n = buf.shape
    half = r // 2
    rels = ((1, 0), (0, 1), (1, 1))

    def body(in_ref, out_ref, sib_ref, chips_ref, send, recv):
        x, y, c = lax.axis_index("x"), lax.axis_index("y"), lax.axis_index("c")
        sibling = (x, y, 1 - c)
        chip = 2 * x + y
        rows = pl.ds(pl.multiple_of(c * half, SUBLANES), half)

        swap = pltpu.make_async_remote_copy(src_ref=in_ref, dst_ref=sib_ref, send_sem=send.at[0], recv_sem=recv.at[0],
                                            device_id=sibling, device_id_type=MESH)
        swap.start()
        swap.wait()
        chips_ref[chip] = in_ref[rows, :] + sib_ref[rows, :]

        sends = []
        for k, (rx, ry) in enumerate(rels):
            cp = pltpu.make_async_remote_copy(
                src_ref=chips_ref.at[chip], dst_ref=chips_ref.at[chip], send_sem=send.at[1 + k],
                recv_sem=recv.at[1 + k], device_id=(_flip(x, rx), _flip(y, ry), c), device_id_type=MESH)
            cp.start()
            sends.append(cp)
        for cp in sends:
            cp.wait()
        total = chips_ref[0]
        for s in range(1, N_CHIPS):
            total = total + chips_ref[s]
        out_ref[rows, :] = total

        back = pltpu.make_async_remote_copy(src_ref=out_ref.at[rows, :], dst_ref=out_ref.at[rows, :],
                                            send_sem=send.at[4], recv_sem=recv.at[4],
                                            device_id=sibling, device_id_type=MESH)
        back.start()
        back.wait()

    vmem = pl.BlockSpec(memory_space=pltpu.VMEM)
    return pl.pallas_call(
        body, name=name, in_specs=[vmem], out_specs=vmem,
        out_shape=jax.ShapeDtypeStruct((r, n), F32),
        scratch_shapes=[pltpu.VMEM((r, n), F32), pltpu.VMEM((N_CHIPS, half, n), F32),
                        pltpu.SemaphoreType.DMA((5,)), pltpu.SemaphoreType.DMA((5,))],
        compiler_params=pltpu.CompilerParams(has_side_effects=True, vmem_limit_bytes=VMEM_LIMIT),
    )(buf)


def _pack(arrs):
    flat = []
    for a in arrs:
        v = a.reshape(-1)
        pad = (-v.shape[0]) % LANES
        if pad:
            v = jnp.pad(v, (0, pad))
        flat.append(v)
    v = jnp.concatenate(flat)
    pad = (-v.shape[0]) % (LANES * SUBLANES)
    if pad:
        v = jnp.pad(v, (0, pad))
    return v.reshape(-1, LANES)


def _unpack(buf, shapes):
    v = buf.reshape(-1)
    out, off = [], 0
    for s in shapes:
        n = math.prod(s)
        out.append(v[off:off + n].reshape(s))
        off += n + (-n) % LANES
    return out


def kernel(x, norm_g, final_g, lru_w_in, lru_conv_w, lru_conv_b, lru_wa, lru_ba, lru_wx, lru_bx, lru_a_param, lru_w_out, fox_w_in, fox_b_f, fox_w_out, loss_target, m_norm_g, m_final_g, m_lru_w_in, m_lru_conv_w, m_lru_conv_b, m_lru_wa, m_lru_ba, m_lru_wx, m_lru_bx, m_lru_a_param, m_lru_w_out, m_fox_w_in, m_fox_b_f, m_fox_w_out, v_norm_g, v_final_g, v_lru_w_in, v_lru_conv_w, v_lru_conv_b, v_lru_wa, v_lru_ba, v_lru_wx, v_lru_bx, v_lru_a_param, v_lru_w_out, v_fox_w_in, v_fox_b_f, v_fox_w_out):
    t, d = x.shape[1], x.shape[2]
    w = lru_wa.shape[1] * LRU_BLOCK_W
    f = FOX_HEADS * FOX_HEAD_DIM
    npair = f // LANES
    x0 = x.reshape(t, d)
    tgt = loss_target.reshape(t, d)
    chip = 2 * lax.axis_index("x") + lax.axis_index("y")

    g_lwi, g_lwo, g_fwi, g_fwo, g_cw = _gather_chips(
        [lru_w_in[0].astype(BF16), lru_w_out[0].astype(BF16), fox_w_in[0].astype(BF16), fox_w_out[0].astype(BF16),
         lru_conv_w[0]], name="gather_weights")
    cg = w // 2
    lwi = jnp.concatenate([g_lwi[0], g_lwi[2], g_lwi[1], g_lwi[3]], axis=1)
    lwo = g_lwo.reshape(w, d)
    fwi = jnp.concatenate([g_fwi[s] for s in range(N_CHIPS)], axis=1)
    w_qkv, w_g2 = fwi[:, :3 * f], fwi[:, 3 * f:4 * f]
    w_f = jnp.pad(fwi[:, 4 * f:], ((0, 0), (0, LANES - FOX_HEADS)))
    fwo = g_fwo.reshape(f, d)
    conv_w = jnp.concatenate([g_cw[s] for s in range(N_CHIPS)], axis=1)
    conv_b, ba, bx, a_param = lru_conv_b, lru_ba, lru_bx, lru_a_param
    wa, wx = lru_wa[0], lru_wx[0]
    b_f = jnp.pad(fox_b_f, ((0, 0), (0, LANES - FOX_HEADS)))

    h0 = _rmsnorm(x0, norm_g[0], name="norm0")
    u = _matmul(h0, lwi, name="lru_in")
    y1, hs = _lru_fwd(u, conv_w, conv_b, wa, ba, wx, bx, a_param, cg=cg, name="lru_fwd")
    x1 = _matmul(y1, lwo, add=x0, name="lru_out")
    h1 = _rmsnorm(x1, norm_g[1], name="norm1")
    qkv = _matmul(h1, w_qkv, out_dtype=BF16, name="fox_qkv")
    gate2 = _matmul(h1, w_g2, name="fox_gate")
    flog = _matmul(h1, w_f, name="fox_f")
    cum = _fgate_fwd(flog, b_f, name="fgate_fwd")
    cum16 = cum[:, :FOX_HEADS]
    ckt = cum16.T.reshape(npair, 2, t)
    cke = jnp.repeat(cum16, FOX_HEAD_DIM, axis=1)
    o, y2, lse = _attn_fwd(qkv, ckt, gate2, name="attn_fwd")
    x2 = _matmul(y2, fwo, add=x1, name="fox_out")
    lsum, dx2, dgf = _final_loss(x2, tgt, final_g, name="final_loss")
    loss = lax.psum(0.5 * jnp.sum(lsum) / d, ("x", "y", "c"))

    d_fwo = _matmul(y2, dx2, ta=True, out_dtype=BF16, name="d_fox_w_out")
    dy2 = _matmul(dx2, fwo, tb=True, name="d_y2")
    do, dgate2, dl = _fox_post_bwd(dy2, o, gate2, name="fox_post_bwd")
    lt = lse
    dt = dl[:, :FOX_HEADS].T.reshape(npair, 2, t)
    dq, dk, dv, dck, dcq = _attn_bwd(qkv, do, lt, dt, cke, name="attn_bwd")
    dcum = jnp.pad((dck + dcq).reshape(FOX_HEADS, t).T, ((0, 0), (0, LANES - FOX_HEADS)))
    dflog, db_f = _fgate_bwd(dcum, flog, b_f, name="fgate_bwd")
    du2 = jnp.concatenate([dq.astype(BF16), dk, dv, dgate2], axis=1)
    dflog_b = dflog.astype(BF16)
    dh1 = _matmul(du2, fwi[:, :4 * f], tb=True, name="d_h1_a")
    dh1 = _matmul(dflog_b, w_f, tb=True, add=dh1, name="d_h1_b")
    d_fwi_a = _matmul(h1, du2, ta=True, out_dtype=BF16, name="d_fox_w_in_a")
    d_fwi_b = _matmul(h1, dflog_b, ta=True, out_dtype=BF16, name="d_fox_w_in_b")
    d_fwi = jnp.concatenate([d_fwi_a, d_fwi_b[:, :FOX_HEADS]], axis=1)
    dx1, dg1 = _rmsnorm_bwd(dh1, x1, norm_g[1], dx2, name="norm1_bwd")

    d_lwo = _matmul(y1, dx1, ta=True, out_dtype=BF16, name="d_lru_w_out")
    dy1 = _matmul(dx1, lwo, tb=True, name="d_y1")
    dxb, dgate, d_cw, d_cb, d_wa, d_ba, d_wx, d_bx, d_ap = _lru_bwd(
        u, hs, dy1, conv_w, conv_b, wa, ba, wx, bx, a_param, cg=cg, name="lru_bwd")
    du = jnp.concatenate([dxb[:, :cg], dgate[:, :cg], dxb[:, cg:], dgate[:, cg:]], axis=1)
    dh0 = _matmul(du, lwi, tb=True, name="d_h0")
    d_lwi_p = _matmul(h0, du, ta=True, out_dtype=BF16, name="d_lru_w_in")
    dx0, dg0 = _rmsnorm_bwd(dh0, x0, norm_g[0], dx1, name="norm0_bwd")

    csz = cg
    g_lwi4 = jnp.stack([d_lwi_p[:, 0:csz], d_lwi_p[:, 2 * csz:3 * csz], d_lwi_p[:, csz:2 * csz],
                        d_lwi_p[:, 3 * csz:]])
    n_fwi = fox_w_in.shape[2]
    g_fwi4 = jnp.stack([d_fwi[:, s * n_fwi:(s + 1) * n_fwi] for s in range(N_CHIPS)])
    g_lwo4 = d_lwo.reshape(N_CHIPS, w // N_CHIPS, d)
    g_fwo4 = d_fwo.reshape(N_CHIPS, f // N_CHIPS, d)
    lands = _scatter_pieces([g_lwi4, g_lwo4, g_fwi4, g_fwo4], name="scatter_grads")
    halves = [_sum_slots(l, name="sum_" + nm) for l, nm in zip(lands, ("lru_w_in", "lru_w_out", "fox_w_in", "fox_w_out"))]
    big_g = _join_cores(halves, name="join_cores")

    small_g = [jnp.concatenate([dg0, dg1], axis=0), dgf.reshape(d), d_cw, d_cb, d_wa, d_ba, d_wx, d_bx, d_ap,
               db_f[:, :FOX_HEADS]]
    gsum = _allreduce_small(_pack(small_g), name="allreduce_small")
    zc = jnp.zeros((CONV_WIDTH, w), F32)
    pk_w = _pack([norm_g, final_g, zc, lru_conv_b, lru_wa, lru_ba, lru_wx, lru_bx, lru_a_param, fox_b_f])
    pk_m = _pack([m_norm_g, m_final_g, zc, m_lru_conv_b, m_lru_wa, m_lru_ba, m_lru_wx, m_lru_bx, m_lru_a_param,
                  m_fox_b_f])
    pk_v = _pack([v_norm_g, v_final_g, zc + 1.0, v_lru_conv_b, v_lru_wa, v_lru_ba, v_lru_wx, v_lru_bx,
                  v_lru_a_param, v_fox_b_f])
    s_delta, s_m, s_v = _adamw(pk_w[None], gsum[None], pk_m[None], pk_v[None], name="adamw_small")
    out_shapes = [norm_g.shape, final_g.shape, (CONV_WIDTH, w), lru_conv_b.shape, lru_wa.shape, lru_ba.shape,
                  lru_wx.shape, lru_bx.shape, lru_a_param.shape, fox_b_f.shape]
    sg = _unpack(gsum, out_shapes)
    sd = _unpack(s_delta, out_shapes)
    sm = _unpack(s_m, out_shapes)
    sv = _unpack(s_v, out_shapes)

    ncw = lru_conv_w.shape[2]
    g_cw_loc = lax.dynamic_slice_in_dim(sg[2], chip * ncw, ncw, axis=1)
    g_cw_loc = g_cw_loc[None]
    cw_d, cw_m, cw_v = _adamw(lru_conv_w, g_cw_loc, m_lru_conv_w, v_lru_conv_w, name="adamw_conv_w")

    big = []
    for nm, wt, g, mm, vv in (("lru_w_in", lru_w_in, big_g[0], m_lru_w_in, v_lru_w_in),
                              ("lru_w_out", lru_w_out, big_g[1], m_lru_w_out, v_lru_w_out),
                              ("fox_w_in", fox_w_in, big_g[2], m_fox_w_in, v_fox_w_in),
                              ("fox_w_out", fox_w_out, big_g[3], m_fox_w_out, v_fox_w_out)):
        big.append((g,) + tuple(_adamw(wt, g, mm, vv, name="adamw_" + nm)))

    def assemble(idx):
        small = (sg, sd, sm, sv)[idx]
        cw = (g_cw_loc, cw_d, cw_m, cw_v)[idx]
        return [small[0], small[1], big[0][idx], cw, small[3], small[4], small[5], small[6], small[7], small[8],
                big[1][idx], big[2][idx], small[9], big[3][idx]]

    grad_x = dx0.reshape(1, t, d)
    return (loss, grad_x, *assemble(0), *assemble(1), *assemble(2), *assemble(3))
```

```python
import functools
import math

import jax
import jax.numpy as jnp
from jax import lax
from jax.experimental import pallas as pl
from jax.experimental.pallas import tpu as pltpu

F32 = jnp.float32
BF16 = jnp.bfloat16

EPS = 1e-6
LRU_C = 8.0
LRU_BLOCK_W = 128
CONV_WIDTH = 4
FOX_HEADS = 16
FOX_HEAD_DIM = 64
NEG_INF = -1e30
ADAM_LR = 0.001
ADAM_B1 = 0.9
ADAM_B2 = 0.999
ADAM_EPS = 1e-08
ADAM_WD = 0.01
ADAM_STEP = 10

LANES = 128
SUBLANES = 8
VMEM_LIMIT = 56 * 1024 * 1024
N_CHIPS = 4
N_DEV = 8
MESH = pl.DeviceIdType.MESH
HIGHEST = lax.Precision.HIGHEST


def _tile(n, pref):
    t = min(n, pref)
    while n % t:
        t //= 2
    return t


def _cparams(dims=None):
    return pltpu.CompilerParams(dimension_semantics=dims, vmem_limit_bytes=VMEM_LIMIT)


def _sigmoid(x):
    return jax.nn.sigmoid(x)


def _log1p(x):
    u = 1.0 + x
    return jnp.where(u == 1.0, x, jnp.log(u) * x / (u - 1.0))


def _softplus(x):
    return jnp.maximum(x, 0.0) + _log1p(jnp.exp(-jnp.abs(x)))


MM_TILE = 1024
MM_FULL_K = 1536


def _matmul(a, b, *, name, ta=False, tb=False, out_dtype=F32, add=None, tm=MM_TILE, tn=MM_TILE, tk=None):
    if ta:
        kdim, m = a.shape
    else:
        m, kdim = a.shape
    if tb:
        n, kb = b.shape
    else:
        kb, n = b.shape
    assert kdim == kb, (a.shape, b.shape, ta, tb)
    if tk is None:
        tk = kdim if kdim <= MM_FULL_K else MM_TILE
    tm, tn, tk = _tile(m, tm), _tile(n, tn), _tile(kdim, tk)
    nk = kdim // tk
    dn = (((0 if ta else 1,), (1 if tb else 0,)), ((), ()))
    has_add = add is not None

    def body(*refs):
        if has_add:
            a_ref, b_ref, add_ref, o_ref = refs[:4]
        else:
            a_ref, b_ref, o_ref = refs[:3]
        part = lax.dot_general(a_ref[...].astype(BF16), b_ref[...].astype(BF16), dn, preferred_element_type=F32)

        def finish(r):
            if has_add:
                r = r + add_ref[...].astype(F32)
            o_ref[...] = r.astype(o_ref.dtype)

        if nk == 1:
            finish(part)
            return
        acc_ref = refs[-1]
        k = pl.program_id(2)

        @pl.when(k == 0)
        def _():
            acc_ref[...] = part

        @pl.when(k > 0)
        def _():
            acc_ref[...] += part

        @pl.when(k == nk - 1)
        def _():
            finish(acc_ref[...])

    a_spec = pl.BlockSpec((tk, tm), lambda i, j, k: (k, i)) if ta else pl.BlockSpec((tm, tk), lambda i, j, k: (i, k))
    b_spec = pl.BlockSpec((tn, tk), lambda i, j, k: (j, k)) if tb else pl.BlockSpec((tk, tn), lambda i, j, k: (k, j))
    o_spec = pl.BlockSpec((tm, tn), lambda i, j, k: (i, j))
    in_specs = [a_spec, b_spec] + ([o_spec] if has_add else [])
    args = (a, b) + ((add,) if has_add else ())
    return pl.pallas_call(
        body, name=name, grid=(m // tm, n // tn, nk), in_specs=in_specs, out_specs=o_spec,
        out_shape=jax.ShapeDtypeStruct((m, n), out_dtype),
        scratch_shapes=[pltpu.VMEM((tm, tn), F32)] if nk > 1 else [],
        compiler_params=_cparams(("parallel", "parallel", "arbitrary")),
    )(*args)


def _rmsnorm(x, g, *, name):
    t, d = x.shape
    tt = _tile(t, 512)

    def body(x_ref, g_ref, o_ref):
        xf = x_ref[...]
        rstd = lax.rsqrt(jnp.mean(xf * xf, axis=-1, keepdims=True) + EPS)
        o_ref[...] = (xf * rstd * g_ref[...]).astype(o_ref.dtype)

    return pl.pallas_call(
        body, name=name, grid=(t // tt,),
        in_specs=[pl.BlockSpec((tt, d), lambda i: (i, 0)), pl.BlockSpec((1, d), lambda i: (0, 0))],
        out_specs=pl.BlockSpec((tt, d), lambda i: (i, 0)),
        out_shape=jax.ShapeDtypeStruct((t, d), BF16),
        compiler_params=_cparams(("parallel",)),
    )(x, g.reshape(1, d))


def _rmsnorm_bwd(dh, x, g, dres, *, name):
    t, d = x.shape
    tt = _tile(t, 512)

    def body(dh_ref, x_ref, g_ref, dres_ref, dx_ref, dg_ref):
        i = pl.program_id(0)

        @pl.when(i == 0)
        def _():
            dg_ref[...] = jnp.zeros_like(dg_ref)

        xf = x_ref[...]
        rstd = lax.rsqrt(jnp.mean(xf * xf, axis=-1, keepdims=True) + EPS)
        xhat = xf * rstd
        dhf = dh_ref[...].astype(F32)
        dxhat = dhf * g_ref[...]
        mt = jnp.mean(dxhat * xhat, axis=-1, keepdims=True)
        dx_ref[...] = dres_ref[...] + rstd * (dxhat - xhat * mt)
        dg_ref[...] += jnp.sum(dhf * xhat, axis=0, keepdims=True)

    blk = pl.BlockSpec((tt, d), lambda i: (i, 0))
    vec = pl.BlockSpec((1, d), lambda i: (0, 0))
    return pl.pallas_call(
        body, name=name, grid=(t // tt,),
        in_specs=[blk, blk, vec, blk], out_specs=[blk, vec],
        out_shape=[jax.ShapeDtypeStruct((t, d), F32), jax.ShapeDtypeStruct((1, d), F32)],
        compiler_params=_cparams(("arbitrary",)),
    )(dh, x, g.reshape(1, d), dres)


def _final_loss(x2, tgt, g, *, name):
    t, d = x2.shape
    tt = _tile(t, 512)

    def body(x_ref, t_ref, g_ref, l_ref, dx_ref, dg_ref):
        i = pl.program_id(0)

        @pl.when(i == 0)
        def _():
            dg_ref[...] = jnp.zeros_like(dg_ref)
            l_ref[...] = jnp.zeros_like(l_ref)

        xf = x_ref[...]
        gg = g_ref[...]
        rstd = lax.rsqrt(jnp.mean(xf * xf, axis=-1, keepdims=True) + EPS)
        xhat = xf * rstd
        err = xhat * gg - t_ref[...]
        l_ref[...] += jnp.sum(err * err, axis=0, keepdims=True)
        dy = err * (1.0 / d)
        dxhat = dy * gg
        mt = jnp.mean(dxhat * xhat, axis=-1, keepdims=True)
        dx_ref[...] = rstd * (dxhat - xhat * mt)
        dg_ref[...] += jnp.sum(dy * xhat, axis=0, keepdims=True)

    blk = pl.BlockSpec((tt, d), lambda i: (i, 0))
    vec = pl.BlockSpec((1, d), lambda i: (0, 0))
    return pl.pallas_call(
        body, name=name, grid=(t // tt,),
        in_specs=[blk, blk, vec], out_specs=[vec, blk, vec],
        out_shape=[jax.ShapeDtypeStruct((1, d), F32), jax.ShapeDtypeStruct((t, d), F32),
                   jax.ShapeDtypeStruct((1, d), F32)],
        compiler_params=_cparams(("arbitrary",)),
    )(x2, tgt, g.reshape(1, d))


def _shift_down(prev8, cur, s):
    ext = jnp.concatenate([prev8, cur], axis=0)
    if s == 0:
        return cur
    return pltpu.roll(ext, s, 0)[SUBLANES:, :]


def _shift_up(cur, next8, s):
    if s == 0:
        return cur
    n = cur.shape[0]
    ext = jnp.concatenate([cur, next8], axis=0)
    return pltpu.roll(ext, n + SUBLANES - s, 0)[:n, :]


def _lru_gates(xc, wa, ba, wx, bx, sp):
    xcb = xc.astype(BF16)
    r = _sigmoid(jnp.dot(xcb, wa, preferred_element_type=F32) + ba)
    ig = _sigmoid(jnp.dot(xcb, wx, preferred_element_type=F32) + bx)
    log_a = -LRU_C * r * sp
    a = jnp.exp(log_a)
    mult = jnp.sqrt(-jnp.tanh(log_a) * (a * a + 1.0))
    return r, ig, a, mult


def _lru_specs(tt, cg, n_groups, nt, reverse):
    ncol = cg // LANES
    if reverse:
        ti = lambda i: nt - 1 - i
    else:
        ti = lambda i: i
    hb = tt // SUBLANES
    cur = lambda col: pl.BlockSpec((tt, cg), lambda g, i: (ti(i), 2 * g + col))
    prev = lambda col: pl.BlockSpec((SUBLANES, cg), lambda g, i: (jnp.maximum(ti(i) * hb - 1, 0), 2 * g + col))
    chan = lambda rows: pl.BlockSpec((rows, cg), lambda g, i: (0, g))
    wblk = pl.BlockSpec((ncol, LRU_BLOCK_W, LRU_BLOCK_W), lambda g, i: (g, 0, 0))
    plain = pl.BlockSpec((tt, cg), lambda g, i: (ti(i), g))
    plain_prev = pl.BlockSpec((SUBLANES, cg), lambda g, i: (jnp.maximum(ti(i) * hb - 1, 0), g))
    return cur, prev, chan, wblk, plain, plain_prev


def _lru_fwd(u, conv_w, conv_b, wa, ba, wx, bx, a_param, *, cg, name):
    t, w2 = u.shape
    w = w2 // 2
    n_groups = w // cg
    ncol = cg // LANES
    tt = _tile(t, 256)
    nt = t // tt
    cur, prev, chan, wblk, plain, _ = _lru_specs(tt, cg, n_groups, nt, False)

    def body(xb_ref, xp_ref, gate_ref, cw_ref, cb_ref, wa_ref, ba_ref, wx_ref, bx_ref, ap_ref,
             y_ref, hs_ref, h_ref, a_s, b_s):
        i = pl.program_id(1)

        @pl.when(i == 0)
        def _():
            h_ref[...] = jnp.zeros_like(h_ref)

        keep = (i > 0).astype(F32)
        for n in range(ncol):
            sl = slice(n * LANES, (n + 1) * LANES)
            xb = xb_ref[:, sl]
            xp = xp_ref[:, sl] * keep
            xc = cb_ref[:, sl] + cw_ref[3:4, sl] * xb
            for s in range(1, CONV_WIDTH):
                xc = xc + cw_ref[3 - s:4 - s, sl] * _shift_down(xp, xb, s)
            sp = _softplus(-ap_ref[:, sl])
            _, ig, a, mult = _lru_gates(xc, wa_ref[n].astype(BF16), ba_ref[:, sl],
                                        wx_ref[n].astype(BF16), bx_ref[:, sl], sp)
            a_s[:, sl] = a
            b_s[:, sl] = mult * (ig * xc)

        def step(tr, h):
            h = a_s[pl.ds(tr, 1), :] * h + b_s[pl.ds(tr, 1), :]
            hs_ref[pl.ds(tr, 1), :] = h
            return h

        h = lax.fori_loop(0, tt, step, h_ref[0:1, :], unroll=8)
        h_ref[0:1, :] = h
        gate = gate_ref[...]
        y_ref[...] = (hs_ref[...] * (gate * _sigmoid(gate))).astype(y_ref.dtype)

    return pl.pallas_call(
        body, name=name, grid=(n_groups, nt),
        in_specs=[cur(0), prev(0), cur(1), chan(CONV_WIDTH), chan(1), wblk, chan(1), wblk, chan(1), chan(1)],
        out_specs=[plain, plain],
        out_shape=[jax.ShapeDtypeStruct((t, w), BF16), jax.ShapeDtypeStruct((t, w), F32)],
        scratch_shapes=[pltpu.VMEM((SUBLANES, cg), F32), pltpu.VMEM((tt, cg), F32), pltpu.VMEM((tt, cg), F32)],
        compiler_params=_cparams(("parallel", "arbitrary")),
    )(u, u, u, conv_w, conv_b, wa, ba, wx, bx, a_param)


def _lru_bwd(u, hs, dy, conv_w, conv_b, wa, ba, wx, bx, a_param, *, cg, name):
    t, w2 = u.shape
    w = w2 // 2
    n_groups = w // cg
    ncol = cg // LANES
    tt = _tile(t, 256)
    nt = t // tt
    cur, prev, chan, wblk, plain, plain_prev = _lru_specs(tt, cg, n_groups, nt, True)
    tn_dims = (((0,), (0,)), ((), ()))
    nt_dims = (((1,), (1,)), ((), ()))

    def body(xb_ref, xp_ref, gate_ref, hs_ref, hp_ref, dy_ref, cw_ref, cb_ref, wa_ref, ba_ref, wx_ref, bx_ref,
             ap_ref, dxb_ref, dgate_ref, dcw_ref, dcb_ref, dwa_ref, dba_ref, dwx_ref, dbx_ref, dsp_ref,
             c_ref, nx_ref, a_s, dhs_s, lam_s):
        i = pl.program_id(1)
        first_time_block = i == nt - 1

        @pl.when(i == 0)
        def _():
            c_ref[...] = jnp.zeros_like(c_ref)
            nx_ref[...] = jnp.zeros_like(nx_ref)
            for r in (dcw_ref, dcb_ref, dwa_ref, dba_ref, dwx_ref, dbx_ref, dsp_ref):
                r[...] = jnp.zeros_like(r)

        keep = jnp.where(first_time_block, 0.0, 1.0).astype(F32)
        gate = gate_ref[...]
        sg = _sigmoid(gate)
        dyv = dy_ref[...]
        hsv = hs_ref[...]
        dhs_s[...] = dyv * (gate * sg)
        dgate_ref[...] = (dyv * hsv * (sg * (1.0 + gate * (1.0 - sg)))).astype(dgate_ref.dtype)

        saved = []
        for n in range(ncol):
            sl = slice(n * LANES, (n + 1) * LANES)
            xb = xb_ref[:, sl]
            xp = xp_ref[:, sl] * keep
            shifted = [xb] + [_shift_down(xp, xb, s) for s in range(1, CONV_WIDTH)]
            xc = cb_ref[:, sl] + cw_ref[3:4, sl] * xb
            for s in range(1, CONV_WIDTH):
                xc = xc + cw_ref[3 - s:4 - s, sl] * shifted[s]
            sp = _softplus(-ap_ref[:, sl])
            wab = wa_ref[n].astype(BF16)
            wxb = wx_ref[n].astype(BF16)
            r, ig, a, mult = _lru_gates(xc, wab, ba_ref[:, sl], wxb, bx_ref[:, sl], sp)
            a_s[:, sl] = a
            saved.append((sl, shifted, xc, sp, wab, wxb, r, ig, a, mult))

        def step(k, c):
            tr = tt - 1 - k
            lam = dhs_s[pl.ds(tr, 1), :] + c
            lam_s[pl.ds(tr, 1), :] = lam
            return a_s[pl.ds(tr, 1), :] * lam

        c_ref[0:1, :] = lax.fori_loop(0, tt, step, c_ref[0:1, :], unroll=8)

        for n in range(ncol):
            sl, shifted, xc, sp, wab, wxb, r, ig, a, mult = saved[n]
            lam = lam_s[:, sl]
            hprev = _shift_down(hp_ref[:, sl] * keep, hs_ref[:, sl], 1)
            da = lam * hprev
            dmult = lam * (ig * xc)
            dlog_a = da * a - dmult * (a * a / mult)
            di = lam * (mult * xc)
            dxc = lam * (mult * ig)
            dr = dlog_a * (-LRU_C * sp)
            dsp_ref[:, sl] += jnp.sum(dlog_a * (-LRU_C * r), axis=0, keepdims=True)
            dza = dr * (r * (1.0 - r))
            dzx = di * (ig * (1.0 - ig))
            dba_ref[:, sl] += jnp.sum(dza, axis=0, keepdims=True)
            dbx_ref[:, sl] += jnp.sum(dzx, axis=0, keepdims=True)
            xcb = xc.astype(BF16)
            dzab = dza.astype(BF16)
            dzxb = dzx.astype(BF16)
            dwa_ref[n] += lax.dot_general(xcb, dzab, tn_dims, preferred_element_type=F32)
            dwx_ref[n] += lax.dot_general(xcb, dzxb, tn_dims, preferred_element_type=F32)
            dxc = dxc + lax.dot_general(dzab, wab, nt_dims, preferred_element_type=F32)
            dxc = dxc + lax.dot_general(dzxb, wxb, nt_dims, preferred_element_type=F32)
            dcb_ref[:, sl] += jnp.sum(dxc, axis=0, keepdims=True)
            for s in range(CONV_WIDTH):
                dcw_ref[3 - s:4 - s, sl] += jnp.sum(dxc * shifted[s], axis=0, keepdims=True)
            nx = nx_ref[:, sl]
            dxb = cw_ref[3:4, sl] * dxc
            for s in range(1, CONV_WIDTH):
                dxb = dxb + cw_ref[3 - s:4 - s, sl] * _shift_up(dxc, nx, s)
            dxb_ref[:, sl] = dxb.astype(dxb_ref.dtype)
            nx_ref[:, sl] = dxc[0:SUBLANES, :]

        @pl.when(first_time_block)
        def _():
            dsp_ref[...] = dsp_ref[...] * (-_sigmoid(-ap_ref[...]))

    dxb_spec = pl.BlockSpec((tt, cg), lambda g, i: (nt - 1 - i, g))
    outs = pl.pallas_call(
        body, name=name, grid=(n_groups, nt),
        in_specs=[cur(0), prev(0), cur(1), plain, plain_prev, plain, chan(CONV_WIDTH), chan(1), wblk, chan(1), wblk,
                  chan(1), chan(1)],
        out_specs=[dxb_spec, dxb_spec, chan(CONV_WIDTH), chan(1), wblk, chan(1), wblk, chan(1), chan(1)],
        out_shape=[jax.ShapeDtypeStruct((t, w), BF16), jax.ShapeDtypeStruct((t, w), BF16),
                   jax.ShapeDtypeStruct(conv_w.shape, F32), jax.ShapeDtypeStruct(conv_b.shape, F32),
                   jax.ShapeDtypeStruct(wa.shape, F32), jax.ShapeDtypeStruct(ba.shape, F32),
                   jax.ShapeDtypeStruct(wx.shape, F32), jax.ShapeDtypeStruct(bx.shape, F32),
                   jax.ShapeDtypeStruct(a_param.shape, F32)],
        scratch_shapes=[pltpu.VMEM((SUBLANES, cg), F32), pltpu.VMEM((SUBLANES, cg), F32),
                        pltpu.VMEM((tt, cg), F32), pltpu.VMEM((tt, cg), F32), pltpu.VMEM((tt, cg), F32)],
        compiler_params=_cparams(("parallel", "arbitrary")),
    )(u, u, u, hs, hs, dy, conv_w, conv_b, wa, ba, wx, bx, a_param)
    return outs


def _fgate_fwd(f, b_f, *, name):
    t, n = f.shape
    tt = _tile(t, 256)

    def body(f_ref, b_ref, cum_ref, carry_ref):
        i = pl.program_id(0)

        @pl.when(i == 0)
        def _():
            carry_ref[...] = jnp.zeros_like(carry_ref)

        z = f_ref[...] + b_ref[...]
        lf = jnp.minimum(z, 0.0) - _log1p(jnp.exp(-jnp.abs(z)))
        row = lax.broadcasted_iota(jnp.int32, (tt, tt), 0)
        col = lax.broadcasted_iota(jnp.int32, (tt, tt), 1)
        tri = (col <= row).astype(F32)
        cum = jnp.dot(tri, lf, precision=HIGHEST, preferred_element_type=F32) + carry_ref[0:1, :]
        cum_ref[...] = cum
        carry_ref[0:1, :] = cum[tt - 1:tt, :]

    return pl.pallas_call(
        body, name=name, grid=(t // tt,),
        in_specs=[pl.BlockSpec((tt, n), lambda i: (i, 0)), pl.BlockSpec((1, n), lambda i: (0, 0))],
        out_specs=pl.BlockSpec((tt, n), lambda i: (i, 0)),
        out_shape=jax.ShapeDtypeStruct((t, n), F32),
        scratch_shapes=[pltpu.VMEM((SUBLANES, n), F32)],
        compiler_params=_cparams(("arbitrary",)),
    )(f, b_f)


def _fgate_bwd(dcum, f, b_f, *, name):
    t, n = f.shape
    tt = _tile(t, 256)
    nt = t // tt

    def body(dc_ref, f_ref, b_ref, df_ref, db_ref, carry_ref):
        i = pl.program_id(0)

        @pl.when(i == 0)
        def _():
            carry_ref[...] = jnp.zeros_like(carry_ref)
            db_ref[...] = jnp.zeros_like(db_ref)

        row = lax.broadcasted_iota(jnp.int32, (tt, tt), 0)
        col = lax.broadcasted_iota(jnp.int32, (tt, tt), 1)
        triu = (col >= row).astype(F32)
        dlf = jnp.dot(triu, dc_ref[...], precision=HIGHEST, preferred_element_type=F32) + carry_ref[0:1, :]
        carry_ref[0:1, :] = dlf[0:1, :]
        z = f_ref[...] + b_ref[...]
        df = dlf * _sigmoid(-z)
        df_ref[...] = df
        db_ref[...] += jnp.sum(df, axis=0, keepdims=True)

    blk = pl.BlockSpec((tt, n), lambda i: (nt - 1 - i, 0))
    vec = pl.BlockSpec((1, n), lambda i: (0, 0))
    return pl.pallas_call(
        body, name=name, grid=(nt,),
        in_specs=[blk, blk, vec], out_specs=[blk, vec],
        out_shape=[jax.ShapeDtypeStruct((t, n), F32), jax.ShapeDtypeStruct((1, n), F32)],
        scratch_shapes=[pltpu.VMEM((SUBLANES, n), F32)],
        compiler_params=_cparams(("arbitrary",)),
    )(dcum, f, b_f)


def _attn_fwd(start, qkv, ckt, gate, *, name):
    t = qkv.shape[0]
    f = gate.shape[1]
    npair = f // LANES
    tq = _tile(t, ATTN_TILE)
    nq = t // tq
    scale = 1.0 / math.sqrt(FOX_HEAD_DIM)
    nt_dims = (((1,), (1,)), ((), ()))

    def body(start_ref, q_ref, k_ref, v_ref, ck_ref, g_ref, o_ref, y_ref, l_ref):
        i = pl.program_id(1)
        first = start_ref[pl.program_id(0), i]
        lane = lax.broadcasted_iota(jnp.int32, (tq, LANES), 1)
        lo = lane < FOX_HEAD_DIM
        q2 = q_ref[...] * scale
        qs = (jnp.where(lo, q2, 0).astype(BF16), jnp.where(lo, 0, q2).astype(BF16))
        row = lax.broadcasted_iota(jnp.int32, (tq, tq), 0)
        col = lax.broadcasted_iota(jnp.int32, (tq, tq), 1)
        causal = col <= row

        def kv_step(j, carry, masked):
            off = pl.multiple_of(j * tq, tq)
            kj = k_ref[pl.ds(off, tq), :]
            vj = v_ref[pl.ds(off, tq), :]
            ck = ck_ref[:, pl.ds(off, tq)]
            new = []
            for h in range(2):
                m, l, acc = carry[h]
                s = lax.dot_general(qs[h], kj, nt_dims, preferred_element_type=F32) - ck[h:h + 1, :]
                if masked:
                    s = jnp.where(causal, s, NEG_INF)
                m_new = jnp.maximum(m, jnp.max(s, axis=-1, keepdims=True))
                alpha = jnp.exp(m - m_new)
                p = jnp.exp(s - m_new)
                l = alpha * l + jnp.sum(p, axis=-1, keepdims=True)
                acc = alpha * acc + jnp.dot(p.astype(BF16), vj, preferred_element_type=F32)
                new.append((m_new, l, acc))
            return tuple(new)

        init = tuple((jnp.full((tq, 1), NEG_INF, F32), jnp.zeros((tq, 1), F32), jnp.zeros((tq, LANES), F32))
                     for _ in range(2))
        carry = lax.fori_loop(first, i, lambda j, c: kv_step(j, c, False), init)
        (m0, l0, a0), (m1, l1, a1) = kv_step(i, carry, True)
        o = jnp.where(lo, a0 / l0, a1 / l1)
        o_ref[...] = o
        gate_v = g_ref[...]
        y_ref[...] = (o * (gate_v * _sigmoid(gate_v))).astype(y_ref.dtype)
        lse_t = jnp.transpose(jnp.where(lo, m0 + jnp.log(l0), m1 + jnp.log(l1)))
        l_ref[0:1, :] = lse_t[0:1, :]
        l_ref[1:2, :] = lse_t[FOX_HEAD_DIM:FOX_HEAD_DIM + 1, :]

    blk = lambda base: pl.BlockSpec((tq, LANES), lambda p, i, s: (i, base + p))
    full = lambda base: pl.BlockSpec((t, LANES), lambda p, i, s: (0, base + p))
    return pl.pallas_call(
        body, name=name,
        grid_spec=pltpu.PrefetchScalarGridSpec(
            num_scalar_prefetch=1, grid=(npair, nq),
            in_specs=[blk(0), full(npair), full(2 * npair), pl.BlockSpec((None, 2, t), lambda p, i, s: (p, 0, 0)),
                      blk(0)],
            out_specs=[blk(0), blk(0), pl.BlockSpec((None, 2, tq), lambda p, i, s: (p, 0, i))]),
        out_shape=[jax.ShapeDtypeStruct((t, f), F32), jax.ShapeDtypeStruct((t, f), BF16),
                   jax.ShapeDtypeStruct((npair, 2, t), F32)],
        compiler_params=_cparams(("parallel", "arbitrary")),
    )(start, qkv, qkv, qkv, ckt, gate)


def _attn_bwd(end, qkv, do, lt, dt, cke, *, name):
    t, f = do.shape
    npair = f // LANES
    tk = _tile(t, ATTN_TILE)
    nk = t // tk
    scale = 1.0 / math.sqrt(FOX_HEAD_DIM)
    nt_dims = (((1,), (1,)), ((), ()))
    tn_dims = (((0,), (0,)), ((), ()))

    def body(end_ref, k_ref, v_ref, q_ref, do_ref, l_ref, d_ref, ck_ref, dq_ref, dk_ref, dv_ref, dck_ref, dcq_ref):
        j = pl.program_id(1)
        last = end_ref[pl.program_id(0), j]

        @pl.when(j == 0)
        def _():
            dq_ref[...] = jnp.zeros_like(dq_ref)
            dcq_ref[...] = jnp.zeros_like(dcq_ref)

        lane = lax.broadcasted_iota(jnp.int32, (tk, LANES), 1)
        lo = lane < FOX_HEAD_DIM
        sel = (lo, jnp.logical_not(lo))
        kj = k_ref[...]
        vj = v_ref[...]
        km = tuple(jnp.where(sel[h], kj, 0).astype(BF16) for h in range(2))
        ckv = ck_ref[...]
        ckh = (ckv[:, 0:1], ckv[:, FOX_HEAD_DIM:FOX_HEAD_DIM + 1])
        row = lax.broadcasted_iota(jnp.int32, (tk, tk), 0)
        col = lax.broadcasted_iota(jnp.int32, (tk, tk), 1)
        causal = row <= col

        def q_step(i, carry, masked):
            dk_acc, dv_acc, dck = carry
            off = pl.multiple_of(i * tk, tk)
            qi = q_ref[pl.ds(off, tk), :]
            doi = do_ref[pl.ds(off, tk), :]
            lrow = l_ref[:, pl.ds(off, tk)]
            drow = d_ref[:, pl.ds(off, tk)]
            dq_add = jnp.zeros((tk, LANES), F32)
            new_dck = []
            for h in range(2):
                qm = jnp.where(sel[h], qi, 0).astype(BF16)
                dom = jnp.where(sel[h], doi, 0).astype(BF16)
                st = lax.dot_general(kj, qm, nt_dims, preferred_element_type=F32) * scale
                st = st - ckh[h] - lrow[h:h + 1, :]
                if masked:
                    st = jnp.where(causal, st, NEG_INF)
                pt = jnp.exp(st)
                dpt = lax.dot_general(vj, dom, nt_dims, preferred_element_type=F32)
                dst = pt * (dpt - drow[h:h + 1, :])
                ptb = pt.astype(BF16)
                dstb = dst.astype(BF16)
                dv_acc = dv_acc + jnp.dot(ptb, dom, preferred_element_type=F32)
                dk_acc = dk_acc + jnp.dot(dstb, qm, preferred_element_type=F32)
                dq_add = dq_add + lax.dot_general(dstb, km[h], tn_dims, preferred_element_type=F32)
                new_dck.append(dck[h] - jnp.sum(dst, axis=-1, keepdims=True))
                dcq_ref[h:h + 1, pl.ds(off, tk)] += jnp.sum(dst, axis=0, keepdims=True)
            dq_ref[pl.ds(off, tk), :] += dq_add * scale
            return dk_acc, dv_acc, tuple(new_dck)

        zero = jnp.zeros((tk, LANES), F32)
        carry = (zero, zero, (jnp.zeros((tk, 1), F32), jnp.zeros((tk, 1), F32)))
        carry = q_step(j, carry, True)
        dk_acc, dv_acc, dck = lax.fori_loop(j + 1, last, lambda i, c: q_step(i, c, False), carry)
        dk_ref[...] = (dk_acc * scale).astype(dk_ref.dtype)
        dv_ref[...] = dv_acc.astype(dv_ref.dtype)
        dck_t = jnp.transpose(jnp.where(lo, dck[0], dck[1]))
        dck_ref[0:1, :] = dck_t[0:1, :]
        dck_ref[1:2, :] = dck_t[FOX_HEAD_DIM:FOX_HEAD_DIM + 1, :]

    blk = lambda base: pl.BlockSpec((tk, LANES), lambda p, j, e: (j, base + p))
    full = lambda base: pl.BlockSpec((t, LANES), lambda p, j, e: (0, base + p))
    rows = pl.BlockSpec((None, 2, t), lambda p, j, e: (p, 0, 0))
    return pl.pallas_call(
        body, name=name,
        grid_spec=pltpu.PrefetchScalarGridSpec(
            num_scalar_prefetch=1, grid=(npair, nk),
            in_specs=[blk(npair), blk(2 * npair), full(0), full(0), rows, rows, blk(0)],
            out_specs=[full(0), blk(0), blk(0), pl.BlockSpec((None, 2, tk), lambda p, j, e: (p, 0, j)), rows]),
        out_shape=[jax.ShapeDtypeStruct((t, f), F32), jax.ShapeDtypeStruct((t, f), BF16),
                   jax.ShapeDtypeStruct((t, f), BF16), jax.ShapeDtypeStruct((npair, 2, t), F32),
                   jax.ShapeDtypeStruct((npair, 2, t), F32)],
        compiler_params=_cparams(("parallel", "arbitrary")),
    )(end, qkv, qkv, qkv, do, lt, dt, cke)


ATTN_TILE = 512
EXP_ZERO = -104.0
BOUND_SLACK = 1.02


def _attn_row_stats(qkv, *, name):
    t = qkv.shape[0]
    f = qkv.shape[1] // 3
    tt = _tile(t, 512)

    def body(q_ref, k_ref, s_ref):
        q = q_ref[...].astype(F32)
        k = k_ref[...].astype(F32)
        chan = lax.broadcasted_iota(jnp.int32, (f, LANES), 0) // FOX_HEAD_DIM
        lane = lax.broadcasted_iota(jnp.int32, (f, LANES), 1)
        acc = jnp.zeros((tt, LANES), F32)
        for off, val in ((0, q * q), (FOX_HEADS, q * k), (2 * FOX_HEADS, k * k)):
            pick = (chan == lane - off).astype(BF16)
            acc = acc + jnp.dot(val.astype(BF16), pick, preferred_element_type=F32)
        s_ref[...] = acc

    return pl.pallas_call(
        body, name=name, grid=(t // tt,),
        in_specs=[pl.BlockSpec((tt, f), lambda i: (i, 0)), pl.BlockSpec((tt, f), lambda i: (i, 1))],
        out_specs=pl.BlockSpec((tt, LANES), lambda i: (i, 0)),
        out_shape=jax.ShapeDtypeStruct((t, LANES), F32),
        compiler_params=_cparams(("parallel",)),
    )(qkv, qkv)


def _attn_skip_tables(stats, cum16, tile):
    t = stats.shape[0]
    nb = t // tile
    scale = 1.0 / math.sqrt(FOX_HEAD_DIM)
    qn = jnp.sqrt(stats[:, :FOX_HEADS]) * scale
    sii = stats[:, FOX_HEADS:2 * FOX_HEADS] * scale - cum16
    kmax = jnp.max(jnp.sqrt(stats[:, 2 * FOX_HEADS:3 * FOX_HEADS]), axis=0, keepdims=True)
    arow = qn * kmax * BOUND_SLACK - sii + 0.5 * BOUND_SLACK
    a_blk = jnp.max(arow.reshape(nb, tile, FOX_HEADS), axis=1)
    c_blk = -cum16.reshape(nb, tile, FOX_HEADS)[:, tile - 1, :]
    dead = (a_blk[:, None, :] + c_blk[None, :, :]) < EXP_ZERO
    start_h = jnp.sum(dead.astype(jnp.int32), axis=1)
    blk = jnp.arange(nb, dtype=jnp.int32)
    start = jnp.minimum(jnp.min(start_h.reshape(nb, FOX_HEADS // 2, 2), axis=2), blk[:, None]).T
    needs = start[:, :, None] <= blk[None, None, :]
    end = jnp.max(jnp.where(needs, blk[None, :, None] + 1, 0), axis=1)
    return start, jnp.maximum(end, blk[None, :] + 1)


def _fox_post_bwd(dy, o, gate, *, name):
    t, f = dy.shape
    tt = _tile(t, 512)

    def body(dy_ref, o_ref, g_ref, do_ref, dg_ref, dl_ref):
        g = g_ref[...]
        sg = _sigmoid(g)
        dyv = dy_ref[...]
        ov = o_ref[...]
        do = dyv * (g * sg)
        do_ref[...] = do.astype(do_ref.dtype)
        dg_ref[...] = (dyv * ov * (sg * (1.0 + g * (1.0 - sg)))).astype(dg_ref.dtype)
        chan = lax.broadcasted_iota(jnp.int32, (f, LANES), 0)
        head = lax.broadcasted_iota(jnp.int32, (f, LANES), 1)
        pick = (chan // FOX_HEAD_DIM == head).astype(F32)
        dl_ref[...] = jnp.dot(do * ov, pick, precision=HIGHEST, preferred_element_type=F32)

    blk = pl.BlockSpec((tt, f), lambda i: (i, 0))
    return pl.pallas_call(
        body, name=name, grid=(t // tt,),
        in_specs=[blk, blk, blk], out_specs=[blk, blk, pl.BlockSpec((tt, LANES), lambda i: (i, 0))],
        out_shape=[jax.ShapeDtypeStruct((t, f), BF16), jax.ShapeDtypeStruct((t, f), BF16),
                   jax.ShapeDtypeStruct((t, LANES), F32)],
        compiler_params=_cparams(("parallel",)),
    )(dy, o, gate)


def _adamw(w, g, m, v, *, name):
    _, r, c = w.shape
    tr = _tile(r, 256) if r % SUBLANES == 0 else r
    c1 = 1.0 - ADAM_B1 ** ADAM_STEP
    c2 = 1.0 - ADAM_B2 ** ADAM_STEP

    def body(w_ref, g_ref, m_ref, v_ref, d_ref, mo_ref, vo_ref):
        gv = g_ref[...]
        mn = ADAM_B1 * m_ref[...] + (1.0 - ADAM_B1) * gv
        vn = ADAM_B2 * v_ref[...] + (1.0 - ADAM_B2) * (gv * gv)
        mo_ref[...] = mn
        vo_ref[...] = vn
        d_ref[...] = -ADAM_LR * ((mn / c1) / (jnp.sqrt(vn / c2) + ADAM_EPS) + ADAM_WD * w_ref[...])

    blk = pl.BlockSpec((None, tr, c), lambda i: (0, i, 0))
    return pl.pallas_call(
        body, name=name, grid=(r // tr,), in_specs=[blk] * 4, out_specs=[blk] * 3,
        out_shape=[jax.ShapeDtypeStruct((1, r, c), F32)] * 3,
        compiler_params=_cparams(("parallel",)),
    )(w, g, m, v)


def _sum_slots(land, *, name):
    ns, r, c = land.shape
    tr = _tile(r, 64) if r % SUBLANES == 0 else r

    def body(l_ref, o_ref):
        acc = l_ref[0].astype(F32)
        for s in range(1, ns):
            acc = acc + l_ref[s].astype(F32)
        o_ref[...] = acc

    return pl.pallas_call(
        body, name=name, grid=(r // tr,),
        in_specs=[pl.BlockSpec((ns, tr, c), lambda i: (0, i, 0))],
        out_specs=pl.BlockSpec((tr, c), lambda i: (i, 0)),
        out_shape=jax.ShapeDtypeStruct((r, c), F32),
        compiler_params=_cparams(("parallel",)),
    )(land)


ANY = pl.BlockSpec(memory_space=pl.ANY)


def _flip(v, bit):
    return 1 - v if bit else v


def _gather_chips(shards, *, name):
    n = len(shards)
    rels = ((1, 0), (0, 1), (1, 1))

    def body(*refs):
        ins, outs = refs[:n], refs[n:2 * n]
        send, recv, loc = refs[2 * n:]
        x, y, c = lax.axis_index("x"), lax.axis_index("y"), lax.axis_index("c")
        me = 2 * x + y
        copies = []
        for k in range(n):
            cp = pltpu.make_async_copy(ins[k], outs[k].at[me], loc.at[k])
            cp.start()
            copies.append(cp)
        for r, (rx, ry) in enumerate(rels):
            for k in range(n):
                cp = pltpu.make_async_remote_copy(
                    src_ref=ins[k], dst_ref=outs[k].at[me], send_sem=send.at[r * n + k], recv_sem=recv.at[r * n + k],
                    device_id=(_flip(x, rx), _flip(y, ry), c), device_id_type=MESH)
                cp.start()
                copies.append(cp)
        for cp in copies:
            cp.wait()

    return pl.pallas_call(
        body, name=name, in_specs=[ANY] * n, out_specs=[ANY] * n,
        out_shape=[jax.ShapeDtypeStruct((N_CHIPS,) + s.shape, s.dtype) for s in shards],
        scratch_shapes=[pltpu.SemaphoreType.DMA((3 * n,)), pltpu.SemaphoreType.DMA((3 * n,)),
                        pltpu.SemaphoreType.DMA((n,))],
        compiler_params=pltpu.CompilerParams(has_side_effects=True),
    )(*shards)


_RELS7 = tuple((r >> 2 & 1, r >> 1 & 1, r & 1) for r in range(1, N_DEV))


def _scatter_pieces(grads, *, name):
    n = len(grads)

    def body(*refs):
        ins, outs = refs[:n], refs[n:2 * n]
        send, recv, loc = refs[2 * n:]
        x, y, c = lax.axis_index("x"), lax.axis_index("y"), lax.axis_index("c")
        me = 4 * x + 2 * y + c
        copies = []

        def piece(k, px, py, pc):
            half = ins[k].shape[1] // 2
            return ins[k].at[2 * px + py, pl.ds(pc * half, half), :]

        for k in range(n):
            cp = pltpu.make_async_copy(piece(k, x, y, c), outs[k].at[me], loc.at[k])
            cp.start()
            copies.append(cp)
        for r, (rx, ry, rc) in enumerate(_RELS7):
            tx, ty, tc = _flip(x, rx), _flip(y, ry), _flip(c, rc)
            for k in range(n):
                cp = pltpu.make_async_remote_copy(
                    src_ref=piece(k, tx, ty, tc), dst_ref=outs[k].at[me], send_sem=send.at[r * n + k],
                    recv_sem=recv.at[r * n + k], device_id=(tx, ty, tc), device_id_type=MESH)
                cp.start()
                copies.append(cp)
        for cp in copies:
            cp.wait()

    return pl.pallas_call(
        body, name=name, in_specs=[ANY] * n, out_specs=[ANY] * n,
        out_shape=[jax.ShapeDtypeStruct((N_DEV, g.shape[1] // 2, g.shape[2]), g.dtype) for g in grads],
        scratch_shapes=[pltpu.SemaphoreType.DMA((7 * n,)), pltpu.SemaphoreType.DMA((7 * n,)),
                        pltpu.SemaphoreType.DMA((n,))],
        compiler_params=pltpu.CompilerParams(has_side_effects=True),
    )(*grads)


def _join_cores(halves, *, name):
    n = len(halves)

    def body(*refs):
        ins, outs = refs[:n], refs[n:2 * n]
        send, recv, loc = refs[2 * n:]
        x, y, c = lax.axis_index("x"), lax.axis_index("y"), lax.axis_index("c")
        copies = []
        for k in range(n):
            half = ins[k].shape[0]
            mine = outs[k].at[0, pl.ds(c * half, half), :]
            cp = pltpu.make_async_copy(ins[k], mine, loc.at[k])
            cp.start()
            copies.append(cp)
            cp = pltpu.make_async_remote_copy(
                src_ref=ins[k], dst_ref=mine, send_sem=send.at[k], recv_sem=recv.at[k],
                device_id=(x, y, 1 - c), device_id_type=MESH)
            cp.start()
            copies.append(cp)
        for cp in copies:
            cp.wait()

    in_vmem = pl.BlockSpec(memory_space=pltpu.VMEM)
    return pl.pallas_call(
        body, name=name, in_specs=[in_vmem] * n, out_specs=[in_vmem] * n,
        out_shape=[jax.ShapeDtypeStruct((1, 2 * h.shape[0], h.shape[1]), h.dtype) for h in halves],
        scratch_shapes=[pltpu.SemaphoreType.DMA((n,)), pltpu.SemaphoreType.DMA((n,)), pltpu.SemaphoreType.DMA((n,))],
        compiler_params=pltpu.CompilerParams(has_side_effects=True, vmem_limit_bytes=VMEM_LIMIT),
    )(*halves)


def _allreduce_small(buf, *, name):
    r, n = buf.shape
    half = r // 2
    rels = ((1, 0), (0, 1), (1, 1))

    def body(in_ref, out_ref, sib_ref, chips_ref, send, recv):
        x, y, c = lax.axis_index("x"), lax.axis_index("y"), lax.axis_index("c")
        sibling = (x, y, 1 - c)
        chip = 2 * x + y
        rows = pl.ds(pl.multiple_of(c * half, SUBLANES), half)

        swap = pltpu.make_async_remote_copy(src_ref=in_ref, dst_ref=sib_ref, send_sem=send.at[0], recv_sem=recv.at[0],
                                            device_id=sibling, device_id_type=MESH)
        swap.start()
        swap.wait()
        chips_ref[chip] = in_ref[rows, :] + sib_ref[rows, :]

        sends = []
        for k, (rx, ry) in enumerate(rels):
            cp = pltpu.make_async_remote_copy(
                src_ref=chips_ref.at[chip], dst_ref=chips_ref.at[chip], send_sem=send.at[1 + k],
                recv_sem=recv.at[1 + k], device_id=(_flip(x, rx), _flip(y, ry), c), device_id_type=MESH)
            cp.start()
            sends.append(cp)
        for cp in sends:
            cp.wait()
        total = chips_ref[0]
        for s in range(1, N_CHIPS):
            total = total + chips_ref[s]
        out_ref[rows, :] = total

        back = pltpu.make_async_remote_copy(src_ref=out_ref.at[rows, :], dst_ref=out_ref.at[rows, :],
                                            send_sem=send.at[4], recv_sem=recv.at[4],
                                            device_id=sibling, device_id_type=MESH)
        back.start()
        back.wait()

    vmem = pl.BlockSpec(memory_space=pltpu.VMEM)
    return pl.pallas_call(
        body, name=name, in_specs=[vmem], out_specs=vmem,
        out_shape=jax.ShapeDtypeStruct((r, n), F32),
        scratch_shapes=[pltpu.VMEM((r, n), F32), pltpu.VMEM((N_CHIPS, half, n), F32),
                        pltpu.SemaphoreType.DMA((5,)), pltpu.SemaphoreType.DMA((5,))],
        compiler_params=pltpu.CompilerParams(has_side_effects=True, vmem_limit_bytes=VMEM_LIMIT),
    )(buf)


def _pack(arrs):
    flat = []
    for a in arrs:
        v = a.reshape(-1)
        pad = (-v.shape[0]) % LANES
        if pad:
            v = jnp.pad(v, (0, pad))
        flat.append(v)
    v = jnp.concatenate(flat)
    pad = (-v.shape[0]) % (LANES * SUBLANES)
    if pad:
        v = jnp.pad(v, (0, pad))
    return v.reshape(-1, LANES)


def _unpack(buf, shapes):
    v = buf.reshape(-1)
    out, off = [], 0
    for s in shapes:
        n = math.prod(s)
        out.append(v[off:off + n].reshape(s))
        off += n + (-n) % LANES
    return out


def kernel(x, norm_g, final_g, lru_w_in, lru_conv_w, lru_conv_b, lru_wa, lru_ba, lru_wx, lru_bx, lru_a_param, lru_w_out, fox_w_in, fox_b_f, fox_w_out, loss_target, m_norm_g, m_final_g, m_lru_w_in, m_lru_conv_w, m_lru_conv_b, m_lru_wa, m_lru_ba, m_lru_wx, m_lru_bx, m_lru_a_param, m_lru_w_out, m_fox_w_in, m_fox_b_f, m_fox_w_out, v_norm_g, v_final_g, v_lru_w_in, v_lru_conv_w, v_lru_conv_b, v_lru_wa, v_lru_ba, v_lru_wx, v_lru_bx, v_lru_a_param, v_lru_w_out, v_fox_w_in, v_fox_b_f, v_fox_w_out):
    t, d = x.shape[1], x.shape[2]
    w = lru_wa.shape[1] * LRU_BLOCK_W
    f = FOX_HEADS * FOX_HEAD_DIM
    npair = f // LANES
    x0 = x.reshape(t, d)
    tgt = loss_target.reshape(t, d)
    chip = 2 * lax.axis_index("x") + lax.axis_index("y")

    g_lwi, g_lwo, g_fwi, g_fwo, g_cw = _gather_chips(
        [lru_w_in[0].astype(BF16), lru_w_out[0].astype(BF16), fox_w_in[0].astype(BF16), fox_w_out[0].astype(BF16),
         lru_conv_w[0]], name="gather_weights")
    cg = w // 2
    lwi = jnp.concatenate([g_lwi[0], g_lwi[2], g_lwi[1], g_lwi[3]], axis=1)
    lwo = g_lwo.reshape(w, d)
    fwi = jnp.concatenate([g_fwi[s] for s in range(N_CHIPS)], axis=1)
    w_qkv, w_g2 = fwi[:, :3 * f], fwi[:, 3 * f:4 * f]
    w_f = jnp.pad(fwi[:, 4 * f:], ((0, 0), (0, LANES - FOX_HEADS)))
    fwo = g_fwo.reshape(f, d)
    conv_w = jnp.concatenate([g_cw[s] for s in range(N_CHIPS)], axis=1)
    conv_b, ba, bx, a_param = lru_conv_b, lru_ba, lru_bx, lru_a_param
    wa, wx = lru_wa[0], lru_wx[0]
    b_f = jnp.pad(fox_b_f, ((0, 0), (0, LANES - FOX_HEADS)))

    h0 = _rmsnorm(x0, norm_g[0], name="norm0")
    u = _matmul(h0, lwi, name="lru_in")
    y1, hs = _lru_fwd(u, conv_w, conv_b, wa, ba, wx, bx, a_param, cg=cg, name="lru_fwd")
    x1 = _matmul(y1, lwo, add=x0, name="lru_out")
    h1 = _rmsnorm(x1, norm_g[1], name="norm1")
    qkv = _matmul(h1, w_qkv, out_dtype=BF16, name="fox_qkv")
    gate2 = _matmul(h1, w_g2, name="fox_gate")
    flog = _matmul(h1, w_f, name="fox_f")
    cum = _fgate_fwd(flog, b_f, name="fgate_fwd")
    cum16 = cum[:, :FOX_HEADS]
    ckt = cum16.T.reshape(npair, 2, t)
    cke = jnp.repeat(cum16, FOX_HEAD_DIM, axis=1)
    a_start, a_end = _attn_skip_tables(_attn_row_stats(qkv, name="attn_row_stats"), cum16, _tile(t, ATTN_TILE))
    o, y2, lse = _attn_fwd(a_start, qkv, ckt, gate2, name="attn_fwd")
    x2 = _matmul(y2, fwo, add=x1, name="fox_out")
    lsum, dx2, dgf = _final_loss(x2, tgt, final_g, name="final_loss")
    loss = lax.psum(0.5 * jnp.sum(lsum) / d, ("x", "y", "c"))

    d_fwo = _matmul(y2, dx2, ta=True, out_dtype=BF16, name="d_fox_w_out")
    dy2 = _matmul(dx2, fwo, tb=True, name="d_y2")
    do, dgate2, dl = _fox_post_bwd(dy2, o, gate2, name="fox_post_bwd")
    lt = lse
    dt = dl[:, :FOX_HEADS].T.reshape(npair, 2, t)
    dq, dk, dv, dck, dcq = _attn_bwd(a_end, qkv, do, lt, dt, cke, name="attn_bwd")
    dcum = jnp.pad((dck + dcq).reshape(FOX_HEADS, t).T, ((0, 0), (0, LANES - FOX_HEADS)))
    dflog, db_f = _fgate_bwd(dcum, flog, b_f, name="fgate_bwd")
    du2 = jnp.concatenate([dq.astype(BF16), dk, dv, dgate2], axis=1)
    dflog_b = dflog.astype(BF16)
    dh1 = _matmul(du2, fwi[:, :4 * f], tb=True, name="d_h1_a")
    dh1 = _matmul(dflog_b, w_f, tb=True, add=dh1, name="d_h1_b")
    d_fwi_a = _matmul(h1, du2, ta=True, out_dtype=BF16, name="d_fox_w_in_a")
    d_fwi_b = _matmul(h1, dflog_b, ta=True, out_dtype=BF16, name="d_fox_w_in_b")
    d_fwi = jnp.concatenate([d_fwi_a, d_fwi_b[:, :FOX_HEADS]], axis=1)
    dx1, dg1 = _rmsnorm_bwd(dh1, x1, norm_g[1], dx2, name="norm1_bwd")

    d_lwo = _matmul(y1, dx1, ta=True, out_dtype=BF16, name="d_lru_w_out")
    dy1 = _matmul(dx1, lwo, tb=True, name="d_y1")
    dxb, dgate, d_cw, d_cb, d_wa, d_ba, d_wx, d_bx, d_ap = _lru_bwd(
        u, hs, dy1, conv_w, conv_b, wa, ba, wx, bx, a_param, cg=cg, name="lru_bwd")
    du = jnp.concatenate([dxb[:, :cg], dgate[:, :cg], dxb[:, cg:], dgate[:, cg:]], axis=1)
    dh0 = _matmul(du, lwi, tb=True, name="d_h0")
    d_lwi_p = _matmul(h0, du, ta=True, out_dtype=BF16, name="d_lru_w_in")
    dx0, dg0 = _rmsnorm_bwd(dh0, x0, norm_g[0], dx1, name="norm0_bwd")

    csz = cg
    g_lwi4 = jnp.stack([d_lwi_p[:, 0:csz], d_lwi_p[:, 2 * csz:3 * csz], d_lwi_p[:, csz:2 * csz],
                        d_lwi_p[:, 3 * csz:]])
    n_fwi = fox_w_in.shape[2]
    g_fwi4 = jnp.stack([d_fwi[:, s * n_fwi:(s + 1) * n_fwi] for s in range(N_CHIPS)])
    g_lwo4 = d_lwo.reshape(N_CHIPS, w // N_CHIPS, d)
    g_fwo4 = d_fwo.reshape(N_CHIPS, f // N_CHIPS, d)
    lands = _scatter_pieces([g_lwi4, g_lwo4, g_fwi4, g_fwo4], name="scatter_grads")
    halves = [_sum_slots(l, name="sum_" + nm) for l, nm in zip(lands, ("lru_w_in", "lru_w_out", "fox_w_in", "fox_w_out"))]
    big_g = _join_cores(halves, name="join_cores")

    small_g = [jnp.concatenate([dg0, dg1], axis=0), dgf.reshape(d), d_cw, d_cb, d_wa, d_ba, d_wx, d_bx, d_ap,
               db_f[:, :FOX_HEADS]]
    gsum = _allreduce_small(_pack(small_g), name="allreduce_small")
    zc = jnp.zeros((CONV_WIDTH, w), F32)
    pk_w = _pack([norm_g, final_g, zc, lru_conv_b, lru_wa, lru_ba, lru_wx, lru_bx, lru_a_param, fox_b_f])
    pk_m = _pack([m_norm_g, m_final_g, zc, m_lru_conv_b, m_lru_wa, m_lru_ba, m_lru_wx, m_lru_bx, m_lru_a_param,
                  m_fox_b_f])
    pk_v = _pack([v_norm_g, v_final_g, zc + 1.0, v_lru_conv_b, v_lru_wa, v_lru_ba, v_lru_wx, v_lru_bx,
                  v_lru_a_param, v_fox_b_f])
    s_delta, s_m, s_v = _adamw(pk_w[None], gsum[None], pk_m[None], pk_v[None], name="adamw_small")
    out_shapes = [norm_g.shape, final_g.shape, (CONV_WIDTH, w), lru_conv_b.shape, lru_wa.shape, lru_ba.shape,
                  lru_wx.shape, lru_bx.shape, lru_a_param.shape, fox_b_f.shape]
    sg = _unpack(gsum, out_shapes)
    sd = _unpack(s_delta, out_shapes)
    sm = _unpack(s_m, out_shapes)
    sv = _unpack(s_v, out_shapes)

    ncw = lru_conv_w.shape[2]
    g_cw_loc = lax.dynamic_slice_in_dim(sg[2], chip * ncw, ncw, axis=1)
    g_cw_loc = g_cw_loc[None]
    cw_d, cw_m, cw_v = _adamw(lru_conv_w, g_cw_loc, m_lru_conv_w, v_lru_conv_w, name="adamw_conv_w")

    big = []
    for nm, wt, g, mm, vv in (("lru_w_in", lru_w_in, big_g[0], m_lru_w_in, v_lru_w_in),
                              ("lru_w_out", lru_w_out, big_g[1], m_lru_w_out, v_lru_w_out),
                              ("fox_w_in", fox_w_in, big_g[2], m_fox_w_in, v_fox_w_in),
                              ("fox_w_out", fox_w_out, big_g[3], m_fox_w_out, v_fox_w_out)):
        big.append((g,) + tuple(_adamw(wt, g, mm, vv, name="adamw_" + nm)))

    def assemble(idx):
        small = (sg, sd, sm, sv)[idx]
        cw = (g_cw_loc, cw_d, cw_m, cw_v)[idx]
        return [small[0], small[1], big[0][idx], cw, small[3], small[4], small[5], small[6], small[7], small[8],
                big[1][idx], big[2][idx], small[9], big[3][idx]]

    grad_x = dx0.reshape(1, t, d)
    return (loss, grad_x, *assemble(0), *assemble(1), *assemble(2), *assemble(3))
```

```python
import functools
import math

import jax
import jax.numpy as jnp
from jax import lax
from jax.experimental import pallas as pl
from jax.experimental.pallas import tpu as pltpu

F32 = jnp.float32
BF16 = jnp.bfloat16

EPS = 1e-6
LRU_C = 8.0
LRU_BLOCK_W = 128
CONV_WIDTH = 4
FOX_HEADS = 16
FOX_HEAD_DIM = 64
NEG_INF = -1e30
ADAM_LR = 0.001
ADAM_B1 = 0.9
ADAM_B2 = 0.999
ADAM_EPS = 1e-08
ADAM_WD = 0.01
ADAM_STEP = 10

LANES = 128
SUBLANES = 8
VMEM_LIMIT = 56 * 1024 * 1024
TINY = 1e-30
N_CHIPS = 4
N_DEV = 8
MESH = pl.DeviceIdType.MESH
HIGHEST = lax.Precision.HIGHEST


def _tile(n, pref):
    t = min(n, pref)
    while n % t:
        t //= 2
    return t


def _cparams(dims=None):
    return pltpu.CompilerParams(dimension_semantics=dims, vmem_limit_bytes=VMEM_LIMIT)


def _sigmoid(x):
    return 0.5 * jnp.tanh(0.5 * x) + 0.5


def _log1p(x):
    u = 1.0 + x
    return jnp.where(u == 1.0, x, jnp.log(u) * x / (u - 1.0))


def _softplus(x):
    return jnp.maximum(x, 0.0) + _log1p(jnp.exp(-jnp.abs(x)))


MM_TILE = 1024
MM_FULL_K = 1536


def _matmul(a, b, *, name, ta=False, tb=False, out_dtype=F32, add=None, tm=MM_TILE, tn=MM_TILE, tk=None):
    if ta:
        kdim, m = a.shape
    else:
        m, kdim = a.shape
    if tb:
        n, kb = b.shape
    else:
        kb, n = b.shape
    assert kdim == kb, (a.shape, b.shape, ta, tb)
    if tk is None:
        tk = kdim if kdim <= MM_FULL_K else MM_TILE
    tm, tn, tk = _tile(m, tm), _tile(n, tn), _tile(kdim, tk)
    nk = kdim // tk
    dn = (((0 if ta else 1,), (1 if tb else 0,)), ((), ()))
    has_add = add is not None

    def body(*refs):
        if has_add:
            a_ref, b_ref, add_ref, o_ref = refs[:4]
        else:
            a_ref, b_ref, o_ref = refs[:3]
        part = lax.dot_general(a_ref[...].astype(BF16), b_ref[...].astype(BF16), dn, preferred_element_type=F32)

        def finish(r):
            if has_add:
                r = r + add_ref[...].astype(F32)
            o_ref[...] = r.astype(o_ref.dtype)

        if nk == 1:
            finish(part)
            return
        acc_ref = refs[-1]
        k = pl.program_id(2)

        @pl.when(k == 0)
        def _():
            acc_ref[...] = part

        @pl.when(k > 0)
        def _():
            acc_ref[...] += part

        @pl.when(k == nk - 1)
        def _():
            finish(acc_ref[...])

    a_spec = pl.BlockSpec((tk, tm), lambda i, j, k: (k, i)) if ta else pl.BlockSpec((tm, tk), lambda i, j, k: (i, k))
    b_spec = pl.BlockSpec((tn, tk), lambda i, j, k: (j, k)) if tb else pl.BlockSpec((tk, tn), lambda i, j, k: (k, j))
    o_spec = pl.BlockSpec((tm, tn), lambda i, j, k: (i, j))
    in_specs = [a_spec, b_spec] + ([o_spec] if has_add else [])
    args = (a, b) + ((add,) if has_add else ())
    return pl.pallas_call(
        body, name=name, grid=(m // tm, n // tn, nk), in_specs=in_specs, out_specs=o_spec,
        out_shape=jax.ShapeDtypeStruct((m, n), out_dtype),
        scratch_shapes=[pltpu.VMEM((tm, tn), F32)] if nk > 1 else [],
        compiler_params=_cparams(("parallel", "parallel", "arbitrary")),
    )(*args)


def _rmsnorm(x, g, *, name):
    t, d = x.shape
    tt = _tile(t, 512)

    def body(x_ref, g_ref, o_ref):
        xf = x_ref[...]
        rstd = lax.rsqrt(jnp.mean(xf * xf, axis=-1, keepdims=True) + EPS)
        o_ref[...] = (xf * rstd * g_ref[...]).astype(o_ref.dtype)

    return pl.pallas_call(
        body, name=name, grid=(t // tt,),
        in_specs=[pl.BlockSpec((tt, d), lambda i: (i, 0)), pl.BlockSpec((1, d), lambda i: (0, 0))],
        out_specs=pl.BlockSpec((tt, d), lambda i: (i, 0)),
        out_shape=jax.ShapeDtypeStruct((t, d), BF16),
        compiler_params=_cparams(("parallel",)),
    )(x, g.reshape(1, d))


def _rmsnorm_bwd(dh, x, g, dres, *, name):
    t, d = x.shape
    tt = _tile(t, 512)

    def body(dh_ref, x_ref, g_ref, dres_ref, dx_ref, dg_ref):
        i = pl.program_id(0)

        @pl.when(i == 0)
        def _():
            dg_ref[...] = jnp.zeros_like(dg_ref)

        xf = x_ref[...]
        rstd = lax.rsqrt(jnp.mean(xf * xf, axis=-1, keepdims=True) + EPS)
        xhat = xf * rstd
        dhf = dh_ref[...].astype(F32)
        dxhat = dhf * g_ref[...]
        mt = jnp.mean(dxhat * xhat, axis=-1, keepdims=True)
        dx_ref[...] = dres_ref[...] + rstd * (dxhat - xhat * mt)
        dg_ref[...] += jnp.sum(dhf * xhat, axis=0, keepdims=True)

    blk = pl.BlockSpec((tt, d), lambda i: (i, 0))
    vec = pl.BlockSpec((1, d), lambda i: (0, 0))
    return pl.pallas_call(
        body, name=name, grid=(t // tt,),
        in_specs=[blk, blk, vec, blk], out_specs=[blk, vec],
        out_shape=[jax.ShapeDtypeStruct((t, d), F32), jax.ShapeDtypeStruct((1, d), F32)],
        compiler_params=_cparams(("arbitrary",)),
    )(dh, x, g.reshape(1, d), dres)


def _final_loss(x2, tgt, g, *, name):
    t, d = x2.shape
    tt = _tile(t, 512)

    def body(x_ref, t_ref, g_ref, l_ref, dx_ref, dg_ref):
        i = pl.program_id(0)

        @pl.when(i == 0)
        def _():
            dg_ref[...] = jnp.zeros_like(dg_ref)
            l_ref[...] = jnp.zeros_like(l_ref)

        xf = x_ref[...]
        gg = g_ref[...]
        rstd = lax.rsqrt(jnp.mean(xf * xf, axis=-1, keepdims=True) + EPS)
        xhat = xf * rstd
        err = xhat * gg - t_ref[...]
        l_ref[...] += jnp.sum(err * err, axis=0, keepdims=True)
        dy = err * (1.0 / d)
        dxhat = dy * gg
        mt = jnp.mean(dxhat * xhat, axis=-1, keepdims=True)
        dx_ref[...] = rstd * (dxhat - xhat * mt)
        dg_ref[...] += jnp.sum(dy * xhat, axis=0, keepdims=True)

    blk = pl.BlockSpec((tt, d), lambda i: (i, 0))
    vec = pl.BlockSpec((1, d), lambda i: (0, 0))
    return pl.pallas_call(
        body, name=name, grid=(t // tt,),
        in_specs=[blk, blk, vec], out_specs=[vec, blk, vec],
        out_shape=[jax.ShapeDtypeStruct((1, d), F32), jax.ShapeDtypeStruct((t, d), F32),
                   jax.ShapeDtypeStruct((1, d), F32)],
        compiler_params=_cparams(("arbitrary",)),
    )(x2, tgt, g.reshape(1, d))


def _shift_down(prev8, cur, s):
    ext = jnp.concatenate([prev8, cur], axis=0)
    if s == 0:
        return cur
    return pltpu.roll(ext, s, 0)[SUBLANES:, :]


def _shift_up(cur, next8, s):
    if s == 0:
        return cur
    n = cur.shape[0]
    ext = jnp.concatenate([cur, next8], axis=0)
    return pltpu.roll(ext, n + SUBLANES - s, 0)[:n, :]


def _lru_gates(xc, wa, ba, wx, bx, sp):
    xcb = xc.astype(BF16)
    r = _sigmoid(jnp.dot(xcb, wa, preferred_element_type=F32) + ba)
    ig = _sigmoid(jnp.dot(xcb, wx, preferred_element_type=F32) + bx)
    log_a = -LRU_C * r * sp
    a = jnp.exp(log_a)
    z = -jnp.tanh(log_a) * (a * a + 1.0)
    inv_mult = lax.rsqrt(jnp.maximum(z, TINY))
    return r, ig, a, z * inv_mult, inv_mult


def _lru_specs(tt, cg, n_groups, nt, reverse):
    ncol = cg // LANES
    if reverse:
        ti = lambda i: nt - 1 - i
    else:
        ti = lambda i: i
    hb = tt // SUBLANES
    cur = lambda col: pl.BlockSpec((tt, cg), lambda g, i: (ti(i), 2 * g + col))
    prev = lambda col: pl.BlockSpec((SUBLANES, cg), lambda g, i: (jnp.maximum(ti(i) * hb - 1, 0), 2 * g + col))
    chan = lambda rows: pl.BlockSpec((rows, cg), lambda g, i: (0, g))
    wblk = pl.BlockSpec((ncol, LRU_BLOCK_W, LRU_BLOCK_W), lambda g, i: (g, 0, 0))
    plain = pl.BlockSpec((tt, cg), lambda g, i: (ti(i), g))
    plain_prev = pl.BlockSpec((SUBLANES, cg), lambda g, i: (jnp.maximum(ti(i) * hb - 1, 0), g))
    return cur, prev, chan, wblk, plain, plain_prev


def _lru_fwd(u, conv_w, conv_b, wa, ba, wx, bx, a_param, *, cg, name):
    t, w2 = u.shape
    w = w2 // 2
    n_groups = w // cg
    ncol = cg // LANES
    tt = _tile(t, 256)
    nt = t // tt
    cur, prev, chan, wblk, plain, _ = _lru_specs(tt, cg, n_groups, nt, False)

    def body(xb_ref, xp_ref, gate_ref, cw_ref, cb_ref, wa_ref, ba_ref, wx_ref, bx_ref, ap_ref,
             y_ref, hs_ref, h_ref, a_s, b_s):
        i = pl.program_id(1)

        @pl.when(i == 0)
        def _():
            h_ref[...] = jnp.zeros_like(h_ref)

        keep = (i > 0).astype(F32)
        for n in range(ncol):
            sl = slice(n * LANES, (n + 1) * LANES)
            xb = xb_ref[:, sl]
            xp = xp_ref[:, sl] * keep
            xc = cb_ref[:, sl] + cw_ref[3:4, sl] * xb
            for s in range(1, CONV_WIDTH):
                xc = xc + cw_ref[3 - s:4 - s, sl] * _shift_down(xp, xb, s)
            sp = _softplus(-ap_ref[:, sl])
            _, ig, a, mult, _ = _lru_gates(xc, wa_ref[n].astype(BF16), ba_ref[:, sl],
                                           wx_ref[n].astype(BF16), bx_ref[:, sl], sp)
            a_s[:, sl] = a
            b_s[:, sl] = mult * (ig * xc)

        def step(g, h):
            base = pl.multiple_of(g * SUBLANES, SUBLANES)
            for r in range(SUBLANES):
                h = a_s[pl.ds(base + r, 1), :] * h + b_s[pl.ds(base + r, 1), :]
                hs_ref[pl.ds(base + r, 1), :] = h
            return h

        h = lax.fori_loop(0, tt // SUBLANES, step, h_ref[0:1, :])
        h_ref[0:1, :] = h
        gate = gate_ref[...]
        y_ref[...] = (hs_ref[...] * (gate * _sigmoid(gate))).astype(y_ref.dtype)

    return pl.pallas_call(
        body, name=name, grid=(n_groups, nt),
        in_specs=[cur(0), prev(0), cur(1), chan(CONV_WIDTH), chan(1), wblk, chan(1), wblk, chan(1), chan(1)],
        out_specs=[plain, plain],
        out_shape=[jax.ShapeDtypeStruct((t, w), BF16), jax.ShapeDtypeStruct((t, w), F32)],
        scratch_shapes=[pltpu.VMEM((SUBLANES, cg), F32), pltpu.VMEM((tt, cg), F32), pltpu.VMEM((tt, cg), F32)],
        compiler_params=_cparams(("parallel", "arbitrary")),
    )(u, u, u, conv_w, conv_b, wa, ba, wx, bx, a_param)


def _lru_bwd(u, hs, dy, conv_w, conv_b, wa, ba, wx, bx, a_param, *, cg, name):
    t, w2 = u.shape
    w = w2 // 2
    n_groups = w // cg
    ncol = cg // LANES
    tt = _tile(t, 256)
    nt = t // tt
    cur, prev, chan, wblk, plain, plain_prev = _lru_specs(tt, cg, n_groups, nt, True)
    tn_dims = (((0,), (0,)), ((), ()))
    nt_dims = (((1,), (1,)), ((), ()))

    def body(xb_ref, xp_ref, gate_ref, hs_ref, hp_ref, dy_ref, cw_ref, cb_ref, wa_ref, ba_ref, wx_ref, bx_ref,
             ap_ref, dxb_ref, dgate_ref, dcw_ref, dcb_ref, dwa_ref, dba_ref, dwx_ref, dbx_ref, dsp_ref,
             c_ref, nx_ref, a_s, dhs_s, lam_s):
        i = pl.program_id(1)
        first_time_block = i == nt - 1

        @pl.when(i == 0)
        def _():
            c_ref[...] = jnp.zeros_like(c_ref)
            nx_ref[...] = jnp.zeros_like(nx_ref)
            for r in (dcw_ref, dcb_ref, dwa_ref, dba_ref, dwx_ref, dbx_ref, dsp_ref):
                r[...] = jnp.zeros_like(r)

        keep = jnp.where(first_time_block, 0.0, 1.0).astype(F32)
        gate = gate_ref[...]
        sg = _sigmoid(gate)
        dyv = dy_ref[...]
        hsv = hs_ref[...]
        dhs_s[...] = dyv * (gate * sg)
        dgate_ref[...] = (dyv * hsv * (sg * (1.0 + gate * (1.0 - sg)))).astype(dgate_ref.dtype)

        saved = []
        for n in range(ncol):
            sl = slice(n * LANES, (n + 1) * LANES)
            xb = xb_ref[:, sl]
            xp = xp_ref[:, sl] * keep
            shifted = [xb] + [_shift_down(xp, xb, s) for s in range(1, CONV_WIDTH)]
            xc = cb_ref[:, sl] + cw_ref[3:4, sl] * xb
            for s in range(1, CONV_WIDTH):
                xc = xc + cw_ref[3 - s:4 - s, sl] * shifted[s]
            sp = _softplus(-ap_ref[:, sl])
            wab = wa_ref[n].astype(BF16)
            wxb = wx_ref[n].astype(BF16)
            r, ig, a, mult, inv_mult = _lru_gates(xc, wab, ba_ref[:, sl], wxb, bx_ref[:, sl], sp)
            a_s[:, sl] = a
            saved.append((sl, shifted, xc, sp, wab, wxb, r, ig, a, mult, inv_mult))

        def step(g, c):
            base = pl.multiple_of(tt - SUBLANES - g * SUBLANES, SUBLANES)
            for r in range(SUBLANES - 1, -1, -1):
                lam = dhs_s[pl.ds(base + r, 1), :] + c
                lam_s[pl.ds(base + r, 1), :] = lam
                c = a_s[pl.ds(base + r, 1), :] * lam
            return c

        c_ref[0:1, :] = lax.fori_loop(0, tt // SUBLANES, step, c_ref[0:1, :])

        for n in range(ncol):
            sl, shifted, xc, sp, wab, wxb, r, ig, a, mult, inv_mult = saved[n]
            lam = lam_s[:, sl]
            hprev = _shift_down(hp_ref[:, sl] * keep, hs_ref[:, sl], 1)
            da = lam * hprev
            dmult = lam * (ig * xc)
            dlog_a = da * a - dmult * (a * a * inv_mult)
            di = lam * (mult * xc)
            dxc = lam * (mult * ig)
            dr = dlog_a * (-LRU_C * sp)
            dsp_ref[:, sl] += jnp.sum(dlog_a * (-LRU_C * r), axis=0, keepdims=True)
            dza = dr * (r * (1.0 - r))
            dzx = di * (ig * (1.0 - ig))
            dba_ref[:, sl] += jnp.sum(dza, axis=0, keepdims=True)
            dbx_ref[:, sl] += jnp.sum(dzx, axis=0, keepdims=True)
            xcb = xc.astype(BF16)
            dzab = dza.astype(BF16)
            dzxb = dzx.astype(BF16)
            dwa_ref[n] += lax.dot_general(xcb, dzab, tn_dims, preferred_element_type=F32)
            dwx_ref[n] += lax.dot_general(xcb, dzxb, tn_dims, preferred_element_type=F32)
            dxc = dxc + lax.dot_general(dzab, wab, nt_dims, preferred_element_type=F32)
            dxc = dxc + lax.dot_general(dzxb, wxb, nt_dims, preferred_element_type=F32)
            dcb_ref[:, sl] += jnp.sum(dxc, axis=0, keepdims=True)
            for s in range(CONV_WIDTH):
                dcw_ref[3 - s:4 - s, sl] += jnp.sum(dxc * shifted[s], axis=0, keepdims=True)
            nx = nx_ref[:, sl]
            dxb = cw_ref[3:4, sl] * dxc
            for s in range(1, CONV_WIDTH):
                dxb = dxb + cw_ref[3 - s:4 - s, sl] * _shift_up(dxc, nx, s)
            dxb_ref[:, sl] = dxb.astype(dxb_ref.dtype)
            nx_ref[:, sl] = dxc[0:SUBLANES, :]

        @pl.when(first_time_block)
        def _():
            dsp_ref[...] = dsp_ref[...] * (-_sigmoid(-ap_ref[...]))

    dxb_spec = pl.BlockSpec((tt, cg), lambda g, i: (nt - 1 - i, g))
    outs = pl.pallas_call(
        body, name=name, grid=(n_groups, nt),
        in_specs=[cur(0), prev(0), cur(1), plain, plain_prev, plain, chan(CONV_WIDTH), chan(1), wblk, chan(1), wblk,
                  chan(1), chan(1)],
        out_specs=[dxb_spec, dxb_spec, chan(CONV_WIDTH), chan(1), wblk, chan(1), wblk, chan(1), chan(1)],
        out_shape=[jax.ShapeDtypeStruct((t, w), BF16), jax.ShapeDtypeStruct((t, w), BF16),
                   jax.ShapeDtypeStruct(conv_w.shape, F32), jax.ShapeDtypeStruct(conv_b.shape, F32),
                   jax.ShapeDtypeStruct(wa.shape, F32), jax.ShapeDtypeStruct(ba.shape, F32),
                   jax.ShapeDtypeStruct(wx.shape, F32), jax.ShapeDtypeStruct(bx.shape, F32),
                   jax.ShapeDtypeStruct(a_param.shape, F32)],
        scratch_shapes=[pltpu.VMEM((SUBLANES, cg), F32), pltpu.VMEM((SUBLANES, cg), F32),
                        pltpu.VMEM((tt, cg), F32), pltpu.VMEM((tt, cg), F32), pltpu.VMEM((tt, cg), F32)],
        compiler_params=_cparams(("parallel", "arbitrary")),
    )(u, u, u, hs, hs, dy, conv_w, conv_b, wa, ba, wx, bx, a_param)
    return outs


def _fgate_fwd(f, b_f, *, name):
    t, n = f.shape
    tt = _tile(t, 256)

    def body(f_ref, b_ref, cum_ref, carry_ref):
        i = pl.program_id(0)

        @pl.when(i == 0)
        def _():
            carry_ref[...] = jnp.zeros_like(carry_ref)

        z = f_ref[...] + b_ref[...]
        lf = jnp.minimum(z, 0.0) - _log1p(jnp.exp(-jnp.abs(z)))
        row = lax.broadcasted_iota(jnp.int32, (tt, tt), 0)
        col = lax.broadcasted_iota(jnp.int32, (tt, tt), 1)
        tri = (col <= row).astype(F32)
        cum = jnp.dot(tri, lf, precision=HIGHEST, preferred_element_type=F32) + carry_ref[0:1, :]
        cum_ref[...] = cum
        carry_ref[0:1, :] = cum[tt - 1:tt, :]

    return pl.pallas_call(
        body, name=name, grid=(t // tt,),
        in_specs=[pl.BlockSpec((tt, n), lambda i: (i, 0)), pl.BlockSpec((1, n), lambda i: (0, 0))],
        out_specs=pl.BlockSpec((tt, n), lambda i: (i, 0)),
        out_shape=jax.ShapeDtypeStruct((t, n), F32),
        scratch_shapes=[pltpu.VMEM((SUBLANES, n), F32)],
        compiler_params=_cparams(("arbitrary",)),
    )(f, b_f)


def _fgate_bwd(dcum, f, b_f, *, name):
    t, n = f.shape
    tt = _tile(t, 256)
    nt = t // tt

    def body(dc_ref, f_ref, b_ref, df_ref, db_ref, carry_ref):
        i = pl.program_id(0)

        @pl.when(i == 0)
        def _():
            carry_ref[...] = jnp.zeros_like(carry_ref)
            db_ref[...] = jnp.zeros_like(db_ref)

        row = lax.broadcasted_iota(jnp.int32, (tt, tt), 0)
        col = lax.broadcasted_iota(jnp.int32, (tt, tt), 1)
        triu = (col >= row).astype(F32)
        dlf = jnp.dot(triu, dc_ref[...], precision=HIGHEST, preferred_element_type=F32) + carry_ref[0:1, :]
        carry_ref[0:1, :] = dlf[0:1, :]
        z = f_ref[...] + b_ref[...]
        df = dlf * _sigmoid(-z)
        df_ref[...] = df
        db_ref[...] += jnp.sum(df, axis=0, keepdims=True)

    blk = pl.BlockSpec((tt, n), lambda i: (nt - 1 - i, 0))
    vec = pl.BlockSpec((1, n), lambda i: (0, 0))
    return pl.pallas_call(
        body, name=name, grid=(nt,),
        in_specs=[blk, blk, vec], out_specs=[blk, vec],
        out_shape=[jax.ShapeDtypeStruct((t, n), F32), jax.ShapeDtypeStruct((1, n), F32)],
        scratch_shapes=[pltpu.VMEM((SUBLANES, n), F32)],
        compiler_params=_cparams(("arbitrary",)),
    )(dcum, f, b_f)


def _attn_fwd(start, qkv, ckt, gate, *, name):
    t = qkv.shape[0]
    f = gate.shape[1]
    npair = f // LANES
    tq = _tile(t, ATTN_TILE)
    nq = t // tq
    scale = 1.0 / math.sqrt(FOX_HEAD_DIM)
    nt_dims = (((1,), (1,)), ((), ()))

    def body(start_ref, q_ref, k_ref, v_ref, ck_ref, g_ref, o_ref, y_ref, l_ref):
        i = pl.program_id(1)
        first = start_ref[pl.program_id(0), i]
        lane = lax.broadcasted_iota(jnp.int32, (tq, LANES), 1)
        lo = lane < FOX_HEAD_DIM
        q2 = q_ref[...] * scale
        qs = (jnp.where(lo, q2, 0).astype(BF16), jnp.where(lo, 0, q2).astype(BF16))
        row = lax.broadcasted_iota(jnp.int32, (tq, tq), 0)
        col = lax.broadcasted_iota(jnp.int32, (tq, tq), 1)
        causal = col <= row

        def kv_step(j, carry, masked):
            off = pl.multiple_of(j * tq, tq)
            kj = k_ref[pl.ds(off, tq), :]
            vj = v_ref[pl.ds(off, tq), :]
            ck = ck_ref[:, pl.ds(off, tq)]
            new = []
            for h in range(2):
                m, l, acc = carry[h]
                s = lax.dot_general(qs[h], kj, nt_dims, preferred_element_type=F32) - ck[h:h + 1, :]
                if masked:
                    s = jnp.where(causal, s, NEG_INF)
                m_new = jnp.maximum(m, jnp.max(s, axis=-1, keepdims=True))
                alpha = jnp.exp(m - m_new)
                p = jnp.exp(s - m_new)
                l = alpha * l + jnp.sum(p, axis=-1, keepdims=True)
                acc = alpha * acc + jnp.dot(p.astype(BF16), vj, preferred_element_type=F32)
                new.append((m_new, l, acc))
            return tuple(new)

        init = tuple((jnp.full((tq, 1), NEG_INF, F32), jnp.zeros((tq, 1), F32), jnp.zeros((tq, LANES), F32))
                     for _ in range(2))
        carry = lax.fori_loop(first, i, lambda j, c: kv_step(j, c, False), init)
        (m0, l0, a0), (m1, l1, a1) = kv_step(i, carry, True)
        o = jnp.where(lo, a0 / l0, a1 / l1)
        o_ref[...] = o
        gate_v = g_ref[...]
        y_ref[...] = (o * (gate_v * _sigmoid(gate_v))).astype(y_ref.dtype)
        lse_t = jnp.transpose(jnp.where(lo, m0 + jnp.log(l0), m1 + jnp.log(l1)))
        l_ref[0:1, :] = lse_t[0:1, :]
        l_ref[1:2, :] = lse_t[FOX_HEAD_DIM:FOX_HEAD_DIM + 1, :]

    blk = lambda base: pl.BlockSpec((tq, LANES), lambda p, i, s: (i, base + p))
    full = lambda base: pl.BlockSpec((t, LANES), lambda p, i, s: (0, base + p))
    return pl.pallas_call(
        body, name=name,
        grid_spec=pltpu.PrefetchScalarGridSpec(
            num_scalar_prefetch=1, grid=(npair, nq),
            in_specs=[blk(0), full(npair), full(2 * npair), pl.BlockSpec((None, 2, t), lambda p, i, s: (p, 0, 0)),
                      blk(0)],
            out_specs=[blk(0), blk(0), pl.BlockSpec((None, 2, tq), lambda p, i, s: (p, 0, i))]),
        out_shape=[jax.ShapeDtypeStruct((t, f), F32), jax.ShapeDtypeStruct((t, f), BF16),
                   jax.ShapeDtypeStruct((npair, 2, t), F32)],
        compiler_params=_cparams(("parallel", "arbitrary")),
    )(start, qkv, qkv, qkv, ckt, gate)


def _attn_bwd(end, qkv, do, lt, dt, cke, *, name):
    t, f = do.shape
    npair = f // LANES
    tk = _tile(t, ATTN_TILE)
    nk = t // tk
    scale = 1.0 / math.sqrt(FOX_HEAD_DIM)
    nt_dims = (((1,), (1,)), ((), ()))
    tn_dims = (((0,), (0,)), ((), ()))

    def body(end_ref, k_ref, v_ref, q_ref, do_ref, l_ref, d_ref, ck_ref, dq_ref, dk_ref, dv_ref, dck_ref, dcq_ref):
        j = pl.program_id(1)
        last = end_ref[pl.program_id(0), j]

        @pl.when(j == 0)
        def _():
            dq_ref[...] = jnp.zeros_like(dq_ref)
            dcq_ref[...] = jnp.zeros_like(dcq_ref)

        lane = lax.broadcasted_iota(jnp.int32, (tk, LANES), 1)
        lo = lane < FOX_HEAD_DIM
        sel = (lo, jnp.logical_not(lo))
        kj = k_ref[...]
        vj = v_ref[...]
        km = tuple(jnp.where(sel[h], kj, 0).astype(BF16) for h in range(2))
        ckv = ck_ref[...]
        ckh = (ckv[:, 0:1], ckv[:, FOX_HEAD_DIM:FOX_HEAD_DIM + 1])
        row = lax.broadcasted_iota(jnp.int32, (tk, tk), 0)
        col = lax.broadcasted_iota(jnp.int32, (tk, tk), 1)
        causal = row <= col

        def q_step(i, carry, masked):
            dk_acc, dv_acc, dck = carry
            off = pl.multiple_of(i * tk, tk)
            qi = q_ref[pl.ds(off, tk), :]
            doi = do_ref[pl.ds(off, tk), :]
            lrow = l_ref[:, pl.ds(off, tk)]
            drow = d_ref[:, pl.ds(off, tk)]
            dq_add = jnp.zeros((tk, LANES), F32)
            new_dck = []
            for h in range(2):
                qm = jnp.where(sel[h], qi, 0).astype(BF16)
                dom = jnp.where(sel[h], doi, 0).astype(BF16)
                st = lax.dot_general(kj, qm, nt_dims, preferred_element_type=F32) * scale
                st = st - ckh[h] - lrow[h:h + 1, :]
                if masked:
                    st = jnp.where(causal, st, NEG_INF)
                pt = jnp.exp(st)
                dpt = lax.dot_general(vj, dom, nt_dims, preferred_element_type=F32)
                dst = pt * (dpt - drow[h:h + 1, :])
                ptb = pt.astype(BF16)
                dstb = dst.astype(BF16)
                dv_acc = dv_acc + jnp.dot(ptb, dom, preferred_element_type=F32)
                dk_acc = dk_acc + jnp.dot(dstb, qm, preferred_element_type=F32)
                dq_add = dq_add + lax.dot_general(dstb, km[h], tn_dims, preferred_element_type=F32)
                new_dck.append(dck[h] - jnp.sum(dst, axis=-1, keepdims=True))
                dcq_ref[h:h + 1, pl.ds(off, tk)] += jnp.sum(dst, axis=0, keepdims=True)
            dq_ref[pl.ds(off, tk), :] += dq_add * scale
            return dk_acc, dv_acc, tuple(new_dck)

        zero = jnp.zeros((tk, LANES), F32)
        carry = (zero, zero, (jnp.zeros((tk, 1), F32), jnp.zeros((tk, 1), F32)))
        carry = q_step(j, carry, True)
        dk_acc, dv_acc, dck = lax.fori_loop(j + 1, last, lambda i, c: q_step(i, c, False), carry)
        dk_ref[...] = (dk_acc * scale).astype(dk_ref.dtype)
        dv_ref[...] = dv_acc.astype(dv_ref.dtype)
        dck_t = jnp.transpose(jnp.where(lo, dck[0], dck[1]))
        dck_ref[0:1, :] = dck_t[0:1, :]
        dck_ref[1:2, :] = dck_t[FOX_HEAD_DIM:FOX_HEAD_DIM + 1, :]

    blk = lambda base: pl.BlockSpec((tk, LANES), lambda p, j, e: (j, base + p))
    full = lambda base: pl.BlockSpec((t, LANES), lambda p, j, e: (0, base + p))
    rows = pl.BlockSpec((None, 2, t), lambda p, j, e: (p, 0, 0))
    return pl.pallas_call(
        body, name=name,
        grid_spec=pltpu.PrefetchScalarGridSpec(
            num_scalar_prefetch=1, grid=(npair, nk),
            in_specs=[blk(npair), blk(2 * npair), full(0), full(0), rows, rows, blk(0)],
            out_specs=[full(0), blk(0), blk(0), pl.BlockSpec((None, 2, tk), lambda p, j, e: (p, 0, j)), rows]),
        out_shape=[jax.ShapeDtypeStruct((t, f), F32), jax.ShapeDtypeStruct((t, f), BF16),
                   jax.ShapeDtypeStruct((t, f), BF16), jax.ShapeDtypeStruct((npair, 2, t), F32),
                   jax.ShapeDtypeStruct((npair, 2, t), F32)],
        compiler_params=_cparams(("parallel", "arbitrary")),
    )(end, qkv, qkv, qkv, do, lt, dt, cke)


ATTN_TILE = 512
EXP_ZERO = -104.0
BOUND_SLACK = 1.02


def _attn_row_stats(qkv, *, name):
    t = qkv.shape[0]
    f = qkv.shape[1] // 3
    tt = _tile(t, 512)

    def body(q_ref, k_ref, s_ref):
        q = q_ref[...].astype(F32)
        k = k_ref[...].astype(F32)
        chan = lax.broadcasted_iota(jnp.int32, (f, LANES), 0) // FOX_HEAD_DIM
        lane = lax.broadcasted_iota(jnp.int32, (f, LANES), 1)
        acc = jnp.zeros((tt, LANES), F32)
        for off, val in ((0, q * q), (FOX_HEADS, q * k), (2 * FOX_HEADS, k * k)):
            pick = (chan == lane - off).astype(BF16)
            acc = acc + jnp.dot(val.astype(BF16), pick, preferred_element_type=F32)
        s_ref[...] = acc

    return pl.pallas_call(
        body, name=name, grid=(t // tt,),
        in_specs=[pl.BlockSpec((tt, f), lambda i: (i, 0)), pl.BlockSpec((tt, f), lambda i: (i, 1))],
        out_specs=pl.BlockSpec((tt, LANES), lambda i: (i, 0)),
        out_shape=jax.ShapeDtypeStruct((t, LANES), F32),
        compiler_params=_cparams(("parallel",)),
    )(qkv, qkv)


def _attn_skip_tables(stats, cum16, tile):
    t = stats.shape[0]
    nb = t // tile
    scale = 1.0 / math.sqrt(FOX_HEAD_DIM)
    qn = jnp.sqrt(stats[:, :FOX_HEADS]) * scale
    sii = stats[:, FOX_HEADS:2 * FOX_HEADS] * scale - cum16
    kmax = jnp.max(jnp.sqrt(stats[:, 2 * FOX_HEADS:3 * FOX_HEADS]), axis=0, keepdims=True)
    arow = qn * kmax * BOUND_SLACK - sii + 0.5 * BOUND_SLACK
    a_blk = jnp.max(arow.reshape(nb, tile, FOX_HEADS), axis=1)
    c_blk = -cum16.reshape(nb, tile, FOX_HEADS)[:, tile - 1, :]
    dead = (a_blk[:, None, :] + c_blk[None, :, :]) < EXP_ZERO
    start_h = jnp.sum(dead.astype(jnp.int32), axis=1)
    blk = jnp.arange(nb, dtype=jnp.int32)
    start = jnp.minimum(jnp.min(start_h.reshape(nb, FOX_HEADS // 2, 2), axis=2), blk[:, None]).T
    needs = start[:, :, None] <= blk[None, None, :]
    end = jnp.max(jnp.where(needs, blk[None, :, None] + 1, 0), axis=1)
    return start, jnp.maximum(end, blk[None, :] + 1)


def _fox_post_bwd(dy, o, gate, *, name):
    t, f = dy.shape
    tt = _tile(t, 512)

    def body(dy_ref, o_ref, g_ref, do_ref, dg_ref, dl_ref):
        g = g_ref[...]
        sg = _sigmoid(g)
        dyv = dy_ref[...]
        ov = o_ref[...]
        do = dyv * (g * sg)
        do_ref[...] = do.astype(do_ref.dtype)
        dg_ref[...] = (dyv * ov * (sg * (1.0 + g * (1.0 - sg)))).astype(dg_ref.dtype)
        chan = lax.broadcasted_iota(jnp.int32, (f, LANES), 0)
        head = lax.broadcasted_iota(jnp.int32, (f, LANES), 1)
        pick = (chan // FOX_HEAD_DIM == head).astype(F32)
        dl_ref[...] = jnp.dot(do * ov, pick, precision=HIGHEST, preferred_element_type=F32)

    blk = pl.BlockSpec((tt, f), lambda i: (i, 0))
    return pl.pallas_call(
        body, name=name, grid=(t // tt,),
        in_specs=[blk, blk, blk], out_specs=[blk, blk, pl.BlockSpec((tt, LANES), lambda i: (i, 0))],
        out_shape=[jax.ShapeDtypeStruct((t, f), BF16), jax.ShapeDtypeStruct((t, f), BF16),
                   jax.ShapeDtypeStruct((t, LANES), F32)],
        compiler_params=_cparams(("parallel",)),
    )(dy, o, gate)


def _adamw(w, g, m, v, *, name):
    _, r, c = w.shape
    tr = _tile(r, 256) if r % SUBLANES == 0 else r
    c1 = 1.0 - ADAM_B1 ** ADAM_STEP
    c2 = 1.0 - ADAM_B2 ** ADAM_STEP

    def body(w_ref, g_ref, m_ref, v_ref, d_ref, mo_ref, vo_ref):
        gv = g_ref[...]
        mn = ADAM_B1 * m_ref[...] + (1.0 - ADAM_B1) * gv
        vn = ADAM_B2 * v_ref[...] + (1.0 - ADAM_B2) * (gv * gv)
        mo_ref[...] = mn
        vo_ref[...] = vn
        d_ref[...] = -ADAM_LR * ((mn / c1) / (jnp.sqrt(vn / c2) + ADAM_EPS) + ADAM_WD * w_ref[...])

    blk = pl.BlockSpec((None, tr, c), lambda i: (0, i, 0))
    return pl.pallas_call(
        body, name=name, grid=(r // tr,), in_specs=[blk] * 4, out_specs=[blk] * 3,
        out_shape=[jax.ShapeDtypeStruct((1, r, c), F32)] * 3,
        compiler_params=_cparams(("parallel",)),
    )(w, g, m, v)


def _sum_slots(land, *, name):
    ns, r, c = land.shape
    tr = _tile(r, 64) if r % SUBLANES == 0 else r

    def body(l_ref, o_ref):
        acc = l_ref[0].astype(F32)
        for s in range(1, ns):
            acc = acc + l_ref[s].astype(F32)
        o_ref[...] = acc

    return pl.pallas_call(
        body, name=name, grid=(r // tr,),
        in_specs=[pl.BlockSpec((ns, tr, c), lambda i: (0, i, 0))],
        out_specs=pl.BlockSpec((tr, c), lambda i: (i, 0)),
        out_shape=jax.ShapeDtypeStruct((r, c), F32),
        compiler_params=_cparams(("parallel",)),
    )(land)


ANY = pl.BlockSpec(memory_space=pl.ANY)


def _flip(v, bit):
    return 1 - v if bit else v


def _gather_chips(shards, small, *, name):
    n = len(shards)
    rels = ((1, 0), (0, 1), (1, 1))

    def body(*refs):
        ins, small_in = refs[:n], refs[n]
        outs, small_out = refs[n + 1:2 * n + 1], refs[2 * n + 1]
        send, recv, loc = refs[2 * n + 2:]
        x, y, c = lax.axis_index("x"), lax.axis_index("y"), lax.axis_index("c")
        me = 2 * x + y
        sibling = (x, y, 1 - c)
        local = [pltpu.make_async_copy(ins[k], outs[k].at[me], loc.at[k]) for k in range(n)]
        local.append(pltpu.make_async_copy(small_in, small_out.at[me], loc.at[n]))
        for cp in local:
            cp.start()

        def rows(k):
            half = ins[k].shape[0] // 2
            return pl.ds(pl.multiple_of(c * half, SUBLANES), half)

        sends = []
        for r, (rx, ry) in enumerate(rels):
            to = (_flip(x, rx), _flip(y, ry), c)
            for k in range(n):
                cp = pltpu.make_async_remote_copy(
                    src_ref=ins[k].at[rows(k), :], dst_ref=outs[k].at[me, rows(k), :],
                    send_sem=send.at[r * n + k], recv_sem=recv.at[r * n + k], device_id=to, device_id_type=MESH)
                cp.start()
                sends.append(cp)
            cp = pltpu.make_async_remote_copy(
                src_ref=small_in, dst_ref=small_out.at[me], send_sem=send.at[6 * n + r], recv_sem=recv.at[6 * n + r],
                device_id=to, device_id_type=MESH)
            cp.start()
            sends.append(cp)
        for r, (rx, ry) in enumerate(rels):
            src_chip = 2 * _flip(x, rx) + _flip(y, ry)
            for k in range(n):
                landed = outs[k].at[src_chip, rows(k), :]
                sends[r * (n + 1) + k].wait_recv()
                cp = pltpu.make_async_remote_copy(
                    src_ref=landed, dst_ref=landed, send_sem=send.at[3 * n + r * n + k],
                    recv_sem=recv.at[3 * n + r * n + k], device_id=sibling, device_id_type=MESH)
                cp.start()
                sends.append(cp)
            sends[r * (n + 1) + n].wait_recv()
        for cp in sends[:3 * (n + 1)]:
            cp.wait_send()
        for cp in sends[3 * (n + 1):]:
            cp.wait()
        for cp in local:
            cp.wait()

    vmem = pl.BlockSpec(memory_space=pltpu.VMEM)
    return pl.pallas_call(
        body, name=name, in_specs=[vmem] * (n + 1), out_specs=[vmem] * (n + 1),
        out_shape=[jax.ShapeDtypeStruct((N_CHIPS,) + s.shape, s.dtype) for s in list(shards) + [small]],
        scratch_shapes=[pltpu.SemaphoreType.DMA((6 * n + 3,)), pltpu.SemaphoreType.DMA((6 * n + 3,)),
                        pltpu.SemaphoreType.DMA((n + 1,))],
        compiler_params=pltpu.CompilerParams(has_side_effects=True, vmem_limit_bytes=VMEM_LIMIT),
    )(*shards, small)


_RELS7 = tuple((r >> 2 & 1, r >> 1 & 1, r & 1) for r in range(1, N_DEV))


def _scatter_pieces(grads, *, name):
    n = len(grads)

    def body(*refs):
        ins, outs = refs[:n], refs[n:2 * n]
        send, recv, loc = refs[2 * n:]
        x, y, c = lax.axis_index("x"), lax.axis_index("y"), lax.axis_index("c")
        me = 4 * x + 2 * y + c
        copies = []

        def piece(k, px, py, pc):
            half = ins[k].shape[1] // 2
            return ins[k].at[2 * px + py, pl.ds(pc * half, half), :]

        for k in range(n):
            cp = pltpu.make_async_copy(piece(k, x, y, c), outs[k].at[me], loc.at[k])
            cp.start()
            copies.append(cp)
        for r, (rx, ry, rc) in enumerate(_RELS7):
            tx, ty, tc = _flip(x, rx), _flip(y, ry), _flip(c, rc)
            for k in range(n):
                cp = pltpu.make_async_remote_copy(
                    src_ref=piece(k, tx, ty, tc), dst_ref=outs[k].at[me], send_sem=send.at[r * n + k],
                    recv_sem=recv.at[r * n + k], device_id=(tx, ty, tc), device_id_type=MESH)
                cp.start()
                copies.append(cp)
        for cp in copies:
            cp.wait()

    return pl.pallas_call(
        body, name=name, in_specs=[ANY] * n, out_specs=[ANY] * n,
        out_shape=[jax.ShapeDtypeStruct((N_DEV, g.shape[1] // 2, g.shape[2]), g.dtype) for g in grads],
        scratch_shapes=[pltpu.SemaphoreType.DMA((7 * n,)), pltpu.SemaphoreType.DMA((7 * n,)),
                        pltpu.SemaphoreType.DMA((n,))],
        compiler_params=pltpu.CompilerParams(has_side_effects=True),
    )(*grads)


def _join_cores(halves, *, name):
    n = len(halves)

    def body(*refs):
        ins, outs = refs[:n], refs[n:2 * n]
        send, recv, loc = refs[2 * n:]
        x, y, c = lax.axis_index("x"), lax.axis_index("y"), lax.axis_index("c")
        copies = []
        for k in range(n):
            half = ins[k].shape[0]
            mine = outs[k].at[0, pl.ds(c * half, half), :]
            cp = pltpu.make_async_copy(ins[k], mine, loc.at[k])
            cp.start()
            copies.append(cp)
            cp = pltpu.make_async_remote_copy(
                src_ref=ins[k], dst_ref=mine, send_sem=send.at[k], recv_sem=recv.at[k],
                device_id=(x, y, 1 - c), device_id_type=MESH)
            cp.start()
            copies.append(cp)
        for cp in copies:
            cp.wait()

    in_vmem = pl.BlockSpec(memory_space=pltpu.VMEM)
    return pl.pallas_call(
        body, name=name, in_specs=[in_vmem] * n, out_specs=[in_vmem] * n,
        out_shape=[jax.ShapeDtypeStruct((1, 2 * h.shape[0], h.shape[1]), h.dtype) for h in halves],
        scratch_shapes=[pltpu.SemaphoreType.DMA((n,)), pltpu.SemaphoreType.DMA((n,)), pltpu.SemaphoreType.DMA((n,))],
        compiler_params=pltpu.CompilerParams(has_side_effects=True, vmem_limit_bytes=VMEM_LIMIT),
    )(*halves)


def _allreduce_small(buf, *, name):
    r, n = buf.shape
    half = r // 2
    rels = ((1, 0), (0, 1), (1, 1))

    def body(in_ref, out_ref, sib_ref, chips_ref, send, recv):
        x, y, c = lax.axis_index("x"), lax.axis_index("y"), lax.axis_index("c")
        sibling = (x, y, 1 - c)
        chip = 2 * x + y
        rows = pl.ds(pl.multiple_of(c * half, SUBLANES), half)

        swap = pltpu.make_async_remote_copy(src_ref=in_ref, dst_ref=sib_ref, send_sem=send.at[0], recv_sem=recv.at[0],
                                            device_id=sibling, device_id_type=MESH)
        swap.start()
        swap.wait()
        chips_ref[chip] = in_ref[rows, :] + sib_ref[rows, :]

        sends = []
        for k, (rx, ry) in enumerate(rels):
            cp = pltpu.make_async_remote_copy(
                src_ref=chips_ref.at[chip], dst_ref=chips_ref.at[chip], send_sem=send.at[1 + k],
                recv_sem=recv.at[1 + k], device_id=(_flip(x, rx), _flip(y, ry), c), device_id_type=MESH)
            cp.start()
            sends.append(cp)
        for cp in sends:
            cp.wait()
        total = chips_ref[0]
        for s in range(1, N_CHIPS):
            total = total + chips_ref[s]
        out_ref[rows, :] = total

        back = pltpu.make_async_remote_copy(src_ref=out_ref.at[rows, :], dst_ref=out_ref.at[rows, :],
                                            send_sem=send.at[4], recv_sem=recv.at[4],
                                            device_id=sibling, device_id_type=MESH)
        back.start()
        back.wait()

    vmem = pl.BlockSpec(memory_space=pltpu.VMEM)
    return pl.pallas_call(
        body, name=name, in_specs=[vmem], out_specs=vmem,
        out_shape=jax.ShapeDtypeStruct((r, n), F32),
        scratch_shapes=[pltpu.VMEM((r, n), F32), pltpu.VMEM((N_CHIPS, half, n), F32),
                        pltpu.SemaphoreType.DMA((5,)), pltpu.SemaphoreType.DMA((5,))],
        compiler_params=pltpu.CompilerParams(has_side_effects=True, vmem_limit_bytes=VMEM_LIMIT),
    )(buf)


def _pack(arrs):
    flat = []
    for a in arrs:
        v = a.reshape(-1)
        pad = (-v.shape[0]) % LANES
        if pad:
            v = jnp.pad(v, (0, pad))
        flat.append(v)
    v = jnp.concatenate(flat)
    pad = (-v.shape[0]) % (LANES * SUBLANES)
    if pad:
        v = jnp.pad(v, (0, pad))
    return v.reshape(-1, LANES)


def _unpack(buf, shapes):
    v = buf.reshape(-1)
    out, off = [], 0
    for s in shapes:
        n = math.prod(s)
        out.append(v[off:off + n].reshape(s))
        off += n + (-n) % LANES
    return out


def kernel(x, norm_g, final_g, lru_w_in, lru_conv_w, lru_conv_b, lru_wa, lru_ba, lru_wx, lru_bx, lru_a_param, lru_w_out, fox_w_in, fox_b_f, fox_w_out, loss_target, m_norm_g, m_final_g, m_lru_w_in, m_lru_conv_w, m_lru_conv_b, m_lru_wa, m_lru_ba, m_lru_wx, m_lru_bx, m_lru_a_param, m_lru_w_out, m_fox_w_in, m_fox_b_f, m_fox_w_out, v_norm_g, v_final_g, v_lru_w_in, v_lru_conv_w, v_lru_conv_b, v_lru_wa, v_lru_ba, v_lru_wx, v_lru_bx, v_lru_a_param, v_lru_w_out, v_fox_w_in, v_fox_b_f, v_fox_w_out):
    t, d = x.shape[1], x.shape[2]
    w = lru_wa.shape[1] * LRU_BLOCK_W
    f = FOX_HEADS * FOX_HEAD_DIM
    npair = f // LANES
    x0 = x.reshape(t, d)
    tgt = loss_target.reshape(t, d)
    chip = 2 * lax.axis_index("x") + lax.axis_index("y")

    g_lwi, g_lwo, g_fwi, g_fwo, g_cw = _gather_chips(
        [lru_w_in[0].astype(BF16), lru_w_out[0].astype(BF16), fox_w_in[0].astype(BF16), fox_w_out[0].astype(BF16)],
        lru_conv_w[0], name="gather_weights")
    cg = w // 2
    lwi = jnp.concatenate([g_lwi[0], g_lwi[2], g_lwi[1], g_lwi[3]], axis=1)
    lwo = g_lwo.reshape(w, d)
    fwi = jnp.concatenate([g_fwi[s] for s in range(N_CHIPS)], axis=1)
    w_qkv, w_g2 = fwi[:, :3 * f], fwi[:, 3 * f:4 * f]
    w_f = jnp.pad(fwi[:, 4 * f:], ((0, 0), (0, LANES - FOX_HEADS)))
    fwo = g_fwo.reshape(f, d)
    conv_w = jnp.concatenate([g_cw[s] for s in range(N_CHIPS)], axis=1)
    conv_b, ba, bx, a_param = lru_conv_b, lru_ba, lru_bx, lru_a_param
    wa, wx = lru_wa[0], lru_wx[0]
    b_f = jnp.pad(fox_b_f, ((0, 0), (0, LANES - FOX_HEADS)))

    h0 = _rmsnorm(x0, norm_g[0], name="norm0")
    u = _matmul(h0, lwi, name="lru_in")
    y1, hs = _lru_fwd(u, conv_w, conv_b, wa, ba, wx, bx, a_param, cg=cg, name="lru_fwd")
    x1 = _matmul(y1, lwo, add=x0, name="lru_out")
    h1 = _rmsnorm(x1, norm_g[1], name="norm1")
    qkv = _matmul(h1, w_qkv, out_dtype=BF16, name="fox_qkv")
    gate2 = _matmul(h1, w_g2, name="fox_gate")
    flog = _matmul(h1, w_f, name="fox_f")
    cum = _fgate_fwd(flog, b_f, name="fgate_fwd")
    cum16 = cum[:, :FOX_HEADS]
    ckt = cum16.T.reshape(npair, 2, t)
    cke = jnp.repeat(cum16, FOX_HEAD_DIM, axis=1)
    a_start, a_end = _attn_skip_tables(_attn_row_stats(qkv, name="attn_row_stats"), cum16, _tile(t, ATTN_TILE))
    o, y2, lse = _attn_fwd(a_start, qkv, ckt, gate2, name="attn_fwd")
    x2 = _matmul(y2, fwo, add=x1, name="fox_out")
    lsum, dx2, dgf = _final_loss(x2, tgt, final_g, name="final_loss")
    loss = lax.psum(0.5 * jnp.sum(lsum) / d, ("x", "y", "c"))

    d_fwo = _matmul(y2, dx2, ta=True, out_dtype=BF16, name="d_fox_w_out")
    dy2 = _matmul(dx2, fwo, tb=True, name="d_y2")
    do, dgate2, dl = _fox_post_bwd(dy2, o, gate2, name="fox_post_bwd")
    lt = lse
    dt = dl[:, :FOX_HEADS].T.reshape(npair, 2, t)
    dq, dk, dv, dck, dcq = _attn_bwd(a_end, qkv, do, lt, dt, cke, name="attn_bwd")
    dcum = jnp.pad((dck + dcq).reshape(FOX_HEADS, t).T, ((0, 0), (0, LANES - FOX_HEADS)))
    dflog, db_f = _fgate_bwd(dcum, flog, b_f, name="fgate_bwd")
    du2 = jnp.concatenate([dq.astype(BF16), dk, dv, dgate2], axis=1)
    dflog_b = dflog.astype(BF16)
    dh1 = _matmul(du2, fwi[:, :4 * f], tb=True, name="d_h1_a")
    dh1 = _matmul(dflog_b, w_f, tb=True, add=dh1, name="d_h1_b")
    d_fwi_a = _matmul(h1, du2, ta=True, out_dtype=BF16, name="d_fox_w_in_a")
    d_fwi_b = _matmul(h1, dflog_b, ta=True, out_dtype=BF16, name="d_fox_w_in_b")
    d_fwi = jnp.concatenate([d_fwi_a, d_fwi_b[:, :FOX_HEADS]], axis=1)
    dx1, dg1 = _rmsnorm_bwd(dh1, x1, norm_g[1], dx2, name="norm1_bwd")

    d_lwo = _matmul(y1, dx1, ta=True, out_dtype=BF16, name="d_lru_w_out")
    dy1 = _matmul(dx1, lwo, tb=True, name="d_y1")
    dxb, dgate, d_cw, d_cb, d_wa, d_ba, d_wx, d_bx, d_ap = _lru_bwd(
        u, hs, dy1, conv_w, conv_b, wa, ba, wx, bx, a_param, cg=cg, name="lru_bwd")
    du = jnp.concatenate([dxb[:, :cg], dgate[:, :cg], dxb[:, cg:], dgate[:, cg:]], axis=1)
    dh0 = _matmul(du, lwi, tb=True, name="d_h0")
    d_lwi_p = _matmul(h0, du, ta=True, out_dtype=BF16, name="d_lru_w_in")
    dx0, dg0 = _rmsnorm_bwd(dh0, x0, norm_g[0], dx1, name="norm0_bwd")

    csz = cg
    g_lwi4 = jnp.stack([d_lwi_p[:, 0:csz], d_lwi_p[:, 2 * csz:3 * csz], d_lwi_p[:, csz:2 * csz],
                        d_lwi_p[:, 3 * csz:]])
    n_fwi = fox_w_in.shape[2]
    g_fwi4 = jnp.stack([d_fwi[:, s * n_fwi:(s + 1) * n_fwi] for s in range(N_CHIPS)])
    g_lwo4 = d_lwo.reshape(N_CHIPS, w // N_CHIPS, d)
    g_fwo4 = d_fwo.reshape(N_CHIPS, f // N_CHIPS, d)
    lands = _scatter_pieces([g_lwi4, g_lwo4, g_fwi4, g_fwo4], name="scatter_grads")
    halves = [_sum_slots(l, name="sum_" + nm) for l, nm in zip(lands, ("lru_w_in", "lru_w_out", "fox_w_in", "fox_w_out"))]
    big_g = _join_cores(halves, name="join_cores")

    small_g = [jnp.concatenate([dg0, dg1], axis=0), dgf.reshape(d), d_cw, d_cb, d_wa, d_ba, d_wx, d_bx, d_ap,
               db_f[:, :FOX_HEADS]]
    gsum = _allreduce_small(_pack(small_g), name="allreduce_small")
    zc = jnp.zeros((CONV_WIDTH, w), F32)
    pk_w = _pack([norm_g, final_g, zc, lru_conv_b, lru_wa, lru_ba, lru_wx, lru_bx, lru_a_param, fox_b_f])
    pk_m = _pack([m_norm_g, m_final_g, zc, m_lru_conv_b, m_lru_wa, m_lru_ba, m_lru_wx, m_lru_bx, m_lru_a_param,
                  m_fox_b_f])
    pk_v = _pack([v_norm_g, v_final_g, zc + 1.0, v_lru_conv_b, v_lru_wa, v_lru_ba, v_lru_wx, v_lru_bx,
                  v_lru_a_param, v_fox_b_f])
    s_delta, s_m, s_v = _adamw(pk_w[None], gsum[None], pk_m[None], pk_v[None], name="adamw_small")
    out_shapes = [norm_g.shape, final_g.shape, (CONV_WIDTH, w), lru_conv_b.shape, lru_wa.shape, lru_ba.shape,
                  lru_wx.shape, lru_bx.shape, lru_a_param.shape, fox_b_f.shape]
    sg = _unpack(gsum, out_shapes)
    sd = _unpack(s_delta, out_shapes)
    sm = _unpack(s_m, out_shapes)
    sv = _unpack(s_v, out_shapes)

    ncw = lru_conv_w.shape[2]
    g_cw_loc = lax.dynamic_slice_in_dim(sg[2], chip * ncw, ncw, axis=1)
    g_cw_loc = g_cw_loc[None]
    cw_d, cw_m, cw_v = _adamw(lru_conv_w, g_cw_loc, m_lru_conv_w, v_lru_conv_w, name="adamw_conv_w")

    big = []
    for nm, wt, g, mm, vv in (("lru_w_in", lru_w_in, big_g[0], m_lru_w_in, v_lru_w_in),
                              ("lru_w_out", lru_w_out, big_g[1], m_lru_w_out, v_lru_w_out),
                              ("fox_w_in", fox_w_in, big_g[2], m_fox_w_in, v_fox_w_in),
                              ("fox_w_out", fox_w_out, big_g[3], m_fox_w_out, v_fox_w_out)):
        big.append((g,) + tuple(_adamw(wt, g, mm, vv, name="adamw_" + nm)))

    def assemble(idx):
        small = (sg, sd, sm, sv)[idx]
        cw = (g_cw_loc, cw_d, cw_m, cw_v)[idx]
        return [small[0], small[1], big[0][idx], cw, small[3], small[4], small[5], small[6], small[7], small[8],
                big[1][idx], big[2][idx], small[9], big[3][idx]]

    grad_x = dx0.reshape(1, t, d)
    return (loss, grad_x, *assemble(0), *assemble(1), *assemble(2), *assemble(3))
```

```python
import functools
import math

import jax
import jax.numpy as jnp
from jax import lax
from jax.experimental import pallas as pl
from jax.experimental.pallas import tpu as pltpu

F32 = jnp.float32
BF16 = jnp.bfloat16

EPS = 1e-6
LRU_C = 8.0
LRU_BLOCK_W = 128
CONV_WIDTH = 4
FOX_HEADS = 16
FOX_HEAD_DIM = 64
NEG_INF = -1e30
ADAM_LR = 0.001
ADAM_B1 = 0.9
ADAM_B2 = 0.999
ADAM_EPS = 1e-08
ADAM_WD = 0.01
ADAM_STEP = 10

LANES = 128
SUBLANES = 8
VMEM_LIMIT = 56 * 1024 * 1024
TINY = 1e-30
N_CHIPS = 4
N_DEV = 8
MESH = pl.DeviceIdType.MESH
HIGHEST = lax.Precision.HIGHEST


def _tile(n, pref):
    t = min(n, pref)
    while n % t:
        t //= 2
    return t


def _cparams(dims=None):
    return pltpu.CompilerParams(dimension_semantics=dims, vmem_limit_bytes=VMEM_LIMIT)


def _sigmoid(x):
    return 0.5 * jnp.tanh(0.5 * x) + 0.5


def _log1p(x):
    u = 1.0 + x
    return jnp.where(u == 1.0, x, jnp.log(u) * x / (u - 1.0))


def _softplus(x):
    return jnp.maximum(x, 0.0) + _log1p(jnp.exp(-jnp.abs(x)))


MM_TILE = 1024
MM_FULL_K = 1536


def _matmul(a, b, *, name, ta=False, tb=False, out_dtype=F32, add=None, tm=MM_TILE, tn=MM_TILE, tk=None):
    if ta:
        kdim, m = a.shape
    else:
        m, kdim = a.shape
    if tb:
        n, kb = b.shape
    else:
        kb, n = b.shape
    assert kdim == kb, (a.shape, b.shape, ta, tb)
    if tk is None:
        tk = kdim if kdim <= MM_FULL_K else MM_TILE
    tm, tn, tk = _tile(m, tm), _tile(n, tn), _tile(kdim, tk)
    nk = kdim // tk
    dn = (((0 if ta else 1,), (1 if tb else 0,)), ((), ()))
    has_add = add is not None

    def body(*refs):
        if has_add:
            a_ref, b_ref, add_ref, o_ref = refs[:4]
        else:
            a_ref, b_ref, o_ref = refs[:3]
        part = lax.dot_general(a_ref[...].astype(BF16), b_ref[...].astype(BF16), dn, preferred_element_type=F32)

        def finish(r):
            if has_add:
                r = r + add_ref[...].astype(F32)
            o_ref[...] = r.astype(o_ref.dtype)

        if nk == 1:
            finish(part)
            return
        acc_ref = refs[-1]
        k = pl.program_id(2)

        @pl.when(k == 0)
        def _():
            acc_ref[...] = part

        @pl.when(k > 0)
        def _():
            acc_ref[...] += part

        @pl.when(k == nk - 1)
        def _():
            finish(acc_ref[...])

    a_spec = pl.BlockSpec((tk, tm), lambda i, j, k: (k, i)) if ta else pl.BlockSpec((tm, tk), lambda i, j, k: (i, k))
    b_spec = pl.BlockSpec((tn, tk), lambda i, j, k: (j, k)) if tb else pl.BlockSpec((tk, tn), lambda i, j, k: (k, j))
    o_spec = pl.BlockSpec((tm, tn), lambda i, j, k: (i, j))
    in_specs = [a_spec, b_spec] + ([o_spec] if has_add else [])
    args = (a, b) + ((add,) if has_add else ())
    return pl.pallas_call(
        body, name=name, grid=(m // tm, n // tn, nk), in_specs=in_specs, out_specs=o_spec,
        out_shape=jax.ShapeDtypeStruct((m, n), out_dtype),
        scratch_shapes=[pltpu.VMEM((tm, tn), F32)] if nk > 1 else [],
        compiler_params=_cparams(("parallel", "parallel", "arbitrary")),
    )(*args)


def _matmul_kparts(parts, b, *, chunk, name, tm=MM_TILE, tn=MM_TILE):
    npart = len(parts)
    m = parts[0].shape[0]
    n, kdim = b.shape
    nk = kdim // chunk
    assert nk * chunk == kdim and sum(p.shape[1] for p in parts) == kdim and nk % npart == 0
    tm, tn = _tile(m, tm), _tile(n, tn)
    dn = (((1,), (1,)), ((), ()))

    def body(*refs):
        a_refs, b_ref, o_ref, acc_ref = refs[:npart], refs[npart], refs[npart + 1], refs[npart + 2]
        k = pl.program_id(2)

        @pl.when(k == 0)
        def _():
            acc_ref[...] = jnp.zeros_like(acc_ref)

        for s in range(npart):
            @pl.when(lax.rem(k, npart) == s)
            def _(s=s):
                acc_ref[...] += lax.dot_general(a_refs[s][...].astype(BF16), b_ref[...].astype(BF16), dn,
                                                preferred_element_type=F32)

        @pl.when(k == nk - 1)
        def _():
            o_ref[...] = acc_ref[...].astype(o_ref.dtype)

    a_specs = [pl.BlockSpec((tm, chunk), lambda i, j, k: (i, k // npart)) for _ in range(npart)]
    return pl.pallas_call(
        body, name=name, grid=(m // tm, n // tn, nk),
        in_specs=a_specs + [pl.BlockSpec((tn, chunk), lambda i, j, k: (j, k))],
        out_specs=pl.BlockSpec((tm, tn), lambda i, j, k: (i, j)),
        out_shape=jax.ShapeDtypeStruct((m, n), F32),
        scratch_shapes=[pltpu.VMEM((tm, tn), F32)],
        compiler_params=_cparams(("parallel", "parallel", "arbitrary")),
    )(*parts, b)


def _matmul_nparts(a, parts, *, chunk, out_dtype, name, tm=MM_TILE, tk=MM_TILE):
    npart = len(parts)
    t, m = a.shape
    n = sum(p.shape[1] for p in parts)
    nj = n // chunk
    assert nj * chunk == n and nj % npart == 0
    tm, tk = _tile(m, tm), _tile(t, tk)
    nk = t // tk
    dn = (((0,), (0,)), ((), ()))

    def body(*refs):
        a_ref, b_refs, o_ref, acc_ref = refs[0], refs[1:1 + npart], refs[1 + npart], refs[2 + npart]
        j, k = pl.program_id(1), pl.program_id(2)

        @pl.when(k == 0)
        def _():
            acc_ref[...] = jnp.zeros_like(acc_ref)

        for s in range(npart):
            @pl.when(lax.rem(j, npart) == s)
            def _(s=s):
                acc_ref[...] += lax.dot_general(a_ref[...].astype(BF16), b_refs[s][...].astype(BF16), dn,
                                                preferred_element_type=F32)

        @pl.when(k == nk - 1)
        def _():
            o_ref[...] = acc_ref[...].astype(o_ref.dtype)

    def b_spec(s):
        return pl.BlockSpec((tk, chunk), lambda i, j, k: (jnp.where(lax.rem(j, npart) == s, k, 0), j // npart))

    return pl.pallas_call(
        body, name=name, grid=(m // tm, nj, nk),
        in_specs=[pl.BlockSpec((tk, tm), lambda i, j, k: (k, i))] + [b_spec(s) for s in range(npart)],
        out_specs=pl.BlockSpec((tm, chunk), lambda i, j, k: (i, j)),
        out_shape=jax.ShapeDtypeStruct((m, n), out_dtype),
        scratch_shapes=[pltpu.VMEM((tm, chunk), F32)],
        compiler_params=_cparams(("parallel", "parallel", "arbitrary")),
    )(a, *parts)


def _rmsnorm(x, g, *, name):
    t, d = x.shape
    tt = _tile(t, 512)

    def body(x_ref, g_ref, o_ref):
        xf = x_ref[...]
        rstd = lax.rsqrt(jnp.mean(xf * xf, axis=-1, keepdims=True) + EPS)
        o_ref[...] = (xf * rstd * g_ref[...]).astype(o_ref.dtype)

    return pl.pallas_call(
        body, name=name, grid=(t // tt,),
        in_specs=[pl.BlockSpec((tt, d), lambda i: (i, 0)), pl.BlockSpec((1, d), lambda i: (0, 0))],
        out_specs=pl.BlockSpec((tt, d), lambda i: (i, 0)),
        out_shape=jax.ShapeDtypeStruct((t, d), BF16),
        compiler_params=_cparams(("parallel",)),
    )(x, g.reshape(1, d))


def _rmsnorm_bwd(dh, x, g, dres, *, name):
    t, d = x.shape
    tt = _tile(t, 512)

    def body(dh_ref, x_ref, g_ref, dres_ref, dx_ref, dg_ref):
        i = pl.program_id(0)

        @pl.when(i == 0)
        def _():
            dg_ref[...] = jnp.zeros_like(dg_ref)

        xf = x_ref[...]
        rstd = lax.rsqrt(jnp.mean(xf * xf, axis=-1, keepdims=True) + EPS)
        xhat = xf * rstd
        dhf = dh_ref[...].astype(F32)
        dxhat = dhf * g_ref[...]
        mt = jnp.mean(dxhat * xhat, axis=-1, keepdims=True)
        dx_ref[...] = dres_ref[...] + rstd * (dxhat - xhat * mt)
        dg_ref[...] += jnp.sum(dhf * xhat, axis=0, keepdims=True)

    blk = pl.BlockSpec((tt, d), lambda i: (i, 0))
    vec = pl.BlockSpec((1, d), lambda i: (0, 0))
    return pl.pallas_call(
        body, name=name, grid=(t // tt,),
        in_specs=[blk, blk, vec, blk], out_specs=[blk, vec],
        out_shape=[jax.ShapeDtypeStruct((t, d), F32), jax.ShapeDtypeStruct((1, d), F32)],
        compiler_params=_cparams(("arbitrary",)),
    )(dh, x, g.reshape(1, d), dres)


def _final_loss(x2, tgt, g, *, name):
    t, d = x2.shape
    tt = _tile(t, 512)

    def body(x_ref, t_ref, g_ref, l_ref, dx_ref, dg_ref):
        i = pl.program_id(0)

        @pl.when(i == 0)
        def _():
            dg_ref[...] = jnp.zeros_like(dg_ref)
            l_ref[...] = jnp.zeros_like(l_ref)

        xf = x_ref[...]
        gg = g_ref[...]
        rstd = lax.rsqrt(jnp.mean(xf * xf, axis=-1, keepdims=True) + EPS)
        xhat = xf * rstd
        err = xhat * gg - t_ref[...]
        l_ref[...] += jnp.sum(err * err, axis=0, keepdims=True)
        dy = err * (1.0 / d)
        dxhat = dy * gg
        mt = jnp.mean(dxhat * xhat, axis=-1, keepdims=True)
        dx_ref[...] = rstd * (dxhat - xhat * mt)
        dg_ref[...] += jnp.sum(dy * xhat, axis=0, keepdims=True)

    blk = pl.BlockSpec((tt, d), lambda i: (i, 0))
    vec = pl.BlockSpec((1, d), lambda i: (0, 0))
    return pl.pallas_call(
        body, name=name, grid=(t // tt,),
        in_specs=[blk, blk, vec], out_specs=[vec, blk, vec],
        out_shape=[jax.ShapeDtypeStruct((1, d), F32), jax.ShapeDtypeStruct((t, d), F32),
                   jax.ShapeDtypeStruct((1, d), F32)],
        compiler_params=_cparams(("arbitrary",)),
    )(x2, tgt, g.reshape(1, d))


def _shift_down(prev8, cur, s):
    ext = jnp.concatenate([prev8, cur], axis=0)
    if s == 0:
        return cur
    return pltpu.roll(ext, s, 0)[SUBLANES:, :]


def _shift_up(cur, next8, s):
    if s == 0:
        return cur
    n = cur.shape[0]
    ext = jnp.concatenate([cur, next8], axis=0)
    return pltpu.roll(ext, n + SUBLANES - s, 0)[:n, :]


def _lru_gates(xc, wa, ba, wx, bx, sp):
    xcb = xc.astype(BF16)
    r = _sigmoid(jnp.dot(xcb, wa, preferred_element_type=F32) + ba)
    ig = _sigmoid(jnp.dot(xcb, wx, preferred_element_type=F32) + bx)
    log_a = -LRU_C * r * sp
    a = jnp.exp(log_a)
    z = -jnp.tanh(log_a) * (a * a + 1.0)
    inv_mult = lax.rsqrt(jnp.maximum(z, TINY))
    return r, ig, a, z * inv_mult, inv_mult


def _lru_specs(tt, cg, n_groups, nt, reverse):
    ncol = cg // LANES
    if reverse:
        ti = lambda i: nt - 1 - i
    else:
        ti = lambda i: i
    hb = tt // SUBLANES
    cur = lambda col: pl.BlockSpec((tt, cg), lambda g, i: (ti(i), 2 * g + col))
    prev = lambda col: pl.BlockSpec((SUBLANES, cg), lambda g, i: (jnp.maximum(ti(i) * hb - 1, 0), 2 * g + col))
    chan = lambda rows: pl.BlockSpec((rows, cg), lambda g, i: (0, g))
    wblk = pl.BlockSpec((ncol, LRU_BLOCK_W, LRU_BLOCK_W), lambda g, i: (g, 0, 0))
    plain = pl.BlockSpec((tt, cg), lambda g, i: (ti(i), g))
    plain_prev = pl.BlockSpec((SUBLANES, cg), lambda g, i: (jnp.maximum(ti(i) * hb - 1, 0), g))
    return cur, prev, chan, wblk, plain, plain_prev


def _lru_fwd(u, conv_w, conv_b, wa, ba, wx, bx, a_param, *, cg, name):
    t, w2 = u.shape
    w = w2 // 2
    n_groups = w // cg
    ncol = cg // LANES
    tt = _tile(t, 256)
    nt = t // tt
    cur, prev, chan, wblk, plain, _ = _lru_specs(tt, cg, n_groups, nt, False)

    def body(xb_ref, xp_ref, gate_ref, cw_ref, cb_ref, wa_ref, ba_ref, wx_ref, bx_ref, ap_ref,
             y_ref, hs_ref, h_ref, a_s, b_s):
        i = pl.program_id(1)

        @pl.when(i == 0)
        def _():
            h_ref[...] = jnp.zeros_like(h_ref)

        keep = (i > 0).astype(F32)
        for n in range(ncol):
            sl = slice(n * LANES, (n + 1) * LANES)
            xb = xb_ref[:, sl]
            xp = xp_ref[:, sl] * keep
            xc = cb_ref[:, sl] + cw_ref[3:4, sl] * xb
            for s in range(1, CONV_WIDTH):
                xc = xc + cw_ref[3 - s:4 - s, sl] * _shift_down(xp, xb, s)
            sp = _softplus(-ap_ref[:, sl])
            _, ig, a, mult, _ = _lru_gates(xc, wa_ref[n].astype(BF16), ba_ref[:, sl],
                                           wx_ref[n].astype(BF16), bx_ref[:, sl], sp)
            a_s[:, sl] = a
            b_s[:, sl] = mult * (ig * xc)

        def step(g, h):
            base = pl.multiple_of(g * SUBLANES, SUBLANES)
            for r in range(SUBLANES):
                h = a_s[pl.ds(base + r, 1), :] * h + b_s[pl.ds(base + r, 1), :]
                hs_ref[pl.ds(base + r, 1), :] = h
            return h

        h = lax.fori_loop(0, tt // SUBLANES, step, h_ref[0:1, :])
        h_ref[0:1, :] = h
        gate = gate_ref[...]
        y_ref[...] = (hs_ref[...] * (gate * _sigmoid(gate))).astype(y_ref.dtype)

    return pl.pallas_call(
        body, name=name, grid=(n_groups, nt),
        in_specs=[cur(0), prev(0), cur(1), chan(CONV_WIDTH), chan(1), wblk, chan(1), wblk, chan(1), chan(1)],
        out_specs=[plain, plain],
        out_shape=[jax.ShapeDtypeStruct((t, w), BF16), jax.ShapeDtypeStruct((t, w), F32)],
        scratch_shapes=[pltpu.VMEM((SUBLANES, cg), F32), pltpu.VMEM((tt, cg), F32), pltpu.VMEM((tt, cg), F32)],
        compiler_params=_cparams(("parallel", "arbitrary")),
    )(u, u, u, conv_w, conv_b, wa, ba, wx, bx, a_param)


def _lru_bwd(u, hs, dy, conv_w, conv_b, wa, ba, wx, bx, a_param, *, cg, name):
    t, w2 = u.shape
    w = w2 // 2
    n_groups = w // cg
    ncol = cg // LANES
    tt = _tile(t, 256)
    nt = t // tt
    cur, prev, chan, wblk, plain, plain_prev = _lru_specs(tt, cg, n_groups, nt, True)
    tn_dims = (((0,), (0,)), ((), ()))
    nt_dims = (((1,), (1,)), ((), ()))

    def body(xb_ref, xp_ref, gate_ref, hs_ref, hp_ref, dy_ref, cw_ref, cb_ref, wa_ref, ba_ref, wx_ref, bx_ref,
             ap_ref, dxb_ref, dgate_ref, dcw_ref, dcb_ref, dwa_ref, dba_ref, dwx_ref, dbx_ref, dsp_ref,
             c_ref, nx_ref, a_s, dhs_s, lam_s):
        i = pl.program_id(1)
        first_time_block = i == nt - 1

        @pl.when(i == 0)
        def _():
            c_ref[...] = jnp.zeros_like(c_ref)
            nx_ref[...] = jnp.zeros_like(nx_ref)
            for r in (dcw_ref, dcb_ref, dwa_ref, dba_ref, dwx_ref, dbx_ref, dsp_ref):
                r[...] = jnp.zeros_like(r)

        keep = jnp.where(first_time_block, 0.0, 1.0).astype(F32)
        gate = gate_ref[...]
        sg = _sigmoid(gate)
        dyv = dy_ref[...]
        hsv = hs_ref[...]
        dhs_s[...] = dyv * (gate * sg)
        dgate_ref[...] = (dyv * hsv * (sg * (1.0 + gate * (1.0 - sg)))).astype(dgate_ref.dtype)

        saved = []
        for n in range(ncol):
            sl = slice(n * LANES, (n + 1) * LANES)
            xb = xb_ref[:, sl]
            xp = xp_ref[:, sl] * keep
            shifted = [xb] + [_shift_down(xp, xb, s) for s in range(1, CONV_WIDTH)]
            xc = cb_ref[:, sl] + cw_ref[3:4, sl] * xb
            for s in range(1, CONV_WIDTH):
                xc = xc + cw_ref[3 - s:4 - s, sl] * shifted[s]
            sp = _softplus(-ap_ref[:, sl])
            wab = wa_ref[n].astype(BF16)
            wxb = wx_ref[n].astype(BF16)
            r, ig, a, mult, inv_mult = _lru_gates(xc, wab, ba_ref[:, sl], wxb, bx_ref[:, sl], sp)
            a_s[:, sl] = a
            saved.append((sl, shifted, xc, sp, wab, wxb, r, ig, a, mult, inv_mult))

        def step(g, c):
            base = pl.multiple_of(tt - SUBLANES - g * SUBLANES, SUBLANES)
            for r in range(SUBLANES - 1, -1, -1):
                lam = dhs_s[pl.ds(base + r, 1), :] + c
                lam_s[pl.ds(base + r, 1), :] = lam
                c = a_s[pl.ds(base + r, 1), :] * lam
            return c

        c_ref[0:1, :] = lax.fori_loop(0, tt // SUBLANES, step, c_ref[0:1, :])

        for n in range(ncol):
            sl, shifted, xc, sp, wab, wxb, r, ig, a, mult, inv_mult = saved[n]
            lam = lam_s[:, sl]
            hprev = _shift_down(hp_ref[:, sl] * keep, hs_ref[:, sl], 1)
            da = lam * hprev
            dmult = lam * (ig * xc)
            dlog_a = da * a - dmult * (a * a * inv_mult)
            di = lam * (mult * xc)
            dxc = lam * (mult * ig)
            dr = dlog_a * (-LRU_C * sp)
            dsp_ref[:, sl] += jnp.sum(dlog_a * (-LRU_C * r), axis=0, keepdims=True)
            dza = dr * (r * (1.0 - r))
            dzx = di * (ig * (1.0 - ig))
            dba_ref[:, sl] += jnp.sum(dza, axis=0, keepdims=True)
            dbx_ref[:, sl] += jnp.sum(dzx, axis=0, keepdims=True)
            xcb = xc.astype(BF16)
            dzab = dza.astype(BF16)
            dzxb = dzx.astype(BF16)
            dwa_ref[n] += lax.dot_general(xcb, dzab, tn_dims, preferred_element_type=F32)
            dwx_ref[n] += lax.dot_general(xcb, dzxb, tn_dims, preferred_element_type=F32)
            dxc = dxc + lax.dot_general(dzab, wab, nt_dims, preferred_element_type=F32)
            dxc = dxc + lax.dot_general(dzxb, wxb, nt_dims, preferred_element_type=F32)
            dcb_ref[:, sl] += jnp.sum(dxc, axis=0, keepdims=True)
            for s in range(CONV_WIDTH):
                dcw_ref[3 - s:4 - s, sl] += jnp.sum(dxc * shifted[s], axis=0, keepdims=True)
            nx = nx_ref[:, sl]
            dxb = cw_ref[3:4, sl] * dxc
            for s in range(1, CONV_WIDTH):
                dxb = dxb + cw_ref[3 - s:4 - s, sl] * _shift_up(dxc, nx, s)
            dxb_ref[:, sl] = dxb.astype(dxb_ref.dtype)
            nx_ref[:, sl] = dxc[0:SUBLANES, :]

        @pl.when(first_time_block)
        def _():
            dsp_ref[...] = dsp_ref[...] * (-_sigmoid(-ap_ref[...]))

    dxb_spec = pl.BlockSpec((tt, cg), lambda g, i: (nt - 1 - i, g))
    outs = pl.pallas_call(
        body, name=name, grid=(n_groups, nt),
        in_specs=[cur(0), prev(0), cur(1), plain, plain_prev, plain, chan(CONV_WIDTH), chan(1), wblk, chan(1), wblk,
                  chan(1), chan(1)],
        out_specs=[dxb_spec, dxb_spec, chan(CONV_WIDTH), chan(1), wblk, chan(1), wblk, chan(1), chan(1)],
        out_shape=[jax.ShapeDtypeStruct((t, w), BF16), jax.ShapeDtypeStruct((t, w), BF16),
                   jax.ShapeDtypeStruct(conv_w.shape, F32), jax.ShapeDtypeStruct(conv_b.shape, F32),
                   jax.ShapeDtypeStruct(wa.shape, F32), jax.ShapeDtypeStruct(ba.shape, F32),
                   jax.ShapeDtypeStruct(wx.shape, F32), jax.ShapeDtypeStruct(bx.shape, F32),
                   jax.ShapeDtypeStruct(a_param.shape, F32)],
        scratch_shapes=[pltpu.VMEM((SUBLANES, cg), F32), pltpu.VMEM((SUBLANES, cg), F32),
                        pltpu.VMEM((tt, cg), F32), pltpu.VMEM((tt, cg), F32), pltpu.VMEM((tt, cg), F32)],
        compiler_params=_cparams(("parallel", "arbitrary")),
    )(u, u, u, hs, hs, dy, conv_w, conv_b, wa, ba, wx, bx, a_param)
    return outs


def _fgate_fwd(f, b_f, *, name):
    t, n = f.shape
    tt = _tile(t, 256)
    width = FOX_HEADS * FOX_HEAD_DIM

    def body(f_ref, b_ref, cum_ref, wide_ref, carry_ref):
        i = pl.program_id(0)

        @pl.when(i == 0)
        def _():
            carry_ref[...] = jnp.zeros_like(carry_ref)

        z = f_ref[...] + b_ref[...]
        lf = jnp.minimum(z, 0.0) - _log1p(jnp.exp(-jnp.abs(z)))
        row = lax.broadcasted_iota(jnp.int32, (tt, tt), 0)
        col = lax.broadcasted_iota(jnp.int32, (tt, tt), 1)
        tri = (col <= row).astype(F32)
        cum = jnp.dot(tri, lf, precision=HIGHEST, preferred_element_type=F32) + carry_ref[0:1, :]
        cum_ref[...] = cum
        carry_ref[0:1, :] = cum[tt - 1:tt, :]
        head = lax.broadcasted_iota(jnp.int32, (n, width), 0)
        chan = lax.broadcasted_iota(jnp.int32, (n, width), 1) // FOX_HEAD_DIM
        wide_ref[...] = jnp.dot(cum, (head == chan).astype(F32), precision=HIGHEST, preferred_element_type=F32)

    return pl.pallas_call(
        body, name=name, grid=(t // tt,),
        in_specs=[pl.BlockSpec((tt, n), lambda i: (i, 0)), pl.BlockSpec((1, n), lambda i: (0, 0))],
        out_specs=[pl.BlockSpec((tt, n), lambda i: (i, 0)), pl.BlockSpec((tt, width), lambda i: (i, 0))],
        out_shape=[jax.ShapeDtypeStruct((t, n), F32), jax.ShapeDtypeStruct((t, width), F32)],
        scratch_shapes=[pltpu.VMEM((SUBLANES, n), F32)],
        compiler_params=_cparams(("arbitrary",)),
    )(f, b_f)


def _fgate_bwd(dcum, f, b_f, *, name):
    t, n = f.shape
    tt = _tile(t, 256)
    nt = t // tt

    def body(dc_ref, f_ref, b_ref, df_ref, db_ref, carry_ref):
        i = pl.program_id(0)

        @pl.when(i == 0)
        def _():
            carry_ref[...] = jnp.zeros_like(carry_ref)
            db_ref[...] = jnp.zeros_like(db_ref)

        row = lax.broadcasted_iota(jnp.int32, (tt, tt), 0)
        col = lax.broadcasted_iota(jnp.int32, (tt, tt), 1)
        triu = (col >= row).astype(F32)
        dlf = jnp.dot(triu, dc_ref[...], precision=HIGHEST, preferred_element_type=F32) + carry_ref[0:1, :]
        carry_ref[0:1, :] = dlf[0:1, :]
        z = f_ref[...] + b_ref[...]
        df = dlf * _sigmoid(-z)
        df_ref[...] = df
        db_ref[...] += jnp.sum(df, axis=0, keepdims=True)

    blk = pl.BlockSpec((tt, n), lambda i: (nt - 1 - i, 0))
    vec = pl.BlockSpec((1, n), lambda i: (0, 0))
    return pl.pallas_call(
        body, name=name, grid=(nt,),
        in_specs=[blk, blk, vec], out_specs=[blk, vec],
        out_shape=[jax.ShapeDtypeStruct((t, n), F32), jax.ShapeDtypeStruct((1, n), F32)],
        scratch_shapes=[pltpu.VMEM((SUBLANES, n), F32)],
        compiler_params=_cparams(("arbitrary",)),
    )(dcum, f, b_f)


def _attn_fwd(start, qkv, ckt, gate, *, name, tq, tk):
    t = qkv.shape[0]
    f = gate.shape[1]
    npair = f // LANES
    nq = t // tq
    ratio = tq // tk
    assert tq == ratio * tk and t == nq * tq
    scale = 1.0 / math.sqrt(FOX_HEAD_DIM)
    nt_dims = (((1,), (1,)), ((), ()))

    def body(start_ref, q_ref, k_ref, v_ref, ck_ref, g_ref, o_ref, y_ref, l_ref):
        i = pl.program_id(1)
        first = start_ref[pl.program_id(0), i]
        lane = lax.broadcasted_iota(jnp.int32, (tq, LANES), 1)
        lo = lane < FOX_HEAD_DIM
        q2 = q_ref[...] * scale
        qs = (jnp.where(lo, q2, 0).astype(BF16), jnp.where(lo, 0, q2).astype(BF16))
        row = lax.broadcasted_iota(jnp.int32, (tq, tk), 0)
        col = lax.broadcasted_iota(jnp.int32, (tq, tk), 1)

        def kv_step(j, carry, diag):
            off = pl.multiple_of(j * tk, tk)
            kj = k_ref[pl.ds(off, tk), :]
            vj = v_ref[pl.ds(off, tk), :]
            ck = ck_ref[:, pl.ds(off, tk)]
            new = []
            for h in range(2):
                m, l, acc = carry[h]
                s = lax.dot_general(qs[h], kj, nt_dims, preferred_element_type=F32) - ck[h:h + 1, :]
                if diag is not None:
                    s = jnp.where(col + diag * tk <= row, s, NEG_INF)
                m_new = jnp.maximum(m, jnp.max(s, axis=-1, keepdims=True))
                alpha = jnp.exp(m - m_new)
                p = jnp.exp(s - m_new)
                l = alpha * l + jnp.sum(p, axis=-1, keepdims=True)
                acc = alpha * acc + jnp.dot(p.astype(BF16), vj, preferred_element_type=F32)
                new.append((m_new, l, acc))
            return tuple(new)

        init = tuple((jnp.full((tq, 1), NEG_INF, F32), jnp.zeros((tq, 1), F32), jnp.zeros((tq, LANES), F32))
                     for _ in range(2))
        carry = lax.fori_loop(first, i * ratio, lambda j, c: kv_step(j, c, None), init)
        for d in range(ratio):
            carry = kv_step(i * ratio + d, carry, d)
        (m0, l0, a0), (m1, l1, a1) = carry
        o = jnp.where(lo, a0 / l0, a1 / l1)
        o_ref[...] = o
        gate_v = g_ref[...]
        y_ref[...] = (o * (gate_v * _sigmoid(gate_v))).astype(y_ref.dtype)
        lse_t = jnp.transpose(jnp.where(lo, m0 + jnp.log(l0), m1 + jnp.log(l1)))
        l_ref[0:1, :] = lse_t[0:1, :]
        l_ref[1:2, :] = lse_t[FOX_HEAD_DIM:FOX_HEAD_DIM + 1, :]

    blk = lambda base: pl.BlockSpec((tq, LANES), lambda p, i, s: (i, base + p))
    full = lambda base: pl.BlockSpec((t, LANES), lambda p, i, s: (0, base + p))
    return pl.pallas_call(
        body, name=name,
        grid_spec=pltpu.PrefetchScalarGridSpec(
            num_scalar_prefetch=1, grid=(npair, nq),
            in_specs=[blk(0), full(npair), full(2 * npair), pl.BlockSpec((None, 2, t), lambda p, i, s: (p, 0, 0)),
                      blk(0)],
            out_specs=[blk(0), blk(0), pl.BlockSpec((None, 2, tq), lambda p, i, s: (p, 0, i))]),
        out_shape=[jax.ShapeDtypeStruct((t, f), F32), jax.ShapeDtypeStruct((t, f), BF16),
                   jax.ShapeDtypeStruct((npair, 2, t), F32)],
        compiler_params=_cparams(("parallel", "arbitrary")),
    )(start, qkv, qkv, qkv, ckt, gate)


def _attn_bwd(end, qkv, do, lt, dt, cke, *, name):
    t, f = do.shape
    npair = f // LANES
    tk = _tile(t, ATTN_TILE)
    nk = t // tk
    scale = 1.0 / math.sqrt(FOX_HEAD_DIM)
    nt_dims = (((1,), (1,)), ((), ()))
    tn_dims = (((0,), (0,)), ((), ()))

    def body(end_ref, k_ref, v_ref, q_ref, do_ref, l_ref, d_ref, ck_ref, dq_out_ref, dk_ref, dv_ref, dck_ref, dcq_ref,
             dq_ref):
        j = pl.program_id(1)
        last = end_ref[pl.program_id(0), j]

        @pl.when(j == 0)
        def _():
            dq_ref[...] = jnp.zeros_like(dq_ref)
            dcq_ref[...] = jnp.zeros_like(dcq_ref)

        lane = lax.broadcasted_iota(jnp.int32, (tk, LANES), 1)
        lo = lane < FOX_HEAD_DIM
        sel = (lo, jnp.logical_not(lo))
        kj = k_ref[...]
        vj = v_ref[...]
        km = tuple(jnp.where(sel[h], kj, 0).astype(BF16) for h in range(2))
        ckv = ck_ref[...]
        ckh = (ckv[:, 0:1], ckv[:, FOX_HEAD_DIM:FOX_HEAD_DIM + 1])
        row = lax.broadcasted_iota(jnp.int32, (tk, tk), 0)
        col = lax.broadcasted_iota(jnp.int32, (tk, tk), 1)
        causal = row <= col

        def q_step(i, carry, masked):
            dk_acc, dv_acc, dck = carry
            off = pl.multiple_of(i * tk, tk)
            qi = q_ref[pl.ds(off, tk), :]
            doi = do_ref[pl.ds(off, tk), :]
            lrow = l_ref[:, pl.ds(off, tk)]
            drow = d_ref[:, pl.ds(off, tk)]
            dq_add = jnp.zeros((tk, LANES), F32)
            new_dck = []
            for h in range(2):
                qm = jnp.where(sel[h], qi, 0).astype(BF16)
                dom = jnp.where(sel[h], doi, 0).astype(BF16)
                st = lax.dot_general(kj, qm, nt_dims, preferred_element_type=F32) * scale
                st = st - ckh[h] - lrow[h:h + 1, :]
                if masked:
                    st = jnp.where(causal, st, NEG_INF)
                pt = jnp.exp(st)
                dpt = lax.dot_general(vj, dom, nt_dims, preferred_element_type=F32)
                dst = pt * (dpt - drow[h:h + 1, :])
                ptb = pt.astype(BF16)
                dstb = dst.astype(BF16)
                dv_acc = dv_acc + jnp.dot(ptb, dom, preferred_element_type=F32)
                dk_acc = dk_acc + jnp.dot(dstb, qm, preferred_element_type=F32)
                dq_add = dq_add + lax.dot_general(dstb, km[h], tn_dims, preferred_element_type=F32)
                new_dck.append(dck[h] - jnp.sum(dst, axis=-1, keepdims=True))
                dcq_ref[h:h + 1, pl.ds(off, tk)] += jnp.sum(dst, axis=0, keepdims=True)
            dq_ref[pl.ds(off, tk), :] += dq_add * scale
            return dk_acc, dv_acc, tuple(new_dck)

        zero = jnp.zeros((tk, LANES), F32)
        carry = (zero, zero, (jnp.zeros((tk, 1), F32), jnp.zeros((tk, 1), F32)))
        carry = q_step(j, carry, True)
        dk_acc, dv_acc, dck = lax.fori_loop(j + 1, last, lambda i, c: q_step(i, c, False), carry)
        dk_ref[...] = (dk_acc * scale).astype(dk_ref.dtype)
        dv_ref[...] = dv_acc.astype(dv_ref.dtype)
        dck_t = jnp.transpose(jnp.where(lo, dck[0], dck[1]))
        dck_ref[0:1, :] = dck_t[0:1, :]
        dck_ref[1:2, :] = dck_t[FOX_HEAD_DIM:FOX_HEAD_DIM + 1, :]

        @pl.when(j == nk - 1)
        def _():
            dq_out_ref[...] = dq_ref[...].astype(dq_out_ref.dtype)

    blk = lambda base: pl.BlockSpec((tk, LANES), lambda p, j, e: (j, base + p))
    full = lambda base: pl.BlockSpec((t, LANES), lambda p, j, e: (0, base + p))
    rows = pl.BlockSpec((None, 2, t), lambda p, j, e: (p, 0, 0))
    return pl.pallas_call(
        body, name=name,
        grid_spec=pltpu.PrefetchScalarGridSpec(
            num_scalar_prefetch=1, grid=(npair, nk),
            in_specs=[blk(npair), blk(2 * npair), full(0), full(0), rows, rows, blk(0)],
            out_specs=[full(0), blk(0), blk(0), pl.BlockSpec((None, 2, tk), lambda p, j, e: (p, 0, j)), rows],
            scratch_shapes=[pltpu.VMEM((t, LANES), F32)]),
        out_shape=[jax.ShapeDtypeStruct((t, f), BF16), jax.ShapeDtypeStruct((t, f), BF16),
                   jax.ShapeDtypeStruct((t, f), BF16), jax.ShapeDtypeStruct((npair, 2, t), F32),
                   jax.ShapeDtypeStruct((npair, 2, t), F32)],
        compiler_params=_cparams(("parallel", "arbitrary")),
    )(end, qkv, qkv, qkv, do, lt, dt, cke)


ATTN_TILE = 512
ATTN_FWD_QUERIES = 512
EXP_ZERO = -104.0
BOUND_SLACK = 1.02


def _attn_row_stats(qkv, *, name):
    t = qkv.shape[0]
    f = qkv.shape[1] // 3
    tt = _tile(t, 512)

    def body(q_ref, k_ref, s_ref):
        q = q_ref[...].astype(F32)
        k = k_ref[...].astype(F32)
        chan = lax.broadcasted_iota(jnp.int32, (f, LANES), 0) // FOX_HEAD_DIM
        lane = lax.broadcasted_iota(jnp.int32, (f, LANES), 1)
        acc = jnp.zeros((tt, LANES), F32)
        for off, val in ((0, q * q), (FOX_HEADS, q * k), (2 * FOX_HEADS, k * k)):
            pick = (chan == lane - off).astype(BF16)
            acc = acc + jnp.dot(val.astype(BF16), pick, preferred_element_type=F32)
        s_ref[...] = acc

    return pl.pallas_call(
        body, name=name, grid=(t // tt,),
        in_specs=[pl.BlockSpec((tt, f), lambda i: (i, 0)), pl.BlockSpec((tt, f), lambda i: (i, 1))],
        out_specs=pl.BlockSpec((tt, LANES), lambda i: (i, 0)),
        out_shape=jax.ShapeDtypeStruct((t, LANES), F32),
        compiler_params=_cparams(("parallel",)),
    )(qkv, qkv)


def _attn_skip_tables(stats, cum16, tile):
    t = stats.shape[0]
    nb = t // tile
    scale = 1.0 / math.sqrt(FOX_HEAD_DIM)
    qn = jnp.sqrt(stats[:, :FOX_HEADS]) * scale
    sii = stats[:, FOX_HEADS:2 * FOX_HEADS] * scale - cum16
    kmax = jnp.max(jnp.sqrt(stats[:, 2 * FOX_HEADS:3 * FOX_HEADS]), axis=0, keepdims=True)
    arow = qn * kmax * BOUND_SLACK - sii + 0.5 * BOUND_SLACK
    a_blk = jnp.max(arow.reshape(nb, tile, FOX_HEADS), axis=1)
    c_blk = -cum16.reshape(nb, tile, FOX_HEADS)[:, tile - 1, :]
    dead = (a_blk[:, None, :] + c_blk[None, :, :]) < EXP_ZERO
    start_h = jnp.sum(dead.astype(jnp.int32), axis=1)
    blk = jnp.arange(nb, dtype=jnp.int32)
    start = jnp.minimum(jnp.min(start_h.reshape(nb, FOX_HEADS // 2, 2), axis=2), blk[:, None]).T
    needs = start[:, :, None] <= blk[None, None, :]
    end = jnp.max(jnp.where(needs, blk[None, :, None] + 1, 0), axis=1)
    return start, jnp.maximum(end, blk[None, :] + 1)


def _fox_post_bwd(dy, o, gate, *, name):
    t, f = dy.shape
    tt = _tile(t, 512)

    def body(dy_ref, o_ref, g_ref, do_ref, dg_ref, dl_ref):
        g = g_ref[...]
        sg = _sigmoid(g)
        dyv = dy_ref[...]
        ov = o_ref[...]
        do = dyv * (g * sg)
        do_ref[...] = do.astype(do_ref.dtype)
        dg_ref[...] = (dyv * ov * (sg * (1.0 + g * (1.0 - sg)))).astype(dg_ref.dtype)
        chan = lax.broadcasted_iota(jnp.int32, (f, LANES), 0)
        head = lax.broadcasted_iota(jnp.int32, (f, LANES), 1)
        pick = (chan // FOX_HEAD_DIM == head).astype(F32)
        dl_ref[...] = jnp.dot(do * ov, pick, precision=HIGHEST, preferred_element_type=F32)

    blk = pl.BlockSpec((tt, f), lambda i: (i, 0))
    return pl.pallas_call(
        body, name=name, grid=(t // tt,),
        in_specs=[blk, blk, blk], out_specs=[blk, blk, pl.BlockSpec((tt, LANES), lambda i: (i, 0))],
        out_shape=[jax.ShapeDtypeStruct((t, f), BF16), jax.ShapeDtypeStruct((t, f), BF16),
                   jax.ShapeDtypeStruct((t, LANES), F32)],
        compiler_params=_cparams(("parallel",)),
    )(dy, o, gate)


def _adamw(w, g, m, v, *, name):
    _, r, c = w.shape
    tr = _tile(r, 256) if r % SUBLANES == 0 else r
    c1 = 1.0 - ADAM_B1 ** ADAM_STEP
    c2 = 1.0 - ADAM_B2 ** ADAM_STEP

    def body(w_ref, g_ref, m_ref, v_ref, d_ref, mo_ref, vo_ref):
        gv = g_ref[...]
        mn = ADAM_B1 * m_ref[...] + (1.0 - ADAM_B1) * gv
        vn = ADAM_B2 * v_ref[...] + (1.0 - ADAM_B2) * (gv * gv)
        mo_ref[...] = mn
        vo_ref[...] = vn
        d_ref[...] = -ADAM_LR * ((mn / c1) / (jnp.sqrt(vn / c2) + ADAM_EPS) + ADAM_WD * w_ref[...])

    blk = pl.BlockSpec((None, tr, c), lambda i: (0, i, 0))
    return pl.pallas_call(
        body, name=name, grid=(r // tr,), in_specs=[blk] * 4, out_specs=[blk] * 3,
        out_shape=[jax.ShapeDtypeStruct((1, r, c), F32)] * 3,
        compiler_params=_cparams(("parallel",)),
    )(w, g, m, v)


def _sum_slots(land, *, name):
    ns, r, c = land.shape
    tr = _tile(r, 64) if r % SUBLANES == 0 else r

    def body(l_ref, o_ref):
        acc = l_ref[0].astype(F32)
        for s in range(1, ns):
            acc = acc + l_ref[s].astype(F32)
        o_ref[...] = acc

    return pl.pallas_call(
        body, name=name, grid=(r // tr,),
        in_specs=[pl.BlockSpec((ns, tr, c), lambda i: (0, i, 0))],
        out_specs=pl.BlockSpec((tr, c), lambda i: (i, 0)),
        out_shape=jax.ShapeDtypeStruct((r, c), F32),
        compiler_params=_cparams(("parallel",)),
    )(land)


ANY = pl.BlockSpec(memory_space=pl.ANY)


def _flip(v, bit):
    return 1 - v if bit else v


def _gather_chips(shards, small, *, name):
    n = len(shards)
    rels = ((1, 0), (0, 1), (1, 1))

    def body(*refs):
        ins, small_in = refs[:n], refs[n]
        outs, small_out = refs[n + 1:2 * n + 1], refs[2 * n + 1]
        send, recv, loc = refs[2 * n + 2:]
        x, y, c = lax.axis_index("x"), lax.axis_index("y"), lax.axis_index("c")
        me = 2 * x + y
        sibling = (x, y, 1 - c)
        local = [pltpu.make_async_copy(ins[k], outs[k].at[me], loc.at[k]) for k in range(n)]
        local.append(pltpu.make_async_copy(small_in, small_out.at[me], loc.at[n]))
        for cp in local:
            cp.start()

        def rows(k):
            half = ins[k].shape[0] // 2
            return pl.ds(pl.multiple_of(c * half, SUBLANES), half)

        sends = []
        for r, (rx, ry) in enumerate(rels):
            to = (_flip(x, rx), _flip(y, ry), c)
            for k in range(n):
                cp = pltpu.make_async_remote_copy(
                    src_ref=ins[k].at[rows(k), :], dst_ref=outs[k].at[me, rows(k), :],
                    send_sem=send.at[r * n + k], recv_sem=recv.at[r * n + k], device_id=to, device_id_type=MESH)
                cp.start()
                sends.append(cp)
            cp = pltpu.make_async_remote_copy(
                src_ref=small_in, dst_ref=small_out.at[me], send_sem=send.at[6 * n + r], recv_sem=recv.at[6 * n + r],
                device_id=to, device_id_type=MESH)
            cp.start()
            sends.append(cp)
        for r, (rx, ry) in enumerate(rels):
            src_chip = 2 * _flip(x, rx) + _flip(y, ry)
            for k in range(n):
                landed = outs[k].at[src_chip, rows(k), :]
                sends[r * (n + 1) + k].wait_recv()
                cp = pltpu.make_async_remote_copy(
                    src_ref=landed, dst_ref=landed, send_sem=send.at[3 * n + r * n + k],
                    recv_sem=recv.at[3 * n + r * n + k], device_id=sibling, device_id_type=MESH)
                cp.start()
                sends.append(cp)
            sends[r * (n + 1) + n].wait_recv()
        for cp in sends[:3 * (n + 1)]:
            cp.wait_send()
        for cp in sends[3 * (n + 1):]:
            cp.wait()
        for cp in local:
            cp.wait()

    vmem = pl.BlockSpec(memory_space=pltpu.VMEM)
    return pl.pallas_call(
        body, name=name, in_specs=[vmem] * (n + 1), out_specs=[vmem] * (n + 1),
        out_shape=[jax.ShapeDtypeStruct((N_CHIPS,) + s.shape, s.dtype) for s in list(shards) + [small]],
        scratch_shapes=[pltpu.SemaphoreType.DMA((6 * n + 3,)), pltpu.SemaphoreType.DMA((6 * n + 3,)),
                        pltpu.SemaphoreType.DMA((n + 1,))],
        compiler_params=pltpu.CompilerParams(has_side_effects=True, vmem_limit_bytes=VMEM_LIMIT),
    )(*shards, small)


_RELS7 = tuple((r >> 2 & 1, r >> 1 & 1, r & 1) for r in range(1, N_DEV))


def _scatter_pieces(grads, *, name):
    n = len(grads)

    def body(*refs):
        ins, outs = refs[:n], refs[n:2 * n]
        send, recv, loc = refs[2 * n:]
        x, y, c = lax.axis_index("x"), lax.axis_index("y"), lax.axis_index("c")
        me = 4 * x + 2 * y + c
        copies = []

        def piece(k, px, py, pc):
            half = ins[k].shape[1] // 2
            return ins[k].at[2 * px + py, pl.ds(pc * half, half), :]

        for k in range(n):
            cp = pltpu.make_async_copy(piece(k, x, y, c), outs[k].at[me], loc.at[k])
            cp.start()
            copies.append(cp)
        for r, (rx, ry, rc) in enumerate(_RELS7):
            tx, ty, tc = _flip(x, rx), _flip(y, ry), _flip(c, rc)
            for k in range(n):
                cp = pltpu.make_async_remote_copy(
                    src_ref=piece(k, tx, ty, tc), dst_ref=outs[k].at[me], send_sem=send.at[r * n + k],
                    recv_sem=recv.at[r * n + k], device_id=(tx, ty, tc), device_id_type=MESH)
                cp.start()
                copies.append(cp)
        for cp in copies:
            cp.wait()

    return pl.pallas_call(
        body, name=name, in_specs=[ANY] * n, out_specs=[ANY] * n,
        out_shape=[jax.ShapeDtypeStruct((N_DEV, g.shape[1] // 2, g.shape[2]), g.dtype) for g in grads],
        scratch_shapes=[pltpu.SemaphoreType.DMA((7 * n,)), pltpu.SemaphoreType.DMA((7 * n,)),
                        pltpu.SemaphoreType.DMA((n,))],
        compiler_params=pltpu.CompilerParams(has_side_effects=True),
    )(*grads)


def _join_cores(halves, *, name):
    n = len(halves)

    def body(*refs):
        ins, outs = refs[:n], refs[n:2 * n]
        send, recv, loc = refs[2 * n:]
        x, y, c = lax.axis_index("x"), lax.axis_index("y"), lax.axis_index("c")
        copies = []
        for k in range(n):
            half = ins[k].shape[0]
            mine = outs[k].at[0, pl.ds(c * half, half), :]
            cp = pltpu.make_async_copy(ins[k], mine, loc.at[k])
            cp.start()
            copies.append(cp)
            cp = pltpu.make_async_remote_copy(
                src_ref=ins[k], dst_ref=mine, send_sem=send.at[k], recv_sem=recv.at[k],
                device_id=(x, y, 1 - c), device_id_type=MESH)
            cp.start()
            copies.append(cp)
        for cp in copies:
            cp.wait()

    in_vmem = pl.BlockSpec(memory_space=pltpu.VMEM)
    return pl.pallas_call(
        body, name=name, in_specs=[in_vmem] * n, out_specs=[in_vmem] * n,
        out_shape=[jax.ShapeDtypeStruct((1, 2 * h.shape[0], h.shape[1]), h.dtype) for h in halves],
        scratch_shapes=[pltpu.SemaphoreType.DMA((n,)), pltpu.SemaphoreType.DMA((n,)), pltpu.SemaphoreType.DMA((n,))],
        compiler_params=pltpu.CompilerParams(has_side_effects=True, vmem_limit_bytes=VMEM_LIMIT),
    )(*halves)


def _allreduce_small(buf, *, name):
    r, n = buf.shape
    half = r // 2
    rels = ((1, 0), (0, 1), (1, 1))

    def body(in_ref, out_ref, sib_ref, chips_ref, send, recv):
        x, y, c = lax.axis_index("x"), lax.axis_index("y"), lax.axis_index("c")
        sibling = (x, y, 1 - c)
        chip = 2 * x + y
        rows = pl.ds(pl.multiple_of(c * half, SUBLANES), half)

        swap = pltpu.make_async_remote_copy(src_ref=in_ref, dst_ref=sib_ref, send_sem=send.at[0], recv_sem=recv.at[0],
                                            device_id=sibling, device_id_type=MESH)
        swap.start()
        swap.wait()
        chips_ref[chip] = in_ref[rows, :] + sib_ref[rows, :]

        sends = []
        for k, (rx, ry) in enumerate(rels):
            cp = pltpu.make_async_remote_copy(
                src_ref=chips_ref.at[chip], dst_ref=chips_ref.at[chip], send_sem=send.at[1 + k],
                recv_sem=recv.at[1 + k], device_id=(_flip(x, rx), _flip(y, ry), c), device_id_type=MESH)
            cp.start()
            sends.append(cp)
        for cp in sends:
            cp.wait()
        total = chips_ref[0]
        for s in range(1, N_CHIPS):
            total = total + chips_ref[s]
        out_ref[rows, :] = total

        back = pltpu.make_async_remote_copy(src_ref=out_ref.at[rows, :], dst_ref=out_ref.at[rows, :],
                                            send_sem=send.at[4], recv_sem=recv.at[4],
                                            device_id=sibling, device_id_type=MESH)
        back.start()
        back.wait()

    vmem = pl.BlockSpec(memory_space=pltpu.VMEM)
    return pl.pallas_call(
        body, name=name, in_specs=[vmem], out_specs=vmem,
        out_shape=jax.ShapeDtypeStruct((r, n), F32),
        scratch_shapes=[pltpu.VMEM((r, n), F32), pltpu.VMEM((N_CHIPS, half, n), F32),
                        pltpu.SemaphoreType.DMA((5,)), pltpu.SemaphoreType.DMA((5,))],
        compiler_params=pltpu.CompilerParams(has_side_effects=True, vmem_limit_bytes=VMEM_LIMIT),
    )(buf)


def _pack(arrs):
    flat = []
    for a in arrs:
        v = a.reshape(-1)
        pad = (-v.shape[0]) % LANES
        if pad:
            v = jnp.pad(v, (0, pad))
        flat.append(v)
    v = jnp.concatenate(flat)
    pad = (-v.shape[0]) % (LANES * SUBLANES)
    if pad:
        v = jnp.pad(v, (0, pad))
    return v.reshape(-1, LANES)


def _unpack(buf, shapes):
    v = buf.reshape(-1)
    out, off = [], 0
    for s in shapes:
        n = math.prod(s)
        out.append(v[off:off + n].reshape(s))
        off += n + (-n) % LANES
    return out


def kernel(x, norm_g, final_g, lru_w_in, lru_conv_w, lru_conv_b, lru_wa, lru_ba, lru_wx, lru_bx, lru_a_param, lru_w_out, fox_w_in, fox_b_f, fox_w_out, loss_target, m_norm_g, m_final_g, m_lru_w_in, m_lru_conv_w, m_lru_conv_b, m_lru_wa, m_lru_ba, m_lru_wx, m_lru_bx, m_lru_a_param, m_lru_w_out, m_fox_w_in, m_fox_b_f, m_fox_w_out, v_norm_g, v_final_g, v_lru_w_in, v_lru_conv_w, v_lru_conv_b, v_lru_wa, v_lru_ba, v_lru_wx, v_lru_bx, v_lru_a_param, v_lru_w_out, v_fox_w_in, v_fox_b_f, v_fox_w_out):
    t, d = x.shape[1], x.shape[2]
    w = lru_wa.shape[1] * LRU_BLOCK_W
    f = FOX_HEADS * FOX_HEAD_DIM
    npair = f // LANES
    x0 = x.reshape(t, d)
    tgt = loss_target.reshape(t, d)
    chip = 2 * lax.axis_index("x") + lax.axis_index("y")

    g_lwi, g_lwo, g_fwi, g_fwo, g_cw = _gather_chips(
        [lru_w_in[0].astype(BF16), lru_w_out[0].astype(BF16), fox_w_in[0].astype(BF16), fox_w_out[0].astype(BF16)],
        lru_conv_w[0], name="gather_weights")
    cg = w // 2
    lwi = jnp.concatenate([g_lwi[0], g_lwi[2], g_lwi[1], g_lwi[3]], axis=1)
    lwo = g_lwo.reshape(w, d)
    fwi = jnp.concatenate([g_fwi[s] for s in range(N_CHIPS)], axis=1)
    w_qkv, w_g2 = fwi[:, :3 * f], fwi[:, 3 * f:4 * f]
    w_f = jnp.pad(fwi[:, 4 * f:], ((0, 0), (0, LANES - FOX_HEADS)))
    fwo = g_fwo.reshape(f, d)
    conv_w = jnp.concatenate([g_cw[s] for s in range(N_CHIPS)], axis=1)
    conv_b, ba, bx, a_param = lru_conv_b, lru_ba, lru_bx, lru_a_param
    wa, wx = lru_wa[0], lru_wx[0]
    b_f = jnp.pad(fox_b_f, ((0, 0), (0, LANES - FOX_HEADS)))

    h0 = _rmsnorm(x0, norm_g[0], name="norm0")
    u = _matmul(h0, lwi, name="lru_in")
    y1, hs = _lru_fwd(u, conv_w, conv_b, wa, ba, wx, bx, a_param, cg=cg, name="lru_fwd")
    x1 = _matmul(y1, lwo, add=x0, name="lru_out")
    h1 = _rmsnorm(x1, norm_g[1], name="norm1")
    qkv = _matmul(h1, w_qkv, out_dtype=BF16, name="fox_qkv")
    gate2 = _matmul(h1, w_g2, name="fox_gate")
    flog = _matmul(h1, w_f, name="fox_f")
    cum, cke = _fgate_fwd(flog, b_f, name="fgate_fwd")
    cum16 = cum[:, :FOX_HEADS]
    ckt = cum16.T.reshape(npair, 2, t)
    a_tk, a_tq = _tile(t, ATTN_TILE), _tile(t, ATTN_FWD_QUERIES)
    a_start, a_end = _attn_skip_tables(_attn_row_stats(qkv, name="attn_row_stats"), cum16, a_tk)
    a_start_fwd = jnp.min(a_start.reshape(npair, t // a_tq, a_tq // a_tk), axis=2)
    o, y2, lse = _attn_fwd(a_start_fwd, qkv, ckt, gate2, name="attn_fwd", tq=a_tq, tk=a_tk)
    x2 = _matmul(y2, fwo, add=x1, name="fox_out")
    lsum, dx2, dgf = _final_loss(x2, tgt, final_g, name="final_loss")
    loss = lax.psum(0.5 * jnp.sum(lsum) / d, ("x", "y", "c"))

    d_fwo = _matmul(y2, dx2, ta=True, out_dtype=BF16, name="d_fox_w_out")
    dy2 = _matmul(dx2, fwo, tb=True, name="d_y2")
    do, dgate2, dl = _fox_post_bwd(dy2, o, gate2, name="fox_post_bwd")
    lt = lse
    dt = dl[:, :FOX_HEADS].T.reshape(npair, 2, t)
    dq, dk, dv, dck, dcq = _attn_bwd(a_end, qkv, do, lt, dt, cke, name="attn_bwd")
    dcum = jnp.pad((dck + dcq).reshape(FOX_HEADS, t).T, ((0, 0), (0, LANES - FOX_HEADS)))
    dflog, db_f = _fgate_bwd(dcum, flog, b_f, name="fgate_bwd")
    du2 = [dq, dk, dv, dgate2]
    dflog_b = dflog.astype(BF16)
    dh1 = _matmul_kparts(du2, fwi[:, :4 * f], chunk=f, name="d_h1_a")
    dh1 = _matmul(dflog_b, w_f, tb=True, add=dh1, name="d_h1_b")
    d_fwi_a = _matmul_nparts(h1, du2, chunk=f, out_dtype=BF16, name="d_fox_w_in_a")
    d_fwi_b = _matmul(h1, dflog_b, ta=True, out_dtype=BF16, name="d_fox_w_in_b")
    d_fwi = jnp.concatenate([d_fwi_a, d_fwi_b[:, :FOX_HEADS]], axis=1)
    dx1, dg1 = _rmsnorm_bwd(dh1, x1, norm_g[1], dx2, name="norm1_bwd")

    d_lwo = _matmul(y1, dx1, ta=True, out_dtype=BF16, name="d_lru_w_out")
    dy1 = _matmul(dx1, lwo, tb=True, name="d_y1")
    dxb, dgate, d_cw, d_cb, d_wa, d_ba, d_wx, d_bx, d_ap = _lru_bwd(
        u, hs, dy1, conv_w, conv_b, wa, ba, wx, bx, a_param, cg=cg, name="lru_bwd")
    dh0 = _matmul_kparts([dxb, dgate], lwi, chunk=cg, name="d_h0")
    d_lwi_p = _matmul_nparts(h0, [dxb, dgate], chunk=cg, out_dtype=BF16, name="d_lru_w_in")
    dx0, dg0 = _rmsnorm_bwd(dh0, x0, norm_g[0], dx1, name="norm0_bwd")

    csz = cg
    g_lwi4 = jnp.stack([d_lwi_p[:, 0:csz], d_lwi_p[:, 2 * csz:3 * csz], d_lwi_p[:, csz:2 * csz],
                        d_lwi_p[:, 3 * csz:]])
    n_fwi = fox_w_in.shape[2]
    g_fwi4 = jnp.stack([d_fwi[:, s * n_fwi:(s + 1) * n_fwi] for s in range(N_CHIPS)])
    g_lwo4 = d_lwo.reshape(N_CHIPS, w // N_CHIPS, d)
    g_fwo4 = d_fwo.reshape(N_CHIPS, f // N_CHIPS, d)
    lands = _scatter_pieces([g_lwi4, g_lwo4, g_fwi4, g_fwo4], name="scatter_grads")
    halves = [_sum_slots(l, name="sum_" + nm) for l, nm in zip(lands, ("lru_w_in", "lru_w_out", "fox_w_in", "fox_w_out"))]
    big_g = _join_cores(halves, name="join_cores")

    small_g = [jnp.concatenate([dg0, dg1], axis=0), dgf.reshape(d), d_cw, d_cb, d_wa, d_ba, d_wx, d_bx, d_ap,
               db_f[:, :FOX_HEADS]]
    gsum = _allreduce_small(_pack(small_g), name="allreduce_small")
    zc = jnp.zeros((CONV_WIDTH, w), F32)
    pk_w = _pack([norm_g, final_g, zc, lru_conv_b, lru_wa, lru_ba, lru_wx, lru_bx, lru_a_param, fox_b_f])
    pk_m = _pack([m_norm_g, m_final_g, zc, m_lru_conv_b, m_lru_wa, m_lru_ba, m_lru_wx, m_lru_bx, m_lru_a_param,
                  m_fox_b_f])
    pk_v = _pack([v_norm_g, v_final_g, zc + 1.0, v_lru_conv_b, v_lru_wa, v_lru_ba, v_lru_wx, v_lru_bx,
                  v_lru_a_param, v_fox_b_f])
    s_delta, s_m, s_v = _adamw(pk_w[None], gsum[None], pk_m[None], pk_v[None], name="adamw_small")
    out_shapes = [norm_g.shape, final_g.shape, (CONV_WIDTH, w), lru_conv_b.shape, lru_wa.shape, lru_ba.shape,
                  lru_wx.shape, lru_bx.shape, lru_a_param.shape, fox_b_f.shape]
    sg = _unpack(gsum, out_shapes)
    sd = _unpack(s_delta, out_shapes)
    sm = _unpack(s_m, out_shapes)
    sv = _unpack(s_v, out_shapes)

    ncw = lru_conv_w.shape[2]
    g_cw_loc = lax.dynamic_slice_in_dim(sg[2], chip * ncw, ncw, axis=1)
    g_cw_loc = g_cw_loc[None]
    cw_d, cw_m, cw_v = _adamw(lru_conv_w, g_cw_loc, m_lru_conv_w, v_lru_conv_w, name="adamw_conv_w")

    big = []
    for nm, wt, g, mm, vv in (("lru_w_in", lru_w_in, big_g[0], m_lru_w_in, v_lru_w_in),
                              ("lru_w_out", lru_w_out, big_g[1], m_lru_w_out, v_lru_w_out),
                              ("fox_w_in", fox_w_in, big_g[2], m_fox_w_in, v_fox_w_in),
                              ("fox_w_out", fox_w_out, big_g[3], m_fox_w_out, v_fox_w_out)):
        big.append((g,) + tuple(_adamw(wt, g, mm, vv, name="adamw_" + nm)))

    def assemble(idx):
        small = (sg, sd, sm, sv)[idx]
        cw = (g_cw_loc, cw_d, cw_m, cw_v)[idx]
        return [small[0], small[1], big[0][idx], cw, small[3], small[4], small[5], small[6], small[7], small[8],
                big[1][idx], big[2][idx], small[9], big[3][idx]]

    grad_x = dx0.reshape(1, t, d)
    return (loss, grad_x, *assemble(0), *assemble(1), *assemble(2), *assemble(3))
```

```python
import functools
import math

import jax
import jax.numpy as jnp
from jax import lax
from jax.experimental import pallas as pl
from jax.experimental.pallas import tpu as pltpu

F32 = jnp.float32
BF16 = jnp.bfloat16

EPS = 1e-6
LRU_C = 8.0
LRU_BLOCK_W = 128
CONV_WIDTH = 4
FOX_HEADS = 16
FOX_HEAD_DIM = 64
NEG_INF = -1e30
ADAM_LR = 0.001
ADAM_B1 = 0.9
ADAM_B2 = 0.999
ADAM_EPS = 1e-08
ADAM_WD = 0.01
ADAM_STEP = 10

LANES = 128
SUBLANES = 8
VMEM_LIMIT = 56 * 1024 * 1024
TINY = 1e-30
N_CHIPS = 4
N_DEV = 8
MESH = pl.DeviceIdType.MESH
HIGHEST = lax.Precision.HIGHEST


def _tile(n, pref):
    t = min(n, pref)
    while n % t:
        t //= 2
    return t


def _cparams(dims=None):
    return pltpu.CompilerParams(dimension_semantics=dims, vmem_limit_bytes=VMEM_LIMIT)


def _sigmoid(x):
    return 0.5 * jnp.tanh(0.5 * x) + 0.5


def _log1p(x):
    u = 1.0 + x
    return jnp.where(u == 1.0, x, jnp.log(u) * x / (u - 1.0))


def _softplus(x):
    return jnp.maximum(x, 0.0) + _log1p(jnp.exp(-jnp.abs(x)))


MM_TILE = 1024
MM_FULL_K = 1536


def _matmul(a, b, *, name, ta=False, tb=False, out_dtype=F32, add=None, tm=MM_TILE, tn=MM_TILE, tk=None):
    if ta:
        kdim, m = a.shape
    else:
        m, kdim = a.shape
    if tb:
        n, kb = b.shape
    else:
        kb, n = b.shape
    assert kdim == kb, (a.shape, b.shape, ta, tb)
    if tk is None:
        tk = kdim if kdim <= MM_FULL_K else MM_TILE
    tm, tn, tk = _tile(m, tm), _tile(n, tn), _tile(kdim, tk)
    nk = kdim // tk
    dn = (((0 if ta else 1,), (1 if tb else 0,)), ((), ()))
    has_add = add is not None

    def body(*refs):
        if has_add:
            a_ref, b_ref, add_ref, o_ref = refs[:4]
        else:
            a_ref, b_ref, o_ref = refs[:3]
        part = lax.dot_general(a_ref[...].astype(BF16), b_ref[...].astype(BF16), dn, preferred_element_type=F32)

        def finish(r):
            if has_add:
                r = r + add_ref[...].astype(F32)
            o_ref[...] = r.astype(o_ref.dtype)

        if nk == 1:
            finish(part)
            return
        acc_ref = refs[-1]
        k = pl.program_id(2)

        @pl.when(k == 0)
        def _():
            acc_ref[...] = part

        @pl.when(k > 0)
        def _():
            acc_ref[...] += part

        @pl.when(k == nk - 1)
        def _():
            finish(acc_ref[...])

    a_spec = pl.BlockSpec((tk, tm), lambda i, j, k: (k, i)) if ta else pl.BlockSpec((tm, tk), lambda i, j, k: (i, k))
    b_spec = pl.BlockSpec((tn, tk), lambda i, j, k: (j, k)) if tb else pl.BlockSpec((tk, tn), lambda i, j, k: (k, j))
    o_spec = pl.BlockSpec((tm, tn), lambda i, j, k: (i, j))
    in_specs = [a_spec, b_spec] + ([o_spec] if has_add else [])
    args = (a, b) + ((add,) if has_add else ())
    return pl.pallas_call(
        body, name=name, grid=(m // tm, n // tn, nk), in_specs=in_specs, out_specs=o_spec,
        out_shape=jax.ShapeDtypeStruct((m, n), out_dtype),
        scratch_shapes=[pltpu.VMEM((tm, tn), F32)] if nk > 1 else [],
        compiler_params=_cparams(("parallel", "parallel", "arbitrary")),
    )(*args)


def _matmul_kparts(parts, b, *, chunk, name, tm=MM_TILE, tn=MM_TILE):
    npart = len(parts)
    m = parts[0].shape[0]
    n, kdim = b.shape
    nk = kdim // chunk
    assert nk * chunk == kdim and sum(p.shape[1] for p in parts) == kdim and nk % npart == 0
    tm, tn = _tile(m, tm), _tile(n, tn)
    dn = (((1,), (1,)), ((), ()))

    def body(*refs):
        a_refs, b_ref, o_ref, acc_ref = refs[:npart], refs[npart], refs[npart + 1], refs[npart + 2]
        k = pl.program_id(2)

        @pl.when(k == 0)
        def _():
            acc_ref[...] = jnp.zeros_like(acc_ref)

        for s in range(npart):
            @pl.when(lax.rem(k, npart) == s)
            def _(s=s):
                acc_ref[...] += lax.dot_general(a_refs[s][...].astype(BF16), b_ref[...].astype(BF16), dn,
                                                preferred_element_type=F32)

        @pl.when(k == nk - 1)
        def _():
            o_ref[...] = acc_ref[...].astype(o_ref.dtype)

    a_specs = [pl.BlockSpec((tm, chunk), lambda i, j, k: (i, k // npart)) for _ in range(npart)]
    return pl.pallas_call(
        body, name=name, grid=(m // tm, n // tn, nk),
        in_specs=a_specs + [pl.BlockSpec((tn, chunk), lambda i, j, k: (j, k))],
        out_specs=pl.BlockSpec((tm, tn), lambda i, j, k: (i, j)),
        out_shape=jax.ShapeDtypeStruct((m, n), F32),
        scratch_shapes=[pltpu.VMEM((tm, tn), F32)],
        compiler_params=_cparams(("parallel", "parallel", "arbitrary")),
    )(*parts, b)


def _matmul_nparts(a, parts, *, chunk, out_dtype, name, tm=MM_TILE, tk=MM_TILE):
    npart = len(parts)
    t, m = a.shape
    n = sum(p.shape[1] for p in parts)
    nj = n // chunk
    assert nj * chunk == n and nj % npart == 0
    tm, tk = _tile(m, tm), _tile(t, tk)
    nk = t // tk
    dn = (((0,), (0,)), ((), ()))

    def body(*refs):
        a_ref, b_refs, o_ref, acc_ref = refs[0], refs[1:1 + npart], refs[1 + npart], refs[2 + npart]
        j, k = pl.program_id(1), pl.program_id(2)

        @pl.when(k == 0)
        def _():
            acc_ref[...] = jnp.zeros_like(acc_ref)

        for s in range(npart):
            @pl.when(lax.rem(j, npart) == s)
            def _(s=s):
                acc_ref[...] += lax.dot_general(a_ref[...].astype(BF16), b_refs[s][...].astype(BF16), dn,
                                                preferred_element_type=F32)

        @pl.when(k == nk - 1)
        def _():
            o_ref[...] = acc_ref[...].astype(o_ref.dtype)

    def b_spec(s):
        return pl.BlockSpec((tk, chunk), lambda i, j, k: (jnp.where(lax.rem(j, npart) == s, k, 0), j // npart))

    return pl.pallas_call(
        body, name=name, grid=(m // tm, nj, nk),
        in_specs=[pl.BlockSpec((tk, tm), lambda i, j, k: (k, i))] + [b_spec(s) for s in range(npart)],
        out_specs=pl.BlockSpec((tm, chunk), lambda i, j, k: (i, j)),
        out_shape=jax.ShapeDtypeStruct((m, n), out_dtype),
        scratch_shapes=[pltpu.VMEM((tm, chunk), F32)],
        compiler_params=_cparams(("parallel", "parallel", "arbitrary")),
    )(a, *parts)


def _rmsnorm(x, g, *, name):
    t, d = x.shape
    tt = _tile(t, 512)

    def body(x_ref, g_ref, o_ref):
        xf = x_ref[...]
        rstd = lax.rsqrt(jnp.mean(xf * xf, axis=-1, keepdims=True) + EPS)
        o_ref[...] = (xf * rstd * g_ref[...]).astype(o_ref.dtype)

    return pl.pallas_call(
        body, name=name, grid=(t // tt,),
        in_specs=[pl.BlockSpec((tt, d), lambda i: (i, 0)), pl.BlockSpec((1, d), lambda i: (0, 0))],
        out_specs=pl.BlockSpec((tt, d), lambda i: (i, 0)),
        out_shape=jax.ShapeDtypeStruct((t, d), BF16),
        compiler_params=_cparams(("parallel",)),
    )(x, g.reshape(1, d))


def _rmsnorm_bwd(dh, x, g, dres, *, name):
    t, d = x.shape
    tt = _tile(t, 512)

    def body(dh_ref, x_ref, g_ref, dres_ref, dx_ref, dg_ref):
        i = pl.program_id(0)

        @pl.when(i == 0)
        def _():
            dg_ref[...] = jnp.zeros_like(dg_ref)

        xf = x_ref[...]
        rstd = lax.rsqrt(jnp.mean(xf * xf, axis=-1, keepdims=True) + EPS)
        xhat = xf * rstd
        dhf = dh_ref[...].astype(F32)
        dxhat = dhf * g_ref[...]
        mt = jnp.mean(dxhat * xhat, axis=-1, keepdims=True)
        dx_ref[...] = dres_ref[...] + rstd * (dxhat - xhat * mt)
        dg_ref[...] += jnp.sum(dhf * xhat, axis=0, keepdims=True)

    blk = pl.BlockSpec((tt, d), lambda i: (i, 0))
    vec = pl.BlockSpec((1, d), lambda i: (0, 0))
    return pl.pallas_call(
        body, name=name, grid=(t // tt,),
        in_specs=[blk, blk, vec, blk], out_specs=[blk, vec],
        out_shape=[jax.ShapeDtypeStruct((t, d), F32), jax.ShapeDtypeStruct((1, d), F32)],
        compiler_params=_cparams(("arbitrary",)),
    )(dh, x, g.reshape(1, d), dres)


def _final_loss(x2, tgt, g, *, name):
    t, d = x2.shape
    tt = _tile(t, 512)

    def body(x_ref, t_ref, g_ref, l_ref, dx_ref, dg_ref):
        i = pl.program_id(0)

        @pl.when(i == 0)
        def _():
            dg_ref[...] = jnp.zeros_like(dg_ref)
            l_ref[...] = jnp.zeros_like(l_ref)

        xf = x_ref[...]
        gg = g_ref[...]
        rstd = lax.rsqrt(jnp.mean(xf * xf, axis=-1, keepdims=True) + EPS)
        xhat = xf * rstd
        err = xhat * gg - t_ref[...]
        l_ref[...] += jnp.sum(err * err, axis=0, keepdims=True)
        dy = err * (1.0 / d)
        dxhat = dy * gg
        mt = jnp.mean(dxhat * xhat, axis=-1, keepdims=True)
        dx_ref[...] = rstd * (dxhat - xhat * mt)
        dg_ref[...] += jnp.sum(dy * xhat, axis=0, keepdims=True)

    blk = pl.BlockSpec((tt, d), lambda i: (i, 0))
    vec = pl.BlockSpec((1, d), lambda i: (0, 0))
    return pl.pallas_call(
        body, name=name, grid=(t // tt,),
        in_specs=[blk, blk, vec], out_specs=[vec, blk, vec],
        out_shape=[jax.ShapeDtypeStruct((1, d), F32), jax.ShapeDtypeStruct((t, d), F32),
                   jax.ShapeDtypeStruct((1, d), F32)],
        compiler_params=_cparams(("arbitrary",)),
    )(x2, tgt, g.reshape(1, d))


def _shift_down(prev8, cur, s):
    ext = jnp.concatenate([prev8, cur], axis=0)
    if s == 0:
        return cur
    return pltpu.roll(ext, s, 0)[SUBLANES:, :]


def _shift_up(cur, next8, s):
    if s == 0:
        return cur
    n = cur.shape[0]
    ext = jnp.concatenate([cur, next8], axis=0)
    return pltpu.roll(ext, n + SUBLANES - s, 0)[:n, :]


def _lru_gates(xc, wa, ba, wx, bx, sp):
    xcb = xc.astype(BF16)
    r = _sigmoid(jnp.dot(xcb, wa, preferred_element_type=F32) + ba)
    ig = _sigmoid(jnp.dot(xcb, wx, preferred_element_type=F32) + bx)
    log_a = -LRU_C * r * sp
    a = jnp.exp(log_a)
    z = -jnp.tanh(log_a) * (a * a + 1.0)
    inv_mult = lax.rsqrt(jnp.maximum(z, TINY))
    return r, ig, a, z * inv_mult, inv_mult


def _lru_specs(tt, cg, n_groups, nt, reverse):
    ncol = cg // LANES
    if reverse:
        ti = lambda i: nt - 1 - i
    else:
        ti = lambda i: i
    hb = tt // SUBLANES
    cur = lambda col: pl.BlockSpec((tt, cg), lambda g, i: (ti(i), 2 * g + col))
    prev = lambda col: pl.BlockSpec((SUBLANES, cg), lambda g, i: (jnp.maximum(ti(i) * hb - 1, 0), 2 * g + col))
    chan = lambda rows: pl.BlockSpec((rows, cg), lambda g, i: (0, g))
    wblk = pl.BlockSpec((ncol, LRU_BLOCK_W, LRU_BLOCK_W), lambda g, i: (g, 0, 0))
    plain = pl.BlockSpec((tt, cg), lambda g, i: (ti(i), g))
    plain_prev = pl.BlockSpec((SUBLANES, cg), lambda g, i: (jnp.maximum(ti(i) * hb - 1, 0), g))
    return cur, prev, chan, wblk, plain, plain_prev


def _lru_fwd(u, conv_w, conv_b, wa, ba, wx, bx, a_param, *, cg, name):
    t, w2 = u.shape
    w = w2 // 2
    n_groups = w // cg
    ncol = cg // LANES
    tt = _tile(t, 256)
    nt = t // tt
    cur, prev, chan, wblk, plain, _ = _lru_specs(tt, cg, n_groups, nt, False)

    def body(xb_ref, xp_ref, gate_ref, cw_ref, cb_ref, wa_ref, ba_ref, wx_ref, bx_ref, ap_ref,
             y_ref, hs_ref, h_ref, a_s, b_s):
        i = pl.program_id(1)

        @pl.when(i == 0)
        def _():
            h_ref[...] = jnp.zeros_like(h_ref)

        keep = (i > 0).astype(F32)
        for n in range(ncol):
            sl = slice(n * LANES, (n + 1) * LANES)
            xb = xb_ref[:, sl]
            xp = xp_ref[:, sl] * keep
            xc = cb_ref[:, sl] + cw_ref[3:4, sl] * xb
            for s in range(1, CONV_WIDTH):
                xc = xc + cw_ref[3 - s:4 - s, sl] * _shift_down(xp, xb, s)
            sp = _softplus(-ap_ref[:, sl])
            _, ig, a, mult, _ = _lru_gates(xc, wa_ref[n].astype(BF16), ba_ref[:, sl],
                                           wx_ref[n].astype(BF16), bx_ref[:, sl], sp)
            a_s[:, sl] = a
            b_s[:, sl] = mult * (ig * xc)

        def step(g, h):
            base = pl.multiple_of(g * SUBLANES, SUBLANES)
            for r in range(SUBLANES):
                h = a_s[pl.ds(base + r, 1), :] * h + b_s[pl.ds(base + r, 1), :]
                hs_ref[pl.ds(base + r, 1), :] = h
            return h

        h = lax.fori_loop(0, tt // SUBLANES, step, h_ref[0:1, :])
        h_ref[0:1, :] = h
        gate = gate_ref[...]
        y_ref[...] = (hs_ref[...] * (gate * _sigmoid(gate))).astype(y_ref.dtype)

    return pl.pallas_call(
        body, name=name, grid=(n_groups, nt),
        in_specs=[cur(0), prev(0), cur(1), chan(CONV_WIDTH), chan(1), wblk, chan(1), wblk, chan(1), chan(1)],
        out_specs=[plain, plain],
        out_shape=[jax.ShapeDtypeStruct((t, w), BF16), jax.ShapeDtypeStruct((t, w), F32)],
        scratch_shapes=[pltpu.VMEM((SUBLANES, cg), F32), pltpu.VMEM((tt, cg), F32), pltpu.VMEM((tt, cg), F32)],
        compiler_params=_cparams(("parallel", "arbitrary")),
    )(u, u, u, conv_w, conv_b, wa, ba, wx, bx, a_param)


def _lru_bwd(u, hs, dy, conv_w, conv_b, wa, ba, wx, bx, a_param, *, cg, name, ride=()):
    nride = len(ride)
    t, w2 = u.shape
    w = w2 // 2
    n_groups = w // cg
    ncol = cg // LANES
    tt = _tile(t, 256)
    nt = t // tt
    cur, prev, chan, wblk, plain, plain_prev = _lru_specs(tt, cg, n_groups, nt, True)
    tn_dims = (((0,), (0,)), ((), ()))
    nt_dims = (((1,), (1,)), ((), ()))

    def body(*refs):
        n_in, n_out, n_scr = 13, 9, 5
        ins, rest = refs[:n_in], refs[n_in:]
        ride_in, rest = rest[:nride], rest[nride:]
        outs, rest = rest[:n_out], rest[n_out:]
        ride_out, rest = rest[:nride], rest[nride:]
        scr, sems = rest[:n_scr], rest[n_scr:]
        if not nride:
            core(*ins, *outs, *scr)
            return
        step = pl.program_id(0) * nt + pl.program_id(1)

        @pl.when(step == 0)
        def _():
            for cp in _scatter_copies(ride_in, ride_out, *sems):
                cp.start()

        core(*ins, *outs, *scr)

        @pl.when(step == n_groups * nt - 1)
        def _():
            for cp in _scatter_copies(ride_in, ride_out, *sems):
                cp.wait()

    def core(xb_ref, xp_ref, gate_ref, hs_ref, hp_ref, dy_ref, cw_ref, cb_ref, wa_ref, ba_ref, wx_ref, bx_ref,
             ap_ref, dxb_ref, dgate_ref, dcw_ref, dcb_ref, dwa_ref, dba_ref, dwx_ref, dbx_ref, dsp_ref,
             c_ref, nx_ref, a_s, dhs_s, lam_s):
        i = pl.program_id(1)
        first_time_block = i == nt - 1

        @pl.when(i == 0)
        def _():
            c_ref[...] = jnp.zeros_like(c_ref)
            nx_ref[...] = jnp.zeros_like(nx_ref)
            for r in (dcw_ref, dcb_ref, dwa_ref, dba_ref, dwx_ref, dbx_ref, dsp_ref):
                r[...] = jnp.zeros_like(r)

        keep = jnp.where(first_time_block, 0.0, 1.0).astype(F32)
        gate = gate_ref[...]
        sg = _sigmoid(gate)
        dyv = dy_ref[...]
        hsv = hs_ref[...]
        dhs_s[...] = dyv * (gate * sg)
        dgate_ref[...] = (dyv * hsv * (sg * (1.0 + gate * (1.0 - sg)))).astype(dgate_ref.dtype)

        saved = []
        for n in range(ncol):
            sl = slice(n * LANES, (n + 1) * LANES)
            xb = xb_ref[:, sl]
            xp = xp_ref[:, sl] * keep
            shifted = [xb] + [_shift_down(xp, xb, s) for s in range(1, CONV_WIDTH)]
            xc = cb_ref[:, sl] + cw_ref[3:4, sl] * xb
            for s in range(1, CONV_WIDTH):
                xc = xc + cw_ref[3 - s:4 - s, sl] * shifted[s]
            sp = _softplus(-ap_ref[:, sl])
            wab = wa_ref[n].astype(BF16)
            wxb = wx_ref[n].astype(BF16)
            r, ig, a, mult, inv_mult = _lru_gates(xc, wab, ba_ref[:, sl], wxb, bx_ref[:, sl], sp)
            a_s[:, sl] = a
            saved.append((sl, shifted, xc, sp, wab, wxb, r, ig, a, mult, inv_mult))

        def step(g, c):
            base = pl.multiple_of(tt - SUBLANES - g * SUBLANES, SUBLANES)
            for r in range(SUBLANES - 1, -1, -1):
                lam = dhs_s[pl.ds(base + r, 1), :] + c
                lam_s[pl.ds(base + r, 1), :] = lam
                c = a_s[pl.ds(base + r, 1), :] * lam
            return c

        c_ref[0:1, :] = lax.fori_loop(0, tt // SUBLANES, step, c_ref[0:1, :])

        for n in range(ncol):
            sl, shifted, xc, sp, wab, wxb, r, ig, a, mult, inv_mult = saved[n]
            lam = lam_s[:, sl]
            hprev = _shift_down(hp_ref[:, sl] * keep, hs_ref[:, sl], 1)
            da = lam * hprev
            dmult = lam * (ig * xc)
            dlog_a = da * a - dmult * (a * a * inv_mult)
            di = lam * (mult * xc)
            dxc = lam * (mult * ig)
            dr = dlog_a * (-LRU_C * sp)
            dsp_ref[:, sl] += jnp.sum(dlog_a * (-LRU_C * r), axis=0, keepdims=True)
            dza = dr * (r * (1.0 - r))
            dzx = di * (ig * (1.0 - ig))
            dba_ref[:, sl] += jnp.sum(dza, axis=0, keepdims=True)
            dbx_ref[:, sl] += jnp.sum(dzx, axis=0, keepdims=True)
            xcb = xc.astype(BF16)
            dzab = dza.astype(BF16)
            dzxb = dzx.astype(BF16)
            dwa_ref[n] += lax.dot_general(xcb, dzab, tn_dims, preferred_element_type=F32)
            dwx_ref[n] += lax.dot_general(xcb, dzxb, tn_dims, preferred_element_type=F32)
            dxc = dxc + lax.dot_general(dzab, wab, nt_dims, preferred_element_type=F32)
            dxc = dxc + lax.dot_general(dzxb, wxb, nt_dims, preferred_element_type=F32)
            dcb_ref[:, sl] += jnp.sum(dxc, axis=0, keepdims=True)
            for s in range(CONV_WIDTH):
                dcw_ref[3 - s:4 - s, sl] += jnp.sum(dxc * shifted[s], axis=0, keepdims=True)
            nx = nx_ref[:, sl]
            dxb = cw_ref[3:4, sl] * dxc
            for s in range(1, CONV_WIDTH):
                dxb = dxb + cw_ref[3 - s:4 - s, sl] * _shift_up(dxc, nx, s)
            dxb_ref[:, sl] = dxb.astype(dxb_ref.dtype)
            nx_ref[:, sl] = dxc[0:SUBLANES, :]

        @pl.when(first_time_block)
        def _():
            dsp_ref[...] = dsp_ref[...] * (-_sigmoid(-ap_ref[...]))

    dxb_spec = pl.BlockSpec((tt, cg), lambda g, i: (nt - 1 - i, g))
    any_spec = pl.BlockSpec(memory_space=pl.ANY)
    ride_shape, ride_sems = _scatter_shapes(ride) if nride else ([], [])
    outs = pl.pallas_call(
        body, name=name, grid=(n_groups, nt),
        in_specs=[cur(0), prev(0), cur(1), plain, plain_prev, plain, chan(CONV_WIDTH), chan(1), wblk, chan(1), wblk,
                  chan(1), chan(1)] + [any_spec] * nride,
        out_specs=[dxb_spec, dxb_spec, chan(CONV_WIDTH), chan(1), wblk, chan(1), wblk, chan(1), chan(1)]
        + [any_spec] * nride,
        out_shape=[jax.ShapeDtypeStruct((t, w), BF16), jax.ShapeDtypeStruct((t, w), BF16),
                   jax.ShapeDtypeStruct(conv_w.shape, F32), jax.ShapeDtypeStruct(conv_b.shape, F32),
                   jax.ShapeDtypeStruct(wa.shape, F32), jax.ShapeDtypeStruct(ba.shape, F32),
                   jax.ShapeDtypeStruct(wx.shape, F32), jax.ShapeDtypeStruct(bx.shape, F32),
                   jax.ShapeDtypeStruct(a_param.shape, F32)] + ride_shape,
        scratch_shapes=[pltpu.VMEM((SUBLANES, cg), F32), pltpu.VMEM((SUBLANES, cg), F32),
                        pltpu.VMEM((tt, cg), F32), pltpu.VMEM((tt, cg), F32), pltpu.VMEM((tt, cg), F32)] + ride_sems,
        compiler_params=_cparams(("arbitrary", "arbitrary")),
    )(u, u, u, hs, hs, dy, conv_w, conv_b, wa, ba, wx, bx, a_param, *ride)
    return outs[:9], outs[9:]


def _fgate_fwd(f, b_f, *, name):
    t, n = f.shape
    tt = _tile(t, 256)
    width = FOX_HEADS * FOX_HEAD_DIM

    def body(f_ref, b_ref, cum_ref, wide_ref, carry_ref):
        i = pl.program_id(0)

        @pl.when(i == 0)
        def _():
            carry_ref[...] = jnp.zeros_like(carry_ref)

        z = f_ref[...] + b_ref[...]
        lf = jnp.minimum(z, 0.0) - _log1p(jnp.exp(-jnp.abs(z)))
        row = lax.broadcasted_iota(jnp.int32, (tt, tt), 0)
        col = lax.broadcasted_iota(jnp.int32, (tt, tt), 1)
        tri = (col <= row).astype(F32)
        cum = jnp.dot(tri, lf, precision=HIGHEST, preferred_element_type=F32) + carry_ref[0:1, :]
        cum_ref[...] = cum
        carry_ref[0:1, :] = cum[tt - 1:tt, :]
        head = lax.broadcasted_iota(jnp.int32, (n, width), 0)
        chan = lax.broadcasted_iota(jnp.int32, (n, width), 1) // FOX_HEAD_DIM
        wide_ref[...] = jnp.dot(cum, (head == chan).astype(F32), precision=HIGHEST, preferred_element_type=F32)

    return pl.pallas_call(
        body, name=name, grid=(t // tt,),
        in_specs=[pl.BlockSpec((tt, n), lambda i: (i, 0)), pl.BlockSpec((1, n), lambda i: (0, 0))],
        out_specs=[pl.BlockSpec((tt, n), lambda i: (i, 0)), pl.BlockSpec((tt, width), lambda i: (i, 0))],
        out_shape=[jax.ShapeDtypeStruct((t, n), F32), jax.ShapeDtypeStruct((t, width), F32)],
        scratch_shapes=[pltpu.VMEM((SUBLANES, n), F32)],
        compiler_params=_cparams(("arbitrary",)),
    )(f, b_f)


def _fgate_bwd(dcum, f, b_f, *, name):
    t, n = f.shape
    tt = _tile(t, 256)
    nt = t // tt

    def body(dc_ref, f_ref, b_ref, df_ref, db_ref, carry_ref):
        i = pl.program_id(0)

        @pl.when(i == 0)
        def _():
            carry_ref[...] = jnp.zeros_like(carry_ref)
            db_ref[...] = jnp.zeros_like(db_ref)

        row = lax.broadcasted_iota(jnp.int32, (tt, tt), 0)
        col = lax.broadcasted_iota(jnp.int32, (tt, tt), 1)
        triu = (col >= row).astype(F32)
        dlf = jnp.dot(triu, dc_ref[...], precision=HIGHEST, preferred_element_type=F32) + carry_ref[0:1, :]
        carry_ref[0:1, :] = dlf[0:1, :]
        z = f_ref[...] + b_ref[...]
        df = dlf * _sigmoid(-z)
        df_ref[...] = df
        db_ref[...] += jnp.sum(df, axis=0, keepdims=True)

    blk = pl.BlockSpec((tt, n), lambda i: (nt - 1 - i, 0))
    vec = pl.BlockSpec((1, n), lambda i: (0, 0))
    return pl.pallas_call(
        body, name=name, grid=(nt,),
        in_specs=[blk, blk, vec], out_specs=[blk, vec],
        out_shape=[jax.ShapeDtypeStruct((t, n), F32), jax.ShapeDtypeStruct((1, n), F32)],
        scratch_shapes=[pltpu.VMEM((SUBLANES, n), F32)],
        compiler_params=_cparams(("arbitrary",)),
    )(dcum, f, b_f)


def _attn_fwd(start, qkv, ckt, gate, *, name, tq, tk):
    t = qkv.shape[0]
    f = gate.shape[1]
    npair = f // LANES
    nq = t // tq
    ratio = tq // tk
    assert tq == ratio * tk and t == nq * tq
    scale = 1.0 / math.sqrt(FOX_HEAD_DIM)
    nt_dims = (((1,), (1,)), ((), ()))

    def body(start_ref, q_ref, k_ref, v_ref, ck_ref, g_ref, o_ref, y_ref, l_ref):
        i = pl.program_id(1)
        first = start_ref[pl.program_id(0), i]
        lane = lax.broadcasted_iota(jnp.int32, (tq, LANES), 1)
        lo = lane < FOX_HEAD_DIM
        q2 = q_ref[...] * scale
        qs = (jnp.where(lo, q2, 0).astype(BF16), jnp.where(lo, 0, q2).astype(BF16))
        row = lax.broadcasted_iota(jnp.int32, (tq, tk), 0)
        col = lax.broadcasted_iota(jnp.int32, (tq, tk), 1)

        def kv_step(j, carry, diag):
            off = pl.multiple_of(j * tk, tk)
            kj = k_ref[pl.ds(off, tk), :]
            vj = v_ref[pl.ds(off, tk), :]
            ck = ck_ref[:, pl.ds(off, tk)]
            new = []
            for h in range(2):
                m, l, acc = carry[h]
                s = lax.dot_general(qs[h], kj, nt_dims, preferred_element_type=F32) - ck[h:h + 1, :]
                if diag is not None:
                    s = jnp.where(col + diag * tk <= row, s, NEG_INF)
                m_new = jnp.maximum(m, jnp.max(s, axis=-1, keepdims=True))
                alpha = jnp.exp(m - m_new)
                p = jnp.exp(s - m_new)
                l = alpha * l + jnp.sum(p, axis=-1, keepdims=True)
                acc = alpha * acc + jnp.dot(p.astype(BF16), vj, preferred_element_type=F32)
                new.append((m_new, l, acc))
            return tuple(new)

        init = tuple((jnp.full((tq, 1), NEG_INF, F32), jnp.zeros((tq, 1), F32), jnp.zeros((tq, LANES), F32))
                     for _ in range(2))
        carry = lax.fori_loop(first, i * ratio, lambda j, c: kv_step(j, c, None), init)
        for d in range(ratio):
            carry = kv_step(i * ratio + d, carry, d)
        (m0, l0, a0), (m1, l1, a1) = carry
        o = jnp.where(lo, a0 / l0, a1 / l1)
        o_ref[...] = o
        gate_v = g_ref[...]
        y_ref[...] = (o * (gate_v * _sigmoid(gate_v))).astype(y_ref.dtype)
        lse_t = jnp.transpose(jnp.where(lo, m0 + jnp.log(l0), m1 + jnp.log(l1)))
        l_ref[0:1, :] = lse_t[0:1, :]
        l_ref[1:2, :] = lse_t[FOX_HEAD_DIM:FOX_HEAD_DIM + 1, :]

    blk = lambda base: pl.BlockSpec((tq, LANES), lambda p, i, s: (i, base + p))
    full = lambda base: pl.BlockSpec((t, LANES), lambda p, i, s: (0, base + p))
    return pl.pallas_call(
        body, name=name,
        grid_spec=pltpu.PrefetchScalarGridSpec(
            num_scalar_prefetch=1, grid=(npair, nq),
            in_specs=[blk(0), full(npair), full(2 * npair), pl.BlockSpec((None, 2, t), lambda p, i, s: (p, 0, 0)),
                      blk(0)],
            out_specs=[blk(0), blk(0), pl.BlockSpec((None, 2, tq), lambda p, i, s: (p, 0, i))]),
        out_shape=[jax.ShapeDtypeStruct((t, f), F32), jax.ShapeDtypeStruct((t, f), BF16),
                   jax.ShapeDtypeStruct((npair, 2, t), F32)],
        compiler_params=_cparams(("parallel", "arbitrary")),
    )(start, qkv, qkv, qkv, ckt, gate)


def _attn_bwd(end, qkv, do, lt, dt, cke, *, name):
    t, f = do.shape
    npair = f // LANES
    tk = _tile(t, ATTN_TILE)
    nk = t // tk
    scale = 1.0 / math.sqrt(FOX_HEAD_DIM)
    nt_dims = (((1,), (1,)), ((), ()))
    tn_dims = (((0,), (0,)), ((), ()))

    def body(end_ref, k_ref, v_ref, q_ref, do_ref, l_ref, d_ref, ck_ref, dq_out_ref, dk_ref, dv_ref, dck_ref, dcq_ref,
             dq_ref):
        j = pl.program_id(1)
        last = end_ref[pl.program_id(0), j]

        @pl.when(j == 0)
        def _():
            dq_ref[...] = jnp.zeros_like(dq_ref)
            dcq_ref[...] = jnp.zeros_like(dcq_ref)

        lane = lax.broadcasted_iota(jnp.int32, (tk, LANES), 1)
        lo = lane < FOX_HEAD_DIM
        sel = (lo, jnp.logical_not(lo))
        kj = k_ref[...]
        vj = v_ref[...]
        km = tuple(jnp.where(sel[h], kj, 0).astype(BF16) for h in range(2))
        ckv = ck_ref[...]
        ckh = (ckv[:, 0:1], ckv[:, FOX_HEAD_DIM:FOX_HEAD_DIM + 1])
        row = lax.broadcasted_iota(jnp.int32, (tk, tk), 0)
        col = lax.broadcasted_iota(jnp.int32, (tk, tk), 1)
        causal = row <= col

        def q_step(i, carry, masked):
            dk_acc, dv_acc, dck = carry
            off = pl.multiple_of(i * tk, tk)
            qi = q_ref[pl.ds(off, tk), :]
            doi = do_ref[pl.ds(off, tk), :]
            lrow = l_ref[:, pl.ds(off, tk)]
            drow = d_ref[:, pl.ds(off, tk)]
            dq_add = jnp.zeros((tk, LANES), F32)
            new_dck = []
            for h in range(2):
                qm = jnp.where(sel[h], qi, 0).astype(BF16)
                dom = jnp.where(sel[h], doi, 0).astype(BF16)
                st = lax.dot_general(kj, qm, nt_dims, preferred_element_type=F32) * scale
                st = st - ckh[h] - lrow[h:h + 1, :]
                if masked:
                    st = jnp.where(causal, st, NEG_INF)
                pt = jnp.exp(st)
                dpt = lax.dot_general(vj, dom, nt_dims, preferred_element_type=F32)
                dst = pt * (dpt - drow[h:h + 1, :])
                ptb = pt.astype(BF16)
                dstb = dst.astype(BF16)
                dv_acc = dv_acc + jnp.dot(ptb, dom, preferred_element_type=F32)
                dk_acc = dk_acc + jnp.dot(dstb, qm, preferred_element_type=F32)
                dq_add = dq_add + lax.dot_general(dstb, km[h], tn_dims, preferred_element_type=F32)
                new_dck.append(dck[h] - jnp.sum(dst, axis=-1, keepdims=True))
                dcq_ref[h:h + 1, pl.ds(off, tk)] += jnp.sum(dst, axis=0, keepdims=True)
            dq_ref[pl.ds(off, tk), :] += dq_add * scale
            return dk_acc, dv_acc, tuple(new_dck)

        zero = jnp.zeros((tk, LANES), F32)
        carry = (zero, zero, (jnp.zeros((tk, 1), F32), jnp.zeros((tk, 1), F32)))
        carry = q_step(j, carry, True)
        dk_acc, dv_acc, dck = lax.fori_loop(j + 1, last, lambda i, c: q_step(i, c, False), carry)
        dk_ref[...] = (dk_acc * scale).astype(dk_ref.dtype)
        dv_ref[...] = dv_acc.astype(dv_ref.dtype)
        dck_t = jnp.transpose(jnp.where(lo, dck[0], dck[1]))
        dck_ref[0:1, :] = dck_t[0:1, :]
        dck_ref[1:2, :] = dck_t[FOX_HEAD_DIM:FOX_HEAD_DIM + 1, :]

        @pl.when(j == nk - 1)
        def _():
            dq_out_ref[...] = dq_ref[...].astype(dq_out_ref.dtype)

    blk = lambda base: pl.BlockSpec((tk, LANES), lambda p, j, e: (j, base + p))
    full = lambda base: pl.BlockSpec((t, LANES), lambda p, j, e: (0, base + p))
    rows = pl.BlockSpec((None, 2, t), lambda p, j, e: (p, 0, 0))
    return pl.pallas_call(
        body, name=name,
        grid_spec=pltpu.PrefetchScalarGridSpec(
            num_scalar_prefetch=1, grid=(npair, nk),
            in_specs=[blk(npair), blk(2 * npair), full(0), full(0), rows, rows, blk(0)],
            out_specs=[full(0), blk(0), blk(0), pl.BlockSpec((None, 2, tk), lambda p, j, e: (p, 0, j)), rows],
            scratch_shapes=[pltpu.VMEM((t, LANES), F32)]),
        out_shape=[jax.ShapeDtypeStruct((t, f), BF16), jax.ShapeDtypeStruct((t, f), BF16),
                   jax.ShapeDtypeStruct((t, f), BF16), jax.ShapeDtypeStruct((npair, 2, t), F32),
                   jax.ShapeDtypeStruct((npair, 2, t), F32)],
        compiler_params=_cparams(("parallel", "arbitrary")),
    )(end, qkv, qkv, qkv, do, lt, dt, cke)


ATTN_TILE = 512
ATTN_FWD_QUERIES = 512
EXP_ZERO = -104.0
BOUND_SLACK = 1.02


def _attn_row_stats(qkv, *, name):
    t = qkv.shape[0]
    f = qkv.shape[1] // 3
    tt = _tile(t, 512)

    def body(q_ref, k_ref, s_ref):
        q = q_ref[...].astype(F32)
        k = k_ref[...].astype(F32)
        chan = lax.broadcasted_iota(jnp.int32, (f, LANES), 0) // FOX_HEAD_DIM
        lane = lax.broadcasted_iota(jnp.int32, (f, LANES), 1)
        acc = jnp.zeros((tt, LANES), F32)
        for off, val in ((0, q * q), (FOX_HEADS, q * k), (2 * FOX_HEADS, k * k)):
            pick = (chan == lane - off).astype(BF16)
            acc = acc + jnp.dot(val.astype(BF16), pick, preferred_element_type=F32)
        s_ref[...] = acc

    return pl.pallas_call(
        body, name=name, grid=(t // tt,),
        in_specs=[pl.BlockSpec((tt, f), lambda i: (i, 0)), pl.BlockSpec((tt, f), lambda i: (i, 1))],
        out_specs=pl.BlockSpec((tt, LANES), lambda i: (i, 0)),
        out_shape=jax.ShapeDtypeStruct((t, LANES), F32),
        compiler_params=_cparams(("parallel",)),
    )(qkv, qkv)


def _attn_skip_tables(stats, cum16, tile):
    t = stats.shape[0]
    nb = t // tile
    scale = 1.0 / math.sqrt(FOX_HEAD_DIM)
    qn = jnp.sqrt(stats[:, :FOX_HEADS]) * scale
    sii = stats[:, FOX_HEADS:2 * FOX_HEADS] * scale - cum16
    kmax = jnp.max(jnp.sqrt(stats[:, 2 * FOX_HEADS:3 * FOX_HEADS]), axis=0, keepdims=True)
    arow = qn * kmax * BOUND_SLACK - sii + 0.5 * BOUND_SLACK
    a_blk = jnp.max(arow.reshape(nb, tile, FOX_HEADS), axis=1)
    c_blk = -cum16.reshape(nb, tile, FOX_HEADS)[:, tile - 1, :]
    dead = (a_blk[:, None, :] + c_blk[None, :, :]) < EXP_ZERO
    start_h = jnp.sum(dead.astype(jnp.int32), axis=1)
    blk = jnp.arange(nb, dtype=jnp.int32)
    start = jnp.minimum(jnp.min(start_h.reshape(nb, FOX_HEADS // 2, 2), axis=2), blk[:, None]).T
    needs = start[:, :, None] <= blk[None, None, :]
    end = jnp.max(jnp.where(needs, blk[None, :, None] + 1, 0), axis=1)
    return start, jnp.maximum(end, blk[None, :] + 1)


def _fox_post_bwd(dy, o, gate, *, name):
    t, f = dy.shape
    tt = _tile(t, 512)

    def body(dy_ref, o_ref, g_ref, do_ref, dg_ref, dl_ref):
        g = g_ref[...]
        sg = _sigmoid(g)
        dyv = dy_ref[...]
        ov = o_ref[...]
        do = dyv * (g * sg)
        do_ref[...] = do.astype(do_ref.dtype)
        dg_ref[...] = (dyv * ov * (sg * (1.0 + g * (1.0 - sg)))).astype(dg_ref.dtype)
        chan = lax.broadcasted_iota(jnp.int32, (f, LANES), 0)
        head = lax.broadcasted_iota(jnp.int32, (f, LANES), 1)
        pick = (chan // FOX_HEAD_DIM == head).astype(F32)
        dl_ref[...] = jnp.dot(do * ov, pick, precision=HIGHEST, preferred_element_type=F32)

    blk = pl.BlockSpec((tt, f), lambda i: (i, 0))
    return pl.pallas_call(
        body, name=name, grid=(t // tt,),
        in_specs=[blk, blk, blk], out_specs=[blk, blk, pl.BlockSpec((tt, LANES), lambda i: (i, 0))],
        out_shape=[jax.ShapeDtypeStruct((t, f), BF16), jax.ShapeDtypeStruct((t, f), BF16),
                   jax.ShapeDtypeStruct((t, LANES), F32)],
        compiler_params=_cparams(("parallel",)),
    )(dy, o, gate)


def _adamw(w, g, m, v, *, name):
    _, r, c = w.shape
    tr = _tile(r, 256) if r % SUBLANES == 0 else r
    c1 = 1.0 - ADAM_B1 ** ADAM_STEP
    c2 = 1.0 - ADAM_B2 ** ADAM_STEP

    def body(w_ref, g_ref, m_ref, v_ref, d_ref, mo_ref, vo_ref):
        gv = g_ref[...]
        mn = ADAM_B1 * m_ref[...] + (1.0 - ADAM_B1) * gv
        vn = ADAM_B2 * v_ref[...] + (1.0 - ADAM_B2) * (gv * gv)
        mo_ref[...] = mn
        vo_ref[...] = vn
        d_ref[...] = -ADAM_LR * ((mn / c1) / (jnp.sqrt(vn / c2) + ADAM_EPS) + ADAM_WD * w_ref[...])

    blk = pl.BlockSpec((None, tr, c), lambda i: (0, i, 0))
    return pl.pallas_call(
        body, name=name, grid=(r // tr,), in_specs=[blk] * 4, out_specs=[blk] * 3,
        out_shape=[jax.ShapeDtypeStruct((1, r, c), F32)] * 3,
        compiler_params=_cparams(("parallel",)),
    )(w, g, m, v)


def _sum_slots(land, *, name):
    ns, r, c = land.shape
    tr = _tile(r, 64) if r % SUBLANES == 0 else r

    def body(l_ref, o_ref):
        acc = l_ref[0].astype(F32)
        for s in range(1, ns):
            acc = acc + l_ref[s].astype(F32)
        o_ref[...] = acc

    return pl.pallas_call(
        body, name=name, grid=(r // tr,),
        in_specs=[pl.BlockSpec((ns, tr, c), lambda i: (0, i, 0))],
        out_specs=pl.BlockSpec((tr, c), lambda i: (i, 0)),
        out_shape=jax.ShapeDtypeStruct((r, c), F32),
        compiler_params=_cparams(("parallel",)),
    )(land)


ANY = pl.BlockSpec(memory_space=pl.ANY)


def _flip(v, bit):
    return 1 - v if bit else v


def _gather_chips(shards, small, *, name):
    n = len(shards)
    rels = ((1, 0), (0, 1), (1, 1))

    def body(*refs):
        ins, small_in = refs[:n], refs[n]
        outs, small_out = refs[n + 1:2 * n + 1], refs[2 * n + 1]
        send, recv, loc = refs[2 * n + 2:]
        x, y, c = lax.axis_index("x"), lax.axis_index("y"), lax.axis_index("c")
        me = 2 * x + y
        sibling = (x, y, 1 - c)
        local = [pltpu.make_async_copy(ins[k], outs[k].at[me], loc.at[k]) for k in range(n)]
        local.append(pltpu.make_async_copy(small_in, small_out.at[me], loc.at[n]))
        for cp in local:
            cp.start()

        def rows(k):
            half = ins[k].shape[0] // 2
            return pl.ds(pl.multiple_of(c * half, SUBLANES), half)

        sends = []
        for r, (rx, ry) in enumerate(rels):
            to = (_flip(x, rx), _flip(y, ry), c)
            for k in range(n):
                cp = pltpu.make_async_remote_copy(
                    src_ref=ins[k].at[rows(k), :], dst_ref=outs[k].at[me, rows(k), :],
                    send_sem=send.at[r * n + k], recv_sem=recv.at[r * n + k], device_id=to, device_id_type=MESH)
                cp.start()
                sends.append(cp)
            cp = pltpu.make_async_remote_copy(
                src_ref=small_in, dst_ref=small_out.at[me], send_sem=send.at[6 * n + r], recv_sem=recv.at[6 * n + r],
                device_id=to, device_id_type=MESH)
            cp.start()
            sends.append(cp)
        for r, (rx, ry) in enumerate(rels):
            src_chip = 2 * _flip(x, rx) + _flip(y, ry)
            for k in range(n):
                landed = outs[k].at[src_chip, rows(k), :]
                sends[r * (n + 1) + k].wait_recv()
                cp = pltpu.make_async_remote_copy(
                    src_ref=landed, dst_ref=landed, send_sem=send.at[3 * n + r * n + k],
                    recv_sem=recv.at[3 * n + r * n + k], device_id=sibling, device_id_type=MESH)
                cp.start()
                sends.append(cp)
            sends[r * (n + 1) + n].wait_recv()
        for cp in sends[:3 * (n + 1)]:
            cp.wait_send()
        for cp in sends[3 * (n + 1):]:
            cp.wait()
        for cp in local:
            cp.wait()

    vmem = pl.BlockSpec(memory_space=pltpu.VMEM)
    return pl.pallas_call(
        body, name=name, in_specs=[vmem] * (n + 1), out_specs=[vmem] * (n + 1),
        out_shape=[jax.ShapeDtypeStruct((N_CHIPS,) + s.shape, s.dtype) for s in list(shards) + [small]],
        scratch_shapes=[pltpu.SemaphoreType.DMA((6 * n + 3,)), pltpu.SemaphoreType.DMA((6 * n + 3,)),
                        pltpu.SemaphoreType.DMA((n + 1,))],
        compiler_params=pltpu.CompilerParams(has_side_effects=True, vmem_limit_bytes=VMEM_LIMIT),
    )(*shards, small)


_RELS7 = tuple((r >> 2 & 1, r >> 1 & 1, r & 1) for r in range(1, N_DEV))


def _scatter_copies(ins, outs, send, recv, loc):
    n = len(ins)
    x, y, c = lax.axis_index("x"), lax.axis_index("y"), lax.axis_index("c")
    me = 4 * x + 2 * y + c

    def piece(k, px, py, pc):
        half = ins[k].shape[1] // 2
        return ins[k].at[2 * px + py, pl.ds(pc * half, half), :]

    copies = [pltpu.make_async_copy(piece(k, x, y, c), outs[k].at[me], loc.at[k]) for k in range(n)]
    for r, (rx, ry, rc) in enumerate(_RELS7):
        tx, ty, tc = _flip(x, rx), _flip(y, ry), _flip(c, rc)
        for k in range(n):
            copies.append(pltpu.make_async_remote_copy(
                src_ref=piece(k, tx, ty, tc), dst_ref=outs[k].at[me], send_sem=send.at[r * n + k],
                recv_sem=recv.at[r * n + k], device_id=(tx, ty, tc), device_id_type=MESH))
    return copies


def _scatter_shapes(grads):
    n = len(grads)
    out_shape = [jax.ShapeDtypeStruct((N_DEV, g.shape[1] // 2, g.shape[2]), g.dtype) for g in grads]
    sems = [pltpu.SemaphoreType.DMA((7 * n,)), pltpu.SemaphoreType.DMA((7 * n,)), pltpu.SemaphoreType.DMA((n,))]
    return out_shape, sems


def _scatter_pieces(grads, *, name):
    n = len(grads)

    def body(*refs):
        copies = _scatter_copies(refs[:n], refs[n:2 * n], *refs[2 * n:])
        for cp in copies:
            cp.start()
        for cp in copies:
            cp.wait()

    out_shape, sems = _scatter_shapes(grads)
    return pl.pallas_call(
        body, name=name, in_specs=[ANY] * n, out_specs=[ANY] * n, out_shape=out_shape, scratch_shapes=sems,
        compiler_params=pltpu.CompilerParams(has_side_effects=True),
    )(*grads)


def _join_cores(halves, *, name):
    n = len(halves)

    def body(*refs):
        ins, outs = refs[:n], refs[n:2 * n]
        send, recv, loc = refs[2 * n:]
        x, y, c = lax.axis_index("x"), lax.axis_index("y"), lax.axis_index("c")
        copies = []
        for k in range(n):
            half = ins[k].shape[0]
            mine = outs[k].at[0, pl.ds(c * half, half), :]
            cp = pltpu.make_async_copy(ins[k], mine, loc.at[k])
            cp.start()
            copies.append(cp)
            cp = pltpu.make_async_remote_copy(
                src_ref=ins[k], dst_ref=mine, send_sem=send.at[k], recv_sem=recv.at[k],
                device_id=(x, y, 1 - c), device_id_type=MESH)
            cp.start()
            copies.append(cp)
        for cp in copies:
            cp.wait()

    in_vmem = pl.BlockSpec(memory_space=pltpu.VMEM)
    return pl.pallas_call(
        body, name=name, in_specs=[in_vmem] * n, out_specs=[in_vmem] * n,
        out_shape=[jax.ShapeDtypeStruct((1, 2 * h.shape[0], h.shape[1]), h.dtype) for h in halves],
        scratch_shapes=[pltpu.SemaphoreType.DMA((n,)), pltpu.SemaphoreType.DMA((n,)), pltpu.SemaphoreType.DMA((n,))],
        compiler_params=pltpu.CompilerParams(has_side_effects=True, vmem_limit_bytes=VMEM_LIMIT),
    )(*halves)


def _allreduce_small(buf, *, name):
    r, n = buf.shape
    half = r // 2
    rels = ((1, 0), (0, 1), (1, 1))

    def body(in_ref, out_ref, sib_ref, chips_ref, send, recv):
        x, y, c = lax.axis_index("x"), lax.axis_index("y"), lax.axis_index("c")
        sibling = (x, y, 1 - c)
        chip = 2 * x + y
        rows = pl.ds(pl.multiple_of(c * half, SUBLANES), half)

        swap = pltpu.make_async_remote_copy(src_ref=in_ref, dst_ref=sib_ref, send_sem=send.at[0], recv_sem=recv.at[0],
                                            device_id=sibling, device_id_type=MESH)
        swap.start()
        swap.wait()
        chips_ref[chip] = in_ref[rows, :] + sib_ref[rows, :]

        sends = []
        for k, (rx, ry) in enumerate(rels):
            cp = pltpu.make_async_remote_copy(
                src_ref=chips_ref.at[chip], dst_ref=chips_ref.at[chip], send_sem=send.at[1 + k],
                recv_sem=recv.at[1 + k], device_id=(_flip(x, rx), _flip(y, ry), c), device_id_type=MESH)
            cp.start()
            sends.append(cp)
        for cp in sends:
            cp.wait()
        total = chips_ref[0]
        for s in range(1, N_CHIPS):
            total = total + chips_ref[s]
        out_ref[rows, :] = total

        back = pltpu.make_async_remote_copy(src_ref=out_ref.at[rows, :], dst_ref=out_ref.at[rows, :],
                                            send_sem=send.at[4], recv_sem=recv.at[4],
                                            device_id=sibling, device_id_type=MESH)
        back.start()
        back.wait()

    vmem = pl.BlockSpec(memory_space=pltpu.VMEM)
    return pl.pallas_call(
        body, name=name, in_specs=[vmem], out_specs=vmem,
        out_shape=jax.ShapeDtypeStruct((r, n), F32),
        scratch_shapes=[pltpu.VMEM((r, n), F32), pltpu.VMEM((N_CHIPS, half, n), F32),
                        pltpu.SemaphoreType.DMA((5,)), pltpu.SemaphoreType.DMA((5,))],
        compiler_params=pltpu.CompilerParams(has_side_effects=True, vmem_limit_bytes=VMEM_LIMIT),
    )(buf)


def _pack(arrs):
    flat = []
    for a in arrs:
        v = a.reshape(-1)
        pad = (-v.shape[0]) % LANES
        if pad:
            v = jnp.pad(v, (0, pad))
        flat.append(v)
    v = jnp.concatenate(flat)
    pad = (-v.shape[0]) % (LANES * SUBLANES)
    if pad:
        v = jnp.pad(v, (0, pad))
    return v.reshape(-1, LANES)


def _unpack(buf, shapes):
    v = buf.reshape(-1)
    out, off = [], 0
    for s in shapes:
        n = math.prod(s)
        out.append(v[off:off + n].reshape(s))
        off += n + (-n) % LANES
    return out


def kernel(x, norm_g, final_g, lru_w_in, lru_conv_w, lru_conv_b, lru_wa, lru_ba, lru_wx, lru_bx, lru_a_param, lru_w_out, fox_w_in, fox_b_f, fox_w_out, loss_target, m_norm_g, m_final_g, m_lru_w_in, m_lru_conv_w, m_lru_conv_b, m_lru_wa, m_lru_ba, m_lru_wx, m_lru_bx, m_lru_a_param, m_lru_w_out, m_fox_w_in, m_fox_b_f, m_fox_w_out, v_norm_g, v_final_g, v_lru_w_in, v_lru_conv_w, v_lru_conv_b, v_lru_wa, v_lru_ba, v_lru_wx, v_lru_bx, v_lru_a_param, v_lru_w_out, v_fox_w_in, v_fox_b_f, v_fox_w_out):
    t, d = x.shape[1], x.shape[2]
    w = lru_wa.shape[1] * LRU_BLOCK_W
    f = FOX_HEADS * FOX_HEAD_DIM
    npair = f // LANES
    x0 = x.reshape(t, d)
    tgt = loss_target.reshape(t, d)
    chip = 2 * lax.axis_index("x") + lax.axis_index("y")

    g_lwi, g_lwo, g_fwi, g_fwo, g_cw = _gather_chips(
        [lru_w_in[0].astype(BF16), lru_w_out[0].astype(BF16), fox_w_in[0].astype(BF16), fox_w_out[0].astype(BF16)],
        lru_conv_w[0], name="gather_weights")
    cg = w // 2
    lwi = jnp.concatenate([g_lwi[0], g_lwi[2], g_lwi[1], g_lwi[3]], axis=1)
    lwo = g_lwo.reshape(w, d)
    fwi = jnp.concatenate([g_fwi[s] for s in range(N_CHIPS)], axis=1)
    w_qkv, w_g2 = fwi[:, :3 * f], fwi[:, 3 * f:4 * f]
    w_f = jnp.pad(fwi[:, 4 * f:], ((0, 0), (0, LANES - FOX_HEADS)))
    fwo = g_fwo.reshape(f, d)
    conv_w = jnp.concatenate([g_cw[s] for s in range(N_CHIPS)], axis=1)
    conv_b, ba, bx, a_param = lru_conv_b, lru_ba, lru_bx, lru_a_param
    wa, wx = lru_wa[0], lru_wx[0]
    b_f = jnp.pad(fox_b_f, ((0, 0), (0, LANES - FOX_HEADS)))

    h0 = _rmsnorm(x0, norm_g[0], name="norm0")
    u = _matmul(h0, lwi, name="lru_in")
    y1, hs = _lru_fwd(u, conv_w, conv_b, wa, ba, wx, bx, a_param, cg=cg, name="lru_fwd")
    x1 = _matmul(y1, lwo, add=x0, name="lru_out")
    h1 = _rmsnorm(x1, norm_g[1], name="norm1")
    qkv = _matmul(h1, w_qkv, out_dtype=BF16, name="fox_qkv")
    gate2 = _matmul(h1, w_g2, name="fox_gate")
    flog = _matmul(h1, w_f, name="fox_f")
    cum, cke = _fgate_fwd(flog, b_f, name="fgate_fwd")
    cum16 = cum[:, :FOX_HEADS]
    ckt = cum16.T.reshape(npair, 2, t)
    a_tk, a_tq = _tile(t, ATTN_TILE), _tile(t, ATTN_FWD_QUERIES)
    a_start, a_end = _attn_skip_tables(_attn_row_stats(qkv, name="attn_row_stats"), cum16, a_tk)
    a_start_fwd = jnp.min(a_start.reshape(npair, t // a_tq, a_tq // a_tk), axis=2)
    o, y2, lse = _attn_fwd(a_start_fwd, qkv, ckt, gate2, name="attn_fwd", tq=a_tq, tk=a_tk)
    x2 = _matmul(y2, fwo, add=x1, name="fox_out")
    lsum, dx2, dgf = _final_loss(x2, tgt, final_g, name="final_loss")
    loss = lax.psum(0.5 * jnp.sum(lsum) / d, ("x", "y", "c"))

    d_fwo = _matmul(y2, dx2, ta=True, out_dtype=BF16, name="d_fox_w_out")
    dy2 = _matmul(dx2, fwo, tb=True, name="d_y2")
    do, dgate2, dl = _fox_post_bwd(dy2, o, gate2, name="fox_post_bwd")
    lt = lse
    dt = dl[:, :FOX_HEADS].T.reshape(npair, 2, t)
    dq, dk, dv, dck, dcq = _attn_bwd(a_end, qkv, do, lt, dt, cke, name="attn_bwd")
    dcum = jnp.pad((dck + dcq).reshape(FOX_HEADS, t).T, ((0, 0), (0, LANES - FOX_HEADS)))
    dflog, db_f = _fgate_bwd(dcum, flog, b_f, name="fgate_bwd")
    du2 = [dq, dk, dv, dgate2]
    dflog_b = dflog.astype(BF16)
    dh1 = _matmul_kparts(du2, fwi[:, :4 * f], chunk=f, name="d_h1_a")
    dh1 = _matmul(dflog_b, w_f, tb=True, add=dh1, name="d_h1_b")
    d_fwi_a = _matmul_nparts(h1, du2, chunk=f, out_dtype=BF16, name="d_fox_w_in_a")
    d_fwi_b = _matmul(h1, dflog_b, ta=True, out_dtype=BF16, name="d_fox_w_in_b")
    d_fwi = jnp.concatenate([d_fwi_a, d_fwi_b[:, :FOX_HEADS]], axis=1)
    dx1, dg1 = _rmsnorm_bwd(dh1, x1, norm_g[1], dx2, name="norm1_bwd")

    d_lwo = _matmul(y1, dx1, ta=True, out_dtype=BF16, name="d_lru_w_out")
    dy1 = _matmul(dx1, lwo, tb=True, name="d_y1")
    n_fwi = fox_w_in.shape[2]
    g_fwi4 = jnp.stack([d_fwi[:, s * n_fwi:(s + 1) * n_fwi] for s in range(N_CHIPS)])
    g_fwo4 = d_fwo.reshape(N_CHIPS, f // N_CHIPS, d)
    g_lwo4 = d_lwo.reshape(N_CHIPS, w // N_CHIPS, d)
    (dxb, dgate, d_cw, d_cb, d_wa, d_ba, d_wx, d_bx, d_ap), lands_early = _lru_bwd(
        u, hs, dy1, conv_w, conv_b, wa, ba, wx, bx, a_param, cg=cg, name="lru_bwd", ride=[g_lwo4, g_fwi4, g_fwo4])
    dh0 = _matmul_kparts([dxb, dgate], lwi, chunk=cg, name="d_h0")
    d_lwi_p = _matmul_nparts(h0, [dxb, dgate], chunk=cg, out_dtype=BF16, name="d_lru_w_in")
    dx0, dg0 = _rmsnorm_bwd(dh0, x0, norm_g[0], dx1, name="norm0_bwd")

    csz = cg
    g_lwi4 = jnp.stack([d_lwi_p[:, 0:csz], d_lwi_p[:, 2 * csz:3 * csz], d_lwi_p[:, csz:2 * csz],
                        d_lwi_p[:, 3 * csz:]])
    lands = list(_scatter_pieces([g_lwi4], name="scatter_grads")) + list(lands_early)
    halves = [_sum_slots(l, name="sum_" + nm) for l, nm in zip(lands, ("lru_w_in", "lru_w_out", "fox_w_in", "fox_w_out"))]
    big_g = _join_cores(halves, name="join_cores")

    small_g = [jnp.concatenate([dg0, dg1], axis=0), dgf.reshape(d), d_cw, d_cb, d_wa, d_ba, d_wx, d_bx, d_ap,
               db_f[:, :FOX_HEADS]]
    gsum = _allreduce_small(_pack(small_g), name="allreduce_small")
    zc = jnp.zeros((CONV_WIDTH, w), F32)
    pk_w = _pack([norm_g, final_g, zc, lru_conv_b, lru_wa, lru_ba, lru_wx, lru_bx, lru_a_param, fox_b_f])
    pk_m = _pack([m_norm_g, m_final_g, zc, m_lru_conv_b, m_lru_wa, m_lru_ba, m_lru_wx, m_lru_bx, m_lru_a_param,
                  m_fox_b_f])
    pk_v = _pack([v_norm_g, v_final_g, zc + 1.0, v_lru_conv_b, v_lru_wa, v_lru_ba, v_lru_wx, v_lru_bx,
                  v_lru_a_param, v_fox_b_f])
    s_delta, s_m, s_v = _adamw(pk_w[None], gsum[None], pk_m[None], pk_v[None], name="adamw_small")
    out_shapes = [norm_g.shape, final_g.shape, (CONV_WIDTH, w), lru_conv_b.shape, lru_wa.shape, lru_ba.shape,
                  lru_wx.shape, lru_bx.shape, lru_a_param.shape, fox_b_f.shape]
    sg = _unpack(gsum, out_shapes)
    sd = _unpack(s_delta, out_shapes)
    sm = _unpack(s_m, out_shapes)
    sv = _unpack(s_v, out_shapes)

    ncw = lru_conv_w.shape[2]
    g_cw_loc = lax.dynamic_slice_in_dim(sg[2], chip * ncw, ncw, axis=1)
    g_cw_loc = g_cw_loc[None]
    cw_d, cw_m, cw_v = _adamw(lru_conv_w, g_cw_loc, m_lru_conv_w, v_lru_conv_w, name="adamw_conv_w")

    big = []
    for nm, wt, g, mm, vv in (("lru_w_in", lru_w_in, big_g[0], m_lru_w_in, v_lru_w_in),
                              ("lru_w_out", lru_w_out, big_g[1], m_lru_w_out, v_lru_w_out),
                              ("fox_w_in", fox_w_in, big_g[2], m_fox_w_in, v_fox_w_in),
                              ("fox_w_out", fox_w_out, big_g[3], m_fox_w_out, v_fox_w_out)):
        big.append((g,) + tuple(_adamw(wt, g, mm, vv, name="adamw_" + nm)))

    def assemble(idx):
        small = (sg, sd, sm, sv)[idx]
        cw = (g_cw_loc, cw_d, cw_m, cw_v)[idx]
        return [small[0], small[1], big[0][idx], cw, small[3], small[4], small[5], small[6], small[7], small[8],
                big[1][idx], big[2][idx], small[9], big[3][idx]]

    grad_x = dx0.reshape(1, t, d)
    return (loss, grad_x, *assemble(0), *assemble(1), *assemble(2), *assemble(3))
```

```python
import functools
import math

import jax
import jax.numpy as jnp
from jax import lax
from jax.experimental import pallas as pl
from jax.experimental.pallas import tpu as pltpu

F32 = jnp.float32
BF16 = jnp.bfloat16

EPS = 1e-6
LRU_C = 8.0
LRU_BLOCK_W = 128
CONV_WIDTH = 4
FOX_HEADS = 16
FOX_HEAD_DIM = 64
NEG_INF = -1e30
ADAM_LR = 0.001
ADAM_B1 = 0.9
ADAM_B2 = 0.999
ADAM_EPS = 1e-08
ADAM_WD = 0.01
ADAM_STEP = 10

LANES = 128
SUBLANES = 8
VMEM_LIMIT = 56 * 1024 * 1024
TINY = 1e-30
N_CHIPS = 4
N_DEV = 8
MESH = pl.DeviceIdType.MESH
HIGHEST = lax.Precision.HIGHEST


def _tile(n, pref):
    t = min(n, pref)
    while n % t:
        t //= 2
    return t


def _cparams(dims=None):
    return pltpu.CompilerParams(dimension_semantics=dims, vmem_limit_bytes=VMEM_LIMIT)


def _sigmoid(x):
    return 0.5 * jnp.tanh(0.5 * x) + 0.5


def _log1p(x):
    u = 1.0 + x
    return jnp.where(u == 1.0, x, jnp.log(u) * x / (u - 1.0))


def _softplus(x):
    return jnp.maximum(x, 0.0) + _log1p(jnp.exp(-jnp.abs(x)))


MM_TILE = 1024
MM_FULL_K = 1536


def _matmul(a, b, *, name, ta=False, tb=False, out_dtype=F32, add=None, tm=MM_TILE, tn=MM_TILE, tk=None):
    if ta:
        kdim, m = a.shape
    else:
        m, kdim = a.shape
    if tb:
        n, kb = b.shape
    else:
        kb, n = b.shape
    assert kdim == kb, (a.shape, b.shape, ta, tb)
    if tk is None:
        tk = kdim if kdim <= MM_FULL_K else MM_TILE
    tm, tn, tk = _tile(m, tm), _tile(n, tn), _tile(kdim, tk)
    nk = kdim // tk
    dn = (((0 if ta else 1,), (1 if tb else 0,)), ((), ()))
    has_add = add is not None

    def body(*refs):
        if has_add:
            a_ref, b_ref, add_ref, o_ref = refs[:4]
        else:
            a_ref, b_ref, o_ref = refs[:3]
        part = lax.dot_general(a_ref[...].astype(BF16), b_ref[...].astype(BF16), dn, preferred_element_type=F32)

        def finish(r):
            if has_add:
                r = r + add_ref[...].astype(F32)
            o_ref[...] = r.astype(o_ref.dtype)

        if nk == 1:
            finish(part)
            return
        acc_ref = refs[-1]
        k = pl.program_id(2)

        @pl.when(k == 0)
        def _():
            acc_ref[...] = part

        @pl.when(k > 0)
        def _():
            acc_ref[...] += part

        @pl.when(k == nk - 1)
        def _():
            finish(acc_ref[...])

    a_spec = pl.BlockSpec((tk, tm), lambda i, j, k: (k, i)) if ta else pl.BlockSpec((tm, tk), lambda i, j, k: (i, k))
    b_spec = pl.BlockSpec((tn, tk), lambda i, j, k: (j, k)) if tb else pl.BlockSpec((tk, tn), lambda i, j, k: (k, j))
    o_spec = pl.BlockSpec((tm, tn), lambda i, j, k: (i, j))
    in_specs = [a_spec, b_spec] + ([o_spec] if has_add else [])
    args = (a, b) + ((add,) if has_add else ())
    return pl.pallas_call(
        body, name=name, grid=(m // tm, n // tn, nk), in_specs=in_specs, out_specs=o_spec,
        out_shape=jax.ShapeDtypeStruct((m, n), out_dtype),
        scratch_shapes=[pltpu.VMEM((tm, tn), F32)] if nk > 1 else [],
        compiler_params=_cparams(("parallel", "parallel", "arbitrary")),
    )(*args)


def _matmul_kparts(parts, b, *, chunk, name, tm=MM_TILE, tn=MM_TILE):
    npart = len(parts)
    m = parts[0].shape[0]
    n, kdim = b.shape
    nk = kdim // chunk
    assert nk * chunk == kdim and sum(p.shape[1] for p in parts) == kdim and nk % npart == 0
    tm, tn = _tile(m, tm), _tile(n, tn)
    dn = (((1,), (1,)), ((), ()))

    def body(*refs):
        a_refs, b_ref, o_ref, acc_ref = refs[:npart], refs[npart], refs[npart + 1], refs[npart + 2]
        k = pl.program_id(2)

        @pl.when(k == 0)
        def _():
            acc_ref[...] = jnp.zeros_like(acc_ref)

        for s in range(npart):
            @pl.when(lax.rem(k, npart) == s)
            def _(s=s):
                acc_ref[...] += lax.dot_general(a_refs[s][...].astype(BF16), b_ref[...].astype(BF16), dn,
                                                preferred_element_type=F32)

        @pl.when(k == nk - 1)
        def _():
            o_ref[...] = acc_ref[...].astype(o_ref.dtype)

    a_specs = [pl.BlockSpec((tm, chunk), lambda i, j, k: (i, k // npart)) for _ in range(npart)]
    return pl.pallas_call(
        body, name=name, grid=(m // tm, n // tn, nk),
        in_specs=a_specs + [pl.BlockSpec((tn, chunk), lambda i, j, k: (j, k))],
        out_specs=pl.BlockSpec((tm, tn), lambda i, j, k: (i, j)),
        out_shape=jax.ShapeDtypeStruct((m, n), F32),
        scratch_shapes=[pltpu.VMEM((tm, tn), F32)],
        compiler_params=_cparams(("parallel", "parallel", "arbitrary")),
    )(*parts, b)


def _matmul_nparts(a, parts, *, chunk, out_dtype, name, tm=MM_TILE, tk=MM_TILE):
    npart = len(parts)
    t, m = a.shape
    n = sum(p.shape[1] for p in parts)
    nj = n // chunk
    assert nj * chunk == n and nj % npart == 0
    tm, tk = _tile(m, tm), _tile(t, tk)
    nk = t // tk
    dn = (((0,), (0,)), ((), ()))

    def body(*refs):
        a_ref, b_refs, o_ref, acc_ref = refs[0], refs[1:1 + npart], refs[1 + npart], refs[2 + npart]
        j, k = pl.program_id(1), pl.program_id(2)

        @pl.when(k == 0)
        def _():
            acc_ref[...] = jnp.zeros_like(acc_ref)

        for s in range(npart):
            @pl.when(lax.rem(j, npart) == s)
            def _(s=s):
                acc_ref[...] += lax.dot_general(a_ref[...].astype(BF16), b_refs[s][...].astype(BF16), dn,
                                                preferred_element_type=F32)

        @pl.when(k == nk - 1)
        def _():
            o_ref[...] = acc_ref[...].astype(o_ref.dtype)

    def b_spec(s):
        return pl.BlockSpec((tk, chunk), lambda i, j, k: (jnp.where(lax.rem(j, npart) == s, k, 0), j // npart))

    return pl.pallas_call(
        body, name=name, grid=(m // tm, nj, nk),
        in_specs=[pl.BlockSpec((tk, tm), lambda i, j, k: (k, i))] + [b_spec(s) for s in range(npart)],
        out_specs=pl.BlockSpec((tm, chunk), lambda i, j, k: (i, j)),
        out_shape=jax.ShapeDtypeStruct((m, n), out_dtype),
        scratch_shapes=[pltpu.VMEM((tm, chunk), F32)],
        compiler_params=_cparams(("parallel", "parallel", "arbitrary")),
    )(a, *parts)


def _rmsnorm(x, g, *, name):
    t, d = x.shape
    tt = _tile(t, 512)

    def body(x_ref, g_ref, o_ref):
        xf = x_ref[...]
        rstd = lax.rsqrt(jnp.mean(xf * xf, axis=-1, keepdims=True) + EPS)
        o_ref[...] = (xf * rstd * g_ref[...]).astype(o_ref.dtype)

    return pl.pallas_call(
        body, name=name, grid=(t // tt,),
        in_specs=[pl.BlockSpec((tt, d), lambda i: (i, 0)), pl.BlockSpec((1, d), lambda i: (0, 0))],
        out_specs=pl.BlockSpec((tt, d), lambda i: (i, 0)),
        out_shape=jax.ShapeDtypeStruct((t, d), BF16),
        compiler_params=_cparams(("parallel",)),
    )(x, g.reshape(1, d))


def _rmsnorm_bwd(dh, x, g, dres, *, name):
    t, d = x.shape
    tt = _tile(t, 512)

    def body(dh_ref, x_ref, g_ref, dres_ref, dx_ref, dg_ref):
        i = pl.program_id(0)

        @pl.when(i == 0)
        def _():
            dg_ref[...] = jnp.zeros_like(dg_ref)

        xf = x_ref[...]
        rstd = lax.rsqrt(jnp.mean(xf * xf, axis=-1, keepdims=True) + EPS)
        xhat = xf * rstd
        dhf = dh_ref[...].astype(F32)
        dxhat = dhf * g_ref[...]
        mt = jnp.mean(dxhat * xhat, axis=-1, keepdims=True)
        dx_ref[...] = dres_ref[...] + rstd * (dxhat - xhat * mt)
        dg_ref[...] += jnp.sum(dhf * xhat, axis=0, keepdims=True)

    blk = pl.BlockSpec((tt, d), lambda i: (i, 0))
    vec = pl.BlockSpec((1, d), lambda i: (0, 0))
    return pl.pallas_call(
        body, name=name, grid=(t // tt,),
        in_specs=[blk, blk, vec, blk], out_specs=[blk, vec],
        out_shape=[jax.ShapeDtypeStruct((t, d), F32), jax.ShapeDtypeStruct((1, d), F32)],
        compiler_params=_cparams(("arbitrary",)),
    )(dh, x, g.reshape(1, d), dres)


def _final_loss(x2, tgt, g, *, name):
    t, d = x2.shape
    tt = _tile(t, 512)

    def body(x_ref, t_ref, g_ref, l_ref, dx_ref, dg_ref):
        i = pl.program_id(0)

        @pl.when(i == 0)
        def _():
            dg_ref[...] = jnp.zeros_like(dg_ref)
            l_ref[...] = jnp.zeros_like(l_ref)

        xf = x_ref[...]
        gg = g_ref[...]
        rstd = lax.rsqrt(jnp.mean(xf * xf, axis=-1, keepdims=True) + EPS)
        xhat = xf * rstd
        err = xhat * gg - t_ref[...]
        l_ref[...] += jnp.sum(err * err, axis=0, keepdims=True)
        dy = err * (1.0 / d)
        dxhat = dy * gg
        mt = jnp.mean(dxhat * xhat, axis=-1, keepdims=True)
        dx_ref[...] = rstd * (dxhat - xhat * mt)
        dg_ref[...] += jnp.sum(dy * xhat, axis=0, keepdims=True)

    blk = pl.BlockSpec((tt, d), lambda i: (i, 0))
    vec = pl.BlockSpec((1, d), lambda i: (0, 0))
    return pl.pallas_call(
        body, name=name, grid=(t // tt,),
        in_specs=[blk, blk, vec], out_specs=[vec, blk, vec],
        out_shape=[jax.ShapeDtypeStruct((1, d), F32), jax.ShapeDtypeStruct((t, d), F32),
                   jax.ShapeDtypeStruct((1, d), F32)],
        compiler_params=_cparams(("arbitrary",)),
    )(x2, tgt, g.reshape(1, d))


def _shift_down(prev8, cur, s):
    ext = jnp.concatenate([prev8, cur], axis=0)
    if s == 0:
        return cur
    return pltpu.roll(ext, s, 0)[SUBLANES:, :]


def _shift_up(cur, next8, s):
    if s == 0:
        return cur
    n = cur.shape[0]
    ext = jnp.concatenate([cur, next8], axis=0)
    return pltpu.roll(ext, n + SUBLANES - s, 0)[:n, :]


def _lru_gates(xc, wa, ba, wx, bx, sp):
    xcb = xc.astype(BF16)
    r = _sigmoid(jnp.dot(xcb, wa, preferred_element_type=F32) + ba)
    ig = _sigmoid(jnp.dot(xcb, wx, preferred_element_type=F32) + bx)
    log_a = -LRU_C * r * sp
    a = jnp.exp(log_a)
    z = -jnp.tanh(log_a) * (a * a + 1.0)
    inv_mult = lax.rsqrt(jnp.maximum(z, TINY))
    return r, ig, a, z * inv_mult, inv_mult


def _lru_specs(tt, cg, n_groups, nt, reverse):
    ncol = cg // LANES
    if reverse:
        ti = lambda i: nt - 1 - i
    else:
        ti = lambda i: i
    hb = tt // SUBLANES
    cur = lambda col: pl.BlockSpec((tt, cg), lambda g, i: (ti(i), 2 * g + col))
    prev = lambda col: pl.BlockSpec((SUBLANES, cg), lambda g, i: (jnp.maximum(ti(i) * hb - 1, 0), 2 * g + col))
    chan = lambda rows: pl.BlockSpec((rows, cg), lambda g, i: (0, g))
    wblk = pl.BlockSpec((ncol, LRU_BLOCK_W, LRU_BLOCK_W), lambda g, i: (g, 0, 0))
    plain = pl.BlockSpec((tt, cg), lambda g, i: (ti(i), g))
    plain_prev = pl.BlockSpec((SUBLANES, cg), lambda g, i: (jnp.maximum(ti(i) * hb - 1, 0), g))
    return cur, prev, chan, wblk, plain, plain_prev


def _lru_fwd(u, conv_w, conv_b, wa, ba, wx, bx, a_param, *, cg, name):
    t, w2 = u.shape
    w = w2 // 2
    n_groups = w // cg
    ncol = cg // LANES
    tt = _tile(t, 256)
    nt = t // tt
    cur, prev, chan, wblk, plain, _ = _lru_specs(tt, cg, n_groups, nt, False)

    def body(xb_ref, xp_ref, gate_ref, cw_ref, cb_ref, wa_ref, ba_ref, wx_ref, bx_ref, ap_ref,
             y_ref, hs_ref, h_ref, a_s, b_s):
        i = pl.program_id(1)

        @pl.when(i == 0)
        def _():
            h_ref[...] = jnp.zeros_like(h_ref)

        keep = (i > 0).astype(F32)
        for n in range(ncol):
            sl = slice(n * LANES, (n + 1) * LANES)
            xb = xb_ref[:, sl]
            xp = xp_ref[:, sl] * keep
            xc = cb_ref[:, sl] + cw_ref[3:4, sl] * xb
            for s in range(1, CONV_WIDTH):
                xc = xc + cw_ref[3 - s:4 - s, sl] * _shift_down(xp, xb, s)
            sp = _softplus(-ap_ref[:, sl])
            _, ig, a, mult, _ = _lru_gates(xc, wa_ref[n].astype(BF16), ba_ref[:, sl],
                                           wx_ref[n].astype(BF16), bx_ref[:, sl], sp)
            a_s[:, sl] = a
            b_s[:, sl] = mult * (ig * xc)

        def step(g, h):
            base = pl.multiple_of(g * SUBLANES, SUBLANES)
            for r in range(SUBLANES):
                h = a_s[pl.ds(base + r, 1), :] * h + b_s[pl.ds(base + r, 1), :]
                hs_ref[pl.ds(base + r, 1), :] = h
            return h

        h = lax.fori_loop(0, tt // SUBLANES, step, h_ref[0:1, :])
        h_ref[0:1, :] = h
        gate = gate_ref[...]
        y_ref[...] = (hs_ref[...] * (gate * _sigmoid(gate))).astype(y_ref.dtype)

    return pl.pallas_call(
        body, name=name, grid=(n_groups, nt),
        in_specs=[cur(0), prev(0), cur(1), chan(CONV_WIDTH), chan(1), wblk, chan(1), wblk, chan(1), chan(1)],
        out_specs=[plain, plain],
        out_shape=[jax.ShapeDtypeStruct((t, w), BF16), jax.ShapeDtypeStruct((t, w), F32)],
        scratch_shapes=[pltpu.VMEM((SUBLANES, cg), F32), pltpu.VMEM((tt, cg), F32), pltpu.VMEM((tt, cg), F32)],
        compiler_params=_cparams(("parallel", "arbitrary")),
    )(u, u, u, conv_w, conv_b, wa, ba, wx, bx, a_param)


def _lru_bwd(u, hs, dy, conv_w, conv_b, wa, ba, wx, bx, a_param, *, cg, name, ride=()):
    nride = len(ride)
    t, w2 = u.shape
    w = w2 // 2
    n_groups = w // cg
    ncol = cg // LANES
    tt = _tile(t, 256)
    nt = t // tt
    cur, prev, chan, wblk, plain, plain_prev = _lru_specs(tt, cg, n_groups, nt, True)
    tn_dims = (((0,), (0,)), ((), ()))
    nt_dims = (((1,), (1,)), ((), ()))

    def body(*refs):
        n_in, n_out, n_scr = 13, 9, 5
        ins, rest = refs[:n_in], refs[n_in:]
        ride_in, rest = rest[:nride], rest[nride:]
        outs, rest = rest[:n_out], rest[n_out:]
        ride_out, rest = rest[:nride], rest[nride:]
        scr, sems = rest[:n_scr], rest[n_scr:]
        if not nride:
            core(*ins, *outs, *scr)
            return
        step = pl.program_id(0) * nt + pl.program_id(1)

        @pl.when(step == 0)
        def _():
            for cp in _scatter_copies(ride_in, ride_out, *sems):
                cp.start()

        core(*ins, *outs, *scr)

        @pl.when(step == n_groups * nt - 1)
        def _():
            for cp in _scatter_copies(ride_in, ride_out, *sems):
                cp.wait()

    def core(xb_ref, xp_ref, gate_ref, hs_ref, hp_ref, dy_ref, cw_ref, cb_ref, wa_ref, ba_ref, wx_ref, bx_ref,
             ap_ref, dxb_ref, dgate_ref, dcw_ref, dcb_ref, dwa_ref, dba_ref, dwx_ref, dbx_ref, dsp_ref,
             c_ref, nx_ref, a_s, dhs_s, lam_s):
        i = pl.program_id(1)
        first_time_block = i == nt - 1

        @pl.when(i == 0)
        def _():
            c_ref[...] = jnp.zeros_like(c_ref)
            nx_ref[...] = jnp.zeros_like(nx_ref)
            for r in (dcw_ref, dcb_ref, dwa_ref, dba_ref, dwx_ref, dbx_ref, dsp_ref):
                r[...] = jnp.zeros_like(r)

        keep = jnp.where(first_time_block, 0.0, 1.0).astype(F32)
        gate = gate_ref[...]
        sg = _sigmoid(gate)
        dyv = dy_ref[...]
        hsv = hs_ref[...]
        dhs_s[...] = dyv * (gate * sg)
        dgate_ref[...] = (dyv * hsv * (sg * (1.0 + gate * (1.0 - sg)))).astype(dgate_ref.dtype)

        saved = []
        for n in range(ncol):
            sl = slice(n * LANES, (n + 1) * LANES)
            xb = xb_ref[:, sl]
            xp = xp_ref[:, sl] * keep
            shifted = [xb] + [_shift_down(xp, xb, s) for s in range(1, CONV_WIDTH)]
            xc = cb_ref[:, sl] + cw_ref[3:4, sl] * xb
            for s in range(1, CONV_WIDTH):
                xc = xc + cw_ref[3 - s:4 - s, sl] * shifted[s]
            sp = _softplus(-ap_ref[:, sl])
            wab = wa_ref[n].astype(BF16)
            wxb = wx_ref[n].astype(BF16)
            r, ig, a, mult, inv_mult = _lru_gates(xc, wab, ba_ref[:, sl], wxb, bx_ref[:, sl], sp)
            a_s[:, sl] = a
            saved.append((sl, shifted, xc, sp, wab, wxb, r, ig, a, mult, inv_mult))

        def step(g, c):
            base = pl.multiple_of(tt - SUBLANES - g * SUBLANES, SUBLANES)
            for r in range(SUBLANES - 1, -1, -1):
                lam = dhs_s[pl.ds(base + r, 1), :] + c
                lam_s[pl.ds(base + r, 1), :] = lam
                c = a_s[pl.ds(base + r, 1), :] * lam
            return c

        c_ref[0:1, :] = lax.fori_loop(0, tt // SUBLANES, step, c_ref[0:1, :])

        for n in range(ncol):
            sl, shifted, xc, sp, wab, wxb, r, ig, a, mult, inv_mult = saved[n]
            lam = lam_s[:, sl]
            hprev = _shift_down(hp_ref[:, sl] * keep, hs_ref[:, sl], 1)
            da = lam * hprev
            dmult = lam * (ig * xc)
            dlog_a = da * a - dmult * (a * a * inv_mult)
            di = lam * (mult * xc)
            dxc = lam * (mult * ig)
            dr = dlog_a * (-LRU_C * sp)
            dsp_ref[:, sl] += jnp.sum(dlog_a * (-LRU_C * r), axis=0, keepdims=True)
            dza = dr * (r * (1.0 - r))
            dzx = di * (ig * (1.0 - ig))
            dba_ref[:, sl] += jnp.sum(dza, axis=0, keepdims=True)
            dbx_ref[:, sl] += jnp.sum(dzx, axis=0, keepdims=True)
            xcb = xc.astype(BF16)
            dzab = dza.astype(BF16)
            dzxb = dzx.astype(BF16)
            dwa_ref[n] += lax.dot_general(xcb, dzab, tn_dims, preferred_element_type=F32)
            dwx_ref[n] += lax.dot_general(xcb, dzxb, tn_dims, preferred_element_type=F32)
            dxc = dxc + lax.dot_general(dzab, wab, nt_dims, preferred_element_type=F32)
            dxc = dxc + lax.dot_general(dzxb, wxb, nt_dims, preferred_element_type=F32)
            dcb_ref[:, sl] += jnp.sum(dxc, axis=0, keepdims=True)
            for s in range(CONV_WIDTH):
                dcw_ref[3 - s:4 - s, sl] += jnp.sum(dxc * shifted[s], axis=0, keepdims=True)
            nx = nx_ref[:, sl]
            dxb = cw_ref[3:4, sl] * dxc
            for s in range(1, CONV_WIDTH):
                dxb = dxb + cw_ref[3 - s:4 - s, sl] * _shift_up(dxc, nx, s)
            dxb_ref[:, sl] = dxb.astype(dxb_ref.dtype)
            nx_ref[:, sl] = dxc[0:SUBLANES, :]

        @pl.when(first_time_block)
        def _():
            dsp_ref[...] = dsp_ref[...] * (-_sigmoid(-ap_ref[...]))

    dxb_spec = pl.BlockSpec((tt, cg), lambda g, i: (nt - 1 - i, g))
    any_spec = pl.BlockSpec(memory_space=pl.ANY)
    ride_shape, ride_sems = _scatter_shapes(ride) if nride else ([], [])
    outs = pl.pallas_call(
        body, name=name, grid=(n_groups, nt),
        in_specs=[cur(0), prev(0), cur(1), plain, plain_prev, plain, chan(CONV_WIDTH), chan(1), wblk, chan(1), wblk,
                  chan(1), chan(1)] + [any_spec] * nride,
        out_specs=[dxb_spec, dxb_spec, chan(CONV_WIDTH), chan(1), wblk, chan(1), wblk, chan(1), chan(1)]
        + [any_spec] * nride,
        out_shape=[jax.ShapeDtypeStruct((t, w), BF16), jax.ShapeDtypeStruct((t, w), BF16),
                   jax.ShapeDtypeStruct(conv_w.shape, F32), jax.ShapeDtypeStruct(conv_b.shape, F32),
                   jax.ShapeDtypeStruct(wa.shape, F32), jax.ShapeDtypeStruct(ba.shape, F32),
                   jax.ShapeDtypeStruct(wx.shape, F32), jax.ShapeDtypeStruct(bx.shape, F32),
                   jax.ShapeDtypeStruct(a_param.shape, F32)] + ride_shape,
        scratch_shapes=[pltpu.VMEM((SUBLANES, cg), F32), pltpu.VMEM((SUBLANES, cg), F32),
                        pltpu.VMEM((tt, cg), F32), pltpu.VMEM((tt, cg), F32), pltpu.VMEM((tt, cg), F32)] + ride_sems,
        compiler_params=_cparams(("arbitrary", "arbitrary")),
    )(u, u, u, hs, hs, dy, conv_w, conv_b, wa, ba, wx, bx, a_param, *ride)
    return outs[:9], outs[9:]


def _fgate_fwd(f, b_f, *, name):
    t, n = f.shape
    tt = _tile(t, 256)
    width = FOX_HEADS * FOX_HEAD_DIM

    def body(f_ref, b_ref, cum_ref, wide_ref, carry_ref):
        i = pl.program_id(0)

        @pl.when(i == 0)
        def _():
            carry_ref[...] = jnp.zeros_like(carry_ref)

        z = f_ref[...] + b_ref[...]
        lf = jnp.minimum(z, 0.0) - _log1p(jnp.exp(-jnp.abs(z)))
        row = lax.broadcasted_iota(jnp.int32, (tt, tt), 0)
        col = lax.broadcasted_iota(jnp.int32, (tt, tt), 1)
        tri = (col <= row).astype(F32)
        cum = jnp.dot(tri, lf, precision=HIGHEST, preferred_element_type=F32) + carry_ref[0:1, :]
        cum_ref[...] = cum
        carry_ref[0:1, :] = cum[tt - 1:tt, :]
        head = lax.broadcasted_iota(jnp.int32, (n, width), 0)
        chan = lax.broadcasted_iota(jnp.int32, (n, width), 1) // FOX_HEAD_DIM
        wide_ref[...] = jnp.dot(cum, (head == chan).astype(F32), precision=HIGHEST, preferred_element_type=F32)

    return pl.pallas_call(
        body, name=name, grid=(t // tt,),
        in_specs=[pl.BlockSpec((tt, n), lambda i: (i, 0)), pl.BlockSpec((1, n), lambda i: (0, 0))],
        out_specs=[pl.BlockSpec((tt, n), lambda i: (i, 0)), pl.BlockSpec((tt, width), lambda i: (i, 0))],
        out_shape=[jax.ShapeDtypeStruct((t, n), F32), jax.ShapeDtypeStruct((t, width), F32)],
        scratch_shapes=[pltpu.VMEM((SUBLANES, n), F32)],
        compiler_params=_cparams(("arbitrary",)),
    )(f, b_f)


def _fgate_bwd(dcum, f, b_f, *, name):
    t, n = f.shape
    tt = _tile(t, 256)
    nt = t // tt

    def body(dc_ref, f_ref, b_ref, df_ref, db_ref, carry_ref):
        i = pl.program_id(0)

        @pl.when(i == 0)
        def _():
            carry_ref[...] = jnp.zeros_like(carry_ref)
            db_ref[...] = jnp.zeros_like(db_ref)

        row = lax.broadcasted_iota(jnp.int32, (tt, tt), 0)
        col = lax.broadcasted_iota(jnp.int32, (tt, tt), 1)
        triu = (col >= row).astype(F32)
        dlf = jnp.dot(triu, dc_ref[...], precision=HIGHEST, preferred_element_type=F32) + carry_ref[0:1, :]
        carry_ref[0:1, :] = dlf[0:1, :]
        z = f_ref[...] + b_ref[...]
        df = dlf * _sigmoid(-z)
        df_ref[...] = df
        db_ref[...] += jnp.sum(df, axis=0, keepdims=True)

    blk = pl.BlockSpec((tt, n), lambda i: (nt - 1 - i, 0))
    vec = pl.BlockSpec((1, n), lambda i: (0, 0))
    return pl.pallas_call(
        body, name=name, grid=(nt,),
        in_specs=[blk, blk, vec], out_specs=[blk, vec],
        out_shape=[jax.ShapeDtypeStruct((t, n), F32), jax.ShapeDtypeStruct((1, n), F32)],
        scratch_shapes=[pltpu.VMEM((SUBLANES, n), F32)],
        compiler_params=_cparams(("arbitrary",)),
    )(dcum, f, b_f)


def _attn_fwd(start, qkv, ckt, gate, *, name, tq, tk):
    t = qkv.shape[0]
    f = gate.shape[1]
    npair = f // LANES
    nq = t // tq
    ratio = tq // tk
    assert tq == ratio * tk and t == nq * tq
    scale = 1.0 / math.sqrt(FOX_HEAD_DIM)
    nt_dims = (((1,), (1,)), ((), ()))

    def body(start_ref, q_ref, k_ref, v_ref, ck_ref, g_ref, o_ref, y_ref, l_ref):
        i = pl.program_id(1)
        pair = pl.program_id(0)
        firsts = (start_ref[2 * pair, i], start_ref[2 * pair + 1, i])
        both = jnp.maximum(firsts[0], firsts[1])
        lane = lax.broadcasted_iota(jnp.int32, (tq, LANES), 1)
        lo = lane < FOX_HEAD_DIM
        q2 = q_ref[...] * scale
        qs = (jnp.where(lo, q2, 0).astype(BF16), jnp.where(lo, 0, q2).astype(BF16))
        row = lax.broadcasted_iota(jnp.int32, (tq, tk), 0)
        col = lax.broadcasted_iota(jnp.int32, (tq, tk), 1)

        def kv_step(j, carry, diag, heads=(0, 1)):
            off = pl.multiple_of(j * tk, tk)
            kj = k_ref[pl.ds(off, tk), :]
            vj = v_ref[pl.ds(off, tk), :]
            ck = ck_ref[:, pl.ds(off, tk)]
            new = list(carry)
            for h in heads:
                m, l, acc = carry[h]
                s = lax.dot_general(qs[h], kj, nt_dims, preferred_element_type=F32) - ck[h:h + 1, :]
                if diag is not None:
                    s = jnp.where(col + diag * tk <= row, s, NEG_INF)
                m_new = jnp.maximum(m, jnp.max(s, axis=-1, keepdims=True))
                alpha = jnp.exp(m - m_new)
                p = jnp.exp(s - m_new)
                l = alpha * l + jnp.sum(p, axis=-1, keepdims=True)
                acc = alpha * acc + jnp.dot(p.astype(BF16), vj, preferred_element_type=F32)
                new[h] = (m_new, l, acc)
            return tuple(new)

        carry = tuple((jnp.full((tq, 1), NEG_INF, F32), jnp.zeros((tq, 1), F32), jnp.zeros((tq, LANES), F32))
                      for _ in range(2))
        for h in range(2):
            carry = lax.fori_loop(firsts[h], both, lambda j, c, h=h: kv_step(j, c, None, (h,)), carry)
        carry = lax.fori_loop(both, i * ratio, lambda j, c: kv_step(j, c, None), carry)
        for d in range(ratio):
            carry = kv_step(i * ratio + d, carry, d)
        (m0, l0, a0), (m1, l1, a1) = carry
        o = jnp.where(lo, a0 / l0, a1 / l1)
        o_ref[...] = o
        gate_v = g_ref[...]
        y_ref[...] = (o * (gate_v * _sigmoid(gate_v))).astype(y_ref.dtype)
        lse_t = jnp.transpose(jnp.where(lo, m0 + jnp.log(l0), m1 + jnp.log(l1)))
        l_ref[0:1, :] = lse_t[0:1, :]
        l_ref[1:2, :] = lse_t[FOX_HEAD_DIM:FOX_HEAD_DIM + 1, :]

    blk = lambda base: pl.BlockSpec((tq, LANES), lambda p, i, s: (i, base + p))
    full = lambda base: pl.BlockSpec((t, LANES), lambda p, i, s: (0, base + p))
    return pl.pallas_call(
        body, name=name,
        grid_spec=pltpu.PrefetchScalarGridSpec(
            num_scalar_prefetch=1, grid=(npair, nq),
            in_specs=[blk(0), full(npair), full(2 * npair), pl.BlockSpec((None, 2, t), lambda p, i, s: (p, 0, 0)),
                      blk(0)],
            out_specs=[blk(0), blk(0), pl.BlockSpec((None, 2, tq), lambda p, i, s: (p, 0, i))]),
        out_shape=[jax.ShapeDtypeStruct((t, f), F32), jax.ShapeDtypeStruct((t, f), BF16),
                   jax.ShapeDtypeStruct((npair, 2, t), F32)],
        compiler_params=_cparams(("parallel", "arbitrary")),
    )(start, qkv, qkv, qkv, ckt, gate)


def _attn_bwd(end, qkv, do, lt, dt, cke, *, name):
    t, f = do.shape
    npair = f // LANES
    tk = _tile(t, ATTN_TILE)
    nk = t // tk
    scale = 1.0 / math.sqrt(FOX_HEAD_DIM)
    nt_dims = (((1,), (1,)), ((), ()))
    tn_dims = (((0,), (0,)), ((), ()))

    def body(end_ref, k_ref, v_ref, q_ref, do_ref, l_ref, d_ref, ck_ref, dq_out_ref, dk_ref, dv_ref, dck_ref, dcq_ref,
             dq_ref):
        j = pl.program_id(1)
        pair = pl.program_id(0)
        lasts = (end_ref[2 * pair, j], end_ref[2 * pair + 1, j])
        both = jnp.minimum(lasts[0], lasts[1])

        @pl.when(j == 0)
        def _():
            dq_ref[...] = jnp.zeros_like(dq_ref)
            dcq_ref[...] = jnp.zeros_like(dcq_ref)

        lane = lax.broadcasted_iota(jnp.int32, (tk, LANES), 1)
        lo = lane < FOX_HEAD_DIM
        sel = (lo, jnp.logical_not(lo))
        kj = k_ref[...]
        vj = v_ref[...]
        km = tuple(jnp.where(sel[h], kj, 0).astype(BF16) for h in range(2))
        ckv = ck_ref[...]
        ckh = (ckv[:, 0:1], ckv[:, FOX_HEAD_DIM:FOX_HEAD_DIM + 1])
        row = lax.broadcasted_iota(jnp.int32, (tk, tk), 0)
        col = lax.broadcasted_iota(jnp.int32, (tk, tk), 1)
        causal = row <= col

        def q_step(i, carry, masked, heads=(0, 1)):
            dk_acc, dv_acc, dck = carry
            off = pl.multiple_of(i * tk, tk)
            qi = q_ref[pl.ds(off, tk), :]
            doi = do_ref[pl.ds(off, tk), :]
            lrow = l_ref[:, pl.ds(off, tk)]
            drow = d_ref[:, pl.ds(off, tk)]
            dq_add = jnp.zeros((tk, LANES), F32)
            new_dck = list(dck)
            for h in heads:
                qm = jnp.where(sel[h], qi, 0).astype(BF16)
                dom = jnp.where(sel[h], doi, 0).astype(BF16)
                st = lax.dot_general(kj, qm, nt_dims, preferred_element_type=F32) * scale
                st = st - ckh[h] - lrow[h:h + 1, :]
                if masked:
                    st = jnp.where(causal, st, NEG_INF)
                pt = jnp.exp(st)
                dpt = lax.dot_general(vj, dom, nt_dims, preferred_element_type=F32)
                dst = pt * (dpt - drow[h:h + 1, :])
                ptb = pt.astype(BF16)
                dstb = dst.astype(BF16)
                dv_acc = dv_acc + jnp.dot(ptb, dom, preferred_element_type=F32)
                dk_acc = dk_acc + jnp.dot(dstb, qm, preferred_element_type=F32)
                dq_add = dq_add + lax.dot_general(dstb, km[h], tn_dims, preferred_element_type=F32)
                new_dck[h] = dck[h] - jnp.sum(dst, axis=-1, keepdims=True)
                dcq_ref[h:h + 1, pl.ds(off, tk)] += jnp.sum(dst, axis=0, keepdims=True)
            dq_ref[pl.ds(off, tk), :] += dq_add * scale
            return dk_acc, dv_acc, tuple(new_dck)

        zero = jnp.zeros((tk, LANES), F32)
        carry = (zero, zero, (jnp.zeros((tk, 1), F32), jnp.zeros((tk, 1), F32)))
        carry = q_step(j, carry, True)
        carry = lax.fori_loop(j + 1, both, lambda i, c: q_step(i, c, False), carry)
        for h in range(2):
            carry = lax.fori_loop(both, lasts[h], lambda i, c, h=h: q_step(i, c, False, (h,)), carry)
        dk_acc, dv_acc, dck = carry
        dk_ref[...] = (dk_acc * scale).astype(dk_ref.dtype)
        dv_ref[...] = dv_acc.astype(dv_ref.dtype)
        dck_t = jnp.transpose(jnp.where(lo, dck[0], dck[1]))
        dck_ref[0:1, :] = dck_t[0:1, :]
        dck_ref[1:2, :] = dck_t[FOX_HEAD_DIM:FOX_HEAD_DIM + 1, :]

        @pl.when(j == nk - 1)
        def _():
            dq_out_ref[...] = dq_ref[...].astype(dq_out_ref.dtype)

    blk = lambda base: pl.BlockSpec((tk, LANES), lambda p, j, e: (j, base + p))
    full = lambda base: pl.BlockSpec((t, LANES), lambda p, j, e: (0, base + p))
    rows = pl.BlockSpec((None, 2, t), lambda p, j, e: (p, 0, 0))
    return pl.pallas_call(
        body, name=name,
        grid_spec=pltpu.PrefetchScalarGridSpec(
            num_scalar_prefetch=1, grid=(npair, nk),
            in_specs=[blk(npair), blk(2 * npair), full(0), full(0), rows, rows, blk(0)],
            out_specs=[full(0), blk(0), blk(0), pl.BlockSpec((None, 2, tk), lambda p, j, e: (p, 0, j)), rows],
            scratch_shapes=[pltpu.VMEM((t, LANES), F32)]),
        out_shape=[jax.ShapeDtypeStruct((t, f), BF16), jax.ShapeDtypeStruct((t, f), BF16),
                   jax.ShapeDtypeStruct((t, f), BF16), jax.ShapeDtypeStruct((npair, 2, t), F32),
                   jax.ShapeDtypeStruct((npair, 2, t), F32)],
        compiler_params=_cparams(("parallel", "arbitrary")),
    )(end, qkv, qkv, qkv, do, lt, dt, cke)


ATTN_TILE = 512
ATTN_FWD_QUERIES = 512
EXP_ZERO = -104.0
BOUND_SLACK = 1.02


def _attn_row_stats(qkv, *, name):
    t = qkv.shape[0]
    f = qkv.shape[1] // 3
    tt = _tile(t, 512)

    def body(q_ref, k_ref, s_ref):
        q = q_ref[...].astype(F32)
        k = k_ref[...].astype(F32)
        chan = lax.broadcasted_iota(jnp.int32, (f, LANES), 0) // FOX_HEAD_DIM
        lane = lax.broadcasted_iota(jnp.int32, (f, LANES), 1)
        acc = jnp.zeros((tt, LANES), F32)
        for off, val in ((0, q * q), (FOX_HEADS, q * k), (2 * FOX_HEADS, k * k)):
            pick = (chan == lane - off).astype(BF16)
            acc = acc + jnp.dot(val.astype(BF16), pick, preferred_element_type=F32)
        s_ref[...] = acc

    return pl.pallas_call(
        body, name=name, grid=(t // tt,),
        in_specs=[pl.BlockSpec((tt, f), lambda i: (i, 0)), pl.BlockSpec((tt, f), lambda i: (i, 1))],
        out_specs=pl.BlockSpec((tt, LANES), lambda i: (i, 0)),
        out_shape=jax.ShapeDtypeStruct((t, LANES), F32),
        compiler_params=_cparams(("parallel",)),
    )(qkv, qkv)


def _attn_skip_tables(stats, cum16, tile):
    t = stats.shape[0]
    nb = t // tile
    scale = 1.0 / math.sqrt(FOX_HEAD_DIM)
    qn = jnp.sqrt(stats[:, :FOX_HEADS]) * scale
    sii = stats[:, FOX_HEADS:2 * FOX_HEADS] * scale - cum16
    kmax = jnp.max(jnp.sqrt(stats[:, 2 * FOX_HEADS:3 * FOX_HEADS]), axis=0, keepdims=True)
    arow = qn * kmax * BOUND_SLACK - sii + 0.5 * BOUND_SLACK
    a_blk = jnp.max(arow.reshape(nb, tile, FOX_HEADS), axis=1)
    c_blk = -cum16.reshape(nb, tile, FOX_HEADS)[:, tile - 1, :]
    dead = (a_blk[:, None, :] + c_blk[None, :, :]) < EXP_ZERO
    start_h = jnp.sum(dead.astype(jnp.int32), axis=1)
    blk = jnp.arange(nb, dtype=jnp.int32)
    start = jnp.minimum(start_h, blk[:, None]).T
    needs = start[:, :, None] <= blk[None, None, :]
    end = jnp.max(jnp.where(needs, blk[None, :, None] + 1, 0), axis=1)
    return start, jnp.maximum(end, blk[None, :] + 1)


def _fox_post_bwd(dy, o, gate, *, name):
    t, f = dy.shape
    tt = _tile(t, 512)

    def body(dy_ref, o_ref, g_ref, do_ref, dg_ref, dl_ref):
        g = g_ref[...]
        sg = _sigmoid(g)
        dyv = dy_ref[...]
        ov = o_ref[...]
        do = dyv * (g * sg)
        do_ref[...] = do.astype(do_ref.dtype)
        dg_ref[...] = (dyv * ov * (sg * (1.0 + g * (1.0 - sg)))).astype(dg_ref.dtype)
        chan = lax.broadcasted_iota(jnp.int32, (f, LANES), 0)
        head = lax.broadcasted_iota(jnp.int32, (f, LANES), 1)
        pick = (chan // FOX_HEAD_DIM == head).astype(F32)
        dl_ref[...] = jnp.dot(do * ov, pick, precision=HIGHEST, preferred_element_type=F32)

    blk = pl.BlockSpec((tt, f), lambda i: (i, 0))
    return pl.pallas_call(
        body, name=name, grid=(t // tt,),
        in_specs=[blk, blk, blk], out_specs=[blk, blk, pl.BlockSpec((tt, LANES), lambda i: (i, 0))],
        out_shape=[jax.ShapeDtypeStruct((t, f), BF16), jax.ShapeDtypeStruct((t, f), BF16),
                   jax.ShapeDtypeStruct((t, LANES), F32)],
        compiler_params=_cparams(("parallel",)),
    )(dy, o, gate)


def _adamw(w, g, m, v, *, name):
    _, r, c = w.shape
    tr = _tile(r, 256) if r % SUBLANES == 0 else r
    c1 = 1.0 - ADAM_B1 ** ADAM_STEP
    c2 = 1.0 - ADAM_B2 ** ADAM_STEP

    def body(w_ref, g_ref, m_ref, v_ref, d_ref, mo_ref, vo_ref):
        gv = g_ref[...]
        mn = ADAM_B1 * m_ref[...] + (1.0 - ADAM_B1) * gv
        vn = ADAM_B2 * v_ref[...] + (1.0 - ADAM_B2) * (gv * gv)
        mo_ref[...] = mn
        vo_ref[...] = vn
        d_ref[...] = -ADAM_LR * ((mn / c1) / (jnp.sqrt(vn / c2) + ADAM_EPS) + ADAM_WD * w_ref[...])

    blk = pl.BlockSpec((None, tr, c), lambda i: (0, i, 0))
    return pl.pallas_call(
        body, name=name, grid=(r // tr,), in_specs=[blk] * 4, out_specs=[blk] * 3,
        out_shape=[jax.ShapeDtypeStruct((1, r, c), F32)] * 3,
        compiler_params=_cparams(("parallel",)),
    )(w, g, m, v)


def _sum_slots(land, *, name):
    ns, r, c = land.shape
    tr = _tile(r, 64) if r % SUBLANES == 0 else r

    def body(l_ref, o_ref):
        acc = l_ref[0].astype(F32)
        for s in range(1, ns):
            acc = acc + l_ref[s].astype(F32)
        o_ref[...] = acc

    return pl.pallas_call(
        body, name=name, grid=(r // tr,),
        in_specs=[pl.BlockSpec((ns, tr, c), lambda i: (0, i, 0))],
        out_specs=pl.BlockSpec((tr, c), lambda i: (i, 0)),
        out_shape=jax.ShapeDtypeStruct((r, c), F32),
        compiler_params=_cparams(("parallel",)),
    )(land)


ANY = pl.BlockSpec(memory_space=pl.ANY)


def _flip(v, bit):
    return 1 - v if bit else v


def _gather_chips(shards, small, *, name):
    n = len(shards)
    rels = ((1, 0), (0, 1), (1, 1))

    def body(*refs):
        ins, small_in = refs[:n], refs[n]
        outs, small_out = refs[n + 1:2 * n + 1], refs[2 * n + 1]
        send, recv, loc = refs[2 * n + 2:]
        x, y, c = lax.axis_index("x"), lax.axis_index("y"), lax.axis_index("c")
        me = 2 * x + y
        sibling = (x, y, 1 - c)
        local = [pltpu.make_async_copy(ins[k], outs[k].at[me], loc.at[k]) for k in range(n)]
        local.append(pltpu.make_async_copy(small_in, small_out.at[me], loc.at[n]))
        for cp in local:
            cp.start()

        def rows(k):
            half = ins[k].shape[0] // 2
            return pl.ds(pl.multiple_of(c * half, SUBLANES), half)

        sends = []
        for r, (rx, ry) in enumerate(rels):
            to = (_flip(x, rx), _flip(y, ry), c)
            for k in range(n):
                cp = pltpu.make_async_remote_copy(
                    src_ref=ins[k].at[rows(k), :], dst_ref=outs[k].at[me, rows(k), :],
                    send_sem=send.at[r * n + k], recv_sem=recv.at[r * n + k], device_id=to, device_id_type=MESH)
                cp.start()
                sends.append(cp)
            cp = pltpu.make_async_remote_copy(
                src_ref=small_in, dst_ref=small_out.at[me], send_sem=send.at[6 * n + r], recv_sem=recv.at[6 * n + r],
                device_id=to, device_id_type=MESH)
            cp.start()
            sends.append(cp)
        for r, (rx, ry) in enumerate(rels):
            src_chip = 2 * _flip(x, rx) + _flip(y, ry)
            for k in range(n):
                landed = outs[k].at[src_chip, rows(k), :]
                sends[r * (n + 1) + k].wait_recv()
                cp = pltpu.make_async_remote_copy(
                    src_ref=landed, dst_ref=landed, send_sem=send.at[3 * n + r * n + k],
                    recv_sem=recv.at[3 * n + r * n + k], device_id=sibling, device_id_type=MESH)
                cp.start()
                sends.append(cp)
            sends[r * (n + 1) + n].wait_recv()
        for cp in sends[:3 * (n + 1)]:
            cp.wait_send()
        for cp in sends[3 * (n + 1):]:
            cp.wait()
        for cp in local:
            cp.wait()

    vmem = pl.BlockSpec(memory_space=pltpu.VMEM)
    return pl.pallas_call(
        body, name=name, in_specs=[vmem] * (n + 1), out_specs=[vmem] * (n + 1),
        out_shape=[jax.ShapeDtypeStruct((N_CHIPS,) + s.shape, s.dtype) for s in list(shards) + [small]],
        scratch_shapes=[pltpu.SemaphoreType.DMA((6 * n + 3,)), pltpu.SemaphoreType.DMA((6 * n + 3,)),
                        pltpu.SemaphoreType.DMA((n + 1,))],
        compiler_params=pltpu.CompilerParams(has_side_effects=True, vmem_limit_bytes=VMEM_LIMIT),
    )(*shards, small)


_RELS7 = tuple((r >> 2 & 1, r >> 1 & 1, r & 1) for r in range(1, N_DEV))


def _scatter_copies(ins, outs, send, recv, loc):
    n = len(ins)
    x, y, c = lax.axis_index("x"), lax.axis_index("y"), lax.axis_index("c")
    me = 4 * x + 2 * y + c

    def piece(k, px, py, pc):
        half = ins[k].shape[1] // 2
        return ins[k].at[2 * px + py, pl.ds(pc * half, half), :]

    copies = [pltpu.make_async_copy(piece(k, x, y, c), outs[k].at[me], loc.at[k]) for k in range(n)]
    for r, (rx, ry, rc) in enumerate(_RELS7):
        tx, ty, tc = _flip(x, rx), _flip(y, ry), _flip(c, rc)
        for k in range(n):
            copies.append(pltpu.make_async_remote_copy(
                src_ref=piece(k, tx, ty, tc), dst_ref=outs[k].at[me], send_sem=send.at[r * n + k],
                recv_sem=recv.at[r * n + k], device_id=(tx, ty, tc), device_id_type=MESH))
    return copies


def _scatter_shapes(grads):
    n = len(grads)
    out_shape = [jax.ShapeDtypeStruct((N_DEV, g.shape[1] // 2, g.shape[2]), g.dtype) for g in grads]
    sems = [pltpu.SemaphoreType.DMA((7 * n,)), pltpu.SemaphoreType.DMA((7 * n,)), pltpu.SemaphoreType.DMA((n,))]
    return out_shape, sems


def _scatter_pieces(grads, *, name):
    n = len(grads)

    def body(*refs):
        copies = _scatter_copies(refs[:n], refs[n:2 * n], *refs[2 * n:])
        for cp in copies:
            cp.start()
        for cp in copies:
            cp.wait()

    out_shape, sems = _scatter_shapes(grads)
    return pl.pallas_call(
        body, name=name, in_specs=[ANY] * n, out_specs=[ANY] * n, out_shape=out_shape, scratch_shapes=sems,
        compiler_params=pltpu.CompilerParams(has_side_effects=True),
    )(*grads)


def _join_cores(halves, *, name):
    n = len(halves)

    def body(*refs):
        ins, outs = refs[:n], refs[n:2 * n]
        send, recv, loc = refs[2 * n:]
        x, y, c = lax.axis_index("x"), lax.axis_index("y"), lax.axis_index("c")
        copies = []
        for k in range(n):
            half = ins[k].shape[0]
            mine = outs[k].at[0, pl.ds(c * half, half), :]
            cp = pltpu.make_async_copy(ins[k], mine, loc.at[k])
            cp.start()
            copies.append(cp)
            cp = pltpu.make_async_remote_copy(
                src_ref=ins[k], dst_ref=mine, send_sem=send.at[k], recv_sem=recv.at[k],
                device_id=(x, y, 1 - c), device_id_type=MESH)
            cp.start()
            copies.append(cp)
        for cp in copies:
            cp.wait()

    in_vmem = pl.BlockSpec(memory_space=pltpu.VMEM)
    return pl.pallas_call(
        body, name=name, in_specs=[in_vmem] * n, out_specs=[in_vmem] * n,
        out_shape=[jax.ShapeDtypeStruct((1, 2 * h.shape[0], h.shape[1]), h.dtype) for h in halves],
        scratch_shapes=[pltpu.SemaphoreType.DMA((n,)), pltpu.SemaphoreType.DMA((n,)), pltpu.SemaphoreType.DMA((n,))],
        compiler_params=pltpu.CompilerParams(has_side_effects=True, vmem_limit_bytes=VMEM_LIMIT),
    )(*halves)


def _allreduce_small(buf, *, name):
    r, n = buf.shape
    half = r // 2
    rels = ((1, 0), (0, 1), (1, 1))

    def body(in_ref, out_ref, sib_ref, chips_ref, send, recv):
        x, y, c = lax.axis_index("x"), lax.axis_index("y"), lax.axis_index("c")
        sibling = (x, y, 1 - c)
        chip = 2 * x + y
        rows = pl.ds(pl.multiple_of(c * half, SUBLANES), half)

        swap = pltpu.make_async_remote_copy(src_ref=in_ref, dst_ref=sib_ref, send_sem=send.at[0], recv_sem=recv.at[0],
                                            device_id=sibling, device_id_type=MESH)
        swap.start()
        swap.wait()
        chips_ref[chip] = in_ref[rows, :] + sib_ref[rows, :]

        sends = []
        for k, (rx, ry) in enumerate(rels):
            cp = pltpu.make_async_remote_copy(
                src_ref=chips_ref.at[chip], dst_ref=chips_ref.at[chip], send_sem=send.at[1 + k],
                recv_sem=recv.at[1 + k], device_id=(_flip(x, rx), _flip(y, ry), c), device_id_type=MESH)
            cp.start()
            sends.append(cp)
        for cp in sends:
            cp.wait()
        total = chips_ref[0]
        for s in range(1, N_CHIPS):
            total = total + chips_ref[s]
        out_ref[rows, :] = total

        back = pltpu.make_async_remote_copy(src_ref=out_ref.at[rows, :], dst_ref=out_ref.at[rows, :],
                                            send_sem=send.at[4], recv_sem=recv.at[4],
                                            device_id=sibling, device_id_type=MESH)
        back.start()
        back.wait()

    vmem = pl.BlockSpec(memory_space=pltpu.VMEM)
    return pl.pallas_call(
        body, name=name, in_specs=[vmem], out_specs=vmem,
        out_shape=jax.ShapeDtypeStruct((r, n), F32),
        scratch_shapes=[pltpu.VMEM((r, n), F32), pltpu.VMEM((N_CHIPS, half, n), F32),
                        pltpu.SemaphoreType.DMA((5,)), pltpu.SemaphoreType.DMA((5,))],
        compiler_params=pltpu.CompilerParams(has_side_effects=True, vmem_limit_bytes=VMEM_LIMIT),
    )(buf)


def _pack(arrs):
    flat = []
    for a in arrs:
        v = a.reshape(-1)
        pad = (-v.shape[0]) % LANES
        if pad:
            v = jnp.pad(v, (0, pad))
        flat.append(v)
    v = jnp.concatenate(flat)
    pad = (-v.shape[0]) % (LANES * SUBLANES)
    if pad:
        v = jnp.pad(v, (0, pad))
    return v.reshape(-1, LANES)


def _unpack(buf, shapes):
    v = buf.reshape(-1)
    out, off = [], 0
    for s in shapes:
        n = math.prod(s)
        out.append(v[off:off + n].reshape(s))
        off += n + (-n) % LANES
    return out


def kernel(x, norm_g, final_g, lru_w_in, lru_conv_w, lru_conv_b, lru_wa, lru_ba, lru_wx, lru_bx, lru_a_param, lru_w_out, fox_w_in, fox_b_f, fox_w_out, loss_target, m_norm_g, m_final_g, m_lru_w_in, m_lru_conv_w, m_lru_conv_b, m_lru_wa, m_lru_ba, m_lru_wx, m_lru_bx, m_lru_a_param, m_lru_w_out, m_fox_w_in, m_fox_b_f, m_fox_w_out, v_norm_g, v_final_g, v_lru_w_in, v_lru_conv_w, v_lru_conv_b, v_lru_wa, v_lru_ba, v_lru_wx, v_lru_bx, v_lru_a_param, v_lru_w_out, v_fox_w_in, v_fox_b_f, v_fox_w_out):
    t, d = x.shape[1], x.shape[2]
    w = lru_wa.shape[1] * LRU_BLOCK_W
    f = FOX_HEADS * FOX_HEAD_DIM
    npair = f // LANES
    x0 = x.reshape(t, d)
    tgt = loss_target.reshape(t, d)
    chip = 2 * lax.axis_index("x") + lax.axis_index("y")

    g_lwi, g_lwo, g_fwi, g_fwo, g_cw = _gather_chips(
        [lru_w_in[0].astype(BF16), lru_w_out[0].astype(BF16), fox_w_in[0].astype(BF16), fox_w_out[0].astype(BF16)],
        lru_conv_w[0], name="gather_weights")
    cg = w // 2
    lwi = jnp.concatenate([g_lwi[0], g_lwi[2], g_lwi[1], g_lwi[3]], axis=1)
    lwo = g_lwo.reshape(w, d)
    fwi = jnp.concatenate([g_fwi[s] for s in range(N_CHIPS)], axis=1)
    w_qkv, w_g2 = fwi[:, :3 * f], fwi[:, 3 * f:4 * f]
    w_f = jnp.pad(fwi[:, 4 * f:], ((0, 0), (0, LANES - FOX_HEADS)))
    fwo = g_fwo.reshape(f, d)
    conv_w = jnp.concatenate([g_cw[s] for s in range(N_CHIPS)], axis=1)
    conv_b, ba, bx, a_param = lru_conv_b, lru_ba, lru_bx, lru_a_param
    wa, wx = lru_wa[0], lru_wx[0]
    b_f = jnp.pad(fox_b_f, ((0, 0), (0, LANES - FOX_HEADS)))

    h0 = _rmsnorm(x0, norm_g[0], name="norm0")
    u = _matmul(h0, lwi, name="lru_in")
    y1, hs = _lru_fwd(u, conv_w, conv_b, wa, ba, wx, bx, a_param, cg=cg, name="lru_fwd")
    x1 = _matmul(y1, lwo, add=x0, name="lru_out")
    h1 = _rmsnorm(x1, norm_g[1], name="norm1")
    qkv = _matmul(h1, w_qkv, out_dtype=BF16, name="fox_qkv")
    gate2 = _matmul(h1, w_g2, name="fox_gate")
    flog = _matmul(h1, w_f, name="fox_f")
    cum, cke = _fgate_fwd(flog, b_f, name="fgate_fwd")
    cum16 = cum[:, :FOX_HEADS]
    ckt = cum16.T.reshape(npair, 2, t)
    a_tk, a_tq = _tile(t, ATTN_TILE), _tile(t, ATTN_FWD_QUERIES)
    a_start, a_end = _attn_skip_tables(_attn_row_stats(qkv, name="attn_row_stats"), cum16, a_tk)
    a_start_fwd = jnp.min(a_start.reshape(FOX_HEADS, t // a_tq, a_tq // a_tk), axis=2)
    o, y2, lse = _attn_fwd(a_start_fwd, qkv, ckt, gate2, name="attn_fwd", tq=a_tq, tk=a_tk)
    x2 = _matmul(y2, fwo, add=x1, name="fox_out")
    lsum, dx2, dgf = _final_loss(x2, tgt, final_g, name="final_loss")
    loss = lax.psum(0.5 * jnp.sum(lsum) / d, ("x", "y", "c"))

    d_fwo = _matmul(y2, dx2, ta=True, out_dtype=BF16, name="d_fox_w_out")
    dy2 = _matmul(dx2, fwo, tb=True, name="d_y2")
    do, dgate2, dl = _fox_post_bwd(dy2, o, gate2, name="fox_post_bwd")
    lt = lse
    dt = dl[:, :FOX_HEADS].T.reshape(npair, 2, t)
    dq, dk, dv, dck, dcq = _attn_bwd(a_end, qkv, do, lt, dt, cke, name="attn_bwd")
    dcum = jnp.pad((dck + dcq).reshape(FOX_HEADS, t).T, ((0, 0), (0, LANES - FOX_HEADS)))
    dflog, db_f = _fgate_bwd(dcum, flog, b_f, name="fgate_bwd")
    du2 = [dq, dk, dv, dgate2]
    dflog_b = dflog.astype(BF16)
    dh1 = _matmul_kparts(du2, fwi[:, :4 * f], chunk=f, name="d_h1_a")
    dh1 = _matmul(dflog_b, w_f, tb=True, add=dh1, name="d_h1_b")
    d_fwi_a = _matmul_nparts(h1, du2, chunk=f, out_dtype=BF16, name="d_fox_w_in_a")
    d_fwi_b = _matmul(h1, dflog_b, ta=True, out_dtype=BF16, name="d_fox_w_in_b")
    d_fwi = jnp.concatenate([d_fwi_a, d_fwi_b[:, :FOX_HEADS]], axis=1)
    dx1, dg1 = _rmsnorm_bwd(dh1, x1, norm_g[1], dx2, name="norm1_bwd")

    d_lwo = _matmul(y1, dx1, ta=True, out_dtype=BF16, name="d_lru_w_out")
    dy1 = _matmul(dx1, lwo, tb=True, name="d_y1")
    n_fwi = fox_w_in.shape[2]
    g_fwi4 = jnp.stack([d_fwi[:, s * n_fwi:(s + 1) * n_fwi] for s in range(N_CHIPS)])
    g_fwo4 = d_fwo.reshape(N_CHIPS, f // N_CHIPS, d)
    g_lwo4 = d_lwo.reshape(N_CHIPS, w // N_CHIPS, d)
    (dxb, dgate, d_cw, d_cb, d_wa, d_ba, d_wx, d_bx, d_ap), lands_early = _lru_bwd(
        u, hs, dy1, conv_w, conv_b, wa, ba, wx, bx, a_param, cg=cg, name="lru_bwd", ride=[g_lwo4, g_fwi4, g_fwo4])
    dh0 = _matmul_kparts([dxb, dgate], lwi, chunk=cg, name="d_h0")
    d_lwi_p = _matmul_nparts(h0, [dxb, dgate], chunk=cg, out_dtype=BF16, name="d_lru_w_in")
    dx0, dg0 = _rmsnorm_bwd(dh0, x0, norm_g[0], dx1, name="norm0_bwd")

    csz = cg
    g_lwi4 = jnp.stack([d_lwi_p[:, 0:csz], d_lwi_p[:, 2 * csz:3 * csz], d_lwi_p[:, csz:2 * csz],
                        d_lwi_p[:, 3 * csz:]])
    lands = list(_scatter_pieces([g_lwi4], name="scatter_grads")) + list(lands_early)
    halves = [_sum_slots(l, name="sum_" + nm) for l, nm in zip(lands, ("lru_w_in", "lru_w_out", "fox_w_in", "fox_w_out"))]
    big_g = _join_cores(halves, name="join_cores")

    small_g = [jnp.concatenate([dg0, dg1], axis=0), dgf.reshape(d), d_cw, d_cb, d_wa, d_ba, d_wx, d_bx, d_ap,
               db_f[:, :FOX_HEADS]]
    gsum = _allreduce_small(_pack(small_g), name="allreduce_small")
    zc = jnp.zeros((CONV_WIDTH, w), F32)
    pk_w = _pack([norm_g, final_g, zc, lru_conv_b, lru_wa, lru_ba, lru_wx, lru_bx, lru_a_param, fox_b_f])
    pk_m = _pack([m_norm_g, m_final_g, zc, m_lru_conv_b, m_lru_wa, m_lru_ba, m_lru_wx, m_lru_bx, m_lru_a_param,
                  m_fox_b_f])
    pk_v = _pack([v_norm_g, v_final_g, zc + 1.0, v_lru_conv_b, v_lru_wa, v_lru_ba, v_lru_wx, v_lru_bx,
                  v_lru_a_param, v_fox_b_f])
    s_delta, s_m, s_v = _adamw(pk_w[None], gsum[None], pk_m[None], pk_v[None], name="adamw_small")
    out_shapes = [norm_g.shape, final_g.shape, (CONV_WIDTH, w), lru_conv_b.shape, lru_wa.shape, lru_ba.shape,
                  lru_wx.shape, lru_bx.shape, lru_a_param.shape, fox_b_f.shape]
    sg = _unpack(gsum, out_shapes)
    sd = _unpack(s_delta, out_shapes)
    sm = _unpack(s_m, out_shapes)
    sv = _unpack(s_v, out_shapes)

    ncw = lru_conv_w.shape[2]
    g_cw_loc = lax.dynamic_slice_in_dim(sg[2], chip * ncw, ncw, axis=1)
    g_cw_loc = g_cw_loc[None]
    cw_d, cw_m, cw_v = _adamw(lru_conv_w, g_cw_loc, m_lru_conv_w, v_lru_conv_w, name="adamw_conv_w")

    big = []
    for nm, wt, g, mm, vv in (("lru_w_in", lru_w_in, big_g[0], m_lru_w_in, v_lru_w_in),
                              ("lru_w_out", lru_w_out, big_g[1], m_lru_w_out, v_lru_w_out),
                              ("fox_w_in", fox_w_in, big_g[2], m_fox_w_in, v_fox_w_in),
                              ("fox_w_out", fox_w_out, big_g[3], m_fox_w_out, v_fox_w_out)):
        big.append((g,) + tuple(_adamw(wt, g, mm, vv, name="adamw_" + nm)))

    def assemble(idx):
        small = (sg, sd, sm, sv)[idx]
        cw = (g_cw_loc, cw_d, cw_m, cw_v)[idx]
        return [small[0], small[1], big[0][idx], cw, small[3], small[4], small[5], small[6], small[7], small[8],
                big[1][idx], big[2][idx], small[9], big[3][idx]]

    grad_x = dx0.reshape(1, t, d)
    return (loss, grad_x, *assemble(0), *assemble(1), *assemble(2), *assemble(3))
```

```python
import functools
import math

import jax
import jax.numpy as jnp
from jax import lax
from jax.experimental import pallas as pl
from jax.experimental.pallas import tpu as pltpu

F32 = jnp.float32
BF16 = jnp.bfloat16

EPS = 1e-6
LRU_C = 8.0
LRU_BLOCK_W = 128
CONV_WIDTH = 4
FOX_HEADS = 16
FOX_HEAD_DIM = 64
NEG_INF = -1e30
ADAM_LR = 0.001
ADAM_B1 = 0.9
ADAM_B2 = 0.999
ADAM_EPS = 1e-08
ADAM_WD = 0.01
ADAM_STEP = 10

LANES = 128
SUBLANES = 8
VMEM_LIMIT = 56 * 1024 * 1024
TINY = 1e-30
N_CHIPS = 4
N_DEV = 8
MESH = pl.DeviceIdType.MESH
HIGHEST = lax.Precision.HIGHEST


def _tile(n, pref):
    t = min(n, pref)
    while n % t:
        t //= 2
    return t


def _cparams(dims=None):
    return pltpu.CompilerParams(dimension_semantics=dims, vmem_limit_bytes=VMEM_LIMIT)


def _sigmoid(x):
    return 0.5 * jnp.tanh(0.5 * x) + 0.5


def _log1p(x):
    u = 1.0 + x
    return jnp.where(u == 1.0, x, jnp.log(u) * x / (u - 1.0))


def _softplus(x):
    return jnp.maximum(x, 0.0) + _log1p(jnp.exp(-jnp.abs(x)))


MM_TILE = 1024
MM_FULL_K = 1536


def _matmul(a, b, *, name, ta=False, tb=False, out_dtype=F32, add=None, tm=MM_TILE, tn=MM_TILE, tk=None):
    if ta:
        kdim, m = a.shape
    else:
        m, kdim = a.shape
    if tb:
        n, kb = b.shape
    else:
        kb, n = b.shape
    assert kdim == kb, (a.shape, b.shape, ta, tb)
    if tk is None:
        tk = kdim if kdim <= MM_FULL_K else MM_TILE
    tm, tn, tk = _tile(m, tm), _tile(n, tn), _tile(kdim, tk)
    nk = kdim // tk
    dn = (((0 if ta else 1,), (1 if tb else 0,)), ((), ()))
    has_add = add is not None

    def body(*refs):
        if has_add:
            a_ref, b_ref, add_ref, o_ref = refs[:4]
        else:
            a_ref, b_ref, o_ref = refs[:3]
        part = lax.dot_general(a_ref[...].astype(BF16), b_ref[...].astype(BF16), dn, preferred_element_type=F32)

        def finish(r):
            if has_add:
                r = r + add_ref[...].astype(F32)
            o_ref[...] = r.astype(o_ref.dtype)

        if nk == 1:
            finish(part)
            return
        acc_ref = refs[-1]
        k = pl.program_id(2)

        @pl.when(k == 0)
        def _():
            acc_ref[...] = part

        @pl.when(k > 0)
        def _():
            acc_ref[...] += part

        @pl.when(k == nk - 1)
        def _():
            finish(acc_ref[...])

    a_spec = pl.BlockSpec((tk, tm), lambda i, j, k: (k, i)) if ta else pl.BlockSpec((tm, tk), lambda i, j, k: (i, k))
    b_spec = pl.BlockSpec((tn, tk), lambda i, j, k: (j, k)) if tb else pl.BlockSpec((tk, tn), lambda i, j, k: (k, j))
    o_spec = pl.BlockSpec((tm, tn), lambda i, j, k: (i, j))
    in_specs = [a_spec, b_spec] + ([o_spec] if has_add else [])
    args = (a, b) + ((add,) if has_add else ())
    return pl.pallas_call(
        body, name=name, grid=(m // tm, n // tn, nk), in_specs=in_specs, out_specs=o_spec,
        out_shape=jax.ShapeDtypeStruct((m, n), out_dtype),
        scratch_shapes=[pltpu.VMEM((tm, tn), F32)] if nk > 1 else [],
        compiler_params=_cparams(("parallel", "parallel", "arbitrary")),
    )(*args)


def _matmul_kparts(parts, b, *, chunk, name, tm=MM_TILE, tn=MM_TILE):
    npart = len(parts)
    m = parts[0].shape[0]
    n, kdim = b.shape
    nk = kdim // chunk
    assert nk * chunk == kdim and sum(p.shape[1] for p in parts) == kdim and nk % npart == 0
    tm, tn = _tile(m, tm), _tile(n, tn)
    dn = (((1,), (1,)), ((), ()))

    def body(*refs):
        a_refs, b_ref, o_ref, acc_ref = refs[:npart], refs[npart], refs[npart + 1], refs[npart + 2]
        k = pl.program_id(2)

        @pl.when(k == 0)
        def _():
            acc_ref[...] = jnp.zeros_like(acc_ref)

        for s in range(npart):
            @pl.when(lax.rem(k, npart) == s)
            def _(s=s):
                acc_ref[...] += lax.dot_general(a_refs[s][...].astype(BF16), b_ref[...].astype(BF16), dn,
                                                preferred_element_type=F32)

        @pl.when(k == nk - 1)
        def _():
            o_ref[...] = acc_ref[...].astype(o_ref.dtype)

    a_specs = [pl.BlockSpec((tm, chunk), lambda i, j, k: (i, k // npart)) for _ in range(npart)]
    return pl.pallas_call(
        body, name=name, grid=(m // tm, n // tn, nk),
        in_specs=a_specs + [pl.BlockSpec((tn, chunk), lambda i, j, k: (j, k))],
        out_specs=pl.BlockSpec((tm, tn), lambda i, j, k: (i, j)),
        out_shape=jax.ShapeDtypeStruct((m, n), F32),
        scratch_shapes=[pltpu.VMEM((tm, tn), F32)],
        compiler_params=_cparams(("parallel", "parallel", "arbitrary")),
    )(*parts, b)


def _matmul_nparts(a, parts, *, chunk, out_dtype, name, tm=MM_TILE, tk=MM_TILE):
    npart = len(parts)
    t, m = a.shape
    n = sum(p.shape[1] for p in parts)
    nj = n // chunk
    assert nj * chunk == n and nj % npart == 0
    tm, tk = _tile(m, tm), _tile(t, tk)
    nk = t // tk
    dn = (((0,), (0,)), ((), ()))

    def body(*refs):
        a_ref, b_refs, o_ref, acc_ref = refs[0], refs[1:1 + npart], refs[1 + npart], refs[2 + npart]
        j, k = pl.program_id(1), pl.program_id(2)

        @pl.when(k == 0)
        def _():
            acc_ref[...] = jnp.zeros_like(acc_ref)

        for s in range(npart):
            @pl.when(lax.rem(j, npart) == s)
            def _(s=s):
                acc_ref[...] += lax.dot_general(a_ref[...].astype(BF16), b_refs[s][...].astype(BF16), dn,
                                                preferred_element_type=F32)

        @pl.when(k == nk - 1)
        def _():
            o_ref[...] = acc_ref[...].astype(o_ref.dtype)

    def b_spec(s):
        return pl.BlockSpec((tk, chunk), lambda i, j, k: (jnp.where(lax.rem(j, npart) == s, k, 0), j // npart))

    return pl.pallas_call(
        body, name=name, grid=(m // tm, nj, nk),
        in_specs=[pl.BlockSpec((tk, tm), lambda i, j, k: (k, i))] + [b_spec(s) for s in range(npart)],
        out_specs=pl.BlockSpec((tm, chunk), lambda i, j, k: (i, j)),
        out_shape=jax.ShapeDtypeStruct((m, n), out_dtype),
        scratch_shapes=[pltpu.VMEM((tm, chunk), F32)],
        compiler_params=_cparams(("parallel", "parallel", "arbitrary")),
    )(a, *parts)


def _rmsnorm(x, g, *, name):
    t, d = x.shape
    tt = _tile(t, 512)

    def body(x_ref, g_ref, o_ref):
        xf = x_ref[...]
        rstd = lax.rsqrt(jnp.mean(xf * xf, axis=-1, keepdims=True) + EPS)
        o_ref[...] = (xf * rstd * g_ref[...]).astype(o_ref.dtype)

    return pl.pallas_call(
        body, name=name, grid=(t // tt,),
        in_specs=[pl.BlockSpec((tt, d), lambda i: (i, 0)), pl.BlockSpec((1, d), lambda i: (0, 0))],
        out_specs=pl.BlockSpec((tt, d), lambda i: (i, 0)),
        out_shape=jax.ShapeDtypeStruct((t, d), BF16),
        compiler_params=_cparams(("parallel",)),
    )(x, g.reshape(1, d))


def _rmsnorm_bwd(dh, x, g, dres, *, name):
    t, d = x.shape
    tt = _tile(t, 512)

    def body(dh_ref, x_ref, g_ref, dres_ref, dx_ref, dg_ref):
        i = pl.program_id(0)

        @pl.when(i == 0)
        def _():
            dg_ref[...] = jnp.zeros_like(dg_ref)

        xf = x_ref[...]
        rstd = lax.rsqrt(jnp.mean(xf * xf, axis=-1, keepdims=True) + EPS)
        xhat = xf * rstd
        dhf = dh_ref[...].astype(F32)
        dxhat = dhf * g_ref[...]
        mt = jnp.mean(dxhat * xhat, axis=-1, keepdims=True)
        dx_ref[...] = dres_ref[...] + rstd * (dxhat - xhat * mt)
        dg_ref[...] += jnp.sum(dhf * xhat, axis=0, keepdims=True)

    blk = pl.BlockSpec((tt, d), lambda i: (i, 0))
    vec = pl.BlockSpec((1, d), lambda i: (0, 0))
    return pl.pallas_call(
        body, name=name, grid=(t // tt,),
        in_specs=[blk, blk, vec, blk], out_specs=[blk, vec],
        out_shape=[jax.ShapeDtypeStruct((t, d), F32), jax.ShapeDtypeStruct((1, d), F32)],
        compiler_params=_cparams(("arbitrary",)),
    )(dh, x, g.reshape(1, d), dres)


def _final_loss(x2, tgt, g, *, name):
    t, d = x2.shape
    tt = _tile(t, 512)

    def body(x_ref, t_ref, g_ref, l_ref, dx_ref, dg_ref):
        i = pl.program_id(0)

        @pl.when(i == 0)
        def _():
            dg_ref[...] = jnp.zeros_like(dg_ref)
            l_ref[...] = jnp.zeros_like(l_ref)

        xf = x_ref[...]
        gg = g_ref[...]
        rstd = lax.rsqrt(jnp.mean(xf * xf, axis=-1, keepdims=True) + EPS)
        xhat = xf * rstd
        err = xhat * gg - t_ref[...]
        l_ref[...] += jnp.sum(err * err, axis=0, keepdims=True)
        dy = err * (1.0 / d)
        dxhat = dy * gg
        mt = jnp.mean(dxhat * xhat, axis=-1, keepdims=True)
        dx_ref[...] = rstd * (dxhat - xhat * mt)
        dg_ref[...] += jnp.sum(dy * xhat, axis=0, keepdims=True)

    blk = pl.BlockSpec((tt, d), lambda i: (i, 0))
    vec = pl.BlockSpec((1, d), lambda i: (0, 0))
    return pl.pallas_call(
        body, name=name, grid=(t // tt,),
        in_specs=[blk, blk, vec], out_specs=[vec, blk, vec],
        out_shape=[jax.ShapeDtypeStruct((1, d), F32), jax.ShapeDtypeStruct((t, d), F32),
                   jax.ShapeDtypeStruct((1, d), F32)],
        compiler_params=_cparams(("arbitrary",)),
    )(x2, tgt, g.reshape(1, d))


def _shift_down(prev8, cur, s):
    ext = jnp.concatenate([prev8, cur], axis=0)
    if s == 0:
        return cur
    return pltpu.roll(ext, s, 0)[SUBLANES:, :]


def _shift_up(cur, next8, s):
    if s == 0:
        return cur
    n = cur.shape[0]
    ext = jnp.concatenate([cur, next8], axis=0)
    return pltpu.roll(ext, n + SUBLANES - s, 0)[:n, :]


def _lru_gates(xc, wa, ba, wx, bx, sp):
    xcb = xc.astype(BF16)
    r = _sigmoid(jnp.dot(xcb, wa, preferred_element_type=F32) + ba)
    ig = _sigmoid(jnp.dot(xcb, wx, preferred_element_type=F32) + bx)
    log_a = -LRU_C * r * sp
    a = jnp.exp(log_a)
    z = -jnp.tanh(log_a) * (a * a + 1.0)
    inv_mult = lax.rsqrt(jnp.maximum(z, TINY))
    return r, ig, a, z * inv_mult, inv_mult


def _lru_specs(tt, cg, n_groups, nt, reverse):
    ncol = cg // LANES
    if reverse:
        ti = lambda i: nt - 1 - i
    else:
        ti = lambda i: i
    hb = tt // SUBLANES
    cur = lambda col: pl.BlockSpec((tt, cg), lambda g, i: (ti(i), 2 * g + col))
    prev = lambda col: pl.BlockSpec((SUBLANES, cg), lambda g, i: (jnp.maximum(ti(i) * hb - 1, 0), 2 * g + col))
    chan = lambda rows: pl.BlockSpec((rows, cg), lambda g, i: (0, g))
    wblk = pl.BlockSpec((ncol, LRU_BLOCK_W, LRU_BLOCK_W), lambda g, i: (g, 0, 0))
    plain = pl.BlockSpec((tt, cg), lambda g, i: (ti(i), g))
    plain_prev = pl.BlockSpec((SUBLANES, cg), lambda g, i: (jnp.maximum(ti(i) * hb - 1, 0), g))
    return cur, prev, chan, wblk, plain, plain_prev


def _chip_gather_copies(ins, outs, send, recv, loc):
    n = len(ins)
    x, y, c = lax.axis_index("x"), lax.axis_index("y"), lax.axis_index("c")
    me = 2 * x + y
    copies = [pltpu.make_async_copy(ins[k], outs[k].at[me], loc.at[k]) for k in range(n)]
    for r, (rx, ry) in enumerate(((1, 0), (0, 1), (1, 1))):
        for k in range(n):
            copies.append(pltpu.make_async_remote_copy(
                src_ref=ins[k], dst_ref=outs[k].at[me], send_sem=send.at[r * n + k], recv_sem=recv.at[r * n + k],
                device_id=(_flip(x, rx), _flip(y, ry), c), device_id_type=MESH))
    return copies


def _lru_fwd(u, conv_w, conv_b, wa, ba, wx, bx, a_param, *, cg, name, ride=()):
    nride = len(ride)
    t, w2 = u.shape
    w = w2 // 2
    n_groups = w // cg
    ncol = cg // LANES
    tt = _tile(t, 256)
    nt = t // tt
    cur, prev, chan, wblk, plain, _ = _lru_specs(tt, cg, n_groups, nt, False)

    def body(*refs):
        n_in, n_out, n_scr = 10, 2, 3
        ins, rest = refs[:n_in], refs[n_in:]
        ride_in, rest = rest[:nride], rest[nride:]
        outs, rest = rest[:n_out], rest[n_out:]
        ride_out, rest = rest[:nride], rest[nride:]
        scr, sems = rest[:n_scr], rest[n_scr:]
        if not nride:
            core(*ins, *outs, *scr)
            return
        step = pl.program_id(0) * nt + pl.program_id(1)

        @pl.when(step == 0)
        def _():
            for cp in _chip_gather_copies(ride_in, ride_out, *sems):
                cp.start()

        core(*ins, *outs, *scr)

        @pl.when(step == n_groups * nt - 1)
        def _():
            for cp in _chip_gather_copies(ride_in, ride_out, *sems):
                cp.wait()

    def core(xb_ref, xp_ref, gate_ref, cw_ref, cb_ref, wa_ref, ba_ref, wx_ref, bx_ref, ap_ref,
             y_ref, hs_ref, h_ref, a_s, b_s):
        i = pl.program_id(1)

        @pl.when(i == 0)
        def _():
            h_ref[...] = jnp.zeros_like(h_ref)

        keep = (i > 0).astype(F32)
        for n in range(ncol):
            sl = slice(n * LANES, (n + 1) * LANES)
            xb = xb_ref[:, sl]
            xp = xp_ref[:, sl] * keep
            xc = cb_ref[:, sl] + cw_ref[3:4, sl] * xb
            for s in range(1, CONV_WIDTH):
                xc = xc + cw_ref[3 - s:4 - s, sl] * _shift_down(xp, xb, s)
            sp = _softplus(-ap_ref[:, sl])
            _, ig, a, mult, _ = _lru_gates(xc, wa_ref[n].astype(BF16), ba_ref[:, sl],
                                           wx_ref[n].astype(BF16), bx_ref[:, sl], sp)
            a_s[:, sl] = a
            b_s[:, sl] = mult * (ig * xc)

        def step(g, h):
            base = pl.multiple_of(g * SUBLANES, SUBLANES)
            for r in range(SUBLANES):
                h = a_s[pl.ds(base + r, 1), :] * h + b_s[pl.ds(base + r, 1), :]
                hs_ref[pl.ds(base + r, 1), :] = h
            return h

        h = lax.fori_loop(0, tt // SUBLANES, step, h_ref[0:1, :])
        h_ref[0:1, :] = h
        gate = gate_ref[...]
        y_ref[...] = (hs_ref[...] * (gate * _sigmoid(gate))).astype(y_ref.dtype)

    any_spec = pl.BlockSpec(memory_space=pl.ANY)
    ride_sems = [pltpu.SemaphoreType.DMA((3 * nride,)), pltpu.SemaphoreType.DMA((3 * nride,)),
                 pltpu.SemaphoreType.DMA((nride,))] if nride else []
    outs = pl.pallas_call(
        body, name=name, grid=(n_groups, nt),
        in_specs=[cur(0), prev(0), cur(1), chan(CONV_WIDTH), chan(1), wblk, chan(1), wblk, chan(1), chan(1)]
        + [any_spec] * nride,
        out_specs=[plain, plain] + [any_spec] * nride,
        out_shape=[jax.ShapeDtypeStruct((t, w), BF16), jax.ShapeDtypeStruct((t, w), F32)]
        + [jax.ShapeDtypeStruct((N_CHIPS,) + r.shape, r.dtype) for r in ride],
        scratch_shapes=[pltpu.VMEM((SUBLANES, cg), F32), pltpu.VMEM((tt, cg), F32), pltpu.VMEM((tt, cg), F32)]
        + ride_sems,
        compiler_params=_cparams(("arbitrary", "arbitrary")),
    )(u, u, u, conv_w, conv_b, wa, ba, wx, bx, a_param, *ride)
    return outs[0], outs[1], outs[2:]


def _lru_bwd(u, hs, dy, conv_w, conv_b, wa, ba, wx, bx, a_param, *, cg, name, ride=()):
    nride = len(ride)
    t, w2 = u.shape
    w = w2 // 2
    n_groups = w // cg
    ncol = cg // LANES
    tt = _tile(t, 256)
    nt = t // tt
    cur, prev, chan, wblk, plain, plain_prev = _lru_specs(tt, cg, n_groups, nt, True)
    tn_dims = (((0,), (0,)), ((), ()))
    nt_dims = (((1,), (1,)), ((), ()))

    def body(*refs):
        n_in, n_out, n_scr = 13, 9, 5
        ins, rest = refs[:n_in], refs[n_in:]
        ride_in, rest = rest[:nride], rest[nride:]
        outs, rest = rest[:n_out], rest[n_out:]
        ride_out, rest = rest[:nride], rest[nride:]
        scr, sems = rest[:n_scr], rest[n_scr:]
        if not nride:
            core(*ins, *outs, *scr)
            return
        step = pl.program_id(0) * nt + pl.program_id(1)

        @pl.when(step == 0)
        def _():
            for cp in _scatter_copies(ride_in, ride_out, *sems):
                cp.start()

        core(*ins, *outs, *scr)

        @pl.when(step == n_groups * nt - 1)
        def _():
            for cp in _scatter_copies(ride_in, ride_out, *sems):
                cp.wait()

    def core(xb_ref, xp_ref, gate_ref, hs_ref, hp_ref, dy_ref, cw_ref, cb_ref, wa_ref, ba_ref, wx_ref, bx_ref,
             ap_ref, dxb_ref, dgate_ref, dcw_ref, dcb_ref, dwa_ref, dba_ref, dwx_ref, dbx_ref, dsp_ref,
             c_ref, nx_ref, a_s, dhs_s, lam_s):
        i = pl.program_id(1)
        first_time_block = i == nt - 1

        @pl.when(i == 0)
        def _():
            c_ref[...] = jnp.zeros_like(c_ref)
            nx_ref[...] = jnp.zeros_like(nx_ref)
            for r in (dcw_ref, dcb_ref, dwa_ref, dba_ref, dwx_ref, dbx_ref, dsp_ref):
                r[...] = jnp.zeros_like(r)

        keep = jnp.where(first_time_block, 0.0, 1.0).astype(F32)
        gate = gate_ref[...]
        sg = _sigmoid(gate)
        dyv = dy_ref[...]
        hsv = hs_ref[...]
        dhs_s[...] = dyv * (gate * sg)
        dgate_ref[...] = (dyv * hsv * (sg * (1.0 + gate * (1.0 - sg)))).astype(dgate_ref.dtype)

        saved = []
        for n in range(ncol):
            sl = slice(n * LANES, (n + 1) * LANES)
            xb = xb_ref[:, sl]
            xp = xp_ref[:, sl] * keep
            shifted = [xb] + [_shift_down(xp, xb, s) for s in range(1, CONV_WIDTH)]
            xc = cb_ref[:, sl] + cw_ref[3:4, sl] * xb
            for s in range(1, CONV_WIDTH):
                xc = xc + cw_ref[3 - s:4 - s, sl] * shifted[s]
            sp = _softplus(-ap_ref[:, sl])
            wab = wa_ref[n].astype(BF16)
            wxb = wx_ref[n].astype(BF16)
            r, ig, a, mult, inv_mult = _lru_gates(xc, wab, ba_ref[:, sl], wxb, bx_ref[:, sl], sp)
            a_s[:, sl] = a
            saved.append((sl, shifted, xc, sp, wab, wxb, r, ig, a, mult, inv_mult))

        def step(g, c):
            base = pl.multiple_of(tt - SUBLANES - g * SUBLANES, SUBLANES)
            for r in range(SUBLANES - 1, -1, -1):
                lam = dhs_s[pl.ds(base + r, 1), :] + c
                lam_s[pl.ds(base + r, 1), :] = lam
                c = a_s[pl.ds(base + r, 1), :] * lam
            return c

        c_ref[0:1, :] = lax.fori_loop(0, tt // SUBLANES, step, c_ref[0:1, :])

        for n in range(ncol):
            sl, shifted, xc, sp, wab, wxb, r, ig, a, mult, inv_mult = saved[n]
            lam = lam_s[:, sl]
            hprev = _shift_down(hp_ref[:, sl] * keep, hs_ref[:, sl], 1)
            da = lam * hprev
            dmult = lam * (ig * xc)
            dlog_a = da * a - dmult * (a * a * inv_mult)
            di = lam * (mult * xc)
            dxc = lam * (mult * ig)
            dr = dlog_a * (-LRU_C * sp)
            dsp_ref[:, sl] += jnp.sum(dlog_a * (-LRU_C * r), axis=0, keepdims=True)
            dza = dr * (r * (1.0 - r))
            dzx = di * (ig * (1.0 - ig))
            dba_ref[:, sl] += jnp.sum(dza, axis=0, keepdims=True)
            dbx_ref[:, sl] += jnp.sum(dzx, axis=0, keepdims=True)
            xcb = xc.astype(BF16)
            dzab = dza.astype(BF16)
            dzxb = dzx.astype(BF16)
            dwa_ref[n] += lax.dot_general(xcb, dzab, tn_dims, preferred_element_type=F32)
            dwx_ref[n] += lax.dot_general(xcb, dzxb, tn_dims, preferred_element_type=F32)
            dxc = dxc + lax.dot_general(dzab, wab, nt_dims, preferred_element_type=F32)
            dxc = dxc + lax.dot_general(dzxb, wxb, nt_dims, preferred_element_type=F32)
            dcb_ref[:, sl] += jnp.sum(dxc, axis=0, keepdims=True)
            for s in range(CONV_WIDTH):
                dcw_ref[3 - s:4 - s, sl] += jnp.sum(dxc * shifted[s], axis=0, keepdims=True)
            nx = nx_ref[:, sl]
            dxb = cw_ref[3:4, sl] * dxc
            for s in range(1, CONV_WIDTH):
                dxb = dxb + cw_ref[3 - s:4 - s, sl] * _shift_up(dxc, nx, s)
            dxb_ref[:, sl] = dxb.astype(dxb_ref.dtype)
            nx_ref[:, sl] = dxc[0:SUBLANES, :]

        @pl.when(first_time_block)
        def _():
            dsp_ref[...] = dsp_ref[...] * (-_sigmoid(-ap_ref[...]))

    dxb_spec = pl.BlockSpec((tt, cg), lambda g, i: (nt - 1 - i, g))
    any_spec = pl.BlockSpec(memory_space=pl.ANY)
    ride_shape, ride_sems = _scatter_shapes(ride) if nride else ([], [])
    outs = pl.pallas_call(
        body, name=name, grid=(n_groups, nt),
        in_specs=[cur(0), prev(0), cur(1), plain, plain_prev, plain, chan(CONV_WIDTH), chan(1), wblk, chan(1), wblk,
                  chan(1), chan(1)] + [any_spec] * nride,
        out_specs=[dxb_spec, dxb_spec, chan(CONV_WIDTH), chan(1), wblk, chan(1), wblk, chan(1), chan(1)]
        + [any_spec] * nride,
        out_shape=[jax.ShapeDtypeStruct((t, w), BF16), jax.ShapeDtypeStruct((t, w), BF16),
                   jax.ShapeDtypeStruct(conv_w.shape, F32), jax.ShapeDtypeStruct(conv_b.shape, F32),
                   jax.ShapeDtypeStruct(wa.shape, F32), jax.ShapeDtypeStruct(ba.shape, F32),
                   jax.ShapeDtypeStruct(wx.shape, F32), jax.ShapeDtypeStruct(bx.shape, F32),
                   jax.ShapeDtypeStruct(a_param.shape, F32)] + ride_shape,
        scratch_shapes=[pltpu.VMEM((SUBLANES, cg), F32), pltpu.VMEM((SUBLANES, cg), F32),
                        pltpu.VMEM((tt, cg), F32), pltpu.VMEM((tt, cg), F32), pltpu.VMEM((tt, cg), F32)] + ride_sems,
        compiler_params=_cparams(("arbitrary", "arbitrary")),
    )(u, u, u, hs, hs, dy, conv_w, conv_b, wa, ba, wx, bx, a_param, *ride)
    return outs[:9], outs[9:]


def _fgate_fwd(f, b_f, *, name):
    t, n = f.shape
    tt = _tile(t, 256)
    width = FOX_HEADS * FOX_HEAD_DIM

    def body(f_ref, b_ref, cum_ref, wide_ref, carry_ref):
        i = pl.program_id(0)

        @pl.when(i == 0)
        def _():
            carry_ref[...] = jnp.zeros_like(carry_ref)

        z = f_ref[...] + b_ref[...]
        lf = jnp.minimum(z, 0.0) - _log1p(jnp.exp(-jnp.abs(z)))
        row = lax.broadcasted_iota(jnp.int32, (tt, tt), 0)
        col = lax.broadcasted_iota(jnp.int32, (tt, tt), 1)
        tri = (col <= row).astype(F32)
        cum = jnp.dot(tri, lf, precision=HIGHEST, preferred_element_type=F32) + carry_ref[0:1, :]
        cum_ref[...] = cum
        carry_ref[0:1, :] = cum[tt - 1:tt, :]
        head = lax.broadcasted_iota(jnp.int32, (n, width), 0)
        chan = lax.broadcasted_iota(jnp.int32, (n, width), 1) // FOX_HEAD_DIM
        spread = (head == chan).astype(BF16)
        hi = cum.astype(BF16)
        rest = cum - hi.astype(F32)
        mid = rest.astype(BF16)
        low = (rest - mid.astype(F32)).astype(BF16)
        wide = jnp.dot(hi, spread, preferred_element_type=F32) + jnp.dot(mid, spread, preferred_element_type=F32)
        wide_ref[...] = wide + jnp.dot(low, spread, preferred_element_type=F32)

    return pl.pallas_call(
        body, name=name, grid=(t // tt,),
        in_specs=[pl.BlockSpec((tt, n), lambda i: (i, 0)), pl.BlockSpec((1, n), lambda i: (0, 0))],
        out_specs=[pl.BlockSpec((tt, n), lambda i: (i, 0)), pl.BlockSpec((tt, width), lambda i: (i, 0))],
        out_shape=[jax.ShapeDtypeStruct((t, n), F32), jax.ShapeDtypeStruct((t, width), F32)],
        scratch_shapes=[pltpu.VMEM((SUBLANES, n), F32)],
        compiler_params=_cparams(("arbitrary",)),
    )(f, b_f)


def _fgate_bwd(dcum, f, b_f, *, name):
    t, n = f.shape
    tt = _tile(t, 256)
    nt = t // tt

    def body(dc_ref, f_ref, b_ref, df_ref, db_ref, carry_ref):
        i = pl.program_id(0)

        @pl.when(i == 0)
        def _():
            carry_ref[...] = jnp.zeros_like(carry_ref)
            db_ref[...] = jnp.zeros_like(db_ref)

        row = lax.broadcasted_iota(jnp.int32, (tt, tt), 0)
        col = lax.broadcasted_iota(jnp.int32, (tt, tt), 1)
        triu = (col >= row).astype(F32)
        dlf = jnp.dot(triu, dc_ref[...], precision=HIGHEST, preferred_element_type=F32) + carry_ref[0:1, :]
        carry_ref[0:1, :] = dlf[0:1, :]
        z = f_ref[...] + b_ref[...]
        df = dlf * _sigmoid(-z)
        df_ref[...] = df
        db_ref[...] += jnp.sum(df, axis=0, keepdims=True)

    blk = pl.BlockSpec((tt, n), lambda i: (nt - 1 - i, 0))
    vec = pl.BlockSpec((1, n), lambda i: (0, 0))
    return pl.pallas_call(
        body, name=name, grid=(nt,),
        in_specs=[blk, blk, vec], out_specs=[blk, vec],
        out_shape=[jax.ShapeDtypeStruct((t, n), F32), jax.ShapeDtypeStruct((1, n), F32)],
        scratch_shapes=[pltpu.VMEM((SUBLANES, n), F32)],
        compiler_params=_cparams(("arbitrary",)),
    )(dcum, f, b_f)


def _attn_fwd(start, qkv, ckt, gate, *, name, tq, tk):
    t = qkv.shape[0]
    f = gate.shape[1]
    npair = f // LANES
    nq = t // tq
    ratio = tq // tk
    assert tq == ratio * tk and t == nq * tq
    scale = 1.0 / math.sqrt(FOX_HEAD_DIM)
    nt_dims = (((1,), (1,)), ((), ()))

    def body(start_ref, q_ref, k_ref, v_ref, ck_ref, g_ref, o_ref, y_ref, l_ref):
        i = pl.program_id(1)
        pair = pl.program_id(0)
        firsts = (start_ref[2 * pair, i], start_ref[2 * pair + 1, i])
        both = jnp.maximum(firsts[0], firsts[1])
        lane = lax.broadcasted_iota(jnp.int32, (tq, LANES), 1)
        lo = lane < FOX_HEAD_DIM
        q2 = q_ref[...] * scale
        qs = (jnp.where(lo, q2, 0).astype(BF16), jnp.where(lo, 0, q2).astype(BF16))
        row = lax.broadcasted_iota(jnp.int32, (tq, tk), 0)
        col = lax.broadcasted_iota(jnp.int32, (tq, tk), 1)

        def kv_step(j, carry, diag, heads=(0, 1)):
            off = pl.multiple_of(j * tk, tk)
            kj = k_ref[pl.ds(off, tk), :]
            vj = v_ref[pl.ds(off, tk), :]
            ck = ck_ref[:, pl.ds(off, tk)]
            new = list(carry)
            for h in heads:
                m, l, acc = carry[h]
                s = lax.dot_general(qs[h], kj, nt_dims, preferred_element_type=F32) - ck[h:h + 1, :]
                if diag is not None:
                    s = jnp.where(col + diag * tk <= row, s, NEG_INF)
                m_new = jnp.maximum(m, jnp.max(s, axis=-1, keepdims=True))
                alpha = jnp.exp(m - m_new)
                p = jnp.exp(s - m_new)
                l = alpha * l + jnp.sum(p, axis=-1, keepdims=True)
                acc = alpha * acc + jnp.dot(p.astype(BF16), vj, preferred_element_type=F32)
                new[h] = (m_new, l, acc)
            return tuple(new)

        carry = tuple((jnp.full((tq, 1), NEG_INF, F32), jnp.zeros((tq, 1), F32), jnp.zeros((tq, LANES), F32))
                      for _ in range(2))
        for h in range(2):
            carry = lax.fori_loop(firsts[h], both, lambda j, c, h=h: kv_step(j, c, None, (h,)), carry)
        carry = lax.fori_loop(both, i * ratio, lambda j, c: kv_step(j, c, None), carry)
        for d in range(ratio):
            carry = kv_step(i * ratio + d, carry, d)
        (m0, l0, a0), (m1, l1, a1) = carry
        o = jnp.where(lo, a0 / l0, a1 / l1)
        o_ref[...] = o
        gate_v = g_ref[...]
        y_ref[...] = (o * (gate_v * _sigmoid(gate_v))).astype(y_ref.dtype)
        lse_t = jnp.transpose(jnp.where(lo, m0 + jnp.log(l0), m1 + jnp.log(l1)))
        l_ref[0:1, :] = lse_t[0:1, :]
        l_ref[1:2, :] = lse_t[FOX_HEAD_DIM:FOX_HEAD_DIM + 1, :]

    blk = lambda base: pl.BlockSpec((tq, LANES), lambda p, i, s: (i, base + p))
    full = lambda base: pl.BlockSpec((t, LANES), lambda p, i, s: (0, base + p))
    return pl.pallas_call(
        body, name=name,
        grid_spec=pltpu.PrefetchScalarGridSpec(
            num_scalar_prefetch=1, grid=(npair, nq),
            in_specs=[blk(0), full(npair), full(2 * npair), pl.BlockSpec((None, 2, t), lambda p, i, s: (p, 0, 0)),
                      blk(0)],
            out_specs=[blk(0), blk(0), pl.BlockSpec((None, 2, tq), lambda p, i, s: (p, 0, i))]),
        out_shape=[jax.ShapeDtypeStruct((t, f), F32), jax.ShapeDtypeStruct((t, f), BF16),
                   jax.ShapeDtypeStruct((npair, 2, t), F32)],
        compiler_params=_cparams(("parallel", "arbitrary")),
    )(start, qkv, qkv, qkv, ckt, gate)


def _attn_bwd(end, qkv, do, lt, dt, cke, *, name):
    t, f = do.shape
    npair = f // LANES
    tk = _tile(t, ATTN_TILE)
    nk = t // tk
    scale = 1.0 / math.sqrt(FOX_HEAD_DIM)
    nt_dims = (((1,), (1,)), ((), ()))
    tn_dims = (((0,), (0,)), ((), ()))

    def body(end_ref, k_ref, v_ref, q_ref, do_ref, l_ref, d_ref, ck_ref, dq_out_ref, dk_ref, dv_ref, dck_ref, dcq_ref,
             dq_ref):
        j = pl.program_id(1)
        pair = pl.program_id(0)
        lasts = (end_ref[2 * pair, j], end_ref[2 * pair + 1, j])
        both = jnp.minimum(lasts[0], lasts[1])

        @pl.when(j == 0)
        def _():
            dq_ref[...] = jnp.zeros_like(dq_ref)
            dcq_ref[...] = jnp.zeros_like(dcq_ref)

        lane = lax.broadcasted_iota(jnp.int32, (tk, LANES), 1)
        lo = lane < FOX_HEAD_DIM
        sel = (lo, jnp.logical_not(lo))
        kj = k_ref[...]
        vj = v_ref[...]
        km = tuple(jnp.where(sel[h], kj, 0).astype(BF16) for h in range(2))
        ckv = ck_ref[...]
        ckh = (ckv[:, 0:1], ckv[:, FOX_HEAD_DIM:FOX_HEAD_DIM + 1])
        row = lax.broadcasted_iota(jnp.int32, (tk, tk), 0)
        col = lax.broadcasted_iota(jnp.int32, (tk, tk), 1)
        causal = row <= col

        def q_step(i, carry, masked, heads=(0, 1)):
            dk_acc, dv_acc, dck = carry
            off = pl.multiple_of(i * tk, tk)
            qi = q_ref[pl.ds(off, tk), :]
            doi = do_ref[pl.ds(off, tk), :]
            lrow = l_ref[:, pl.ds(off, tk)]
            drow = d_ref[:, pl.ds(off, tk)]
            dq_add = jnp.zeros((tk, LANES), F32)
            new_dck = list(dck)
            for h in heads:
                qm = jnp.where(sel[h], qi, 0).astype(BF16)
                dom = jnp.where(sel[h], doi, 0).astype(BF16)
                st = lax.dot_general(kj, qm, nt_dims, preferred_element_type=F32) * scale
                st = st - ckh[h] - lrow[h:h + 1, :]
                if masked:
                    st = jnp.where(causal, st, NEG_INF)
                pt = jnp.exp(st)
                dpt = lax.dot_general(vj, dom, nt_dims, preferred_element_type=F32)
                dst = pt * (dpt - drow[h:h + 1, :])
                ptb = pt.astype(BF16)
                dstb = dst.astype(BF16)
                dv_acc = dv_acc + jnp.dot(ptb, dom, preferred_element_type=F32)
                dk_acc = dk_acc + jnp.dot(dstb, qm, preferred_element_type=F32)
                dq_add = dq_add + lax.dot_general(dstb, km[h], tn_dims, preferred_element_type=F32)
                new_dck[h] = dck[h] - jnp.sum(dst, axis=-1, keepdims=True)
                dcq_ref[h:h + 1, pl.ds(off, tk)] += jnp.sum(dst, axis=0, keepdims=True)
            dq_ref[pl.ds(off, tk), :] += dq_add * scale
            return dk_acc, dv_acc, tuple(new_dck)

        zero = jnp.zeros((tk, LANES), F32)
        carry = (zero, zero, (jnp.zeros((tk, 1), F32), jnp.zeros((tk, 1), F32)))
        carry = q_step(j, carry, True)
        carry = lax.fori_loop(j + 1, both, lambda i, c: q_step(i, c, False), carry)
        for h in range(2):
            carry = lax.fori_loop(both, lasts[h], lambda i, c, h=h: q_step(i, c, False, (h,)), carry)
        dk_acc, dv_acc, dck = carry
        dk_ref[...] = (dk_acc * scale).astype(dk_ref.dtype)
        dv_ref[...] = dv_acc.astype(dv_ref.dtype)
        dck_t = jnp.transpose(jnp.where(lo, dck[0], dck[1]))
        dck_ref[0:1, :] = dck_t[0:1, :]
        dck_ref[1:2, :] = dck_t[FOX_HEAD_DIM:FOX_HEAD_DIM + 1, :]

        @pl.when(j == nk - 1)
        def _():
            dq_out_ref[...] = dq_ref[...].astype(dq_out_ref.dtype)

    blk = lambda base: pl.BlockSpec((tk, LANES), lambda p, j, e: (j, base + p))
    full = lambda base: pl.BlockSpec((t, LANES), lambda p, j, e: (0, base + p))
    rows = pl.BlockSpec((None, 2, t), lambda p, j, e: (p, 0, 0))
    return pl.pallas_call(
        body, name=name,
        grid_spec=pltpu.PrefetchScalarGridSpec(
            num_scalar_prefetch=1, grid=(npair, nk),
            in_specs=[blk(npair), blk(2 * npair), full(0), full(0), rows, rows, blk(0)],
            out_specs=[full(0), blk(0), blk(0), pl.BlockSpec((None, 2, tk), lambda p, j, e: (p, 0, j)), rows],
            scratch_shapes=[pltpu.VMEM((t, LANES), F32)]),
        out_shape=[jax.ShapeDtypeStruct((t, f), BF16), jax.ShapeDtypeStruct((t, f), BF16),
                   jax.ShapeDtypeStruct((t, f), BF16), jax.ShapeDtypeStruct((npair, 2, t), F32),
                   jax.ShapeDtypeStruct((npair, 2, t), F32)],
        compiler_params=_cparams(("parallel", "arbitrary")),
    )(end, qkv, qkv, qkv, do, lt, dt, cke)


ATTN_TILE = 512
ATTN_FWD_QUERIES = 512
EXP_ZERO = -104.0
BOUND_SLACK = 1.02


def _attn_row_stats(qkv, *, name):
    t = qkv.shape[0]
    f = qkv.shape[1] // 3
    tt = _tile(t, 512)

    def body(q_ref, k_ref, s_ref):
        q = q_ref[...].astype(F32)
        k = k_ref[...].astype(F32)
        chan = lax.broadcasted_iota(jnp.int32, (f, LANES), 0) // FOX_HEAD_DIM
        lane = lax.broadcasted_iota(jnp.int32, (f, LANES), 1)
        acc = jnp.zeros((tt, LANES), F32)
        for off, val in ((0, q * q), (FOX_HEADS, q * k), (2 * FOX_HEADS, k * k)):
            pick = (chan == lane - off).astype(BF16)
            acc = acc + jnp.dot(val.astype(BF16), pick, preferred_element_type=F32)
        s_ref[...] = acc

    return pl.pallas_call(
        body, name=name, grid=(t // tt,),
        in_specs=[pl.BlockSpec((tt, f), lambda i: (i, 0)), pl.BlockSpec((tt, f), lambda i: (i, 1))],
        out_specs=pl.BlockSpec((tt, LANES), lambda i: (i, 0)),
        out_shape=jax.ShapeDtypeStruct((t, LANES), F32),
        compiler_params=_cparams(("parallel",)),
    )(qkv, qkv)


def _attn_skip_tables(stats, cum16, tile):
    t = stats.shape[0]
    nb = t // tile
    scale = 1.0 / math.sqrt(FOX_HEAD_DIM)
    qn = jnp.sqrt(stats[:, :FOX_HEADS]) * scale
    sii = stats[:, FOX_HEADS:2 * FOX_HEADS] * scale - cum16
    kmax = jnp.max(jnp.sqrt(stats[:, 2 * FOX_HEADS:3 * FOX_HEADS]), axis=0, keepdims=True)
    arow = qn * kmax * BOUND_SLACK - sii + 0.5 * BOUND_SLACK
    a_blk = jnp.max(arow.reshape(nb, tile, FOX_HEADS), axis=1)
    c_blk = -cum16.reshape(nb, tile, FOX_HEADS)[:, tile - 1, :]
    dead = (a_blk[:, None, :] + c_blk[None, :, :]) < EXP_ZERO
    start_h = jnp.sum(dead.astype(jnp.int32), axis=1)
    blk = jnp.arange(nb, dtype=jnp.int32)
    start = jnp.minimum(start_h, blk[:, None]).T
    needs = start[:, :, None] <= blk[None, None, :]
    end = jnp.max(jnp.where(needs, blk[None, :, None] + 1, 0), axis=1)
    return start, jnp.maximum(end, blk[None, :] + 1)


def _fox_post_bwd(dy, o, gate, *, name):
    t, f = dy.shape
    tt = _tile(t, 512)

    def body(dy_ref, o_ref, g_ref, do_ref, dg_ref, dl_ref):
        g = g_ref[...]
        sg = _sigmoid(g)
        dyv = dy_ref[...]
        ov = o_ref[...]
        do = dyv * (g * sg)
        do_ref[...] = do.astype(do_ref.dtype)
        dg_ref[...] = (dyv * ov * (sg * (1.0 + g * (1.0 - sg)))).astype(dg_ref.dtype)
        chan = lax.broadcasted_iota(jnp.int32, (f, LANES), 0)
        head = lax.broadcasted_iota(jnp.int32, (f, LANES), 1)
        pick = (chan // FOX_HEAD_DIM == head).astype(F32)
        dl_ref[...] = jnp.dot(do * ov, pick, precision=HIGHEST, preferred_element_type=F32)

    blk = pl.BlockSpec((tt, f), lambda i: (i, 0))
    return pl.pallas_call(
        body, name=name, grid=(t // tt,),
        in_specs=[blk, blk, blk], out_specs=[blk, blk, pl.BlockSpec((tt, LANES), lambda i: (i, 0))],
        out_shape=[jax.ShapeDtypeStruct((t, f), BF16), jax.ShapeDtypeStruct((t, f), BF16),
                   jax.ShapeDtypeStruct((t, LANES), F32)],
        compiler_params=_cparams(("parallel",)),
    )(dy, o, gate)


def _adamw(w, g, m, v, *, name):
    _, r, c = w.shape
    tr = _tile(r, 256) if r % SUBLANES == 0 else r
    c1 = 1.0 - ADAM_B1 ** ADAM_STEP
    c2 = 1.0 - ADAM_B2 ** ADAM_STEP

    def body(w_ref, g_ref, m_ref, v_ref, d_ref, mo_ref, vo_ref):
        gv = g_ref[...]
        mn = ADAM_B1 * m_ref[...] + (1.0 - ADAM_B1) * gv
        vn = ADAM_B2 * v_ref[...] + (1.0 - ADAM_B2) * (gv * gv)
        mo_ref[...] = mn
        vo_ref[...] = vn
        d_ref[...] = -ADAM_LR * ((mn / c1) / (jnp.sqrt(vn / c2) + ADAM_EPS) + ADAM_WD * w_ref[...])

    blk = pl.BlockSpec((None, tr, c), lambda i: (0, i, 0))
    return pl.pallas_call(
        body, name=name, grid=(r // tr,), in_specs=[blk] * 4, out_specs=[blk] * 3,
        out_shape=[jax.ShapeDtypeStruct((1, r, c), F32)] * 3,
        compiler_params=_cparams(("parallel",)),
    )(w, g, m, v)


def _sum_slots(land, *, name):
    ns, r, c = land.shape
    tr = _tile(r, 64) if r % SUBLANES == 0 else r

    def body(l_ref, o_ref):
        acc = l_ref[0].astype(F32)
        for s in range(1, ns):
            acc = acc + l_ref[s].astype(F32)
        o_ref[...] = acc

    return pl.pallas_call(
        body, name=name, grid=(r // tr,),
        in_specs=[pl.BlockSpec((ns, tr, c), lambda i: (0, i, 0))],
        out_specs=pl.BlockSpec((tr, c), lambda i: (i, 0)),
        out_shape=jax.ShapeDtypeStruct((r, c), F32),
        compiler_params=_cparams(("parallel",)),
    )(land)


ANY = pl.BlockSpec(memory_space=pl.ANY)


def _flip(v, bit):
    return 1 - v if bit else v


def _gather_chips(shards, small, *, name):
    n = len(shards)
    rels = ((1, 0), (0, 1), (1, 1))

    def body(*refs):
        ins, small_in = refs[:n], refs[n]
        outs, small_out = refs[n + 1:2 * n + 1], refs[2 * n + 1]
        send, recv, loc = refs[2 * n + 2:]
        x, y, c = lax.axis_index("x"), lax.axis_index("y"), lax.axis_index("c")
        me = 2 * x + y
        sibling = (x, y, 1 - c)
        local = [pltpu.make_async_copy(ins[k], outs[k].at[me], loc.at[k]) for k in range(n)]
        local.append(pltpu.make_async_copy(small_in, small_out.at[me], loc.at[n]))
        for cp in local:
            cp.start()

        def rows(k):
            half = ins[k].shape[0] // 2
            return pl.ds(pl.multiple_of(c * half, SUBLANES), half)

        sends = []
        for r, (rx, ry) in enumerate(rels):
            to = (_flip(x, rx), _flip(y, ry), c)
            for k in range(n):
                cp = pltpu.make_async_remote_copy(
                    src_ref=ins[k].at[rows(k), :], dst_ref=outs[k].at[me, rows(k), :],
                    send_sem=send.at[r * n + k], recv_sem=recv.at[r * n + k], device_id=to, device_id_type=MESH)
                cp.start()
                sends.append(cp)
            cp = pltpu.make_async_remote_copy(
                src_ref=small_in, dst_ref=small_out.at[me], send_sem=send.at[6 * n + r], recv_sem=recv.at[6 * n + r],
                device_id=to, device_id_type=MESH)
            cp.start()
            sends.append(cp)
        for r, (rx, ry) in enumerate(rels):
            src_chip = 2 * _flip(x, rx) + _flip(y, ry)
            for k in range(n):
                landed = outs[k].at[src_chip, rows(k), :]
                sends[r * (n + 1) + k].wait_recv()
                cp = pltpu.make_async_remote_copy(
                    src_ref=landed, dst_ref=landed, send_sem=send.at[3 * n + r * n + k],
                    recv_sem=recv.at[3 * n + r * n + k], device_id=sibling, device_id_type=MESH)
                cp.start()
                sends.append(cp)
            sends[r * (n + 1) + n].wait_recv()
        for cp in sends[:3 * (n + 1)]:
            cp.wait_send()
        for cp in sends[3 * (n + 1):]:
            cp.wait()
        for cp in local:
            cp.wait()

    vmem = pl.BlockSpec(memory_space=pltpu.VMEM)
    return pl.pallas_call(
        body, name=name, in_specs=[vmem] * (n + 1), out_specs=[vmem] * (n + 1),
        out_shape=[jax.ShapeDtypeStruct((N_CHIPS,) + s.shape, s.dtype) for s in list(shards) + [small]],
        scratch_shapes=[pltpu.SemaphoreType.DMA((6 * n + 3,)), pltpu.SemaphoreType.DMA((6 * n + 3,)),
                        pltpu.SemaphoreType.DMA((n + 1,))],
        compiler_params=pltpu.CompilerParams(has_side_effects=True, vmem_limit_bytes=VMEM_LIMIT),
    )(*shards, small)


_RELS7 = tuple((r >> 2 & 1, r >> 1 & 1, r & 1) for r in range(1, N_DEV))


def _scatter_copies(ins, outs, send, recv, loc):
    n = len(ins)
    x, y, c = lax.axis_index("x"), lax.axis_index("y"), lax.axis_index("c")
    me = 4 * x + 2 * y + c

    def piece(k, px, py, pc):
        half = ins[k].shape[1] // 2
        return ins[k].at[2 * px + py, pl.ds(pc * half, half), :]

    copies = [pltpu.make_async_copy(piece(k, x, y, c), outs[k].at[me], loc.at[k]) for k in range(n)]
    for r, (rx, ry, rc) in enumerate(_RELS7):
        tx, ty, tc = _flip(x, rx), _flip(y, ry), _flip(c, rc)
        for k in range(n):
            copies.append(pltpu.make_async_remote_copy(
                src_ref=piece(k, tx, ty, tc), dst_ref=outs[k].at[me], send_sem=send.at[r * n + k],
                recv_sem=recv.at[r * n + k], device_id=(tx, ty, tc), device_id_type=MESH))
    return copies


def _scatter_shapes(grads):
    n = len(grads)
    out_shape = [jax.ShapeDtypeStruct((N_DEV, g.shape[1] // 2, g.shape[2]), g.dtype) for g in grads]
    sems = [pltpu.SemaphoreType.DMA((7 * n,)), pltpu.SemaphoreType.DMA((7 * n,)), pltpu.SemaphoreType.DMA((n,))]
    return out_shape, sems


def _scatter_pieces(grads, *, name):
    n = len(grads)

    def body(*refs):
        copies = _scatter_copies(refs[:n], refs[n:2 * n], *refs[2 * n:])
        for cp in copies:
            cp.start()
        for cp in copies:
            cp.wait()

    out_shape, sems = _scatter_shapes(grads)
    return pl.pallas_call(
        body, name=name, in_specs=[ANY] * n, out_specs=[ANY] * n, out_shape=out_shape, scratch_shapes=sems,
        compiler_params=pltpu.CompilerParams(has_side_effects=True),
    )(*grads)


def _join_cores(halves, *, name):
    n = len(halves)

    def body(*refs):
        ins, outs = refs[:n], refs[n:2 * n]
        send, recv, loc = refs[2 * n:]
        x, y, c = lax.axis_index("x"), lax.axis_index("y"), lax.axis_index("c")
        copies = []
        for k in range(n):
            half = ins[k].shape[0]
            mine = outs[k].at[0, pl.ds(c * half, half), :]
            cp = pltpu.make_async_copy(ins[k], mine, loc.at[k])
            cp.start()
            copies.append(cp)
            cp = pltpu.make_async_remote_copy(
                src_ref=ins[k], dst_ref=mine, send_sem=send.at[k], recv_sem=recv.at[k],
                device_id=(x, y, 1 - c), device_id_type=MESH)
            cp.start()
            copies.append(cp)
        for cp in copies:
            cp.wait()

    in_vmem = pl.BlockSpec(memory_space=pltpu.VMEM)
    return pl.pallas_call(
        body, name=name, in_specs=[in_vmem] * n, out_specs=[in_vmem] * n,
        out_shape=[jax.ShapeDtypeStruct((1, 2 * h.shape[0], h.shape[1]), h.dtype) for h in halves],
        scratch_shapes=[pltpu.SemaphoreType.DMA((n,)), pltpu.SemaphoreType.DMA((n,)), pltpu.SemaphoreType.DMA((n,))],
        compiler_params=pltpu.CompilerParams(has_side_effects=True, vmem_limit_bytes=VMEM_LIMIT),
    )(*halves)


def _allreduce_small(buf, *, name):
    r, n = buf.shape
    half = r // 2
    rels = ((1, 0), (0, 1), (1, 1))

    def body(in_ref, out_ref, sib_ref, chips_ref, send, recv):
        x, y, c = lax.axis_index("x"), lax.axis_index("y"), lax.axis_index("c")
        sibling = (x, y, 1 - c)
        chip = 2 * x + y
        rows = pl.ds(pl.multiple_of(c * half, SUBLANES), half)

        swap = pltpu.make_async_remote_copy(src_ref=in_ref, dst_ref=sib_ref, send_sem=send.at[0], recv_sem=recv.at[0],
                                            device_id=sibling, device_id_type=MESH)
        swap.start()
        swap.wait()
        chips_ref[chip] = in_ref[rows, :] + sib_ref[rows, :]

        sends = []
        for k, (rx, ry) in enumerate(rels):
            cp = pltpu.make_async_remote_copy(
                src_ref=chips_ref.at[chip], dst_ref=chips_ref.at[chip], send_sem=send.at[1 + k],
                recv_sem=recv.at[1 + k], device_id=(_flip(x, rx), _flip(y, ry), c), device_id_type=MESH)
            cp.start()
            sends.append(cp)
        for cp in sends:
            cp.wait()
        total = chips_ref[0]
        for s in range(1, N_CHIPS):
            total = total + chips_ref[s]
        out_ref[rows, :] = total

        back = pltpu.make_async_remote_copy(src_ref=out_ref.at[rows, :], dst_ref=out_ref.at[rows, :],
                                            send_sem=send.at[4], recv_sem=recv.at[4],
                                            device_id=sibling, device_id_type=MESH)
        back.start()
        back.wait()

    vmem = pl.BlockSpec(memory_space=pltpu.VMEM)
    return pl.pallas_call(
        body, name=name, in_specs=[vmem], out_specs=vmem,
        out_shape=jax.ShapeDtypeStruct((r, n), F32),
        scratch_shapes=[pltpu.VMEM((r, n), F32), pltpu.VMEM((N_CHIPS, half, n), F32),
                        pltpu.SemaphoreType.DMA((5,)), pltpu.SemaphoreType.DMA((5,))],
        compiler_params=pltpu.CompilerParams(has_side_effects=True, vmem_limit_bytes=VMEM_LIMIT),
    )(buf)


def _pack(arrs):
    flat = []
    for a in arrs:
        v = a.reshape(-1)
        pad = (-v.shape[0]) % LANES
        if pad:
            v = jnp.pad(v, (0, pad))
        flat.append(v)
    v = jnp.concatenate(flat)
    pad = (-v.shape[0]) % (LANES * SUBLANES)
    if pad:
        v = jnp.pad(v, (0, pad))
    return v.reshape(-1, LANES)


def _unpack(buf, shapes):
    v = buf.reshape(-1)
    out, off = [], 0
    for s in shapes:
        n = math.prod(s)
        out.append(v[off:off + n].reshape(s))
        off += n + (-n) % LANES
    return out


def kernel(x, norm_g, final_g, lru_w_in, lru_conv_w, lru_conv_b, lru_wa, lru_ba, lru_wx, lru_bx, lru_a_param, lru_w_out, fox_w_in, fox_b_f, fox_w_out, loss_target, m_norm_g, m_final_g, m_lru_w_in, m_lru_conv_w, m_lru_conv_b, m_lru_wa, m_lru_ba, m_lru_wx, m_lru_bx, m_lru_a_param, m_lru_w_out, m_fox_w_in, m_fox_b_f, m_fox_w_out, v_norm_g, v_final_g, v_lru_w_in, v_lru_conv_w, v_lru_conv_b, v_lru_wa, v_lru_ba, v_lru_wx, v_lru_bx, v_lru_a_param, v_lru_w_out, v_fox_w_in, v_fox_b_f, v_fox_w_out):
    t, d = x.shape[1], x.shape[2]
    w = lru_wa.shape[1] * LRU_BLOCK_W
    f = FOX_HEADS * FOX_HEAD_DIM
    npair = f // LANES
    x0 = x.reshape(t, d)
    tgt = loss_target.reshape(t, d)
    chip = 2 * lax.axis_index("x") + lax.axis_index("y")

    g_lwi, g_lwo, g_cw = _gather_chips(
        [lru_w_in[0].astype(BF16), lru_w_out[0].astype(BF16)], lru_conv_w[0], name="gather_weights")
    cg = w // 2
    lwi = jnp.concatenate([g_lwi[0], g_lwi[2], g_lwi[1], g_lwi[3]], axis=1)
    lwo = g_lwo.reshape(w, d)
    conv_w = jnp.concatenate([g_cw[s] for s in range(N_CHIPS)], axis=1)
    conv_b, ba, bx, a_param = lru_conv_b, lru_ba, lru_bx, lru_a_param
    wa, wx = lru_wa[0], lru_wx[0]
    b_f = jnp.pad(fox_b_f, ((0, 0), (0, LANES - FOX_HEADS)))

    h0 = _rmsnorm(x0, norm_g[0], name="norm0")
    u = _matmul(h0, lwi, name="lru_in")
    y1, hs, (g_fwi, g_fwo) = _lru_fwd(u, conv_w, conv_b, wa, ba, wx, bx, a_param, cg=cg, name="lru_fwd",
                                      ride=[fox_w_in[0].astype(BF16), fox_w_out[0].astype(BF16)])
    fwi = jnp.concatenate([g_fwi[s] for s in range(N_CHIPS)], axis=1)
    w_qkv, w_g2 = fwi[:, :3 * f], fwi[:, 3 * f:4 * f]
    w_f = jnp.pad(fwi[:, 4 * f:], ((0, 0), (0, LANES - FOX_HEADS)))
    fwo = g_fwo.reshape(f, d)
    x1 = _matmul(y1, lwo, add=x0, name="lru_out")
    h1 = _rmsnorm(x1, norm_g[1], name="norm1")
    qkv = _matmul(h1, w_qkv, out_dtype=BF16, name="fox_qkv")
    gate2 = _matmul(h1, w_g2, name="fox_gate")
    flog = _matmul(h1, w_f, name="fox_f")
    cum, cke = _fgate_fwd(flog, b_f, name="fgate_fwd")
    cum16 = cum[:, :FOX_HEADS]
    ckt = cum16.T.reshape(npair, 2, t)
    a_tk, a_tq = _tile(t, ATTN_TILE), _tile(t, ATTN_FWD_QUERIES)
    a_start, a_end = _attn_skip_tables(_attn_row_stats(qkv, name="attn_row_stats"), cum16, a_tk)
    a_start_fwd = jnp.min(a_start.reshape(FOX_HEADS, t // a_tq, a_tq // a_tk), axis=2)
    o, y2, lse = _attn_fwd(a_start_fwd, qkv, ckt, gate2, name="attn_fwd", tq=a_tq, tk=a_tk)
    x2 = _matmul(y2, fwo, add=x1, name="fox_out")
    lsum, dx2, dgf = _final_loss(x2, tgt, final_g, name="final_loss")
    loss = lax.psum(0.5 * jnp.sum(lsum) / d, ("x", "y", "c"))

    d_fwo = _matmul(y2, dx2, ta=True, out_dtype=BF16, name="d_fox_w_out")
    dy2 = _matmul(dx2, fwo, tb=True, name="d_y2")
    do, dgate2, dl = _fox_post_bwd(dy2, o, gate2, name="fox_post_bwd")
    lt = lse
    dt = dl[:, :FOX_HEADS].T.reshape(npair, 2, t)
    dq, dk, dv, dck, dcq = _attn_bwd(a_end, qkv, do, lt, dt, cke, name="attn_bwd")
    dcum = jnp.pad((dck + dcq).reshape(FOX_HEADS, t).T, ((0, 0), (0, LANES - FOX_HEADS)))
    dflog, db_f = _fgate_bwd(dcum, flog, b_f, name="fgate_bwd")
    du2 = [dq, dk, dv, dgate2]
    dflog_b = dflog.astype(BF16)
    dh1 = _matmul_kparts(du2, fwi[:, :4 * f], chunk=f, name="d_h1_a")
    dh1 = _matmul(dflog_b, w_f, tb=True, add=dh1, name="d_h1_b")
    d_fwi_a = _matmul_nparts(h1, du2, chunk=f, out_dtype=BF16, name="d_fox_w_in_a")
    d_fwi_b = _matmul(h1, dflog_b, ta=True, out_dtype=BF16, name="d_fox_w_in_b")
    d_fwi = jnp.concatenate([d_fwi_a, d_fwi_b[:, :FOX_HEADS]], axis=1)
    dx1, dg1 = _rmsnorm_bwd(dh1, x1, norm_g[1], dx2, name="norm1_bwd")

    d_lwo = _matmul(y1, dx1, ta=True, out_dtype=BF16, name="d_lru_w_out")
    dy1 = _matmul(dx1, lwo, tb=True, name="d_y1")
    n_fwi = fox_w_in.shape[2]
    g_fwi4 = jnp.stack([d_fwi[:, s * n_fwi:(s + 1) * n_fwi] for s in range(N_CHIPS)])
    g_fwo4 = d_fwo.reshape(N_CHIPS, f // N_CHIPS, d)
    g_lwo4 = d_lwo.reshape(N_CHIPS, w // N_CHIPS, d)
    (dxb, dgate, d_cw, d_cb, d_wa, d_ba, d_wx, d_bx, d_ap), lands_early = _lru_bwd(
        u, hs, dy1, conv_w, conv_b, wa, ba, wx, bx, a_param, cg=cg, name="lru_bwd", ride=[g_lwo4, g_fwi4, g_fwo4])
    dh0 = _matmul_kparts([dxb, dgate], lwi, chunk=cg, name="d_h0")
    d_lwi_p = _matmul_nparts(h0, [dxb, dgate], chunk=cg, out_dtype=BF16, name="d_lru_w_in")
    dx0, dg0 = _rmsnorm_bwd(dh0, x0, norm_g[0], dx1, name="norm0_bwd")

    csz = cg
    g_lwi4 = jnp.stack([d_lwi_p[:, 0:csz], d_lwi_p[:, 2 * csz:3 * csz], d_lwi_p[:, csz:2 * csz],
                        d_lwi_p[:, 3 * csz:]])
    lands = list(_scatter_pieces([g_lwi4], name="scatter_grads")) + list(lands_early)
    halves = [_sum_slots(l, name="sum_" + nm) for l, nm in zip(lands, ("lru_w_in", "lru_w_out", "fox_w_in", "fox_w_out"))]
    big_g = _join_cores(halves, name="join_cores")

    small_g = [jnp.concatenate([dg0, dg1], axis=0), dgf.reshape(d), d_cw, d_cb, d_wa, d_ba, d_wx, d_bx, d_ap,
               db_f[:, :FOX_HEADS]]
    gsum = _allreduce_small(_pack(small_g), name="allreduce_small")
    zc = jnp.zeros((CONV_WIDTH, w), F32)
    pk_w = _pack([norm_g, final_g, zc, lru_conv_b, lru_wa, lru_ba, lru_wx, lru_bx, lru_a_param, fox_b_f])
    pk_m = _pack([m_norm_g, m_final_g, zc, m_lru_conv_b, m_lru_wa, m_lru_ba, m_lru_wx, m_lru_bx, m_lru_a_param,
                  m_fox_b_f])
    pk_v = _pack([v_norm_g, v_final_g, zc + 1.0, v_lru_conv_b, v_lru_wa, v_lru_ba, v_lru_wx, v_lru_bx,
                  v_lru_a_param, v_fox_b_f])
    s_delta, s_m, s_v = _adamw(pk_w[None], gsum[None], pk_m[None], pk_v[None], name="adamw_small")
    out_shapes = [norm_g.shape, final_g.shape, (CONV_WIDTH, w), lru_conv_b.shape, lru_wa.shape, lru_ba.shape,
                  lru_wx.shape, lru_bx.shape, lru_a_param.shape, fox_b_f.shape]
    sg = _unpack(gsum, out_shapes)
    sd = _unpack(s_delta, out_shapes)
    sm = _unpack(s_m, out_shapes)
    sv = _unpack(s_v, out_shapes)

    ncw = lru_conv_w.shape[2]
    g_cw_loc = lax.dynamic_slice_in_dim(sg[2], chip * ncw, ncw, axis=1)
    g_cw_loc = g_cw_loc[None]
    cw_d, cw_m, cw_v = _adamw(lru_conv_w, g_cw_loc, m_lru_conv_w, v_lru_conv_w, name="adamw_conv_w")

    big = []
    for nm, wt, g, mm, vv in (("lru_w_in", lru_w_in, big_g[0], m_lru_w_in, v_lru_w_in),
                              ("lru_w_out", lru_w_out, big_g[1], m_lru_w_out, v_lru_w_out),
                              ("fox_w_in", fox_w_in, big_g[2], m_fox_w_in, v_fox_w_in),
                              ("fox_w_out", fox_w_out, big_g[3], m_fox_w_out, v_fox_w_out)):
        big.append((g,) + tuple(_adamw(wt, g, mm, vv, name="adamw_" + nm)))

    def assemble(idx):
        small = (sg, sd, sm, sv)[idx]
        cw = (g_cw_loc, cw_d, cw_m, cw_v)[idx]
        return [small[0], small[1], big[0][idx], cw, small[3], small[4], small[5], small[6], small[7], small[8],
                big[1][idx], big[2][idx], small[9], big[3][idx]]

    grad_x = dx0.reshape(1, t, d)
    return (loss, grad_x, *assemble(0), *assemble(1), *assemble(2), *assemble(3))
```

```python
import functools
import math

import jax
import jax.numpy as jnp
from jax import lax
from jax.experimental import pallas as pl
from jax.experimental.pallas import tpu as pltpu

F32 = jnp.float32
BF16 = jnp.bfloat16

EPS = 1e-6
LRU_C = 8.0
LRU_BLOCK_W = 128
CONV_WIDTH = 4
FOX_HEADS = 16
FOX_HEAD_DIM = 64
NEG_INF = -1e30
ADAM_LR = 0.001
ADAM_B1 = 0.9
ADAM_B2 = 0.999
ADAM_EPS = 1e-08
ADAM_WD = 0.01
ADAM_STEP = 10

LANES = 128
SUBLANES = 8
VMEM_LIMIT = 56 * 1024 * 1024
TINY = 1e-30
N_CHIPS = 4
N_DEV = 8
MESH = pl.DeviceIdType.MESH
HIGHEST = lax.Precision.HIGHEST


def _tile(n, pref):
    t = min(n, pref)
    while n % t:
        t //= 2
    return t


def _cparams(dims=None):
    return pltpu.CompilerParams(dimension_semantics=dims, vmem_limit_bytes=VMEM_LIMIT)


def _sigmoid(x):
    return 0.5 * jnp.tanh(0.5 * x) + 0.5


def _log1p(x):
    u = 1.0 + x
    return jnp.where(u == 1.0, x, jnp.log(u) * x / (u - 1.0))


def _softplus(x):
    return jnp.maximum(x, 0.0) + _log1p(jnp.exp(-jnp.abs(x)))


MM_TILE = 1024
MM_FULL_K = 1536


def _matmul(a, b, *, name, ta=False, tb=False, out_dtype=F32, add=None, tm=MM_TILE, tn=MM_TILE, tk=None):
    if ta:
        kdim, m = a.shape
    else:
        m, kdim = a.shape
    if tb:
        n, kb = b.shape
    else:
        kb, n = b.shape
    assert kdim == kb, (a.shape, b.shape, ta, tb)
    if tk is None:
        tk = kdim if kdim <= MM_FULL_K else MM_TILE
    tm, tn, tk = _tile(m, tm), _tile(n, tn), _tile(kdim, tk)
    nk = kdim // tk
    dn = (((0 if ta else 1,), (1 if tb else 0,)), ((), ()))
    has_add = add is not None

    def body(*refs):
        if has_add:
            a_ref, b_ref, add_ref, o_ref = refs[:4]
        else:
            a_ref, b_ref, o_ref = refs[:3]
        part = lax.dot_general(a_ref[...].astype(BF16), b_ref[...].astype(BF16), dn, preferred_element_type=F32)

        def finish(r):
            if has_add:
                r = r + add_ref[...].astype(F32)
            o_ref[...] = r.astype(o_ref.dtype)

        if nk == 1:
            finish(part)
            return
        acc_ref = refs[-1]
        k = pl.program_id(2)

        @pl.when(k == 0)
        def _():
            acc_ref[...] = part

        @pl.when(k > 0)
        def _():
            acc_ref[...] += part

        @pl.when(k == nk - 1)
        def _():
            finish(acc_ref[...])

    a_spec = pl.BlockSpec((tk, tm), lambda i, j, k: (k, i)) if ta else pl.BlockSpec((tm, tk), lambda i, j, k: (i, k))
    b_spec = pl.BlockSpec((tn, tk), lambda i, j, k: (j, k)) if tb else pl.BlockSpec((tk, tn), lambda i, j, k: (k, j))
    o_spec = pl.BlockSpec((tm, tn), lambda i, j, k: (i, j))
    in_specs = [a_spec, b_spec] + ([o_spec] if has_add else [])
    args = (a, b) + ((add,) if has_add else ())
    return pl.pallas_call(
        body, name=name, grid=(m // tm, n // tn, nk), in_specs=in_specs, out_specs=o_spec,
        out_shape=jax.ShapeDtypeStruct((m, n), out_dtype),
        scratch_shapes=[pltpu.VMEM((tm, tn), F32)] if nk > 1 else [],
        compiler_params=_cparams(("parallel", "parallel", "arbitrary")),
    )(*args)


def _matmul_kparts(parts, b, *, chunk, name, tm=MM_TILE, tn=MM_TILE):
    npart = len(parts)
    m = parts[0].shape[0]
    n, kdim = b.shape
    nk = kdim // chunk
    assert nk * chunk == kdim and sum(p.shape[1] for p in parts) == kdim and nk % npart == 0
    tm, tn = _tile(m, tm), _tile(n, tn)
    dn = (((1,), (1,)), ((), ()))

    def body(*refs):
        a_refs, b_ref, o_ref, acc_ref = refs[:npart], refs[npart], refs[npart + 1], refs[npart + 2]
        k = pl.program_id(2)

        @pl.when(k == 0)
        def _():
            acc_ref[...] = jnp.zeros_like(acc_ref)

        for s in range(npart):
            @pl.when(lax.rem(k, npart) == s)
            def _(s=s):
                acc_ref[...] += lax.dot_general(a_refs[s][...].astype(BF16), b_ref[...].astype(BF16), dn,
                                                preferred_element_type=F32)

        @pl.when(k == nk - 1)
        def _():
            o_ref[...] = acc_ref[...].astype(o_ref.dtype)

    a_specs = [pl.BlockSpec((tm, chunk), lambda i, j, k: (i, k // npart)) for _ in range(npart)]
    return pl.pallas_call(
        body, name=name, grid=(m // tm, n // tn, nk),
        in_specs=a_specs + [pl.BlockSpec((tn, chunk), lambda i, j, k: (j, k))],
        out_specs=pl.BlockSpec((tm, tn), lambda i, j, k: (i, j)),
        out_shape=jax.ShapeDtypeStruct((m, n), F32),
        scratch_shapes=[pltpu.VMEM((tm, tn), F32)],
        compiler_params=_cparams(("parallel", "parallel", "arbitrary")),
    )(*parts, b)


def _matmul_nparts(a, parts, *, chunk, out_dtype, name, tm=MM_TILE, tk=MM_TILE):
    npart = len(parts)
    t, m = a.shape
    n = sum(p.shape[1] for p in parts)
    nj = n // chunk
    assert nj * chunk == n and nj % npart == 0
    tm, tk = _tile(m, tm), _tile(t, tk)
    nk = t // tk
    dn = (((0,), (0,)), ((), ()))

    def body(*refs):
        a_ref, b_refs, o_ref, acc_ref = refs[0], refs[1:1 + npart], refs[1 + npart], refs[2 + npart]
        j, k = pl.program_id(1), pl.program_id(2)

        @pl.when(k == 0)
        def _():
            acc_ref[...] = jnp.zeros_like(acc_ref)

        for s in range(npart):
            @pl.when(lax.rem(j, npart) == s)
            def _(s=s):
                acc_ref[...] += lax.dot_general(a_ref[...].astype(BF16), b_refs[s][...].astype(BF16), dn,
                                                preferred_element_type=F32)

        @pl.when(k == nk - 1)
        def _():
            o_ref[...] = acc_ref[...].astype(o_ref.dtype)

    def b_spec(s):
        return pl.BlockSpec((tk, chunk), lambda i, j, k: (jnp.where(lax.rem(j, npart) == s, k, 0), j // npart))

    return pl.pallas_call(
        body, name=name, grid=(m // tm, nj, nk),
        in_specs=[pl.BlockSpec((tk, tm), lambda i, j, k: (k, i))] + [b_spec(s) for s in range(npart)],
        out_specs=pl.BlockSpec((tm, chunk), lambda i, j, k: (i, j)),
        out_shape=jax.ShapeDtypeStruct((m, n), out_dtype),
        scratch_shapes=[pltpu.VMEM((tm, chunk), F32)],
        compiler_params=_cparams(("parallel", "parallel", "arbitrary")),
    )(a, *parts)


def _rmsnorm(x, g, *, name):
    t, d = x.shape
    tt = _tile(t, 512)

    def body(x_ref, g_ref, o_ref):
        xf = x_ref[...]
        rstd = lax.rsqrt(jnp.mean(xf * xf, axis=-1, keepdims=True) + EPS)
        o_ref[...] = (xf * rstd * g_ref[...]).astype(o_ref.dtype)

    return pl.pallas_call(
        body, name=name, grid=(t // tt,),
        in_specs=[pl.BlockSpec((tt, d), lambda i: (i, 0)), pl.BlockSpec((1, d), lambda i: (0, 0))],
        out_specs=pl.BlockSpec((tt, d), lambda i: (i, 0)),
        out_shape=jax.ShapeDtypeStruct((t, d), BF16),
        compiler_params=_cparams(("parallel",)),
    )(x, g.reshape(1, d))


def _rmsnorm_bwd(dh, x, g, dres, *, name, ride=()):
    nride = len(ride)
    t, d = x.shape
    tt = _tile(t, 512)
    nt = t // tt

    def body(*refs):
        ins, rest = refs[:4], refs[4:]
        ride_in, rest = rest[:nride], rest[nride:]
        outs, rest = rest[:2], rest[2:]
        ride_out, sems = rest[:nride], rest[nride:]
        if not nride:
            core(*ins, *outs)
            return
        i = pl.program_id(0)

        @pl.when(i == 0)
        def _():
            for cp in _scatter_copies(ride_in, ride_out, *sems):
                cp.start()

        core(*ins, *outs)

        @pl.when(i == nt - 1)
        def _():
            for cp in _scatter_copies(ride_in, ride_out, *sems):
                cp.wait()

    def core(dh_ref, x_ref, g_ref, dres_ref, dx_ref, dg_ref):
        i = pl.program_id(0)

        @pl.when(i == 0)
        def _():
            dg_ref[...] = jnp.zeros_like(dg_ref)

        xf = x_ref[...]
        rstd = lax.rsqrt(jnp.mean(xf * xf, axis=-1, keepdims=True) + EPS)
        xhat = xf * rstd
        dhf = dh_ref[...].astype(F32)
        dxhat = dhf * g_ref[...]
        mt = jnp.mean(dxhat * xhat, axis=-1, keepdims=True)
        dx_ref[...] = dres_ref[...] + rstd * (dxhat - xhat * mt)
        dg_ref[...] += jnp.sum(dhf * xhat, axis=0, keepdims=True)

    blk = pl.BlockSpec((tt, d), lambda i: (i, 0))
    vec = pl.BlockSpec((1, d), lambda i: (0, 0))
    any_spec = pl.BlockSpec(memory_space=pl.ANY)
    ride_shape, ride_sems = _scatter_shapes(ride) if nride else ([], [])
    outs = pl.pallas_call(
        body, name=name, grid=(nt,),
        in_specs=[blk, blk, vec, blk] + [any_spec] * nride, out_specs=[blk, vec] + [any_spec] * nride,
        out_shape=[jax.ShapeDtypeStruct((t, d), F32), jax.ShapeDtypeStruct((1, d), F32)] + ride_shape,
        scratch_shapes=ride_sems,
        compiler_params=_cparams(("arbitrary",)),
    )(dh, x, g.reshape(1, d), dres, *ride)
    return outs[0], outs[1], outs[2:]


def _final_loss(x2, tgt, g, *, name):
    t, d = x2.shape
    tt = _tile(t, 512)

    def body(x_ref, t_ref, g_ref, l_ref, dx_ref, dg_ref):
        i = pl.program_id(0)

        @pl.when(i == 0)
        def _():
            dg_ref[...] = jnp.zeros_like(dg_ref)
            l_ref[...] = jnp.zeros_like(l_ref)

        xf = x_ref[...]
        gg = g_ref[...]
        rstd = lax.rsqrt(jnp.mean(xf * xf, axis=-1, keepdims=True) + EPS)
        xhat = xf * rstd
        err = xhat * gg - t_ref[...]
        l_ref[...] += jnp.sum(err * err, axis=0, keepdims=True)
        dy = err * (1.0 / d)
        dxhat = dy * gg
        mt = jnp.mean(dxhat * xhat, axis=-1, keepdims=True)
        dx_ref[...] = rstd * (dxhat - xhat * mt)
        dg_ref[...] += jnp.sum(dy * xhat, axis=0, keepdims=True)

    blk = pl.BlockSpec((tt, d), lambda i: (i, 0))
    vec = pl.BlockSpec((1, d), lambda i: (0, 0))
    return pl.pallas_call(
        body, name=name, grid=(t // tt,),
        in_specs=[blk, blk, vec], out_specs=[vec, blk, vec],
        out_shape=[jax.ShapeDtypeStruct((1, d), F32), jax.ShapeDtypeStruct((t, d), F32),
                   jax.ShapeDtypeStruct((1, d), F32)],
        compiler_params=_cparams(("arbitrary",)),
    )(x2, tgt, g.reshape(1, d))


def _shift_down(prev8, cur, s):
    ext = jnp.concatenate([prev8, cur], axis=0)
    if s == 0:
        return cur
    return pltpu.roll(ext, s, 0)[SUBLANES:, :]


def _shift_up(cur, next8, s):
    if s == 0:
        return cur
    n = cur.shape[0]
    ext = jnp.concatenate([cur, next8], axis=0)
    return pltpu.roll(ext, n + SUBLANES - s, 0)[:n, :]


def _lru_gates(xc, wa, ba, wx, bx, sp):
    xcb = xc.astype(BF16)
    r = _sigmoid(jnp.dot(xcb, wa, preferred_element_type=F32) + ba)
    ig = _sigmoid(jnp.dot(xcb, wx, preferred_element_type=F32) + bx)
    log_a = -LRU_C * r * sp
    a = jnp.exp(log_a)
    z = -jnp.tanh(log_a) * (a * a + 1.0)
    inv_mult = lax.rsqrt(jnp.maximum(z, TINY))
    return r, ig, a, z * inv_mult, inv_mult


def _lru_specs(tt, cg, n_groups, nt, reverse):
    ncol = cg // LANES
    if reverse:
        ti = lambda i: nt - 1 - i
    else:
        ti = lambda i: i
    hb = tt // SUBLANES
    cur = lambda col: pl.BlockSpec((tt, cg), lambda g, i: (ti(i), 2 * g + col))
    prev = lambda col: pl.BlockSpec((SUBLANES, cg), lambda g, i: (jnp.maximum(ti(i) * hb - 1, 0), 2 * g + col))
    chan = lambda rows: pl.BlockSpec((rows, cg), lambda g, i: (0, g))
    wblk = pl.BlockSpec((ncol, LRU_BLOCK_W, LRU_BLOCK_W), lambda g, i: (g, 0, 0))
    plain = pl.BlockSpec((tt, cg), lambda g, i: (ti(i), g))
    plain_prev = pl.BlockSpec((SUBLANES, cg), lambda g, i: (jnp.maximum(ti(i) * hb - 1, 0), g))
    return cur, prev, chan, wblk, plain, plain_prev


def _chip_gather_copies(ins, outs, send, recv, loc):
    n = len(ins)
    x, y, c = lax.axis_index("x"), lax.axis_index("y"), lax.axis_index("c")
    me = 2 * x + y
    copies = [pltpu.make_async_copy(ins[k], outs[k].at[me], loc.at[k]) for k in range(n)]
    for r, (rx, ry) in enumerate(((1, 0), (0, 1), (1, 1))):
        for k in range(n):
            copies.append(pltpu.make_async_remote_copy(
                src_ref=ins[k], dst_ref=outs[k].at[me], send_sem=send.at[r * n + k], recv_sem=recv.at[r * n + k],
                device_id=(_flip(x, rx), _flip(y, ry), c), device_id_type=MESH))
    return copies


def _lru_fwd(u, conv_w, conv_b, wa, ba, wx, bx, a_param, *, cg, name, ride=()):
    nride = len(ride)
    t, w2 = u.shape
    w = w2 // 2
    n_groups = w // cg
    ncol = cg // LANES
    tt = _tile(t, 256)
    nt = t // tt
    cur, prev, chan, wblk, plain, _ = _lru_specs(tt, cg, n_groups, nt, False)

    def body(*refs):
        n_in, n_out, n_scr = 10, 2, 3
        ins, rest = refs[:n_in], refs[n_in:]
        ride_in, rest = rest[:nride], rest[nride:]
        outs, rest = rest[:n_out], rest[n_out:]
        ride_out, rest = rest[:nride], rest[nride:]
        scr, sems = rest[:n_scr], rest[n_scr:]
        if not nride:
            core(*ins, *outs, *scr)
            return
        step = pl.program_id(0) * nt + pl.program_id(1)

        @pl.when(step == 0)
        def _():
            for cp in _chip_gather_copies(ride_in, ride_out, *sems):
                cp.start()

        core(*ins, *outs, *scr)

        @pl.when(step == n_groups * nt - 1)
        def _():
            for cp in _chip_gather_copies(ride_in, ride_out, *sems):
                cp.wait()

    def core(xb_ref, xp_ref, gate_ref, cw_ref, cb_ref, wa_ref, ba_ref, wx_ref, bx_ref, ap_ref,
             y_ref, hs_ref, h_ref, a_s, b_s):
        i = pl.program_id(1)

        @pl.when(i == 0)
        def _():
            h_ref[...] = jnp.zeros_like(h_ref)

        keep = (i > 0).astype(F32)
        for n in range(ncol):
            sl = slice(n * LANES, (n + 1) * LANES)
            xb = xb_ref[:, sl]
            xp = xp_ref[:, sl] * keep
            xc = cb_ref[:, sl] + cw_ref[3:4, sl] * xb
            for s in range(1, CONV_WIDTH):
                xc = xc + cw_ref[3 - s:4 - s, sl] * _shift_down(xp, xb, s)
            sp = _softplus(-ap_ref[:, sl])
            _, ig, a, mult, _ = _lru_gates(xc, wa_ref[n].astype(BF16), ba_ref[:, sl],
                                           wx_ref[n].astype(BF16), bx_ref[:, sl], sp)
            a_s[:, sl] = a
            b_s[:, sl] = mult * (ig * xc)

        def step(g, h):
            base = pl.multiple_of(g * SUBLANES, SUBLANES)
            for r in range(SUBLANES):
                h = a_s[pl.ds(base + r, 1), :] * h + b_s[pl.ds(base + r, 1), :]
                hs_ref[pl.ds(base + r, 1), :] = h
            return h

        h = lax.fori_loop(0, tt // SUBLANES, step, h_ref[0:1, :])
        h_ref[0:1, :] = h
        gate = gate_ref[...]
        y_ref[...] = (hs_ref[...] * (gate * _sigmoid(gate))).astype(y_ref.dtype)

    any_spec = pl.BlockSpec(memory_space=pl.ANY)
    ride_sems = [pltpu.SemaphoreType.DMA((3 * nride,)), pltpu.SemaphoreType.DMA((3 * nride,)),
                 pltpu.SemaphoreType.DMA((nride,))] if nride else []
    outs = pl.pallas_call(
        body, name=name, grid=(n_groups, nt),
        in_specs=[cur(0), prev(0), cur(1), chan(CONV_WIDTH), chan(1), wblk, chan(1), wblk, chan(1), chan(1)]
        + [any_spec] * nride,
        out_specs=[plain, plain] + [any_spec] * nride,
        out_shape=[jax.ShapeDtypeStruct((t, w), BF16), jax.ShapeDtypeStruct((t, w), F32)]
        + [jax.ShapeDtypeStruct((N_CHIPS,) + r.shape, r.dtype) for r in ride],
        scratch_shapes=[pltpu.VMEM((SUBLANES, cg), F32), pltpu.VMEM((tt, cg), F32), pltpu.VMEM((tt, cg), F32)]
        + ride_sems,
        compiler_params=_cparams(("arbitrary", "arbitrary")),
    )(u, u, u, conv_w, conv_b, wa, ba, wx, bx, a_param, *ride)
    return outs[0], outs[1], outs[2:]


def _lru_bwd(u, hs, dy, conv_w, conv_b, wa, ba, wx, bx, a_param, *, cg, name, ride=()):
    nride = len(ride)
    t, w2 = u.shape
    w = w2 // 2
    n_groups = w // cg
    ncol = cg // LANES
    tt = _tile(t, 256)
    nt = t // tt
    cur, prev, chan, wblk, plain, plain_prev = _lru_specs(tt, cg, n_groups, nt, True)
    tn_dims = (((0,), (0,)), ((), ()))
    nt_dims = (((1,), (1,)), ((), ()))

    def body(*refs):
        n_in, n_out, n_scr = 13, 9, 5
        ins, rest = refs[:n_in], refs[n_in:]
        ride_in, rest = rest[:nride], rest[nride:]
        outs, rest = rest[:n_out], rest[n_out:]
        ride_out, rest = rest[:nride], rest[nride:]
        scr, sems = rest[:n_scr], rest[n_scr:]
        if not nride:
            core(*ins, *outs, *scr)
            return
        step = pl.program_id(0) * nt + pl.program_id(1)

        @pl.when(step == 0)
        def _():
            for cp in _scatter_copies(ride_in, ride_out, *sems):
                cp.start()

        core(*ins, *outs, *scr)

        @pl.when(step == n_groups * nt - 1)
        def _():
            for cp in _scatter_copies(ride_in, ride_out, *sems):
                cp.wait()

    def core(xb_ref, xp_ref, gate_ref, hs_ref, hp_ref, dy_ref, cw_ref, cb_ref, wa_ref, ba_ref, wx_ref, bx_ref,
             ap_ref, dxb_ref, dgate_ref, dcw_ref, dcb_ref, dwa_ref, dba_ref, dwx_ref, dbx_ref, dsp_ref,
             c_ref, nx_ref, a_s, dhs_s, lam_s):
        i = pl.program_id(1)
        first_time_block = i == nt - 1

        @pl.when(i == 0)
        def _():
            c_ref[...] = jnp.zeros_like(c_ref)
            nx_ref[...] = jnp.zeros_like(nx_ref)
            for r in (dcw_ref, dcb_ref, dwa_ref, dba_ref, dwx_ref, dbx_ref, dsp_ref):
                r[...] = jnp.zeros_like(r)

        keep = jnp.where(first_time_block, 0.0, 1.0).astype(F32)
        gate = gate_ref[...]
        sg = _sigmoid(gate)
        dyv = dy_ref[...]
        hsv = hs_ref[...]
        dhs_s[...] = dyv * (gate * sg)
        dgate_ref[...] = (dyv * hsv * (sg * (1.0 + gate * (1.0 - sg)))).astype(dgate_ref.dtype)

        saved = []
        for n in range(ncol):
            sl = slice(n * LANES, (n + 1) * LANES)
            xb = xb_ref[:, sl]
            xp = xp_ref[:, sl] * keep
            shifted = [xb] + [_shift_down(xp, xb, s) for s in range(1, CONV_WIDTH)]
            xc = cb_ref[:, sl] + cw_ref[3:4, sl] * xb
            for s in range(1, CONV_WIDTH):
                xc = xc + cw_ref[3 - s:4 - s, sl] * shifted[s]
            sp = _softplus(-ap_ref[:, sl])
            wab = wa_ref[n].astype(BF16)
            wxb = wx_ref[n].astype(BF16)
            r, ig, a, mult, inv_mult = _lru_gates(xc, wab, ba_ref[:, sl], wxb, bx_ref[:, sl], sp)
            a_s[:, sl] = a
            saved.append((sl, shifted, xc, sp, wab, wxb, r, ig, a, mult, inv_mult))

        def step(g, c):
            base = pl.multiple_of(tt - SUBLANES - g * SUBLANES, SUBLANES)
            for r in range(SUBLANES - 1, -1, -1):
                lam = dhs_s[pl.ds(base + r, 1), :] + c
                lam_s[pl.ds(base + r, 1), :] = lam
                c = a_s[pl.ds(base + r, 1), :] * lam
            return c

        c_ref[0:1, :] = lax.fori_loop(0, tt // SUBLANES, step, c_ref[0:1, :])

        for n in range(ncol):
            sl, shifted, xc, sp, wab, wxb, r, ig, a, mult, inv_mult = saved[n]
            lam = lam_s[:, sl]
            hprev = _shift_down(hp_ref[:, sl] * keep, hs_ref[:, sl], 1)
            da = lam * hprev
            dmult = lam * (ig * xc)
            dlog_a = da * a - dmult * (a * a * inv_mult)
            di = lam * (mult * xc)
            dxc = lam * (mult * ig)
            dr = dlog_a * (-LRU_C * sp)
            dsp_ref[:, sl] += jnp.sum(dlog_a * (-LRU_C * r), axis=0, keepdims=True)
            dza = dr * (r * (1.0 - r))
            dzx = di * (ig * (1.0 - ig))
            dba_ref[:, sl] += jnp.sum(dza, axis=0, keepdims=True)
            dbx_ref[:, sl] += jnp.sum(dzx, axis=0, keepdims=True)
            xcb = xc.astype(BF16)
            dzab = dza.astype(BF16)
            dzxb = dzx.astype(BF16)
            dwa_ref[n] += lax.dot_general(xcb, dzab, tn_dims, preferred_element_type=F32)
            dwx_ref[n] += lax.dot_general(xcb, dzxb, tn_dims, preferred_element_type=F32)
            dxc = dxc + lax.dot_general(dzab, wab, nt_dims, preferred_element_type=F32)
            dxc = dxc + lax.dot_general(dzxb, wxb, nt_dims, preferred_element_type=F32)
            dcb_ref[:, sl] += jnp.sum(dxc, axis=0, keepdims=True)
            for s in range(CONV_WIDTH):
                dcw_ref[3 - s:4 - s, sl] += jnp.sum(dxc * shifted[s], axis=0, keepdims=True)
            nx = nx_ref[:, sl]
            dxb = cw_ref[3:4, sl] * dxc
            for s in range(1, CONV_WIDTH):
                dxb = dxb + cw_ref[3 - s:4 - s, sl] * _shift_up(dxc, nx, s)
            dxb_ref[:, sl] = dxb.astype(dxb_ref.dtype)
            nx_ref[:, sl] = dxc[0:SUBLANES, :]

        @pl.when(first_time_block)
        def _():
            dsp_ref[...] = dsp_ref[...] * (-_sigmoid(-ap_ref[...]))

    dxb_spec = pl.BlockSpec((tt, cg), lambda g, i: (nt - 1 - i, g))
    any_spec = pl.BlockSpec(memory_space=pl.ANY)
    ride_shape, ride_sems = _scatter_shapes(ride) if nride else ([], [])
    outs = pl.pallas_call(
        body, name=name, grid=(n_groups, nt),
        in_specs=[cur(0), prev(0), cur(1), plain, plain_prev, plain, chan(CONV_WIDTH), chan(1), wblk, chan(1), wblk,
                  chan(1), chan(1)] + [any_spec] * nride,
        out_specs=[dxb_spec, dxb_spec, chan(CONV_WIDTH), chan(1), wblk, chan(1), wblk, chan(1), chan(1)]
        + [any_spec] * nride,
        out_shape=[jax.ShapeDtypeStruct((t, w), BF16), jax.ShapeDtypeStruct((t, w), BF16),
                   jax.ShapeDtypeStruct(conv_w.shape, F32), jax.ShapeDtypeStruct(conv_b.shape, F32),
                   jax.ShapeDtypeStruct(wa.shape, F32), jax.ShapeDtypeStruct(ba.shape, F32),
                   jax.ShapeDtypeStruct(wx.shape, F32), jax.ShapeDtypeStruct(bx.shape, F32),
                   jax.ShapeDtypeStruct(a_param.shape, F32)] + ride_shape,
        scratch_shapes=[pltpu.VMEM((SUBLANES, cg), F32), pltpu.VMEM((SUBLANES, cg), F32),
                        pltpu.VMEM((tt, cg), F32), pltpu.VMEM((tt, cg), F32), pltpu.VMEM((tt, cg), F32)] + ride_sems,
        compiler_params=_cparams(("arbitrary", "arbitrary")),
    )(u, u, u, hs, hs, dy, conv_w, conv_b, wa, ba, wx, bx, a_param, *ride)
    return outs[:9], outs[9:]


def _fgate_fwd(f, b_f, *, name):
    t, n = f.shape
    tt = _tile(t, 256)
    width = FOX_HEADS * FOX_HEAD_DIM

    def body(f_ref, b_ref, cum_ref, wide_ref, carry_ref):
        i = pl.program_id(0)

        @pl.when(i == 0)
        def _():
            carry_ref[...] = jnp.zeros_like(carry_ref)

        z = f_ref[...] + b_ref[...]
        lf = jnp.minimum(z, 0.0) - _log1p(jnp.exp(-jnp.abs(z)))
        row = lax.broadcasted_iota(jnp.int32, (tt, tt), 0)
        col = lax.broadcasted_iota(jnp.int32, (tt, tt), 1)
        tri = (col <= row).astype(F32)
        cum = jnp.dot(tri, lf, precision=HIGHEST, preferred_element_type=F32) + carry_ref[0:1, :]
        cum_ref[...] = cum
        carry_ref[0:1, :] = cum[tt - 1:tt, :]
        head = lax.broadcasted_iota(jnp.int32, (n, width), 0)
        chan = lax.broadcasted_iota(jnp.int32, (n, width), 1) // FOX_HEAD_DIM
        spread = (head == chan).astype(BF16)
        hi = cum.astype(BF16)
        rest = cum - hi.astype(F32)
        mid = rest.astype(BF16)
        low = (rest - mid.astype(F32)).astype(BF16)
        wide = jnp.dot(hi, spread, preferred_element_type=F32) + jnp.dot(mid, spread, preferred_element_type=F32)
        wide_ref[...] = wide + jnp.dot(low, spread, preferred_element_type=F32)

    return pl.pallas_call(
        body, name=name, grid=(t // tt,),
        in_specs=[pl.BlockSpec((tt, n), lambda i: (i, 0)), pl.BlockSpec((1, n), lambda i: (0, 0))],
        out_specs=[pl.BlockSpec((tt, n), lambda i: (i, 0)), pl.BlockSpec((tt, width), lambda i: (i, 0))],
        out_shape=[jax.ShapeDtypeStruct((t, n), F32), jax.ShapeDtypeStruct((t, width), F32)],
        scratch_shapes=[pltpu.VMEM((SUBLANES, n), F32)],
        compiler_params=_cparams(("arbitrary",)),
    )(f, b_f)


def _fgate_bwd(dcum, f, b_f, *, name):
    t, n = f.shape
    tt = _tile(t, 256)
    nt = t // tt

    def body(dc_ref, f_ref, b_ref, df_ref, db_ref, carry_ref):
        i = pl.program_id(0)

        @pl.when(i == 0)
        def _():
            carry_ref[...] = jnp.zeros_like(carry_ref)
            db_ref[...] = jnp.zeros_like(db_ref)

        row = lax.broadcasted_iota(jnp.int32, (tt, tt), 0)
        col = lax.broadcasted_iota(jnp.int32, (tt, tt), 1)
        triu = (col >= row).astype(F32)
        dlf = jnp.dot(triu, dc_ref[...], precision=HIGHEST, preferred_element_type=F32) + carry_ref[0:1, :]
        carry_ref[0:1, :] = dlf[0:1, :]
        z = f_ref[...] + b_ref[...]
        df = dlf * _sigmoid(-z)
        df_ref[...] = df
        db_ref[...] += jnp.sum(df, axis=0, keepdims=True)

    blk = pl.BlockSpec((tt, n), lambda i: (nt - 1 - i, 0))
    vec = pl.BlockSpec((1, n), lambda i: (0, 0))
    return pl.pallas_call(
        body, name=name, grid=(nt,),
        in_specs=[blk, blk, vec], out_specs=[blk, vec],
        out_shape=[jax.ShapeDtypeStruct((t, n), F32), jax.ShapeDtypeStruct((1, n), F32)],
        scratch_shapes=[pltpu.VMEM((SUBLANES, n), F32)],
        compiler_params=_cparams(("arbitrary",)),
    )(dcum, f, b_f)


def _attn_fwd(start, qkv, ckt, gate, *, name, tq, tk):
    t = qkv.shape[0]
    f = gate.shape[1]
    npair = f // LANES
    nq = t // tq
    ratio = tq // tk
    assert tq == ratio * tk and t == nq * tq
    scale = 1.0 / math.sqrt(FOX_HEAD_DIM)
    nt_dims = (((1,), (1,)), ((), ()))

    def body(start_ref, q_ref, k_ref, v_ref, ck_ref, g_ref, o_ref, y_ref, l_ref):
        i = pl.program_id(1)
        pair = pl.program_id(0)
        firsts = (start_ref[2 * pair, i], start_ref[2 * pair + 1, i])
        both = jnp.maximum(firsts[0], firsts[1])
        lane = lax.broadcasted_iota(jnp.int32, (tq, LANES), 1)
        lo = lane < FOX_HEAD_DIM
        q2 = q_ref[...] * scale
        qs = (jnp.where(lo, q2, 0).astype(BF16), jnp.where(lo, 0, q2).astype(BF16))
        row = lax.broadcasted_iota(jnp.int32, (tq, tk), 0)
        col = lax.broadcasted_iota(jnp.int32, (tq, tk), 1)

        def kv_step(j, carry, diag, heads=(0, 1)):
            off = pl.multiple_of(j * tk, tk)
            kj = k_ref[pl.ds(off, tk), :]
            vj = v_ref[pl.ds(off, tk), :]
            ck = ck_ref[:, pl.ds(off, tk)]
            new = list(carry)
            for h in heads:
                m, l, acc = carry[h]
                s = lax.dot_general(qs[h], kj, nt_dims, preferred_element_type=F32) - ck[h:h + 1, :]
                if diag is not None:
                    s = jnp.where(col + diag * tk <= row, s, NEG_INF)
                m_new = jnp.maximum(m, jnp.max(s, axis=-1, keepdims=True))
                alpha = jnp.exp(m - m_new)
                p = jnp.exp(s - m_new)
                l = alpha * l + jnp.sum(p, axis=-1, keepdims=True)
                acc = alpha * acc + jnp.dot(p.astype(BF16), vj, preferred_element_type=F32)
                new[h] = (m_new, l, acc)
            return tuple(new)

        carry = tuple((jnp.full((tq, 1), NEG_INF, F32), jnp.zeros((tq, 1), F32), jnp.zeros((tq, LANES), F32))
                      for _ in range(2))
        for h in range(2):
            carry = lax.fori_loop(firsts[h], both, lambda j, c, h=h: kv_step(j, c, None, (h,)), carry)
        carry = lax.fori_loop(both, i * ratio, lambda j, c: kv_step(j, c, None), carry)
        for d in range(ratio):
            carry = kv_step(i * ratio + d, carry, d)
        (m0, l0, a0), (m1, l1, a1) = carry
        o = jnp.where(lo, a0 / l0, a1 / l1)
        o_ref[...] = o
        gate_v = g_ref[...]
        y_ref[...] = (o * (gate_v * _sigmoid(gate_v))).astype(y_ref.dtype)
        lse_t = jnp.transpose(jnp.where(lo, m0 + jnp.log(l0), m1 + jnp.log(l1)))
        l_ref[0:1, :] = lse_t[0:1, :]
        l_ref[1:2, :] = lse_t[FOX_HEAD_DIM:FOX_HEAD_DIM + 1, :]

    blk = lambda base: pl.BlockSpec((tq, LANES), lambda p, i, s: (i, base + p))
    full = lambda base: pl.BlockSpec((t, LANES), lambda p, i, s: (0, base + p))
    return pl.pallas_call(
        body, name=name,
        grid_spec=pltpu.PrefetchScalarGridSpec(
            num_scalar_prefetch=1, grid=(npair, nq),
            in_specs=[blk(0), full(npair), full(2 * npair), pl.BlockSpec((None, 2, t), lambda p, i, s: (p, 0, 0)),
                      blk(0)],
            out_specs=[blk(0), blk(0), pl.BlockSpec((None, 2, tq), lambda p, i, s: (p, 0, i))]),
        out_shape=[jax.ShapeDtypeStruct((t, f), F32), jax.ShapeDtypeStruct((t, f), BF16),
                   jax.ShapeDtypeStruct((npair, 2, t), F32)],
        compiler_params=_cparams(("parallel", "arbitrary")),
    )(start, qkv, qkv, qkv, ckt, gate)


def _attn_bwd(end, qkv, do, lt, dt, cke, *, name):
    t, f = do.shape
    npair = f // LANES
    tk = _tile(t, ATTN_TILE)
    nk = t // tk
    scale = 1.0 / math.sqrt(FOX_HEAD_DIM)
    nt_dims = (((1,), (1,)), ((), ()))
    tn_dims = (((0,), (0,)), ((), ()))

    def body(end_ref, k_ref, v_ref, q_ref, do_ref, l_ref, d_ref, ck_ref, dq_out_ref, dk_ref, dv_ref, dck_ref, dcq_ref,
             dq_ref, dk_s, dv_s, dck_s):
        j = pl.program_id(1)
        pair = pl.program_id(0)
        lasts = (end_ref[2 * pair, j], end_ref[2 * pair + 1, j])
        both = jnp.minimum(lasts[0], lasts[1])

        @pl.when(j == 0)
        def _():
            dq_ref[...] = jnp.zeros_like(dq_ref)
            dcq_ref[...] = jnp.zeros_like(dcq_ref)

        lane = lax.broadcasted_iota(jnp.int32, (tk, LANES), 1)
        lo = lane < FOX_HEAD_DIM
        sel = (lo, jnp.logical_not(lo))
        kj = k_ref[...]
        vj = v_ref[...]
        km = tuple(jnp.where(sel[h], kj, 0).astype(BF16) for h in range(2))
        ckv = ck_ref[...]
        ckh = (ckv[:, 0:1], ckv[:, FOX_HEAD_DIM:FOX_HEAD_DIM + 1])
        row = lax.broadcasted_iota(jnp.int32, (tk, tk), 0)
        col = lax.broadcasted_iota(jnp.int32, (tk, tk), 1)
        causal = row <= col

        def q_step(i, carry, masked, heads=(0, 1)):
            off = pl.multiple_of(i * tk, tk)
            qi = q_ref[pl.ds(off, tk), :]
            doi = do_ref[pl.ds(off, tk), :]
            lrow = l_ref[:, pl.ds(off, tk)]
            drow = d_ref[:, pl.ds(off, tk)]
            dq_add = jnp.zeros((tk, LANES), F32)
            for h in heads:
                qm = jnp.where(sel[h], qi, 0).astype(BF16)
                dom = jnp.where(sel[h], doi, 0).astype(BF16)
                st = lax.dot_general(kj, qm, nt_dims, preferred_element_type=F32) * scale
                st = st - ckh[h] - lrow[h:h + 1, :]
                if masked:
                    st = jnp.where(causal, st, NEG_INF)
                pt = jnp.exp(st)
                dpt = lax.dot_general(vj, dom, nt_dims, preferred_element_type=F32)
                dst = pt * (dpt - drow[h:h + 1, :])
                ptb = pt.astype(BF16)
                dstb = dst.astype(BF16)
                dv_s[...] += jnp.dot(ptb, dom, preferred_element_type=F32)
                dk_s[...] += jnp.dot(dstb, qm, preferred_element_type=F32)
                dq_add = dq_add + lax.dot_general(dstb, km[h], tn_dims, preferred_element_type=F32)
                dck_s[:, h:h + 1] -= jnp.sum(dst, axis=-1, keepdims=True)
                dcq_ref[h:h + 1, pl.ds(off, tk)] += jnp.sum(dst, axis=0, keepdims=True)
            dq_ref[pl.ds(off, tk), :] += dq_add * scale
            return carry

        dk_s[...] = jnp.zeros_like(dk_s)
        dv_s[...] = jnp.zeros_like(dv_s)
        dck_s[...] = jnp.zeros_like(dck_s)
        carry = 0
        carry = q_step(j, carry, True)
        carry = lax.fori_loop(j + 1, both, lambda i, c: q_step(i, c, False), carry)
        for h in range(2):
            carry = lax.fori_loop(both, lasts[h], lambda i, c, h=h: q_step(i, c, False, (h,)), carry)
        dk_acc, dv_acc = dk_s[...], dv_s[...]
        dck = (dck_s[:, 0:1], dck_s[:, 1:2])
        dk_ref[...] = (dk_acc * scale).astype(dk_ref.dtype)
        dv_ref[...] = dv_acc.astype(dv_ref.dtype)
        dck_t = jnp.transpose(jnp.where(lo, dck[0], dck[1]))
        dck_ref[0:1, :] = dck_t[0:1, :]
        dck_ref[1:2, :] = dck_t[FOX_HEAD_DIM:FOX_HEAD_DIM + 1, :]

        @pl.when(j == nk - 1)
        def _():
            dq_out_ref[...] = dq_ref[...].astype(dq_out_ref.dtype)

    blk = lambda base: pl.BlockSpec((tk, LANES), lambda p, j, e: (j, base + p))
    full = lambda base: pl.BlockSpec((t, LANES), lambda p, j, e: (0, base + p))
    rows = pl.BlockSpec((None, 2, t), lambda p, j, e: (p, 0, 0))
    return pl.pallas_call(
        body, name=name,
        grid_spec=pltpu.PrefetchScalarGridSpec(
            num_scalar_prefetch=1, grid=(npair, nk),
            in_specs=[blk(npair), blk(2 * npair), full(0), full(0), rows, rows, blk(0)],
            out_specs=[full(0), blk(0), blk(0), pl.BlockSpec((None, 2, tk), lambda p, j, e: (p, 0, j)), rows],
            scratch_shapes=[pltpu.VMEM((t, LANES), F32), pltpu.VMEM((tk, LANES), F32), pltpu.VMEM((tk, LANES), F32),
                            pltpu.VMEM((tk, LANES), F32)]),
        out_shape=[jax.ShapeDtypeStruct((t, f), BF16), jax.ShapeDtypeStruct((t, f), BF16),
                   jax.ShapeDtypeStruct((t, f), BF16), jax.ShapeDtypeStruct((npair, 2, t), F32),
                   jax.ShapeDtypeStruct((npair, 2, t), F32)],
        compiler_params=_cparams(("parallel", "arbitrary")),
    )(end, qkv, qkv, qkv, do, lt, dt, cke)


ATTN_TILE = 512
ATTN_FWD_QUERIES = 512
EXP_ZERO = -104.0
BOUND_SLACK = 1.02


def _attn_row_stats(qkv, *, name):
    t = qkv.shape[0]
    f = qkv.shape[1] // 3
    tt = _tile(t, 512)

    def body(q_ref, k_ref, s_ref):
        q = q_ref[...].astype(F32)
        k = k_ref[...].astype(F32)
        chan = lax.broadcasted_iota(jnp.int32, (f, LANES), 0) // FOX_HEAD_DIM
        lane = lax.broadcasted_iota(jnp.int32, (f, LANES), 1)
        acc = jnp.zeros((tt, LANES), F32)
        for off, val in ((0, q * q), (FOX_HEADS, q * k), (2 * FOX_HEADS, k * k)):
            pick = (chan == lane - off).astype(BF16)
            acc = acc + jnp.dot(val.astype(BF16), pick, preferred_element_type=F32)
        s_ref[...] = acc

    return pl.pallas_call(
        body, name=name, grid=(t // tt,),
        in_specs=[pl.BlockSpec((tt, f), lambda i: (i, 0)), pl.BlockSpec((tt, f), lambda i: (i, 1))],
        out_specs=pl.BlockSpec((tt, LANES), lambda i: (i, 0)),
        out_shape=jax.ShapeDtypeStruct((t, LANES), F32),
        compiler_params=_cparams(("parallel",)),
    )(qkv, qkv)


def _attn_skip_tables(stats, cum16, tile):
    t = stats.shape[0]
    nb = t // tile
    scale = 1.0 / math.sqrt(FOX_HEAD_DIM)
    qn = jnp.sqrt(stats[:, :FOX_HEADS]) * scale
    sii = stats[:, FOX_HEADS:2 * FOX_HEADS] * scale - cum16
    kmax = jnp.max(jnp.sqrt(stats[:, 2 * FOX_HEADS:3 * FOX_HEADS]), axis=0, keepdims=True)
    arow = qn * kmax * BOUND_SLACK - sii + 0.5 * BOUND_SLACK
    a_blk = jnp.max(arow.reshape(nb, tile, FOX_HEADS), axis=1)
    c_blk = -cum16.reshape(nb, tile, FOX_HEADS)[:, tile - 1, :]
    dead = (a_blk[:, None, :] + c_blk[None, :, :]) < EXP_ZERO
    start_h = jnp.sum(dead.astype(jnp.int32), axis=1)
    blk = jnp.arange(nb, dtype=jnp.int32)
    start = jnp.minimum(start_h, blk[:, None]).T
    needs = start[:, :, None] <= blk[None, None, :]
    end = jnp.max(jnp.where(needs, blk[None, :, None] + 1, 0), axis=1)
    return start, jnp.maximum(end, blk[None, :] + 1)


def _fox_post_bwd(dy, o, gate, *, name):
    t, f = dy.shape
    tt = _tile(t, 512)

    def body(dy_ref, o_ref, g_ref, do_ref, dg_ref, dl_ref):
        g = g_ref[...]
        sg = _sigmoid(g)
        dyv = dy_ref[...]
        ov = o_ref[...]
        do = dyv * (g * sg)
        do_ref[...] = do.astype(do_ref.dtype)
        dg_ref[...] = (dyv * ov * (sg * (1.0 + g * (1.0 - sg)))).astype(dg_ref.dtype)
        chan = lax.broadcasted_iota(jnp.int32, (f, LANES), 0)
        head = lax.broadcasted_iota(jnp.int32, (f, LANES), 1)
        pick = (chan // FOX_HEAD_DIM == head).astype(F32)
        dl_ref[...] = jnp.dot(do * ov, pick, precision=HIGHEST, preferred_element_type=F32)

    blk = pl.BlockSpec((tt, f), lambda i: (i, 0))
    return pl.pallas_call(
        body, name=name, grid=(t // tt,),
        in_specs=[blk, blk, blk], out_specs=[blk, blk, pl.BlockSpec((tt, LANES), lambda i: (i, 0))],
        out_shape=[jax.ShapeDtypeStruct((t, f), BF16), jax.ShapeDtypeStruct((t, f), BF16),
                   jax.ShapeDtypeStruct((t, LANES), F32)],
        compiler_params=_cparams(("parallel",)),
    )(dy, o, gate)


def _adamw(w, g, m, v, *, name):
    _, r, c = w.shape
    tr = _tile(r, 256) if r % SUBLANES == 0 else r
    c1 = 1.0 - ADAM_B1 ** ADAM_STEP
    c2 = 1.0 - ADAM_B2 ** ADAM_STEP

    def body(w_ref, g_ref, m_ref, v_ref, d_ref, mo_ref, vo_ref):
        gv = g_ref[...]
        mn = ADAM_B1 * m_ref[...] + (1.0 - ADAM_B1) * gv
        vn = ADAM_B2 * v_ref[...] + (1.0 - ADAM_B2) * (gv * gv)
        mo_ref[...] = mn
        vo_ref[...] = vn
        d_ref[...] = -ADAM_LR * ((mn / c1) / (jnp.sqrt(vn / c2) + ADAM_EPS) + ADAM_WD * w_ref[...])

    blk = pl.BlockSpec((None, tr, c), lambda i: (0, i, 0))
    return pl.pallas_call(
        body, name=name, grid=(r // tr,), in_specs=[blk] * 4, out_specs=[blk] * 3,
        out_shape=[jax.ShapeDtypeStruct((1, r, c), F32)] * 3,
        compiler_params=_cparams(("parallel",)),
    )(w, g, m, v)


def _sum_slots(land, *, name):
    ns, r, c = land.shape
    tr = _tile(r, 64) if r % SUBLANES == 0 else r

    def body(l_ref, o_ref):
        acc = l_ref[0].astype(F32)
        for s in range(1, ns):
            acc = acc + l_ref[s].astype(F32)
        o_ref[...] = acc

    return pl.pallas_call(
        body, name=name, grid=(r // tr,),
        in_specs=[pl.BlockSpec((ns, tr, c), lambda i: (0, i, 0))],
        out_specs=pl.BlockSpec((tr, c), lambda i: (i, 0)),
        out_shape=jax.ShapeDtypeStruct((r, c), F32),
        compiler_params=_cparams(("parallel",)),
    )(land)


ANY = pl.BlockSpec(memory_space=pl.ANY)


def _flip(v, bit):
    return 1 - v if bit else v


def _gather_chips(shards, small, *, name):
    n = len(shards)
    rels = ((1, 0), (0, 1), (1, 1))

    def body(*refs):
        ins, small_in = refs[:n], refs[n]
        outs, small_out = refs[n + 1:2 * n + 1], refs[2 * n + 1]
        send, recv, loc = refs[2 * n + 2:]
        x, y, c = lax.axis_index("x"), lax.axis_index("y"), lax.axis_index("c")
        me = 2 * x + y
        sibling = (x, y, 1 - c)
        local = [pltpu.make_async_copy(ins[k], outs[k].at[me], loc.at[k]) for k in range(n)]
        local.append(pltpu.make_async_copy(small_in, small_out.at[me], loc.at[n]))
        for cp in local:
            cp.start()

        def rows(k):
            half = ins[k].shape[0] // 2
            return pl.ds(pl.multiple_of(c * half, SUBLANES), half)

        sends = []
        for r, (rx, ry) in enumerate(rels):
            to = (_flip(x, rx), _flip(y, ry), c)
            for k in range(n):
                cp = pltpu.make_async_remote_copy(
                    src_ref=ins[k].at[rows(k), :], dst_ref=outs[k].at[me, rows(k), :],
                    send_sem=send.at[r * n + k], recv_sem=recv.at[r * n + k], device_id=to, device_id_type=MESH)
                cp.start()
                sends.append(cp)
            cp = pltpu.make_async_remote_copy(
                src_ref=small_in, dst_ref=small_out.at[me], send_sem=send.at[6 * n + r], recv_sem=recv.at[6 * n + r],
                device_id=to, device_id_type=MESH)
            cp.start()
            sends.append(cp)
        for r, (rx, ry) in enumerate(rels):
            src_chip = 2 * _flip(x, rx) + _flip(y, ry)
            for k in range(n):
                landed = outs[k].at[src_chip, rows(k), :]
                sends[r * (n + 1) + k].wait_recv()
                cp = pltpu.make_async_remote_copy(
                    src_ref=landed, dst_ref=landed, send_sem=send.at[3 * n + r * n + k],
                    recv_sem=recv.at[3 * n + r * n + k], device_id=sibling, device_id_type=MESH)
                cp.start()
                sends.append(cp)
            sends[r * (n + 1) + n].wait_recv()
        for cp in sends[:3 * (n + 1)]:
            cp.wait_send()
        for cp in sends[3 * (n + 1):]:
            cp.wait()
        for cp in local:
            cp.wait()

    vmem = pl.BlockSpec(memory_space=pltpu.VMEM)
    return pl.pallas_call(
        body, name=name, in_specs=[vmem] * (n + 1), out_specs=[vmem] * (n + 1),
        out_shape=[jax.ShapeDtypeStruct((N_CHIPS,) + s.shape, s.dtype) for s in list(shards) + [small]],
        scratch_shapes=[pltpu.SemaphoreType.DMA((6 * n + 3,)), pltpu.SemaphoreType.DMA((6 * n + 3,)),
                        pltpu.SemaphoreType.DMA((n + 1,))],
        compiler_params=pltpu.CompilerParams(has_side_effects=True, vmem_limit_bytes=VMEM_LIMIT),
    )(*shards, small)


_RELS7 = tuple((r >> 2 & 1, r >> 1 & 1, r & 1) for r in range(1, N_DEV))


def _scatter_copies(ins, outs, send, recv, loc):
    n = len(ins)
    x, y, c = lax.axis_index("x"), lax.axis_index("y"), lax.axis_index("c")
    me = 4 * x + 2 * y + c

    def piece(k, px, py, pc):
        half = ins[k].shape[1] // 2
        return ins[k].at[2 * px + py, pl.ds(pc * half, half), :]

    copies = [pltpu.make_async_copy(piece(k, x, y, c), outs[k].at[me], loc.at[k]) for k in range(n)]
    for r, (rx, ry, rc) in enumerate(_RELS7):
        tx, ty, tc = _flip(x, rx), _flip(y, ry), _flip(c, rc)
        for k in range(n):
            copies.append(pltpu.make_async_remote_copy(
                src_ref=piece(k, tx, ty, tc), dst_ref=outs[k].at[me], send_sem=send.at[r * n + k],
                recv_sem=recv.at[r * n + k], device_id=(tx, ty, tc), device_id_type=MESH))
    return copies


def _scatter_shapes(grads):
    n = len(grads)
    out_shape = [jax.ShapeDtypeStruct((N_DEV, g.shape[1] // 2, g.shape[2]), g.dtype) for g in grads]
    sems = [pltpu.SemaphoreType.DMA((7 * n,)), pltpu.SemaphoreType.DMA((7 * n,)), pltpu.SemaphoreType.DMA((n,))]
    return out_shape, sems


def _scatter_pieces(grads, *, name):
    n = len(grads)

    def body(*refs):
        copies = _scatter_copies(refs[:n], refs[n:2 * n], *refs[2 * n:])
        for cp in copies:
            cp.start()
        for cp in copies:
            cp.wait()

    out_shape, sems = _scatter_shapes(grads)
    return pl.pallas_call(
        body, name=name, in_specs=[ANY] * n, out_specs=[ANY] * n, out_shape=out_shape, scratch_shapes=sems,
        compiler_params=pltpu.CompilerParams(has_side_effects=True),
    )(*grads)


def _join_cores(halves, *, name):
    n = len(halves)

    def body(*refs):
        ins, outs = refs[:n], refs[n:2 * n]
        send, recv, loc = refs[2 * n:]
        x, y, c = lax.axis_index("x"), lax.axis_index("y"), lax.axis_index("c")
        copies = []
        for k in range(n):
            half = ins[k].shape[0]
            mine = outs[k].at[0, pl.ds(c * half, half), :]
            cp = pltpu.make_async_copy(ins[k], mine, loc.at[k])
            cp.start()
            copies.append(cp)
            cp = pltpu.make_async_remote_copy(
                src_ref=ins[k], dst_ref=mine, send_sem=send.at[k], recv_sem=recv.at[k],
                device_id=(x, y, 1 - c), device_id_type=MESH)
            cp.start()
            copies.append(cp)
        for cp in copies:
            cp.wait()

    in_vmem = pl.BlockSpec(memory_space=pltpu.VMEM)
    return pl.pallas_call(
        body, name=name, in_specs=[in_vmem] * n, out_specs=[in_vmem] * n,
        out_shape=[jax.ShapeDtypeStruct((1, 2 * h.shape[0], h.shape[1]), h.dtype) for h in halves],
        scratch_shapes=[pltpu.SemaphoreType.DMA((n,)), pltpu.SemaphoreType.DMA((n,)), pltpu.SemaphoreType.DMA((n,))],
        compiler_params=pltpu.CompilerParams(has_side_effects=True, vmem_limit_bytes=VMEM_LIMIT),
    )(*halves)


def _allreduce_small(buf, *, name):
    r, n = buf.shape
    half = r // 2
    rels = ((1, 0), (0, 1), (1, 1))

    def body(in_ref, out_ref, sib_ref, chips_ref, send, recv):
        x, y, c = lax.axis_index("x"), lax.axis_index("y"), lax.axis_index("c")
        sibling = (x, y, 1 - c)
        chip = 2 * x + y
        rows = pl.ds(pl.multiple_of(c * half, SUBLANES), half)

        swap = pltpu.make_async_remote_copy(src_ref=in_ref, dst_ref=sib_ref, send_sem=send.at[0], recv_sem=recv.at[0],
                                            device_id=sibling, device_id_type=MESH)
        swap.start()
        swap.wait()
        chips_ref[chip] = in_ref[rows, :] + sib_ref[rows, :]

        sends = []
        for k, (rx, ry) in enumerate(rels):
            cp = pltpu.make_async_remote_copy(
                src_ref=chips_ref.at[chip], dst_ref=chips_ref.at[chip], send_sem=send.at[1 + k],
                recv_sem=recv.at[1 + k], device_id=(_flip(x, rx), _flip(y, ry), c), device_id_type=MESH)
            cp.start()
            sends.append(cp)
        for cp in sends:
            cp.wait()
        total = chips_ref[0]
        for s in range(1, N_CHIPS):
            total = total + chips_ref[s]
        out_ref[rows, :] = total

        back = pltpu.make_async_remote_copy(src_ref=out_ref.at[rows, :], dst_ref=out_ref.at[rows, :],
                                            send_sem=send.at[4], recv_sem=recv.at[4],
                                            device_id=sibling, device_id_type=MESH)
        back.start()
        back.wait()

    vmem = pl.BlockSpec(memory_space=pltpu.VMEM)
    return pl.pallas_call(
        body, name=name, in_specs=[vmem], out_specs=vmem,
        out_shape=jax.ShapeDtypeStruct((r, n), F32),
        scratch_shapes=[pltpu.VMEM((r, n), F32), pltpu.VMEM((N_CHIPS, half, n), F32),
                        pltpu.SemaphoreType.DMA((5,)), pltpu.SemaphoreType.DMA((5,))],
        compiler_params=pltpu.CompilerParams(has_side_effects=True, vmem_limit_bytes=VMEM_LIMIT),
    )(buf)


def _pack(arrs):
    flat = []
    for a in arrs:
        v = a.reshape(-1)
        pad = (-v.shape[0]) % LANES
        if pad:
            v = jnp.pad(v, (0, pad))
        flat.append(v)
    v = jnp.concatenate(flat)
    pad = (-v.shape[0]) % (LANES * SUBLANES)
    if pad:
        v = jnp.pad(v, (0, pad))
    return v.reshape(-1, LANES)


def _unpack(buf, shapes):
    v = buf.reshape(-1)
    out, off = [], 0
    for s in shapes:
        n = math.prod(s)
        out.append(v[off:off + n].reshape(s))
        off += n + (-n) % LANES
    return out


def kernel(x, norm_g, final_g, lru_w_in, lru_conv_w, lru_conv_b, lru_wa, lru_ba, lru_wx, lru_bx, lru_a_param, lru_w_out, fox_w_in, fox_b_f, fox_w_out, loss_target, m_norm_g, m_final_g, m_lru_w_in, m_lru_conv_w, m_lru_conv_b, m_lru_wa, m_lru_ba, m_lru_wx, m_lru_bx, m_lru_a_param, m_lru_w_out, m_fox_w_in, m_fox_b_f, m_fox_w_out, v_norm_g, v_final_g, v_lru_w_in, v_lru_conv_w, v_lru_conv_b, v_lru_wa, v_lru_ba, v_lru_wx, v_lru_bx, v_lru_a_param, v_lru_w_out, v_fox_w_in, v_fox_b_f, v_fox_w_out):
    t, d = x.shape[1], x.shape[2]
    w = lru_wa.shape[1] * LRU_BLOCK_W
    f = FOX_HEADS * FOX_HEAD_DIM
    npair = f // LANES
    x0 = x.reshape(t, d)
    tgt = loss_target.reshape(t, d)
    chip = 2 * lax.axis_index("x") + lax.axis_index("y")

    g_lwi, g_lwo, g_cw = _gather_chips(
        [lru_w_in[0].astype(BF16), lru_w_out[0].astype(BF16)], lru_conv_w[0], name="gather_weights")
    cg = w // 2
    lwi = jnp.concatenate([g_lwi[0], g_lwi[2], g_lwi[1], g_lwi[3]], axis=1)
    lwo = g_lwo.reshape(w, d)
    conv_w = jnp.concatenate([g_cw[s] for s in range(N_CHIPS)], axis=1)
    conv_b, ba, bx, a_param = lru_conv_b, lru_ba, lru_bx, lru_a_param
    wa, wx = lru_wa[0], lru_wx[0]
    b_f = jnp.pad(fox_b_f, ((0, 0), (0, LANES - FOX_HEADS)))

    h0 = _rmsnorm(x0, norm_g[0], name="norm0")
    u = _matmul(h0, lwi, name="lru_in")
    y1, hs, (g_fwi, g_fwo) = _lru_fwd(u, conv_w, conv_b, wa, ba, wx, bx, a_param, cg=cg, name="lru_fwd",
                                      ride=[fox_w_in[0].astype(BF16), fox_w_out[0].astype(BF16)])
    fwi = jnp.concatenate([g_fwi[s] for s in range(N_CHIPS)], axis=1)
    w_qkv, w_g2 = fwi[:, :3 * f], fwi[:, 3 * f:4 * f]
    w_f = jnp.pad(fwi[:, 4 * f:], ((0, 0), (0, LANES - FOX_HEADS)))
    fwo = g_fwo.reshape(f, d)
    x1 = _matmul(y1, lwo, add=x0, name="lru_out")
    h1 = _rmsnorm(x1, norm_g[1], name="norm1")
    qkv = _matmul(h1, w_qkv, out_dtype=BF16, name="fox_qkv")
    gate2 = _matmul(h1, w_g2, name="fox_gate")
    flog = _matmul(h1, w_f, name="fox_f")
    cum, cke = _fgate_fwd(flog, b_f, name="fgate_fwd")
    cum16 = cum[:, :FOX_HEADS]
    ckt = cum16.T.reshape(npair, 2, t)
    a_tk, a_tq = _tile(t, ATTN_TILE), _tile(t, ATTN_FWD_QUERIES)
    a_start, a_end = _attn_skip_tables(_attn_row_stats(qkv, name="attn_row_stats"), cum16, a_tk)
    a_start_fwd = jnp.min(a_start.reshape(FOX_HEADS, t // a_tq, a_tq // a_tk), axis=2)
    o, y2, lse = _attn_fwd(a_start_fwd, qkv, ckt, gate2, name="attn_fwd", tq=a_tq, tk=a_tk)
    x2 = _matmul(y2, fwo, add=x1, name="fox_out")
    lsum, dx2, dgf = _final_loss(x2, tgt, final_g, name="final_loss")
    loss = lax.psum(0.5 * jnp.sum(lsum) / d, ("x", "y", "c"))

    d_fwo = _matmul(y2, dx2, ta=True, out_dtype=BF16, name="d_fox_w_out")
    dy2 = _matmul(dx2, fwo, tb=True, name="d_y2")
    do, dgate2, dl = _fox_post_bwd(dy2, o, gate2, name="fox_post_bwd")
    lt = lse
    dt = dl[:, :FOX_HEADS].T.reshape(npair, 2, t)
    dq, dk, dv, dck, dcq = _attn_bwd(a_end, qkv, do, lt, dt, cke, name="attn_bwd")
    dcum = jnp.pad((dck + dcq).reshape(FOX_HEADS, t).T, ((0, 0), (0, LANES - FOX_HEADS)))
    dflog, db_f = _fgate_bwd(dcum, flog, b_f, name="fgate_bwd")
    du2 = [dq, dk, dv, dgate2]
    dflog_b = dflog.astype(BF16)
    dh1 = _matmul_kparts(du2, fwi[:, :4 * f], chunk=f, name="d_h1_a")
    dh1 = _matmul(dflog_b, w_f, tb=True, add=dh1, name="d_h1_b")
    d_fwi_a = _matmul_nparts(h1, du2, chunk=f, out_dtype=BF16, name="d_fox_w_in_a")
    d_fwi_b = _matmul(h1, dflog_b, ta=True, out_dtype=BF16, name="d_fox_w_in_b")
    d_fwi = jnp.concatenate([d_fwi_a, d_fwi_b[:, :FOX_HEADS]], axis=1)
    dx1, dg1, _ = _rmsnorm_bwd(dh1, x1, norm_g[1], dx2, name="norm1_bwd")

    d_lwo = _matmul(y1, dx1, ta=True, out_dtype=BF16, name="d_lru_w_out")
    dy1 = _matmul(dx1, lwo, tb=True, name="d_y1")
    n_fwi = fox_w_in.shape[2]
    g_fwi4 = jnp.stack([d_fwi[:, s * n_fwi:(s + 1) * n_fwi] for s in range(N_CHIPS)])
    g_fwo4 = d_fwo.reshape(N_CHIPS, f // N_CHIPS, d)
    g_lwo4 = d_lwo.reshape(N_CHIPS, w // N_CHIPS, d)
    (dxb, dgate, d_cw, d_cb, d_wa, d_ba, d_wx, d_bx, d_ap), lands_early = _lru_bwd(
        u, hs, dy1, conv_w, conv_b, wa, ba, wx, bx, a_param, cg=cg, name="lru_bwd", ride=[g_lwo4, g_fwi4, g_fwo4])
    dh0 = _matmul_kparts([dxb, dgate], lwi, chunk=cg, name="d_h0")
    d_lwi_p = _matmul_nparts(h0, [dxb, dgate], chunk=cg, out_dtype=BF16, name="d_lru_w_in")
    csz = cg
    g_lwi4 = jnp.stack([d_lwi_p[:, 0:csz], d_lwi_p[:, 2 * csz:3 * csz], d_lwi_p[:, csz:2 * csz],
                        d_lwi_p[:, 3 * csz:]])
    dx0, dg0, lands_last = _rmsnorm_bwd(dh0, x0, norm_g[0], dx1, name="norm0_bwd", ride=[g_lwi4])
    lands = list(lands_last) + list(lands_early)
    halves = [_sum_slots(l, name="sum_" + nm) for l, nm in zip(lands, ("lru_w_in", "lru_w_out", "fox_w_in", "fox_w_out"))]
    big_g = _join_cores(halves, name="join_cores")

    small_g = [jnp.concatenate([dg0, dg1], axis=0), dgf.reshape(d), d_cw, d_cb, d_wa, d_ba, d_wx, d_bx, d_ap,
               db_f[:, :FOX_HEADS]]
    gsum = _allreduce_small(_pack(small_g), name="allreduce_small")
    zc = jnp.zeros((CONV_WIDTH, w), F32)
    pk_w = _pack([norm_g, final_g, zc, lru_conv_b, lru_wa, lru_ba, lru_wx, lru_bx, lru_a_param, fox_b_f])
    pk_m = _pack([m_norm_g, m_final_g, zc, m_lru_conv_b, m_lru_wa, m_lru_ba, m_lru_wx, m_lru_bx, m_lru_a_param,
                  m_fox_b_f])
    pk_v = _pack([v_norm_g, v_final_g, zc + 1.0, v_lru_conv_b, v_lru_wa, v_lru_ba, v_lru_wx, v_lru_bx,
                  v_lru_a_param, v_fox_b_f])
    s_delta, s_m, s_v = _adamw(pk_w[None], gsum[None], pk_m[None], pk_v[None], name="adamw_small")
    out_shapes = [norm_g.shape, final_g.shape, (CONV_WIDTH, w), lru_conv_b.shape, lru_wa.shape, lru_ba.shape,
                  lru_wx.shape, lru_bx.shape, lru_a_param.shape, fox_b_f.shape]
    sg = _unpack(gsum, out_shapes)
    sd = _unpack(s_delta, out_shapes)
    sm = _unpack(s_m, out_shapes)
    sv = _unpack(s_v, out_shapes)

    ncw = lru_conv_w.shape[2]
    g_cw_loc = lax.dynamic_slice_in_dim(sg[2], chip * ncw, ncw, axis=1)
    g_cw_loc = g_cw_loc[None]
    cw_d, cw_m, cw_v = _adamw(lru_conv_w, g_cw_loc, m_lru_conv_w, v_lru_conv_w, name="adamw_conv_w")

    big = []
    for nm, wt, g, mm, vv in (("lru_w_in", lru_w_in, big_g[0], m_lru_w_in, v_lru_w_in),
                              ("lru_w_out", lru_w_out, big_g[1], m_lru_w_out, v_lru_w_out),
                              ("fox_w_in", fox_w_in, big_g[2], m_fox_w_in, v_fox_w_in),
                              ("fox_w_out", fox_w_out, big_g[3], m_fox_w_out, v_fox_w_out)):
        big.append((g,) + tuple(_adamw(wt, g, mm, vv, name="adamw_" + nm)))

    def assemble(idx):
        small = (sg, sd, sm, sv)[idx]
        cw = (g_cw_loc, cw_d, cw_m, cw_v)[idx]
        return [small[0], small[1], big[0][idx], cw, small[3], small[4], small[5], small[6], small[7], small[8],
                big[1][idx], big[2][idx], small[9], big[3][idx]]

    grad_x = dx0.reshape(1, t, d)
    return (loss, grad_x, *assemble(0), *assemble(1), *assemble(2), *assemble(3))
```

```python
import math

import jax
import jax.numpy as jnp
from jax import lax
from jax.experimental import pallas as pl
from jax.experimental.pallas import tpu as pltpu

F32 = jnp.float32
BF16 = jnp.bfloat16

EPS = 1e-6
LRU_C = 8.0
LRU_BLOCK_W = 128
CONV_WIDTH = 4
FOX_HEADS = 16
FOX_HEAD_DIM = 64
NEG_INF = -1e30
ADAM_LR = 0.001
ADAM_B1 = 0.9
ADAM_B2 = 0.999
ADAM_EPS = 1e-08
ADAM_WD = 0.01
ADAM_STEP = 10

LANES = 128
SUBLANES = 8
VMEM_LIMIT = 56 * 1024 * 1024
TINY = 1e-30
N_CHIPS = 4
N_DEV = 8
MESH = pl.DeviceIdType.MESH


def _tile(n, pref):
    t = min(n, pref)
    while n % t:
        t //= 2
    return t


def _cparams(dims=None):
    return pltpu.CompilerParams(dimension_semantics=dims, vmem_limit_bytes=VMEM_LIMIT)


def _sigmoid(x):
    return 0.5 * jnp.tanh(0.5 * x) + 0.5


def _log1p(x):
    u = 1.0 + x
    return jnp.where(u == 1.0, x, jnp.log(u) * x / (u - 1.0))


def _bf16_pieces(x):
    hi = x.astype(BF16)
    rest = x - hi.astype(F32)
    mid = rest.astype(BF16)
    return hi, mid, (rest - mid.astype(F32)).astype(BF16)


def _dot_01_left(m01, x):
    return sum(jnp.dot(m01, p, preferred_element_type=F32) for p in _bf16_pieces(x))


def _dot_01_right(x, m01):
    return sum(jnp.dot(p, m01, preferred_element_type=F32) for p in _bf16_pieces(x))


def _softplus(x):
    return jnp.maximum(x, 0.0) + _log1p(jnp.exp(-jnp.abs(x)))


MM_TILE = 1024
MM_FULL_K = 1536


def _matmul(a, b, *, name, ta=False, tb=False, out_dtype=F32, add=None, tm=MM_TILE, tn=MM_TILE, tk=None):
    if ta:
        kdim, m = a.shape
    else:
        m, kdim = a.shape
    if tb:
        n, kb = b.shape
    else:
        kb, n = b.shape
    assert kdim == kb, (a.shape, b.shape, ta, tb)
    if tk is None:
        tk = kdim if kdim <= MM_FULL_K else MM_TILE
    tm, tn, tk = _tile(m, tm), _tile(n, tn), _tile(kdim, tk)
    nk = kdim // tk
    dn = (((0 if ta else 1,), (1 if tb else 0,)), ((), ()))
    has_add = add is not None

    def body(*refs):
        if has_add:
            a_ref, b_ref, add_ref, o_ref = refs[:4]
        else:
            a_ref, b_ref, o_ref = refs[:3]
        part = lax.dot_general(a_ref[...].astype(BF16), b_ref[...].astype(BF16), dn, preferred_element_type=F32)

        def finish(r):
            if has_add:
                r = r + add_ref[...].astype(F32)
            o_ref[...] = r.astype(o_ref.dtype)

        if nk == 1:
            finish(part)
            return
        acc_ref = refs[-1]
        k = pl.program_id(2)

        @pl.when(k == 0)
        def _():
            acc_ref[...] = part

        @pl.when(k > 0)
        def _():
            acc_ref[...] += part

        @pl.when(k == nk - 1)
        def _():
            finish(acc_ref[...])

    a_spec = pl.BlockSpec((tk, tm), lambda i, j, k: (k, i)) if ta else pl.BlockSpec((tm, tk), lambda i, j, k: (i, k))
    b_spec = pl.BlockSpec((tn, tk), lambda i, j, k: (j, k)) if tb else pl.BlockSpec((tk, tn), lambda i, j, k: (k, j))
    o_spec = pl.BlockSpec((tm, tn), lambda i, j, k: (i, j))
    in_specs = [a_spec, b_spec] + ([o_spec] if has_add else [])
    args = (a, b) + ((add,) if has_add else ())
    return pl.pallas_call(
        body, name=name, grid=(m // tm, n // tn, nk), in_specs=in_specs, out_specs=o_spec,
        out_shape=jax.ShapeDtypeStruct((m, n), out_dtype),
        scratch_shapes=[pltpu.VMEM((tm, tn), F32)] if nk > 1 else [],
        compiler_params=_cparams(("parallel", "parallel", "arbitrary")),
    )(*args)


def _matmul_kparts(parts, b, *, chunk, name, tm=MM_TILE, tn=MM_TILE):
    npart = len(parts)
    m = parts[0].shape[0]
    n, kdim = b.shape
    nk = kdim // chunk
    assert nk * chunk == kdim and sum(p.shape[1] for p in parts) == kdim and nk % npart == 0
    tm, tn = _tile(m, tm), _tile(n, tn)
    dn = (((1,), (1,)), ((), ()))

    def body(*refs):
        a_refs, b_ref, o_ref, acc_ref = refs[:npart], refs[npart], refs[npart + 1], refs[npart + 2]
        k = pl.program_id(2)

        @pl.when(k == 0)
        def _():
            acc_ref[...] = jnp.zeros_like(acc_ref)

        for s in range(npart):
            @pl.when(lax.rem(k, npart) == s)
            def _(s=s):
                acc_ref[...] += lax.dot_general(a_refs[s][...].astype(BF16), b_ref[...].astype(BF16), dn,
                                                preferred_element_type=F32)

        @pl.when(k == nk - 1)
        def _():
            o_ref[...] = acc_ref[...].astype(o_ref.dtype)

    a_specs = [pl.BlockSpec((tm, chunk), lambda i, j, k: (i, k // npart)) for _ in range(npart)]
    return pl.pallas_call(
        body, name=name, grid=(m // tm, n // tn, nk),
        in_specs=a_specs + [pl.BlockSpec((tn, chunk), lambda i, j, k: (j, k))],
        out_specs=pl.BlockSpec((tm, tn), lambda i, j, k: (i, j)),
        out_shape=jax.ShapeDtypeStruct((m, n), F32),
        scratch_shapes=[pltpu.VMEM((tm, tn), F32)],
        compiler_params=_cparams(("parallel", "parallel", "arbitrary")),
    )(*parts, b)


def _matmul_nparts(a, parts, *, chunk, out_dtype, name, tm=MM_TILE, tk=MM_TILE):
    npart = len(parts)
    t, m = a.shape
    n = sum(p.shape[1] for p in parts)
    nj = n // chunk
    assert nj * chunk == n and nj % npart == 0
    tm, tk = _tile(m, tm), _tile(t, tk)
    nk = t // tk
    dn = (((0,), (0,)), ((), ()))

    def body(*refs):
        a_ref, b_refs, o_ref, acc_ref = refs[0], refs[1:1 + npart], refs[1 + npart], refs[2 + npart]
        j, k = pl.program_id(1), pl.program_id(2)

        @pl.when(k == 0)
        def _():
            acc_ref[...] = jnp.zeros_like(acc_ref)

        for s in range(npart):
            @pl.when(lax.rem(j, npart) == s)
            def _(s=s):
                acc_ref[...] += lax.dot_general(a_ref[...].astype(BF16), b_refs[s][...].astype(BF16), dn,
                                                preferred_element_type=F32)

        @pl.when(k == nk - 1)
        def _():
            o_ref[...] = acc_ref[...].astype(o_ref.dtype)

    def b_spec(s):
        return pl.BlockSpec((tk, chunk), lambda i, j, k: (jnp.where(lax.rem(j, npart) == s, k, 0), j // npart))

    return pl.pallas_call(
        body, name=name, grid=(m // tm, nj, nk),
        in_specs=[pl.BlockSpec((tk, tm), lambda i, j, k: (k, i))] + [b_spec(s) for s in range(npart)],
        out_specs=pl.BlockSpec((tm, chunk), lambda i, j, k: (i, j)),
        out_shape=jax.ShapeDtypeStruct((m, n), out_dtype),
        scratch_shapes=[pltpu.VMEM((tm, chunk), F32)],
        compiler_params=_cparams(("parallel", "parallel", "arbitrary")),
    )(a, *parts)


def _rmsnorm(x, g, *, name):
    t, d = x.shape
    tt = _tile(t, 512)

    def body(x_ref, g_ref, o_ref):
        xf = x_ref[...]
        rstd = lax.rsqrt(jnp.mean(xf * xf, axis=-1, keepdims=True) + EPS)
        o_ref[...] = (xf * rstd * g_ref[...]).astype(o_ref.dtype)

    return pl.pallas_call(
        body, name=name, grid=(t // tt,),
        in_specs=[pl.BlockSpec((tt, d), lambda i: (i, 0)), pl.BlockSpec((1, d), lambda i: (0, 0))],
        out_specs=pl.BlockSpec((tt, d), lambda i: (i, 0)),
        out_shape=jax.ShapeDtypeStruct((t, d), BF16),
        compiler_params=_cparams(("parallel",)),
    )(x, g.reshape(1, d))


def _rmsnorm_bwd(dh, x, g, dres, *, name, ride=()):
    nride = len(ride)
    t, d = x.shape
    tt = _tile(t, 512)
    nt = t // tt

    def body(*refs):
        ins, rest = refs[:4], refs[4:]
        ride_in, rest = rest[:nride], rest[nride:]
        outs, rest = rest[:2], rest[2:]
        ride_out, sems = rest[:nride], rest[nride:]
        if not nride:
            core(*ins, *outs)
            return
        i = pl.program_id(0)

        @pl.when(i == 0)
        def _():
            for cp in _scatter_copies(ride_in, ride_out, *sems):
                cp.start()

        core(*ins, *outs)

        @pl.when(i == nt - 1)
        def _():
            for cp in _scatter_copies(ride_in, ride_out, *sems):
                cp.wait()

    def core(dh_ref, x_ref, g_ref, dres_ref, dx_ref, dg_ref):
        i = pl.program_id(0)

        @pl.when(i == 0)
        def _():
            dg_ref[...] = jnp.zeros_like(dg_ref)

        xf = x_ref[...]
        rstd = lax.rsqrt(jnp.mean(xf * xf, axis=-1, keepdims=True) + EPS)
        xhat = xf * rstd
        dhf = dh_ref[...].astype(F32)
        dxhat = dhf * g_ref[...]
        mt = jnp.mean(dxhat * xhat, axis=-1, keepdims=True)
        dx_ref[...] = dres_ref[...] + rstd * (dxhat - xhat * mt)
        dg_ref[...] += jnp.sum(dhf * xhat, axis=0, keepdims=True)

    blk = pl.BlockSpec((tt, d), lambda i: (i, 0))
    vec = pl.BlockSpec((1, d), lambda i: (0, 0))
    any_spec = pl.BlockSpec(memory_space=pl.ANY)
    ride_shape, ride_sems = _scatter_shapes(ride) if nride else ([], [])
    outs = pl.pallas_call(
        body, name=name, grid=(nt,),
        in_specs=[blk, blk, vec, blk] + [any_spec] * nride, out_specs=[blk, vec] + [any_spec] * nride,
        out_shape=[jax.ShapeDtypeStruct((t, d), F32), jax.ShapeDtypeStruct((1, d), F32)] + ride_shape,
        scratch_shapes=ride_sems,
        compiler_params=_cparams(("arbitrary",)),
    )(dh, x, g.reshape(1, d), dres, *ride)
    return outs[0], outs[1], outs[2:]


def _final_loss(x2, tgt, g, *, name):
    t, d = x2.shape
    tt = _tile(t, 512)

    def body(x_ref, t_ref, g_ref, l_ref, dx_ref, dg_ref):
        i = pl.program_id(0)

        @pl.when(i == 0)
        def _():
            dg_ref[...] = jnp.zeros_like(dg_ref)
            l_ref[...] = jnp.zeros_like(l_ref)

        xf = x_ref[...]
        gg = g_ref[...]
        rstd = lax.rsqrt(jnp.mean(xf * xf, axis=-1, keepdims=True) + EPS)
        xhat = xf * rstd
        err = xhat * gg - t_ref[...]
        l_ref[...] += jnp.sum(err * err, axis=0, keepdims=True)
        dy = err * (1.0 / d)
        dxhat = dy * gg
        mt = jnp.mean(dxhat * xhat, axis=-1, keepdims=True)
        dx_ref[...] = rstd * (dxhat - xhat * mt)
        dg_ref[...] += jnp.sum(dy * xhat, axis=0, keepdims=True)

    blk = pl.BlockSpec((tt, d), lambda i: (i, 0))
    vec = pl.BlockSpec((1, d), lambda i: (0, 0))
    return pl.pallas_call(
        body, name=name, grid=(t // tt,),
        in_specs=[blk, blk, vec], out_specs=[vec, blk, vec],
        out_shape=[jax.ShapeDtypeStruct((1, d), F32), jax.ShapeDtypeStruct((t, d), F32),
                   jax.ShapeDtypeStruct((1, d), F32)],
        compiler_params=_cparams(("arbitrary",)),
    )(x2, tgt, g.reshape(1, d))


def _shift_down(prev8, cur, s):
    ext = jnp.concatenate([prev8, cur], axis=0)
    if s == 0:
        return cur
    return pltpu.roll(ext, s, 0)[SUBLANES:, :]


def _shift_up(cur, next8, s):
    if s == 0:
        return cur
    n = cur.shape[0]
    ext = jnp.concatenate([cur, next8], axis=0)
    return pltpu.roll(ext, n + SUBLANES - s, 0)[:n, :]


def _lru_gates(xc, wa, ba, wx, bx, sp):
    xcb = xc.astype(BF16)
    r = _sigmoid(jnp.dot(xcb, wa, preferred_element_type=F32) + ba)
    ig = _sigmoid(jnp.dot(xcb, wx, preferred_element_type=F32) + bx)
    log_a = -LRU_C * r * sp
    a = jnp.exp(log_a)
    z = -jnp.tanh(log_a) * (a * a + 1.0)
    inv_mult = lax.rsqrt(jnp.maximum(z, TINY))
    return r, ig, a, z * inv_mult, inv_mult


def _lru_specs(tt, cg, n_groups, nt, reverse):
    ncol = cg // LANES
    if reverse:
        ti = lambda i: nt - 1 - i
    else:
        ti = lambda i: i
    hb = tt // SUBLANES
    cur = lambda col: pl.BlockSpec((tt, cg), lambda g, i: (ti(i), 2 * g + col))
    prev = lambda col: pl.BlockSpec((SUBLANES, cg), lambda g, i: (jnp.maximum(ti(i) * hb - 1, 0), 2 * g + col))
    chan = lambda rows: pl.BlockSpec((rows, cg), lambda g, i: (0, g))
    wblk = pl.BlockSpec((ncol, LRU_BLOCK_W, LRU_BLOCK_W), lambda g, i: (g, 0, 0))
    plain = pl.BlockSpec((tt, cg), lambda g, i: (ti(i), g))
    plain_prev = pl.BlockSpec((SUBLANES, cg), lambda g, i: (jnp.maximum(ti(i) * hb - 1, 0), g))
    return cur, prev, chan, wblk, plain, plain_prev


def _chip_gather_copies(ins, outs, send, recv, loc):
    n = len(ins)
    x, y, c = lax.axis_index("x"), lax.axis_index("y"), lax.axis_index("c")
    me = 2 * x + y
    copies = [pltpu.make_async_copy(ins[k], outs[k].at[me], loc.at[k]) for k in range(n)]
    for r, (rx, ry) in enumerate(((1, 0), (0, 1), (1, 1))):
        for k in range(n):
            copies.append(pltpu.make_async_remote_copy(
                src_ref=ins[k], dst_ref=outs[k].at[me], send_sem=send.at[r * n + k], recv_sem=recv.at[r * n + k],
                device_id=(_flip(x, rx), _flip(y, ry), c), device_id_type=MESH))
    return copies


def _lru_fwd(u, conv_w, conv_b, wa, ba, wx, bx, a_param, *, cg, name, ride=()):
    nride = len(ride)
    t, w2 = u.shape
    w = w2 // 2
    n_groups = w // cg
    ncol = cg // LANES
    tt = _tile(t, 256)
    nt = t // tt
    cur, prev, chan, wblk, plain, _ = _lru_specs(tt, cg, n_groups, nt, False)

    def body(*refs):
        n_in, n_out, n_scr = 10, 2, 3
        ins, rest = refs[:n_in], refs[n_in:]
        ride_in, rest = rest[:nride], rest[nride:]
        outs, rest = rest[:n_out], rest[n_out:]
        ride_out, rest = rest[:nride], rest[nride:]
        scr, sems = rest[:n_scr], rest[n_scr:]
        if not nride:
            core(*ins, *outs, *scr)
            return
        step = pl.program_id(0) * nt + pl.program_id(1)

        @pl.when(step == 0)
        def _():
            for cp in _chip_gather_copies(ride_in, ride_out, *sems):
                cp.start()

        core(*ins, *outs, *scr)

        @pl.when(step == n_groups * nt - 1)
        def _():
            for cp in _chip_gather_copies(ride_in, ride_out, *sems):
                cp.wait()

    def core(xb_ref, xp_ref, gate_ref, cw_ref, cb_ref, wa_ref, ba_ref, wx_ref, bx_ref, ap_ref,
             y_ref, hs_ref, h_ref, a_s, b_s):
        i = pl.program_id(1)

        @pl.when(i == 0)
        def _():
            h_ref[...] = jnp.zeros_like(h_ref)

        keep = (i > 0).astype(F32)
        for n in range(ncol):
            sl = slice(n * LANES, (n + 1) * LANES)
            xb = xb_ref[:, sl]
            xp = xp_ref[:, sl] * keep
            xc = cb_ref[:, sl] + cw_ref[3:4, sl] * xb
            for s in range(1, CONV_WIDTH):
                xc = xc + cw_ref[3 - s:4 - s, sl] * _shift_down(xp, xb, s)
            sp = _softplus(-ap_ref[:, sl])
            _, ig, a, mult, _ = _lru_gates(xc, wa_ref[n].astype(BF16), ba_ref[:, sl],
                                           wx_ref[n].astype(BF16), bx_ref[:, sl], sp)
            a_s[:, sl] = a
            b_s[:, sl] = mult * (ig * xc)

        def step(g, h):
            base = pl.multiple_of(g * SUBLANES, SUBLANES)
            for r in range(SUBLANES):
                h = a_s[pl.ds(base + r, 1), :] * h + b_s[pl.ds(base + r, 1), :]
                hs_ref[pl.ds(base + r, 1), :] = h
            return h

        h = lax.fori_loop(0, tt // SUBLANES, step, h_ref[0:1, :])
        h_ref[0:1, :] = h
        gate = gate_ref[...]
        y_ref[...] = (hs_ref[...] * (gate * _sigmoid(gate))).astype(y_ref.dtype)

    any_spec = pl.BlockSpec(memory_space=pl.ANY)
    ride_sems = [pltpu.SemaphoreType.DMA((3 * nride,)), pltpu.SemaphoreType.DMA((3 * nride,)),
                 pltpu.SemaphoreType.DMA((nride,))] if nride else []
    outs = pl.pallas_call(
        body, name=name, grid=(n_groups, nt),
        in_specs=[cur(0), prev(0), cur(1), chan(CONV_WIDTH), chan(1), wblk, chan(1), wblk, chan(1), chan(1)]
        + [any_spec] * nride,
        out_specs=[plain, plain] + [any_spec] * nride,
        out_shape=[jax.ShapeDtypeStruct((t, w), BF16), jax.ShapeDtypeStruct((t, w), F32)]
        + [jax.ShapeDtypeStruct((N_CHIPS,) + r.shape, r.dtype) for r in ride],
        scratch_shapes=[pltpu.VMEM((SUBLANES, cg), F32), pltpu.VMEM((tt, cg), F32), pltpu.VMEM((tt, cg), F32)]
        + ride_sems,
        compiler_params=_cparams(("arbitrary", "arbitrary")),
    )(u, u, u, conv_w, conv_b, wa, ba, wx, bx, a_param, *ride)
    return outs[0], outs[1], outs[2:]


def _lru_bwd(u, hs, dy, conv_w, conv_b, wa, ba, wx, bx, a_param, *, cg, name, ride=()):
    nride = len(ride)
    t, w2 = u.shape
    w = w2 // 2
    n_groups = w // cg
    ncol = cg // LANES
    tt = _tile(t, 256)
    nt = t // tt
    cur, prev, chan, wblk, plain, plain_prev = _lru_specs(tt, cg, n_groups, nt, True)
    tn_dims = (((0,), (0,)), ((), ()))
    nt_dims = (((1,), (1,)), ((), ()))

    def body(*refs):
        n_in, n_out, n_scr = 13, 9, 5
        ins, rest = refs[:n_in], refs[n_in:]
        ride_in, rest = rest[:nride], rest[nride:]
        outs, rest = rest[:n_out], rest[n_out:]
        ride_out, rest = rest[:nride], rest[nride:]
        scr, sems = rest[:n_scr], rest[n_scr:]
        if not nride:
            core(*ins, *outs, *scr)
            return
        step = pl.program_id(0) * nt + pl.program_id(1)

        @pl.when(step == 0)
        def _():
            for cp in _scatter_copies(ride_in, ride_out, *sems):
                cp.start()

        core(*ins, *outs, *scr)

        @pl.when(step == n_groups * nt - 1)
        def _():
            for cp in _scatter_copies(ride_in, ride_out, *sems):
                cp.wait()

    def core(xb_ref, xp_ref, gate_ref, hs_ref, hp_ref, dy_ref, cw_ref, cb_ref, wa_ref, ba_ref, wx_ref, bx_ref,
             ap_ref, dxb_ref, dgate_ref, dcw_ref, dcb_ref, dwa_ref, dba_ref, dwx_ref, dbx_ref, dsp_ref,
             c_ref, nx_ref, a_s, dhs_s, lam_s):
        i = pl.program_id(1)
        first_time_block = i == nt - 1

        @pl.when(i == 0)
        def _():
            c_ref[...] = jnp.zeros_like(c_ref)
            nx_ref[...] = jnp.zeros_like(nx_ref)
            for r in (dcw_ref, dcb_ref, dwa_ref, dba_ref, dwx_ref, dbx_ref, dsp_ref):
                r[...] = jnp.zeros_like(r)

        keep = jnp.where(first_time_block, 0.0, 1.0).astype(F32)
        gate = gate_ref[...]
        sg = _sigmoid(gate)
        dyv = dy_ref[...]
        hsv = hs_ref[...]
        dhs_s[...] = dyv * (gate * sg)
        dgate_ref[...] = (dyv * hsv * (sg * (1.0 + gate * (1.0 - sg)))).astype(dgate_ref.dtype)

        saved = []
        for n in range(ncol):
            sl = slice(n * LANES, (n + 1) * LANES)
            xb = xb_ref[:, sl]
            xp = xp_ref[:, sl] * keep
            shifted = [xb] + [_shift_down(xp, xb, s) for s in range(1, CONV_WIDTH)]
            xc = cb_ref[:, sl] + cw_ref[3:4, sl] * xb
            for s in range(1, CONV_WIDTH):
                xc = xc + cw_ref[3 - s:4 - s, sl] * shifted[s]
            sp = _softplus(-ap_ref[:, sl])
            wab = wa_ref[n].astype(BF16)
            wxb = wx_ref[n].astype(BF16)
            r, ig, a, mult, inv_mult = _lru_gates(xc, wab, ba_ref[:, sl], wxb, bx_ref[:, sl], sp)
            a_s[:, sl] = a
            saved.append((sl, shifted, xc, sp, wab, wxb, r, ig, a, mult, inv_mult))

        def step(g, c):
            base = pl.multiple_of(tt - SUBLANES - g * SUBLANES, SUBLANES)
            for r in range(SUBLANES - 1, -1, -1):
                lam = dhs_s[pl.ds(base + r, 1), :] + c
                lam_s[pl.ds(base + r, 1), :] = lam
                c = a_s[pl.ds(base + r, 1), :] * lam
            return c

        c_ref[0:1, :] = lax.fori_loop(0, tt // SUBLANES, step, c_ref[0:1, :])

        for n in range(ncol):
            sl, shifted, xc, sp, wab, wxb, r, ig, a, mult, inv_mult = saved[n]
            lam = lam_s[:, sl]
            hprev = _shift_down(hp_ref[:, sl] * keep, hs_ref[:, sl], 1)
            da = lam * hprev
            dmult = lam * (ig * xc)
            dlog_a = da * a - dmult * (a * a * inv_mult)
            di = lam * (mult * xc)
            dxc = lam * (mult * ig)
            dr = dlog_a * (-LRU_C * sp)
            dsp_ref[:, sl] += jnp.sum(dlog_a * (-LRU_C * r), axis=0, keepdims=True)
            dza = dr * (r * (1.0 - r))
            dzx = di * (ig * (1.0 - ig))
            dba_ref[:, sl] += jnp.sum(dza, axis=0, keepdims=True)
            dbx_ref[:, sl] += jnp.sum(dzx, axis=0, keepdims=True)
            xcb = xc.astype(BF16)
            dzab = dza.astype(BF16)
            dzxb = dzx.astype(BF16)
            dwa_ref[n] += lax.dot_general(xcb, dzab, tn_dims, preferred_element_type=F32)
            dwx_ref[n] += lax.dot_general(xcb, dzxb, tn_dims, preferred_element_type=F32)
            dxc = dxc + lax.dot_general(dzab, wab, nt_dims, preferred_element_type=F32)
            dxc = dxc + lax.dot_general(dzxb, wxb, nt_dims, preferred_element_type=F32)
            dcb_ref[:, sl] += jnp.sum(dxc, axis=0, keepdims=True)
            for s in range(CONV_WIDTH):
                dcw_ref[3 - s:4 - s, sl] += jnp.sum(dxc * shifted[s], axis=0, keepdims=True)
            nx = nx_ref[:, sl]
            dxb = cw_ref[3:4, sl] * dxc
            for s in range(1, CONV_WIDTH):
                dxb = dxb + cw_ref[3 - s:4 - s, sl] * _shift_up(dxc, nx, s)
            dxb_ref[:, sl] = dxb.astype(dxb_ref.dtype)
            nx_ref[:, sl] = dxc[0:SUBLANES, :]

        @pl.when(first_time_block)
        def _():
            dsp_ref[...] = dsp_ref[...] * (-_sigmoid(-ap_ref[...]))

    dxb_spec = pl.BlockSpec((tt, cg), lambda g, i: (nt - 1 - i, g))
    any_spec = pl.BlockSpec(memory_space=pl.ANY)
    ride_shape, ride_sems = _scatter_shapes(ride) if nride else ([], [])
    outs = pl.pallas_call(
        body, name=name, grid=(n_groups, nt),
        in_specs=[cur(0), prev(0), cur(1), plain, plain_prev, plain, chan(CONV_WIDTH), chan(1), wblk, chan(1), wblk,
                  chan(1), chan(1)] + [any_spec] * nride,
        out_specs=[dxb_spec, dxb_spec, chan(CONV_WIDTH), chan(1), wblk, chan(1), wblk, chan(1), chan(1)]
        + [any_spec] * nride,
        out_shape=[jax.ShapeDtypeStruct((t, w), BF16), jax.ShapeDtypeStruct((t, w), BF16),
                   jax.ShapeDtypeStruct(conv_w.shape, F32), jax.ShapeDtypeStruct(conv_b.shape, F32),
                   jax.ShapeDtypeStruct(wa.shape, F32), jax.ShapeDtypeStruct(ba.shape, F32),
                   jax.ShapeDtypeStruct(wx.shape, F32), jax.ShapeDtypeStruct(bx.shape, F32),
                   jax.ShapeDtypeStruct(a_param.shape, F32)] + ride_shape,
        scratch_shapes=[pltpu.VMEM((SUBLANES, cg), F32), pltpu.VMEM((SUBLANES, cg), F32),
                        pltpu.VMEM((tt, cg), F32), pltpu.VMEM((tt, cg), F32), pltpu.VMEM((tt, cg), F32)] + ride_sems,
        compiler_params=_cparams(("arbitrary", "arbitrary")),
    )(u, u, u, hs, hs, dy, conv_w, conv_b, wa, ba, wx, bx, a_param, *ride)
    return outs[:9], outs[9:]


def _fgate_fwd(f, b_f, *, name):
    t, n = f.shape
    tt = _tile(t, 256)
    width = FOX_HEADS * FOX_HEAD_DIM

    def body(f_ref, b_ref, cum_ref, wide_ref, carry_ref):
        i = pl.program_id(0)

        @pl.when(i == 0)
        def _():
            carry_ref[...] = jnp.zeros_like(carry_ref)

        z = f_ref[...] + b_ref[...]
        lf = jnp.minimum(z, 0.0) - _log1p(jnp.exp(-jnp.abs(z)))
        row = lax.broadcasted_iota(jnp.int32, (tt, tt), 0)
        col = lax.broadcasted_iota(jnp.int32, (tt, tt), 1)
        tri = (col <= row).astype(BF16)
        cum = _dot_01_left(tri, lf) + carry_ref[0:1, :]
        cum_ref[...] = cum
        carry_ref[0:1, :] = cum[tt - 1:tt, :]
        head = lax.broadcasted_iota(jnp.int32, (n, width), 0)
        chan = lax.broadcasted_iota(jnp.int32, (n, width), 1) // FOX_HEAD_DIM
        wide_ref[...] = _dot_01_right(cum, (head == chan).astype(BF16))

    return pl.pallas_call(
        body, name=name, grid=(t // tt,),
        in_specs=[pl.BlockSpec((tt, n), lambda i: (i, 0)), pl.BlockSpec((1, n), lambda i: (0, 0))],
        out_specs=[pl.BlockSpec((tt, n), lambda i: (i, 0)), pl.BlockSpec((tt, width), lambda i: (i, 0))],
        out_shape=[jax.ShapeDtypeStruct((t, n), F32), jax.ShapeDtypeStruct((t, width), F32)],
        scratch_shapes=[pltpu.VMEM((SUBLANES, n), F32)],
        compiler_params=_cparams(("arbitrary",)),
    )(f, b_f)


def _fgate_bwd(dcum, f, b_f, *, name):
    t, n = f.shape
    tt = _tile(t, 256)
    nt = t // tt

    def body(dc_ref, f_ref, b_ref, df_ref, db_ref, carry_ref):
        i = pl.program_id(0)

        @pl.when(i == 0)
        def _():
            carry_ref[...] = jnp.zeros_like(carry_ref)
            db_ref[...] = jnp.zeros_like(db_ref)

        row = lax.broadcasted_iota(jnp.int32, (tt, tt), 0)
        col = lax.broadcasted_iota(jnp.int32, (tt, tt), 1)
        triu = (col >= row).astype(BF16)
        dlf = _dot_01_left(triu, dc_ref[...]) + carry_ref[0:1, :]
        carry_ref[0:1, :] = dlf[0:1, :]
        z = f_ref[...] + b_ref[...]
        df = dlf * _sigmoid(-z)
        df_ref[...] = df
        db_ref[...] += jnp.sum(df, axis=0, keepdims=True)

    blk = pl.BlockSpec((tt, n), lambda i: (nt - 1 - i, 0))
    vec = pl.BlockSpec((1, n), lambda i: (0, 0))
    return pl.pallas_call(
        body, name=name, grid=(nt,),
        in_specs=[blk, blk, vec], out_specs=[blk, vec],
        out_shape=[jax.ShapeDtypeStruct((t, n), F32), jax.ShapeDtypeStruct((1, n), F32)],
        scratch_shapes=[pltpu.VMEM((SUBLANES, n), F32)],
        compiler_params=_cparams(("arbitrary",)),
    )(dcum, f, b_f)


def _attn_fwd(start, qkv, ckt, gate, *, name, tq, tk):
    t = qkv.shape[0]
    f = gate.shape[1]
    npair = f // LANES
    nq = t // tq
    ratio = tq // tk
    assert tq == ratio * tk and t == nq * tq
    scale = 1.0 / math.sqrt(FOX_HEAD_DIM)
    nt_dims = (((1,), (1,)), ((), ()))

    def body(start_ref, q_ref, k_ref, v_ref, ck_ref, g_ref, o_ref, y_ref, l_ref):
        i = pl.program_id(1)
        pair = pl.program_id(0)
        firsts = (start_ref[2 * pair, i], start_ref[2 * pair + 1, i])
        both = jnp.maximum(firsts[0], firsts[1])
        lane = lax.broadcasted_iota(jnp.int32, (tq, LANES), 1)
        lo = lane < FOX_HEAD_DIM
        q2 = q_ref[...] * scale
        qs = (jnp.where(lo, q2, 0).astype(BF16), jnp.where(lo, 0, q2).astype(BF16))
        row = lax.broadcasted_iota(jnp.int32, (tq, tk), 0)
        col = lax.broadcasted_iota(jnp.int32, (tq, tk), 1)

        def kv_step(j, carry, diag, heads=(0, 1)):
            off = pl.multiple_of(j * tk, tk)
            kj = k_ref[pl.ds(off, tk), :]
            vj = v_ref[pl.ds(off, tk), :]
            ck = ck_ref[:, pl.ds(off, tk)]
            new = list(carry)
            for h in heads:
                m, l, acc = carry[h]
                s = lax.dot_general(qs[h], kj, nt_dims, preferred_element_type=F32) - ck[h:h + 1, :]
                if diag is not None:
                    s = jnp.where(col + diag * tk <= row, s, NEG_INF)
                m_new = jnp.maximum(m, jnp.max(s, axis=-1, keepdims=True))
                alpha = jnp.exp(m - m_new)
                p = jnp.exp(s - m_new)
                l = alpha * l + jnp.sum(p, axis=-1, keepdims=True)
                acc = alpha * acc + jnp.dot(p.astype(BF16), vj, preferred_element_type=F32)
                new[h] = (m_new, l, acc)
            return tuple(new)

        carry = tuple((jnp.full((tq, 1), NEG_INF, F32), jnp.zeros((tq, 1), F32), jnp.zeros((tq, LANES), F32))
                      for _ in range(2))
        for h in range(2):
            carry = lax.fori_loop(firsts[h], both, lambda j, c, h=h: kv_step(j, c, None, (h,)), carry)
        carry = lax.fori_loop(both, i * ratio, lambda j, c: kv_step(j, c, None), carry)
        for d in range(ratio):
            carry = kv_step(i * ratio + d, carry, d)
        (m0, l0, a0), (m1, l1, a1) = carry
        o = jnp.where(lo, a0 / l0, a1 / l1)
        o_ref[...] = o
        gate_v = g_ref[...]
        y_ref[...] = (o * (gate_v * _sigmoid(gate_v))).astype(y_ref.dtype)
        lse_t = jnp.transpose(jnp.where(lo, m0 + jnp.log(l0), m1 + jnp.log(l1)))
        l_ref[0:1, :] = lse_t[0:1, :]
        l_ref[1:2, :] = lse_t[FOX_HEAD_DIM:FOX_HEAD_DIM + 1, :]

    blk = lambda base: pl.BlockSpec((tq, LANES), lambda p, i, s: (i, base + p))
    full = lambda base: pl.BlockSpec((t, LANES), lambda p, i, s: (0, base + p))
    return pl.pallas_call(
        body, name=name,
        grid_spec=pltpu.PrefetchScalarGridSpec(
            num_scalar_prefetch=1, grid=(npair, nq),
            in_specs=[blk(0), full(npair), full(2 * npair), pl.BlockSpec((None, 2, t), lambda p, i, s: (p, 0, 0)),
                      blk(0)],
            out_specs=[blk(0), blk(0), pl.BlockSpec((None, 2, tq), lambda p, i, s: (p, 0, i))]),
        out_shape=[jax.ShapeDtypeStruct((t, f), F32), jax.ShapeDtypeStruct((t, f), BF16),
                   jax.ShapeDtypeStruct((npair, 2, t), F32)],
        compiler_params=_cparams(("parallel", "arbitrary")),
    )(start, qkv, qkv, qkv, ckt, gate)


def _attn_bwd(end, qkv, do, lt, dt, cke, *, name):
    t, f = do.shape
    npair = f // LANES
    tk = _tile(t, ATTN_TILE)
    nk = t // tk
    scale = 1.0 / math.sqrt(FOX_HEAD_DIM)
    nt_dims = (((1,), (1,)), ((), ()))
    tn_dims = (((0,), (0,)), ((), ()))

    def body(end_ref, k_ref, v_ref, q_ref, do_ref, l_ref, d_ref, ck_ref, dq_out_ref, dk_ref, dv_ref, dck_ref, dcq_ref,
             dq_ref, dk_s, dv_s, dck_s):
        j = pl.program_id(1)
        pair = pl.program_id(0)
        lasts = (end_ref[2 * pair, j], end_ref[2 * pair + 1, j])
        both = jnp.minimum(lasts[0], lasts[1])

        @pl.when(j == 0)
        def _():
            dq_ref[...] = jnp.zeros_like(dq_ref)
            dcq_ref[...] = jnp.zeros_like(dcq_ref)

        lane = lax.broadcasted_iota(jnp.int32, (tk, LANES), 1)
        lo = lane < FOX_HEAD_DIM
        sel = (lo, jnp.logical_not(lo))
        kj = k_ref[...]
        vj = v_ref[...]
        km = tuple(jnp.where(sel[h], kj, 0).astype(BF16) for h in range(2))
        ckv = ck_ref[...]
        ckh = (ckv[:, 0:1], ckv[:, FOX_HEAD_DIM:FOX_HEAD_DIM + 1])
        row = lax.broadcasted_iota(jnp.int32, (tk, tk), 0)
        col = lax.broadcasted_iota(jnp.int32, (tk, tk), 1)
        causal = row <= col

        def q_step(i, carry, masked, heads=(0, 1)):
            off = pl.multiple_of(i * tk, tk)
            qi = q_ref[pl.ds(off, tk), :]
            doi = do_ref[pl.ds(off, tk), :]
            lrow = l_ref[:, pl.ds(off, tk)]
            drow = d_ref[:, pl.ds(off, tk)]
            dq_add = jnp.zeros((tk, LANES), F32)
            for h in heads:
                qm = jnp.where(sel[h], qi, 0).astype(BF16)
                dom = jnp.where(sel[h], doi, 0).astype(BF16)
                st = lax.dot_general(kj, qm, nt_dims, preferred_element_type=F32) * scale
                st = st - ckh[h] - lrow[h:h + 1, :]
                if masked:
                    st = jnp.where(causal, st, NEG_INF)
                pt = jnp.exp(st)
                dpt = lax.dot_general(vj, dom, nt_dims, preferred_element_type=F32)
                dst = pt * (dpt - drow[h:h + 1, :])
                ptb = pt.astype(BF16)
                dstb = dst.astype(BF16)
                dv_s[...] += jnp.dot(ptb, dom, preferred_element_type=F32)
                dk_s[...] += jnp.dot(dstb, qm, preferred_element_type=F32)
                dq_add = dq_add + lax.dot_general(dstb, km[h], tn_dims, preferred_element_type=F32)
                dck_s[:, h:h + 1] -= jnp.sum(dst, axis=-1, keepdims=True)
                dcq_ref[h:h + 1, pl.ds(off, tk)] += jnp.sum(dst, axis=0, keepdims=True)
            dq_ref[pl.ds(off, tk), :] += dq_add * scale
            return carry

        dk_s[...] = jnp.zeros_like(dk_s)
        dv_s[...] = jnp.zeros_like(dv_s)
        dck_s[...] = jnp.zeros_like(dck_s)
        carry = 0
        carry = q_step(j, carry, True)
        carry = lax.fori_loop(j + 1, both, lambda i, c: q_step(i, c, False), carry)
        for h in range(2):
            carry = lax.fori_loop(both, lasts[h], lambda i, c, h=h: q_step(i, c, False, (h,)), carry)
        dk_acc, dv_acc = dk_s[...], dv_s[...]
        dck = (dck_s[:, 0:1], dck_s[:, 1:2])
        dk_ref[...] = (dk_acc * scale).astype(dk_ref.dtype)
        dv_ref[...] = dv_acc.astype(dv_ref.dtype)
        dck_t = jnp.transpose(jnp.where(lo, dck[0], dck[1]))
        dck_ref[0:1, :] = dck_t[0:1, :]
        dck_ref[1:2, :] = dck_t[FOX_HEAD_DIM:FOX_HEAD_DIM + 1, :]

        @pl.when(j == nk - 1)
        def _():
            dq_out_ref[...] = dq_ref[...].astype(dq_out_ref.dtype)

    blk = lambda base: pl.BlockSpec((tk, LANES), lambda p, j, e: (j, base + p))
    full = lambda base: pl.BlockSpec((t, LANES), lambda p, j, e: (0, base + p))
    rows = pl.BlockSpec((None, 2, t), lambda p, j, e: (p, 0, 0))
    return pl.pallas_call(
        body, name=name,
        grid_spec=pltpu.PrefetchScalarGridSpec(
            num_scalar_prefetch=1, grid=(npair, nk),
            in_specs=[blk(npair), blk(2 * npair), full(0), full(0), rows, rows, blk(0)],
            out_specs=[full(0), blk(0), blk(0), pl.BlockSpec((None, 2, tk), lambda p, j, e: (p, 0, j)), rows],
            scratch_shapes=[pltpu.VMEM((t, LANES), F32), pltpu.VMEM((tk, LANES), F32), pltpu.VMEM((tk, LANES), F32),
                            pltpu.VMEM((tk, LANES), F32)]),
        out_shape=[jax.ShapeDtypeStruct((t, f), BF16), jax.ShapeDtypeStruct((t, f), BF16),
                   jax.ShapeDtypeStruct((t, f), BF16), jax.ShapeDtypeStruct((npair, 2, t), F32),
                   jax.ShapeDtypeStruct((npair, 2, t), F32)],
        compiler_params=_cparams(("parallel", "arbitrary")),
    )(end, qkv, qkv, qkv, do, lt, dt, cke)


ATTN_TILE = 512
ATTN_FWD_QUERIES = 512
EXP_ZERO = -104.0
BOUND_SLACK = 1.02


def _attn_row_stats(qkv, *, name):
    t = qkv.shape[0]
    f = qkv.shape[1] // 3
    tt = _tile(t, 512)

    def body(q_ref, k_ref, s_ref):
        q = q_ref[...].astype(F32)
        k = k_ref[...].astype(F32)
        chan = lax.broadcasted_iota(jnp.int32, (f, LANES), 0) // FOX_HEAD_DIM
        lane = lax.broadcasted_iota(jnp.int32, (f, LANES), 1)
        acc = jnp.zeros((tt, LANES), F32)
        for off, val in ((0, q * q), (FOX_HEADS, q * k), (2 * FOX_HEADS, k * k)):
            pick = (chan == lane - off).astype(BF16)
            acc = acc + jnp.dot(val.astype(BF16), pick, preferred_element_type=F32)
        s_ref[...] = acc

    return pl.pallas_call(
        body, name=name, grid=(t // tt,),
        in_specs=[pl.BlockSpec((tt, f), lambda i: (i, 0)), pl.BlockSpec((tt, f), lambda i: (i, 1))],
        out_specs=pl.BlockSpec((tt, LANES), lambda i: (i, 0)),
        out_shape=jax.ShapeDtypeStruct((t, LANES), F32),
        compiler_params=_cparams(("parallel",)),
    )(qkv, qkv)


def _attn_skip_tables(stats, cum16, tile):
    t = stats.shape[0]
    nb = t // tile
    scale = 1.0 / math.sqrt(FOX_HEAD_DIM)
    qn = jnp.sqrt(stats[:, :FOX_HEADS]) * scale
    sii = stats[:, FOX_HEADS:2 * FOX_HEADS] * scale - cum16
    kmax = jnp.max(jnp.sqrt(stats[:, 2 * FOX_HEADS:3 * FOX_HEADS]), axis=0, keepdims=True)
    arow = qn * kmax * BOUND_SLACK - sii + 0.5 * BOUND_SLACK
    a_blk = jnp.max(arow.reshape(nb, tile, FOX_HEADS), axis=1)
    c_blk = -cum16.reshape(nb, tile, FOX_HEADS)[:, tile - 1, :]
    dead = (a_blk[:, None, :] + c_blk[None, :, :]) < EXP_ZERO
    start_h = jnp.sum(dead.astype(jnp.int32), axis=1)
    blk = jnp.arange(nb, dtype=jnp.int32)
    start = jnp.minimum(start_h, blk[:, None]).T
    needs = start[:, :, None] <= blk[None, None, :]
    end = jnp.max(jnp.where(needs, blk[None, :, None] + 1, 0), axis=1)
    return start, jnp.maximum(end, blk[None, :] + 1)


def _fox_post_bwd(dy, o, gate, *, name):
    t, f = dy.shape
    tt = _tile(t, 512)

    def body(dy_ref, o_ref, g_ref, do_ref, dg_ref, dl_ref):
        g = g_ref[...]
        sg = _sigmoid(g)
        dyv = dy_ref[...]
        ov = o_ref[...]
        do = dyv * (g * sg)
        do_ref[...] = do.astype(do_ref.dtype)
        dg_ref[...] = (dyv * ov * (sg * (1.0 + g * (1.0 - sg)))).astype(dg_ref.dtype)
        chan = lax.broadcasted_iota(jnp.int32, (f, LANES), 0)
        head = lax.broadcasted_iota(jnp.int32, (f, LANES), 1)
        pick = (chan // FOX_HEAD_DIM == head).astype(BF16)
        dl_ref[...] = _dot_01_right(do * ov, pick)

    blk = pl.BlockSpec((tt, f), lambda i: (i, 0))
    return pl.pallas_call(
        body, name=name, grid=(t // tt,),
        in_specs=[blk, blk, blk], out_specs=[blk, blk, pl.BlockSpec((tt, LANES), lambda i: (i, 0))],
        out_shape=[jax.ShapeDtypeStruct((t, f), BF16), jax.ShapeDtypeStruct((t, f), BF16),
                   jax.ShapeDtypeStruct((t, LANES), F32)],
        compiler_params=_cparams(("parallel",)),
    )(dy, o, gate)


def _adamw(w, g, m, v, *, name):
    _, r, c = w.shape
    tr = _tile(r, 256) if r % SUBLANES == 0 else r
    c1 = 1.0 - ADAM_B1 ** ADAM_STEP
    c2 = 1.0 - ADAM_B2 ** ADAM_STEP

    def body(w_ref, g_ref, m_ref, v_ref, d_ref, mo_ref, vo_ref):
        gv = g_ref[...]
        mn = ADAM_B1 * m_ref[...] + (1.0 - ADAM_B1) * gv
        vn = ADAM_B2 * v_ref[...] + (1.0 - ADAM_B2) * (gv * gv)
        mo_ref[...] = mn
        vo_ref[...] = vn
        d_ref[...] = -ADAM_LR * ((mn / c1) / (jnp.sqrt(vn / c2) + ADAM_EPS) + ADAM_WD * w_ref[...])

    blk = pl.BlockSpec((None, tr, c), lambda i: (0, i, 0))
    return pl.pallas_call(
        body, name=name, grid=(r // tr,), in_specs=[blk] * 4, out_specs=[blk] * 3,
        out_shape=[jax.ShapeDtypeStruct((1, r, c), F32)] * 3,
        compiler_params=_cparams(("parallel",)),
    )(w, g, m, v)


def _sum_slots(land, *, name):
    ns, r, c = land.shape
    tr = _tile(r, 64) if r % SUBLANES == 0 else r

    def body(l_ref, o_ref):
        acc = l_ref[0].astype(F32)
        for s in range(1, ns):
            acc = acc + l_ref[s].astype(F32)
        o_ref[...] = acc

    return pl.pallas_call(
        body, name=name, grid=(r // tr,),
        in_specs=[pl.BlockSpec((ns, tr, c), lambda i: (0, i, 0))],
        out_specs=pl.BlockSpec((tr, c), lambda i: (i, 0)),
        out_shape=jax.ShapeDtypeStruct((r, c), F32),
        compiler_params=_cparams(("parallel",)),
    )(land)


ANY = pl.BlockSpec(memory_space=pl.ANY)


def _flip(v, bit):
    return 1 - v if bit else v


def _gather_chips(shards, small, *, name):
    n = len(shards)
    rels = ((1, 0), (0, 1), (1, 1))

    def body(*refs):
        ins, small_in = refs[:n], refs[n]
        outs, small_out = refs[n + 1:2 * n + 1], refs[2 * n + 1]
        send, recv, loc = refs[2 * n + 2:]
        x, y, c = lax.axis_index("x"), lax.axis_index("y"), lax.axis_index("c")
        me = 2 * x + y
        sibling = (x, y, 1 - c)
        local = [pltpu.make_async_copy(ins[k], outs[k].at[me], loc.at[k]) for k in range(n)]
        local.append(pltpu.make_async_copy(small_in, small_out.at[me], loc.at[n]))
        for cp in local:
            cp.start()

        def rows(k):
            half = ins[k].shape[0] // 2
            return pl.ds(pl.multiple_of(c * half, SUBLANES), half)

        sends = []
        for r, (rx, ry) in enumerate(rels):
            to = (_flip(x, rx), _flip(y, ry), c)
            for k in range(n):
                cp = pltpu.make_async_remote_copy(
                    src_ref=ins[k].at[rows(k), :], dst_ref=outs[k].at[me, rows(k), :],
                    send_sem=send.at[r * n + k], recv_sem=recv.at[r * n + k], device_id=to, device_id_type=MESH)
                cp.start()
                sends.append(cp)
            cp = pltpu.make_async_remote_copy(
                src_ref=small_in, dst_ref=small_out.at[me], send_sem=send.at[6 * n + r], recv_sem=recv.at[6 * n + r],
                device_id=to, device_id_type=MESH)
            cp.start()
            sends.append(cp)
        for r, (rx, ry) in enumerate(rels):
            src_chip = 2 * _flip(x, rx) + _flip(y, ry)
            for k in range(n):
                landed = outs[k].at[src_chip, rows(k), :]
                sends[r * (n + 1) + k].wait_recv()
                cp = pltpu.make_async_remote_copy(
                    src_ref=landed, dst_ref=landed, send_sem=send.at[3 * n + r * n + k],
                    recv_sem=recv.at[3 * n + r * n + k], device_id=sibling, device_id_type=MESH)
                cp.start()
                sends.append(cp)
            sends[r * (n + 1) + n].wait_recv()
        for cp in sends[:3 * (n + 1)]:
            cp.wait_send()
        for cp in sends[3 * (n + 1):]:
            cp.wait()
        for cp in local:
            cp.wait()

    vmem = pl.BlockSpec(memory_space=pltpu.VMEM)
    return pl.pallas_call(
        body, name=name, in_specs=[vmem] * (n + 1), out_specs=[vmem] * (n + 1),
        out_shape=[jax.ShapeDtypeStruct((N_CHIPS,) + s.shape, s.dtype) for s in list(shards) + [small]],
        scratch_shapes=[pltpu.SemaphoreType.DMA((6 * n + 3,)), pltpu.SemaphoreType.DMA((6 * n + 3,)),
                        pltpu.SemaphoreType.DMA((n + 1,))],
        compiler_params=pltpu.CompilerParams(has_side_effects=True, vmem_limit_bytes=VMEM_LIMIT),
    )(*shards, small)


_RELS7 = tuple((r >> 2 & 1, r >> 1 & 1, r & 1) for r in range(1, N_DEV))


def _scatter_copies(ins, outs, send, recv, loc):
    n = len(ins)
    x, y, c = lax.axis_index("x"), lax.axis_index("y"), lax.axis_index("c")
    me = 4 * x + 2 * y + c

    def piece(k, px, py, pc):
        half = ins[k].shape[1] // 2
        return ins[k].at[2 * px + py, pl.ds(pc * half, half), :]

    copies = [pltpu.make_async_copy(piece(k, x, y, c), outs[k].at[me], loc.at[k]) for k in range(n)]
    for r, (rx, ry, rc) in enumerate(_RELS7):
        tx, ty, tc = _flip(x, rx), _flip(y, ry), _flip(c, rc)
        for k in range(n):
            copies.append(pltpu.make_async_remote_copy(
                src_ref=piece(k, tx, ty, tc), dst_ref=outs[k].at[me], send_sem=send.at[r * n + k],
                recv_sem=recv.at[r * n + k], device_id=(tx, ty, tc), device_id_type=MESH))
    return copies


def _scatter_shapes(grads):
    n = len(grads)
    out_shape = [jax.ShapeDtypeStruct((N_DEV, g.shape[1] // 2, g.shape[2]), g.dtype) for g in grads]
    sems = [pltpu.SemaphoreType.DMA((7 * n,)), pltpu.SemaphoreType.DMA((7 * n,)), pltpu.SemaphoreType.DMA((n,))]
    return out_shape, sems


def _join_cores(halves, *, name):
    n = len(halves)

    def body(*refs):
        ins, outs = refs[:n], refs[n:2 * n]
        send, recv, loc = refs[2 * n:]
        x, y, c = lax.axis_index("x"), lax.axis_index("y"), lax.axis_index("c")
        copies = []
        for k in range(n):
            half = ins[k].shape[0]
            mine = outs[k].at[0, pl.ds(c * half, half), :]
            cp = pltpu.make_async_copy(ins[k], mine, loc.at[k])
            cp.start()
            copies.append(cp)
            cp = pltpu.make_async_remote_copy(
                src_ref=ins[k], dst_ref=mine, send_sem=send.at[k], recv_sem=recv.at[k],
                device_id=(x, y, 1 - c), device_id_type=MESH)
            cp.start()
            copies.append(cp)
        for cp in copies:
            cp.wait()

    in_vmem = pl.BlockSpec(memory_space=pltpu.VMEM)
    return pl.pallas_call(
        body, name=name, in_specs=[in_vmem] * n, out_specs=[in_vmem] * n,
        out_shape=[jax.ShapeDtypeStruct((1, 2 * h.shape[0], h.shape[1]), h.dtype) for h in halves],
        scratch_shapes=[pltpu.SemaphoreType.DMA((n,)), pltpu.SemaphoreType.DMA((n,)), pltpu.SemaphoreType.DMA((n,))],
        compiler_params=pltpu.CompilerParams(has_side_effects=True, vmem_limit_bytes=VMEM_LIMIT),
    )(*halves)


def _allreduce_small(buf, *, name):
    r, n = buf.shape
    half = r // 2
    rels = ((1, 0), (0, 1), (1, 1))

    def body(in_ref, out_ref, sib_ref, chips_ref, send, recv):
        x, y, c = lax.axis_index("x"), lax.axis_index("y"), lax.axis_index("c")
        sibling = (x, y, 1 - c)
        chip = 2 * x + y
        rows = pl.ds(pl.multiple_of(c * half, SUBLANES), half)

        swap = pltpu.make_async_remote_copy(src_ref=in_ref, dst_ref=sib_ref, send_sem=send.at[0], recv_sem=recv.at[0],
                                            device_id=sibling, device_id_type=MESH)
        swap.start()
        swap.wait()
        chips_ref[chip] = in_ref[rows, :] + sib_ref[rows, :]

        sends = []
        for k, (rx, ry) in enumerate(rels):
            cp = pltpu.make_async_remote_copy(
                src_ref=chips_ref.at[chip], dst_ref=chips_ref.at[chip], send_sem=send.at[1 + k],
                recv_sem=recv.at[1 + k], device_id=(_flip(x, rx), _flip(y, ry), c), device_id_type=MESH)
            cp.start()
            sends.append(cp)
        for cp in sends:
            cp.wait()
        total = chips_ref[0]
        for s in range(1, N_CHIPS):
            total = total + chips_ref[s]
        out_ref[rows, :] = total

        back = pltpu.make_async_remote_copy(src_ref=out_ref.at[rows, :], dst_ref=out_ref.at[rows, :],
                                            send_sem=send.at[4], recv_sem=recv.at[4],
                                            device_id=sibling, device_id_type=MESH)
        back.start()
        back.wait()

    vmem = pl.BlockSpec(memory_space=pltpu.VMEM)
    return pl.pallas_call(
        body, name=name, in_specs=[vmem], out_specs=vmem,
        out_shape=jax.ShapeDtypeStruct((r, n), F32),
        scratch_shapes=[pltpu.VMEM((r, n), F32), pltpu.VMEM((N_CHIPS, half, n), F32),
                        pltpu.SemaphoreType.DMA((5,)), pltpu.SemaphoreType.DMA((5,))],
        compiler_params=pltpu.CompilerParams(has_side_effects=True, vmem_limit_bytes=VMEM_LIMIT),
    )(buf)


def _pack(arrs):
    flat = []
    for a in arrs:
        v = a.reshape(-1)
        pad = (-v.shape[0]) % LANES
        if pad:
            v = jnp.pad(v, (0, pad))
        flat.append(v)
    v = jnp.concatenate(flat)
    pad = (-v.shape[0]) % (LANES * SUBLANES)
    if pad:
        v = jnp.pad(v, (0, pad))
    return v.reshape(-1, LANES)


def _unpack(buf, shapes):
    v = buf.reshape(-1)
    out, off = [], 0
    for s in shapes:
        n = math.prod(s)
        out.append(v[off:off + n].reshape(s))
        off += n + (-n) % LANES
    return out


def kernel(x, norm_g, final_g, lru_w_in, lru_conv_w, lru_conv_b, lru_wa, lru_ba, lru_wx, lru_bx, lru_a_param, lru_w_out, fox_w_in, fox_b_f, fox_w_out, loss_target, m_norm_g, m_final_g, m_lru_w_in, m_lru_conv_w, m_lru_conv_b, m_lru_wa, m_lru_ba, m_lru_wx, m_lru_bx, m_lru_a_param, m_lru_w_out, m_fox_w_in, m_fox_b_f, m_fox_w_out, v_norm_g, v_final_g, v_lru_w_in, v_lru_conv_w, v_lru_conv_b, v_lru_wa, v_lru_ba, v_lru_wx, v_lru_bx, v_lru_a_param, v_lru_w_out, v_fox_w_in, v_fox_b_f, v_fox_w_out):
    t, d = x.shape[1], x.shape[2]
    w = lru_wa.shape[1] * LRU_BLOCK_W
    f = FOX_HEADS * FOX_HEAD_DIM
    npair = f // LANES
    x0 = x.reshape(t, d)
    tgt = loss_target.reshape(t, d)
    chip = 2 * lax.axis_index("x") + lax.axis_index("y")

    g_lwi, g_lwo, g_cw = _gather_chips(
        [lru_w_in[0].astype(BF16), lru_w_out[0].astype(BF16)], lru_conv_w[0], name="gather_weights")
    cg = w // 2
    lwi = jnp.concatenate([g_lwi[0], g_lwi[2], g_lwi[1], g_lwi[3]], axis=1)
    lwo = g_lwo.reshape(w, d)
    conv_w = jnp.concatenate([g_cw[s] for s in range(N_CHIPS)], axis=1)
    conv_b, ba, bx, a_param = lru_conv_b, lru_ba, lru_bx, lru_a_param
    wa, wx = lru_wa[0], lru_wx[0]
    b_f = jnp.pad(fox_b_f, ((0, 0), (0, LANES - FOX_HEADS)))

    h0 = _rmsnorm(x0, norm_g[0], name="norm0")
    u = _matmul(h0, lwi, name="lru_in")
    y1, hs, (g_fwi, g_fwo) = _lru_fwd(u, conv_w, conv_b, wa, ba, wx, bx, a_param, cg=cg, name="lru_fwd",
                                      ride=[fox_w_in[0].astype(BF16), fox_w_out[0].astype(BF16)])
    fwi = jnp.concatenate([g_fwi[s] for s in range(N_CHIPS)], axis=1)
    w_qkv, w_g2 = fwi[:, :3 * f], fwi[:, 3 * f:4 * f]
    w_f = jnp.pad(fwi[:, 4 * f:], ((0, 0), (0, LANES - FOX_HEADS)))
    fwo = g_fwo.reshape(f, d)
    x1 = _matmul(y1, lwo, add=x0, name="lru_out")
    h1 = _rmsnorm(x1, norm_g[1], name="norm1")
    qkv = _matmul(h1, w_qkv, out_dtype=BF16, name="fox_qkv")
    gate2 = _matmul(h1, w_g2, name="fox_gate")
    flog = _matmul(h1, w_f, name="fox_f")
    cum, cke = _fgate_fwd(flog, b_f, name="fgate_fwd")
    cum16 = cum[:, :FOX_HEADS]
    ckt = cum16.T.reshape(npair, 2, t)
    a_tk, a_tq = _tile(t, ATTN_TILE), _tile(t, ATTN_FWD_QUERIES)
    a_start, a_end = _attn_skip_tables(_attn_row_stats(qkv, name="attn_row_stats"), cum16, a_tk)
    a_start_fwd = jnp.min(a_start.reshape(FOX_HEADS, t // a_tq, a_tq // a_tk), axis=2)
    o, y2, lse = _attn_fwd(a_start_fwd, qkv, ckt, gate2, name="attn_fwd", tq=a_tq, tk=a_tk)
    x2 = _matmul(y2, fwo, add=x1, name="fox_out")
    lsum, dx2, dgf = _final_loss(x2, tgt, final_g, name="final_loss")
    loss = lax.psum(0.5 * jnp.sum(lsum) / d, ("x", "y", "c"))

    d_fwo = _matmul(y2, dx2, ta=True, out_dtype=BF16, name="d_fox_w_out")
    dy2 = _matmul(dx2, fwo, tb=True, name="d_y2")
    do, dgate2, dl = _fox_post_bwd(dy2, o, gate2, name="fox_post_bwd")
    lt = lse
    dt = dl[:, :FOX_HEADS].T.reshape(npair, 2, t)
    dq, dk, dv, dck, dcq = _attn_bwd(a_end, qkv, do, lt, dt, cke, name="attn_bwd")
    dcum = jnp.pad((dck + dcq).reshape(FOX_HEADS, t).T, ((0, 0), (0, LANES - FOX_HEADS)))
    dflog, db_f = _fgate_bwd(dcum, flog, b_f, name="fgate_bwd")
    du2 = [dq, dk, dv, dgate2]
    dflog_b = dflog.astype(BF16)
    dh1 = _matmul_kparts(du2, fwi[:, :4 * f], chunk=f, name="d_h1_a")
    dh1 = _matmul(dflog_b, w_f, tb=True, add=dh1, name="d_h1_b")
    d_fwi_a = _matmul_nparts(h1, du2, chunk=f, out_dtype=BF16, name="d_fox_w_in_a")
    d_fwi_b = _matmul(h1, dflog_b, ta=True, out_dtype=BF16, name="d_fox_w_in_b")
    d_fwi = jnp.concatenate([d_fwi_a, d_fwi_b[:, :FOX_HEADS]], axis=1)
    dx1, dg1, _ = _rmsnorm_bwd(dh1, x1, norm_g[1], dx2, name="norm1_bwd")

    d_lwo = _matmul(y1, dx1, ta=True, out_dtype=BF16, name="d_lru_w_out")
    dy1 = _matmul(dx1, lwo, tb=True, name="d_y1")
    n_fwi = fox_w_in.shape[2]
    g_fwi4 = jnp.stack([d_fwi[:, s * n_fwi:(s + 1) * n_fwi] for s in range(N_CHIPS)])
    g_fwo4 = d_fwo.reshape(N_CHIPS, f // N_CHIPS, d)
    g_lwo4 = d_lwo.reshape(N_CHIPS, w // N_CHIPS, d)
    (dxb, dgate, d_cw, d_cb, d_wa, d_ba, d_wx, d_bx, d_ap), lands_early = _lru_bwd(
        u, hs, dy1, conv_w, conv_b, wa, ba, wx, bx, a_param, cg=cg, name="lru_bwd", ride=[g_lwo4, g_fwi4, g_fwo4])
    dh0 = _matmul_kparts([dxb, dgate], lwi, chunk=cg, name="d_h0")
    d_lwi_p = _matmul_nparts(h0, [dxb, dgate], chunk=cg, out_dtype=BF16, name="d_lru_w_in")
    csz = cg
    g_lwi4 = jnp.stack([d_lwi_p[:, 0:csz], d_lwi_p[:, 2 * csz:3 * csz], d_lwi_p[:, csz:2 * csz],
                        d_lwi_p[:, 3 * csz:]])
    dx0, dg0, lands_last = _rmsnorm_bwd(dh0, x0, norm_g[0], dx1, name="norm0_bwd", ride=[g_lwi4])
    lands = list(lands_last) + list(lands_early)
    halves = [_sum_slots(l, name="sum_" + nm) for l, nm in zip(lands, ("lru_w_in", "lru_w_out", "fox_w_in", "fox_w_out"))]
    big_g = _join_cores(halves, name="join_cores")

    small_g = [jnp.concatenate([dg0, dg1], axis=0), dgf.reshape(d), d_cw, d_cb, d_wa, d_ba, d_wx, d_bx, d_ap,
               db_f[:, :FOX_HEADS]]
    gsum = _allreduce_small(_pack(small_g), name="allreduce_small")
    zc = jnp.zeros((CONV_WIDTH, w), F32)
    pk_w = _pack([norm_g, final_g, zc, lru_conv_b, lru_wa, lru_ba, lru_wx, lru_bx, lru_a_param, fox_b_f])
    pk_m = _pack([m_norm_g, m_final_g, zc, m_lru_conv_b, m_lru_wa, m_lru_ba, m_lru_wx, m_lru_bx, m_lru_a_param,
                  m_fox_b_f])
    pk_v = _pack([v_norm_g, v_final_g, zc + 1.0, v_lru_conv_b, v_lru_wa, v_lru_ba, v_lru_wx, v_lru_bx,
                  v_lru_a_param, v_fox_b_f])
    s_delta, s_m, s_v = _adamw(pk_w[None], gsum[None], pk_m[None], pk_v[None], name="adamw_small")
    out_shapes = [norm_g.shape, final_g.shape, (CONV_WIDTH, w), lru_conv_b.shape, lru_wa.shape, lru_ba.shape,
                  lru_wx.shape, lru_bx.shape, lru_a_param.shape, fox_b_f.shape]
    sg = _unpack(gsum, out_shapes)
    sd = _unpack(s_delta, out_shapes)
    sm = _unpack(s_m, out_shapes)
    sv = _unpack(s_v, out_shapes)

    ncw = lru_conv_w.shape[2]
    g_cw_loc = lax.dynamic_slice_in_dim(sg[2], chip * ncw, ncw, axis=1)
    g_cw_loc = g_cw_loc[None]
    cw_d, cw_m, cw_v = _adamw(lru_conv_w, g_cw_loc, m_lru_conv_w, v_lru_conv_w, name="adamw_conv_w")

    big = []
    for nm, wt, g, mm, vv in (("lru_w_in", lru_w_in, big_g[0], m_lru_w_in, v_lru_w_in),
                              ("lru_w_out", lru_w_out, big_g[1], m_lru_w_out, v_lru_w_out),
                              ("fox_w_in", fox_w_in, big_g[2], m_fox_w_in, v_fox_w_in),
                              ("fox_w_out", fox_w_out, big_g[3], m_fox_w_out, v_fox_w_out)):
        big.append((g,) + tuple(_adamw(wt, g, mm, vv, name="adamw_" + nm)))

    def assemble(idx):
        small = (sg, sd, sm, sv)[idx]
        cw = (g_cw_loc, cw_d, cw_m, cw_v)[idx]
        return [small[0], small[1], big[0][idx], cw, small[3], small[4], small[5], small[6], small[7], small[8],
                big[1][idx], big[2][idx], small[9], big[3][idx]]

    grad_x = dx0.reshape(1, t, d)
    return (loss, grad_x, *assemble(0), *assemble(1), *assemble(2), *assemble(3))
```

```python
import math

import jax
import jax.numpy as jnp
from jax import lax
from jax.experimental import pallas as pl
from jax.experimental.pallas import tpu as pltpu

F32 = jnp.float32
BF16 = jnp.bfloat16

EPS = 1e-6
LRU_C = 8.0
LRU_BLOCK_W = 128
CONV_WIDTH = 4
FOX_HEADS = 16
FOX_HEAD_DIM = 64
NEG_INF = -1e30
ADAM_LR = 0.001
ADAM_B1 = 0.9
ADAM_B2 = 0.999
ADAM_EPS = 1e-08
ADAM_WD = 0.01
ADAM_STEP = 10

LANES = 128
SUBLANES = 8
VMEM_LIMIT = 56 * 1024 * 1024
TINY = 1e-30
N_CHIPS = 4
N_DEV = 8
MESH = pl.DeviceIdType.MESH


def _tile(n, pref):
    t = min(n, pref)
    while n % t:
        t //= 2
    return t


def _cparams(dims=None):
    return pltpu.CompilerParams(dimension_semantics=dims, vmem_limit_bytes=VMEM_LIMIT)


def _sigmoid(x):
    return 0.5 * jnp.tanh(0.5 * x) + 0.5


def _log1p(x):
    u = 1.0 + x
    return jnp.where(u == 1.0, x, jnp.log(u) * x / (u - 1.0))


def _bf16_pieces(x):
    hi = x.astype(BF16)
    rest = x - hi.astype(F32)
    mid = rest.astype(BF16)
    return hi, mid, (rest - mid.astype(F32)).astype(BF16)


def _dot_01_left(m01, x):
    return sum(jnp.dot(m01, p, preferred_element_type=F32) for p in _bf16_pieces(x))


def _dot_01_right(x, m01):
    return sum(jnp.dot(p, m01, preferred_element_type=F32) for p in _bf16_pieces(x))


def _softplus(x):
    return jnp.maximum(x, 0.0) + _log1p(jnp.exp(-jnp.abs(x)))


MM_TILE = 1024
MM_FULL_K = 1536


def _matmul(a, b, *, name, ta=False, tb=False, out_dtype=F32, add=None, tm=MM_TILE, tn=MM_TILE, tk=None):
    if ta:
        kdim, m = a.shape
    else:
        m, kdim = a.shape
    if tb:
        n, kb = b.shape
    else:
        kb, n = b.shape
    assert kdim == kb, (a.shape, b.shape, ta, tb)
    if tk is None:
        tk = kdim if kdim <= MM_FULL_K else MM_TILE
    tm, tn, tk = _tile(m, tm), _tile(n, tn), _tile(kdim, tk)
    nk = kdim // tk
    dn = (((0 if ta else 1,), (1 if tb else 0,)), ((), ()))
    has_add = add is not None

    def body(*refs):
        if has_add:
            a_ref, b_ref, add_ref, o_ref = refs[:4]
        else:
            a_ref, b_ref, o_ref = refs[:3]
        part = lax.dot_general(a_ref[...].astype(BF16), b_ref[...].astype(BF16), dn, preferred_element_type=F32)

        def finish(r):
            if has_add:
                r = r + add_ref[...].astype(F32)
            o_ref[...] = r.astype(o_ref.dtype)

        if nk == 1:
            finish(part)
            return
        acc_ref = refs[-1]
        k = pl.program_id(2)

        @pl.when(k == 0)
        def _():
            acc_ref[...] = part

        @pl.when(k > 0)
        def _():
            acc_ref[...] += part

        @pl.when(k == nk - 1)
        def _():
            finish(acc_ref[...])

    a_spec = pl.BlockSpec((tk, tm), lambda i, j, k: (k, i)) if ta else pl.BlockSpec((tm, tk), lambda i, j, k: (i, k))
    b_spec = pl.BlockSpec((tn, tk), lambda i, j, k: (j, k)) if tb else pl.BlockSpec((tk, tn), lambda i, j, k: (k, j))
    o_spec = pl.BlockSpec((tm, tn), lambda i, j, k: (i, j))
    in_specs = [a_spec, b_spec] + ([o_spec] if has_add else [])
    args = (a, b) + ((add,) if has_add else ())
    return pl.pallas_call(
        body, name=name, grid=(m // tm, n // tn, nk), in_specs=in_specs, out_specs=o_spec,
        out_shape=jax.ShapeDtypeStruct((m, n), out_dtype),
        scratch_shapes=[pltpu.VMEM((tm, tn), F32)] if nk > 1 else [],
        compiler_params=_cparams(("parallel", "parallel", "arbitrary")),
    )(*args)


def _matmul_kparts(parts, b, *, chunk, name, tm=MM_TILE, tn=MM_TILE, ride=()):
    npart = len(parts)
    nride = len(ride)
    m = parts[0].shape[0]
    n, kdim = b.shape
    nk = kdim // chunk
    assert nk * chunk == kdim and sum(p.shape[1] for p in parts) == kdim and nk % npart == 0
    tm, tn = _tile(m, tm), _tile(n, tn)
    dn = (((1,), (1,)), ((), ()))
    steps = (m // tm) * (n // tn) * nk

    def body(*refs):
        ins, rest = refs[:npart + 1], refs[npart + 1:]
        ride_in, rest = rest[:nride], rest[nride:]
        o_ref, rest = rest[0], rest[1:]
        ride_out, rest = rest[:nride], rest[nride:]
        acc_ref, sems = rest[0], rest[1:]
        if not nride:
            core(*ins, o_ref, acc_ref)
            return
        step = (pl.program_id(0) * (n // tn) + pl.program_id(1)) * nk + pl.program_id(2)

        @pl.when(step == 0)
        def _():
            for cp in _scatter_copies(ride_in, ride_out, *sems):
                cp.start()

        core(*ins, o_ref, acc_ref)

        @pl.when(step == steps - 1)
        def _():
            for cp in _scatter_copies(ride_in, ride_out, *sems):
                cp.wait()

    def core(*refs):
        a_refs, b_ref, o_ref, acc_ref = refs[:npart], refs[npart], refs[npart + 1], refs[npart + 2]
        k = pl.program_id(2)

        @pl.when(k == 0)
        def _():
            acc_ref[...] = jnp.zeros_like(acc_ref)

        for s in range(npart):
            @pl.when(lax.rem(k, npart) == s)
            def _(s=s):
                acc_ref[...] += lax.dot_general(a_refs[s][...].astype(BF16), b_ref[...].astype(BF16), dn,
                                                preferred_element_type=F32)

        @pl.when(k == nk - 1)
        def _():
            o_ref[...] = acc_ref[...].astype(o_ref.dtype)

    a_specs = [pl.BlockSpec((tm, chunk), lambda i, j, k: (i, k // npart)) for _ in range(npart)]
    any_spec = pl.BlockSpec(memory_space=pl.ANY)
    ride_shape, ride_sems = _scatter_shapes(ride) if nride else ([], [])
    outs = pl.pallas_call(
        body, name=name, grid=(m // tm, n // tn, nk),
        in_specs=a_specs + [pl.BlockSpec((tn, chunk), lambda i, j, k: (j, k))] + [any_spec] * nride,
        out_specs=[pl.BlockSpec((tm, tn), lambda i, j, k: (i, j))] + [any_spec] * nride,
        out_shape=[jax.ShapeDtypeStruct((m, n), F32)] + ride_shape,
        scratch_shapes=[pltpu.VMEM((tm, tn), F32)] + ride_sems,
        compiler_params=_cparams(("arbitrary",) * 3 if nride else ("parallel", "parallel", "arbitrary")),
    )(*parts, b, *ride)
    return (outs[0], outs[1:]) if nride else outs[0]


def _matmul_nparts(a, parts, *, chunk, out_dtype, name, tm=MM_TILE, tk=MM_TILE):
    npart = len(parts)
    t, m = a.shape
    n = sum(p.shape[1] for p in parts)
    nj = n // chunk
    assert nj * chunk == n and nj % npart == 0
    tm, tk = _tile(m, tm), _tile(t, tk)
    nk = t // tk
    dn = (((0,), (0,)), ((), ()))

    def body(*refs):
        a_ref, b_refs, o_ref, acc_ref = refs[0], refs[1:1 + npart], refs[1 + npart], refs[2 + npart]
        j, k = pl.program_id(1), pl.program_id(2)

        @pl.when(k == 0)
        def _():
            acc_ref[...] = jnp.zeros_like(acc_ref)

        for s in range(npart):
            @pl.when(lax.rem(j, npart) == s)
            def _(s=s):
                acc_ref[...] += lax.dot_general(a_ref[...].astype(BF16), b_refs[s][...].astype(BF16), dn,
                                                preferred_element_type=F32)

        @pl.when(k == nk - 1)
        def _():
            o_ref[...] = acc_ref[...].astype(o_ref.dtype)

    def b_spec(s):
        return pl.BlockSpec((tk, chunk), lambda i, j, k: (jnp.where(lax.rem(j, npart) == s, k, 0), j // npart))

    return pl.pallas_call(
        body, name=name, grid=(m // tm, nj, nk),
        in_specs=[pl.BlockSpec((tk, tm), lambda i, j, k: (k, i))] + [b_spec(s) for s in range(npart)],
        out_specs=pl.BlockSpec((tm, chunk), lambda i, j, k: (i, j)),
        out_shape=jax.ShapeDtypeStruct((m, n), out_dtype),
        scratch_shapes=[pltpu.VMEM((tm, chunk), F32)],
        compiler_params=_cparams(("parallel", "parallel", "arbitrary")),
    )(a, *parts)


def _rmsnorm(x, g, *, name):
    t, d = x.shape
    tt = _tile(t, 512)

    def body(x_ref, g_ref, o_ref):
        xf = x_ref[...]
        rstd = lax.rsqrt(jnp.mean(xf * xf, axis=-1, keepdims=True) + EPS)
        o_ref[...] = (xf * rstd * g_ref[...]).astype(o_ref.dtype)

    return pl.pallas_call(
        body, name=name, grid=(t // tt,),
        in_specs=[pl.BlockSpec((tt, d), lambda i: (i, 0)), pl.BlockSpec((1, d), lambda i: (0, 0))],
        out_specs=pl.BlockSpec((tt, d), lambda i: (i, 0)),
        out_shape=jax.ShapeDtypeStruct((t, d), BF16),
        compiler_params=_cparams(("parallel",)),
    )(x, g.reshape(1, d))


def _rmsnorm_bwd(dh, x, g, dres, *, name, ride=()):
    nride = len(ride)
    t, d = x.shape
    tt = _tile(t, 512)
    nt = t // tt

    def body(*refs):
        ins, rest = refs[:4], refs[4:]
        ride_in, rest = rest[:nride], rest[nride:]
        outs, rest = rest[:2], rest[2:]
        ride_out, sems = rest[:nride], rest[nride:]
        if not nride:
            core(*ins, *outs)
            return
        i = pl.program_id(0)

        @pl.when(i == 0)
        def _():
            for cp in _scatter_copies(ride_in, ride_out, *sems):
                cp.start()

        core(*ins, *outs)

        @pl.when(i == nt - 1)
        def _():
            for cp in _scatter_copies(ride_in, ride_out, *sems):
                cp.wait()

    def core(dh_ref, x_ref, g_ref, dres_ref, dx_ref, dg_ref):
        i = pl.program_id(0)

        @pl.when(i == 0)
        def _():
            dg_ref[...] = jnp.zeros_like(dg_ref)

        xf = x_ref[...]
        rstd = lax.rsqrt(jnp.mean(xf * xf, axis=-1, keepdims=True) + EPS)
        xhat = xf * rstd
        dhf = dh_ref[...].astype(F32)
        dxhat = dhf * g_ref[...]
        mt = jnp.mean(dxhat * xhat, axis=-1, keepdims=True)
        dx_ref[...] = dres_ref[...] + rstd * (dxhat - xhat * mt)
        dg_ref[...] += jnp.sum(dhf * xhat, axis=0, keepdims=True)

    blk = pl.BlockSpec((tt, d), lambda i: (i, 0))
    vec = pl.BlockSpec((1, d), lambda i: (0, 0))
    any_spec = pl.BlockSpec(memory_space=pl.ANY)
    ride_shape, ride_sems = _scatter_shapes(ride) if nride else ([], [])
    outs = pl.pallas_call(
        body, name=name, grid=(nt,),
        in_specs=[blk, blk, vec, blk] + [any_spec] * nride, out_specs=[blk, vec] + [any_spec] * nride,
        out_shape=[jax.ShapeDtypeStruct((t, d), F32), jax.ShapeDtypeStruct((1, d), F32)] + ride_shape,
        scratch_shapes=ride_sems,
        compiler_params=_cparams(("arbitrary",)),
    )(dh, x, g.reshape(1, d), dres, *ride)
    return outs[0], outs[1], outs[2:]


def _final_loss(x2, tgt, g, *, name):
    t, d = x2.shape
    tt = _tile(t, 512)

    def body(x_ref, t_ref, g_ref, l_ref, dx_ref, dg_ref):
        i = pl.program_id(0)

        @pl.when(i == 0)
        def _():
            dg_ref[...] = jnp.zeros_like(dg_ref)
            l_ref[...] = jnp.zeros_like(l_ref)

        xf = x_ref[...]
        gg = g_ref[...]
        rstd = lax.rsqrt(jnp.mean(xf * xf, axis=-1, keepdims=True) + EPS)
        xhat = xf * rstd
        err = xhat * gg - t_ref[...]
        l_ref[...] += jnp.sum(err * err, axis=0, keepdims=True)
        dy = err * (1.0 / d)
        dxhat = dy * gg
        mt = jnp.mean(dxhat * xhat, axis=-1, keepdims=True)
        dx_ref[...] = rstd * (dxhat - xhat * mt)
        dg_ref[...] += jnp.sum(dy * xhat, axis=0, keepdims=True)

    blk = pl.BlockSpec((tt, d), lambda i: (i, 0))
    vec = pl.BlockSpec((1, d), lambda i: (0, 0))
    return pl.pallas_call(
        body, name=name, grid=(t // tt,),
        in_specs=[blk, blk, vec], out_specs=[vec, blk, vec],
        out_shape=[jax.ShapeDtypeStruct((1, d), F32), jax.ShapeDtypeStruct((t, d), F32),
                   jax.ShapeDtypeStruct((1, d), F32)],
        compiler_params=_cparams(("arbitrary",)),
    )(x2, tgt, g.reshape(1, d))


def _shift_down(prev8, cur, s):
    ext = jnp.concatenate([prev8, cur], axis=0)
    if s == 0:
        return cur
    return pltpu.roll(ext, s, 0)[SUBLANES:, :]


def _shift_up(cur, next8, s):
    if s == 0:
        return cur
    n = cur.shape[0]
    ext = jnp.concatenate([cur, next8], axis=0)
    return pltpu.roll(ext, n + SUBLANES - s, 0)[:n, :]


def _lru_gates(xc, wa, ba, wx, bx, sp):
    xcb = xc.astype(BF16)
    r = _sigmoid(jnp.dot(xcb, wa, preferred_element_type=F32) + ba)
    ig = _sigmoid(jnp.dot(xcb, wx, preferred_element_type=F32) + bx)
    log_a = -LRU_C * r * sp
    a = jnp.exp(log_a)
    z = -jnp.tanh(log_a) * (a * a + 1.0)
    inv_mult = lax.rsqrt(jnp.maximum(z, TINY))
    return r, ig, a, z * inv_mult, inv_mult


def _lru_specs(tt, cg, n_groups, nt, reverse):
    ncol = cg // LANES
    if reverse:
        ti = lambda i: nt - 1 - i
    else:
        ti = lambda i: i
    hb = tt // SUBLANES
    cur = lambda col: pl.BlockSpec((tt, cg), lambda g, i: (ti(i), 2 * g + col))
    prev = lambda col: pl.BlockSpec((SUBLANES, cg), lambda g, i: (jnp.maximum(ti(i) * hb - 1, 0), 2 * g + col))
    chan = lambda rows: pl.BlockSpec((rows, cg), lambda g, i: (0, g))
    wblk = pl.BlockSpec((ncol, LRU_BLOCK_W, LRU_BLOCK_W), lambda g, i: (g, 0, 0))
    plain = pl.BlockSpec((tt, cg), lambda g, i: (ti(i), g))
    plain_prev = pl.BlockSpec((SUBLANES, cg), lambda g, i: (jnp.maximum(ti(i) * hb - 1, 0), g))
    return cur, prev, chan, wblk, plain, plain_prev


def _chip_gather_copies(ins, outs, send, recv, loc):
    n = len(ins)
    x, y, c = lax.axis_index("x"), lax.axis_index("y"), lax.axis_index("c")
    me = 2 * x + y
    copies = [pltpu.make_async_copy(ins[k], outs[k].at[me], loc.at[k]) for k in range(n)]
    for r, (rx, ry) in enumerate(((1, 0), (0, 1), (1, 1))):
        for k in range(n):
            copies.append(pltpu.make_async_remote_copy(
                src_ref=ins[k], dst_ref=outs[k].at[me], send_sem=send.at[r * n + k], recv_sem=recv.at[r * n + k],
                device_id=(_flip(x, rx), _flip(y, ry), c), device_id_type=MESH))
    return copies


def _lru_fwd(u, conv_w, conv_b, wa, ba, wx, bx, a_param, *, cg, name, ride=()):
    nride = len(ride)
    t, w2 = u.shape
    w = w2 // 2
    n_groups = w // cg
    ncol = cg // LANES
    tt = _tile(t, 256)
    nt = t // tt
    cur, prev, chan, wblk, plain, _ = _lru_specs(tt, cg, n_groups, nt, False)

    def body(*refs):
        n_in, n_out, n_scr = 10, 2, 3
        ins, rest = refs[:n_in], refs[n_in:]
        ride_in, rest = rest[:nride], rest[nride:]
        outs, rest = rest[:n_out], rest[n_out:]
        ride_out, rest = rest[:nride], rest[nride:]
        scr, sems = rest[:n_scr], rest[n_scr:]
        if not nride:
            core(*ins, *outs, *scr)
            return
        step = pl.program_id(0) * nt + pl.program_id(1)

        @pl.when(step == 0)
        def _():
            for cp in _chip_gather_copies(ride_in, ride_out, *sems):
                cp.start()

        core(*ins, *outs, *scr)

        @pl.when(step == n_groups * nt - 1)
        def _():
            for cp in _chip_gather_copies(ride_in, ride_out, *sems):
                cp.wait()

    def core(xb_ref, xp_ref, gate_ref, cw_ref, cb_ref, wa_ref, ba_ref, wx_ref, bx_ref, ap_ref,
             y_ref, hs_ref, h_ref, a_s, b_s):
        i = pl.program_id(1)

        @pl.when(i == 0)
        def _():
            h_ref[...] = jnp.zeros_like(h_ref)

        keep = (i > 0).astype(F32)
        for n in range(ncol):
            sl = slice(n * LANES, (n + 1) * LANES)
            xb = xb_ref[:, sl]
            xp = xp_ref[:, sl] * keep
            xc = cb_ref[:, sl] + cw_ref[3:4, sl] * xb
            for s in range(1, CONV_WIDTH):
                xc = xc + cw_ref[3 - s:4 - s, sl] * _shift_down(xp, xb, s)
            sp = _softplus(-ap_ref[:, sl])
            _, ig, a, mult, _ = _lru_gates(xc, wa_ref[n].astype(BF16), ba_ref[:, sl],
                                           wx_ref[n].astype(BF16), bx_ref[:, sl], sp)
            a_s[:, sl] = a
            b_s[:, sl] = mult * (ig * xc)

        def step(g, h):
            base = pl.multiple_of(g * SUBLANES, SUBLANES)
            for r in range(SUBLANES):
                h = a_s[pl.ds(base + r, 1), :] * h + b_s[pl.ds(base + r, 1), :]
                hs_ref[pl.ds(base + r, 1), :] = h
            return h

        h = lax.fori_loop(0, tt // SUBLANES, step, h_ref[0:1, :])
        h_ref[0:1, :] = h
        gate = gate_ref[...]
        y_ref[...] = (hs_ref[...] * (gate * _sigmoid(gate))).astype(y_ref.dtype)

    any_spec = pl.BlockSpec(memory_space=pl.ANY)
    ride_sems = [pltpu.SemaphoreType.DMA((3 * nride,)), pltpu.SemaphoreType.DMA((3 * nride,)),
                 pltpu.SemaphoreType.DMA((nride,))] if nride else []
    outs = pl.pallas_call(
        body, name=name, grid=(n_groups, nt),
        in_specs=[cur(0), prev(0), cur(1), chan(CONV_WIDTH), chan(1), wblk, chan(1), wblk, chan(1), chan(1)]
        + [any_spec] * nride,
        out_specs=[plain, plain] + [any_spec] * nride,
        out_shape=[jax.ShapeDtypeStruct((t, w), BF16), jax.ShapeDtypeStruct((t, w), F32)]
        + [jax.ShapeDtypeStruct((N_CHIPS,) + r.shape, r.dtype) for r in ride],
        scratch_shapes=[pltpu.VMEM((SUBLANES, cg), F32), pltpu.VMEM((tt, cg), F32), pltpu.VMEM((tt, cg), F32)]
        + ride_sems,
        compiler_params=_cparams(("arbitrary", "arbitrary")),
    )(u, u, u, conv_w, conv_b, wa, ba, wx, bx, a_param, *ride)
    return outs[0], outs[1], outs[2:]


def _lru_bwd(u, hs, dy, conv_w, conv_b, wa, ba, wx, bx, a_param, *, cg, name, ride=()):
    nride = len(ride)
    t, w2 = u.shape
    w = w2 // 2
    n_groups = w // cg
    ncol = cg // LANES
    tt = _tile(t, 256)
    nt = t // tt
    cur, prev, chan, wblk, plain, plain_prev = _lru_specs(tt, cg, n_groups, nt, True)
    tn_dims = (((0,), (0,)), ((), ()))
    nt_dims = (((1,), (1,)), ((), ()))

    def body(*refs):
        n_in, n_out, n_scr = 13, 9, 5
        ins, rest = refs[:n_in], refs[n_in:]
        ride_in, rest = rest[:nride], rest[nride:]
        outs, rest = rest[:n_out], rest[n_out:]
        ride_out, rest = rest[:nride], rest[nride:]
        scr, sems = rest[:n_scr], rest[n_scr:]
        if not nride:
            core(*ins, *outs, *scr)
            return
        step = pl.program_id(0) * nt + pl.program_id(1)

        @pl.when(step == 0)
        def _():
            for cp in _scatter_copies(ride_in, ride_out, *sems):
                cp.start()

        core(*ins, *outs, *scr)

        @pl.when(step == n_groups * nt - 1)
        def _():
            for cp in _scatter_copies(ride_in, ride_out, *sems):
                cp.wait()

    def core(xb_ref, xp_ref, gate_ref, hs_ref, hp_ref, dy_ref, cw_ref, cb_ref, wa_ref, ba_ref, wx_ref, bx_ref,
             ap_ref, dxb_ref, dgate_ref, dcw_ref, dcb_ref, dwa_ref, dba_ref, dwx_ref, dbx_ref, dsp_ref,
             c_ref, nx_ref, a_s, dhs_s, lam_s):
        i = pl.program_id(1)
        first_time_block = i == nt - 1

        @pl.when(i == 0)
        def _():
            c_ref[...] = jnp.zeros_like(c_ref)
            nx_ref[...] = jnp.zeros_like(nx_ref)
            for r in (dcw_ref, dcb_ref, dwa_ref, dba_ref, dwx_ref, dbx_ref, dsp_ref):
                r[...] = jnp.zeros_like(r)

        keep = jnp.where(first_time_block, 0.0, 1.0).astype(F32)
        gate = gate_ref[...]
        sg = _sigmoid(gate)
        dyv = dy_ref[...]
        hsv = hs_ref[...]
        dhs_s[...] = dyv * (gate * sg)
        dgate_ref[...] = (dyv * hsv * (sg * (1.0 + gate * (1.0 - sg)))).astype(dgate_ref.dtype)

        saved = []
        for n in range(ncol):
            sl = slice(n * LANES, (n + 1) * LANES)
            xb = xb_ref[:, sl]
            xp = xp_ref[:, sl] * keep
            shifted = [xb] + [_shift_down(xp, xb, s) for s in range(1, CONV_WIDTH)]
            xc = cb_ref[:, sl] + cw_ref[3:4, sl] * xb
            for s in range(1, CONV_WIDTH):
                xc = xc + cw_ref[3 - s:4 - s, sl] * shifted[s]
            sp = _softplus(-ap_ref[:, sl])
            wab = wa_ref[n].astype(BF16)
            wxb = wx_ref[n].astype(BF16)
            r, ig, a, mult, inv_mult = _lru_gates(xc, wab, ba_ref[:, sl], wxb, bx_ref[:, sl], sp)
            a_s[:, sl] = a
            saved.append((sl, shifted, xc, sp, wab, wxb, r, ig, a, mult, inv_mult))

        def step(g, c):
            base = pl.multiple_of(tt - SUBLANES - g * SUBLANES, SUBLANES)
            for r in range(SUBLANES - 1, -1, -1):
                lam = dhs_s[pl.ds(base + r, 1), :] + c
                lam_s[pl.ds(base + r, 1), :] = lam
                c = a_s[pl.ds(base + r, 1), :] * lam
            return c

        c_ref[0:1, :] = lax.fori_loop(0, tt // SUBLANES, step, c_ref[0:1, :])

        for n in range(ncol):
            sl, shifted, xc, sp, wab, wxb, r, ig, a, mult, inv_mult = saved[n]
            lam = lam_s[:, sl]
            hprev = _shift_down(hp_ref[:, sl] * keep, hs_ref[:, sl], 1)
            da = lam * hprev
            dmult = lam * (ig * xc)
            dlog_a = da * a - dmult * (a * a * inv_mult)
            di = lam * (mult * xc)
            dxc = lam * (mult * ig)
            dr = dlog_a * (-LRU_C * sp)
            dsp_ref[:, sl] += jnp.sum(dlog_a * (-LRU_C * r), axis=0, keepdims=True)
            dza = dr * (r * (1.0 - r))
            dzx = di * (ig * (1.0 - ig))
            dba_ref[:, sl] += jnp.sum(dza, axis=0, keepdims=True)
            dbx_ref[:, sl] += jnp.sum(dzx, axis=0, keepdims=True)
            xcb = xc.astype(BF16)
            dzab = dza.astype(BF16)
            dzxb = dzx.astype(BF16)
            dwa_ref[n] += lax.dot_general(xcb, dzab, tn_dims, preferred_element_type=F32)
            dwx_ref[n] += lax.dot_general(xcb, dzxb, tn_dims, preferred_element_type=F32)
            dxc = dxc + lax.dot_general(dzab, wab, nt_dims, preferred_element_type=F32)
            dxc = dxc + lax.dot_general(dzxb, wxb, nt_dims, preferred_element_type=F32)
            dcb_ref[:, sl] += jnp.sum(dxc, axis=0, keepdims=True)
            for s in range(CONV_WIDTH):
                dcw_ref[3 - s:4 - s, sl] += jnp.sum(dxc * shifted[s], axis=0, keepdims=True)
            nx = nx_ref[:, sl]
            dxb = cw_ref[3:4, sl] * dxc
            for s in range(1, CONV_WIDTH):
                dxb = dxb + cw_ref[3 - s:4 - s, sl] * _shift_up(dxc, nx, s)
            dxb_ref[:, sl] = dxb.astype(dxb_ref.dtype)
            nx_ref[:, sl] = dxc[0:SUBLANES, :]

        @pl.when(first_time_block)
        def _():
            dsp_ref[...] = dsp_ref[...] * (-_sigmoid(-ap_ref[...]))

    dxb_spec = pl.BlockSpec((tt, cg), lambda g, i: (nt - 1 - i, g))
    any_spec = pl.BlockSpec(memory_space=pl.ANY)
    ride_shape, ride_sems = _scatter_shapes(ride) if nride else ([], [])
    outs = pl.pallas_call(
        body, name=name, grid=(n_groups, nt),
        in_specs=[cur(0), prev(0), cur(1), plain, plain_prev, plain, chan(CONV_WIDTH), chan(1), wblk, chan(1), wblk,
                  chan(1), chan(1)] + [any_spec] * nride,
        out_specs=[dxb_spec, dxb_spec, chan(CONV_WIDTH), chan(1), wblk, chan(1), wblk, chan(1), chan(1)]
        + [any_spec] * nride,
        out_shape=[jax.ShapeDtypeStruct((t, w), BF16), jax.ShapeDtypeStruct((t, w), BF16),
                   jax.ShapeDtypeStruct(conv_w.shape, F32), jax.ShapeDtypeStruct(conv_b.shape, F32),
                   jax.ShapeDtypeStruct(wa.shape, F32), jax.ShapeDtypeStruct(ba.shape, F32),
                   jax.ShapeDtypeStruct(wx.shape, F32), jax.ShapeDtypeStruct(bx.shape, F32),
                   jax.ShapeDtypeStruct(a_param.shape, F32)] + ride_shape,
        scratch_shapes=[pltpu.VMEM((SUBLANES, cg), F32), pltpu.VMEM((SUBLANES, cg), F32),
                        pltpu.VMEM((tt, cg), F32), pltpu.VMEM((tt, cg), F32), pltpu.VMEM((tt, cg), F32)] + ride_sems,
        compiler_params=_cparams(("arbitrary", "arbitrary")),
    )(u, u, u, hs, hs, dy, conv_w, conv_b, wa, ba, wx, bx, a_param, *ride)
    return outs[:9], outs[9:]


def _fgate_fwd(f, b_f, *, name):
    t, n = f.shape
    tt = _tile(t, 256)
    width = FOX_HEADS * FOX_HEAD_DIM

    def body(f_ref, b_ref, cum_ref, wide_ref, carry_ref):
        i = pl.program_id(0)

        @pl.when(i == 0)
        def _():
            carry_ref[...] = jnp.zeros_like(carry_ref)

        z = f_ref[...] + b_ref[...]
        lf = jnp.minimum(z, 0.0) - _log1p(jnp.exp(-jnp.abs(z)))
        row = lax.broadcasted_iota(jnp.int32, (tt, tt), 0)
        col = lax.broadcasted_iota(jnp.int32, (tt, tt), 1)
        tri = (col <= row).astype(BF16)
        cum = _dot_01_left(tri, lf) + carry_ref[0:1, :]
        cum_ref[...] = cum
        carry_ref[0:1, :] = cum[tt - 1:tt, :]
        head = lax.broadcasted_iota(jnp.int32, (n, width), 0)
        chan = lax.broadcasted_iota(jnp.int32, (n, width), 1) // FOX_HEAD_DIM
        wide_ref[...] = _dot_01_right(cum, (head == chan).astype(BF16))

    return pl.pallas_call(
        body, name=name, grid=(t // tt,),
        in_specs=[pl.BlockSpec((tt, n), lambda i: (i, 0)), pl.BlockSpec((1, n), lambda i: (0, 0))],
        out_specs=[pl.BlockSpec((tt, n), lambda i: (i, 0)), pl.BlockSpec((tt, width), lambda i: (i, 0))],
        out_shape=[jax.ShapeDtypeStruct((t, n), F32), jax.ShapeDtypeStruct((t, width), F32)],
        scratch_shapes=[pltpu.VMEM((SUBLANES, n), F32)],
        compiler_params=_cparams(("arbitrary",)),
    )(f, b_f)


def _fgate_bwd(dcum, f, b_f, *, name):
    t, n = f.shape
    tt = _tile(t, 256)
    nt = t // tt

    def body(dc_ref, f_ref, b_ref, df_ref, db_ref, carry_ref):
        i = pl.program_id(0)

        @pl.when(i == 0)
        def _():
            carry_ref[...] = jnp.zeros_like(carry_ref)
            db_ref[...] = jnp.zeros_like(db_ref)

        row = lax.broadcasted_iota(jnp.int32, (tt, tt), 0)
        col = lax.broadcasted_iota(jnp.int32, (tt, tt), 1)
        triu = (col >= row).astype(BF16)
        dlf = _dot_01_left(triu, dc_ref[...]) + carry_ref[0:1, :]
        carry_ref[0:1, :] = dlf[0:1, :]
        z = f_ref[...] + b_ref[...]
        df = dlf * _sigmoid(-z)
        df_ref[...] = df
        db_ref[...] += jnp.sum(df, axis=0, keepdims=True)

    blk = pl.BlockSpec((tt, n), lambda i: (nt - 1 - i, 0))
    vec = pl.BlockSpec((1, n), lambda i: (0, 0))
    return pl.pallas_call(
        body, name=name, grid=(nt,),
        in_specs=[blk, blk, vec], out_specs=[blk, vec],
        out_shape=[jax.ShapeDtypeStruct((t, n), F32), jax.ShapeDtypeStruct((1, n), F32)],
        scratch_shapes=[pltpu.VMEM((SUBLANES, n), F32)],
        compiler_params=_cparams(("arbitrary",)),
    )(dcum, f, b_f)


def _attn_fwd(start, qkv, ckt, gate, *, name, tq, tk):
    t = qkv.shape[0]
    f = gate.shape[1]
    npair = f // LANES
    nq = t // tq
    ratio = tq // tk
    assert tq == ratio * tk and t == nq * tq
    scale = 1.0 / math.sqrt(FOX_HEAD_DIM)
    nt_dims = (((1,), (1,)), ((), ()))

    def body(start_ref, q_ref, k_ref, v_ref, ck_ref, g_ref, o_ref, y_ref, l_ref):
        i = pl.program_id(1)
        pair = pl.program_id(0)
        firsts = (start_ref[2 * pair, i], start_ref[2 * pair + 1, i])
        both = jnp.maximum(firsts[0], firsts[1])
        lane = lax.broadcasted_iota(jnp.int32, (tq, LANES), 1)
        lo = lane < FOX_HEAD_DIM
        q2 = q_ref[...] * scale
        qs = (jnp.where(lo, q2, 0).astype(BF16), jnp.where(lo, 0, q2).astype(BF16))
        row = lax.broadcasted_iota(jnp.int32, (tq, tk), 0)
        col = lax.broadcasted_iota(jnp.int32, (tq, tk), 1)

        def kv_step(j, carry, diag, heads=(0, 1)):
            off = pl.multiple_of(j * tk, tk)
            kj = k_ref[pl.ds(off, tk), :]
            vj = v_ref[pl.ds(off, tk), :]
            ck = ck_ref[:, pl.ds(off, tk)]
            new = list(carry)
            for h in heads:
                m, l, acc = carry[h]
                s = lax.dot_general(qs[h], kj, nt_dims, preferred_element_type=F32) - ck[h:h + 1, :]
                if diag is not None:
                    s = jnp.where(col + diag * tk <= row, s, NEG_INF)
                m_new = jnp.maximum(m, jnp.max(s, axis=-1, keepdims=True))
                alpha = jnp.exp(m - m_new)
                p = jnp.exp(s - m_new)
                l = alpha * l + jnp.sum(p, axis=-1, keepdims=True)
                acc = alpha * acc + jnp.dot(p.astype(BF16), vj, preferred_element_type=F32)
                new[h] = (m_new, l, acc)
            return tuple(new)

        carry = tuple((jnp.full((tq, 1), NEG_INF, F32), jnp.zeros((tq, 1), F32), jnp.zeros((tq, LANES), F32))
                      for _ in range(2))
        for h in range(2):
            carry = lax.fori_loop(firsts[h], both, lambda j, c, h=h: kv_step(j, c, None, (h,)), carry)
        carry = lax.fori_loop(both, i * ratio, lambda j, c: kv_step(j, c, None), carry)
        for d in range(ratio):
            carry = kv_step(i * ratio + d, carry, d)
        (m0, l0, a0), (m1, l1, a1) = carry
        o = jnp.where(lo, a0 / l0, a1 / l1)
        o_ref[...] = o
        gate_v = g_ref[...]
        y_ref[...] = (o * (gate_v * _sigmoid(gate_v))).astype(y_ref.dtype)
        lse_t = jnp.transpose(jnp.where(lo, m0 + jnp.log(l0), m1 + jnp.log(l1)))
        l_ref[0:1, :] = lse_t[0:1, :]
        l_ref[1:2, :] = lse_t[FOX_HEAD_DIM:FOX_HEAD_DIM + 1, :]

    blk = lambda base: pl.BlockSpec((tq, LANES), lambda p, i, s: (i, base + p))
    full = lambda base: pl.BlockSpec((t, LANES), lambda p, i, s: (0, base + p))
    return pl.pallas_call(
        body, name=name,
        grid_spec=pltpu.PrefetchScalarGridSpec(
            num_scalar_prefetch=1, grid=(npair, nq),
            in_specs=[blk(0), full(npair), full(2 * npair), pl.BlockSpec((None, 2, t), lambda p, i, s: (p, 0, 0)),
                      blk(0)],
            out_specs=[blk(0), blk(0), pl.BlockSpec((None, 2, tq), lambda p, i, s: (p, 0, i))]),
        out_shape=[jax.ShapeDtypeStruct((t, f), F32), jax.ShapeDtypeStruct((t, f), BF16),
                   jax.ShapeDtypeStruct((npair, 2, t), F32)],
        compiler_params=_cparams(("parallel", "arbitrary")),
    )(start, qkv, qkv, qkv, ckt, gate)


def _attn_bwd(end, qkv, do, lt, dt, cke, *, name):
    t, f = do.shape
    npair = f // LANES
    tk = _tile(t, ATTN_TILE)
    nk = t // tk
    scale = 1.0 / math.sqrt(FOX_HEAD_DIM)
    nt_dims = (((1,), (1,)), ((), ()))
    tn_dims = (((0,), (0,)), ((), ()))

    def body(end_ref, k_ref, v_ref, q_ref, do_ref, l_ref, d_ref, ck_ref, dq_out_ref, dk_ref, dv_ref, dck_ref, dcq_ref,
             dq_ref, dk_s, dv_s, dck_s):
        j = pl.program_id(1)
        pair = pl.program_id(0)
        lasts = (end_ref[2 * pair, j], end_ref[2 * pair + 1, j])
        both = jnp.minimum(lasts[0], lasts[1])

        @pl.when(j == 0)
        def _():
            dq_ref[...] = jnp.zeros_like(dq_ref)
            dcq_ref[...] = jnp.zeros_like(dcq_ref)

        lane = lax.broadcasted_iota(jnp.int32, (tk, LANES), 1)
        lo = lane < FOX_HEAD_DIM
        sel = (lo, jnp.logical_not(lo))
        kj = k_ref[...]
        vj = v_ref[...]
        km = tuple(jnp.where(sel[h], kj, 0).astype(BF16) for h in range(2))
        ckv = ck_ref[...]
        ckh = (ckv[:, 0:1], ckv[:, FOX_HEAD_DIM:FOX_HEAD_DIM + 1])
        row = lax.broadcasted_iota(jnp.int32, (tk, tk), 0)
        col = lax.broadcasted_iota(jnp.int32, (tk, tk), 1)
        causal = row <= col

        def q_step(i, carry, masked, heads=(0, 1)):
            off = pl.multiple_of(i * tk, tk)
            qi = q_ref[pl.ds(off, tk), :]
            doi = do_ref[pl.ds(off, tk), :]
            lrow = l_ref[:, pl.ds(off, tk)]
            drow = d_ref[:, pl.ds(off, tk)]
            dq_add = jnp.zeros((tk, LANES), F32)
            for h in heads:
                qm = jnp.where(sel[h], qi, 0).astype(BF16)
                dom = jnp.where(sel[h], doi, 0).astype(BF16)
                st = lax.dot_general(kj, qm, nt_dims, preferred_element_type=F32) * scale
                st = st - ckh[h] - lrow[h:h + 1, :]
                if masked:
                    st = jnp.where(causal, st, NEG_INF)
                pt = jnp.exp(st)
                dpt = lax.dot_general(vj, dom, nt_dims, preferred_element_type=F32)
                dst = pt * (dpt - drow[h:h + 1, :])
                ptb = pt.astype(BF16)
                dstb = dst.astype(BF16)
                dv_s[...] += jnp.dot(ptb, dom, preferred_element_type=F32)
                dk_s[...] += jnp.dot(dstb, qm, preferred_element_type=F32)
                dq_add = dq_add + lax.dot_general(dstb, km[h], tn_dims, preferred_element_type=F32)
                dck_s[:, h:h + 1] -= jnp.sum(dst, axis=-1, keepdims=True)
                dcq_ref[h:h + 1, pl.ds(off, tk)] += jnp.sum(dst, axis=0, keepdims=True)
            dq_ref[pl.ds(off, tk), :] += dq_add * scale
            return carry

        dk_s[...] = jnp.zeros_like(dk_s)
        dv_s[...] = jnp.zeros_like(dv_s)
        dck_s[...] = jnp.zeros_like(dck_s)
        carry = 0
        carry = q_step(j, carry, True)
        carry = lax.fori_loop(j + 1, both, lambda i, c: q_step(i, c, False), carry)
        for h in range(2):
            carry = lax.fori_loop(both, lasts[h], lambda i, c, h=h: q_step(i, c, False, (h,)), carry)
        dk_acc, dv_acc = dk_s[...], dv_s[...]
        dck = (dck_s[:, 0:1], dck_s[:, 1:2])
        dk_ref[...] = (dk_acc * scale).astype(dk_ref.dtype)
        dv_ref[...] = dv_acc.astype(dv_ref.dtype)
        dck_t = jnp.transpose(jnp.where(lo, dck[0], dck[1]))
        dck_ref[0:1, :] = dck_t[0:1, :]
        dck_ref[1:2, :] = dck_t[FOX_HEAD_DIM:FOX_HEAD_DIM + 1, :]

        @pl.when(j == nk - 1)
        def _():
            dq_out_ref[...] = dq_ref[...].astype(dq_out_ref.dtype)

    blk = lambda base: pl.BlockSpec((tk, LANES), lambda p, j, e: (j, base + p))
    full = lambda base: pl.BlockSpec((t, LANES), lambda p, j, e: (0, base + p))
    rows = pl.BlockSpec((None, 2, t), lambda p, j, e: (p, 0, 0))
    return pl.pallas_call(
        body, name=name,
        grid_spec=pltpu.PrefetchScalarGridSpec(
            num_scalar_prefetch=1, grid=(npair, nk),
            in_specs=[blk(npair), blk(2 * npair), full(0), full(0), rows, rows, blk(0)],
            out_specs=[full(0), blk(0), blk(0), pl.BlockSpec((None, 2, tk), lambda p, j, e: (p, 0, j)), rows],
            scratch_shapes=[pltpu.VMEM((t, LANES), F32), pltpu.VMEM((tk, LANES), F32), pltpu.VMEM((tk, LANES), F32),
                            pltpu.VMEM((tk, LANES), F32)]),
        out_shape=[jax.ShapeDtypeStruct((t, f), BF16), jax.ShapeDtypeStruct((t, f), BF16),
                   jax.ShapeDtypeStruct((t, f), BF16), jax.ShapeDtypeStruct((npair, 2, t), F32),
                   jax.ShapeDtypeStruct((npair, 2, t), F32)],
        compiler_params=_cparams(("parallel", "arbitrary")),
    )(end, qkv, qkv, qkv, do, lt, dt, cke)


ATTN_TILE = 512
ATTN_FWD_QUERIES = 512
EXP_ZERO = -104.0
BOUND_SLACK = 1.02


def _attn_row_stats(qkv, *, name):
    t = qkv.shape[0]
    f = qkv.shape[1] // 3
    tt = _tile(t, 512)

    def body(q_ref, k_ref, s_ref):
        q = q_ref[...].astype(F32)
        k = k_ref[...].astype(F32)
        chan = lax.broadcasted_iota(jnp.int32, (f, LANES), 0) // FOX_HEAD_DIM
        lane = lax.broadcasted_iota(jnp.int32, (f, LANES), 1)
        acc = jnp.zeros((tt, LANES), F32)
        for off, val in ((0, q * q), (FOX_HEADS, q * k), (2 * FOX_HEADS, k * k)):
            pick = (chan == lane - off).astype(BF16)
            acc = acc + jnp.dot(val.astype(BF16), pick, preferred_element_type=F32)
        s_ref[...] = acc

    return pl.pallas_call(
        body, name=name, grid=(t // tt,),
        in_specs=[pl.BlockSpec((tt, f), lambda i: (i, 0)), pl.BlockSpec((tt, f), lambda i: (i, 1))],
        out_specs=pl.BlockSpec((tt, LANES), lambda i: (i, 0)),
        out_shape=jax.ShapeDtypeStruct((t, LANES), F32),
        compiler_params=_cparams(("parallel",)),
    )(qkv, qkv)


def _attn_skip_tables(stats, cum16, tile):
    t = stats.shape[0]
    nb = t // tile
    scale = 1.0 / math.sqrt(FOX_HEAD_DIM)
    qn = jnp.sqrt(stats[:, :FOX_HEADS]) * scale
    sii = stats[:, FOX_HEADS:2 * FOX_HEADS] * scale - cum16
    kmax = jnp.max(jnp.sqrt(stats[:, 2 * FOX_HEADS:3 * FOX_HEADS]), axis=0, keepdims=True)
    arow = qn * kmax * BOUND_SLACK - sii + 0.5 * BOUND_SLACK
    a_blk = jnp.max(arow.reshape(nb, tile, FOX_HEADS), axis=1)
    c_blk = -cum16.reshape(nb, tile, FOX_HEADS)[:, tile - 1, :]
    dead = (a_blk[:, None, :] + c_blk[None, :, :]) < EXP_ZERO
    start_h = jnp.sum(dead.astype(jnp.int32), axis=1)
    blk = jnp.arange(nb, dtype=jnp.int32)
    start = jnp.minimum(start_h, blk[:, None]).T
    needs = start[:, :, None] <= blk[None, None, :]
    end = jnp.max(jnp.where(needs, blk[None, :, None] + 1, 0), axis=1)
    return start, jnp.maximum(end, blk[None, :] + 1)


def _fox_post_bwd(dy, o, gate, *, name):
    t, f = dy.shape
    tt = _tile(t, 512)

    def body(dy_ref, o_ref, g_ref, do_ref, dg_ref, dl_ref):
        g = g_ref[...]
        sg = _sigmoid(g)
        dyv = dy_ref[...]
        ov = o_ref[...]
        do = dyv * (g * sg)
        do_ref[...] = do.astype(do_ref.dtype)
        dg_ref[...] = (dyv * ov * (sg * (1.0 + g * (1.0 - sg)))).astype(dg_ref.dtype)
        chan = lax.broadcasted_iota(jnp.int32, (f, LANES), 0)
        head = lax.broadcasted_iota(jnp.int32, (f, LANES), 1)
        pick = (chan // FOX_HEAD_DIM == head).astype(BF16)
        dl_ref[...] = _dot_01_right(do * ov, pick)

    blk = pl.BlockSpec((tt, f), lambda i: (i, 0))
    return pl.pallas_call(
        body, name=name, grid=(t // tt,),
        in_specs=[blk, blk, blk], out_specs=[blk, blk, pl.BlockSpec((tt, LANES), lambda i: (i, 0))],
        out_shape=[jax.ShapeDtypeStruct((t, f), BF16), jax.ShapeDtypeStruct((t, f), BF16),
                   jax.ShapeDtypeStruct((t, LANES), F32)],
        compiler_params=_cparams(("parallel",)),
    )(dy, o, gate)


def _adamw(w, g, m, v, *, name):
    _, r, c = w.shape
    tr = _tile(r, 256) if r % SUBLANES == 0 else r
    c1 = 1.0 - ADAM_B1 ** ADAM_STEP
    c2 = 1.0 - ADAM_B2 ** ADAM_STEP

    def body(w_ref, g_ref, m_ref, v_ref, go_ref, d_ref, mo_ref, vo_ref):
        gv = g_ref[...]
        go_ref[...] = gv
        mn = ADAM_B1 * m_ref[...] + (1.0 - ADAM_B1) * gv
        vn = ADAM_B2 * v_ref[...] + (1.0 - ADAM_B2) * (gv * gv)
        mo_ref[...] = mn
        vo_ref[...] = vn
        d_ref[...] = -ADAM_LR * ((mn / c1) / (jnp.sqrt(vn / c2) + ADAM_EPS) + ADAM_WD * w_ref[...])

    blk = pl.BlockSpec((None, tr, c), lambda i: (0, i, 0))
    return pl.pallas_call(
        body, name=name, grid=(r // tr,), in_specs=[blk] * 4, out_specs=[blk] * 4,
        out_shape=[jax.ShapeDtypeStruct((1, r, c), F32)] * 4,
        compiler_params=_cparams(("parallel",)),
    )(w, g, m, v)


def _sum_slots(land, *, name):
    ns, r, c = land.shape
    tr = _tile(r, 64) if r % SUBLANES == 0 else r

    def body(l_ref, o_ref):
        acc = l_ref[0].astype(F32)
        for s in range(1, ns):
            acc = acc + l_ref[s].astype(F32)
        o_ref[...] = acc

    return pl.pallas_call(
        body, name=name, grid=(r // tr,),
        in_specs=[pl.BlockSpec((ns, tr, c), lambda i: (0, i, 0))],
        out_specs=pl.BlockSpec((tr, c), lambda i: (i, 0)),
        out_shape=jax.ShapeDtypeStruct((r, c), F32),
        compiler_params=_cparams(("parallel",)),
    )(land)


ANY = pl.BlockSpec(memory_space=pl.ANY)


def _flip(v, bit):
    return 1 - v if bit else v


def _gather_chips(shards, small, *, name):
    n = len(shards)
    rels = ((1, 0), (0, 1), (1, 1))

    def body(*refs):
        ins, small_in = refs[:n], refs[n]
        outs, small_out = refs[n + 1:2 * n + 1], refs[2 * n + 1]
        send, recv, loc = refs[2 * n + 2:]
        x, y, c = lax.axis_index("x"), lax.axis_index("y"), lax.axis_index("c")
        me = 2 * x + y
        sibling = (x, y, 1 - c)
        local = [pltpu.make_async_copy(ins[k], outs[k].at[me], loc.at[k]) for k in range(n)]
        local.append(pltpu.make_async_copy(small_in, small_out.at[me], loc.at[n]))
        for cp in local:
            cp.start()

        def rows(k):
            half = ins[k].shape[0] // 2
            return pl.ds(pl.multiple_of(c * half, SUBLANES), half)

        sends = []
        for r, (rx, ry) in enumerate(rels):
            to = (_flip(x, rx), _flip(y, ry), c)
            for k in range(n):
                cp = pltpu.make_async_remote_copy(
                    src_ref=ins[k].at[rows(k), :], dst_ref=outs[k].at[me, rows(k), :],
                    send_sem=send.at[r * n + k], recv_sem=recv.at[r * n + k], device_id=to, device_id_type=MESH)
                cp.start()
                sends.append(cp)
            cp = pltpu.make_async_remote_copy(
                src_ref=small_in, dst_ref=small_out.at[me], send_sem=send.at[6 * n + r], recv_sem=recv.at[6 * n + r],
                device_id=to, device_id_type=MESH)
            cp.start()
            sends.append(cp)
        for r, (rx, ry) in enumerate(rels):
            src_chip = 2 * _flip(x, rx) + _flip(y, ry)
            for k in range(n):
                landed = outs[k].at[src_chip, rows(k), :]
                sends[r * (n + 1) + k].wait_recv()
                cp = pltpu.make_async_remote_copy(
                    src_ref=landed, dst_ref=landed, send_sem=send.at[3 * n + r * n + k],
                    recv_sem=recv.at[3 * n + r * n + k], device_id=sibling, device_id_type=MESH)
                cp.start()
                sends.append(cp)
            sends[r * (n + 1) + n].wait_recv()
        for cp in sends[:3 * (n + 1)]:
            cp.wait_send()
        for cp in sends[3 * (n + 1):]:
            cp.wait()
        for cp in local:
            cp.wait()

    vmem = pl.BlockSpec(memory_space=pltpu.VMEM)
    return pl.pallas_call(
        body, name=name, in_specs=[vmem] * (n + 1), out_specs=[vmem] * (n + 1),
        out_shape=[jax.ShapeDtypeStruct((N_CHIPS,) + s.shape, s.dtype) for s in list(shards) + [small]],
        scratch_shapes=[pltpu.SemaphoreType.DMA((6 * n + 3,)), pltpu.SemaphoreType.DMA((6 * n + 3,)),
                        pltpu.SemaphoreType.DMA((n + 1,))],
        compiler_params=pltpu.CompilerParams(has_side_effects=True, vmem_limit_bytes=VMEM_LIMIT),
    )(*shards, small)


_RELS7 = tuple((r >> 2 & 1, r >> 1 & 1, r & 1) for r in range(1, N_DEV))


def _scatter_copies(ins, outs, send, recv, loc):
    n = len(ins)
    x, y, c = lax.axis_index("x"), lax.axis_index("y"), lax.axis_index("c")
    me = 4 * x + 2 * y + c

    def piece(k, px, py, pc):
        half = ins[k].shape[1] // 2
        return ins[k].at[2 * px + py, pl.ds(pc * half, half), :]

    copies = [pltpu.make_async_copy(piece(k, x, y, c), outs[k].at[me], loc.at[k]) for k in range(n)]
    for r, (rx, ry, rc) in enumerate(_RELS7):
        tx, ty, tc = _flip(x, rx), _flip(y, ry), _flip(c, rc)
        for k in range(n):
            copies.append(pltpu.make_async_remote_copy(
                src_ref=piece(k, tx, ty, tc), dst_ref=outs[k].at[me], send_sem=send.at[r * n + k],
                recv_sem=recv.at[r * n + k], device_id=(tx, ty, tc), device_id_type=MESH))
    return copies


def _scatter_shapes(grads):
    n = len(grads)
    out_shape = [jax.ShapeDtypeStruct((N_DEV, g.shape[1] // 2, g.shape[2]), g.dtype) for g in grads]
    sems = [pltpu.SemaphoreType.DMA((7 * n,)), pltpu.SemaphoreType.DMA((7 * n,)), pltpu.SemaphoreType.DMA((n,))]
    return out_shape, sems


def _join_cores(halves, *, name):
    n = len(halves)

    def body(*refs):
        ins, outs = refs[:n], refs[n:2 * n]
        send, recv, loc = refs[2 * n:]
        x, y, c = lax.axis_index("x"), lax.axis_index("y"), lax.axis_index("c")
        copies = []
        for k in range(n):
            half = ins[k].shape[0]
            mine = outs[k].at[0, pl.ds(c * half, half), :]
            cp = pltpu.make_async_copy(ins[k], mine, loc.at[k])
            cp.start()
            copies.append(cp)
            cp = pltpu.make_async_remote_copy(
                src_ref=ins[k], dst_ref=mine, send_sem=send.at[k], recv_sem=recv.at[k],
                device_id=(x, y, 1 - c), device_id_type=MESH)
            cp.start()
            copies.append(cp)
        for cp in copies:
            cp.wait()

    in_vmem = pl.BlockSpec(memory_space=pltpu.VMEM)
    return pl.pallas_call(
        body, name=name, in_specs=[in_vmem] * n, out_specs=[in_vmem] * n,
        out_shape=[jax.ShapeDtypeStruct((1, 2 * h.shape[0], h.shape[1]), h.dtype) for h in halves],
        scratch_shapes=[pltpu.SemaphoreType.DMA((n,)), pltpu.SemaphoreType.DMA((n,)), pltpu.SemaphoreType.DMA((n,))],
        compiler_params=pltpu.CompilerParams(has_side_effects=True, vmem_limit_bytes=VMEM_LIMIT),
    )(*halves)


def _allreduce_small(buf, *, name):
    r, n = buf.shape
    half = r // 2
    rels = ((1, 0), (0, 1), (1, 1))

    def body(in_ref, out_ref, sib_ref, chips_ref, send, recv):
        x, y, c = lax.axis_index("x"), lax.axis_index("y"), lax.axis_index("c")
        sibling = (x, y, 1 - c)
        chip = 2 * x + y
        rows = pl.ds(pl.multiple_of(c * half, SUBLANES), half)

        swap = pltpu.make_async_remote_copy(src_ref=in_ref, dst_ref=sib_ref, send_sem=send.at[0], recv_sem=recv.at[0],
                                            device_id=sibling, device_id_type=MESH)
        swap.start()
        swap.wait()
        chips_ref[chip] = in_ref[rows, :] + sib_ref[rows, :]

        sends = []
        for k, (rx, ry) in enumerate(rels):
            cp = pltpu.make_async_remote_copy(
                src_ref=chips_ref.at[chip], dst_ref=chips_ref.at[chip], send_sem=send.at[1 + k],
                recv_sem=recv.at[1 + k], device_id=(_flip(x, rx), _flip(y, ry), c), device_id_type=MESH)
            cp.start()
            sends.append(cp)
        for cp in sends:
            cp.wait()
        total = chips_ref[0]
        for s in range(1, N_CHIPS):
            total = total + chips_ref[s]
        out_ref[rows, :] = total

        back = pltpu.make_async_remote_copy(src_ref=out_ref.at[rows, :], dst_ref=out_ref.at[rows, :],
                                            send_sem=send.at[4], recv_sem=recv.at[4],
                                            device_id=sibling, device_id_type=MESH)
        back.start()
        back.wait()

    vmem = pl.BlockSpec(memory_space=pltpu.VMEM)
    return pl.pallas_call(
        body, name=name, in_specs=[vmem], out_specs=vmem,
        out_shape=jax.ShapeDtypeStruct((r, n), F32),
        scratch_shapes=[pltpu.VMEM((r, n), F32), pltpu.VMEM((N_CHIPS, half, n), F32),
                        pltpu.SemaphoreType.DMA((5,)), pltpu.SemaphoreType.DMA((5,))],
        compiler_params=pltpu.CompilerParams(has_side_effects=True, vmem_limit_bytes=VMEM_LIMIT),
    )(buf)


def _pack(arrs):
    flat = []
    for a in arrs:
        v = a.reshape(-1)
        pad = (-v.shape[0]) % LANES
        if pad:
            v = jnp.pad(v, (0, pad))
        flat.append(v)
    v = jnp.concatenate(flat)
    pad = (-v.shape[0]) % (LANES * SUBLANES)
    if pad:
        v = jnp.pad(v, (0, pad))
    return v.reshape(-1, LANES)


def _unpack(buf, shapes):
    v = buf.reshape(-1)
    out, off = [], 0
    for s in shapes:
        n = math.prod(s)
        out.append(v[off:off + n].reshape(s))
        off += n + (-n) % LANES
    return out


def kernel(x, norm_g, final_g, lru_w_in, lru_conv_w, lru_conv_b, lru_wa, lru_ba, lru_wx, lru_bx, lru_a_param, lru_w_out, fox_w_in, fox_b_f, fox_w_out, loss_target, m_norm_g, m_final_g, m_lru_w_in, m_lru_conv_w, m_lru_conv_b, m_lru_wa, m_lru_ba, m_lru_wx, m_lru_bx, m_lru_a_param, m_lru_w_out, m_fox_w_in, m_fox_b_f, m_fox_w_out, v_norm_g, v_final_g, v_lru_w_in, v_lru_conv_w, v_lru_conv_b, v_lru_wa, v_lru_ba, v_lru_wx, v_lru_bx, v_lru_a_param, v_lru_w_out, v_fox_w_in, v_fox_b_f, v_fox_w_out):
    t, d = x.shape[1], x.shape[2]
    w = lru_wa.shape[1] * LRU_BLOCK_W
    f = FOX_HEADS * FOX_HEAD_DIM
    npair = f // LANES
    x0 = x.reshape(t, d)
    tgt = loss_target.reshape(t, d)
    chip = 2 * lax.axis_index("x") + lax.axis_index("y")

    g_lwi, g_lwo, g_cw = _gather_chips(
        [lru_w_in[0].astype(BF16), lru_w_out[0].astype(BF16)], lru_conv_w[0], name="gather_weights")
    cg = w // 2
    lwi = jnp.concatenate([g_lwi[0], g_lwi[2], g_lwi[1], g_lwi[3]], axis=1)
    lwo = g_lwo.reshape(w, d)
    conv_w = jnp.concatenate([g_cw[s] for s in range(N_CHIPS)], axis=1)
    conv_b, ba, bx, a_param = lru_conv_b, lru_ba, lru_bx, lru_a_param
    wa, wx = lru_wa[0], lru_wx[0]
    b_f = jnp.pad(fox_b_f, ((0, 0), (0, LANES - FOX_HEADS)))

    h0 = _rmsnorm(x0, norm_g[0], name="norm0")
    u = _matmul(h0, lwi, name="lru_in")
    y1, hs, (g_fwi, g_fwo) = _lru_fwd(u, conv_w, conv_b, wa, ba, wx, bx, a_param, cg=cg, name="lru_fwd",
                                      ride=[fox_w_in[0].astype(BF16), fox_w_out[0].astype(BF16)])
    fwi = jnp.concatenate([g_fwi[s] for s in range(N_CHIPS)], axis=1)
    w_qkv, w_g2 = fwi[:, :3 * f], fwi[:, 3 * f:4 * f]
    w_f = jnp.pad(fwi[:, 4 * f:], ((0, 0), (0, LANES - FOX_HEADS)))
    fwo = g_fwo.reshape(f, d)
    x1 = _matmul(y1, lwo, add=x0, name="lru_out")
    h1 = _rmsnorm(x1, norm_g[1], name="norm1")
    qkv = _matmul(h1, w_qkv, out_dtype=BF16, name="fox_qkv")
    gate2 = _matmul(h1, w_g2, name="fox_gate")
    flog = _matmul(h1, w_f, name="fox_f")
    cum, cke = _fgate_fwd(flog, b_f, name="fgate_fwd")
    cum16 = cum[:, :FOX_HEADS]
    ckt = cum16.T.reshape(npair, 2, t)
    a_tk, a_tq = _tile(t, ATTN_TILE), _tile(t, ATTN_FWD_QUERIES)
    a_start, a_end = _attn_skip_tables(_attn_row_stats(qkv, name="attn_row_stats"), cum16, a_tk)
    a_start_fwd = jnp.min(a_start.reshape(FOX_HEADS, t // a_tq, a_tq // a_tk), axis=2)
    o, y2, lse = _attn_fwd(a_start_fwd, qkv, ckt, gate2, name="attn_fwd", tq=a_tq, tk=a_tk)
    x2 = _matmul(y2, fwo, add=x1, name="fox_out")
    lsum, dx2, dgf = _final_loss(x2, tgt, final_g, name="final_loss")
    loss = lax.psum(0.5 * jnp.sum(lsum) / d, ("x", "y", "c"))

    d_fwo = _matmul(y2, dx2, ta=True, out_dtype=BF16, name="d_fox_w_out")
    dy2 = _matmul(dx2, fwo, tb=True, name="d_y2")
    do, dgate2, dl = _fox_post_bwd(dy2, o, gate2, name="fox_post_bwd")
    lt = lse
    dt = dl[:, :FOX_HEADS].T.reshape(npair, 2, t)
    dq, dk, dv, dck, dcq = _attn_bwd(a_end, qkv, do, lt, dt, cke, name="attn_bwd")
    dcum = jnp.pad((dck + dcq).reshape(FOX_HEADS, t).T, ((0, 0), (0, LANES - FOX_HEADS)))
    dflog, db_f = _fgate_bwd(dcum, flog, b_f, name="fgate_bwd")
    du2 = [dq, dk, dv, dgate2]
    dflog_b = dflog.astype(BF16)
    dh1 = _matmul_kparts(du2, fwi[:, :4 * f], chunk=f, name="d_h1_a")
    dh1 = _matmul(dflog_b, w_f, tb=True, add=dh1, name="d_h1_b")
    d_fwi_a = _matmul_nparts(h1, du2, chunk=f, out_dtype=BF16, name="d_fox_w_in_a")
    d_fwi_b = _matmul(h1, dflog_b, ta=True, out_dtype=BF16, name="d_fox_w_in_b")
    d_fwi = jnp.concatenate([d_fwi_a, d_fwi_b[:, :FOX_HEADS]], axis=1)
    dx1, dg1, _ = _rmsnorm_bwd(dh1, x1, norm_g[1], dx2, name="norm1_bwd")

    d_lwo = _matmul(y1, dx1, ta=True, out_dtype=BF16, name="d_lru_w_out")
    dy1 = _matmul(dx1, lwo, tb=True, name="d_y1")
    n_fwi = fox_w_in.shape[2]
    g_fwi4 = jnp.stack([d_fwi[:, s * n_fwi:(s + 1) * n_fwi] for s in range(N_CHIPS)])
    g_fwo4 = d_fwo.reshape(N_CHIPS, f // N_CHIPS, d)
    g_lwo4 = d_lwo.reshape(N_CHIPS, w // N_CHIPS, d)
    (dxb, dgate, d_cw, d_cb, d_wa, d_ba, d_wx, d_bx, d_ap), lands_early = _lru_bwd(
        u, hs, dy1, conv_w, conv_b, wa, ba, wx, bx, a_param, cg=cg, name="lru_bwd", ride=[g_lwo4, g_fwi4, g_fwo4])
    d_lwi_p = _matmul_nparts(h0, [dxb, dgate], chunk=cg, out_dtype=BF16, name="d_lru_w_in")
    csz = cg
    g_lwi4 = jnp.stack([d_lwi_p[:, 0:csz], d_lwi_p[:, 2 * csz:3 * csz], d_lwi_p[:, csz:2 * csz],
                        d_lwi_p[:, 3 * csz:]])
    dh0, lands_last = _matmul_kparts([dxb, dgate], lwi, chunk=cg, name="d_h0", ride=[g_lwi4])
    dx0, dg0, _ = _rmsnorm_bwd(dh0, x0, norm_g[0], dx1, name="norm0_bwd")
    lands = list(lands_last) + list(lands_early)
    halves = [_sum_slots(l, name="sum_" + nm) for l, nm in zip(lands, ("lru_w_in", "lru_w_out", "fox_w_in", "fox_w_out"))]
    big_g = _join_cores(halves, name="join_cores")

    small_g = [jnp.concatenate([dg0, dg1], axis=0), dgf.reshape(d), d_cw, d_cb, d_wa, d_ba, d_wx, d_bx, d_ap,
               db_f[:, :FOX_HEADS]]
    gsum = _allreduce_small(_pack(small_g), name="allreduce_small")
    zc = jnp.zeros((CONV_WIDTH, w), F32)
    pk_w = _pack([norm_g, final_g, zc, lru_conv_b, lru_wa, lru_ba, lru_wx, lru_bx, lru_a_param, fox_b_f])
    pk_m = _pack([m_norm_g, m_final_g, zc, m_lru_conv_b, m_lru_wa, m_lru_ba, m_lru_wx, m_lru_bx, m_lru_a_param,
                  m_fox_b_f])
    pk_v = _pack([v_norm_g, v_final_g, zc + 1.0, v_lru_conv_b, v_lru_wa, v_lru_ba, v_lru_wx, v_lru_bx,
                  v_lru_a_param, v_fox_b_f])
    s_g, s_delta, s_m, s_v = _adamw(pk_w[None], gsum[None], pk_m[None], pk_v[None], name="adamw_small")
    out_shapes = [norm_g.shape, final_g.shape, (CONV_WIDTH, w), lru_conv_b.shape, lru_wa.shape, lru_ba.shape,
                  lru_wx.shape, lru_bx.shape, lru_a_param.shape, fox_b_f.shape]
    sg = _unpack(s_g, out_shapes)
    sd = _unpack(s_delta, out_shapes)
    sm = _unpack(s_m, out_shapes)
    sv = _unpack(s_v, out_shapes)

    ncw = lru_conv_w.shape[2]
    g_cw_loc = lax.dynamic_slice_in_dim(sg[2], chip * ncw, ncw, axis=1)
    g_cw_loc, cw_d, cw_m, cw_v = _adamw(lru_conv_w, g_cw_loc[None], m_lru_conv_w, v_lru_conv_w, name="adamw_conv_w")

    big = []
    for nm, wt, g, mm, vv in (("lru_w_in", lru_w_in, big_g[0], m_lru_w_in, v_lru_w_in),
                              ("lru_w_out", lru_w_out, big_g[1], m_lru_w_out, v_lru_w_out),
                              ("fox_w_in", fox_w_in, big_g[2], m_fox_w_in, v_fox_w_in),
                              ("fox_w_out", fox_w_out, big_g[3], m_fox_w_out, v_fox_w_out)):
        big.append(tuple(_adamw(wt, g, mm, vv, name="adamw_" + nm)))

    def assemble(idx):
        small = (sg, sd, sm, sv)[idx]
        cw = (g_cw_loc, cw_d, cw_m, cw_v)[idx]
        return [small[0], small[1], big[0][idx], cw, small[3], small[4], small[5], small[6], small[7], small[8],
                big[1][idx], big[2][idx], small[9], big[3][idx]]

    grad_x = dx0.reshape(1, t, d)
    return (loss, grad_x, *assemble(0), *assemble(1), *assemble(2), *assemble(3))
```

```python
import math

import jax
import jax.numpy as jnp
from jax import lax
from jax.experimental import pallas as pl
from jax.experimental.pallas import tpu as pltpu

F32 = jnp.float32
BF16 = jnp.bfloat16

EPS = 1e-6
LRU_C = 8.0
LRU_BLOCK_W = 128
CONV_WIDTH = 4
FOX_HEADS = 16
FOX_HEAD_DIM = 64
NEG_INF = -1e30
ADAM_LR = 0.001
ADAM_B1 = 0.9
ADAM_B2 = 0.999
ADAM_EPS = 1e-08
ADAM_WD = 0.01
ADAM_STEP = 10

LANES = 128
SUBLANES = 8
VMEM_LIMIT = 56 * 1024 * 1024
TINY = 1e-30
N_CHIPS = 4
N_DEV = 8
MESH = pl.DeviceIdType.MESH


def _tile(n, pref):
    t = min(n, pref)
    while n % t:
        t //= 2
    return t


def _cparams(dims=None):
    return pltpu.CompilerParams(dimension_semantics=dims, vmem_limit_bytes=VMEM_LIMIT)


def _sigmoid(x):
    return 0.5 * jnp.tanh(0.5 * x) + 0.5


def _log1p(x):
    u = 1.0 + x
    return jnp.where(u == 1.0, x, jnp.log(u) * x / (u - 1.0))


def _bf16_pieces(x):
    hi = x.astype(BF16)
    rest = x - hi.astype(F32)
    mid = rest.astype(BF16)
    return hi, mid, (rest - mid.astype(F32)).astype(BF16)


def _dot_01_left(m01, x):
    return sum(jnp.dot(m01, p, preferred_element_type=F32) for p in _bf16_pieces(x))


def _dot_01_right(x, m01):
    return sum(jnp.dot(p, m01, preferred_element_type=F32) for p in _bf16_pieces(x))


def _softplus(x):
    return jnp.maximum(x, 0.0) + _log1p(jnp.exp(-jnp.abs(x)))


MM_TILE = 1024
MM_FULL_K = 1536


def _matmul(a, b, *, name, ta=False, tb=False, out_dtype=F32, add=None, tm=MM_TILE, tn=MM_TILE, tk=None):
    if ta:
        kdim, m = a.shape
    else:
        m, kdim = a.shape
    if tb:
        n, kb = b.shape
    else:
        kb, n = b.shape
    assert kdim == kb, (a.shape, b.shape, ta, tb)
    if tk is None:
        tk = kdim if kdim <= MM_FULL_K else MM_TILE
    tm, tn, tk = _tile(m, tm), _tile(n, tn), _tile(kdim, tk)
    nk = kdim // tk
    dn = (((0 if ta else 1,), (1 if tb else 0,)), ((), ()))
    has_add = add is not None

    def body(*refs):
        if has_add:
            a_ref, b_ref, add_ref, o_ref = refs[:4]
        else:
            a_ref, b_ref, o_ref = refs[:3]
        part = lax.dot_general(a_ref[...].astype(BF16), b_ref[...].astype(BF16), dn, preferred_element_type=F32)

        def finish(r):
            if has_add:
                r = r + add_ref[...].astype(F32)
            o_ref[...] = r.astype(o_ref.dtype)

        if nk == 1:
            finish(part)
            return
        acc_ref = refs[-1]
        k = pl.program_id(2)

        @pl.when(k == 0)
        def _():
            acc_ref[...] = part

        @pl.when(k > 0)
        def _():
            acc_ref[...] += part

        @pl.when(k == nk - 1)
        def _():
            finish(acc_ref[...])

    a_spec = pl.BlockSpec((tk, tm), lambda i, j, k: (k, i)) if ta else pl.BlockSpec((tm, tk), lambda i, j, k: (i, k))
    b_spec = pl.BlockSpec((tn, tk), lambda i, j, k: (j, k)) if tb else pl.BlockSpec((tk, tn), lambda i, j, k: (k, j))
    o_spec = pl.BlockSpec((tm, tn), lambda i, j, k: (i, j))
    in_specs = [a_spec, b_spec] + ([o_spec] if has_add else [])
    args = (a, b) + ((add,) if has_add else ())
    return pl.pallas_call(
        body, name=name, grid=(m // tm, n // tn, nk), in_specs=in_specs, out_specs=o_spec,
        out_shape=jax.ShapeDtypeStruct((m, n), out_dtype),
        scratch_shapes=[pltpu.VMEM((tm, tn), F32)] if nk > 1 else [],
        compiler_params=_cparams(("parallel", "parallel", "arbitrary")),
    )(*args)


def _matmul_kparts(parts, b, *, chunk, name, tm=MM_TILE, tn=MM_TILE, ride=()):
    npart = len(parts)
    nride = len(ride)
    m = parts[0].shape[0]
    n, kdim = b.shape
    nk = kdim // chunk
    assert nk * chunk == kdim and sum(p.shape[1] for p in parts) == kdim and nk % npart == 0
    tm, tn = _tile(m, tm), _tile(n, tn)
    dn = (((1,), (1,)), ((), ()))
    steps = (m // tm) * (n // tn) * nk

    def body(*refs):
        ins, rest = refs[:npart + 1], refs[npart + 1:]
        ride_in, rest = rest[:nride], rest[nride:]
        o_ref, rest = rest[0], rest[1:]
        ride_out, rest = rest[:nride], rest[nride:]
        acc_ref, sems = rest[0], rest[1:]
        if not nride:
            core(*ins, o_ref, acc_ref)
            return
        step = (pl.program_id(0) * (n // tn) + pl.program_id(1)) * nk + pl.program_id(2)

        @pl.when(step == 0)
        def _():
            for cp in _scatter_copies(ride_in, ride_out, *sems):
                cp.start()

        core(*ins, o_ref, acc_ref)

        @pl.when(step == steps - 1)
        def _():
            for cp in _scatter_copies(ride_in, ride_out, *sems):
                cp.wait()

    def core(*refs):
        a_refs, b_ref, o_ref, acc_ref = refs[:npart], refs[npart], refs[npart + 1], refs[npart + 2]
        k = pl.program_id(2)

        @pl.when(k == 0)
        def _():
            acc_ref[...] = jnp.zeros_like(acc_ref)

        for s in range(npart):
            @pl.when(lax.rem(k, npart) == s)
            def _(s=s):
                acc_ref[...] += lax.dot_general(a_refs[s][...].astype(BF16), b_ref[...].astype(BF16), dn,
                                                preferred_element_type=F32)

        @pl.when(k == nk - 1)
        def _():
            o_ref[...] = acc_ref[...].astype(o_ref.dtype)

    a_specs = [pl.BlockSpec((tm, chunk), lambda i, j, k: (i, k // npart)) for _ in range(npart)]
    any_spec = pl.BlockSpec(memory_space=pl.ANY)
    ride_shape, ride_sems = _scatter_shapes(ride) if nride else ([], [])
    outs = pl.pallas_call(
        body, name=name, grid=(m // tm, n // tn, nk),
        in_specs=a_specs + [pl.BlockSpec((tn, chunk), lambda i, j, k: (j, k))] + [any_spec] * nride,
        out_specs=[pl.BlockSpec((tm, tn), lambda i, j, k: (i, j))] + [any_spec] * nride,
        out_shape=[jax.ShapeDtypeStruct((m, n), F32)] + ride_shape,
        scratch_shapes=[pltpu.VMEM((tm, tn), F32)] + ride_sems,
        compiler_params=_cparams(("arbitrary",) * 3 if nride else ("parallel", "parallel", "arbitrary")),
    )(*parts, b, *ride)
    return (outs[0], outs[1:]) if nride else outs[0]


def _matmul_nparts(a, parts, *, chunk, out_dtype, name, tm=MM_TILE, tk=MM_TILE):
    npart = len(parts)
    t, m = a.shape
    n = sum(p.shape[1] for p in parts)
    nj = n // chunk
    assert nj * chunk == n and nj % npart == 0
    tm, tk = _tile(m, tm), _tile(t, tk)
    nk = t // tk
    dn = (((0,), (0,)), ((), ()))

    def body(*refs):
        a_ref, b_refs, o_ref, acc_ref = refs[0], refs[1:1 + npart], refs[1 + npart], refs[2 + npart]
        j, k = pl.program_id(1), pl.program_id(2)

        @pl.when(k == 0)
        def _():
            acc_ref[...] = jnp.zeros_like(acc_ref)

        for s in range(npart):
            @pl.when(lax.rem(j, npart) == s)
            def _(s=s):
                acc_ref[...] += lax.dot_general(a_ref[...].astype(BF16), b_refs[s][...].astype(BF16), dn,
                                                preferred_element_type=F32)

        @pl.when(k == nk - 1)
        def _():
            o_ref[...] = acc_ref[...].astype(o_ref.dtype)

    def b_spec(s):
        return pl.BlockSpec((tk, chunk), lambda i, j, k: (jnp.where(lax.rem(j, npart) == s, k, 0), j // npart))

    return pl.pallas_call(
        body, name=name, grid=(m // tm, nj, nk),
        in_specs=[pl.BlockSpec((tk, tm), lambda i, j, k: (k, i))] + [b_spec(s) for s in range(npart)],
        out_specs=pl.BlockSpec((tm, chunk), lambda i, j, k: (i, j)),
        out_shape=jax.ShapeDtypeStruct((m, n), out_dtype),
        scratch_shapes=[pltpu.VMEM((tm, chunk), F32)],
        compiler_params=_cparams(("parallel", "parallel", "arbitrary")),
    )(a, *parts)


def _rmsnorm(x, g, *, name):
    t, d = x.shape
    tt = _tile(t, 512)

    def body(x_ref, g_ref, o_ref):
        xf = x_ref[...]
        rstd = lax.rsqrt(jnp.mean(xf * xf, axis=-1, keepdims=True) + EPS)
        o_ref[...] = (xf * rstd * g_ref[...]).astype(o_ref.dtype)

    return pl.pallas_call(
        body, name=name, grid=(t // tt,),
        in_specs=[pl.BlockSpec((tt, d), lambda i: (i, 0)), pl.BlockSpec((1, d), lambda i: (0, 0))],
        out_specs=pl.BlockSpec((tt, d), lambda i: (i, 0)),
        out_shape=jax.ShapeDtypeStruct((t, d), BF16),
        compiler_params=_cparams(("parallel",)),
    )(x, g.reshape(1, d))


def _rmsnorm_bwd(dh, x, g, dres, *, name, bf16_copy):
    t, d = x.shape
    tt = _tile(t, 512)
    nt = t // tt

    def body(dh_ref, x_ref, g_ref, dres_ref, dx_ref, *out_refs):
        dg_ref = out_refs[-1]
        i = pl.program_id(0)

        @pl.when(i == 0)
        def _():
            dg_ref[...] = jnp.zeros_like(dg_ref)

        xf = x_ref[...]
        rstd = lax.rsqrt(jnp.mean(xf * xf, axis=-1, keepdims=True) + EPS)
        xhat = xf * rstd
        dhf = dh_ref[...].astype(F32)
        dxhat = dhf * g_ref[...]
        mt = jnp.mean(dxhat * xhat, axis=-1, keepdims=True)
        dx = dres_ref[...] + rstd * (dxhat - xhat * mt)
        dx_ref[...] = dx
        if bf16_copy:
            out_refs[0][...] = dx.astype(BF16)
        dg_ref[...] += jnp.sum(dhf * xhat, axis=0, keepdims=True)

    blk = pl.BlockSpec((tt, d), lambda i: (i, 0))
    vec = pl.BlockSpec((1, d), lambda i: (0, 0))
    low = [jax.ShapeDtypeStruct((t, d), BF16)] if bf16_copy else []
    outs = pl.pallas_call(
        body, name=name, grid=(nt,),
        in_specs=[blk, blk, vec, blk], out_specs=[blk] + [blk] * len(low) + [vec],
        out_shape=[jax.ShapeDtypeStruct((t, d), F32)] + low + [jax.ShapeDtypeStruct((1, d), F32)],
        compiler_params=_cparams(("arbitrary",)),
    )(dh, x, g.reshape(1, d), dres)
    return outs[0], (outs[1] if bf16_copy else None), outs[-1]


def _final_loss(x2, tgt, g, *, name):
    t, d = x2.shape
    tt = _tile(t, 512)

    def body(x_ref, t_ref, g_ref, l_ref, dx_ref, dxb_ref, dg_ref):
        i = pl.program_id(0)

        @pl.when(i == 0)
        def _():
            dg_ref[...] = jnp.zeros_like(dg_ref)
            l_ref[...] = jnp.zeros_like(l_ref)

        xf = x_ref[...]
        gg = g_ref[...]
        rstd = lax.rsqrt(jnp.mean(xf * xf, axis=-1, keepdims=True) + EPS)
        xhat = xf * rstd
        err = xhat * gg - t_ref[...]
        l_ref[...] += jnp.sum(err * err, axis=0, keepdims=True)
        dy = err * (1.0 / d)
        dxhat = dy * gg
        mt = jnp.mean(dxhat * xhat, axis=-1, keepdims=True)
        dx = rstd * (dxhat - xhat * mt)
        dx_ref[...] = dx
        dxb_ref[...] = dx.astype(dxb_ref.dtype)
        dg_ref[...] += jnp.sum(dy * xhat, axis=0, keepdims=True)

    blk = pl.BlockSpec((tt, d), lambda i: (i, 0))
    vec = pl.BlockSpec((1, d), lambda i: (0, 0))
    return pl.pallas_call(
        body, name=name, grid=(t // tt,),
        in_specs=[blk, blk, vec], out_specs=[vec, blk, blk, vec],
        out_shape=[jax.ShapeDtypeStruct((1, d), F32), jax.ShapeDtypeStruct((t, d), F32),
                   jax.ShapeDtypeStruct((t, d), BF16), jax.ShapeDtypeStruct((1, d), F32)],
        compiler_params=_cparams(("arbitrary",)),
    )(x2, tgt, g.reshape(1, d))


def _shift_down(prev8, cur, s):
    ext = jnp.concatenate([prev8, cur], axis=0)
    if s == 0:
        return cur
    return pltpu.roll(ext, s, 0)[SUBLANES:, :]


def _shift_up(cur, next8, s):
    if s == 0:
        return cur
    n = cur.shape[0]
    ext = jnp.concatenate([cur, next8], axis=0)
    return pltpu.roll(ext, n + SUBLANES - s, 0)[:n, :]


def _lru_gates(xc, wa, ba, wx, bx, sp):
    xcb = xc.astype(BF16)
    r = _sigmoid(jnp.dot(xcb, wa, preferred_element_type=F32) + ba)
    ig = _sigmoid(jnp.dot(xcb, wx, preferred_element_type=F32) + bx)
    log_a = -LRU_C * r * sp
    a = jnp.exp(log_a)
    z = -jnp.tanh(log_a) * (a * a + 1.0)
    inv_mult = lax.rsqrt(jnp.maximum(z, TINY))
    return r, ig, a, z * inv_mult, inv_mult


def _lru_specs(tt, cg, n_groups, nt, reverse):
    ncol = cg // LANES
    if reverse:
        ti = lambda i: nt - 1 - i
    else:
        ti = lambda i: i
    hb = tt // SUBLANES
    cur = lambda col: pl.BlockSpec((tt, cg), lambda g, i: (ti(i), 2 * g + col))
    prev = lambda col: pl.BlockSpec((SUBLANES, cg), lambda g, i: (jnp.maximum(ti(i) * hb - 1, 0), 2 * g + col))
    chan = lambda rows: pl.BlockSpec((rows, cg), lambda g, i: (0, g))
    wblk = pl.BlockSpec((ncol, LRU_BLOCK_W, LRU_BLOCK_W), lambda g, i: (g, 0, 0))
    plain = pl.BlockSpec((tt, cg), lambda g, i: (ti(i), g))
    plain_prev = pl.BlockSpec((SUBLANES, cg), lambda g, i: (jnp.maximum(ti(i) * hb - 1, 0), g))
    return cur, prev, chan, wblk, plain, plain_prev


def _chip_gather_copies(ins, outs, send, recv, loc):
    n = len(ins)
    x, y, c = lax.axis_index("x"), lax.axis_index("y"), lax.axis_index("c")
    me = 2 * x + y
    copies = [pltpu.make_async_copy(ins[k], outs[k].at[me], loc.at[k]) for k in range(n)]
    for r, (rx, ry) in enumerate(((1, 0), (0, 1), (1, 1))):
        for k in range(n):
            copies.append(pltpu.make_async_remote_copy(
                src_ref=ins[k], dst_ref=outs[k].at[me], send_sem=send.at[r * n + k], recv_sem=recv.at[r * n + k],
                device_id=(_flip(x, rx), _flip(y, ry), c), device_id_type=MESH))
    return copies


def _lru_fwd(u, conv_w, conv_b, wa, ba, wx, bx, a_param, *, cg, name, ride=()):
    nride = len(ride)
    t, w2 = u.shape
    w = w2 // 2
    n_groups = w // cg
    ncol = cg // LANES
    tt = _tile(t, 256)
    nt = t // tt
    cur, prev, chan, wblk, plain, _ = _lru_specs(tt, cg, n_groups, nt, False)

    def body(*refs):
        n_in, n_out, n_scr = 10, 2, 3
        ins, rest = refs[:n_in], refs[n_in:]
        ride_in, rest = rest[:nride], rest[nride:]
        outs, rest = rest[:n_out], rest[n_out:]
        ride_out, rest = rest[:nride], rest[nride:]
        scr, sems = rest[:n_scr], rest[n_scr:]
        if not nride:
            core(*ins, *outs, *scr)
            return
        step = pl.program_id(0) * nt + pl.program_id(1)

        @pl.when(step == 0)
        def _():
            for cp in _chip_gather_copies(ride_in, ride_out, *sems):
                cp.start()

        core(*ins, *outs, *scr)

        @pl.when(step == n_groups * nt - 1)
        def _():
            for cp in _chip_gather_copies(ride_in, ride_out, *sems):
                cp.wait()

    def core(xb_ref, xp_ref, gate_ref, cw_ref, cb_ref, wa_ref, ba_ref, wx_ref, bx_ref, ap_ref,
             y_ref, hs_ref, h_ref, a_s, b_s):
        i = pl.program_id(1)

        @pl.when(i == 0)
        def _():
            h_ref[...] = jnp.zeros_like(h_ref)

        keep = (i > 0).astype(F32)
        for n in range(ncol):
            sl = slice(n * LANES, (n + 1) * LANES)
            xb = xb_ref[:, sl]
            xp = xp_ref[:, sl] * keep
            xc = cb_ref[:, sl] + cw_ref[3:4, sl] * xb
            for s in range(1, CONV_WIDTH):
                xc = xc + cw_ref[3 - s:4 - s, sl] * _shift_down(xp, xb, s)
            sp = _softplus(-ap_ref[:, sl])
            _, ig, a, mult, _ = _lru_gates(xc, wa_ref[n].astype(BF16), ba_ref[:, sl],
                                           wx_ref[n].astype(BF16), bx_ref[:, sl], sp)
            a_s[:, sl] = a
            b_s[:, sl] = mult * (ig * xc)

        def step(g, h):
            base = pl.multiple_of(g * SUBLANES, SUBLANES)
            for r in range(SUBLANES):
                h = a_s[pl.ds(base + r, 1), :] * h + b_s[pl.ds(base + r, 1), :]
                hs_ref[pl.ds(base + r, 1), :] = h
            return h

        h = lax.fori_loop(0, tt // SUBLANES, step, h_ref[0:1, :])
        h_ref[0:1, :] = h
        gate = gate_ref[...]
        y_ref[...] = (hs_ref[...] * (gate * _sigmoid(gate))).astype(y_ref.dtype)

    any_spec = pl.BlockSpec(memory_space=pl.ANY)
    ride_sems = [pltpu.SemaphoreType.DMA((3 * nride,)), pltpu.SemaphoreType.DMA((3 * nride,)),
                 pltpu.SemaphoreType.DMA((nride,))] if nride else []
    outs = pl.pallas_call(
        body, name=name, grid=(n_groups, nt),
        in_specs=[cur(0), prev(0), cur(1), chan(CONV_WIDTH), chan(1), wblk, chan(1), wblk, chan(1), chan(1)]
        + [any_spec] * nride,
        out_specs=[plain, plain] + [any_spec] * nride,
        out_shape=[jax.ShapeDtypeStruct((t, w), BF16), jax.ShapeDtypeStruct((t, w), F32)]
        + [jax.ShapeDtypeStruct((N_CHIPS,) + r.shape, r.dtype) for r in ride],
        scratch_shapes=[pltpu.VMEM((SUBLANES, cg), F32), pltpu.VMEM((tt, cg), F32), pltpu.VMEM((tt, cg), F32)]
        + ride_sems,
        compiler_params=_cparams(("arbitrary", "arbitrary")),
    )(u, u, u, conv_w, conv_b, wa, ba, wx, bx, a_param, *ride)
    return outs[0], outs[1], outs[2:]


def _lru_bwd(u, hs, dy, conv_w, conv_b, wa, ba, wx, bx, a_param, *, cg, name, ride=()):
    nride = len(ride)
    t, w2 = u.shape
    w = w2 // 2
    n_groups = w // cg
    ncol = cg // LANES
    tt = _tile(t, 256)
    nt = t // tt
    cur, prev, chan, wblk, plain, plain_prev = _lru_specs(tt, cg, n_groups, nt, True)
    tn_dims = (((0,), (0,)), ((), ()))
    nt_dims = (((1,), (1,)), ((), ()))

    def body(*refs):
        n_in, n_out, n_scr = 13, 9, 5
        ins, rest = refs[:n_in], refs[n_in:]
        ride_in, rest = rest[:nride], rest[nride:]
        outs, rest = rest[:n_out], rest[n_out:]
        ride_out, rest = rest[:nride], rest[nride:]
        scr, sems = rest[:n_scr], rest[n_scr:]
        if not nride:
            core(*ins, *outs, *scr)
            return
        step = pl.program_id(0) * nt + pl.program_id(1)

        @pl.when(step == 0)
        def _():
            for cp in _scatter_copies(ride_in, ride_out, *sems):
                cp.start()

        core(*ins, *outs, *scr)

        @pl.when(step == n_groups * nt - 1)
        def _():
            for cp in _scatter_copies(ride_in, ride_out, *sems):
                cp.wait()

    def core(xb_ref, xp_ref, gate_ref, hs_ref, hp_ref, dy_ref, cw_ref, cb_ref, wa_ref, ba_ref, wx_ref, bx_ref,
             ap_ref, dxb_ref, dgate_ref, dcw_ref, dcb_ref, dwa_ref, dba_ref, dwx_ref, dbx_ref, dsp_ref,
             c_ref, nx_ref, a_s, dhs_s, lam_s):
        i = pl.program_id(1)
        first_time_block = i == nt - 1

        @pl.when(i == 0)
        def _():
            c_ref[...] = jnp.zeros_like(c_ref)
            nx_ref[...] = jnp.zeros_like(nx_ref)
            for r in (dcw_ref, dcb_ref, dwa_ref, dba_ref, dwx_ref, dbx_ref, dsp_ref):
                r[...] = jnp.zeros_like(r)

        keep = jnp.where(first_time_block, 0.0, 1.0).astype(F32)
        gate = gate_ref[...]
        sg = _sigmoid(gate)
        dyv = dy_ref[...]
        hsv = hs_ref[...]
        dhs_s[...] = dyv * (gate * sg)
        dgate_ref[...] = (dyv * hsv * (sg * (1.0 + gate * (1.0 - sg)))).astype(dgate_ref.dtype)

        saved = []
        for n in range(ncol):
            sl = slice(n * LANES, (n + 1) * LANES)
            xb = xb_ref[:, sl]
            xp = xp_ref[:, sl] * keep
            shifted = [xb] + [_shift_down(xp, xb, s) for s in range(1, CONV_WIDTH)]
            xc = cb_ref[:, sl] + cw_ref[3:4, sl] * xb
            for s in range(1, CONV_WIDTH):
                xc = xc + cw_ref[3 - s:4 - s, sl] * shifted[s]
            sp = _softplus(-ap_ref[:, sl])
            wab = wa_ref[n].astype(BF16)
            wxb = wx_ref[n].astype(BF16)
            r, ig, a, mult, inv_mult = _lru_gates(xc, wab, ba_ref[:, sl], wxb, bx_ref[:, sl], sp)
            a_s[:, sl] = a
            saved.append((sl, shifted, xc, sp, wab, wxb, r, ig, a, mult, inv_mult))

        def step(g, c):
            base = pl.multiple_of(tt - SUBLANES - g * SUBLANES, SUBLANES)
            for r in range(SUBLANES - 1, -1, -1):
                lam = dhs_s[pl.ds(base + r, 1), :] + c
                lam_s[pl.ds(base + r, 1), :] = lam
                c = a_s[pl.ds(base + r, 1), :] * lam
            return c

        c_ref[0:1, :] = lax.fori_loop(0, tt // SUBLANES, step, c_ref[0:1, :])

        for n in range(ncol):
            sl, shifted, xc, sp, wab, wxb, r, ig, a, mult, inv_mult = saved[n]
            lam = lam_s[:, sl]
            hprev = _shift_down(hp_ref[:, sl] * keep, hs_ref[:, sl], 1)
            da = lam * hprev
            dmult = lam * (ig * xc)
            dlog_a = da * a - dmult * (a * a * inv_mult)
            di = lam * (mult * xc)
            dxc = lam * (mult * ig)
            dr = dlog_a * (-LRU_C * sp)
            dsp_ref[:, sl] += jnp.sum(dlog_a * (-LRU_C * r), axis=0, keepdims=True)
            dza = dr * (r * (1.0 - r))
            dzx = di * (ig * (1.0 - ig))
            dba_ref[:, sl] += jnp.sum(dza, axis=0, keepdims=True)
            dbx_ref[:, sl] += jnp.sum(dzx, axis=0, keepdims=True)
            xcb = xc.astype(BF16)
            dzab = dza.astype(BF16)
            dzxb = dzx.astype(BF16)
            dwa_ref[n] += lax.dot_general(xcb, dzab, tn_dims, preferred_element_type=F32)
            dwx_ref[n] += lax.dot_general(xcb, dzxb, tn_dims, preferred_element_type=F32)
            dxc = dxc + lax.dot_general(dzab, wab, nt_dims, preferred_element_type=F32)
            dxc = dxc + lax.dot_general(dzxb, wxb, nt_dims, preferred_element_type=F32)
            dcb_ref[:, sl] += jnp.sum(dxc, axis=0, keepdims=True)
            for s in range(CONV_WIDTH):
                dcw_ref[3 - s:4 - s, sl] += jnp.sum(dxc * shifted[s], axis=0, keepdims=True)
            nx = nx_ref[:, sl]
            dxb = cw_ref[3:4, sl] * dxc
            for s in range(1, CONV_WIDTH):
                dxb = dxb + cw_ref[3 - s:4 - s, sl] * _shift_up(dxc, nx, s)
            dxb_ref[:, sl] = dxb.astype(dxb_ref.dtype)
            nx_ref[:, sl] = dxc[0:SUBLANES, :]

        @pl.when(first_time_block)
        def _():
            dsp_ref[...] = dsp_ref[...] * (-_sigmoid(-ap_ref[...]))

    dxb_spec = pl.BlockSpec((tt, cg), lambda g, i: (nt - 1 - i, g))
    any_spec = pl.BlockSpec(memory_space=pl.ANY)
    ride_shape, ride_sems = _scatter_shapes(ride) if nride else ([], [])
    outs = pl.pallas_call(
        body, name=name, grid=(n_groups, nt),
        in_specs=[cur(0), prev(0), cur(1), plain, plain_prev, plain, chan(CONV_WIDTH), chan(1), wblk, chan(1), wblk,
                  chan(1), chan(1)] + [any_spec] * nride,
        out_specs=[dxb_spec, dxb_spec, chan(CONV_WIDTH), chan(1), wblk, chan(1), wblk, chan(1), chan(1)]
        + [any_spec] * nride,
        out_shape=[jax.ShapeDtypeStruct((t, w), BF16), jax.ShapeDtypeStruct((t, w), BF16),
                   jax.ShapeDtypeStruct(conv_w.shape, F32), jax.ShapeDtypeStruct(conv_b.shape, F32),
                   jax.ShapeDtypeStruct(wa.shape, F32), jax.ShapeDtypeStruct(ba.shape, F32),
                   jax.ShapeDtypeStruct(wx.shape, F32), jax.ShapeDtypeStruct(bx.shape, F32),
                   jax.ShapeDtypeStruct(a_param.shape, F32)] + ride_shape,
        scratch_shapes=[pltpu.VMEM((SUBLANES, cg), F32), pltpu.VMEM((SUBLANES, cg), F32),
                        pltpu.VMEM((tt, cg), F32), pltpu.VMEM((tt, cg), F32), pltpu.VMEM((tt, cg), F32)] + ride_sems,
        compiler_params=_cparams(("arbitrary", "arbitrary")),
    )(u, u, u, hs, hs, dy, conv_w, conv_b, wa, ba, wx, bx, a_param, *ride)
    return outs[:9], outs[9:]


def _fgate_fwd(f, b_f, *, name):
    t, n = f.shape
    tt = _tile(t, 256)
    width = FOX_HEADS * FOX_HEAD_DIM

    def body(f_ref, b_ref, cum_ref, wide_ref, carry_ref):
        i = pl.program_id(0)

        @pl.when(i == 0)
        def _():
            carry_ref[...] = jnp.zeros_like(carry_ref)

        z = f_ref[...] + b_ref[...]
        lf = jnp.minimum(z, 0.0) - _log1p(jnp.exp(-jnp.abs(z)))
        row = lax.broadcasted_iota(jnp.int32, (tt, tt), 0)
        col = lax.broadcasted_iota(jnp.int32, (tt, tt), 1)
        tri = (col <= row).astype(BF16)
        cum = _dot_01_left(tri, lf) + carry_ref[0:1, :]
        cum_ref[...] = cum
        carry_ref[0:1, :] = cum[tt - 1:tt, :]
        head = lax.broadcasted_iota(jnp.int32, (n, width), 0)
        chan = lax.broadcasted_iota(jnp.int32, (n, width), 1) // FOX_HEAD_DIM
        wide_ref[...] = _dot_01_right(cum, (head == chan).astype(BF16))

    return pl.pallas_call(
        body, name=name, grid=(t // tt,),
        in_specs=[pl.BlockSpec((tt, n), lambda i: (i, 0)), pl.BlockSpec((1, n), lambda i: (0, 0))],
        out_specs=[pl.BlockSpec((tt, n), lambda i: (i, 0)), pl.BlockSpec((tt, width), lambda i: (i, 0))],
        out_shape=[jax.ShapeDtypeStruct((t, n), F32), jax.ShapeDtypeStruct((t, width), F32)],
        scratch_shapes=[pltpu.VMEM((SUBLANES, n), F32)],
        compiler_params=_cparams(("arbitrary",)),
    )(f, b_f)


def _fgate_bwd(dcum, f, b_f, *, name):
    t, n = f.shape
    tt = _tile(t, 256)
    nt = t // tt

    def body(dc_ref, f_ref, b_ref, df_ref, db_ref, carry_ref):
        i = pl.program_id(0)

        @pl.when(i == 0)
        def _():
            carry_ref[...] = jnp.zeros_like(carry_ref)
            db_ref[...] = jnp.zeros_like(db_ref)

        row = lax.broadcasted_iota(jnp.int32, (tt, tt), 0)
        col = lax.broadcasted_iota(jnp.int32, (tt, tt), 1)
        triu = (col >= row).astype(BF16)
        dlf = _dot_01_left(triu, dc_ref[...]) + carry_ref[0:1, :]
        carry_ref[0:1, :] = dlf[0:1, :]
        z = f_ref[...] + b_ref[...]
        df = dlf * _sigmoid(-z)
        df_ref[...] = df
        db_ref[...] += jnp.sum(df, axis=0, keepdims=True)

    blk = pl.BlockSpec((tt, n), lambda i: (nt - 1 - i, 0))
    vec = pl.BlockSpec((1, n), lambda i: (0, 0))
    return pl.pallas_call(
        body, name=name, grid=(nt,),
        in_specs=[blk, blk, vec], out_specs=[blk, vec],
        out_shape=[jax.ShapeDtypeStruct((t, n), F32), jax.ShapeDtypeStruct((1, n), F32)],
        scratch_shapes=[pltpu.VMEM((SUBLANES, n), F32)],
        compiler_params=_cparams(("arbitrary",)),
    )(dcum, f, b_f)


def _attn_fwd(start, qkv, ckt, gate, *, name, tq, tk):
    t = qkv.shape[0]
    f = gate.shape[1]
    npair = f // LANES
    nq = t // tq
    ratio = tq // tk
    assert tq == ratio * tk and t == nq * tq
    scale = 1.0 / math.sqrt(FOX_HEAD_DIM)
    nt_dims = (((1,), (1,)), ((), ()))

    def body(start_ref, q_ref, k_ref, v_ref, ck_ref, g_ref, o_ref, y_ref, l_ref):
        i = pl.program_id(1)
        pair = pl.program_id(0)
        firsts = (start_ref[2 * pair, i], start_ref[2 * pair + 1, i])
        both = jnp.maximum(firsts[0], firsts[1])
        lane = lax.broadcasted_iota(jnp.int32, (tq, LANES), 1)
        lo = lane < FOX_HEAD_DIM
        q2 = q_ref[...] * scale
        qs = (jnp.where(lo, q2, 0).astype(BF16), jnp.where(lo, 0, q2).astype(BF16))
        row = lax.broadcasted_iota(jnp.int32, (tq, tk), 0)
        col = lax.broadcasted_iota(jnp.int32, (tq, tk), 1)

        def kv_step(j, carry, diag, heads=(0, 1)):
            off = pl.multiple_of(j * tk, tk)
            kj = k_ref[pl.ds(off, tk), :]
            vj = v_ref[pl.ds(off, tk), :]
            ck = ck_ref[:, pl.ds(off, tk)]
            new = list(carry)
            for h in heads:
                m, l, acc = carry[h]
                s = lax.dot_general(qs[h], kj, nt_dims, preferred_element_type=F32) - ck[h:h + 1, :]
                if diag is not None:
                    s = jnp.where(col + diag * tk <= row, s, NEG_INF)
                m_new = jnp.maximum(m, jnp.max(s, axis=-1, keepdims=True))
                alpha = jnp.exp(m - m_new)
                p = jnp.exp(s - m_new)
                l = alpha * l + jnp.sum(p, axis=-1, keepdims=True)
                acc = alpha * acc + jnp.dot(p.astype(BF16), vj, preferred_element_type=F32)
                new[h] = (m_new, l, acc)
            return tuple(new)

        carry = tuple((jnp.full((tq, 1), NEG_INF, F32), jnp.zeros((tq, 1), F32), jnp.zeros((tq, LANES), F32))
                      for _ in range(2))
        def run(lo, hi, carry, heads):
            twos = (hi - lo) // 2
            carry = lax.fori_loop(
                0, twos,
                lambda jj, c: kv_step(lo + 2 * jj + 1, kv_step(lo + 2 * jj, c, None, heads), None, heads), carry)
            return lax.fori_loop(lo + 2 * twos, hi, lambda j, c: kv_step(j, c, None, heads), carry)

        for h in range(2):
            carry = run(firsts[h], both, carry, (h,))
        carry = run(both, i * ratio, carry, (0, 1))
        for d in range(ratio):
            carry = kv_step(i * ratio + d, carry, d)
        (m0, l0, a0), (m1, l1, a1) = carry
        o = jnp.where(lo, a0 / l0, a1 / l1)
        o_ref[...] = o
        gate_v = g_ref[...]
        y_ref[...] = (o * (gate_v * _sigmoid(gate_v))).astype(y_ref.dtype)
        lse_t = jnp.transpose(jnp.where(lo, m0 + jnp.log(l0), m1 + jnp.log(l1)))
        l_ref[0:1, :] = lse_t[0:1, :]
        l_ref[1:2, :] = lse_t[FOX_HEAD_DIM:FOX_HEAD_DIM + 1, :]

    blk = lambda base: pl.BlockSpec((tq, LANES), lambda p, i, s: (i, base + p))
    full = lambda base: pl.BlockSpec((t, LANES), lambda p, i, s: (0, base + p))
    return pl.pallas_call(
        body, name=name,
        grid_spec=pltpu.PrefetchScalarGridSpec(
            num_scalar_prefetch=1, grid=(npair, nq),
            in_specs=[blk(0), full(npair), full(2 * npair), pl.BlockSpec((None, 2, t), lambda p, i, s: (p, 0, 0)),
                      blk(0)],
            out_specs=[blk(0), blk(0), pl.BlockSpec((None, 2, tq), lambda p, i, s: (p, 0, i))]),
        out_shape=[jax.ShapeDtypeStruct((t, f), F32), jax.ShapeDtypeStruct((t, f), BF16),
                   jax.ShapeDtypeStruct((npair, 2, t), F32)],
        compiler_params=_cparams(("parallel", "arbitrary")),
    )(start, qkv, qkv, qkv, ckt, gate)


def _attn_bwd(end, qkv, do, lt, dt, cke, *, name):
    t, f = do.shape
    npair = f // LANES
    tk = _tile(t, ATTN_TILE)
    nk = t // tk
    scale = 1.0 / math.sqrt(FOX_HEAD_DIM)
    nt_dims = (((1,), (1,)), ((), ()))
    tn_dims = (((0,), (0,)), ((), ()))

    def body(end_ref, k_ref, v_ref, q_ref, do_ref, l_ref, d_ref, ck_ref, dq_out_ref, dk_ref, dv_ref, dck_ref, dcq_ref,
             dq_ref, dk_s, dv_s, dck_s):
        j = pl.program_id(1)
        pair = pl.program_id(0)
        lasts = (end_ref[2 * pair, j], end_ref[2 * pair + 1, j])
        both = jnp.minimum(lasts[0], lasts[1])

        @pl.when(j == 0)
        def _():
            dq_ref[...] = jnp.zeros_like(dq_ref)
            dcq_ref[...] = jnp.zeros_like(dcq_ref)

        lane = lax.broadcasted_iota(jnp.int32, (tk, LANES), 1)
        lo = lane < FOX_HEAD_DIM
        sel = (lo, jnp.logical_not(lo))
        kj = k_ref[...]
        vj = v_ref[...]
        km = tuple(jnp.where(sel[h], kj, 0).astype(BF16) for h in range(2))
        ckv = ck_ref[...]
        ckh = (ckv[:, 0:1], ckv[:, FOX_HEAD_DIM:FOX_HEAD_DIM + 1])
        row = lax.broadcasted_iota(jnp.int32, (tk, tk), 0)
        col = lax.broadcasted_iota(jnp.int32, (tk, tk), 1)
        causal = row <= col

        def q_step(i, carry, masked, heads=(0, 1)):
            off = pl.multiple_of(i * tk, tk)
            qi = q_ref[pl.ds(off, tk), :]
            doi = do_ref[pl.ds(off, tk), :]
            lrow = l_ref[:, pl.ds(off, tk)]
            drow = d_ref[:, pl.ds(off, tk)]
            dq_add = jnp.zeros((tk, LANES), F32)
            for h in heads:
                qm = jnp.where(sel[h], qi, 0).astype(BF16)
                dom = jnp.where(sel[h], doi, 0).astype(BF16)
                st = lax.dot_general(kj, qm, nt_dims, preferred_element_type=F32) * scale
                st = st - ckh[h] - lrow[h:h + 1, :]
                if masked:
                    st = jnp.where(causal, st, NEG_INF)
                pt = jnp.exp(st)
                dpt = lax.dot_general(vj, dom, nt_dims, preferred_element_type=F32)
                dst = pt * (dpt - drow[h:h + 1, :])
                ptb = pt.astype(BF16)
                dstb = dst.astype(BF16)
                dv_s[...] += jnp.dot(ptb, dom, preferred_element_type=F32)
                dk_s[...] += jnp.dot(dstb, qm, preferred_element_type=F32)
                dq_add = dq_add + lax.dot_general(dstb, km[h], tn_dims, preferred_element_type=F32)
                dck_s[:, h:h + 1] -= jnp.sum(dst, axis=-1, keepdims=True)
                dcq_ref[h:h + 1, pl.ds(off, tk)] += jnp.sum(dst, axis=0, keepdims=True)
            dq_ref[pl.ds(off, tk), :] += dq_add * scale
            return carry

        dk_s[...] = jnp.zeros_like(dk_s)
        dv_s[...] = jnp.zeros_like(dv_s)
        dck_s[...] = jnp.zeros_like(dck_s)
        carry = 0
        carry = q_step(j, carry, True)
        carry = lax.fori_loop(j + 1, both, lambda i, c: q_step(i, c, False), carry)
        for h in range(2):
            carry = lax.fori_loop(both, lasts[h], lambda i, c, h=h: q_step(i, c, False, (h,)), carry)
        dk_acc, dv_acc = dk_s[...], dv_s[...]
        dck = (dck_s[:, 0:1], dck_s[:, 1:2])
        dk_ref[...] = (dk_acc * scale).astype(dk_ref.dtype)
        dv_ref[...] = dv_acc.astype(dv_ref.dtype)
        dck_t = jnp.transpose(jnp.where(lo, dck[0], dck[1]))
        dck_ref[0:1, :] = dck_t[0:1, :]
        dck_ref[1:2, :] = dck_t[FOX_HEAD_DIM:FOX_HEAD_DIM + 1, :]

        @pl.when(j == nk - 1)
        def _():
            dq_out_ref[...] = dq_ref[...].astype(dq_out_ref.dtype)

    blk = lambda base: pl.BlockSpec((tk, LANES), lambda p, j, e: (j, base + p))
    full = lambda base: pl.BlockSpec((t, LANES), lambda p, j, e: (0, base + p))
    rows = pl.BlockSpec((None, 2, t), lambda p, j, e: (p, 0, 0))
    return pl.pallas_call(
        body, name=name,
        grid_spec=pltpu.PrefetchScalarGridSpec(
            num_scalar_prefetch=1, grid=(npair, nk),
            in_specs=[blk(npair), blk(2 * npair), full(0), full(0), rows, rows, blk(0)],
            out_specs=[full(0), blk(0), blk(0), pl.BlockSpec((None, 2, tk), lambda p, j, e: (p, 0, j)), rows],
            scratch_shapes=[pltpu.VMEM((t, LANES), F32), pltpu.VMEM((tk, LANES), F32), pltpu.VMEM((tk, LANES), F32),
                            pltpu.VMEM((tk, LANES), F32)]),
        out_shape=[jax.ShapeDtypeStruct((t, f), BF16), jax.ShapeDtypeStruct((t, f), BF16),
                   jax.ShapeDtypeStruct((t, f), BF16), jax.ShapeDtypeStruct((npair, 2, t), F32),
                   jax.ShapeDtypeStruct((npair, 2, t), F32)],
        compiler_params=_cparams(("parallel", "arbitrary")),
    )(end, qkv, qkv, qkv, do, lt, dt, cke)


ATTN_TILE = 512
ATTN_FWD_QUERIES = 512
EXP_ZERO = -104.0
BOUND_SLACK = 1.02


def _attn_row_stats(qkv, *, name):
    t = qkv.shape[0]
    f = qkv.shape[1] // 3
    tt = _tile(t, 512)

    def body(q_ref, k_ref, s_ref):
        q = q_ref[...].astype(F32)
        k = k_ref[...].astype(F32)
        chan = lax.broadcasted_iota(jnp.int32, (f, LANES), 0) // FOX_HEAD_DIM
        lane = lax.broadcasted_iota(jnp.int32, (f, LANES), 1)
        acc = jnp.zeros((tt, LANES), F32)
        for off, val in ((0, q * q), (FOX_HEADS, q * k), (2 * FOX_HEADS, k * k)):
            pick = (chan == lane - off).astype(BF16)
            acc = acc + jnp.dot(val.astype(BF16), pick, preferred_element_type=F32)
        s_ref[...] = acc

    return pl.pallas_call(
        body, name=name, grid=(t // tt,),
        in_specs=[pl.BlockSpec((tt, f), lambda i: (i, 0)), pl.BlockSpec((tt, f), lambda i: (i, 1))],
        out_specs=pl.BlockSpec((tt, LANES), lambda i: (i, 0)),
        out_shape=jax.ShapeDtypeStruct((t, LANES), F32),
        compiler_params=_cparams(("parallel",)),
    )(qkv, qkv)


def _attn_skip_tables(stats, cum16, tile):
    t = stats.shape[0]
    nb = t // tile
    scale = 1.0 / math.sqrt(FOX_HEAD_DIM)
    qn = jnp.sqrt(stats[:, :FOX_HEADS]) * scale
    sii = stats[:, FOX_HEADS:2 * FOX_HEADS] * scale - cum16
    kmax = jnp.max(jnp.sqrt(stats[:, 2 * FOX_HEADS:3 * FOX_HEADS]), axis=0, keepdims=True)
    arow = qn * kmax * BOUND_SLACK - sii + 0.5 * BOUND_SLACK
    a_blk = jnp.max(arow.reshape(nb, tile, FOX_HEADS), axis=1)
    c_blk = -cum16.reshape(nb, tile, FOX_HEADS)[:, tile - 1, :]
    dead = (a_blk[:, None, :] + c_blk[None, :, :]) < EXP_ZERO
    start_h = jnp.sum(dead.astype(jnp.int32), axis=1)
    blk = jnp.arange(nb, dtype=jnp.int32)
    start = jnp.minimum(start_h, blk[:, None]).T
    needs = start[:, :, None] <= blk[None, None, :]
    end = jnp.max(jnp.where(needs, blk[None, :, None] + 1, 0), axis=1)
    return start, jnp.maximum(end, blk[None, :] + 1)


def _fox_post_bwd(dy, o, gate, *, name):
    t, f = dy.shape
    tt = _tile(t, 512)

    def body(dy_ref, o_ref, g_ref, do_ref, dg_ref, dl_ref):
        g = g_ref[...]
        sg = _sigmoid(g)
        dyv = dy_ref[...]
        ov = o_ref[...]
        do = dyv * (g * sg)
        do_ref[...] = do.astype(do_ref.dtype)
        dg_ref[...] = (dyv * ov * (sg * (1.0 + g * (1.0 - sg)))).astype(dg_ref.dtype)
        chan = lax.broadcasted_iota(jnp.int32, (f, LANES), 0)
        head = lax.broadcasted_iota(jnp.int32, (f, LANES), 1)
        pick = (chan // FOX_HEAD_DIM == head).astype(BF16)
        dl_ref[...] = _dot_01_right(do * ov, pick)

    blk = pl.BlockSpec((tt, f), lambda i: (i, 0))
    return pl.pallas_call(
        body, name=name, grid=(t // tt,),
        in_specs=[blk, blk, blk], out_specs=[blk, blk, pl.BlockSpec((tt, LANES), lambda i: (i, 0))],
        out_shape=[jax.ShapeDtypeStruct((t, f), BF16), jax.ShapeDtypeStruct((t, f), BF16),
                   jax.ShapeDtypeStruct((t, LANES), F32)],
        compiler_params=_cparams(("parallel",)),
    )(dy, o, gate)


def _adamw(w, g, m, v, *, name):
    _, r, c = w.shape
    tr = _tile(r, 256) if r % SUBLANES == 0 else r
    c1 = 1.0 - ADAM_B1 ** ADAM_STEP
    c2 = 1.0 - ADAM_B2 ** ADAM_STEP

    def body(w_ref, g_ref, m_ref, v_ref, go_ref, d_ref, mo_ref, vo_ref):
        gv = g_ref[...]
        go_ref[...] = gv
        mn = ADAM_B1 * m_ref[...] + (1.0 - ADAM_B1) * gv
        vn = ADAM_B2 * v_ref[...] + (1.0 - ADAM_B2) * (gv * gv)
        mo_ref[...] = mn
        vo_ref[...] = vn
        d_ref[...] = -ADAM_LR * ((mn / c1) / (jnp.sqrt(vn / c2) + ADAM_EPS) + ADAM_WD * w_ref[...])

    blk = pl.BlockSpec((None, tr, c), lambda i: (0, i, 0))
    return pl.pallas_call(
        body, name=name, grid=(r // tr,), in_specs=[blk] * 4, out_specs=[blk] * 4,
        out_shape=[jax.ShapeDtypeStruct((1, r, c), F32)] * 4,
        compiler_params=_cparams(("parallel",)),
    )(w, g, m, v)


def _sum_slots(land, *, name):
    ns, r, c = land.shape
    tr = _tile(r, 64) if r % SUBLANES == 0 else r

    def body(l_ref, o_ref):
        acc = l_ref[0].astype(F32)
        for s in range(1, ns):
            acc = acc + l_ref[s].astype(F32)
        o_ref[...] = acc

    return pl.pallas_call(
        body, name=name, grid=(r // tr,),
        in_specs=[pl.BlockSpec((ns, tr, c), lambda i: (0, i, 0))],
        out_specs=pl.BlockSpec((tr, c), lambda i: (i, 0)),
        out_shape=jax.ShapeDtypeStruct((r, c), F32),
        compiler_params=_cparams(("parallel",)),
    )(land)


ANY = pl.BlockSpec(memory_space=pl.ANY)


def _flip(v, bit):
    return 1 - v if bit else v


def _gather_chips(shards, small, *, name):
    n = len(shards)
    rels = ((1, 0), (0, 1), (1, 1))

    def body(*refs):
        ins, small_in = refs[:n], refs[n]
        outs, small_out = refs[n + 1:2 * n + 1], refs[2 * n + 1]
        send, recv, loc = refs[2 * n + 2:]
        x, y, c = lax.axis_index("x"), lax.axis_index("y"), lax.axis_index("c")
        me = 2 * x + y
        sibling = (x, y, 1 - c)
        local = [pltpu.make_async_copy(ins[k], outs[k].at[me], loc.at[k]) for k in range(n)]
        local.append(pltpu.make_async_copy(small_in, small_out.at[me], loc.at[n]))
        for cp in local:
            cp.start()

        def rows(k):
            half = ins[k].shape[0] // 2
            return pl.ds(pl.multiple_of(c * half, SUBLANES), half)

        sends = []
        for r, (rx, ry) in enumerate(rels):
            to = (_flip(x, rx), _flip(y, ry), c)
            for k in range(n):
                cp = pltpu.make_async_remote_copy(
                    src_ref=ins[k].at[rows(k), :], dst_ref=outs[k].at[me, rows(k), :],
                    send_sem=send.at[r * n + k], recv_sem=recv.at[r * n + k], device_id=to, device_id_type=MESH)
                cp.start()
                sends.append(cp)
            cp = pltpu.make_async_remote_copy(
                src_ref=small_in, dst_ref=small_out.at[me], send_sem=send.at[6 * n + r], recv_sem=recv.at[6 * n + r],
                device_id=to, device_id_type=MESH)
            cp.start()
            sends.append(cp)
        for r, (rx, ry) in enumerate(rels):
            src_chip = 2 * _flip(x, rx) + _flip(y, ry)
            for k in range(n):
                landed = outs[k].at[src_chip, rows(k), :]
                sends[r * (n + 1) + k].wait_recv()
                cp = pltpu.make_async_remote_copy(
                    src_ref=landed, dst_ref=landed, send_sem=send.at[3 * n + r * n + k],
                    recv_sem=recv.at[3 * n + r * n + k], device_id=sibling, device_id_type=MESH)
                cp.start()
                sends.append(cp)
            sends[r * (n + 1) + n].wait_recv()
        for cp in sends[:3 * (n + 1)]:
            cp.wait_send()
        for cp in sends[3 * (n + 1):]:
            cp.wait()
        for cp in local:
            cp.wait()

    vmem = pl.BlockSpec(memory_space=pltpu.VMEM)
    return pl.pallas_call(
        body, name=name, in_specs=[vmem] * (n + 1), out_specs=[vmem] * (n + 1),
        out_shape=[jax.ShapeDtypeStruct((N_CHIPS,) + s.shape, s.dtype) for s in list(shards) + [small]],
        scratch_shapes=[pltpu.SemaphoreType.DMA((6 * n + 3,)), pltpu.SemaphoreType.DMA((6 * n + 3,)),
                        pltpu.SemaphoreType.DMA((n + 1,))],
        compiler_params=pltpu.CompilerParams(has_side_effects=True, vmem_limit_bytes=VMEM_LIMIT),
    )(*shards, small)


_RELS7 = tuple((r >> 2 & 1, r >> 1 & 1, r & 1) for r in range(1, N_DEV))


def _scatter_copies(ins, outs, send, recv, loc):
    n = len(ins)
    x, y, c = lax.axis_index("x"), lax.axis_index("y"), lax.axis_index("c")
    me = 4 * x + 2 * y + c

    def piece(k, px, py, pc):
        half = ins[k].shape[1] // 2
        return ins[k].at[2 * px + py, pl.ds(pc * half, half), :]

    copies = [pltpu.make_async_copy(piece(k, x, y, c), outs[k].at[me], loc.at[k]) for k in range(n)]
    for r, (rx, ry, rc) in enumerate(_RELS7):
        tx, ty, tc = _flip(x, rx), _flip(y, ry), _flip(c, rc)
        for k in range(n):
            copies.append(pltpu.make_async_remote_copy(
                src_ref=piece(k, tx, ty, tc), dst_ref=outs[k].at[me], send_sem=send.at[r * n + k],
                recv_sem=recv.at[r * n + k], device_id=(tx, ty, tc), device_id_type=MESH))
    return copies


def _scatter_shapes(grads):
    n = len(grads)
    out_shape = [jax.ShapeDtypeStruct((N_DEV, g.shape[1] // 2, g.shape[2]), g.dtype) for g in grads]
    sems = [pltpu.SemaphoreType.DMA((7 * n,)), pltpu.SemaphoreType.DMA((7 * n,)), pltpu.SemaphoreType.DMA((n,))]
    return out_shape, sems


def _join_cores(halves, *, name):
    n = len(halves)

    def body(*refs):
        ins, outs = refs[:n], refs[n:2 * n]
        send, recv, loc = refs[2 * n:]
        x, y, c = lax.axis_index("x"), lax.axis_index("y"), lax.axis_index("c")
        copies = []
        for k in range(n):
            half = ins[k].shape[0]
            mine = outs[k].at[0, pl.ds(c * half, half), :]
            cp = pltpu.make_async_copy(ins[k], mine, loc.at[k])
            cp.start()
            copies.append(cp)
            cp = pltpu.make_async_remote_copy(
                src_ref=ins[k], dst_ref=mine, send_sem=send.at[k], recv_sem=recv.at[k],
                device_id=(x, y, 1 - c), device_id_type=MESH)
            cp.start()
            copies.append(cp)
        for cp in copies:
            cp.wait()

    in_vmem = pl.BlockSpec(memory_space=pltpu.VMEM)
    return pl.pallas_call(
        body, name=name, in_specs=[in_vmem] * n, out_specs=[in_vmem] * n,
        out_shape=[jax.ShapeDtypeStruct((1, 2 * h.shape[0], h.shape[1]), h.dtype) for h in halves],
        scratch_shapes=[pltpu.SemaphoreType.DMA((n,)), pltpu.SemaphoreType.DMA((n,)), pltpu.SemaphoreType.DMA((n,))],
        compiler_params=pltpu.CompilerParams(has_side_effects=True, vmem_limit_bytes=VMEM_LIMIT),
    )(*halves)


def _allreduce_small(buf, *, name):
    r, n = buf.shape
    half = r // 2
    rels = ((1, 0), (0, 1), (1, 1))

    def body(in_ref, out_ref, sib_ref, chips_ref, send, recv):
        x, y, c = lax.axis_index("x"), lax.axis_index("y"), lax.axis_index("c")
        sibling = (x, y, 1 - c)
        chip = 2 * x + y
        rows = pl.ds(pl.multiple_of(c * half, SUBLANES), half)

        swap = pltpu.make_async_remote_copy(src_ref=in_ref, dst_ref=sib_ref, send_sem=send.at[0], recv_sem=recv.at[0],
                                            device_id=sibling, device_id_type=MESH)
        swap.start()
        swap.wait()
        chips_ref[chip] = in_ref[rows, :] + sib_ref[rows, :]

        sends = []
        for k, (rx, ry) in enumerate(rels):
            cp = pltpu.make_async_remote_copy(
                src_ref=chips_ref.at[chip], dst_ref=chips_ref.at[chip], send_sem=send.at[1 + k],
                recv_sem=recv.at[1 + k], device_id=(_flip(x, rx), _flip(y, ry), c), device_id_type=MESH)
            cp.start()
            sends.append(cp)
        for cp in sends:
            cp.wait()
        total = chips_ref[0]
        for s in range(1, N_CHIPS):
            total = total + chips_ref[s]
        out_ref[rows, :] = total

        back = pltpu.make_async_remote_copy(src_ref=out_ref.at[rows, :], dst_ref=out_ref.at[rows, :],
                                            send_sem=send.at[4], recv_sem=recv.at[4],
                                            device_id=sibling, device_id_type=MESH)
        back.start()
        back.wait()

    vmem = pl.BlockSpec(memory_space=pltpu.VMEM)
    return pl.pallas_call(
        body, name=name, in_specs=[vmem], out_specs=vmem,
        out_shape=jax.ShapeDtypeStruct((r, n), F32),
        scratch_shapes=[pltpu.VMEM((r, n), F32), pltpu.VMEM((N_CHIPS, half, n), F32),
                        pltpu.SemaphoreType.DMA((5,)), pltpu.SemaphoreType.DMA((5,))],
        compiler_params=pltpu.CompilerParams(has_side_effects=True, vmem_limit_bytes=VMEM_LIMIT),
    )(buf)


def _pack(arrs):
    flat = []
    for a in arrs:
        v = a.reshape(-1)
        pad = (-v.shape[0]) % LANES
        if pad:
            v = jnp.pad(v, (0, pad))
        flat.append(v)
    v = jnp.concatenate(flat)
    pad = (-v.shape[0]) % (LANES * SUBLANES)
    if pad:
        v = jnp.pad(v, (0, pad))
    return v.reshape(-1, LANES)


def _unpack(buf, shapes):
    v = buf.reshape(-1)
    out, off = [], 0
    for s in shapes:
        n = math.prod(s)
        out.append(v[off:off + n].reshape(s))
        off += n + (-n) % LANES
    return out


def kernel(x, norm_g, final_g, lru_w_in, lru_conv_w, lru_conv_b, lru_wa, lru_ba, lru_wx, lru_bx, lru_a_param, lru_w_out, fox_w_in, fox_b_f, fox_w_out, loss_target, m_norm_g, m_final_g, m_lru_w_in, m_lru_conv_w, m_lru_conv_b, m_lru_wa, m_lru_ba, m_lru_wx, m_lru_bx, m_lru_a_param, m_lru_w_out, m_fox_w_in, m_fox_b_f, m_fox_w_out, v_norm_g, v_final_g, v_lru_w_in, v_lru_conv_w, v_lru_conv_b, v_lru_wa, v_lru_ba, v_lru_wx, v_lru_bx, v_lru_a_param, v_lru_w_out, v_fox_w_in, v_fox_b_f, v_fox_w_out):
    t, d = x.shape[1], x.shape[2]
    w = lru_wa.shape[1] * LRU_BLOCK_W
    f = FOX_HEADS * FOX_HEAD_DIM
    npair = f // LANES
    x0 = x.reshape(t, d)
    tgt = loss_target.reshape(t, d)
    chip = 2 * lax.axis_index("x") + lax.axis_index("y")

    g_lwi, g_lwo, g_cw = _gather_chips(
        [lru_w_in[0].astype(BF16), lru_w_out[0].astype(BF16)], lru_conv_w[0], name="gather_weights")
    cg = w // 2
    lwi = jnp.concatenate([g_lwi[0], g_lwi[2], g_lwi[1], g_lwi[3]], axis=1)
    lwo = g_lwo.reshape(w, d)
    conv_w = jnp.concatenate([g_cw[s] for s in range(N_CHIPS)], axis=1)
    conv_b, ba, bx, a_param = lru_conv_b, lru_ba, lru_bx, lru_a_param
    wa, wx = lru_wa[0], lru_wx[0]
    b_f = jnp.pad(fox_b_f, ((0, 0), (0, LANES - FOX_HEADS)))

    h0 = _rmsnorm(x0, norm_g[0], name="norm0")
    u = _matmul(h0, lwi, name="lru_in")
    y1, hs, (g_fwi, g_fwo) = _lru_fwd(u, conv_w, conv_b, wa, ba, wx, bx, a_param, cg=cg, name="lru_fwd",
                                      ride=[fox_w_in[0].astype(BF16), fox_w_out[0].astype(BF16)])
    fwi = jnp.concatenate([g_fwi[s] for s in range(N_CHIPS)], axis=1)
    w_qkv, w_g2 = fwi[:, :3 * f], fwi[:, 3 * f:4 * f]
    w_f = jnp.pad(fwi[:, 4 * f:], ((0, 0), (0, LANES - FOX_HEADS)))
    fwo = g_fwo.reshape(f, d)
    x1 = _matmul(y1, lwo, add=x0, name="lru_out")
    h1 = _rmsnorm(x1, norm_g[1], name="norm1")
    qkv = _matmul(h1, w_qkv, out_dtype=BF16, name="fox_qkv")
    gate2 = _matmul(h1, w_g2, name="fox_gate")
    flog = _matmul(h1, w_f, name="fox_f")
    cum, cke = _fgate_fwd(flog, b_f, name="fgate_fwd")
    cum16 = cum[:, :FOX_HEADS]
    ckt = cum16.T.reshape(npair, 2, t)
    a_tk, a_tq = _tile(t, ATTN_TILE), _tile(t, ATTN_FWD_QUERIES)
    a_start, a_end = _attn_skip_tables(_attn_row_stats(qkv, name="attn_row_stats"), cum16, a_tk)
    a_start_fwd = jnp.min(a_start.reshape(FOX_HEADS, t // a_tq, a_tq // a_tk), axis=2)
    o, y2, lse = _attn_fwd(a_start_fwd, qkv, ckt, gate2, name="attn_fwd", tq=a_tq, tk=a_tk)
    x2 = _matmul(y2, fwo, add=x1, name="fox_out")
    lsum, dx2, dx2_b, dgf = _final_loss(x2, tgt, final_g, name="final_loss")
    loss = lax.psum(0.5 * jnp.sum(lsum) / d, ("x", "y", "c"))

    d_fwo = _matmul(y2, dx2_b, ta=True, out_dtype=BF16, name="d_fox_w_out")
    dy2 = _matmul(dx2_b, fwo, tb=True, name="d_y2")
    do, dgate2, dl = _fox_post_bwd(dy2, o, gate2, name="fox_post_bwd")
    lt = lse
    dt = dl[:, :FOX_HEADS].T.reshape(npair, 2, t)
    dq, dk, dv, dck, dcq = _attn_bwd(a_end, qkv, do, lt, dt, cke, name="attn_bwd")
    dcum = jnp.pad((dck + dcq).reshape(FOX_HEADS, t).T, ((0, 0), (0, LANES - FOX_HEADS)))
    dflog, db_f = _fgate_bwd(dcum, flog, b_f, name="fgate_bwd")
    du2 = [dq, dk, dv, dgate2]
    dflog_b = dflog.astype(BF16)
    dh1 = _matmul_kparts(du2, fwi[:, :4 * f], chunk=f, name="d_h1_a")
    dh1 = _matmul(dflog_b, w_f, tb=True, add=dh1, name="d_h1_b")
    d_fwi_a = _matmul_nparts(h1, du2, chunk=f, out_dtype=BF16, name="d_fox_w_in_a")
    d_fwi_b = _matmul(h1, dflog_b, ta=True, out_dtype=BF16, name="d_fox_w_in_b")
    d_fwi = jnp.concatenate([d_fwi_a, d_fwi_b[:, :FOX_HEADS]], axis=1)
    dx1, dx1_b, dg1 = _rmsnorm_bwd(dh1, x1, norm_g[1], dx2, name="norm1_bwd", bf16_copy=True)

    d_lwo = _matmul(y1, dx1_b, ta=True, out_dtype=BF16, name="d_lru_w_out")
    dy1 = _matmul(dx1_b, lwo, tb=True, name="d_y1")
    n_fwi = fox_w_in.shape[2]
    g_fwi4 = jnp.stack([d_fwi[:, s * n_fwi:(s + 1) * n_fwi] for s in range(N_CHIPS)])
    g_fwo4 = d_fwo.reshape(N_CHIPS, f // N_CHIPS, d)
    g_lwo4 = d_lwo.reshape(N_CHIPS, w // N_CHIPS, d)
    (dxb, dgate, d_cw, d_cb, d_wa, d_ba, d_wx, d_bx, d_ap), lands_early = _lru_bwd(
        u, hs, dy1, conv_w, conv_b, wa, ba, wx, bx, a_param, cg=cg, name="lru_bwd", ride=[g_lwo4, g_fwi4, g_fwo4])
    d_lwi_p = _matmul_nparts(h0, [dxb, dgate], chunk=cg, out_dtype=BF16, name="d_lru_w_in")
    csz = cg
    g_lwi4 = jnp.stack([d_lwi_p[:, 0:csz], d_lwi_p[:, 2 * csz:3 * csz], d_lwi_p[:, csz:2 * csz],
                        d_lwi_p[:, 3 * csz:]])
    dh0, lands_last = _matmul_kparts([dxb, dgate], lwi, chunk=cg, name="d_h0", ride=[g_lwi4])
    dx0, _, dg0 = _rmsnorm_bwd(dh0, x0, norm_g[0], dx1, name="norm0_bwd", bf16_copy=False)
    lands = list(lands_last) + list(lands_early)
    halves = [_sum_slots(l, name="sum_" + nm) for l, nm in zip(lands, ("lru_w_in", "lru_w_out", "fox_w_in", "fox_w_out"))]
    big_g = _join_cores(halves, name="join_cores")

    small_g = [jnp.concatenate([dg0, dg1], axis=0), dgf.reshape(d), d_cw, d_cb, d_wa, d_ba, d_wx, d_bx, d_ap,
               db_f[:, :FOX_HEADS]]
    gsum = _allreduce_small(_pack(small_g), name="allreduce_small")
    zc = jnp.zeros((CONV_WIDTH, w), F32)
    pk_w = _pack([norm_g, final_g, zc, lru_conv_b, lru_wa, lru_ba, lru_wx, lru_bx, lru_a_param, fox_b_f])
    pk_m = _pack([m_norm_g, m_final_g, zc, m_lru_conv_b, m_lru_wa, m_lru_ba, m_lru_wx, m_lru_bx, m_lru_a_param,
                  m_fox_b_f])
    pk_v = _pack([v_norm_g, v_final_g, zc + 1.0, v_lru_conv_b, v_lru_wa, v_lru_ba, v_lru_wx, v_lru_bx,
                  v_lru_a_param, v_fox_b_f])
    s_g, s_delta, s_m, s_v = _adamw(pk_w[None], gsum[None], pk_m[None], pk_v[None], name="adamw_small")
    out_shapes = [norm_g.shape, final_g.shape, (CONV_WIDTH, w), lru_conv_b.shape, lru_wa.shape, lru_ba.shape,
                  lru_wx.shape, lru_bx.shape, lru_a_param.shape, fox_b_f.shape]
    sg = _unpack(s_g, out_shapes)
    sd = _unpack(s_delta, out_shapes)
    sm = _unpack(s_m, out_shapes)
    sv = _unpack(s_v, out_shapes)

    ncw = lru_conv_w.shape[2]
    g_cw_loc = lax.dynamic_slice_in_dim(sg[2], chip * ncw, ncw, axis=1)
    g_cw_loc, cw_d, cw_m, cw_v = _adamw(lru_conv_w, g_cw_loc[None], m_lru_conv_w, v_lru_conv_w, name="adamw_conv_w")

    big = []
    for nm, wt, g, mm, vv in (("lru_w_in", lru_w_in, big_g[0], m_lru_w_in, v_lru_w_in),
                              ("lru_w_out", lru_w_out, big_g[1], m_lru_w_out, v_lru_w_out),
                              ("fox_w_in", fox_w_in, big_g[2], m_fox_w_in, v_fox_w_in),
                              ("fox_w_out", fox_w_out, big_g[3], m_fox_w_out, v_fox_w_out)):
        big.append(tuple(_adamw(wt, g, mm, vv, name="adamw_" + nm)))

    def assemble(idx):
        small = (sg, sd, sm, sv)[idx]
        cw = (g_cw_loc, cw_d, cw_m, cw_v)[idx]
        return [small[0], small[1], big[0][idx], cw, small[3], small[4], small[5], small[6], small[7], small[8],
                big[1][idx], big[2][idx], small[9], big[3][idx]]

    grad_x = dx0.reshape(1, t, d)
    return (loss, grad_x, *assemble(0), *assemble(1), *assemble(2), *assemble(3))
```

```python
import math

import jax
import jax.numpy as jnp
from jax import lax
from jax.experimental import pallas as pl
from jax.experimental.pallas import tpu as pltpu

F32 = jnp.float32
BF16 = jnp.bfloat16

EPS = 1e-6
LRU_C = 8.0
LRU_BLOCK_W = 128
CONV_WIDTH = 4
FOX_HEADS = 16
FOX_HEAD_DIM = 64
NEG_INF = -1e30
ADAM_LR = 0.001
ADAM_B1 = 0.9
ADAM_B2 = 0.999
ADAM_EPS = 1e-08
ADAM_WD = 0.01
ADAM_STEP = 10

LANES = 128
SUBLANES = 8
VMEM_LIMIT = 56 * 1024 * 1024
TINY = 1e-30
N_CHIPS = 4
N_DEV = 8
MESH = pl.DeviceIdType.MESH


def _tile(n, pref):
    t = min(n, pref)
    while n % t:
        t //= 2
    return t


def _cparams(dims=None):
    return pltpu.CompilerParams(dimension_semantics=dims, vmem_limit_bytes=VMEM_LIMIT)


def _sigmoid(x):
    return 0.5 * jnp.tanh(0.5 * x) + 0.5


def _log1p(x):
    u = 1.0 + x
    return jnp.where(u == 1.0, x, jnp.log(u) * x / (u - 1.0))


def _bf16_pieces(x):
    hi = x.astype(BF16)
    rest = x - hi.astype(F32)
    mid = rest.astype(BF16)
    return hi, mid, (rest - mid.astype(F32)).astype(BF16)


def _dot_01_left(m01, x):
    return sum(jnp.dot(m01, p, preferred_element_type=F32) for p in _bf16_pieces(x))


def _dot_01_right(x, m01):
    return sum(jnp.dot(p, m01, preferred_element_type=F32) for p in _bf16_pieces(x))


def _softplus(x):
    return jnp.maximum(x, 0.0) + _log1p(jnp.exp(-jnp.abs(x)))


MM_TILE = 1024
MM_FULL_K = 1536


def _matmul(a, b, *, name, ta=False, tb=False, out_dtype=F32, add=None, tm=MM_TILE, tn=MM_TILE, tk=None):
    if ta:
        kdim, m = a.shape
    else:
        m, kdim = a.shape
    if tb:
        n, kb = b.shape
    else:
        kb, n = b.shape
    assert kdim == kb, (a.shape, b.shape, ta, tb)
    if tk is None:
        tk = kdim if kdim <= MM_FULL_K else MM_TILE
    tm, tn, tk = _tile(m, tm), _tile(n, tn), _tile(kdim, tk)
    nk = kdim // tk
    dn = (((0 if ta else 1,), (1 if tb else 0,)), ((), ()))
    has_add = add is not None

    def body(*refs):
        if has_add:
            a_ref, b_ref, add_ref, o_ref = refs[:4]
        else:
            a_ref, b_ref, o_ref = refs[:3]
        part = lax.dot_general(a_ref[...].astype(BF16), b_ref[...].astype(BF16), dn, preferred_element_type=F32)

        def finish(r):
            if has_add:
                r = r + add_ref[...].astype(F32)
            o_ref[...] = r.astype(o_ref.dtype)

        if nk == 1:
            finish(part)
            return
        acc_ref = refs[-1]
        k = pl.program_id(2)

        @pl.when(k == 0)
        def _():
            acc_ref[...] = part

        @pl.when(k > 0)
        def _():
            acc_ref[...] += part

        @pl.when(k == nk - 1)
        def _():
            finish(acc_ref[...])

    a_spec = pl.BlockSpec((tk, tm), lambda i, j, k: (k, i)) if ta else pl.BlockSpec((tm, tk), lambda i, j, k: (i, k))
    b_spec = pl.BlockSpec((tn, tk), lambda i, j, k: (j, k)) if tb else pl.BlockSpec((tk, tn), lambda i, j, k: (k, j))
    o_spec = pl.BlockSpec((tm, tn), lambda i, j, k: (i, j))
    in_specs = [a_spec, b_spec] + ([o_spec] if has_add else [])
    args = (a, b) + ((add,) if has_add else ())
    return pl.pallas_call(
        body, name=name, grid=(m // tm, n // tn, nk), in_specs=in_specs, out_specs=o_spec,
        out_shape=jax.ShapeDtypeStruct((m, n), out_dtype),
        scratch_shapes=[pltpu.VMEM((tm, tn), F32)] if nk > 1 else [],
        compiler_params=_cparams(("parallel", "parallel", "arbitrary")),
    )(*args)


def _matmul_kparts(parts, b, *, chunk, name, tm=MM_TILE, tn=MM_TILE, ride=()):
    npart = len(parts)
    nride = len(ride)
    m = parts[0].shape[0]
    n, kdim = b.shape
    nk = kdim // chunk
    assert nk * chunk == kdim and sum(p.shape[1] for p in parts) == kdim and nk % npart == 0
    tm, tn = _tile(m, tm), _tile(n, tn)
    dn = (((1,), (1,)), ((), ()))
    steps = (m // tm) * (n // tn) * nk

    def body(*refs):
        ins, rest = refs[:npart + 1], refs[npart + 1:]
        ride_in, rest = rest[:nride], rest[nride:]
        o_ref, rest = rest[0], rest[1:]
        ride_out, rest = rest[:nride], rest[nride:]
        acc_ref, sems = rest[0], rest[1:]
        if not nride:
            core(*ins, o_ref, acc_ref)
            return
        step = (pl.program_id(0) * (n // tn) + pl.program_id(1)) * nk + pl.program_id(2)

        @pl.when(step == 0)
        def _():
            for cp in _scatter_copies(ride_in, ride_out, *sems):
                cp.start()

        core(*ins, o_ref, acc_ref)

        @pl.when(step == steps - 1)
        def _():
            for cp in _scatter_copies(ride_in, ride_out, *sems):
                cp.wait()

    def core(*refs):
        a_refs, b_ref, o_ref, acc_ref = refs[:npart], refs[npart], refs[npart + 1], refs[npart + 2]
        k = pl.program_id(2)

        @pl.when(k == 0)
        def _():
            acc_ref[...] = jnp.zeros_like(acc_ref)

        for s in range(npart):
            @pl.when(lax.rem(k, npart) == s)
            def _(s=s):
                acc_ref[...] += lax.dot_general(a_refs[s][...].astype(BF16), b_ref[...].astype(BF16), dn,
                                                preferred_element_type=F32)

        @pl.when(k == nk - 1)
        def _():
            o_ref[...] = acc_ref[...].astype(o_ref.dtype)

    a_specs = [pl.BlockSpec((tm, chunk), lambda i, j, k: (i, k // npart)) for _ in range(npart)]
    any_spec = pl.BlockSpec(memory_space=pl.ANY)
    ride_shape, ride_sems = _scatter_shapes(ride) if nride else ([], [])
    outs = pl.pallas_call(
        body, name=name, grid=(m // tm, n // tn, nk),
        in_specs=a_specs + [pl.BlockSpec((tn, chunk), lambda i, j, k: (j, k))] + [any_spec] * nride,
        out_specs=[pl.BlockSpec((tm, tn), lambda i, j, k: (i, j))] + [any_spec] * nride,
        out_shape=[jax.ShapeDtypeStruct((m, n), F32)] + ride_shape,
        scratch_shapes=[pltpu.VMEM((tm, tn), F32)] + ride_sems,
        compiler_params=_cparams(("arbitrary",) * 3 if nride else ("parallel", "parallel", "arbitrary")),
    )(*parts, b, *ride)
    return (outs[0], outs[1:]) if nride else outs[0]


def _matmul_nparts(a, parts, *, chunk, out_dtype, name, tm=MM_TILE, tk=MM_TILE):
    npart = len(parts)
    t, m = a.shape
    n = sum(p.shape[1] for p in parts)
    nj = n // chunk
    assert nj * chunk == n and nj % npart == 0
    tm, tk = _tile(m, tm), _tile(t, tk)
    nk = t // tk
    dn = (((0,), (0,)), ((), ()))

    def body(*refs):
        a_ref, b_refs, o_ref, acc_ref = refs[0], refs[1:1 + npart], refs[1 + npart], refs[2 + npart]
        j, k = pl.program_id(1), pl.program_id(2)

        @pl.when(k == 0)
        def _():
            acc_ref[...] = jnp.zeros_like(acc_ref)

        for s in range(npart):
            @pl.when(lax.rem(j, npart) == s)
            def _(s=s):
                acc_ref[...] += lax.dot_general(a_ref[...].astype(BF16), b_refs[s][...].astype(BF16), dn,
                                                preferred_element_type=F32)

        @pl.when(k == nk - 1)
        def _():
            o_ref[...] = acc_ref[...].astype(o_ref.dtype)

    def b_spec(s):
        return pl.BlockSpec((tk, chunk), lambda i, j, k: (jnp.where(lax.rem(j, npart) == s, k, 0), j // npart))

    return pl.pallas_call(
        body, name=name, grid=(m // tm, nj, nk),
        in_specs=[pl.BlockSpec((tk, tm), lambda i, j, k: (k, i))] + [b_spec(s) for s in range(npart)],
        out_specs=pl.BlockSpec((tm, chunk), lambda i, j, k: (i, j)),
        out_shape=jax.ShapeDtypeStruct((m, n), out_dtype),
        scratch_shapes=[pltpu.VMEM((tm, chunk), F32)],
        compiler_params=_cparams(("parallel", "parallel", "arbitrary")),
    )(a, *parts)


def _rmsnorm(x, g, *, name):
    t, d = x.shape
    tt = _tile(t, 512)

    def body(x_ref, g_ref, o_ref):
        xf = x_ref[...]
        rstd = lax.rsqrt(jnp.mean(xf * xf, axis=-1, keepdims=True) + EPS)
        o_ref[...] = (xf * rstd * g_ref[...]).astype(o_ref.dtype)

    return pl.pallas_call(
        body, name=name, grid=(t // tt,),
        in_specs=[pl.BlockSpec((tt, d), lambda i: (i, 0)), pl.BlockSpec((1, d), lambda i: (0, 0))],
        out_specs=pl.BlockSpec((tt, d), lambda i: (i, 0)),
        out_shape=jax.ShapeDtypeStruct((t, d), BF16),
        compiler_params=_cparams(("parallel",)),
    )(x, g.reshape(1, d))


def _rmsnorm_bwd(dh, x, g, dres, *, name, bf16_copy):
    t, d = x.shape
    tt = _tile(t, 512)
    nt = t // tt

    def body(dh_ref, x_ref, g_ref, dres_ref, dx_ref, *out_refs):
        dg_ref = out_refs[-1]
        i = pl.program_id(0)

        @pl.when(i == 0)
        def _():
            dg_ref[...] = jnp.zeros_like(dg_ref)

        xf = x_ref[...]
        rstd = lax.rsqrt(jnp.mean(xf * xf, axis=-1, keepdims=True) + EPS)
        xhat = xf * rstd
        dhf = dh_ref[...].astype(F32)
        dxhat = dhf * g_ref[...]
        mt = jnp.mean(dxhat * xhat, axis=-1, keepdims=True)
        dx = dres_ref[...] + rstd * (dxhat - xhat * mt)
        dx_ref[...] = dx
        if bf16_copy:
            out_refs[0][...] = dx.astype(BF16)
        dg_ref[...] += jnp.sum(dhf * xhat, axis=0, keepdims=True)

    blk = pl.BlockSpec((tt, d), lambda i: (i, 0))
    vec = pl.BlockSpec((1, d), lambda i: (0, 0))
    low = [jax.ShapeDtypeStruct((t, d), BF16)] if bf16_copy else []
    outs = pl.pallas_call(
        body, name=name, grid=(nt,),
        in_specs=[blk, blk, vec, blk], out_specs=[blk] + [blk] * len(low) + [vec],
        out_shape=[jax.ShapeDtypeStruct((t, d), F32)] + low + [jax.ShapeDtypeStruct((1, d), F32)],
        compiler_params=_cparams(("arbitrary",)),
    )(dh, x, g.reshape(1, d), dres)
    return outs[0], (outs[1] if bf16_copy else None), outs[-1]


def _final_loss(x2, tgt, g, *, name):
    t, d = x2.shape
    tt = _tile(t, 512)

    def body(x_ref, t_ref, g_ref, l_ref, dx_ref, dxb_ref, dg_ref):
        i = pl.program_id(0)

        @pl.when(i == 0)
        def _():
            dg_ref[...] = jnp.zeros_like(dg_ref)
            l_ref[...] = jnp.zeros_like(l_ref)

        xf = x_ref[...]
        gg = g_ref[...]
        rstd = lax.rsqrt(jnp.mean(xf * xf, axis=-1, keepdims=True) + EPS)
        xhat = xf * rstd
        err = xhat * gg - t_ref[...]
        l_ref[...] += jnp.sum(err * err, axis=0, keepdims=True)
        dy = err * (1.0 / d)
        dxhat = dy * gg
        mt = jnp.mean(dxhat * xhat, axis=-1, keepdims=True)
        dx = rstd * (dxhat - xhat * mt)
        dx_ref[...] = dx
        dxb_ref[...] = dx.astype(dxb_ref.dtype)
        dg_ref[...] += jnp.sum(dy * xhat, axis=0, keepdims=True)

    blk = pl.BlockSpec((tt, d), lambda i: (i, 0))
    vec = pl.BlockSpec((1, d), lambda i: (0, 0))
    return pl.pallas_call(
        body, name=name, grid=(t // tt,),
        in_specs=[blk, blk, vec], out_specs=[vec, blk, blk, vec],
        out_shape=[jax.ShapeDtypeStruct((1, d), F32), jax.ShapeDtypeStruct((t, d), F32),
                   jax.ShapeDtypeStruct((t, d), BF16), jax.ShapeDtypeStruct((1, d), F32)],
        compiler_params=_cparams(("arbitrary",)),
    )(x2, tgt, g.reshape(1, d))


def _shift_down(prev8, cur, s):
    ext = jnp.concatenate([prev8, cur], axis=0)
    if s == 0:
        return cur
    return pltpu.roll(ext, s, 0)[SUBLANES:, :]


def _shift_up(cur, next8, s):
    if s == 0:
        return cur
    n = cur.shape[0]
    ext = jnp.concatenate([cur, next8], axis=0)
    return pltpu.roll(ext, n + SUBLANES - s, 0)[:n, :]


def _lru_gates(xc, wa, ba, wx, bx, sp):
    xcb = xc.astype(BF16)
    r = _sigmoid(jnp.dot(xcb, wa, preferred_element_type=F32) + ba)
    ig = _sigmoid(jnp.dot(xcb, wx, preferred_element_type=F32) + bx)
    log_a = -LRU_C * r * sp
    a = jnp.exp(log_a)
    z = -jnp.tanh(log_a) * (a * a + 1.0)
    inv_mult = lax.rsqrt(jnp.maximum(z, TINY))
    return r, ig, a, z * inv_mult, inv_mult


def _lru_specs(tt, cg, n_groups, nt, reverse):
    ncol = cg // LANES
    if reverse:
        ti = lambda i: nt - 1 - i
    else:
        ti = lambda i: i
    hb = tt // SUBLANES
    cur = lambda col: pl.BlockSpec((tt, cg), lambda g, i: (ti(i), 2 * g + col))
    prev = lambda col: pl.BlockSpec((SUBLANES, cg), lambda g, i: (jnp.maximum(ti(i) * hb - 1, 0), 2 * g + col))
    chan = lambda rows: pl.BlockSpec((rows, cg), lambda g, i: (0, g))
    wblk = pl.BlockSpec((ncol, LRU_BLOCK_W, LRU_BLOCK_W), lambda g, i: (g, 0, 0))
    plain = pl.BlockSpec((tt, cg), lambda g, i: (ti(i), g))
    plain_prev = pl.BlockSpec((SUBLANES, cg), lambda g, i: (jnp.maximum(ti(i) * hb - 1, 0), g))
    return cur, prev, chan, wblk, plain, plain_prev


def _chip_gather_copies(ins, outs, send, recv, loc):
    n = len(ins)
    x, y, c = lax.axis_index("x"), lax.axis_index("y"), lax.axis_index("c")
    me = 2 * x + y
    copies = [pltpu.make_async_copy(ins[k], outs[k].at[me], loc.at[k]) for k in range(n)]
    for r, (rx, ry) in enumerate(((1, 0), (0, 1), (1, 1))):
        for k in range(n):
            copies.append(pltpu.make_async_remote_copy(
                src_ref=ins[k], dst_ref=outs[k].at[me], send_sem=send.at[r * n + k], recv_sem=recv.at[r * n + k],
                device_id=(_flip(x, rx), _flip(y, ry), c), device_id_type=MESH))
    return copies


def _lru_fwd(u, conv_w, conv_b, wa, ba, wx, bx, a_param, *, cg, name, ride=()):
    nride = len(ride)
    t, w2 = u.shape
    w = w2 // 2
    n_groups = w // cg
    ncol = cg // LANES
    tt = _tile(t, 256)
    nt = t // tt
    cur, prev, chan, wblk, plain, _ = _lru_specs(tt, cg, n_groups, nt, False)

    def body(*refs):
        n_in, n_out, n_scr = 10, 2, 3
        ins, rest = refs[:n_in], refs[n_in:]
        ride_in, rest = rest[:nride], rest[nride:]
        outs, rest = rest[:n_out], rest[n_out:]
        ride_out, rest = rest[:nride], rest[nride:]
        scr, sems = rest[:n_scr], rest[n_scr:]
        if not nride:
            core(*ins, *outs, *scr)
            return
        step = pl.program_id(0) * nt + pl.program_id(1)

        @pl.when(step == 0)
        def _():
            for cp in _chip_gather_copies(ride_in, ride_out, *sems):
                cp.start()

        core(*ins, *outs, *scr)

        @pl.when(step == n_groups * nt - 1)
        def _():
            for cp in _chip_gather_copies(ride_in, ride_out, *sems):
                cp.wait()

    def core(xb_ref, xp_ref, gate_ref, cw_ref, cb_ref, wa_ref, ba_ref, wx_ref, bx_ref, ap_ref,
             y_ref, hs_ref, h_ref, a_s, b_s):
        i = pl.program_id(1)

        @pl.when(i == 0)
        def _():
            h_ref[...] = jnp.zeros_like(h_ref)

        keep = (i > 0).astype(F32)
        for n in range(ncol):
            sl = slice(n * LANES, (n + 1) * LANES)
            xb = xb_ref[:, sl]
            xp = xp_ref[:, sl] * keep
            xc = cb_ref[:, sl] + cw_ref[3:4, sl] * xb
            for s in range(1, CONV_WIDTH):
                xc = xc + cw_ref[3 - s:4 - s, sl] * _shift_down(xp, xb, s)
            sp = _softplus(-ap_ref[:, sl])
            _, ig, a, mult, _ = _lru_gates(xc, wa_ref[n].astype(BF16), ba_ref[:, sl],
                                           wx_ref[n].astype(BF16), bx_ref[:, sl], sp)
            a_s[:, sl] = a
            b_s[:, sl] = mult * (ig * xc)

        def step(g, h):
            base = pl.multiple_of(g * SUBLANES, SUBLANES)
            for r in range(SUBLANES):
                h = a_s[pl.ds(base + r, 1), :] * h + b_s[pl.ds(base + r, 1), :]
                hs_ref[pl.ds(base + r, 1), :] = h
            return h

        h = lax.fori_loop(0, tt // SUBLANES, step, h_ref[0:1, :])
        h_ref[0:1, :] = h
        gate = gate_ref[...]
        y_ref[...] = (hs_ref[...] * (gate * _sigmoid(gate))).astype(y_ref.dtype)

    any_spec = pl.BlockSpec(memory_space=pl.ANY)
    ride_sems = [pltpu.SemaphoreType.DMA((3 * nride,)), pltpu.SemaphoreType.DMA((3 * nride,)),
                 pltpu.SemaphoreType.DMA((nride,))] if nride else []
    outs = pl.pallas_call(
        body, name=name, grid=(n_groups, nt),
        in_specs=[cur(0), prev(0), cur(1), chan(CONV_WIDTH), chan(1), wblk, chan(1), wblk, chan(1), chan(1)]
        + [any_spec] * nride,
        out_specs=[plain, plain] + [any_spec] * nride,
        out_shape=[jax.ShapeDtypeStruct((t, w), BF16), jax.ShapeDtypeStruct((t, w), F32)]
        + [jax.ShapeDtypeStruct((N_CHIPS,) + r.shape, r.dtype) for r in ride],
        scratch_shapes=[pltpu.VMEM((SUBLANES, cg), F32), pltpu.VMEM((tt, cg), F32), pltpu.VMEM((tt, cg), F32)]
        + ride_sems,
        compiler_params=_cparams(("arbitrary", "arbitrary")),
    )(u, u, u, conv_w, conv_b, wa, ba, wx, bx, a_param, *ride)
    return outs[0], outs[1], outs[2:]


def _lru_bwd(u, hs, dy, conv_w, conv_b, wa, ba, wx, bx, a_param, *, cg, name, ride=()):
    nride = len(ride)
    t, w2 = u.shape
    w = w2 // 2
    n_groups = w // cg
    ncol = cg // LANES
    tt = _tile(t, 256)
    nt = t // tt
    cur, prev, chan, wblk, plain, plain_prev = _lru_specs(tt, cg, n_groups, nt, True)
    tn_dims = (((0,), (0,)), ((), ()))
    nt_dims = (((1,), (1,)), ((), ()))

    def body(*refs):
        n_in, n_out, n_scr = 13, 9, 5
        ins, rest = refs[:n_in], refs[n_in:]
        ride_in, rest = rest[:nride], rest[nride:]
        outs, rest = rest[:n_out], rest[n_out:]
        ride_out, rest = rest[:nride], rest[nride:]
        scr, sems = rest[:n_scr], rest[n_scr:]
        if not nride:
            core(*ins, *outs, *scr)
            return
        step = pl.program_id(0) * nt + pl.program_id(1)

        @pl.when(step == 0)
        def _():
            for cp in _scatter_copies(ride_in, ride_out, *sems):
                cp.start()

        core(*ins, *outs, *scr)

        @pl.when(step == n_groups * nt - 1)
        def _():
            for cp in _scatter_copies(ride_in, ride_out, *sems):
                cp.wait()

    def core(xb_ref, xp_ref, gate_ref, hs_ref, hp_ref, dy_ref, cw_ref, cb_ref, wa_ref, ba_ref, wx_ref, bx_ref,
             ap_ref, dxb_ref, dgate_ref, dcw_ref, dcb_ref, dwa_ref, dba_ref, dwx_ref, dbx_ref, dsp_ref,
             c_ref, nx_ref, a_s, dhs_s, lam_s):
        i = pl.program_id(1)
        first_time_block = i == nt - 1

        @pl.when(i == 0)
        def _():
            c_ref[...] = jnp.zeros_like(c_ref)
            nx_ref[...] = jnp.zeros_like(nx_ref)
            for r in (dcw_ref, dcb_ref, dwa_ref, dba_ref, dwx_ref, dbx_ref, dsp_ref):
                r[...] = jnp.zeros_like(r)

        keep = jnp.where(first_time_block, 0.0, 1.0).astype(F32)
        gate = gate_ref[...]
        sg = _sigmoid(gate)
        dyv = dy_ref[...]
        hsv = hs_ref[...]
        dhs_s[...] = dyv * (gate * sg)
        dgate_ref[...] = (dyv * hsv * (sg * (1.0 + gate * (1.0 - sg)))).astype(dgate_ref.dtype)

        saved = []
        for n in range(ncol):
            sl = slice(n * LANES, (n + 1) * LANES)
            xb = xb_ref[:, sl]
            xp = xp_ref[:, sl] * keep
            shifted = [xb] + [_shift_down(xp, xb, s) for s in range(1, CONV_WIDTH)]
            xc = cb_ref[:, sl] + cw_ref[3:4, sl] * xb
            for s in range(1, CONV_WIDTH):
                xc = xc + cw_ref[3 - s:4 - s, sl] * shifted[s]
            sp = _softplus(-ap_ref[:, sl])
            wab = wa_ref[n].astype(BF16)
            wxb = wx_ref[n].astype(BF16)
            r, ig, a, mult, inv_mult = _lru_gates(xc, wab, ba_ref[:, sl], wxb, bx_ref[:, sl], sp)
            a_s[:, sl] = a
            saved.append((sl, shifted, xc, sp, wab, wxb, r, ig, a, mult, inv_mult))

        def step(g, c):
            base = pl.multiple_of(tt - SUBLANES - g * SUBLANES, SUBLANES)
            for r in range(SUBLANES - 1, -1, -1):
                lam = dhs_s[pl.ds(base + r, 1), :] + c
                lam_s[pl.ds(base + r, 1), :] = lam
                c = a_s[pl.ds(base + r, 1), :] * lam
            return c

        c_ref[0:1, :] = lax.fori_loop(0, tt // SUBLANES, step, c_ref[0:1, :])

        for n in range(ncol):
            sl, shifted, xc, sp, wab, wxb, r, ig, a, mult, inv_mult = saved[n]
            lam = lam_s[:, sl]
            hprev = _shift_down(hp_ref[:, sl] * keep, hs_ref[:, sl], 1)
            da = lam * hprev
            dmult = lam * (ig * xc)
            dlog_a = da * a - dmult * (a * a * inv_mult)
            di = lam * (mult * xc)
            dxc = lam * (mult * ig)
            dr = dlog_a * (-LRU_C * sp)
            dsp_ref[:, sl] += jnp.sum(dlog_a * (-LRU_C * r), axis=0, keepdims=True)
            dza = dr * (r * (1.0 - r))
            dzx = di * (ig * (1.0 - ig))
            dba_ref[:, sl] += jnp.sum(dza, axis=0, keepdims=True)
            dbx_ref[:, sl] += jnp.sum(dzx, axis=0, keepdims=True)
            xcb = xc.astype(BF16)
            dzab = dza.astype(BF16)
            dzxb = dzx.astype(BF16)
            dwa_ref[n] += lax.dot_general(xcb, dzab, tn_dims, preferred_element_type=F32)
            dwx_ref[n] += lax.dot_general(xcb, dzxb, tn_dims, preferred_element_type=F32)
            dxc = dxc + lax.dot_general(dzab, wab, nt_dims, preferred_element_type=F32)
            dxc = dxc + lax.dot_general(dzxb, wxb, nt_dims, preferred_element_type=F32)
            dcb_ref[:, sl] += jnp.sum(dxc, axis=0, keepdims=True)
            for s in range(CONV_WIDTH):
                dcw_ref[3 - s:4 - s, sl] += jnp.sum(dxc * shifted[s], axis=0, keepdims=True)
            nx = nx_ref[:, sl]
            dxb = cw_ref[3:4, sl] * dxc
            for s in range(1, CONV_WIDTH):
                dxb = dxb + cw_ref[3 - s:4 - s, sl] * _shift_up(dxc, nx, s)
            dxb_ref[:, sl] = dxb.astype(dxb_ref.dtype)
            nx_ref[:, sl] = dxc[0:SUBLANES, :]

        @pl.when(first_time_block)
        def _():
            dsp_ref[...] = dsp_ref[...] * (-_sigmoid(-ap_ref[...]))

    dxb_spec = pl.BlockSpec((tt, cg), lambda g, i: (nt - 1 - i, g))
    any_spec = pl.BlockSpec(memory_space=pl.ANY)
    ride_shape, ride_sems = _scatter_shapes(ride) if nride else ([], [])
    outs = pl.pallas_call(
        body, name=name, grid=(n_groups, nt),
        in_specs=[cur(0), prev(0), cur(1), plain, plain_prev, plain, chan(CONV_WIDTH), chan(1), wblk, chan(1), wblk,
                  chan(1), chan(1)] + [any_spec] * nride,
        out_specs=[dxb_spec, dxb_spec, chan(CONV_WIDTH), chan(1), wblk, chan(1), wblk, chan(1), chan(1)]
        + [any_spec] * nride,
        out_shape=[jax.ShapeDtypeStruct((t, w), BF16), jax.ShapeDtypeStruct((t, w), BF16),
                   jax.ShapeDtypeStruct(conv_w.shape, F32), jax.ShapeDtypeStruct(conv_b.shape, F32),
                   jax.ShapeDtypeStruct(wa.shape, F32), jax.ShapeDtypeStruct(ba.shape, F32),
                   jax.ShapeDtypeStruct(wx.shape, F32), jax.ShapeDtypeStruct(bx.shape, F32),
                   jax.ShapeDtypeStruct(a_param.shape, F32)] + ride_shape,
        scratch_shapes=[pltpu.VMEM((SUBLANES, cg), F32), pltpu.VMEM((SUBLANES, cg), F32),
                        pltpu.VMEM((tt, cg), F32), pltpu.VMEM((tt, cg), F32), pltpu.VMEM((tt, cg), F32)] + ride_sems,
        compiler_params=_cparams(("arbitrary", "arbitrary")),
    )(u, u, u, hs, hs, dy, conv_w, conv_b, wa, ba, wx, bx, a_param, *ride)
    return outs[:9], outs[9:]


def _fgate_fwd(f, b_f, *, name):
    t, n = f.shape
    tt = _tile(t, 256)
    width = FOX_HEADS * FOX_HEAD_DIM

    def body(f_ref, b_ref, cum_ref, wide_ref, carry_ref):
        i = pl.program_id(0)

        @pl.when(i == 0)
        def _():
            carry_ref[...] = jnp.zeros_like(carry_ref)

        z = f_ref[...] + b_ref[...]
        lf = jnp.minimum(z, 0.0) - _log1p(jnp.exp(-jnp.abs(z)))
        row = lax.broadcasted_iota(jnp.int32, (tt, tt), 0)
        col = lax.broadcasted_iota(jnp.int32, (tt, tt), 1)
        tri = (col <= row).astype(BF16)
        cum = _dot_01_left(tri, lf) + carry_ref[0:1, :]
        cum_ref[...] = cum
        carry_ref[0:1, :] = cum[tt - 1:tt, :]
        head = lax.broadcasted_iota(jnp.int32, (n, width), 0)
        chan = lax.broadcasted_iota(jnp.int32, (n, width), 1) // FOX_HEAD_DIM
        wide_ref[...] = _dot_01_right(cum, (head == chan).astype(BF16))

    return pl.pallas_call(
        body, name=name, grid=(t // tt,),
        in_specs=[pl.BlockSpec((tt, n), lambda i: (i, 0)), pl.BlockSpec((1, n), lambda i: (0, 0))],
        out_specs=[pl.BlockSpec((tt, n), lambda i: (i, 0)), pl.BlockSpec((tt, width), lambda i: (i, 0))],
        out_shape=[jax.ShapeDtypeStruct((t, n), F32), jax.ShapeDtypeStruct((t, width), F32)],
        scratch_shapes=[pltpu.VMEM((SUBLANES, n), F32)],
        compiler_params=_cparams(("arbitrary",)),
    )(f, b_f)


def _fgate_bwd(dcum, f, b_f, *, name):
    t, n = f.shape
    tt = _tile(t, 256)
    nt = t // tt

    def body(dc_ref, f_ref, b_ref, df_ref, db_ref, carry_ref):
        i = pl.program_id(0)

        @pl.when(i == 0)
        def _():
            carry_ref[...] = jnp.zeros_like(carry_ref)
            db_ref[...] = jnp.zeros_like(db_ref)

        row = lax.broadcasted_iota(jnp.int32, (tt, tt), 0)
        col = lax.broadcasted_iota(jnp.int32, (tt, tt), 1)
        triu = (col >= row).astype(BF16)
        dlf = _dot_01_left(triu, dc_ref[...]) + carry_ref[0:1, :]
        carry_ref[0:1, :] = dlf[0:1, :]
        z = f_ref[...] + b_ref[...]
        df = dlf * _sigmoid(-z)
        df_ref[...] = df
        db_ref[...] += jnp.sum(df, axis=0, keepdims=True)

    blk = pl.BlockSpec((tt, n), lambda i: (nt - 1 - i, 0))
    vec = pl.BlockSpec((1, n), lambda i: (0, 0))
    return pl.pallas_call(
        body, name=name, grid=(nt,),
        in_specs=[blk, blk, vec], out_specs=[blk, vec],
        out_shape=[jax.ShapeDtypeStruct((t, n), F32), jax.ShapeDtypeStruct((1, n), F32)],
        scratch_shapes=[pltpu.VMEM((SUBLANES, n), F32)],
        compiler_params=_cparams(("arbitrary",)),
    )(dcum, f, b_f)


def _attn_fwd(start, qkv, ckt, gate, *, name, tq, tk):
    t = qkv.shape[0]
    f = gate.shape[1]
    npair = f // LANES
    nq = t // tq
    ratio = tq // tk
    assert tq == ratio * tk and t == nq * tq
    scale = 1.0 / math.sqrt(FOX_HEAD_DIM)
    nt_dims = (((1,), (1,)), ((), ()))

    def body(start_ref, q_ref, k_ref, v_ref, ck_ref, g_ref, o_ref, y_ref, l_ref):
        i = pl.program_id(1)
        e = pl.program_id(2)
        first = start_ref[2 * pl.program_id(0) + e, i]
        lane = lax.broadcasted_iota(jnp.int32, (tq, LANES), 1)
        mine = (lane < FOX_HEAD_DIM) == (e == 0)
        qh = jnp.where(mine, q_ref[...] * scale, 0).astype(BF16)
        row = lax.broadcasted_iota(jnp.int32, (tq, tk), 0)
        col = lax.broadcasted_iota(jnp.int32, (tq, tk), 1)

        def kv_step(j, carry, diag):
            m, l, acc = carry
            off = pl.multiple_of(j * tk, tk)
            kj = k_ref[pl.ds(off, tk), :]
            vj = v_ref[pl.ds(off, tk), :]
            s = lax.dot_general(qh, kj, nt_dims, preferred_element_type=F32) - ck_ref[pl.ds(e, 1), pl.ds(off, tk)]
            if diag is not None:
                s = jnp.where(col + diag * tk <= row, s, NEG_INF)
            m_new = jnp.maximum(m, jnp.max(s, axis=-1, keepdims=True))
            alpha = jnp.exp(m - m_new)
            p = jnp.exp(s - m_new)
            l = alpha * l + jnp.sum(p, axis=-1, keepdims=True)
            acc = alpha * acc + jnp.dot(p.astype(BF16), vj, preferred_element_type=F32)
            return m_new, l, acc

        carry = (jnp.full((tq, 1), NEG_INF, F32), jnp.zeros((tq, 1), F32), jnp.zeros((tq, LANES), F32))
        last = i * ratio
        twos = (last - first) // 2
        carry = lax.fori_loop(
            0, twos, lambda jj, c: kv_step(first + 2 * jj + 1, kv_step(first + 2 * jj, c, None), None), carry)
        carry = lax.fori_loop(first + 2 * twos, last, lambda j, c: kv_step(j, c, None), carry)
        for d in range(ratio):
            carry = kv_step(last + d, carry, d)
        m, l, acc = carry
        o = acc / l
        gate_v = g_ref[...]
        y = o * (gate_v * _sigmoid(gate_v))
        lse_t = jnp.transpose(jnp.broadcast_to(m + jnp.log(l), (tq, LANES)))

        @pl.when(e == 0)
        def _():
            o_ref[...] = o
            y_ref[...] = y.astype(y_ref.dtype)
            l_ref[0:1, :] = lse_t[0:1, :]

        @pl.when(e == 1)
        def _():
            o_ref[...] = jnp.where(mine, o, o_ref[...])
            y_ref[...] = jnp.where(mine, y.astype(y_ref.dtype), y_ref[...])
            l_ref[1:2, :] = lse_t[0:1, :]

    blk = lambda base: pl.BlockSpec((tq, LANES), lambda p, i, e, s: (i, base + p))
    full = lambda base: pl.BlockSpec((t, LANES), lambda p, i, e, s: (0, base + p))
    return pl.pallas_call(
        body, name=name,
        grid_spec=pltpu.PrefetchScalarGridSpec(
            num_scalar_prefetch=1, grid=(npair, nq, 2),
            in_specs=[blk(0), full(npair), full(2 * npair), pl.BlockSpec((None, 2, t), lambda p, i, e, s: (p, 0, 0)),
                      blk(0)],
            out_specs=[blk(0), blk(0), pl.BlockSpec((None, 2, tq), lambda p, i, e, s: (p, 0, i))]),
        out_shape=[jax.ShapeDtypeStruct((t, f), F32), jax.ShapeDtypeStruct((t, f), BF16),
                   jax.ShapeDtypeStruct((npair, 2, t), F32)],
        compiler_params=_cparams(("parallel", "arbitrary", "arbitrary")),
    )(start, qkv, qkv, qkv, ckt, gate)


def _attn_bwd(end, qkv, do, lt, dt, cke, *, name):
    t, f = do.shape
    npair = f // LANES
    tk = _tile(t, ATTN_TILE)
    nk = t // tk
    scale = 1.0 / math.sqrt(FOX_HEAD_DIM)
    nt_dims = (((1,), (1,)), ((), ()))
    tn_dims = (((0,), (0,)), ((), ()))

    def body(end_ref, k_ref, v_ref, q_ref, do_ref, l_ref, d_ref, ck_ref, dq_out_ref, dk_ref, dv_ref, dck_ref, dcq_ref,
             dq_ref, dk_s, dv_s, dck_s):
        j = pl.program_id(1)
        pair = pl.program_id(0)
        lasts = (end_ref[2 * pair, j], end_ref[2 * pair + 1, j])
        both = jnp.minimum(lasts[0], lasts[1])

        @pl.when(j == 0)
        def _():
            dq_ref[...] = jnp.zeros_like(dq_ref)
            dcq_ref[...] = jnp.zeros_like(dcq_ref)

        lane = lax.broadcasted_iota(jnp.int32, (tk, LANES), 1)
        lo = lane < FOX_HEAD_DIM
        sel = (lo, jnp.logical_not(lo))
        kj = k_ref[...]
        vj = v_ref[...]
        km = tuple(jnp.where(sel[h], kj, 0).astype(BF16) for h in range(2))
        ckv = ck_ref[...]
        ckh = (ckv[:, 0:1], ckv[:, FOX_HEAD_DIM:FOX_HEAD_DIM + 1])
        row = lax.broadcasted_iota(jnp.int32, (tk, tk), 0)
        col = lax.broadcasted_iota(jnp.int32, (tk, tk), 1)
        causal = row <= col

        def q_step(i, carry, masked, heads=(0, 1)):
            off = pl.multiple_of(i * tk, tk)
            qi = q_ref[pl.ds(off, tk), :]
            doi = do_ref[pl.ds(off, tk), :]
            lrow = l_ref[:, pl.ds(off, tk)]
            drow = d_ref[:, pl.ds(off, tk)]
            dq_add = jnp.zeros((tk, LANES), F32)
            for h in heads:
                qm = jnp.where(sel[h], qi, 0).astype(BF16)
                dom = jnp.where(sel[h], doi, 0).astype(BF16)
                st = lax.dot_general(kj, qm, nt_dims, preferred_element_type=F32) * scale
                st = st - ckh[h] - lrow[h:h + 1, :]
                if masked:
                    st = jnp.where(causal, st, NEG_INF)
                pt = jnp.exp(st)
                dpt = lax.dot_general(vj, dom, nt_dims, preferred_element_type=F32)
                dst = pt * (dpt - drow[h:h + 1, :])
                ptb = pt.astype(BF16)
                dstb = dst.astype(BF16)
                dv_s[...] += jnp.dot(ptb, dom, preferred_element_type=F32)
                dk_s[...] += jnp.dot(dstb, qm, preferred_element_type=F32)
                dq_add = dq_add + lax.dot_general(dstb, km[h], tn_dims, preferred_element_type=F32)
                dck_s[:, h:h + 1] -= jnp.sum(dst, axis=-1, keepdims=True)
                dcq_ref[h:h + 1, pl.ds(off, tk)] += jnp.sum(dst, axis=0, keepdims=True)
            dq_ref[pl.ds(off, tk), :] += dq_add * scale
            return carry

        dk_s[...] = jnp.zeros_like(dk_s)
        dv_s[...] = jnp.zeros_like(dv_s)
        dck_s[...] = jnp.zeros_like(dck_s)
        carry = 0
        carry = q_step(j, carry, True)
        carry = lax.fori_loop(j + 1, both, lambda i, c: q_step(i, c, False), carry)
        for h in range(2):
            carry = lax.fori_loop(both, lasts[h], lambda i, c, h=h: q_step(i, c, False, (h,)), carry)
        dk_acc, dv_acc = dk_s[...], dv_s[...]
        dck = (dck_s[:, 0:1], dck_s[:, 1:2])
        dk_ref[...] = (dk_acc * scale).astype(dk_ref.dtype)
        dv_ref[...] = dv_acc.astype(dv_ref.dtype)
        dck_t = jnp.transpose(jnp.where(lo, dck[0], dck[1]))
        dck_ref[0:1, :] = dck_t[0:1, :]
        dck_ref[1:2, :] = dck_t[FOX_HEAD_DIM:FOX_HEAD_DIM + 1, :]

        @pl.when(j == nk - 1)
        def _():
            dq_out_ref[...] = dq_ref[...].astype(dq_out_ref.dtype)

    blk = lambda base: pl.BlockSpec((tk, LANES), lambda p, j, e: (j, base + p))
    full = lambda base: pl.BlockSpec((t, LANES), lambda p, j, e: (0, base + p))
    rows = pl.BlockSpec((None, 2, t), lambda p, j, e: (p, 0, 0))
    return pl.pallas_call(
        body, name=name,
        grid_spec=pltpu.PrefetchScalarGridSpec(
            num_scalar_prefetch=1, grid=(npair, nk),
            in_specs=[blk(npair), blk(2 * npair), full(0), full(0), rows, rows, blk(0)],
            out_specs=[full(0), blk(0), blk(0), pl.BlockSpec((None, 2, tk), lambda p, j, e: (p, 0, j)), rows],
            scratch_shapes=[pltpu.VMEM((t, LANES), F32), pltpu.VMEM((tk, LANES), F32), pltpu.VMEM((tk, LANES), F32),
                            pltpu.VMEM((tk, LANES), F32)]),
        out_shape=[jax.ShapeDtypeStruct((t, f), BF16), jax.ShapeDtypeStruct((t, f), BF16),
                   jax.ShapeDtypeStruct((t, f), BF16), jax.ShapeDtypeStruct((npair, 2, t), F32),
                   jax.ShapeDtypeStruct((npair, 2, t), F32)],
        compiler_params=_cparams(("parallel", "arbitrary")),
    )(end, qkv, qkv, qkv, do, lt, dt, cke)


ATTN_TILE = 512
ATTN_FWD_QUERIES = 512
EXP_ZERO = -104.0
BOUND_SLACK = 1.02


def _attn_row_stats(qkv, *, name):
    t = qkv.shape[0]
    f = qkv.shape[1] // 3
    tt = _tile(t, 512)

    def body(q_ref, k_ref, s_ref):
        q = q_ref[...].astype(F32)
        k = k_ref[...].astype(F32)
        chan = lax.broadcasted_iota(jnp.int32, (f, LANES), 0) // FOX_HEAD_DIM
        lane = lax.broadcasted_iota(jnp.int32, (f, LANES), 1)
        acc = jnp.zeros((tt, LANES), F32)
        for off, val in ((0, q * q), (FOX_HEADS, q * k), (2 * FOX_HEADS, k * k)):
            pick = (chan == lane - off).astype(BF16)
            acc = acc + jnp.dot(val.astype(BF16), pick, preferred_element_type=F32)
        s_ref[...] = acc

    return pl.pallas_call(
        body, name=name, grid=(t // tt,),
        in_specs=[pl.BlockSpec((tt, f), lambda i: (i, 0)), pl.BlockSpec((tt, f), lambda i: (i, 1))],
        out_specs=pl.BlockSpec((tt, LANES), lambda i: (i, 0)),
        out_shape=jax.ShapeDtypeStruct((t, LANES), F32),
        compiler_params=_cparams(("parallel",)),
    )(qkv, qkv)


def _attn_skip_tables(stats, cum16, tile):
    t = stats.shape[0]
    nb = t // tile
    scale = 1.0 / math.sqrt(FOX_HEAD_DIM)
    qn = jnp.sqrt(stats[:, :FOX_HEADS]) * scale
    sii = stats[:, FOX_HEADS:2 * FOX_HEADS] * scale - cum16
    kmax = jnp.max(jnp.sqrt(stats[:, 2 * FOX_HEADS:3 * FOX_HEADS]), axis=0, keepdims=True)
    arow = qn * kmax * BOUND_SLACK - sii + 0.5 * BOUND_SLACK
    a_blk = jnp.max(arow.reshape(nb, tile, FOX_HEADS), axis=1)
    c_blk = -cum16.reshape(nb, tile, FOX_HEADS)[:, tile - 1, :]
    dead = (a_blk[:, None, :] + c_blk[None, :, :]) < EXP_ZERO
    start_h = jnp.sum(dead.astype(jnp.int32), axis=1)
    blk = jnp.arange(nb, dtype=jnp.int32)
    start = jnp.minimum(start_h, blk[:, None]).T
    needs = start[:, :, None] <= blk[None, None, :]
    end = jnp.max(jnp.where(needs, blk[None, :, None] + 1, 0), axis=1)
    return start, jnp.maximum(end, blk[None, :] + 1)


def _fox_post_bwd(dy, o, gate, *, name):
    t, f = dy.shape
    tt = _tile(t, 512)

    def body(dy_ref, o_ref, g_ref, do_ref, dg_ref, dl_ref):
        g = g_ref[...]
        sg = _sigmoid(g)
        dyv = dy_ref[...]
        ov = o_ref[...]
        do = dyv * (g * sg)
        do_ref[...] = do.astype(do_ref.dtype)
        dg_ref[...] = (dyv * ov * (sg * (1.0 + g * (1.0 - sg)))).astype(dg_ref.dtype)
        chan = lax.broadcasted_iota(jnp.int32, (f, LANES), 0)
        head = lax.broadcasted_iota(jnp.int32, (f, LANES), 1)
        pick = (chan // FOX_HEAD_DIM == head).astype(BF16)
        dl_ref[...] = _dot_01_right(do * ov, pick)

    blk = pl.BlockSpec((tt, f), lambda i: (i, 0))
    return pl.pallas_call(
        body, name=name, grid=(t // tt,),
        in_specs=[blk, blk, blk], out_specs=[blk, blk, pl.BlockSpec((tt, LANES), lambda i: (i, 0))],
        out_shape=[jax.ShapeDtypeStruct((t, f), BF16), jax.ShapeDtypeStruct((t, f), BF16),
                   jax.ShapeDtypeStruct((t, LANES), F32)],
        compiler_params=_cparams(("parallel",)),
    )(dy, o, gate)


def _adamw(w, g, m, v, *, name):
    _, r, c = w.shape
    tr = _tile(r, 256) if r % SUBLANES == 0 else r
    c1 = 1.0 - ADAM_B1 ** ADAM_STEP
    c2 = 1.0 - ADAM_B2 ** ADAM_STEP

    def body(w_ref, g_ref, m_ref, v_ref, go_ref, d_ref, mo_ref, vo_ref):
        gv = g_ref[...]
        go_ref[...] = gv
        mn = ADAM_B1 * m_ref[...] + (1.0 - ADAM_B1) * gv
        vn = ADAM_B2 * v_ref[...] + (1.0 - ADAM_B2) * (gv * gv)
        mo_ref[...] = mn
        vo_ref[...] = vn
        d_ref[...] = -ADAM_LR * ((mn / c1) / (jnp.sqrt(vn / c2) + ADAM_EPS) + ADAM_WD * w_ref[...])

    blk = pl.BlockSpec((None, tr, c), lambda i: (0, i, 0))
    return pl.pallas_call(
        body, name=name, grid=(r // tr,), in_specs=[blk] * 4, out_specs=[blk] * 4,
        out_shape=[jax.ShapeDtypeStruct((1, r, c), F32)] * 4,
        compiler_params=_cparams(("parallel",)),
    )(w, g, m, v)


def _sum_slots(land, *, name):
    ns, r, c = land.shape
    tr = _tile(r, 64) if r % SUBLANES == 0 else r

    def body(l_ref, o_ref):
        acc = l_ref[0].astype(F32)
        for s in range(1, ns):
            acc = acc + l_ref[s].astype(F32)
        o_ref[...] = acc

    return pl.pallas_call(
        body, name=name, grid=(r // tr,),
        in_specs=[pl.BlockSpec((ns, tr, c), lambda i: (0, i, 0))],
        out_specs=pl.BlockSpec((tr, c), lambda i: (i, 0)),
        out_shape=jax.ShapeDtypeStruct((r, c), F32),
        compiler_params=_cparams(("parallel",)),
    )(land)


ANY = pl.BlockSpec(memory_space=pl.ANY)


def _flip(v, bit):
    return 1 - v if bit else v


def _gather_chips(shards, small, *, name):
    n = len(shards)
    rels = ((1, 0), (0, 1), (1, 1))

    def body(*refs):
        ins, small_in = refs[:n], refs[n]
        outs, small_out = refs[n + 1:2 * n + 1], refs[2 * n + 1]
        send, recv, loc = refs[2 * n + 2:]
        x, y, c = lax.axis_index("x"), lax.axis_index("y"), lax.axis_index("c")
        me = 2 * x + y
        sibling = (x, y, 1 - c)
        local = [pltpu.make_async_copy(ins[k], outs[k].at[me], loc.at[k]) for k in range(n)]
        local.append(pltpu.make_async_copy(small_in, small_out.at[me], loc.at[n]))
        for cp in local:
            cp.start()

        def rows(k):
            half = ins[k].shape[0] // 2
            return pl.ds(pl.multiple_of(c * half, SUBLANES), half)

        sends = []
        for r, (rx, ry) in enumerate(rels):
            to = (_flip(x, rx), _flip(y, ry), c)
            for k in range(n):
                cp = pltpu.make_async_remote_copy(
                    src_ref=ins[k].at[rows(k), :], dst_ref=outs[k].at[me, rows(k), :],
                    send_sem=send.at[r * n + k], recv_sem=recv.at[r * n + k], device_id=to, device_id_type=MESH)
                cp.start()
                sends.append(cp)
            cp = pltpu.make_async_remote_copy(
                src_ref=small_in, dst_ref=small_out.at[me], send_sem=send.at[6 * n + r], recv_sem=recv.at[6 * n + r],
                device_id=to, device_id_type=MESH)
            cp.start()
            sends.append(cp)
        for r, (rx, ry) in enumerate(rels):
            src_chip = 2 * _flip(x, rx) + _flip(y, ry)
            for k in range(n):
                landed = outs[k].at[src_chip, rows(k), :]
                sends[r * (n + 1) + k].wait_recv()
                cp = pltpu.make_async_remote_copy(
                    src_ref=landed, dst_ref=landed, send_sem=send.at[3 * n + r * n + k],
                    recv_sem=recv.at[3 * n + r * n + k], device_id=sibling, device_id_type=MESH)
                cp.start()
                sends.append(cp)
            sends[r * (n + 1) + n].wait_recv()
        for cp in sends[:3 * (n + 1)]:
            cp.wait_send()
        for cp in sends[3 * (n + 1):]:
            cp.wait()
        for cp in local:
            cp.wait()

    vmem = pl.BlockSpec(memory_space=pltpu.VMEM)
    return pl.pallas_call(
        body, name=name, in_specs=[vmem] * (n + 1), out_specs=[vmem] * (n + 1),
        out_shape=[jax.ShapeDtypeStruct((N_CHIPS,) + s.shape, s.dtype) for s in list(shards) + [small]],
        scratch_shapes=[pltpu.SemaphoreType.DMA((6 * n + 3,)), pltpu.SemaphoreType.DMA((6 * n + 3,)),
                        pltpu.SemaphoreType.DMA((n + 1,))],
        compiler_params=pltpu.CompilerParams(has_side_effects=True, vmem_limit_bytes=VMEM_LIMIT),
    )(*shards, small)


_RELS7 = tuple((r >> 2 & 1, r >> 1 & 1, r & 1) for r in range(1, N_DEV))


def _scatter_copies(ins, outs, send, recv, loc):
    n = len(ins)
    x, y, c = lax.axis_index("x"), lax.axis_index("y"), lax.axis_index("c")
    me = 4 * x + 2 * y + c

    def piece(k, px, py, pc):
        half = ins[k].shape[1] // 2
        return ins[k].at[2 * px + py, pl.ds(pc * half, half), :]

    copies = [pltpu.make_async_copy(piece(k, x, y, c), outs[k].at[me], loc.at[k]) for k in range(n)]
    for r, (rx, ry, rc) in enumerate(_RELS7):
        tx, ty, tc = _flip(x, rx), _flip(y, ry), _flip(c, rc)
        for k in range(n):
            copies.append(pltpu.make_async_remote_copy(
                src_ref=piece(k, tx, ty, tc), dst_ref=outs[k].at[me], send_sem=send.at[r * n + k],
                recv_sem=recv.at[r * n + k], device_id=(tx, ty, tc), device_id_type=MESH))
    return copies


def _scatter_shapes(grads):
    n = len(grads)
    out_shape = [jax.ShapeDtypeStruct((N_DEV, g.shape[1] // 2, g.shape[2]), g.dtype) for g in grads]
    sems = [pltpu.SemaphoreType.DMA((7 * n,)), pltpu.SemaphoreType.DMA((7 * n,)), pltpu.SemaphoreType.DMA((n,))]
    return out_shape, sems


def _join_cores(halves, *, name):
    n = len(halves)

    def body(*refs):
        ins, outs = refs[:n], refs[n:2 * n]
        send, recv, loc = refs[2 * n:]
        x, y, c = lax.axis_index("x"), lax.axis_index("y"), lax.axis_index("c")
        copies = []
        for k in range(n):
            half = ins[k].shape[0]
            mine = outs[k].at[0, pl.ds(c * half, half), :]
            cp = pltpu.make_async_copy(ins[k], mine, loc.at[k])
            cp.start()
            copies.append(cp)
            cp = pltpu.make_async_remote_copy(
                src_ref=ins[k], dst_ref=mine, send_sem=send.at[k], recv_sem=recv.at[k],
                device_id=(x, y, 1 - c), device_id_type=MESH)
            cp.start()
            copies.append(cp)
        for cp in copies:
            cp.wait()

    in_vmem = pl.BlockSpec(memory_space=pltpu.VMEM)
    return pl.pallas_call(
        body, name=name, in_specs=[in_vmem] * n, out_specs=[in_vmem] * n,
        out_shape=[jax.ShapeDtypeStruct((1, 2 * h.shape[0], h.shape[1]), h.dtype) for h in halves],
        scratch_shapes=[pltpu.SemaphoreType.DMA((n,)), pltpu.SemaphoreType.DMA((n,)), pltpu.SemaphoreType.DMA((n,))],
        compiler_params=pltpu.CompilerParams(has_side_effects=True, vmem_limit_bytes=VMEM_LIMIT),
    )(*halves)


def _allreduce_small(buf, *, name):
    r, n = buf.shape
    half = r // 2
    rels = ((1, 0), (0, 1), (1, 1))

    def body(in_ref, out_ref, sib_ref, chips_ref, send, recv):
        x, y, c = lax.axis_index("x"), lax.axis_index("y"), lax.axis_index("c")
        sibling = (x, y, 1 - c)
        chip = 2 * x + y
        rows = pl.ds(pl.multiple_of(c * half, SUBLANES), half)

        swap = pltpu.make_async_remote_copy(src_ref=in_ref, dst_ref=sib_ref, send_sem=send.at[0], recv_sem=recv.at[0],
                                            device_id=sibling, device_id_type=MESH)
        swap.start()
        swap.wait()
        chips_ref[chip] = in_ref[rows, :] + sib_ref[rows, :]

        sends = []
        for k, (rx, ry) in enumerate(rels):
            cp = pltpu.make_async_remote_copy(
                src_ref=chips_ref.at[chip], dst_ref=chips_ref.at[chip], send_sem=send.at[1 + k],
                recv_sem=recv.at[1 + k], device_id=(_flip(x, rx), _flip(y, ry), c), device_id_type=MESH)
            cp.start()
            sends.append(cp)
        for cp in sends:
            cp.wait()
        total = chips_ref[0]
        for s in range(1, N_CHIPS):
            total = total + chips_ref[s]
        out_ref[rows, :] = total

        back = pltpu.make_async_remote_copy(src_ref=out_ref.at[rows, :], dst_ref=out_ref.at[rows, :],
                                            send_sem=send.at[4], recv_sem=recv.at[4],
                                            device_id=sibling, device_id_type=MESH)
        back.start()
        back.wait()

    vmem = pl.BlockSpec(memory_space=pltpu.VMEM)
    return pl.pallas_call(
        body, name=name, in_specs=[vmem], out_specs=vmem,
        out_shape=jax.ShapeDtypeStruct((r, n), F32),
        scratch_shapes=[pltpu.VMEM((r, n), F32), pltpu.VMEM((N_CHIPS, half, n), F32),
                        pltpu.SemaphoreType.DMA((5,)), pltpu.SemaphoreType.DMA((5,))],
        compiler_params=pltpu.CompilerParams(has_side_effects=True, vmem_limit_bytes=VMEM_LIMIT),
    )(buf)


def _pack(arrs):
    flat = []
    for a in arrs:
        v = a.reshape(-1)
        pad = (-v.shape[0]) % LANES
        if pad:
            v = jnp.pad(v, (0, pad))
        flat.append(v)
    v = jnp.concatenate(flat)
    pad = (-v.shape[0]) % (LANES * SUBLANES)
    if pad:
        v = jnp.pad(v, (0, pad))
    return v.reshape(-1, LANES)


def _unpack(buf, shapes):
    v = buf.reshape(-1)
    out, off = [], 0
    for s in shapes:
        n = math.prod(s)
        out.append(v[off:off + n].reshape(s))
        off += n + (-n) % LANES
    return out


def kernel(x, norm_g, final_g, lru_w_in, lru_conv_w, lru_conv_b, lru_wa, lru_ba, lru_wx, lru_bx, lru_a_param, lru_w_out, fox_w_in, fox_b_f, fox_w_out, loss_target, m_norm_g, m_final_g, m_lru_w_in, m_lru_conv_w, m_lru_conv_b, m_lru_wa, m_lru_ba, m_lru_wx, m_lru_bx, m_lru_a_param, m_lru_w_out, m_fox_w_in, m_fox_b_f, m_fox_w_out, v_norm_g, v_final_g, v_lru_w_in, v_lru_conv_w, v_lru_conv_b, v_lru_wa, v_lru_ba, v_lru_wx, v_lru_bx, v_lru_a_param, v_lru_w_out, v_fox_w_in, v_fox_b_f, v_fox_w_out):
    t, d = x.shape[1], x.shape[2]
    w = lru_wa.shape[1] * LRU_BLOCK_W
    f = FOX_HEADS * FOX_HEAD_DIM
    npair = f // LANES
    x0 = x.reshape(t, d)
    tgt = loss_target.reshape(t, d)
    chip = 2 * lax.axis_index("x") + lax.axis_index("y")

    g_lwi, g_lwo, g_cw = _gather_chips(
        [lru_w_in[0].astype(BF16), lru_w_out[0].astype(BF16)], lru_conv_w[0], name="gather_weights")
    cg = w // 2
    lwi = jnp.concatenate([g_lwi[0], g_lwi[2], g_lwi[1], g_lwi[3]], axis=1)
    lwo = g_lwo.reshape(w, d)
    conv_w = jnp.concatenate([g_cw[s] for s in range(N_CHIPS)], axis=1)
    conv_b, ba, bx, a_param = lru_conv_b, lru_ba, lru_bx, lru_a_param
    wa, wx = lru_wa[0], lru_wx[0]
    b_f = jnp.pad(fox_b_f, ((0, 0), (0, LANES - FOX_HEADS)))

    h0 = _rmsnorm(x0, norm_g[0], name="norm0")
    u = _matmul(h0, lwi, name="lru_in")
    y1, hs, (g_fwi, g_fwo) = _lru_fwd(u, conv_w, conv_b, wa, ba, wx, bx, a_param, cg=cg, name="lru_fwd",
                                      ride=[fox_w_in[0].astype(BF16), fox_w_out[0].astype(BF16)])
    fwi = jnp.concatenate([g_fwi[s] for s in range(N_CHIPS)], axis=1)
    w_qkv, w_g2 = fwi[:, :3 * f], fwi[:, 3 * f:4 * f]
    w_f = jnp.pad(fwi[:, 4 * f:], ((0, 0), (0, LANES - FOX_HEADS)))
    fwo = g_fwo.reshape(f, d)
    x1 = _matmul(y1, lwo, add=x0, name="lru_out")
    h1 = _rmsnorm(x1, norm_g[1], name="norm1")
    qkv = _matmul(h1, w_qkv, out_dtype=BF16, name="fox_qkv")
    gate2 = _matmul(h1, w_g2, name="fox_gate")
    flog = _matmul(h1, w_f, name="fox_f")
    cum, cke = _fgate_fwd(flog, b_f, name="fgate_fwd")
    cum16 = cum[:, :FOX_HEADS]
    ckt = cum16.T.reshape(npair, 2, t)
    a_tk, a_tq = _tile(t, ATTN_TILE), _tile(t, ATTN_FWD_QUERIES)
    a_start, a_end = _attn_skip_tables(_attn_row_stats(qkv, name="attn_row_stats"), cum16, a_tk)
    a_start_fwd = jnp.min(a_start.reshape(FOX_HEADS, t // a_tq, a_tq // a_tk), axis=2)
    o, y2, lse = _attn_fwd(a_start_fwd, qkv, ckt, gate2, name="attn_fwd", tq=a_tq, tk=a_tk)
    x2 = _matmul(y2, fwo, add=x1, name="fox_out")
    lsum, dx2, dx2_b, dgf = _final_loss(x2, tgt, final_g, name="final_loss")
    loss = lax.psum(0.5 * jnp.sum(lsum) / d, ("x", "y", "c"))

    d_fwo = _matmul(y2, dx2_b, ta=True, out_dtype=BF16, name="d_fox_w_out")
    dy2 = _matmul(dx2_b, fwo, tb=True, name="d_y2")
    do, dgate2, dl = _fox_post_bwd(dy2, o, gate2, name="fox_post_bwd")
    lt = lse
    dt = dl[:, :FOX_HEADS].T.reshape(npair, 2, t)
    dq, dk, dv, dck, dcq = _attn_bwd(a_end, qkv, do, lt, dt, cke, name="attn_bwd")
    dcum = jnp.pad((dck + dcq).reshape(FOX_HEADS, t).T, ((0, 0), (0, LANES - FOX_HEADS)))
    dflog, db_f = _fgate_bwd(dcum, flog, b_f, name="fgate_bwd")
    du2 = [dq, dk, dv, dgate2]
    dflog_b = dflog.astype(BF16)
    dh1 = _matmul_kparts(du2, fwi[:, :4 * f], chunk=f, name="d_h1_a")
    dh1 = _matmul(dflog_b, w_f, tb=True, add=dh1, name="d_h1_b")
    d_fwi_a = _matmul_nparts(h1, du2, chunk=f, out_dtype=BF16, name="d_fox_w_in_a")
    d_fwi_b = _matmul(h1, dflog_b, ta=True, out_dtype=BF16, name="d_fox_w_in_b")
    d_fwi = jnp.concatenate([d_fwi_a, d_fwi_b[:, :FOX_HEADS]], axis=1)
    dx1, dx1_b, dg1 = _rmsnorm_bwd(dh1, x1, norm_g[1], dx2, name="norm1_bwd", bf16_copy=True)

    d_lwo = _matmul(y1, dx1_b, ta=True, out_dtype=BF16, name="d_lru_w_out")
    dy1 = _matmul(dx1_b, lwo, tb=True, name="d_y1")
    n_fwi = fox_w_in.shape[2]
    g_fwi4 = jnp.stack([d_fwi[:, s * n_fwi:(s + 1) * n_fwi] for s in range(N_CHIPS)])
    g_fwo4 = d_fwo.reshape(N_CHIPS, f // N_CHIPS, d)
    g_lwo4 = d_lwo.reshape(N_CHIPS, w // N_CHIPS, d)
    (dxb, dgate, d_cw, d_cb, d_wa, d_ba, d_wx, d_bx, d_ap), lands_early = _lru_bwd(
        u, hs, dy1, conv_w, conv_b, wa, ba, wx, bx, a_param, cg=cg, name="lru_bwd", ride=[g_lwo4, g_fwi4, g_fwo4])
    d_lwi_p = _matmul_nparts(h0, [dxb, dgate], chunk=cg, out_dtype=BF16, name="d_lru_w_in")
    csz = cg
    g_lwi4 = jnp.stack([d_lwi_p[:, 0:csz], d_lwi_p[:, 2 * csz:3 * csz], d_lwi_p[:, csz:2 * csz],
                        d_lwi_p[:, 3 * csz:]])
    dh0, lands_last = _matmul_kparts([dxb, dgate], lwi, chunk=cg, name="d_h0", ride=[g_lwi4])
    dx0, _, dg0 = _rmsnorm_bwd(dh0, x0, norm_g[0], dx1, name="norm0_bwd", bf16_copy=False)
    lands = list(lands_last) + list(lands_early)
    halves = [_sum_slots(l, name="sum_" + nm) for l, nm in zip(lands, ("lru_w_in", "lru_w_out", "fox_w_in", "fox_w_out"))]
    big_g = _join_cores(halves, name="join_cores")

    small_g = [jnp.concatenate([dg0, dg1], axis=0), dgf.reshape(d), d_cw, d_cb, d_wa, d_ba, d_wx, d_bx, d_ap,
               db_f[:, :FOX_HEADS]]
    gsum = _allreduce_small(_pack(small_g), name="allreduce_small")
    zc = jnp.zeros((CONV_WIDTH, w), F32)
    pk_w = _pack([norm_g, final_g, zc, lru_conv_b, lru_wa, lru_ba, lru_wx, lru_bx, lru_a_param, fox_b_f])
    pk_m = _pack([m_norm_g, m_final_g, zc, m_lru_conv_b, m_lru_wa, m_lru_ba, m_lru_wx, m_lru_bx, m_lru_a_param,
                  m_fox_b_f])
    pk_v = _pack([v_norm_g, v_final_g, zc + 1.0, v_lru_conv_b, v_lru_wa, v_lru_ba, v_lru_wx, v_lru_bx,
                  v_lru_a_param, v_fox_b_f])
    s_g, s_delta, s_m, s_v = _adamw(pk_w[None], gsum[None], pk_m[None], pk_v[None], name="adamw_small")
    out_shapes = [norm_g.shape, final_g.shape, (CONV_WIDTH, w), lru_conv_b.shape, lru_wa.shape, lru_ba.shape,
                  lru_wx.shape, lru_bx.shape, lru_a_param.shape, fox_b_f.shape]
    sg = _unpack(s_g, out_shapes)
    sd = _unpack(s_delta, out_shapes)
    sm = _unpack(s_m, out_shapes)
    sv = _unpack(s_v, out_shapes)

    ncw = lru_conv_w.shape[2]
    g_cw_loc = lax.dynamic_slice_in_dim(sg[2], chip * ncw, ncw, axis=1)
    g_cw_loc, cw_d, cw_m, cw_v = _adamw(lru_conv_w, g_cw_loc[None], m_lru_conv_w, v_lru_conv_w, name="adamw_conv_w")

    big = []
    for nm, wt, g, mm, vv in (("lru_w_in", lru_w_in, big_g[0], m_lru_w_in, v_lru_w_in),
                              ("lru_w_out", lru_w_out, big_g[1], m_lru_w_out, v_lru_w_out),
                              ("fox_w_in", fox_w_in, big_g[2], m_fox_w_in, v_fox_w_in),
                              ("fox_w_out", fox_w_out, big_g[3], m_fox_w_out, v_fox_w_out)):
        big.append(tuple(_adamw(wt, g, mm, vv, name="adamw_" + nm)))

    def assemble(idx):
        small = (sg, sd, sm, sv)[idx]
        cw = (g_cw_loc, cw_d, cw_m, cw_v)[idx]
        return [small[0], small[1], big[0][idx], cw, small[3], small[4], small[5], small[6], small[7], small[8],
                big[1][idx], big[2][idx], small[9], big[3][idx]]

    grad_x = dx0.reshape(1, t, d)
    return (loss, grad_x, *assemble(0), *assemble(1), *assemble(2), *assemble(3))
```

```python
import math

import jax
import jax.numpy as jnp
from jax import lax
from jax.experimental import pallas as pl
from jax.experimental.pallas import tpu as pltpu

F32 = jnp.float32
BF16 = jnp.bfloat16

EPS = 1e-6
LRU_C = 8.0
LRU_BLOCK_W = 128
CONV_WIDTH = 4
FOX_HEADS = 16
FOX_HEAD_DIM = 64
NEG_INF = -1e30
ADAM_LR = 0.001
ADAM_B1 = 0.9
ADAM_B2 = 0.999
ADAM_EPS = 1e-08
ADAM_WD = 0.01
ADAM_STEP = 10

LANES = 128
SUBLANES = 8
VMEM_LIMIT = 56 * 1024 * 1024
TINY = 1e-30
N_CHIPS = 4
N_DEV = 8
MESH = pl.DeviceIdType.MESH


def _tile(n, pref):
    t = min(n, pref)
    while n % t:
        t //= 2
    return t


def _cparams(dims=None):
    return pltpu.CompilerParams(dimension_semantics=dims, vmem_limit_bytes=VMEM_LIMIT)


def _sigmoid(x):
    return 0.5 * jnp.tanh(0.5 * x) + 0.5


def _log1p(x):
    u = 1.0 + x
    return jnp.where(u == 1.0, x, jnp.log(u) * x / (u - 1.0))


def _bf16_pieces(x):
    hi = x.astype(BF16)
    rest = x - hi.astype(F32)
    mid = rest.astype(BF16)
    return hi, mid, (rest - mid.astype(F32)).astype(BF16)


def _dot_01_left(m01, x):
    return sum(jnp.dot(m01, p, preferred_element_type=F32) for p in _bf16_pieces(x))


def _dot_01_right(x, m01):
    return sum(jnp.dot(p, m01, preferred_element_type=F32) for p in _bf16_pieces(x))


def _softplus(x):
    return jnp.maximum(x, 0.0) + _log1p(jnp.exp(-jnp.abs(x)))


MM_TILE = 1024
MM_FULL_K = 1536


def _matmul(a, b, *, name, ta=False, tb=False, out_dtype=F32, add=None, tm=MM_TILE, tn=MM_TILE, tk=None):
    if ta:
        kdim, m = a.shape
    else:
        m, kdim = a.shape
    if tb:
        n, kb = b.shape
    else:
        kb, n = b.shape
    assert kdim == kb, (a.shape, b.shape, ta, tb)
    if tk is None:
        tk = kdim if kdim <= MM_FULL_K else MM_TILE
    tm, tn, tk = _tile(m, tm), _tile(n, tn), _tile(kdim, tk)
    nk = kdim // tk
    dn = (((0 if ta else 1,), (1 if tb else 0,)), ((), ()))
    has_add = add is not None

    def body(*refs):
        if has_add:
            a_ref, b_ref, add_ref, o_ref = refs[:4]
        else:
            a_ref, b_ref, o_ref = refs[:3]
        part = lax.dot_general(a_ref[...].astype(BF16), b_ref[...].astype(BF16), dn, preferred_element_type=F32)

        def finish(r):
            if has_add:
                r = r + add_ref[...].astype(F32)
            o_ref[...] = r.astype(o_ref.dtype)

        if nk == 1:
            finish(part)
            return
        acc_ref = refs[-1]
        k = pl.program_id(2)

        @pl.when(k == 0)
        def _():
            acc_ref[...] = part

        @pl.when(k > 0)
        def _():
            acc_ref[...] += part

        @pl.when(k == nk - 1)
        def _():
            finish(acc_ref[...])

    a_spec = pl.BlockSpec((tk, tm), lambda i, j, k: (k, i)) if ta else pl.BlockSpec((tm, tk), lambda i, j, k: (i, k))
    b_spec = pl.BlockSpec((tn, tk), lambda i, j, k: (j, k)) if tb else pl.BlockSpec((tk, tn), lambda i, j, k: (k, j))
    o_spec = pl.BlockSpec((tm, tn), lambda i, j, k: (i, j))
    in_specs = [a_spec, b_spec] + ([o_spec] if has_add else [])
    args = (a, b) + ((add,) if has_add else ())
    return pl.pallas_call(
        body, name=name, grid=(m // tm, n // tn, nk), in_specs=in_specs, out_specs=o_spec,
        out_shape=jax.ShapeDtypeStruct((m, n), out_dtype),
        scratch_shapes=[pltpu.VMEM((tm, tn), F32)] if nk > 1 else [],
        compiler_params=_cparams(("parallel", "parallel", "arbitrary")),
    )(*args)


def _matmul_kparts(parts, b, *, chunk, name, tm=MM_TILE, tn=MM_TILE, ride=()):
    npart = len(parts)
    nride = len(ride)
    m = parts[0].shape[0]
    n, kdim = b.shape
    nk = kdim // chunk
    assert nk * chunk == kdim and sum(p.shape[1] for p in parts) == kdim and nk % npart == 0
    tm, tn = _tile(m, tm), _tile(n, tn)
    dn = (((1,), (1,)), ((), ()))
    steps = (m // tm) * (n // tn) * nk

    def body(*refs):
        ins, rest = refs[:npart + 1], refs[npart + 1:]
        ride_in, rest = rest[:nride], rest[nride:]
        o_ref, rest = rest[0], rest[1:]
        ride_out, rest = rest[:nride], rest[nride:]
        acc_ref, sems = rest[0], rest[1:]
        if not nride:
            core(*ins, o_ref, acc_ref)
            return
        step = (pl.program_id(0) * (n // tn) + pl.program_id(1)) * nk + pl.program_id(2)

        @pl.when(step == 0)
        def _():
            for cp in _scatter_copies(ride_in, ride_out, *sems):
                cp.start()

        core(*ins, o_ref, acc_ref)

        @pl.when(step == steps - 1)
        def _():
            for cp in _scatter_copies(ride_in, ride_out, *sems):
                cp.wait()

    def core(*refs):
        a_refs, b_ref, o_ref, acc_ref = refs[:npart], refs[npart], refs[npart + 1], refs[npart + 2]
        k = pl.program_id(2)

        @pl.when(k == 0)
        def _():
            acc_ref[...] = jnp.zeros_like(acc_ref)

        for s in range(npart):
            @pl.when(lax.rem(k, npart) == s)
            def _(s=s):
                acc_ref[...] += lax.dot_general(a_refs[s][...].astype(BF16), b_ref[...].astype(BF16), dn,
                                                preferred_element_type=F32)

        @pl.when(k == nk - 1)
        def _():
            o_ref[...] = acc_ref[...].astype(o_ref.dtype)

    a_specs = [pl.BlockSpec((tm, chunk), lambda i, j, k: (i, k // npart)) for _ in range(npart)]
    any_spec = pl.BlockSpec(memory_space=pl.ANY)
    ride_shape, ride_sems = _scatter_shapes(ride) if nride else ([], [])
    outs = pl.pallas_call(
        body, name=name, grid=(m // tm, n // tn, nk),
        in_specs=a_specs + [pl.BlockSpec((tn, chunk), lambda i, j, k: (j, k))] + [any_spec] * nride,
        out_specs=[pl.BlockSpec((tm, tn), lambda i, j, k: (i, j))] + [any_spec] * nride,
        out_shape=[jax.ShapeDtypeStruct((m, n), F32)] + ride_shape,
        scratch_shapes=[pltpu.VMEM((tm, tn), F32)] + ride_sems,
        compiler_params=_cparams(("arbitrary",) * 3 if nride else ("parallel", "parallel", "arbitrary")),
    )(*parts, b, *ride)
    return (outs[0], outs[1:]) if nride else outs[0]


def _matmul_nparts(a, parts, *, chunk, out_dtype, name, tm=MM_TILE, tk=MM_TILE):
    npart = len(parts)
    t, m = a.shape
    n = sum(p.shape[1] for p in parts)
    nj = n // chunk
    assert nj * chunk == n and nj % npart == 0
    tm, tk = _tile(m, tm), _tile(t, tk)
    nk = t // tk
    dn = (((0,), (0,)), ((), ()))

    def body(*refs):
        a_ref, b_refs, o_ref, acc_ref = refs[0], refs[1:1 + npart], refs[1 + npart], refs[2 + npart]
        j, k = pl.program_id(1), pl.program_id(2)

        @pl.when(k == 0)
        def _():
            acc_ref[...] = jnp.zeros_like(acc_ref)

        for s in range(npart):
            @pl.when(lax.rem(j, npart) == s)
            def _(s=s):
                acc_ref[...] += lax.dot_general(a_ref[...].astype(BF16), b_refs[s][...].astype(BF16), dn,
                                                preferred_element_type=F32)

        @pl.when(k == nk - 1)
        def _():
            o_ref[...] = acc_ref[...].astype(o_ref.dtype)

    def b_spec(s):
        return pl.BlockSpec((tk, chunk), lambda i, j, k: (jnp.where(lax.rem(j, npart) == s, k, 0), j // npart))

    return pl.pallas_call(
        body, name=name, grid=(m // tm, nj, nk),
        in_specs=[pl.BlockSpec((tk, tm), lambda i, j, k: (k, i))] + [b_spec(s) for s in range(npart)],
        out_specs=pl.BlockSpec((tm, chunk), lambda i, j, k: (i, j)),
        out_shape=jax.ShapeDtypeStruct((m, n), out_dtype),
        scratch_shapes=[pltpu.VMEM((tm, chunk), F32)],
        compiler_params=_cparams(("parallel", "parallel", "arbitrary")),
    )(a, *parts)


def _rmsnorm(x, g, *, name):
    t, d = x.shape
    tt = _tile(t, 512)

    def body(x_ref, g_ref, o_ref):
        xf = x_ref[...]
        rstd = lax.rsqrt(jnp.mean(xf * xf, axis=-1, keepdims=True) + EPS)
        o_ref[...] = (xf * rstd * g_ref[...]).astype(o_ref.dtype)

    return pl.pallas_call(
        body, name=name, grid=(t // tt,),
        in_specs=[pl.BlockSpec((tt, d), lambda i: (i, 0)), pl.BlockSpec((1, d), lambda i: (0, 0))],
        out_specs=pl.BlockSpec((tt, d), lambda i: (i, 0)),
        out_shape=jax.ShapeDtypeStruct((t, d), BF16),
        compiler_params=_cparams(("parallel",)),
    )(x, g.reshape(1, d))


def _rmsnorm_bwd(dh, x, g, dres, *, name, bf16_copy):
    t, d = x.shape
    tt = _tile(t, 512)
    nt = t // tt

    def body(dh_ref, x_ref, g_ref, dres_ref, dx_ref, *out_refs):
        dg_ref = out_refs[-1]
        i = pl.program_id(0)

        @pl.when(i == 0)
        def _():
            dg_ref[...] = jnp.zeros_like(dg_ref)

        xf = x_ref[...]
        rstd = lax.rsqrt(jnp.mean(xf * xf, axis=-1, keepdims=True) + EPS)
        xhat = xf * rstd
        dhf = dh_ref[...].astype(F32)
        dxhat = dhf * g_ref[...]
        mt = jnp.mean(dxhat * xhat, axis=-1, keepdims=True)
        dx = dres_ref[...] + rstd * (dxhat - xhat * mt)
        dx_ref[...] = dx
        if bf16_copy:
            out_refs[0][...] = dx.astype(BF16)
        dg_ref[...] += jnp.sum(dhf * xhat, axis=0, keepdims=True)

    blk = pl.BlockSpec((tt, d), lambda i: (i, 0))
    vec = pl.BlockSpec((1, d), lambda i: (0, 0))
    low = [jax.ShapeDtypeStruct((t, d), BF16)] if bf16_copy else []
    outs = pl.pallas_call(
        body, name=name, grid=(nt,),
        in_specs=[blk, blk, vec, blk], out_specs=[blk] + [blk] * len(low) + [vec],
        out_shape=[jax.ShapeDtypeStruct((t, d), F32)] + low + [jax.ShapeDtypeStruct((1, d), F32)],
        compiler_params=_cparams(("arbitrary",)),
    )(dh, x, g.reshape(1, d), dres)
    return outs[0], (outs[1] if bf16_copy else None), outs[-1]


def _final_loss(x2, tgt, g, *, name):
    t, d = x2.shape
    tt = _tile(t, 512)

    def body(x_ref, t_ref, g_ref, l_ref, dx_ref, dxb_ref, dg_ref):
        i = pl.program_id(0)

        @pl.when(i == 0)
        def _():
            dg_ref[...] = jnp.zeros_like(dg_ref)
            l_ref[...] = jnp.zeros_like(l_ref)

        xf = x_ref[...]
        gg = g_ref[...]
        rstd = lax.rsqrt(jnp.mean(xf * xf, axis=-1, keepdims=True) + EPS)
        xhat = xf * rstd
        err = xhat * gg - t_ref[...]
        l_ref[...] += jnp.sum(err * err, axis=0, keepdims=True)
        dy = err * (1.0 / d)
        dxhat = dy * gg
        mt = jnp.mean(dxhat * xhat, axis=-1, keepdims=True)
        dx = rstd * (dxhat - xhat * mt)
        dx_ref[...] = dx
        dxb_ref[...] = dx.astype(dxb_ref.dtype)
        dg_ref[...] += jnp.sum(dy * xhat, axis=0, keepdims=True)

    blk = pl.BlockSpec((tt, d), lambda i: (i, 0))
    vec = pl.BlockSpec((1, d), lambda i: (0, 0))
    return pl.pallas_call(
        body, name=name, grid=(t // tt,),
        in_specs=[blk, blk, vec], out_specs=[vec, blk, blk, vec],
        out_shape=[jax.ShapeDtypeStruct((1, d), F32), jax.ShapeDtypeStruct((t, d), F32),
                   jax.ShapeDtypeStruct((t, d), BF16), jax.ShapeDtypeStruct((1, d), F32)],
        compiler_params=_cparams(("arbitrary",)),
    )(x2, tgt, g.reshape(1, d))


def _shift_down(prev8, cur, s):
    ext = jnp.concatenate([prev8, cur], axis=0)
    if s == 0:
        return cur
    return pltpu.roll(ext, s, 0)[SUBLANES:, :]


def _shift_up(cur, next8, s):
    if s == 0:
        return cur
    n = cur.shape[0]
    ext = jnp.concatenate([cur, next8], axis=0)
    return pltpu.roll(ext, n + SUBLANES - s, 0)[:n, :]


def _lru_gates(xc, wa, ba, wx, bx, sp):
    xcb = xc.astype(BF16)
    r = _sigmoid(jnp.dot(xcb, wa, preferred_element_type=F32) + ba)
    ig = _sigmoid(jnp.dot(xcb, wx, preferred_element_type=F32) + bx)
    log_a = -LRU_C * r * sp
    a = jnp.exp(log_a)
    z = -jnp.tanh(log_a) * (a * a + 1.0)
    inv_mult = lax.rsqrt(jnp.maximum(z, TINY))
    return r, ig, a, z * inv_mult, inv_mult


def _lru_specs(tt, cg, n_groups, nt, reverse):
    ncol = cg // LANES
    if reverse:
        ti = lambda i: nt - 1 - i
    else:
        ti = lambda i: i
    hb = tt // SUBLANES
    cur = lambda col: pl.BlockSpec((tt, cg), lambda g, i: (ti(i), 2 * g + col))
    prev = lambda col: pl.BlockSpec((SUBLANES, cg), lambda g, i: (jnp.maximum(ti(i) * hb - 1, 0), 2 * g + col))
    chan = lambda rows: pl.BlockSpec((rows, cg), lambda g, i: (0, g))
    wblk = pl.BlockSpec((ncol, LRU_BLOCK_W, LRU_BLOCK_W), lambda g, i: (g, 0, 0))
    plain = pl.BlockSpec((tt, cg), lambda g, i: (ti(i), g))
    plain_prev = pl.BlockSpec((SUBLANES, cg), lambda g, i: (jnp.maximum(ti(i) * hb - 1, 0), g))
    return cur, prev, chan, wblk, plain, plain_prev


def _chip_gather_copies(ins, outs, send, recv, loc):
    n = len(ins)
    x, y, c = lax.axis_index("x"), lax.axis_index("y"), lax.axis_index("c")
    me = 2 * x + y
    copies = [pltpu.make_async_copy(ins[k], outs[k].at[me], loc.at[k]) for k in range(n)]
    for r, (rx, ry) in enumerate(((1, 0), (0, 1), (1, 1))):
        for k in range(n):
            copies.append(pltpu.make_async_remote_copy(
                src_ref=ins[k], dst_ref=outs[k].at[me], send_sem=send.at[r * n + k], recv_sem=recv.at[r * n + k],
                device_id=(_flip(x, rx), _flip(y, ry), c), device_id_type=MESH))
    return copies


def _lru_fwd(u, conv_w, conv_b, wa, ba, wx, bx, a_param, *, cg, name, ride=()):
    nride = len(ride)
    t, w2 = u.shape
    w = w2 // 2
    n_groups = w // cg
    ncol = cg // LANES
    tt = _tile(t, 256)
    nt = t // tt
    cur, prev, chan, wblk, plain, _ = _lru_specs(tt, cg, n_groups, nt, False)

    def body(*refs):
        n_in, n_out, n_scr = 10, 2, 3
        ins, rest = refs[:n_in], refs[n_in:]
        ride_in, rest = rest[:nride], rest[nride:]
        outs, rest = rest[:n_out], rest[n_out:]
        ride_out, rest = rest[:nride], rest[nride:]
        scr, sems = rest[:n_scr], rest[n_scr:]
        if not nride:
            core(*ins, *outs, *scr)
            return
        step = pl.program_id(0) * nt + pl.program_id(1)

        @pl.when(step == 0)
        def _():
            for cp in _chip_gather_copies(ride_in, ride_out, *sems):
                cp.start()

        core(*ins, *outs, *scr)

        @pl.when(step == n_groups * nt - 1)
        def _():
            for cp in _chip_gather_copies(ride_in, ride_out, *sems):
                cp.wait()

    def core(xb_ref, xp_ref, gate_ref, cw_ref, cb_ref, wa_ref, ba_ref, wx_ref, bx_ref, ap_ref,
             y_ref, hs_ref, h_ref, a_s, b_s):
        i = pl.program_id(1)

        @pl.when(i == 0)
        def _():
            h_ref[...] = jnp.zeros_like(h_ref)

        keep = (i > 0).astype(F32)
        for n in range(ncol):
            sl = slice(n * LANES, (n + 1) * LANES)
            xb = xb_ref[:, sl]
            xp = xp_ref[:, sl] * keep
            xc = cb_ref[:, sl] + cw_ref[3:4, sl] * xb
            for s in range(1, CONV_WIDTH):
                xc = xc + cw_ref[3 - s:4 - s, sl] * _shift_down(xp, xb, s)
            sp = _softplus(-ap_ref[:, sl])
            _, ig, a, mult, _ = _lru_gates(xc, wa_ref[n].astype(BF16), ba_ref[:, sl],
                                           wx_ref[n].astype(BF16), bx_ref[:, sl], sp)
            a_s[:, sl] = a
            b_s[:, sl] = mult * (ig * xc)

        def step(g, h):
            base = pl.multiple_of(g * SUBLANES, SUBLANES)
            for r in range(SUBLANES):
                h = a_s[pl.ds(base + r, 1), :] * h + b_s[pl.ds(base + r, 1), :]
                hs_ref[pl.ds(base + r, 1), :] = h
            return h

        h = lax.fori_loop(0, tt // SUBLANES, step, h_ref[0:1, :])
        h_ref[0:1, :] = h
        gate = gate_ref[...]
        y_ref[...] = (hs_ref[...] * (gate * _sigmoid(gate))).astype(y_ref.dtype)

    any_spec = pl.BlockSpec(memory_space=pl.ANY)
    ride_sems = [pltpu.SemaphoreType.DMA((3 * nride,)), pltpu.SemaphoreType.DMA((3 * nride,)),
                 pltpu.SemaphoreType.DMA((nride,))] if nride else []
    outs = pl.pallas_call(
        body, name=name, grid=(n_groups, nt),
        in_specs=[cur(0), prev(0), cur(1), chan(CONV_WIDTH), chan(1), wblk, chan(1), wblk, chan(1), chan(1)]
        + [any_spec] * nride,
        out_specs=[plain, plain] + [any_spec] * nride,
        out_shape=[jax.ShapeDtypeStruct((t, w), BF16), jax.ShapeDtypeStruct((t, w), F32)]
        + [jax.ShapeDtypeStruct((N_CHIPS,) + r.shape, r.dtype) for r in ride],
        scratch_shapes=[pltpu.VMEM((SUBLANES, cg), F32), pltpu.VMEM((tt, cg), F32), pltpu.VMEM((tt, cg), F32)]
        + ride_sems,
        compiler_params=_cparams(("arbitrary", "arbitrary")),
    )(u, u, u, conv_w, conv_b, wa, ba, wx, bx, a_param, *ride)
    return outs[0], outs[1], outs[2:]


def _lru_bwd(u, hs, dy, conv_w, conv_b, wa, ba, wx, bx, a_param, *, cg, name, ride=()):
    nride = len(ride)
    t, w2 = u.shape
    w = w2 // 2
    n_groups = w // cg
    ncol = cg // LANES
    tt = _tile(t, 256)
    nt = t // tt
    cur, prev, chan, wblk, plain, plain_prev = _lru_specs(tt, cg, n_groups, nt, True)
    tn_dims = (((0,), (0,)), ((), ()))
    nt_dims = (((1,), (1,)), ((), ()))

    def body(*refs):
        n_in, n_out, n_scr = 13, 9, 5
        ins, rest = refs[:n_in], refs[n_in:]
        ride_in, rest = rest[:nride], rest[nride:]
        outs, rest = rest[:n_out], rest[n_out:]
        ride_out, rest = rest[:nride], rest[nride:]
        scr, sems = rest[:n_scr], rest[n_scr:]
        if not nride:
            core(*ins, *outs, *scr)
            return
        step = pl.program_id(0) * nt + pl.program_id(1)

        @pl.when(step == 0)
        def _():
            for cp in _scatter_copies(ride_in, ride_out, *sems):
                cp.start()

        core(*ins, *outs, *scr)

        @pl.when(step == n_groups * nt - 1)
        def _():
            for cp in _scatter_copies(ride_in, ride_out, *sems):
                cp.wait()

    def core(xb_ref, xp_ref, gate_ref, hs_ref, hp_ref, dy_ref, cw_ref, cb_ref, wa_ref, ba_ref, wx_ref, bx_ref,
             ap_ref, dxb_ref, dgate_ref, dcw_ref, dcb_ref, dwa_ref, dba_ref, dwx_ref, dbx_ref, dsp_ref,
             c_ref, nx_ref, a_s, dhs_s, lam_s):
        i = pl.program_id(1)
        first_time_block = i == nt - 1

        @pl.when(i == 0)
        def _():
            c_ref[...] = jnp.zeros_like(c_ref)
            nx_ref[...] = jnp.zeros_like(nx_ref)
            for r in (dcw_ref, dcb_ref, dwa_ref, dba_ref, dwx_ref, dbx_ref, dsp_ref):
                r[...] = jnp.zeros_like(r)

        keep = jnp.where(first_time_block, 0.0, 1.0).astype(F32)
        gate = gate_ref[...]
        sg = _sigmoid(gate)
        dyv = dy_ref[...]
        hsv = hs_ref[...]
        dhs_s[...] = dyv * (gate * sg)
        dgate_ref[...] = (dyv * hsv * (sg * (1.0 + gate * (1.0 - sg)))).astype(dgate_ref.dtype)

        saved = []
        for n in range(ncol):
            sl = slice(n * LANES, (n + 1) * LANES)
            xb = xb_ref[:, sl]
            xp = xp_ref[:, sl] * keep
            shifted = [xb] + [_shift_down(xp, xb, s) for s in range(1, CONV_WIDTH)]
            xc = cb_ref[:, sl] + cw_ref[3:4, sl] * xb
            for s in range(1, CONV_WIDTH):
                xc = xc + cw_ref[3 - s:4 - s, sl] * shifted[s]
            sp = _softplus(-ap_ref[:, sl])
            wab = wa_ref[n].astype(BF16)
            wxb = wx_ref[n].astype(BF16)
            r, ig, a, mult, inv_mult = _lru_gates(xc, wab, ba_ref[:, sl], wxb, bx_ref[:, sl], sp)
            a_s[:, sl] = a
            saved.append((sl, shifted, xc, sp, wab, wxb, r, ig, a, mult, inv_mult))

        def step(g, c):
            base = pl.multiple_of(tt - SUBLANES - g * SUBLANES, SUBLANES)
            for r in range(SUBLANES - 1, -1, -1):
                lam = dhs_s[pl.ds(base + r, 1), :] + c
                lam_s[pl.ds(base + r, 1), :] = lam
                c = a_s[pl.ds(base + r, 1), :] * lam
            return c

        c_ref[0:1, :] = lax.fori_loop(0, tt // SUBLANES, step, c_ref[0:1, :])

        for n in range(ncol):
            sl, shifted, xc, sp, wab, wxb, r, ig, a, mult, inv_mult = saved[n]
            lam = lam_s[:, sl]
            hprev = _shift_down(hp_ref[:, sl] * keep, hs_ref[:, sl], 1)
            da = lam * hprev
            dmult = lam * (ig * xc)
            dlog_a = da * a - dmult * (a * a * inv_mult)
            di = lam * (mult * xc)
            dxc = lam * (mult * ig)
            dr = dlog_a * (-LRU_C * sp)
            dsp_ref[:, sl] += jnp.sum(dlog_a * (-LRU_C * r), axis=0, keepdims=True)
            dza = dr * (r * (1.0 - r))
            dzx = di * (ig * (1.0 - ig))
            dba_ref[:, sl] += jnp.sum(dza, axis=0, keepdims=True)
            dbx_ref[:, sl] += jnp.sum(dzx, axis=0, keepdims=True)
            xcb = xc.astype(BF16)
            dzab = dza.astype(BF16)
            dzxb = dzx.astype(BF16)
            dwa_ref[n] += lax.dot_general(xcb, dzab, tn_dims, preferred_element_type=F32)
            dwx_ref[n] += lax.dot_general(xcb, dzxb, tn_dims, preferred_element_type=F32)
            dxc = dxc + lax.dot_general(dzab, wab, nt_dims, preferred_element_type=F32)
            dxc = dxc + lax.dot_general(dzxb, wxb, nt_dims, preferred_element_type=F32)
            dcb_ref[:, sl] += jnp.sum(dxc, axis=0, keepdims=True)
            for s in range(CONV_WIDTH):
                dcw_ref[3 - s:4 - s, sl] += jnp.sum(dxc * shifted[s], axis=0, keepdims=True)
            nx = nx_ref[:, sl]
            dxb = cw_ref[3:4, sl] * dxc
            for s in range(1, CONV_WIDTH):
                dxb = dxb + cw_ref[3 - s:4 - s, sl] * _shift_up(dxc, nx, s)
            dxb_ref[:, sl] = dxb.astype(dxb_ref.dtype)
            nx_ref[:, sl] = dxc[0:SUBLANES, :]

        @pl.when(first_time_block)
        def _():
            dsp_ref[...] = dsp_ref[...] * (-_sigmoid(-ap_ref[...]))

    dxb_spec = pl.BlockSpec((tt, cg), lambda g, i: (nt - 1 - i, g))
    any_spec = pl.BlockSpec(memory_space=pl.ANY)
    ride_shape, ride_sems = _scatter_shapes(ride) if nride else ([], [])
    outs = pl.pallas_call(
        body, name=name, grid=(n_groups, nt),
        in_specs=[cur(0), prev(0), cur(1), plain, plain_prev, plain, chan(CONV_WIDTH), chan(1), wblk, chan(1), wblk,
                  chan(1), chan(1)] + [any_spec] * nride,
        out_specs=[dxb_spec, dxb_spec, chan(CONV_WIDTH), chan(1), wblk, chan(1), wblk, chan(1), chan(1)]
        + [any_spec] * nride,
        out_shape=[jax.ShapeDtypeStruct((t, w), BF16), jax.ShapeDtypeStruct((t, w), BF16),
                   jax.ShapeDtypeStruct(conv_w.shape, F32), jax.ShapeDtypeStruct(conv_b.shape, F32),
                   jax.ShapeDtypeStruct(wa.shape, F32), jax.ShapeDtypeStruct(ba.shape, F32),
                   jax.ShapeDtypeStruct(wx.shape, F32), jax.ShapeDtypeStruct(bx.shape, F32),
                   jax.ShapeDtypeStruct(a_param.shape, F32)] + ride_shape,
        scratch_shapes=[pltpu.VMEM((SUBLANES, cg), F32), pltpu.VMEM((SUBLANES, cg), F32),
                        pltpu.VMEM((tt, cg), F32), pltpu.VMEM((tt, cg), F32), pltpu.VMEM((tt, cg), F32)] + ride_sems,
        compiler_params=_cparams(("arbitrary", "arbitrary")),
    )(u, u, u, hs, hs, dy, conv_w, conv_b, wa, ba, wx, bx, a_param, *ride)
    return outs[:9], outs[9:]


def _fgate_fwd(f, b_f, *, name):
    t, n = f.shape
    tt = _tile(t, 256)
    width = FOX_HEADS * FOX_HEAD_DIM

    def body(f_ref, b_ref, cum_ref, wide_ref, carry_ref):
        i = pl.program_id(0)

        @pl.when(i == 0)
        def _():
            carry_ref[...] = jnp.zeros_like(carry_ref)

        z = f_ref[...] + b_ref[...]
        lf = jnp.minimum(z, 0.0) - _log1p(jnp.exp(-jnp.abs(z)))
        row = lax.broadcasted_iota(jnp.int32, (tt, tt), 0)
        col = lax.broadcasted_iota(jnp.int32, (tt, tt), 1)
        tri = (col <= row).astype(BF16)
        cum = _dot_01_left(tri, lf) + carry_ref[0:1, :]
        cum_ref[...] = cum
        carry_ref[0:1, :] = cum[tt - 1:tt, :]
        head = lax.broadcasted_iota(jnp.int32, (n, width), 0)
        chan = lax.broadcasted_iota(jnp.int32, (n, width), 1) // FOX_HEAD_DIM
        wide_ref[...] = _dot_01_right(cum, (head == chan).astype(BF16))

    return pl.pallas_call(
        body, name=name, grid=(t // tt,),
        in_specs=[pl.BlockSpec((tt, n), lambda i: (i, 0)), pl.BlockSpec((1, n), lambda i: (0, 0))],
        out_specs=[pl.BlockSpec((tt, n), lambda i: (i, 0)), pl.BlockSpec((tt, width), lambda i: (i, 0))],
        out_shape=[jax.ShapeDtypeStruct((t, n), F32), jax.ShapeDtypeStruct((t, width), F32)],
        scratch_shapes=[pltpu.VMEM((SUBLANES, n), F32)],
        compiler_params=_cparams(("arbitrary",)),
    )(f, b_f)


def _fgate_bwd(dcum, f, b_f, *, name):
    t, n = f.shape
    tt = _tile(t, 256)
    nt = t // tt

    def body(dc_ref, f_ref, b_ref, df_ref, db_ref, carry_ref):
        i = pl.program_id(0)

        @pl.when(i == 0)
        def _():
            carry_ref[...] = jnp.zeros_like(carry_ref)
            db_ref[...] = jnp.zeros_like(db_ref)

        row = lax.broadcasted_iota(jnp.int32, (tt, tt), 0)
        col = lax.broadcasted_iota(jnp.int32, (tt, tt), 1)
        triu = (col >= row).astype(BF16)
        dlf = _dot_01_left(triu, dc_ref[...]) + carry_ref[0:1, :]
        carry_ref[0:1, :] = dlf[0:1, :]
        z = f_ref[...] + b_ref[...]
        df = dlf * _sigmoid(-z)
        df_ref[...] = df
        db_ref[...] += jnp.sum(df, axis=0, keepdims=True)

    blk = pl.BlockSpec((tt, n), lambda i: (nt - 1 - i, 0))
    vec = pl.BlockSpec((1, n), lambda i: (0, 0))
    return pl.pallas_call(
        body, name=name, grid=(nt,),
        in_specs=[blk, blk, vec], out_specs=[blk, vec],
        out_shape=[jax.ShapeDtypeStruct((t, n), F32), jax.ShapeDtypeStruct((1, n), F32)],
        scratch_shapes=[pltpu.VMEM((SUBLANES, n), F32)],
        compiler_params=_cparams(("arbitrary",)),
    )(dcum, f, b_f)


def _attn_fwd(start, qkv, ckt, gate, *, name, tq, tk):
    t = qkv.shape[0]
    f = gate.shape[1]
    npair = f // LANES
    nq = t // tq
    ratio = tq // tk
    assert tq == ratio * tk and t == nq * tq
    scale = 1.0 / math.sqrt(FOX_HEAD_DIM)
    nt_dims = (((1,), (1,)), ((), ()))

    def body(start_ref, q_ref, k_ref, v_ref, ck_ref, g_ref, o_ref, y_ref, l_ref):
        i = pl.program_id(1)
        pair = pl.program_id(0)
        firsts = (start_ref[2 * pair, i], start_ref[2 * pair + 1, i])
        both = jnp.maximum(firsts[0], firsts[1])
        lane = lax.broadcasted_iota(jnp.int32, (tq, LANES), 1)
        lo = lane < FOX_HEAD_DIM
        q2 = q_ref[...] * scale
        qs = (jnp.where(lo, q2, 0).astype(BF16), jnp.where(lo, 0, q2).astype(BF16))
        row = lax.broadcasted_iota(jnp.int32, (tq, tk), 0)
        col = lax.broadcasted_iota(jnp.int32, (tq, tk), 1)

        def kv_step(j, carry, diag, heads=(0, 1)):
            off = pl.multiple_of(j * tk, tk)
            kj = k_ref[pl.ds(off, tk), :]
            vj = v_ref[pl.ds(off, tk), :]
            ck = ck_ref[:, pl.ds(off, tk)]
            new = list(carry)
            for h in heads:
                m, l, acc = carry[h]
                s = lax.dot_general(qs[h], kj, nt_dims, preferred_element_type=F32) - ck[h:h + 1, :]
                if diag is not None:
                    s = jnp.where(col + diag * tk <= row, s, NEG_INF)
                m_new = jnp.maximum(m, jnp.max(s, axis=-1, keepdims=True))
                alpha = jnp.exp(m - m_new)
                p = jnp.exp(s - m_new)
                l = alpha * l + jnp.sum(p, axis=-1, keepdims=True)
                acc = alpha * acc + jnp.dot(p.astype(BF16), vj, preferred_element_type=F32)
                new[h] = (m_new, l, acc)
            return tuple(new)

        carry = tuple((jnp.full((tq, 1), NEG_INF, F32), jnp.zeros((tq, 1), F32), jnp.zeros((tq, LANES), F32))
                      for _ in range(2))

        def run(lo_blk, hi_blk, carry, heads):
            twos = (hi_blk - lo_blk) // 2
            carry = lax.fori_loop(
                0, twos,
                lambda jj, c: kv_step(lo_blk + 2 * jj + 1, kv_step(lo_blk + 2 * jj, c, None, heads), None, heads),
                carry)
            return lax.fori_loop(lo_blk + 2 * twos, hi_blk, lambda j, c: kv_step(j, c, None, heads), carry)

        for h in range(2):
            carry = lax.fori_loop(firsts[h], both, lambda j, c, h=h: kv_step(j, c, None, (h,)), carry)
        carry = run(both, i * ratio, carry, (0, 1))
        for d in range(ratio):
            carry = kv_step(i * ratio + d, carry, d)
        (m0, l0, a0), (m1, l1, a1) = carry
        o = jnp.where(lo, a0 / l0, a1 / l1)
        o_ref[...] = o
        gate_v = g_ref[...]
        y_ref[...] = (o * (gate_v * _sigmoid(gate_v))).astype(y_ref.dtype)
        lse_t = jnp.transpose(jnp.where(lo, m0 + jnp.log(l0), m1 + jnp.log(l1)))
        l_ref[0:1, :] = lse_t[0:1, :]
        l_ref[1:2, :] = lse_t[FOX_HEAD_DIM:FOX_HEAD_DIM + 1, :]

    blk = lambda base: pl.BlockSpec((tq, LANES), lambda p, i, s: (i, base + p))
    full = lambda base: pl.BlockSpec((t, LANES), lambda p, i, s: (0, base + p))
    return pl.pallas_call(
        body, name=name,
        grid_spec=pltpu.PrefetchScalarGridSpec(
            num_scalar_prefetch=1, grid=(npair, nq),
            in_specs=[blk(0), full(npair), full(2 * npair), pl.BlockSpec((None, 2, t), lambda p, i, s: (p, 0, 0)),
                      blk(0)],
            out_specs=[blk(0), blk(0), pl.BlockSpec((None, 2, tq), lambda p, i, s: (p, 0, i))]),
        out_shape=[jax.ShapeDtypeStruct((t, f), F32), jax.ShapeDtypeStruct((t, f), BF16),
                   jax.ShapeDtypeStruct((npair, 2, t), F32)],
        compiler_params=_cparams(("parallel", "arbitrary")),
    )(start, qkv, qkv, qkv, ckt, gate)


def _attn_bwd(end, qkv, do, lt, dt, cke, *, name):
    t, f = do.shape
    npair = f // LANES
    tk = _tile(t, ATTN_TILE)
    nk = t // tk
    scale = 1.0 / math.sqrt(FOX_HEAD_DIM)
    nt_dims = (((1,), (1,)), ((), ()))
    tn_dims = (((0,), (0,)), ((), ()))

    def body(end_ref, k_ref, v_ref, q_ref, do_ref, l_ref, d_ref, ck_ref, dq_out_ref, dk_ref, dv_ref, dck_ref, dcq_ref,
             dq_ref, dk_s, dv_s, dck_s):
        j = pl.program_id(1)
        pair = pl.program_id(0)
        lasts = (end_ref[2 * pair, j], end_ref[2 * pair + 1, j])
        both = jnp.minimum(lasts[0], lasts[1])

        @pl.when(j == 0)
        def _():
            dq_ref[...] = jnp.zeros_like(dq_ref)
            dcq_ref[...] = jnp.zeros_like(dcq_ref)

        lane = lax.broadcasted_iota(jnp.int32, (tk, LANES), 1)
        lo = lane < FOX_HEAD_DIM
        sel = (lo, jnp.logical_not(lo))
        kj = k_ref[...]
        vj = v_ref[...]
        km = tuple(jnp.where(sel[h], kj, 0).astype(BF16) for h in range(2))
        ckv = ck_ref[...]
        ckh = (ckv[:, 0:1], ckv[:, FOX_HEAD_DIM:FOX_HEAD_DIM + 1])
        row = lax.broadcasted_iota(jnp.int32, (tk, tk), 0)
        col = lax.broadcasted_iota(jnp.int32, (tk, tk), 1)
        causal = row <= col

        def q_step(i, carry, masked, heads=(0, 1)):
            off = pl.multiple_of(i * tk, tk)
            qi = q_ref[pl.ds(off, tk), :]
            doi = do_ref[pl.ds(off, tk), :]
            lrow = l_ref[:, pl.ds(off, tk)]
            drow = d_ref[:, pl.ds(off, tk)]
            dq_add = jnp.zeros((tk, LANES), F32)
            for h in heads:
                qm = jnp.where(sel[h], qi, 0).astype(BF16)
                dom = jnp.where(sel[h], doi, 0).astype(BF16)
                st = lax.dot_general(kj, qm, nt_dims, preferred_element_type=F32) * scale
                st = st - ckh[h] - lrow[h:h + 1, :]
                if masked:
                    st = jnp.where(causal, st, NEG_INF)
                pt = jnp.exp(st)
                dpt = lax.dot_general(vj, dom, nt_dims, preferred_element_type=F32)
                dst = pt * (dpt - drow[h:h + 1, :])
                ptb = pt.astype(BF16)
                dstb = dst.astype(BF16)
                dv_s[...] += jnp.dot(ptb, dom, preferred_element_type=F32)
                dk_s[...] += jnp.dot(dstb, qm, preferred_element_type=F32)
                dq_add = dq_add + lax.dot_general(dstb, km[h], tn_dims, preferred_element_type=F32)
                dck_s[:, h:h + 1] -= jnp.sum(dst, axis=-1, keepdims=True)
                dcq_ref[h:h + 1, pl.ds(off, tk)] += jnp.sum(dst, axis=0, keepdims=True)
            dq_ref[pl.ds(off, tk), :] += dq_add * scale
            return carry

        dk_s[...] = jnp.zeros_like(dk_s)
        dv_s[...] = jnp.zeros_like(dv_s)
        dck_s[...] = jnp.zeros_like(dck_s)
        carry = 0
        carry = q_step(j, carry, True)
        carry = lax.fori_loop(j + 1, both, lambda i, c: q_step(i, c, False), carry)
        for h in range(2):
            carry = lax.fori_loop(both, lasts[h], lambda i, c, h=h: q_step(i, c, False, (h,)), carry)
        dk_acc, dv_acc = dk_s[...], dv_s[...]
        dck = (dck_s[:, 0:1], dck_s[:, 1:2])
        dk_ref[...] = (dk_acc * scale).astype(dk_ref.dtype)
        dv_ref[...] = dv_acc.astype(dv_ref.dtype)
        dck_t = jnp.transpose(jnp.where(lo, dck[0], dck[1]))
        dck_ref[0:1, :] = dck_t[0:1, :]
        dck_ref[1:2, :] = dck_t[FOX_HEAD_DIM:FOX_HEAD_DIM + 1, :]

        @pl.when(j == nk - 1)
        def _():
            dq_out_ref[...] = dq_ref[...].astype(dq_out_ref.dtype)

    blk = lambda base: pl.BlockSpec((tk, LANES), lambda p, j, e: (j, base + p))
    full = lambda base: pl.BlockSpec((t, LANES), lambda p, j, e: (0, base + p))
    rows = pl.BlockSpec((None, 2, t), lambda p, j, e: (p, 0, 0))
    return pl.pallas_call(
        body, name=name,
        grid_spec=pltpu.PrefetchScalarGridSpec(
            num_scalar_prefetch=1, grid=(npair, nk),
            in_specs=[blk(npair), blk(2 * npair), full(0), full(0), rows, rows, blk(0)],
            out_specs=[full(0), blk(0), blk(0), pl.BlockSpec((None, 2, tk), lambda p, j, e: (p, 0, j)), rows],
            scratch_shapes=[pltpu.VMEM((t, LANES), F32), pltpu.VMEM((tk, LANES), F32), pltpu.VMEM((tk, LANES), F32),
                            pltpu.VMEM((tk, LANES), F32)]),
        out_shape=[jax.ShapeDtypeStruct((t, f), BF16), jax.ShapeDtypeStruct((t, f), BF16),
                   jax.ShapeDtypeStruct((t, f), BF16), jax.ShapeDtypeStruct((npair, 2, t), F32),
                   jax.ShapeDtypeStruct((npair, 2, t), F32)],
        compiler_params=_cparams(("parallel", "arbitrary")),
    )(end, qkv, qkv, qkv, do, lt, dt, cke)


ATTN_TILE = 512
ATTN_FWD_QUERIES = 512
EXP_ZERO = -104.0
BOUND_SLACK = 1.02


def _attn_row_stats(qkv, *, name):
    t = qkv.shape[0]
    f = qkv.shape[1] // 3
    tt = _tile(t, 512)

    def body(q_ref, k_ref, s_ref):
        q = q_ref[...].astype(F32)
        k = k_ref[...].astype(F32)
        chan = lax.broadcasted_iota(jnp.int32, (f, LANES), 0) // FOX_HEAD_DIM
        lane = lax.broadcasted_iota(jnp.int32, (f, LANES), 1)
        acc = jnp.zeros((tt, LANES), F32)
        for off, val in ((0, q * q), (FOX_HEADS, q * k), (2 * FOX_HEADS, k * k)):
            pick = (chan == lane - off).astype(BF16)
            acc = acc + jnp.dot(val.astype(BF16), pick, preferred_element_type=F32)
        s_ref[...] = acc

    return pl.pallas_call(
        body, name=name, grid=(t // tt,),
        in_specs=[pl.BlockSpec((tt, f), lambda i: (i, 0)), pl.BlockSpec((tt, f), lambda i: (i, 1))],
        out_specs=pl.BlockSpec((tt, LANES), lambda i: (i, 0)),
        out_shape=jax.ShapeDtypeStruct((t, LANES), F32),
        compiler_params=_cparams(("parallel",)),
    )(qkv, qkv)


def _attn_skip_tables(stats, cum16, tile):
    t = stats.shape[0]
    nb = t // tile
    scale = 1.0 / math.sqrt(FOX_HEAD_DIM)
    qn = jnp.sqrt(stats[:, :FOX_HEADS]) * scale
    sii = stats[:, FOX_HEADS:2 * FOX_HEADS] * scale - cum16
    kmax = jnp.max(jnp.sqrt(stats[:, 2 * FOX_HEADS:3 * FOX_HEADS]), axis=0, keepdims=True)
    arow = qn * kmax * BOUND_SLACK - sii + 0.5 * BOUND_SLACK
    a_blk = jnp.max(arow.reshape(nb, tile, FOX_HEADS), axis=1)
    c_blk = -cum16.reshape(nb, tile, FOX_HEADS)[:, tile - 1, :]
    dead = (a_blk[:, None, :] + c_blk[None, :, :]) < EXP_ZERO
    start_h = jnp.sum(dead.astype(jnp.int32), axis=1)
    blk = jnp.arange(nb, dtype=jnp.int32)
    start = jnp.minimum(start_h, blk[:, None]).T
    needs = start[:, :, None] <= blk[None, None, :]
    end = jnp.max(jnp.where(needs, blk[None, :, None] + 1, 0), axis=1)
    return start, jnp.maximum(end, blk[None, :] + 1)


def _fox_post_bwd(dy, o, gate, *, name):
    t, f = dy.shape
    tt = _tile(t, 512)

    def body(dy_ref, o_ref, g_ref, do_ref, dg_ref, dl_ref):
        g = g_ref[...]
        sg = _sigmoid(g)
        dyv = dy_ref[...]
        ov = o_ref[...]
        do = dyv * (g * sg)
        do_ref[...] = do.astype(do_ref.dtype)
        dg_ref[...] = (dyv * ov * (sg * (1.0 + g * (1.0 - sg)))).astype(dg_ref.dtype)
        chan = lax.broadcasted_iota(jnp.int32, (f, LANES), 0)
        head = lax.broadcasted_iota(jnp.int32, (f, LANES), 1)
        pick = (chan // FOX_HEAD_DIM == head).astype(BF16)
        dl_ref[...] = _dot_01_right(do * ov, pick)

    blk = pl.BlockSpec((tt, f), lambda i: (i, 0))
    return pl.pallas_call(
        body, name=name, grid=(t // tt,),
        in_specs=[blk, blk, blk], out_specs=[blk, blk, pl.BlockSpec((tt, LANES), lambda i: (i, 0))],
        out_shape=[jax.ShapeDtypeStruct((t, f), BF16), jax.ShapeDtypeStruct((t, f), BF16),
                   jax.ShapeDtypeStruct((t, LANES), F32)],
        compiler_params=_cparams(("parallel",)),
    )(dy, o, gate)


def _adamw(w, g, m, v, *, name):
    _, r, c = w.shape
    tr = _tile(r, 256) if r % SUBLANES == 0 else r
    c1 = 1.0 - ADAM_B1 ** ADAM_STEP
    c2 = 1.0 - ADAM_B2 ** ADAM_STEP

    def body(w_ref, g_ref, m_ref, v_ref, go_ref, d_ref, mo_ref, vo_ref):
        gv = g_ref[...]
        go_ref[...] = gv
        mn = ADAM_B1 * m_ref[...] + (1.0 - ADAM_B1) * gv
        vn = ADAM_B2 * v_ref[...] + (1.0 - ADAM_B2) * (gv * gv)
        mo_ref[...] = mn
        vo_ref[...] = vn
        d_ref[...] = -ADAM_LR * ((mn / c1) / (jnp.sqrt(vn / c2) + ADAM_EPS) + ADAM_WD * w_ref[...])

    blk = pl.BlockSpec((None, tr, c), lambda i: (0, i, 0))
    return pl.pallas_call(
        body, name=name, grid=(r // tr,), in_specs=[blk] * 4, out_specs=[blk] * 4,
        out_shape=[jax.ShapeDtypeStruct((1, r, c), F32)] * 4,
        compiler_params=_cparams(("parallel",)),
    )(w, g, m, v)


def _sum_slots(land, *, name):
    ns, r, c = land.shape
    tr = _tile(r, 64) if r % SUBLANES == 0 else r

    def body(l_ref, o_ref):
        acc = l_ref[0].astype(F32)
        for s in range(1, ns):
            acc = acc + l_ref[s].astype(F32)
        o_ref[...] = acc

    return pl.pallas_call(
        body, name=name, grid=(r // tr,),
        in_specs=[pl.BlockSpec((ns, tr, c), lambda i: (0, i, 0))],
        out_specs=pl.BlockSpec((tr, c), lambda i: (i, 0)),
        out_shape=jax.ShapeDtypeStruct((r, c), F32),
        compiler_params=_cparams(("parallel",)),
    )(land)


ANY = pl.BlockSpec(memory_space=pl.ANY)


def _flip(v, bit):
    return 1 - v if bit else v


def _gather_chips(shards, small, *, name):
    n = len(shards)
    rels = ((1, 0), (0, 1), (1, 1))

    def body(*refs):
        ins, small_in = refs[:n], refs[n]
        outs, small_out = refs[n + 1:2 * n + 1], refs[2 * n + 1]
        send, recv, loc = refs[2 * n + 2:]
        x, y, c = lax.axis_index("x"), lax.axis_index("y"), lax.axis_index("c")
        me = 2 * x + y
        sibling = (x, y, 1 - c)
        local = [pltpu.make_async_copy(ins[k], outs[k].at[me], loc.at[k]) for k in range(n)]
        local.append(pltpu.make_async_copy(small_in, small_out.at[me], loc.at[n]))
        for cp in local:
            cp.start()

        def rows(k):
            half = ins[k].shape[0] // 2
            return pl.ds(pl.multiple_of(c * half, SUBLANES), half)

        sends = []
        for r, (rx, ry) in enumerate(rels):
            to = (_flip(x, rx), _flip(y, ry), c)
            for k in range(n):
                cp = pltpu.make_async_remote_copy(
                    src_ref=ins[k].at[rows(k), :], dst_ref=outs[k].at[me, rows(k), :],
                    send_sem=send.at[r * n + k], recv_sem=recv.at[r * n + k], device_id=to, device_id_type=MESH)
                cp.start()
                sends.append(cp)
            cp = pltpu.make_async_remote_copy(
                src_ref=small_in, dst_ref=small_out.at[me], send_sem=send.at[6 * n + r], recv_sem=recv.at[6 * n + r],
                device_id=to, device_id_type=MESH)
            cp.start()
            sends.append(cp)
        for r, (rx, ry) in enumerate(rels):
            src_chip = 2 * _flip(x, rx) + _flip(y, ry)
            for k in range(n):
                landed = outs[k].at[src_chip, rows(k), :]
                sends[r * (n + 1) + k].wait_recv()
                cp = pltpu.make_async_remote_copy(
                    src_ref=landed, dst_ref=landed, send_sem=send.at[3 * n + r * n + k],
                    recv_sem=recv.at[3 * n + r * n + k], device_id=sibling, device_id_type=MESH)
                cp.start()
                sends.append(cp)
            sends[r * (n + 1) + n].wait_recv()
        for cp in sends[:3 * (n + 1)]:
            cp.wait_send()
        for cp in sends[3 * (n + 1):]:
            cp.wait()
        for cp in local:
            cp.wait()

    vmem = pl.BlockSpec(memory_space=pltpu.VMEM)
    return pl.pallas_call(
        body, name=name, in_specs=[vmem] * (n + 1), out_specs=[vmem] * (n + 1),
        out_shape=[jax.ShapeDtypeStruct((N_CHIPS,) + s.shape, s.dtype) for s in list(shards) + [small]],
        scratch_shapes=[pltpu.SemaphoreType.DMA((6 * n + 3,)), pltpu.SemaphoreType.DMA((6 * n + 3,)),
                        pltpu.SemaphoreType.DMA((n + 1,))],
        compiler_params=pltpu.CompilerParams(has_side_effects=True, vmem_limit_bytes=VMEM_LIMIT),
    )(*shards, small)


_RELS7 = tuple((r >> 2 & 1, r >> 1 & 1, r & 1) for r in range(1, N_DEV))


def _scatter_copies(ins, outs, send, recv, loc):
    n = len(ins)
    x, y, c = lax.axis_index("x"), lax.axis_index("y"), lax.axis_index("c")
    me = 4 * x + 2 * y + c

    def piece(k, px, py, pc):
        half = ins[k].shape[1] // 2
        return ins[k].at[2 * px + py, pl.ds(pc * half, half), :]

    copies = [pltpu.make_async_copy(piece(k, x, y, c), outs[k].at[me], loc.at[k]) for k in range(n)]
    for r, (rx, ry, rc) in enumerate(_RELS7):
        tx, ty, tc = _flip(x, rx), _flip(y, ry), _flip(c, rc)
        for k in range(n):
            copies.append(pltpu.make_async_remote_copy(
                src_ref=piece(k, tx, ty, tc), dst_ref=outs[k].at[me], send_sem=send.at[r * n + k],
                recv_sem=recv.at[r * n + k], device_id=(tx, ty, tc), device_id_type=MESH))
    return copies


def _scatter_shapes(grads):
    n = len(grads)
    out_shape = [jax.ShapeDtypeStruct((N_DEV, g.shape[1] // 2, g.shape[2]), g.dtype) for g in grads]
    sems = [pltpu.SemaphoreType.DMA((7 * n,)), pltpu.SemaphoreType.DMA((7 * n,)), pltpu.SemaphoreType.DMA((n,))]
    return out_shape, sems


def _join_cores(halves, *, name):
    n = len(halves)

    def body(*refs):
        ins, outs = refs[:n], refs[n:2 * n]
        send, recv, loc = refs[2 * n:]
        x, y, c = lax.axis_index("x"), lax.axis_index("y"), lax.axis_index("c")
        copies = []
        for k in range(n):
            half = ins[k].shape[0]
            mine = outs[k].at[0, pl.ds(c * half, half), :]
            cp = pltpu.make_async_copy(ins[k], mine, loc.at[k])
            cp.start()
            copies.append(cp)
            cp = pltpu.make_async_remote_copy(
                src_ref=ins[k], dst_ref=mine, send_sem=send.at[k], recv_sem=recv.at[k],
                device_id=(x, y, 1 - c), device_id_type=MESH)
            cp.start()
            copies.append(cp)
        for cp in copies:
            cp.wait()

    in_vmem = pl.BlockSpec(memory_space=pltpu.VMEM)
    return pl.pallas_call(
        body, name=name, in_specs=[in_vmem] * n, out_specs=[in_vmem] * n,
        out_shape=[jax.ShapeDtypeStruct((1, 2 * h.shape[0], h.shape[1]), h.dtype) for h in halves],
        scratch_shapes=[pltpu.SemaphoreType.DMA((n,)), pltpu.SemaphoreType.DMA((n,)), pltpu.SemaphoreType.DMA((n,))],
        compiler_params=pltpu.CompilerParams(has_side_effects=True, vmem_limit_bytes=VMEM_LIMIT),
    )(*halves)


def _allreduce_small(buf, *, name):
    r, n = buf.shape
    half = r // 2
    rels = ((1, 0), (0, 1), (1, 1))

    def body(in_ref, out_ref, sib_ref, chips_ref, send, recv):
        x, y, c = lax.axis_index("x"), lax.axis_index("y"), lax.axis_index("c")
        sibling = (x, y, 1 - c)
        chip = 2 * x + y
        rows = pl.ds(pl.multiple_of(c * half, SUBLANES), half)

        swap = pltpu.make_async_remote_copy(src_ref=in_ref, dst_ref=sib_ref, send_sem=send.at[0], recv_sem=recv.at[0],
                                            device_id=sibling, device_id_type=MESH)
        swap.start()
        swap.wait()
        chips_ref[chip] = in_ref[rows, :] + sib_ref[rows, :]

        sends = []
        for k, (rx, ry) in enumerate(rels):
            cp = pltpu.make_async_remote_copy(
                src_ref=chips_ref.at[chip], dst_ref=chips_ref.at[chip], send_sem=send.at[1 + k],
                recv_sem=recv.at[1 + k], device_id=(_flip(x, rx), _flip(y, ry), c), device_id_type=MESH)
            cp.start()
            sends.append(cp)
        for cp in sends:
            cp.wait()
        total = chips_ref[0]
        for s in range(1, N_CHIPS):
            total = total + chips_ref[s]
        out_ref[rows, :] = total

        back = pltpu.make_async_remote_copy(src_ref=out_ref.at[rows, :], dst_ref=out_ref.at[rows, :],
                                            send_sem=send.at[4], recv_sem=recv.at[4],
                                            device_id=sibling, device_id_type=MESH)
        back.start()
        back.wait()

    vmem = pl.BlockSpec(memory_space=pltpu.VMEM)
    return pl.pallas_call(
        body, name=name, in_specs=[vmem], out_specs=vmem,
        out_shape=jax.ShapeDtypeStruct((r, n), F32),
        scratch_shapes=[pltpu.VMEM((r, n), F32), pltpu.VMEM((N_CHIPS, half, n), F32),
                        pltpu.SemaphoreType.DMA((5,)), pltpu.SemaphoreType.DMA((5,))],
        compiler_params=pltpu.CompilerParams(has_side_effects=True, vmem_limit_bytes=VMEM_LIMIT),
    )(buf)


def _pack(arrs):
    flat = []
    for a in arrs:
        v = a.reshape(-1)
        pad = (-v.shape[0]) % LANES
        if pad:
            v = jnp.pad(v, (0, pad))
        flat.append(v)
    v = jnp.concatenate(flat)
    pad = (-v.shape[0]) % (LANES * SUBLANES)
    if pad:
        v = jnp.pad(v, (0, pad))
    return v.reshape(-1, LANES)


def _unpack(buf, shapes):
    v = buf.reshape(-1)
    out, off = [], 0
    for s in shapes:
        n = math.prod(s)
        out.append(v[off:off + n].reshape(s))
        off += n + (-n) % LANES
    return out


def kernel(x, norm_g, final_g, lru_w_in, lru_conv_w, lru_conv_b, lru_wa, lru_ba, lru_wx, lru_bx, lru_a_param, lru_w_out, fox_w_in, fox_b_f, fox_w_out, loss_target, m_norm_g, m_final_g, m_lru_w_in, m_lru_conv_w, m_lru_conv_b, m_lru_wa, m_lru_ba, m_lru_wx, m_lru_bx, m_lru_a_param, m_lru_w_out, m_fox_w_in, m_fox_b_f, m_fox_w_out, v_norm_g, v_final_g, v_lru_w_in, v_lru_conv_w, v_lru_conv_b, v_lru_wa, v_lru_ba, v_lru_wx, v_lru_bx, v_lru_a_param, v_lru_w_out, v_fox_w_in, v_fox_b_f, v_fox_w_out):
    t, d = x.shape[1], x.shape[2]
    w = lru_wa.shape[1] * LRU_BLOCK_W
    f = FOX_HEADS * FOX_HEAD_DIM
    npair = f // LANES
    x0 = x.reshape(t, d)
    tgt = loss_target.reshape(t, d)
    chip = 2 * lax.axis_index("x") + lax.axis_index("y")

    g_lwi, g_lwo, g_cw = _gather_chips(
        [lru_w_in[0].astype(BF16), lru_w_out[0].astype(BF16)], lru_conv_w[0], name="gather_weights")
    cg = w // 2
    lwi = jnp.concatenate([g_lwi[0], g_lwi[2], g_lwi[1], g_lwi[3]], axis=1)
    lwo = g_lwo.reshape(w, d)
    conv_w = jnp.concatenate([g_cw[s] for s in range(N_CHIPS)], axis=1)
    conv_b, ba, bx, a_param = lru_conv_b, lru_ba, lru_bx, lru_a_param
    wa, wx = lru_wa[0], lru_wx[0]
    b_f = jnp.pad(fox_b_f, ((0, 0), (0, LANES - FOX_HEADS)))

    h0 = _rmsnorm(x0, norm_g[0], name="norm0")
    u = _matmul(h0, lwi, name="lru_in")
    y1, hs, (g_fwi, g_fwo) = _lru_fwd(u, conv_w, conv_b, wa, ba, wx, bx, a_param, cg=cg, name="lru_fwd",
                                      ride=[fox_w_in[0].astype(BF16), fox_w_out[0].astype(BF16)])
    fwi = jnp.concatenate([g_fwi[s] for s in range(N_CHIPS)], axis=1)
    w_qkv, w_g2 = fwi[:, :3 * f], fwi[:, 3 * f:4 * f]
    w_f = jnp.pad(fwi[:, 4 * f:], ((0, 0), (0, LANES - FOX_HEADS)))
    fwo = g_fwo.reshape(f, d)
    x1 = _matmul(y1, lwo, add=x0, name="lru_out")
    h1 = _rmsnorm(x1, norm_g[1], name="norm1")
    qkv = _matmul(h1, w_qkv, out_dtype=BF16, name="fox_qkv")
    gate2 = _matmul(h1, w_g2, name="fox_gate")
    flog = _matmul(h1, w_f, name="fox_f")
    cum, cke = _fgate_fwd(flog, b_f, name="fgate_fwd")
    cum16 = cum[:, :FOX_HEADS]
    ckt = cum16.T.reshape(npair, 2, t)
    a_tk, a_tq = _tile(t, ATTN_TILE), _tile(t, ATTN_FWD_QUERIES)
    a_start, a_end = _attn_skip_tables(_attn_row_stats(qkv, name="attn_row_stats"), cum16, a_tk)
    a_start_fwd = jnp.min(a_start.reshape(FOX_HEADS, t // a_tq, a_tq // a_tk), axis=2)
    o, y2, lse = _attn_fwd(a_start_fwd, qkv, ckt, gate2, name="attn_fwd", tq=a_tq, tk=a_tk)
    x2 = _matmul(y2, fwo, add=x1, name="fox_out")
    lsum, dx2, dx2_b, dgf = _final_loss(x2, tgt, final_g, name="final_loss")
    loss = lax.psum(0.5 * jnp.sum(lsum) / d, ("x", "y", "c"))

    d_fwo = _matmul(y2, dx2_b, ta=True, out_dtype=BF16, name="d_fox_w_out")
    dy2 = _matmul(dx2_b, fwo, tb=True, name="d_y2")
    do, dgate2, dl = _fox_post_bwd(dy2, o, gate2, name="fox_post_bwd")
    lt = lse
    dt = dl[:, :FOX_HEADS].T.reshape(npair, 2, t)
    dq, dk, dv, dck, dcq = _attn_bwd(a_end, qkv, do, lt, dt, cke, name="attn_bwd")
    dcum = jnp.pad((dck + dcq).reshape(FOX_HEADS, t).T, ((0, 0), (0, LANES - FOX_HEADS)))
    dflog, db_f = _fgate_bwd(dcum, flog, b_f, name="fgate_bwd")
    du2 = [dq, dk, dv, dgate2]
    dflog_b = dflog.astype(BF16)
    dh1 = _matmul_kparts(du2, fwi[:, :4 * f], chunk=f, name="d_h1_a")
    dh1 = _matmul(dflog_b, w_f, tb=True, add=dh1, name="d_h1_b")
    d_fwi_a = _matmul_nparts(h1, du2, chunk=f, out_dtype=BF16, name="d_fox_w_in_a")
    d_fwi_b = _matmul(h1, dflog_b, ta=True, out_dtype=BF16, name="d_fox_w_in_b")
    d_fwi = jnp.concatenate([d_fwi_a, d_fwi_b[:, :FOX_HEADS]], axis=1)
    dx1, dx1_b, dg1 = _rmsnorm_bwd(dh1, x1, norm_g[1], dx2, name="norm1_bwd", bf16_copy=True)

    d_lwo = _matmul(y1, dx1_b, ta=True, out_dtype=BF16, name="d_lru_w_out")
    dy1 = _matmul(dx1_b, lwo, tb=True, name="d_y1")
    n_fwi = fox_w_in.shape[2]
    g_fwi4 = jnp.stack([d_fwi[:, s * n_fwi:(s + 1) * n_fwi] for s in range(N_CHIPS)])
    g_fwo4 = d_fwo.reshape(N_CHIPS, f // N_CHIPS, d)
    g_lwo4 = d_lwo.reshape(N_CHIPS, w // N_CHIPS, d)
    (dxb, dgate, d_cw, d_cb, d_wa, d_ba, d_wx, d_bx, d_ap), lands_early = _lru_bwd(
        u, hs, dy1, conv_w, conv_b, wa, ba, wx, bx, a_param, cg=cg, name="lru_bwd", ride=[g_lwo4, g_fwi4, g_fwo4])
    d_lwi_p = _matmul_nparts(h0, [dxb, dgate], chunk=cg, out_dtype=BF16, name="d_lru_w_in")
    csz = cg
    g_lwi4 = jnp.stack([d_lwi_p[:, 0:csz], d_lwi_p[:, 2 * csz:3 * csz], d_lwi_p[:, csz:2 * csz],
                        d_lwi_p[:, 3 * csz:]])
    dh0, lands_last = _matmul_kparts([dxb, dgate], lwi, chunk=cg, name="d_h0", ride=[g_lwi4])
    dx0, _, dg0 = _rmsnorm_bwd(dh0, x0, norm_g[0], dx1, name="norm0_bwd", bf16_copy=False)
    lands = list(lands_last) + list(lands_early)
    halves = [_sum_slots(l, name="sum_" + nm) for l, nm in zip(lands, ("lru_w_in", "lru_w_out", "fox_w_in", "fox_w_out"))]
    big_g = _join_cores(halves, name="join_cores")

    small_g = [jnp.concatenate([dg0, dg1], axis=0), dgf.reshape(d), d_cw, d_cb, d_wa, d_ba, d_wx, d_bx, d_ap,
               db_f[:, :FOX_HEADS]]
    gsum = _allreduce_small(_pack(small_g), name="allreduce_small")
    zc = jnp.zeros((CONV_WIDTH, w), F32)
    pk_w = _pack([norm_g, final_g, zc, lru_conv_b, lru_wa, lru_ba, lru_wx, lru_bx, lru_a_param, fox_b_f])
    pk_m = _pack([m_norm_g, m_final_g, zc, m_lru_conv_b, m_lru_wa, m_lru_ba, m_lru_wx, m_lru_bx, m_lru_a_param,
                  m_fox_b_f])
    pk_v = _pack([v_norm_g, v_final_g, zc + 1.0, v_lru_conv_b, v_lru_wa, v_lru_ba, v_lru_wx, v_lru_bx,
                  v_lru_a_param, v_fox_b_f])
    s_g, s_delta, s_m, s_v = _adamw(pk_w[None], gsum[None], pk_m[None], pk_v[None], name="adamw_small")
    out_shapes = [norm_g.shape, final_g.shape, (CONV_WIDTH, w), lru_conv_b.shape, lru_wa.shape, lru_ba.shape,
                  lru_wx.shape, lru_bx.shape, lru_a_param.shape, fox_b_f.shape]
    sg = _unpack(s_g, out_shapes)
    sd = _unpack(s_delta, out_shapes)
    sm = _unpack(s_m, out_shapes)
    sv = _unpack(s_v, out_shapes)

    ncw = lru_conv_w.shape[2]
    g_cw_loc = lax.dynamic_slice_in_dim(sg[2], chip * ncw, ncw, axis=1)
    g_cw_loc, cw_d, cw_m, cw_v = _adamw(lru_conv_w, g_cw_loc[None], m_lru_conv_w, v_lru_conv_w, name="adamw_conv_w")

    big = []
    for nm, wt, g, mm, vv in (("lru_w_in", lru_w_in, big_g[0], m_lru_w_in, v_lru_w_in),
                              ("lru_w_out", lru_w_out, big_g[1], m_lru_w_out, v_lru_w_out),
                              ("fox_w_in", fox_w_in, big_g[2], m_fox_w_in, v_fox_w_in),
                              ("fox_w_out", fox_w_out, big_g[3], m_fox_w_out, v_fox_w_out)):
        big.append(tuple(_adamw(wt, g, mm, vv, name="adamw_" + nm)))

    def assemble(idx):
        small = (sg, sd, sm, sv)[idx]
        cw = (g_cw_loc, cw_d, cw_m, cw_v)[idx]
        return [small[0], small[1], big[0][idx], cw, small[3], small[4], small[5], small[6], small[7], small[8],
                big[1][idx], big[2][idx], small[9], big[3][idx]]

    grad_x = dx0.reshape(1, t, d)
    return (loss, grad_x, *assemble(0), *assemble(1), *assemble(2), *assemble(3))
```

```python
import math

import jax
import jax.numpy as jnp
from jax import lax
from jax.experimental import pallas as pl
from jax.experimental.pallas import tpu as pltpu

F32 = jnp.float32
BF16 = jnp.bfloat16

EPS = 1e-6
LRU_C = 8.0
LRU_BLOCK_W = 128
CONV_WIDTH = 4
FOX_HEADS = 16
FOX_HEAD_DIM = 64
NEG_INF = -1e30
ADAM_LR = 0.001
ADAM_B1 = 0.9
ADAM_B2 = 0.999
ADAM_EPS = 1e-08
ADAM_WD = 0.01
ADAM_STEP = 10

LANES = 128
SUBLANES = 8
VMEM_LIMIT = 56 * 1024 * 1024
TINY = 1e-30
N_CHIPS = 4
N_DEV = 8
MESH = pl.DeviceIdType.MESH


def _tile(n, pref):
    t = min(n, pref)
    while n % t:
        t //= 2
    return t


def _cparams(dims=None):
    return pltpu.CompilerParams(dimension_semantics=dims, vmem_limit_bytes=VMEM_LIMIT)


def _sigmoid(x):
    return 0.5 * jnp.tanh(0.5 * x) + 0.5


def _log1p(x):
    u = 1.0 + x
    return jnp.where(u == 1.0, x, jnp.log(u) * x / (u - 1.0))


def _bf16_pieces(x):
    hi = x.astype(BF16)
    rest = x - hi.astype(F32)
    mid = rest.astype(BF16)
    return hi, mid, (rest - mid.astype(F32)).astype(BF16)


def _dot_01_left(m01, x):
    return sum(jnp.dot(m01, p, preferred_element_type=F32) for p in _bf16_pieces(x))


def _dot_01_right(x, m01):
    return sum(jnp.dot(p, m01, preferred_element_type=F32) for p in _bf16_pieces(x))


def _softplus(x):
    return jnp.maximum(x, 0.0) + _log1p(jnp.exp(-jnp.abs(x)))


MM_TILE = 1024
MM_FULL_K = 1536


def _matmul(a, b, *, name, ta=False, tb=False, out_dtype=F32, add=None, tm=MM_TILE, tn=MM_TILE, tk=None):
    if ta:
        kdim, m = a.shape
    else:
        m, kdim = a.shape
    if tb:
        n, kb = b.shape
    else:
        kb, n = b.shape
    assert kdim == kb, (a.shape, b.shape, ta, tb)
    if tk is None:
        tk = kdim if kdim <= MM_FULL_K else MM_TILE
    tm, tn, tk = _tile(m, tm), _tile(n, tn), _tile(kdim, tk)
    nk = kdim // tk
    dn = (((0 if ta else 1,), (1 if tb else 0,)), ((), ()))
    has_add = add is not None

    def body(*refs):
        if has_add:
            a_ref, b_ref, add_ref, o_ref = refs[:4]
        else:
            a_ref, b_ref, o_ref = refs[:3]
        part = lax.dot_general(a_ref[...].astype(BF16), b_ref[...].astype(BF16), dn, preferred_element_type=F32)

        def finish(r):
            if has_add:
                r = r + add_ref[...].astype(F32)
            o_ref[...] = r.astype(o_ref.dtype)

        if nk == 1:
            finish(part)
            return
        acc_ref = refs[-1]
        k = pl.program_id(2)

        @pl.when(k == 0)
        def _():
            acc_ref[...] = part

        @pl.when(k > 0)
        def _():
            acc_ref[...] += part

        @pl.when(k == nk - 1)
        def _():
            finish(acc_ref[...])

    a_spec = pl.BlockSpec((tk, tm), lambda i, j, k: (k, i)) if ta else pl.BlockSpec((tm, tk), lambda i, j, k: (i, k))
    b_spec = pl.BlockSpec((tn, tk), lambda i, j, k: (j, k)) if tb else pl.BlockSpec((tk, tn), lambda i, j, k: (k, j))
    o_spec = pl.BlockSpec((tm, tn), lambda i, j, k: (i, j))
    in_specs = [a_spec, b_spec] + ([o_spec] if has_add else [])
    args = (a, b) + ((add,) if has_add else ())
    return pl.pallas_call(
        body, name=name, grid=(m // tm, n // tn, nk), in_specs=in_specs, out_specs=o_spec,
        out_shape=jax.ShapeDtypeStruct((m, n), out_dtype),
        scratch_shapes=[pltpu.VMEM((tm, tn), F32)] if nk > 1 else [],
        compiler_params=_cparams(("parallel", "parallel", "arbitrary")),
    )(*args)


def _matmul_kparts(parts, b, *, chunk, name, tm=MM_TILE, tn=MM_TILE, ride=()):
    npart = len(parts)
    nride = len(ride)
    m = parts[0].shape[0]
    n, kdim = b.shape
    nk = kdim // chunk
    assert nk * chunk == kdim and sum(p.shape[1] for p in parts) == kdim and nk % npart == 0
    tm, tn = _tile(m, tm), _tile(n, tn)
    dn = (((1,), (1,)), ((), ()))
    steps = (m // tm) * (n // tn) * nk

    def body(*refs):
        ins, rest = refs[:npart + 1], refs[npart + 1:]
        ride_in, rest = rest[:nride], rest[nride:]
        o_ref, rest = rest[0], rest[1:]
        ride_out, rest = rest[:nride], rest[nride:]
        acc_ref, sems = rest[0], rest[1:]
        if not nride:
            core(*ins, o_ref, acc_ref)
            return
        step = (pl.program_id(0) * (n // tn) + pl.program_id(1)) * nk + pl.program_id(2)

        @pl.when(step == 0)
        def _():
            for cp in _scatter_copies(ride_in, ride_out, *sems):
                cp.start()

        core(*ins, o_ref, acc_ref)

        @pl.when(step == steps - 1)
        def _():
            for cp in _scatter_copies(ride_in, ride_out, *sems):
                cp.wait()

    def core(*refs):
        a_refs, b_ref, o_ref, acc_ref = refs[:npart], refs[npart], refs[npart + 1], refs[npart + 2]
        k = pl.program_id(2)

        @pl.when(k == 0)
        def _():
            acc_ref[...] = jnp.zeros_like(acc_ref)

        for s in range(npart):
            @pl.when(lax.rem(k, npart) == s)
            def _(s=s):
                acc_ref[...] += lax.dot_general(a_refs[s][...].astype(BF16), b_ref[...].astype(BF16), dn,
                                                preferred_element_type=F32)

        @pl.when(k == nk - 1)
        def _():
            o_ref[...] = acc_ref[...].astype(o_ref.dtype)

    a_specs = [pl.BlockSpec((tm, chunk), lambda i, j, k: (i, k // npart)) for _ in range(npart)]
    any_spec = pl.BlockSpec(memory_space=pl.ANY)
    ride_shape, ride_sems = _scatter_shapes(ride) if nride else ([], [])
    outs = pl.pallas_call(
        body, name=name, grid=(m // tm, n // tn, nk),
        in_specs=a_specs + [pl.BlockSpec((tn, chunk), lambda i, j, k: (j, k))] + [any_spec] * nride,
        out_specs=[pl.BlockSpec((tm, tn), lambda i, j, k: (i, j))] + [any_spec] * nride,
        out_shape=[jax.ShapeDtypeStruct((m, n), F32)] + ride_shape,
        scratch_shapes=[pltpu.VMEM((tm, tn), F32)] + ride_sems,
        compiler_params=_cparams(("arbitrary",) * 3 if nride else ("parallel", "parallel", "arbitrary")),
    )(*parts, b, *ride)
    return (outs[0], outs[1:]) if nride else outs[0]


def _matmul_nparts(a, parts, *, chunk, out_dtype, name, tm=MM_TILE, tk=MM_TILE):
    npart = len(parts)
    t, m = a.shape
    n = sum(p.shape[1] for p in parts)
    nj = n // chunk
    assert nj * chunk == n and nj % npart == 0
    tm, tk = _tile(m, tm), _tile(t, tk)
    nk = t // tk
    dn = (((0,), (0,)), ((), ()))

    def body(*refs):
        a_ref, b_refs, o_ref, acc_ref = refs[0], refs[1:1 + npart], refs[1 + npart], refs[2 + npart]
        j, k = pl.program_id(1), pl.program_id(2)

        @pl.when(k == 0)
        def _():
            acc_ref[...] = jnp.zeros_like(acc_ref)

        for s in range(npart):
            @pl.when(lax.rem(j, npart) == s)
            def _(s=s):
                acc_ref[...] += lax.dot_general(a_ref[...].astype(BF16), b_refs[s][...].astype(BF16), dn,
                                                preferred_element_type=F32)

        @pl.when(k == nk - 1)
        def _():
            o_ref[...] = acc_ref[...].astype(o_ref.dtype)

    def b_spec(s):
        return pl.BlockSpec((tk, chunk), lambda i, j, k: (jnp.where(lax.rem(j, npart) == s, k, 0), j // npart))

    return pl.pallas_call(
        body, name=name, grid=(m // tm, nj, nk),
        in_specs=[pl.BlockSpec((tk, tm), lambda i, j, k: (k, i))] + [b_spec(s) for s in range(npart)],
        out_specs=pl.BlockSpec((tm, chunk), lambda i, j, k: (i, j)),
        out_shape=jax.ShapeDtypeStruct((m, n), out_dtype),
        scratch_shapes=[pltpu.VMEM((tm, chunk), F32)],
        compiler_params=_cparams(("parallel", "parallel", "arbitrary")),
    )(a, *parts)


def _rmsnorm(x, g, *, name):
    t, d = x.shape
    tt = _tile(t, 512)

    def body(x_ref, g_ref, o_ref):
        xf = x_ref[...]
        rstd = lax.rsqrt(jnp.mean(xf * xf, axis=-1, keepdims=True) + EPS)
        o_ref[...] = (xf * rstd * g_ref[...]).astype(o_ref.dtype)

    return pl.pallas_call(
        body, name=name, grid=(t // tt,),
        in_specs=[pl.BlockSpec((tt, d), lambda i: (i, 0)), pl.BlockSpec((1, d), lambda i: (0, 0))],
        out_specs=pl.BlockSpec((tt, d), lambda i: (i, 0)),
        out_shape=jax.ShapeDtypeStruct((t, d), BF16),
        compiler_params=_cparams(("parallel",)),
    )(x, g.reshape(1, d))


def _rmsnorm_bwd(dh, x, g, dres, *, name, bf16_copy):
    t, d = x.shape
    tt = _tile(t, 512)
    nt = t // tt

    def body(dh_ref, x_ref, g_ref, dres_ref, dx_ref, *out_refs):
        dg_ref = out_refs[-1]
        i = pl.program_id(0)

        @pl.when(i == 0)
        def _():
            dg_ref[...] = jnp.zeros_like(dg_ref)

        xf = x_ref[...]
        rstd = lax.rsqrt(jnp.mean(xf * xf, axis=-1, keepdims=True) + EPS)
        xhat = xf * rstd
        dhf = dh_ref[...].astype(F32)
        dxhat = dhf * g_ref[...]
        mt = jnp.mean(dxhat * xhat, axis=-1, keepdims=True)
        dx = dres_ref[...] + rstd * (dxhat - xhat * mt)
        dx_ref[...] = dx
        if bf16_copy:
            out_refs[0][...] = dx.astype(BF16)
        dg_ref[...] += jnp.sum(dhf * xhat, axis=0, keepdims=True)

    blk = pl.BlockSpec((tt, d), lambda i: (i, 0))
    vec = pl.BlockSpec((1, d), lambda i: (0, 0))
    low = [jax.ShapeDtypeStruct((t, d), BF16)] if bf16_copy else []
    outs = pl.pallas_call(
        body, name=name, grid=(nt,),
        in_specs=[blk, blk, vec, blk], out_specs=[blk] + [blk] * len(low) + [vec],
        out_shape=[jax.ShapeDtypeStruct((t, d), F32)] + low + [jax.ShapeDtypeStruct((1, d), F32)],
        compiler_params=_cparams(("arbitrary",)),
    )(dh, x, g.reshape(1, d), dres)
    return outs[0], (outs[1] if bf16_copy else None), outs[-1]


def _final_loss(x2, tgt, g, *, name):
    t, d = x2.shape
    tt = _tile(t, 512)

    def body(x_ref, t_ref, g_ref, l_ref, dx_ref, dxb_ref, dg_ref):
        i = pl.program_id(0)

        @pl.when(i == 0)
        def _():
            dg_ref[...] = jnp.zeros_like(dg_ref)
            l_ref[...] = jnp.zeros_like(l_ref)

        xf = x_ref[...]
        gg = g_ref[...]
        rstd = lax.rsqrt(jnp.mean(xf * xf, axis=-1, keepdims=True) + EPS)
        xhat = xf * rstd
        err = xhat * gg - t_ref[...]
        l_ref[...] += jnp.sum(err * err, axis=0, keepdims=True)
        dy = err * (1.0 / d)
        dxhat = dy * gg
        mt = jnp.mean(dxhat * xhat, axis=-1, keepdims=True)
        dx = rstd * (dxhat - xhat * mt)
        dx_ref[...] = dx
        dxb_ref[...] = dx.astype(dxb_ref.dtype)
        dg_ref[...] += jnp.sum(dy * xhat, axis=0, keepdims=True)

    blk = pl.BlockSpec((tt, d), lambda i: (i, 0))
    vec = pl.BlockSpec((1, d), lambda i: (0, 0))
    return pl.pallas_call(
        body, name=name, grid=(t // tt,),
        in_specs=[blk, blk, vec], out_specs=[vec, blk, blk, vec],
        out_shape=[jax.ShapeDtypeStruct((1, d), F32), jax.ShapeDtypeStruct((t, d), F32),
                   jax.ShapeDtypeStruct((t, d), BF16), jax.ShapeDtypeStruct((1, d), F32)],
        compiler_params=_cparams(("arbitrary",)),
    )(x2, tgt, g.reshape(1, d))


def _shift_down(prev8, cur, s):
    ext = jnp.concatenate([prev8, cur], axis=0)
    if s == 0:
        return cur
    return pltpu.roll(ext, s, 0)[SUBLANES:, :]


def _shift_up(cur, next8, s):
    if s == 0:
        return cur
    n = cur.shape[0]
    ext = jnp.concatenate([cur, next8], axis=0)
    return pltpu.roll(ext, n + SUBLANES - s, 0)[:n, :]


def _lru_gates(xc, wa, ba, wx, bx, sp):
    xcb = xc.astype(BF16)
    r = _sigmoid(jnp.dot(xcb, wa, preferred_element_type=F32) + ba)
    ig = _sigmoid(jnp.dot(xcb, wx, preferred_element_type=F32) + bx)
    log_a = -LRU_C * r * sp
    a = jnp.exp(log_a)
    z = -jnp.tanh(log_a) * (a * a + 1.0)
    inv_mult = lax.rsqrt(jnp.maximum(z, TINY))
    return r, ig, a, z * inv_mult, inv_mult


def _lru_specs(tt, cg, n_groups, nt, reverse):
    ncol = cg // LANES
    if reverse:
        ti = lambda i: nt - 1 - i
    else:
        ti = lambda i: i
    hb = tt // SUBLANES
    cur = lambda col: pl.BlockSpec((tt, cg), lambda g, i: (ti(i), 2 * g + col))
    prev = lambda col: pl.BlockSpec((SUBLANES, cg), lambda g, i: (jnp.maximum(ti(i) * hb - 1, 0), 2 * g + col))
    chan = lambda rows: pl.BlockSpec((rows, cg), lambda g, i: (0, g))
    wblk = pl.BlockSpec((ncol, LRU_BLOCK_W, LRU_BLOCK_W), lambda g, i: (g, 0, 0))
    plain = pl.BlockSpec((tt, cg), lambda g, i: (ti(i), g))
    plain_prev = pl.BlockSpec((SUBLANES, cg), lambda g, i: (jnp.maximum(ti(i) * hb - 1, 0), g))
    return cur, prev, chan, wblk, plain, plain_prev


def _chip_gather_copies(ins, outs, send, recv, loc):
    n = len(ins)
    x, y, c = lax.axis_index("x"), lax.axis_index("y"), lax.axis_index("c")
    me = 2 * x + y
    copies = [pltpu.make_async_copy(ins[k], outs[k].at[me], loc.at[k]) for k in range(n)]
    for r, (rx, ry) in enumerate(((1, 0), (0, 1), (1, 1))):
        for k in range(n):
            copies.append(pltpu.make_async_remote_copy(
                src_ref=ins[k], dst_ref=outs[k].at[me], send_sem=send.at[r * n + k], recv_sem=recv.at[r * n + k],
                device_id=(_flip(x, rx), _flip(y, ry), c), device_id_type=MESH))
    return copies


def _lru_fwd(u, conv_w, conv_b, wa, ba, wx, bx, a_param, *, cg, name, ride=()):
    nride = len(ride)
    t, w2 = u.shape
    w = w2 // 2
    n_groups = w // cg
    ncol = cg // LANES
    tt = _tile(t, 256)
    nt = t // tt
    cur, prev, chan, wblk, plain, _ = _lru_specs(tt, cg, n_groups, nt, False)

    def body(*refs):
        n_in, n_out, n_scr = 10, 2, 3
        ins, rest = refs[:n_in], refs[n_in:]
        ride_in, rest = rest[:nride], rest[nride:]
        outs, rest = rest[:n_out], rest[n_out:]
        ride_out, rest = rest[:nride], rest[nride:]
        scr, sems = rest[:n_scr], rest[n_scr:]
        if not nride:
            core(*ins, *outs, *scr)
            return
        step = pl.program_id(0) * nt + pl.program_id(1)

        @pl.when(step == 0)
        def _():
            for cp in _chip_gather_copies(ride_in, ride_out, *sems):
                cp.start()

        core(*ins, *outs, *scr)

        @pl.when(step == n_groups * nt - 1)
        def _():
            for cp in _chip_gather_copies(ride_in, ride_out, *sems):
                cp.wait()

    def core(xb_ref, xp_ref, gate_ref, cw_ref, cb_ref, wa_ref, ba_ref, wx_ref, bx_ref, ap_ref,
             y_ref, hs_ref, h_ref, a_s, b_s):
        i = pl.program_id(1)

        @pl.when(i == 0)
        def _():
            h_ref[...] = jnp.zeros_like(h_ref)

        keep = (i > 0).astype(F32)
        for n in range(ncol):
            sl = slice(n * LANES, (n + 1) * LANES)
            xb = xb_ref[:, sl]
            xp = xp_ref[:, sl] * keep
            xc = cb_ref[:, sl] + cw_ref[3:4, sl] * xb
            for s in range(1, CONV_WIDTH):
                xc = xc + cw_ref[3 - s:4 - s, sl] * _shift_down(xp, xb, s)
            sp = _softplus(-ap_ref[:, sl])
            _, ig, a, mult, _ = _lru_gates(xc, wa_ref[n].astype(BF16), ba_ref[:, sl],
                                           wx_ref[n].astype(BF16), bx_ref[:, sl], sp)
            a_s[:, sl] = a
            b_s[:, sl] = mult * (ig * xc)

        def step(g, h):
            base = pl.multiple_of(g * SUBLANES, SUBLANES)
            for r in range(SUBLANES):
                h = a_s[pl.ds(base + r, 1), :] * h + b_s[pl.ds(base + r, 1), :]
                hs_ref[pl.ds(base + r, 1), :] = h
            return h

        h = lax.fori_loop(0, tt // SUBLANES, step, h_ref[0:1, :])
        h_ref[0:1, :] = h
        gate = gate_ref[...]
        y_ref[...] = (hs_ref[...] * (gate * _sigmoid(gate))).astype(y_ref.dtype)

    any_spec = pl.BlockSpec(memory_space=pl.ANY)
    ride_sems = [pltpu.SemaphoreType.DMA((3 * nride,)), pltpu.SemaphoreType.DMA((3 * nride,)),
                 pltpu.SemaphoreType.DMA((nride,))] if nride else []
    outs = pl.pallas_call(
        body, name=name, grid=(n_groups, nt),
        in_specs=[cur(0), prev(0), cur(1), chan(CONV_WIDTH), chan(1), wblk, chan(1), wblk, chan(1), chan(1)]
        + [any_spec] * nride,
        out_specs=[plain, plain] + [any_spec] * nride,
        out_shape=[jax.ShapeDtypeStruct((t, w), BF16), jax.ShapeDtypeStruct((t, w), F32)]
        + [jax.ShapeDtypeStruct((N_CHIPS,) + r.shape, r.dtype) for r in ride],
        scratch_shapes=[pltpu.VMEM((SUBLANES, cg), F32), pltpu.VMEM((tt, cg), F32), pltpu.VMEM((tt, cg), F32)]
        + ride_sems,
        compiler_params=_cparams(("arbitrary", "arbitrary")),
    )(u, u, u, conv_w, conv_b, wa, ba, wx, bx, a_param, *ride)
    return outs[0], outs[1], outs[2:]


def _lru_bwd(u, hs, dy, conv_w, conv_b, wa, ba, wx, bx, a_param, *, cg, name, ride=()):
    nride = len(ride)
    t, w2 = u.shape
    w = w2 // 2
    n_groups = w // cg
    ncol = cg // LANES
    tt = _tile(t, 256)
    nt = t // tt
    cur, prev, chan, wblk, plain, plain_prev = _lru_specs(tt, cg, n_groups, nt, True)
    tn_dims = (((0,), (0,)), ((), ()))
    nt_dims = (((1,), (1,)), ((), ()))

    def body(*refs):
        n_in, n_out, n_scr = 13, 9, 5
        ins, rest = refs[:n_in], refs[n_in:]
        ride_in, rest = rest[:nride], rest[nride:]
        outs, rest = rest[:n_out], rest[n_out:]
        ride_out, rest = rest[:nride], rest[nride:]
        scr, sems = rest[:n_scr], rest[n_scr:]
        if not nride:
            core(*ins, *outs, *scr)
            return
        step = pl.program_id(0) * nt + pl.program_id(1)

        @pl.when(step == 0)
        def _():
            for cp in _scatter_copies(ride_in, ride_out, *sems):
                cp.start()

        core(*ins, *outs, *scr)

        @pl.when(step == n_groups * nt - 1)
        def _():
            for cp in _scatter_copies(ride_in, ride_out, *sems):
                cp.wait()

    def core(xb_ref, xp_ref, gate_ref, hs_ref, hp_ref, dy_ref, cw_ref, cb_ref, wa_ref, ba_ref, wx_ref, bx_ref,
             ap_ref, dxb_ref, dgate_ref, dcw_ref, dcb_ref, dwa_ref, dba_ref, dwx_ref, dbx_ref, dsp_ref,
             c_ref, nx_ref, a_s, dhs_s, lam_s):
        i = pl.program_id(1)
        first_time_block = i == nt - 1

        @pl.when(i == 0)
        def _():
            c_ref[...] = jnp.zeros_like(c_ref)
            nx_ref[...] = jnp.zeros_like(nx_ref)
            for r in (dcw_ref, dcb_ref, dwa_ref, dba_ref, dwx_ref, dbx_ref, dsp_ref):
                r[...] = jnp.zeros_like(r)

        keep = jnp.where(first_time_block, 0.0, 1.0).astype(F32)
        gate = gate_ref[...]
        sg = _sigmoid(gate)
        dyv = dy_ref[...]
        hsv = hs_ref[...]
        dhs_s[...] = dyv * (gate * sg)
        dgate_ref[...] = (dyv * hsv * (sg * (1.0 + gate * (1.0 - sg)))).astype(dgate_ref.dtype)

        saved = []
        for n in range(ncol):
            sl = slice(n * LANES, (n + 1) * LANES)
            xb = xb_ref[:, sl]
            xp = xp_ref[:, sl] * keep
            shifted = [xb] + [_shift_down(xp, xb, s) for s in range(1, CONV_WIDTH)]
            xc = cb_ref[:, sl] + cw_ref[3:4, sl] * xb
            for s in range(1, CONV_WIDTH):
                xc = xc + cw_ref[3 - s:4 - s, sl] * shifted[s]
            sp = _softplus(-ap_ref[:, sl])
            wab = wa_ref[n].astype(BF16)
            wxb = wx_ref[n].astype(BF16)
            r, ig, a, mult, inv_mult = _lru_gates(xc, wab, ba_ref[:, sl], wxb, bx_ref[:, sl], sp)
            a_s[:, sl] = a
            saved.append((sl, shifted, xc, sp, wab, wxb, r, ig, a, mult, inv_mult))

        def step(g, c):
            base = pl.multiple_of(tt - SUBLANES - g * SUBLANES, SUBLANES)
            for r in range(SUBLANES - 1, -1, -1):
                lam = dhs_s[pl.ds(base + r, 1), :] + c
                lam_s[pl.ds(base + r, 1), :] = lam
                c = a_s[pl.ds(base + r, 1), :] * lam
            return c

        c_ref[0:1, :] = lax.fori_loop(0, tt // SUBLANES, step, c_ref[0:1, :])

        for n in range(ncol):
            sl, shifted, xc, sp, wab, wxb, r, ig, a, mult, inv_mult = saved[n]
            lam = lam_s[:, sl]
            hprev = _shift_down(hp_ref[:, sl] * keep, hs_ref[:, sl], 1)
            da = lam * hprev
            dmult = lam * (ig * xc)
            dlog_a = da * a - dmult * (a * a * inv_mult)
            di = lam * (mult * xc)
            dxc = lam * (mult * ig)
            dr = dlog_a * (-LRU_C * sp)
            dsp_ref[:, sl] += jnp.sum(dlog_a * (-LRU_C * r), axis=0, keepdims=True)
            dza = dr * (r * (1.0 - r))
            dzx = di * (ig * (1.0 - ig))
            dba_ref[:, sl] += jnp.sum(dza, axis=0, keepdims=True)
            dbx_ref[:, sl] += jnp.sum(dzx, axis=0, keepdims=True)
            xcb = xc.astype(BF16)
            dzab = dza.astype(BF16)
            dzxb = dzx.astype(BF16)
            dwa_ref[n] += lax.dot_general(xcb, dzab, tn_dims, preferred_element_type=F32)
            dwx_ref[n] += lax.dot_general(xcb, dzxb, tn_dims, preferred_element_type=F32)
            dxc = dxc + lax.dot_general(dzab, wab, nt_dims, preferred_element_type=F32)
            dxc = dxc + lax.dot_general(dzxb, wxb, nt_dims, preferred_element_type=F32)
            dcb_ref[:, sl] += jnp.sum(dxc, axis=0, keepdims=True)
            for s in range(CONV_WIDTH):
                dcw_ref[3 - s:4 - s, sl] += jnp.sum(dxc * shifted[s], axis=0, keepdims=True)
            nx = nx_ref[:, sl]
            dxb = cw_ref[3:4, sl] * dxc
            for s in range(1, CONV_WIDTH):
                dxb = dxb + cw_ref[3 - s:4 - s, sl] * _shift_up(dxc, nx, s)
            dxb_ref[:, sl] = dxb.astype(dxb_ref.dtype)
            nx_ref[:, sl] = dxc[0:SUBLANES, :]

        @pl.when(first_time_block)
        def _():
            dsp_ref[...] = dsp_ref[...] * (-_sigmoid(-ap_ref[...]))

    dxb_spec = pl.BlockSpec((tt, cg), lambda g, i: (nt - 1 - i, g))
    any_spec = pl.BlockSpec(memory_space=pl.ANY)
    ride_shape, ride_sems = _scatter_shapes(ride) if nride else ([], [])
    outs = pl.pallas_call(
        body, name=name, grid=(n_groups, nt),
        in_specs=[cur(0), prev(0), cur(1), plain, plain_prev, plain, chan(CONV_WIDTH), chan(1), wblk, chan(1), wblk,
                  chan(1), chan(1)] + [any_spec] * nride,
        out_specs=[dxb_spec, dxb_spec, chan(CONV_WIDTH), chan(1), wblk, chan(1), wblk, chan(1), chan(1)]
        + [any_spec] * nride,
        out_shape=[jax.ShapeDtypeStruct((t, w), BF16), jax.ShapeDtypeStruct((t, w), BF16),
                   jax.ShapeDtypeStruct(conv_w.shape, F32), jax.ShapeDtypeStruct(conv_b.shape, F32),
                   jax.ShapeDtypeStruct(wa.shape, F32), jax.ShapeDtypeStruct(ba.shape, F32),
                   jax.ShapeDtypeStruct(wx.shape, F32), jax.ShapeDtypeStruct(bx.shape, F32),
                   jax.ShapeDtypeStruct(a_param.shape, F32)] + ride_shape,
        scratch_shapes=[pltpu.VMEM((SUBLANES, cg), F32), pltpu.VMEM((SUBLANES, cg), F32),
                        pltpu.VMEM((tt, cg), F32), pltpu.VMEM((tt, cg), F32), pltpu.VMEM((tt, cg), F32)] + ride_sems,
        compiler_params=_cparams(("arbitrary", "arbitrary")),
    )(u, u, u, hs, hs, dy, conv_w, conv_b, wa, ba, wx, bx, a_param, *ride)
    return outs[:9], outs[9:]


def _fgate_fwd(f, b_f, *, name):
    t, n = f.shape
    tt = _tile(t, 256)
    width = FOX_HEADS * FOX_HEAD_DIM

    def body(f_ref, b_ref, cum_ref, wide_ref, carry_ref):
        i = pl.program_id(0)

        @pl.when(i == 0)
        def _():
            carry_ref[...] = jnp.zeros_like(carry_ref)

        z = f_ref[...] + b_ref[...]
        lf = jnp.minimum(z, 0.0) - _log1p(jnp.exp(-jnp.abs(z)))
        row = lax.broadcasted_iota(jnp.int32, (tt, tt), 0)
        col = lax.broadcasted_iota(jnp.int32, (tt, tt), 1)
        tri = (col <= row).astype(BF16)
        cum = _dot_01_left(tri, lf) + carry_ref[0:1, :]
        cum_ref[...] = cum
        carry_ref[0:1, :] = cum[tt - 1:tt, :]
        head = lax.broadcasted_iota(jnp.int32, (n, width), 0)
        chan = lax.broadcasted_iota(jnp.int32, (n, width), 1) // FOX_HEAD_DIM
        wide_ref[...] = _dot_01_right(cum, (head == chan).astype(BF16))

    return pl.pallas_call(
        body, name=name, grid=(t // tt,),
        in_specs=[pl.BlockSpec((tt, n), lambda i: (i, 0)), pl.BlockSpec((1, n), lambda i: (0, 0))],
        out_specs=[pl.BlockSpec((tt, n), lambda i: (i, 0)), pl.BlockSpec((tt, width), lambda i: (i, 0))],
        out_shape=[jax.ShapeDtypeStruct((t, n), F32), jax.ShapeDtypeStruct((t, width), F32)],
        scratch_shapes=[pltpu.VMEM((SUBLANES, n), F32)],
        compiler_params=_cparams(("arbitrary",)),
    )(f, b_f)


def _fgate_bwd(dcum, f, b_f, *, name):
    t, n = f.shape
    tt = _tile(t, 256)
    nt = t // tt

    def body(dc_ref, f_ref, b_ref, df_ref, db_ref, carry_ref):
        i = pl.program_id(0)

        @pl.when(i == 0)
        def _():
            carry_ref[...] = jnp.zeros_like(carry_ref)
            db_ref[...] = jnp.zeros_like(db_ref)

        row = lax.broadcasted_iota(jnp.int32, (tt, tt), 0)
        col = lax.broadcasted_iota(jnp.int32, (tt, tt), 1)
        triu = (col >= row).astype(BF16)
        dlf = _dot_01_left(triu, dc_ref[...]) + carry_ref[0:1, :]
        carry_ref[0:1, :] = dlf[0:1, :]
        z = f_ref[...] + b_ref[...]
        df = dlf * _sigmoid(-z)
        df_ref[...] = df
        db_ref[...] += jnp.sum(df, axis=0, keepdims=True)

    blk = pl.BlockSpec((tt, n), lambda i: (nt - 1 - i, 0))
    vec = pl.BlockSpec((1, n), lambda i: (0, 0))
    return pl.pallas_call(
        body, name=name, grid=(nt,),
        in_specs=[blk, blk, vec], out_specs=[blk, vec],
        out_shape=[jax.ShapeDtypeStruct((t, n), F32), jax.ShapeDtypeStruct((1, n), F32)],
        scratch_shapes=[pltpu.VMEM((SUBLANES, n), F32)],
        compiler_params=_cparams(("arbitrary",)),
    )(dcum, f, b_f)


def _attn_fwd(start, qkv, ckt, gate, *, name, tq, tk):
    t = qkv.shape[0]
    f = gate.shape[1]
    npair = f // LANES
    nq = t // tq
    ratio = tq // tk
    assert tq == ratio * tk and t == nq * tq
    scale = 1.0 / math.sqrt(FOX_HEAD_DIM)
    nt_dims = (((1,), (1,)), ((), ()))

    def body(start_ref, q_ref, k_ref, v_ref, ck_ref, g_ref, o_ref, y_ref, l_ref, acc_a, acc_b):
        accs = (acc_a, acc_b)
        i = pl.program_id(1)
        pair = pl.program_id(0)
        firsts = (start_ref[2 * pair, i], start_ref[2 * pair + 1, i])
        both = jnp.maximum(firsts[0], firsts[1])
        lane = lax.broadcasted_iota(jnp.int32, (tq, LANES), 1)
        lo = lane < FOX_HEAD_DIM
        q2 = q_ref[...] * scale
        qs = (jnp.where(lo, q2, 0).astype(BF16), jnp.where(lo, 0, q2).astype(BF16))
        row = lax.broadcasted_iota(jnp.int32, (tq, tk), 0)
        col = lax.broadcasted_iota(jnp.int32, (tq, tk), 1)

        def kv_step(j, carry, diag, heads=(0, 1)):
            off = pl.multiple_of(j * tk, tk)
            kj = k_ref[pl.ds(off, tk), :]
            vj = v_ref[pl.ds(off, tk), :]
            ck = ck_ref[:, pl.ds(off, tk)]
            new = list(carry)
            for h in heads:
                m, l = carry[h]
                s = lax.dot_general(qs[h], kj, nt_dims, preferred_element_type=F32) - ck[h:h + 1, :]
                if diag is not None:
                    s = jnp.where(col + diag * tk <= row, s, NEG_INF)
                m_new = jnp.maximum(m, jnp.max(s, axis=-1, keepdims=True))
                alpha = jnp.exp(m - m_new)
                p = jnp.exp(s - m_new)
                l = alpha * l + jnp.sum(p, axis=-1, keepdims=True)
                accs[h][...] = alpha * accs[h][...] + jnp.dot(p.astype(BF16), vj, preferred_element_type=F32)
                new[h] = (m_new, l)
            return tuple(new)

        carry = tuple((jnp.full((tq, 1), NEG_INF, F32), jnp.zeros((tq, 1), F32)) for _ in range(2))
        acc_a[...] = jnp.zeros_like(acc_a)
        acc_b[...] = jnp.zeros_like(acc_b)

        def run(lo_blk, hi_blk, carry, heads):
            twos = (hi_blk - lo_blk) // 2
            carry = lax.fori_loop(
                0, twos,
                lambda jj, c: kv_step(lo_blk + 2 * jj + 1, kv_step(lo_blk + 2 * jj, c, None, heads), None, heads),
                carry)
            return lax.fori_loop(lo_blk + 2 * twos, hi_blk, lambda j, c: kv_step(j, c, None, heads), carry)

        for h in range(2):
            carry = lax.fori_loop(firsts[h], both, lambda j, c, h=h: kv_step(j, c, None, (h,)), carry)
        carry = run(both, i * ratio, carry, (0, 1))
        for d in range(ratio):
            carry = kv_step(i * ratio + d, carry, d)
        (m0, l0), (m1, l1) = carry
        o = jnp.where(lo, acc_a[...] / l0, acc_b[...] / l1)
        o_ref[...] = o
        gate_v = g_ref[...]
        y_ref[...] = (o * (gate_v * _sigmoid(gate_v))).astype(y_ref.dtype)
        lse_t = jnp.transpose(jnp.where(lo, m0 + jnp.log(l0), m1 + jnp.log(l1)))
        l_ref[0:1, :] = lse_t[0:1, :]
        l_ref[1:2, :] = lse_t[FOX_HEAD_DIM:FOX_HEAD_DIM + 1, :]

    blk = lambda base: pl.BlockSpec((tq, LANES), lambda p, i, s: (i, base + p))
    full = lambda base: pl.BlockSpec((t, LANES), lambda p, i, s: (0, base + p))
    return pl.pallas_call(
        body, name=name,
        grid_spec=pltpu.PrefetchScalarGridSpec(
            num_scalar_prefetch=1, grid=(npair, nq),
            in_specs=[blk(0), full(npair), full(2 * npair), pl.BlockSpec((None, 2, t), lambda p, i, s: (p, 0, 0)),
                      blk(0)],
            out_specs=[blk(0), blk(0), pl.BlockSpec((None, 2, tq), lambda p, i, s: (p, 0, i))],
            scratch_shapes=[pltpu.VMEM((tq, LANES), F32), pltpu.VMEM((tq, LANES), F32)]),
        out_shape=[jax.ShapeDtypeStruct((t, f), F32), jax.ShapeDtypeStruct((t, f), BF16),
                   jax.ShapeDtypeStruct((npair, 2, t), F32)],
        compiler_params=_cparams(("parallel", "arbitrary")),
    )(start, qkv, qkv, qkv, ckt, gate)


def _attn_bwd(end, qkv, do, lt, dt, cke, *, name):
    t, f = do.shape
    npair = f // LANES
    tk = _tile(t, ATTN_TILE)
    nk = t // tk
    scale = 1.0 / math.sqrt(FOX_HEAD_DIM)
    nt_dims = (((1,), (1,)), ((), ()))
    tn_dims = (((0,), (0,)), ((), ()))

    def body(end_ref, k_ref, v_ref, q_ref, do_ref, l_ref, d_ref, ck_ref, dq_out_ref, dk_ref, dv_ref, dck_ref, dcq_ref,
             dq_ref, dk_s, dv_s, dck_s):
        j = pl.program_id(1)
        pair = pl.program_id(0)
        lasts = (end_ref[2 * pair, j], end_ref[2 * pair + 1, j])
        both = jnp.minimum(lasts[0], lasts[1])

        @pl.when(j == 0)
        def _():
            dq_ref[...] = jnp.zeros_like(dq_ref)
            dcq_ref[...] = jnp.zeros_like(dcq_ref)

        lane = lax.broadcasted_iota(jnp.int32, (tk, LANES), 1)
        lo = lane < FOX_HEAD_DIM
        sel = (lo, jnp.logical_not(lo))
        kj = k_ref[...]
        vj = v_ref[...]
        km = tuple(jnp.where(sel[h], kj, 0).astype(BF16) for h in range(2))
        ckv = ck_ref[...]
        ckh = (ckv[:, 0:1], ckv[:, FOX_HEAD_DIM:FOX_HEAD_DIM + 1])
        row = lax.broadcasted_iota(jnp.int32, (tk, tk), 0)
        col = lax.broadcasted_iota(jnp.int32, (tk, tk), 1)
        causal = row <= col

        def q_step(i, carry, masked, heads=(0, 1)):
            off = pl.multiple_of(i * tk, tk)
            qi = q_ref[pl.ds(off, tk), :]
            doi = do_ref[pl.ds(off, tk), :]
            lrow = l_ref[:, pl.ds(off, tk)]
            drow = d_ref[:, pl.ds(off, tk)]
            dq_add = jnp.zeros((tk, LANES), F32)
            for h in heads:
                qm = jnp.where(sel[h], qi, 0).astype(BF16)
                dom = jnp.where(sel[h], doi, 0).astype(BF16)
                st = lax.dot_general(kj, qm, nt_dims, preferred_element_type=F32) * scale
                st = st - ckh[h] - lrow[h:h + 1, :]
                if masked:
                    st = jnp.where(causal, st, NEG_INF)
                pt = jnp.exp(st)
                dpt = lax.dot_general(vj, dom, nt_dims, preferred_element_type=F32)
                dst = pt * (dpt - drow[h:h + 1, :])
                ptb = pt.astype(BF16)
                dstb = dst.astype(BF16)
                dv_s[...] += jnp.dot(ptb, dom, preferred_element_type=F32)
                dk_s[...] += jnp.dot(dstb, qm, preferred_element_type=F32)
                dq_add = dq_add + lax.dot_general(dstb, km[h], tn_dims, preferred_element_type=F32)
                dck_s[:, h:h + 1] -= jnp.sum(dst, axis=-1, keepdims=True)
                dcq_ref[h:h + 1, pl.ds(off, tk)] += jnp.sum(dst, axis=0, keepdims=True)
            dq_ref[pl.ds(off, tk), :] += dq_add * scale
            return carry

        dk_s[...] = jnp.zeros_like(dk_s)
        dv_s[...] = jnp.zeros_like(dv_s)
        dck_s[...] = jnp.zeros_like(dck_s)
        carry = 0
        carry = q_step(j, carry, True)
        carry = lax.fori_loop(j + 1, both, lambda i, c: q_step(i, c, False), carry)
        for h in range(2):
            carry = lax.fori_loop(both, lasts[h], lambda i, c, h=h: q_step(i, c, False, (h,)), carry)
        dk_acc, dv_acc = dk_s[...], dv_s[...]
        dck = (dck_s[:, 0:1], dck_s[:, 1:2])
        dk_ref[...] = (dk_acc * scale).astype(dk_ref.dtype)
        dv_ref[...] = dv_acc.astype(dv_ref.dtype)
        dck_t = jnp.transpose(jnp.where(lo, dck[0], dck[1]))
        dck_ref[0:1, :] = dck_t[0:1, :]
        dck_ref[1:2, :] = dck_t[FOX_HEAD_DIM:FOX_HEAD_DIM + 1, :]

        @pl.when(j == nk - 1)
        def _():
            dq_out_ref[...] = dq_ref[...].astype(dq_out_ref.dtype)

    blk = lambda base: pl.BlockSpec((tk, LANES), lambda p, j, e: (j, base + p))
    full = lambda base: pl.BlockSpec((t, LANES), lambda p, j, e: (0, base + p))
    rows = pl.BlockSpec((None, 2, t), lambda p, j, e: (p, 0, 0))
    return pl.pallas_call(
        body, name=name,
        grid_spec=pltpu.PrefetchScalarGridSpec(
            num_scalar_prefetch=1, grid=(npair, nk),
            in_specs=[blk(npair), blk(2 * npair), full(0), full(0), rows, rows, blk(0)],
            out_specs=[full(0), blk(0), blk(0), pl.BlockSpec((None, 2, tk), lambda p, j, e: (p, 0, j)), rows],
            scratch_shapes=[pltpu.VMEM((t, LANES), F32), pltpu.VMEM((tk, LANES), F32), pltpu.VMEM((tk, LANES), F32),
                            pltpu.VMEM((tk, LANES), F32)]),
        out_shape=[jax.ShapeDtypeStruct((t, f), BF16), jax.ShapeDtypeStruct((t, f), BF16),
                   jax.ShapeDtypeStruct((t, f), BF16), jax.ShapeDtypeStruct((npair, 2, t), F32),
                   jax.ShapeDtypeStruct((npair, 2, t), F32)],
        compiler_params=_cparams(("parallel", "arbitrary")),
    )(end, qkv, qkv, qkv, do, lt, dt, cke)


ATTN_TILE = 512
ATTN_FWD_QUERIES = 512
EXP_ZERO = -104.0
BOUND_SLACK = 1.02


def _attn_row_stats(qkv, *, name):
    t = qkv.shape[0]
    f = qkv.shape[1] // 3
    tt = _tile(t, 512)

    def body(q_ref, k_ref, s_ref):
        q = q_ref[...].astype(F32)
        k = k_ref[...].astype(F32)
        chan = lax.broadcasted_iota(jnp.int32, (f, LANES), 0) // FOX_HEAD_DIM
        lane = lax.broadcasted_iota(jnp.int32, (f, LANES), 1)
        acc = jnp.zeros((tt, LANES), F32)
        for off, val in ((0, q * q), (FOX_HEADS, q * k), (2 * FOX_HEADS, k * k)):
            pick = (chan == lane - off).astype(BF16)
            acc = acc + jnp.dot(val.astype(BF16), pick, preferred_element_type=F32)
        s_ref[...] = acc

    return pl.pallas_call(
        body, name=name, grid=(t // tt,),
        in_specs=[pl.BlockSpec((tt, f), lambda i: (i, 0)), pl.BlockSpec((tt, f), lambda i: (i, 1))],
        out_specs=pl.BlockSpec((tt, LANES), lambda i: (i, 0)),
        out_shape=jax.ShapeDtypeStruct((t, LANES), F32),
        compiler_params=_cparams(("parallel",)),
    )(qkv, qkv)


def _attn_skip_tables(stats, cum16, tile):
    t = stats.shape[0]
    nb = t // tile
    scale = 1.0 / math.sqrt(FOX_HEAD_DIM)
    qn = jnp.sqrt(stats[:, :FOX_HEADS]) * scale
    sii = stats[:, FOX_HEADS:2 * FOX_HEADS] * scale - cum16
    kmax = jnp.max(jnp.sqrt(stats[:, 2 * FOX_HEADS:3 * FOX_HEADS]), axis=0, keepdims=True)
    arow = qn * kmax * BOUND_SLACK - sii + 0.5 * BOUND_SLACK
    a_blk = jnp.max(arow.reshape(nb, tile, FOX_HEADS), axis=1)
    c_blk = -cum16.reshape(nb, tile, FOX_HEADS)[:, tile - 1, :]
    dead = (a_blk[:, None, :] + c_blk[None, :, :]) < EXP_ZERO
    start_h = jnp.sum(dead.astype(jnp.int32), axis=1)
    blk = jnp.arange(nb, dtype=jnp.int32)
    start = jnp.minimum(start_h, blk[:, None]).T
    needs = start[:, :, None] <= blk[None, None, :]
    end = jnp.max(jnp.where(needs, blk[None, :, None] + 1, 0), axis=1)
    return start, jnp.maximum(end, blk[None, :] + 1)


def _fox_post_bwd(dy, o, gate, *, name):
    t, f = dy.shape
    tt = _tile(t, 512)

    def body(dy_ref, o_ref, g_ref, do_ref, dg_ref, dl_ref):
        g = g_ref[...]
        sg = _sigmoid(g)
        dyv = dy_ref[...]
        ov = o_ref[...]
        do = dyv * (g * sg)
        do_ref[...] = do.astype(do_ref.dtype)
        dg_ref[...] = (dyv * ov * (sg * (1.0 + g * (1.0 - sg)))).astype(dg_ref.dtype)
        chan = lax.broadcasted_iota(jnp.int32, (f, LANES), 0)
        head = lax.broadcasted_iota(jnp.int32, (f, LANES), 1)
        pick = (chan // FOX_HEAD_DIM == head).astype(BF16)
        dl_ref[...] = _dot_01_right(do * ov, pick)

    blk = pl.BlockSpec((tt, f), lambda i: (i, 0))
    return pl.pallas_call(
        body, name=name, grid=(t // tt,),
        in_specs=[blk, blk, blk], out_specs=[blk, blk, pl.BlockSpec((tt, LANES), lambda i: (i, 0))],
        out_shape=[jax.ShapeDtypeStruct((t, f), BF16), jax.ShapeDtypeStruct((t, f), BF16),
                   jax.ShapeDtypeStruct((t, LANES), F32)],
        compiler_params=_cparams(("parallel",)),
    )(dy, o, gate)


def _adamw(w, g, m, v, *, name):
    _, r, c = w.shape
    tr = _tile(r, 256) if r % SUBLANES == 0 else r
    c1 = 1.0 - ADAM_B1 ** ADAM_STEP
    c2 = 1.0 - ADAM_B2 ** ADAM_STEP

    def body(w_ref, g_ref, m_ref, v_ref, go_ref, d_ref, mo_ref, vo_ref):
        gv = g_ref[...]
        go_ref[...] = gv
        mn = ADAM_B1 * m_ref[...] + (1.0 - ADAM_B1) * gv
        vn = ADAM_B2 * v_ref[...] + (1.0 - ADAM_B2) * (gv * gv)
        mo_ref[...] = mn
        vo_ref[...] = vn
        d_ref[...] = -ADAM_LR * ((mn / c1) / (jnp.sqrt(vn / c2) + ADAM_EPS) + ADAM_WD * w_ref[...])

    blk = pl.BlockSpec((None, tr, c), lambda i: (0, i, 0))
    return pl.pallas_call(
        body, name=name, grid=(r // tr,), in_specs=[blk] * 4, out_specs=[blk] * 4,
        out_shape=[jax.ShapeDtypeStruct((1, r, c), F32)] * 4,
        compiler_params=_cparams(("parallel",)),
    )(w, g, m, v)


def _sum_slots(land, *, name):
    ns, r, c = land.shape
    tr = _tile(r, 64) if r % SUBLANES == 0 else r

    def body(l_ref, o_ref):
        acc = l_ref[0].astype(F32)
        for s in range(1, ns):
            acc = acc + l_ref[s].astype(F32)
        o_ref[...] = acc

    return pl.pallas_call(
        body, name=name, grid=(r // tr,),
        in_specs=[pl.BlockSpec((ns, tr, c), lambda i: (0, i, 0))],
        out_specs=pl.BlockSpec((tr, c), lambda i: (i, 0)),
        out_shape=jax.ShapeDtypeStruct((r, c), F32),
        compiler_params=_cparams(("parallel",)),
    )(land)


ANY = pl.BlockSpec(memory_space=pl.ANY)


def _flip(v, bit):
    return 1 - v if bit else v


def _gather_chips(shards, small, *, name):
    n = len(shards)
    rels = ((1, 0), (0, 1), (1, 1))

    def body(*refs):
        ins, small_in = refs[:n], refs[n]
        outs, small_out = refs[n + 1:2 * n + 1], refs[2 * n + 1]
        send, recv, loc = refs[2 * n + 2:]
        x, y, c = lax.axis_index("x"), lax.axis_index("y"), lax.axis_index("c")
        me = 2 * x + y
        sibling = (x, y, 1 - c)
        local = [pltpu.make_async_copy(ins[k], outs[k].at[me], loc.at[k]) for k in range(n)]
        local.append(pltpu.make_async_copy(small_in, small_out.at[me], loc.at[n]))
        for cp in local:
            cp.start()

        def rows(k):
            half = ins[k].shape[0] // 2
            return pl.ds(pl.multiple_of(c * half, SUBLANES), half)

        sends = []
        for r, (rx, ry) in enumerate(rels):
            to = (_flip(x, rx), _flip(y, ry), c)
            for k in range(n):
                cp = pltpu.make_async_remote_copy(
                    src_ref=ins[k].at[rows(k), :], dst_ref=outs[k].at[me, rows(k), :],
                    send_sem=send.at[r * n + k], recv_sem=recv.at[r * n + k], device_id=to, device_id_type=MESH)
                cp.start()
                sends.append(cp)
            cp = pltpu.make_async_remote_copy(
                src_ref=small_in, dst_ref=small_out.at[me], send_sem=send.at[6 * n + r], recv_sem=recv.at[6 * n + r],
                device_id=to, device_id_type=MESH)
            cp.start()
            sends.append(cp)
        for r, (rx, ry) in enumerate(rels):
            src_chip = 2 * _flip(x, rx) + _flip(y, ry)
            for k in range(n):
                landed = outs[k].at[src_chip, rows(k), :]
                sends[r * (n + 1) + k].wait_recv()
                cp = pltpu.make_async_remote_copy(
                    src_ref=landed, dst_ref=landed, send_sem=send.at[3 * n + r * n + k],
                    recv_sem=recv.at[3 * n + r * n + k], device_id=sibling, device_id_type=MESH)
                cp.start()
                sends.append(cp)
            sends[r * (n + 1) + n].wait_recv()
        for cp in sends[:3 * (n + 1)]:
            cp.wait_send()
        for cp in sends[3 * (n + 1):]:
            cp.wait()
        for cp in local:
            cp.wait()

    vmem = pl.BlockSpec(memory_space=pltpu.VMEM)
    return pl.pallas_call(
        body, name=name, in_specs=[vmem] * (n + 1), out_specs=[vmem] * (n + 1),
        out_shape=[jax.ShapeDtypeStruct((N_CHIPS,) + s.shape, s.dtype) for s in list(shards) + [small]],
        scratch_shapes=[pltpu.SemaphoreType.DMA((6 * n + 3,)), pltpu.SemaphoreType.DMA((6 * n + 3,)),
                        pltpu.SemaphoreType.DMA((n + 1,))],
        compiler_params=pltpu.CompilerParams(has_side_effects=True, vmem_limit_bytes=VMEM_LIMIT),
    )(*shards, small)


_RELS7 = tuple((r >> 2 & 1, r >> 1 & 1, r & 1) for r in range(1, N_DEV))


def _scatter_copies(ins, outs, send, recv, loc):
    n = len(ins)
    x, y, c = lax.axis_index("x"), lax.axis_index("y"), lax.axis_index("c")
    me = 4 * x + 2 * y + c

    def piece(k, px, py, pc):
        half = ins[k].shape[1] // 2
        return ins[k].at[2 * px + py, pl.ds(pc * half, half), :]

    copies = [pltpu.make_async_copy(piece(k, x, y, c), outs[k].at[me], loc.at[k]) for k in range(n)]
    for r, (rx, ry, rc) in enumerate(_RELS7):
        tx, ty, tc = _flip(x, rx), _flip(y, ry), _flip(c, rc)
        for k in range(n):
            copies.append(pltpu.make_async_remote_copy(
                src_ref=piece(k, tx, ty, tc), dst_ref=outs[k].at[me], send_sem=send.at[r * n + k],
                recv_sem=recv.at[r * n + k], device_id=(tx, ty, tc), device_id_type=MESH))
    return copies


def _scatter_shapes(grads):
    n = len(grads)
    out_shape = [jax.ShapeDtypeStruct((N_DEV, g.shape[1] // 2, g.shape[2]), g.dtype) for g in grads]
    sems = [pltpu.SemaphoreType.DMA((7 * n,)), pltpu.SemaphoreType.DMA((7 * n,)), pltpu.SemaphoreType.DMA((n,))]
    return out_shape, sems


def _join_cores(halves, *, name):
    n = len(halves)

    def body(*refs):
        ins, outs = refs[:n], refs[n:2 * n]
        send, recv, loc = refs[2 * n:]
        x, y, c = lax.axis_index("x"), lax.axis_index("y"), lax.axis_index("c")
        copies = []
        for k in range(n):
            half = ins[k].shape[0]
            mine = outs[k].at[0, pl.ds(c * half, half), :]
            cp = pltpu.make_async_copy(ins[k], mine, loc.at[k])
            cp.start()
            copies.append(cp)
            cp = pltpu.make_async_remote_copy(
                src_ref=ins[k], dst_ref=mine, send_sem=send.at[k], recv_sem=recv.at[k],
                device_id=(x, y, 1 - c), device_id_type=MESH)
            cp.start()
            copies.append(cp)
        for cp in copies:
            cp.wait()

    in_vmem = pl.BlockSpec(memory_space=pltpu.VMEM)
    return pl.pallas_call(
        body, name=name, in_specs=[in_vmem] * n, out_specs=[in_vmem] * n,
        out_shape=[jax.ShapeDtypeStruct((1, 2 * h.shape[0], h.shape[1]), h.dtype) for h in halves],
        scratch_shapes=[pltpu.SemaphoreType.DMA((n,)), pltpu.SemaphoreType.DMA((n,)), pltpu.SemaphoreType.DMA((n,))],
        compiler_params=pltpu.CompilerParams(has_side_effects=True, vmem_limit_bytes=VMEM_LIMIT),
    )(*halves)


def _allreduce_small(buf, *, name):
    r, n = buf.shape
    half = r // 2
    rels = ((1, 0), (0, 1), (1, 1))

    def body(in_ref, out_ref, sib_ref, chips_ref, send, recv):
        x, y, c = lax.axis_index("x"), lax.axis_index("y"), lax.axis_index("c")
        sibling = (x, y, 1 - c)
        chip = 2 * x + y
        rows = pl.ds(pl.multiple_of(c * half, SUBLANES), half)

        swap = pltpu.make_async_remote_copy(src_ref=in_ref, dst_ref=sib_ref, send_sem=send.at[0], recv_sem=recv.at[0],
                                            device_id=sibling, device_id_type=MESH)
        swap.start()
        swap.wait()
        chips_ref[chip] = in_ref[rows, :] + sib_ref[rows, :]

        sends = []
        for k, (rx, ry) in enumerate(rels):
            cp = pltpu.make_async_remote_copy(
                src_ref=chips_ref.at[chip], dst_ref=chips_ref.at[chip], send_sem=send.at[1 + k],
                recv_sem=recv.at[1 + k], device_id=(_flip(x, rx), _flip(y, ry), c), device_id_type=MESH)
            cp.start()
            sends.append(cp)
        for cp in sends:
            cp.wait()
        total = chips_ref[0]
        for s in range(1, N_CHIPS):
            total = total + chips_ref[s]
        out_ref[rows, :] = total

        back = pltpu.make_async_remote_copy(src_ref=out_ref.at[rows, :], dst_ref=out_ref.at[rows, :],
                                            send_sem=send.at[4], recv_sem=recv.at[4],
                                            device_id=sibling, device_id_type=MESH)
        back.start()
        back.wait()

    vmem = pl.BlockSpec(memory_space=pltpu.VMEM)
    return pl.pallas_call(
        body, name=name, in_specs=[vmem], out_specs=vmem,
        out_shape=jax.ShapeDtypeStruct((r, n), F32),
        scratch_shapes=[pltpu.VMEM((r, n), F32), pltpu.VMEM((N_CHIPS, half, n), F32),
                        pltpu.SemaphoreType.DMA((5,)), pltpu.SemaphoreType.DMA((5,))],
        compiler_params=pltpu.CompilerParams(has_side_effects=True, vmem_limit_bytes=VMEM_LIMIT),
    )(buf)


def _pack(arrs):
    flat = []
    for a in arrs:
        v = a.reshape(-1)
        pad = (-v.shape[0]) % LANES
        if pad:
            v = jnp.pad(v, (0, pad))
        flat.append(v)
    v = jnp.concatenate(flat)
    pad = (-v.shape[0]) % (LANES * SUBLANES)
    if pad:
        v = jnp.pad(v, (0, pad))
    return v.reshape(-1, LANES)


def _unpack(buf, shapes):
    v = buf.reshape(-1)
    out, off = [], 0
    for s in shapes:
        n = math.prod(s)
        out.append(v[off:off + n].reshape(s))
        off += n + (-n) % LANES
    return out


def kernel(x, norm_g, final_g, lru_w_in, lru_conv_w, lru_conv_b, lru_wa, lru_ba, lru_wx, lru_bx, lru_a_param, lru_w_out, fox_w_in, fox_b_f, fox_w_out, loss_target, m_norm_g, m_final_g, m_lru_w_in, m_lru_conv_w, m_lru_conv_b, m_lru_wa, m_lru_ba, m_lru_wx, m_lru_bx, m_lru_a_param, m_lru_w_out, m_fox_w_in, m_fox_b_f, m_fox_w_out, v_norm_g, v_final_g, v_lru_w_in, v_lru_conv_w, v_lru_conv_b, v_lru_wa, v_lru_ba, v_lru_wx, v_lru_bx, v_lru_a_param, v_lru_w_out, v_fox_w_in, v_fox_b_f, v_fox_w_out):
    t, d = x.shape[1], x.shape[2]
    w = lru_wa.shape[1] * LRU_BLOCK_W
    f = FOX_HEADS * FOX_HEAD_DIM
    npair = f // LANES
    x0 = x.reshape(t, d)
    tgt = loss_target.reshape(t, d)
    chip = 2 * lax.axis_index("x") + lax.axis_index("y")

    g_lwi, g_lwo, g_cw = _gather_chips(
        [lru_w_in[0].astype(BF16), lru_w_out[0].astype(BF16)], lru_conv_w[0], name="gather_weights")
    cg = w // 2
    lwi = jnp.concatenate([g_lwi[0], g_lwi[2], g_lwi[1], g_lwi[3]], axis=1)
    lwo = g_lwo.reshape(w, d)
    conv_w = jnp.concatenate([g_cw[s] for s in range(N_CHIPS)], axis=1)
    conv_b, ba, bx, a_param = lru_conv_b, lru_ba, lru_bx, lru_a_param
    wa, wx = lru_wa[0], lru_wx[0]
    b_f = jnp.pad(fox_b_f, ((0, 0), (0, LANES - FOX_HEADS)))

    h0 = _rmsnorm(x0, norm_g[0], name="norm0")
    u = _matmul(h0, lwi, name="lru_in")
    y1, hs, (g_fwi, g_fwo) = _lru_fwd(u, conv_w, conv_b, wa, ba, wx, bx, a_param, cg=cg, name="lru_fwd",
                                      ride=[fox_w_in[0].astype(BF16), fox_w_out[0].astype(BF16)])
    fwi = jnp.concatenate([g_fwi[s] for s in range(N_CHIPS)], axis=1)
    w_qkv, w_g2 = fwi[:, :3 * f], fwi[:, 3 * f:4 * f]
    w_f = jnp.pad(fwi[:, 4 * f:], ((0, 0), (0, LANES - FOX_HEADS)))
    fwo = g_fwo.reshape(f, d)
    x1 = _matmul(y1, lwo, add=x0, name="lru_out")
    h1 = _rmsnorm(x1, norm_g[1], name="norm1")
    qkv = _matmul(h1, w_qkv, out_dtype=BF16, name="fox_qkv")
    gate2 = _matmul(h1, w_g2, name="fox_gate")
    flog = _matmul(h1, w_f, name="fox_f")
    cum, cke = _fgate_fwd(flog, b_f, name="fgate_fwd")
    cum16 = cum[:, :FOX_HEADS]
    ckt = cum16.T.reshape(npair, 2, t)
    a_tk, a_tq = _tile(t, ATTN_TILE), _tile(t, ATTN_FWD_QUERIES)
    a_start, a_end = _attn_skip_tables(_attn_row_stats(qkv, name="attn_row_stats"), cum16, a_tk)
    a_start_fwd = jnp.min(a_start.reshape(FOX_HEADS, t // a_tq, a_tq // a_tk), axis=2)
    o, y2, lse = _attn_fwd(a_start_fwd, qkv, ckt, gate2, name="attn_fwd", tq=a_tq, tk=a_tk)
    x2 = _matmul(y2, fwo, add=x1, name="fox_out")
    lsum, dx2, dx2_b, dgf = _final_loss(x2, tgt, final_g, name="final_loss")
    loss = lax.psum(0.5 * jnp.sum(lsum) / d, ("x", "y", "c"))

    d_fwo = _matmul(y2, dx2_b, ta=True, out_dtype=BF16, name="d_fox_w_out")
    dy2 = _matmul(dx2_b, fwo, tb=True, name="d_y2")
    do, dgate2, dl = _fox_post_bwd(dy2, o, gate2, name="fox_post_bwd")
    lt = lse
    dt = dl[:, :FOX_HEADS].T.reshape(npair, 2, t)
    dq, dk, dv, dck, dcq = _attn_bwd(a_end, qkv, do, lt, dt, cke, name="attn_bwd")
    dcum = jnp.pad((dck + dcq).reshape(FOX_HEADS, t).T, ((0, 0), (0, LANES - FOX_HEADS)))
    dflog, db_f = _fgate_bwd(dcum, flog, b_f, name="fgate_bwd")
    du2 = [dq, dk, dv, dgate2]
    dflog_b = dflog.astype(BF16)
    dh1 = _matmul_kparts(du2, fwi[:, :4 * f], chunk=f, name="d_h1_a")
    dh1 = _matmul(dflog_b, w_f, tb=True, add=dh1, name="d_h1_b")
    d_fwi_a = _matmul_nparts(h1, du2, chunk=f, out_dtype=BF16, name="d_fox_w_in_a")
    d_fwi_b = _matmul(h1, dflog_b, ta=True, out_dtype=BF16, name="d_fox_w_in_b")
    d_fwi = jnp.concatenate([d_fwi_a, d_fwi_b[:, :FOX_HEADS]], axis=1)
    dx1, dx1_b, dg1 = _rmsnorm_bwd(dh1, x1, norm_g[1], dx2, name="norm1_bwd", bf16_copy=True)

    d_lwo = _matmul(y1, dx1_b, ta=True, out_dtype=BF16, name="d_lru_w_out")
    dy1 = _matmul(dx1_b, lwo, tb=True, name="d_y1")
    n_fwi = fox_w_in.shape[2]
    g_fwi4 = jnp.stack([d_fwi[:, s * n_fwi:(s + 1) * n_fwi] for s in range(N_CHIPS)])
    g_fwo4 = d_fwo.reshape(N_CHIPS, f // N_CHIPS, d)
    g_lwo4 = d_lwo.reshape(N_CHIPS, w // N_CHIPS, d)
    (dxb, dgate, d_cw, d_cb, d_wa, d_ba, d_wx, d_bx, d_ap), lands_early = _lru_bwd(
        u, hs, dy1, conv_w, conv_b, wa, ba, wx, bx, a_param, cg=cg, name="lru_bwd", ride=[g_lwo4, g_fwi4, g_fwo4])
    d_lwi_p = _matmul_nparts(h0, [dxb, dgate], chunk=cg, out_dtype=BF16, name="d_lru_w_in")
    csz = cg
    g_lwi4 = jnp.stack([d_lwi_p[:, 0:csz], d_lwi_p[:, 2 * csz:3 * csz], d_lwi_p[:, csz:2 * csz],
                        d_lwi_p[:, 3 * csz:]])
    dh0, lands_last = _matmul_kparts([dxb, dgate], lwi, chunk=cg, name="d_h0", ride=[g_lwi4])
    dx0, _, dg0 = _rmsnorm_bwd(dh0, x0, norm_g[0], dx1, name="norm0_bwd", bf16_copy=False)
    lands = list(lands_last) + list(lands_early)
    halves = [_sum_slots(l, name="sum_" + nm) for l, nm in zip(lands, ("lru_w_in", "lru_w_out", "fox_w_in", "fox_w_out"))]
    big_g = _join_cores(halves, name="join_cores")

    small_g = [jnp.concatenate([dg0, dg1], axis=0), dgf.reshape(d), d_cw, d_cb, d_wa, d_ba, d_wx, d_bx, d_ap,
               db_f[:, :FOX_HEADS]]
    gsum = _allreduce_small(_pack(small_g), name="allreduce_small")
    zc = jnp.zeros((CONV_WIDTH, w), F32)
    pk_w = _pack([norm_g, final_g, zc, lru_conv_b, lru_wa, lru_ba, lru_wx, lru_bx, lru_a_param, fox_b_f])
    pk_m = _pack([m_norm_g, m_final_g, zc, m_lru_conv_b, m_lru_wa, m_lru_ba, m_lru_wx, m_lru_bx, m_lru_a_param,
                  m_fox_b_f])
    pk_v = _pack([v_norm_g, v_final_g, zc + 1.0, v_lru_conv_b, v_lru_wa, v_lru_ba, v_lru_wx, v_lru_bx,
                  v_lru_a_param, v_fox_b_f])
    s_g, s_delta, s_m, s_v = _adamw(pk_w[None], gsum[None], pk_m[None], pk_v[None], name="adamw_small")
    out_shapes = [norm_g.shape, final_g.shape, (CONV_WIDTH, w), lru_conv_b.shape, lru_wa.shape, lru_ba.shape,
                  lru_wx.shape, lru_bx.shape, lru_a_param.shape, fox_b_f.shape]
    sg = _unpack(s_g, out_shapes)
    sd = _unpack(s_delta, out_shapes)
    sm = _unpack(s_m, out_shapes)
    sv = _unpack(s_v, out_shapes)

    ncw = lru_conv_w.shape[2]
    g_cw_loc = lax.dynamic_slice_in_dim(sg[2], chip * ncw, ncw, axis=1)
    g_cw_loc, cw_d, cw_m, cw_v = _adamw(lru_conv_w, g_cw_loc[None], m_lru_conv_w, v_lru_conv_w, name="adamw_conv_w")

    big = []
    for nm, wt, g, mm, vv in (("lru_w_in", lru_w_in, big_g[0], m_lru_w_in, v_lru_w_in),
                              ("lru_w_out", lru_w_out, big_g[1], m_lru_w_out, v_lru_w_out),
                              ("fox_w_in", fox_w_in, big_g[2], m_fox_w_in, v_fox_w_in),
                              ("fox_w_out", fox_w_out, big_g[3], m_fox_w_out, v_fox_w_out)):
        big.append(tuple(_adamw(wt, g, mm, vv, name="adamw_" + nm)))

    def assemble(idx):
        small = (sg, sd, sm, sv)[idx]
        cw = (g_cw_loc, cw_d, cw_m, cw_v)[idx]
        return [small[0], small[1], big[0][idx], cw, small[3], small[4], small[5], small[6], small[7], small[8],
                big[1][idx], big[2][idx], small[9], big[3][idx]]

    grad_x = dx0.reshape(1, t, d)
    return (loss, grad_x, *assemble(0), *assemble(1), *assemble(2), *assemble(3))
```

```python
import math

import jax
import jax.numpy as jnp
from jax import lax
from jax.experimental import pallas as pl
from jax.experimental.pallas import tpu as pltpu

F32 = jnp.float32
BF16 = jnp.bfloat16

EPS = 1e-6
LRU_C = 8.0
LRU_BLOCK_W = 128
CONV_WIDTH = 4
FOX_HEADS = 16
FOX_HEAD_DIM = 64
NEG_INF = -1e30
ADAM_LR = 0.001
ADAM_B1 = 0.9
ADAM_B2 = 0.999
ADAM_EPS = 1e-08
ADAM_WD = 0.01
ADAM_STEP = 10

LANES = 128
SUBLANES = 8
VMEM_LIMIT = 56 * 1024 * 1024
TINY = 1e-30
N_CHIPS = 4
N_DEV = 8
MESH = pl.DeviceIdType.MESH


def _tile(n, pref):
    t = min(n, pref)
    while n % t:
        t //= 2
    return t


def _cparams(dims=None):
    return pltpu.CompilerParams(dimension_semantics=dims, vmem_limit_bytes=VMEM_LIMIT)


def _sigmoid(x):
    return 0.5 * jnp.tanh(0.5 * x) + 0.5


def _log1p(x):
    u = 1.0 + x
    return jnp.where(u == 1.0, x, jnp.log(u) * x / (u - 1.0))


def _bf16_pieces(x):
    hi = x.astype(BF16)
    rest = x - hi.astype(F32)
    mid = rest.astype(BF16)
    return hi, mid, (rest - mid.astype(F32)).astype(BF16)


def _dot_01_left(m01, x):
    return sum(jnp.dot(m01, p, preferred_element_type=F32) for p in _bf16_pieces(x))


def _dot_01_right(x, m01):
    return sum(jnp.dot(p, m01, preferred_element_type=F32) for p in _bf16_pieces(x))


def _softplus(x):
    return jnp.maximum(x, 0.0) + _log1p(jnp.exp(-jnp.abs(x)))


MM_TILE = 1024
MM_FULL_K = 1536


def _matmul(a, b, *, name, ta=False, tb=False, out_dtype=F32, add=None, tm=MM_TILE, tn=MM_TILE, tk=None):
    if ta:
        kdim, m = a.shape
    else:
        m, kdim = a.shape
    if tb:
        n, kb = b.shape
    else:
        kb, n = b.shape
    assert kdim == kb, (a.shape, b.shape, ta, tb)
    if tk is None:
        tk = kdim if kdim <= MM_FULL_K else MM_TILE
    tm, tn, tk = _tile(m, tm), _tile(n, tn), _tile(kdim, tk)
    nk = kdim // tk
    dn = (((0 if ta else 1,), (1 if tb else 0,)), ((), ()))
    has_add = add is not None

    def body(*refs):
        if has_add:
            a_ref, b_ref, add_ref, o_ref = refs[:4]
        else:
            a_ref, b_ref, o_ref = refs[:3]
        part = lax.dot_general(a_ref[...].astype(BF16), b_ref[...].astype(BF16), dn, preferred_element_type=F32)

        def finish(r):
            if has_add:
                r = r + add_ref[...].astype(F32)
            o_ref[...] = r.astype(o_ref.dtype)

        if nk == 1:
            finish(part)
            return
        acc_ref = refs[-1]
        k = pl.program_id(2)

        @pl.when(k == 0)
        def _():
            acc_ref[...] = part

        @pl.when(k > 0)
        def _():
            acc_ref[...] += part

        @pl.when(k == nk - 1)
        def _():
            finish(acc_ref[...])

    a_spec = pl.BlockSpec((tk, tm), lambda i, j, k: (k, i)) if ta else pl.BlockSpec((tm, tk), lambda i, j, k: (i, k))
    b_spec = pl.BlockSpec((tn, tk), lambda i, j, k: (j, k)) if tb else pl.BlockSpec((tk, tn), lambda i, j, k: (k, j))
    o_spec = pl.BlockSpec((tm, tn), lambda i, j, k: (i, j))
    in_specs = [a_spec, b_spec] + ([o_spec] if has_add else [])
    args = (a, b) + ((add,) if has_add else ())
    return pl.pallas_call(
        body, name=name, grid=(m // tm, n // tn, nk), in_specs=in_specs, out_specs=o_spec,
        out_shape=jax.ShapeDtypeStruct((m, n), out_dtype),
        scratch_shapes=[pltpu.VMEM((tm, tn), F32)] if nk > 1 else [],
        compiler_params=_cparams(("parallel", "parallel", "arbitrary")),
    )(*args)


def _matmul_kparts(parts, b, *, chunk, name, tm=MM_TILE, tn=MM_TILE, ride=()):
    npart = len(parts)
    nride = len(ride)
    m = parts[0].shape[0]
    n, kdim = b.shape
    nk = kdim // chunk
    assert nk * chunk == kdim and sum(p.shape[1] for p in parts) == kdim and nk % npart == 0
    tm, tn = _tile(m, tm), _tile(n, tn)
    dn = (((1,), (1,)), ((), ()))
    steps = (m // tm) * (n // tn) * nk

    def body(*refs):
        ins, rest = refs[:npart + 1], refs[npart + 1:]
        ride_in, rest = rest[:nride], rest[nride:]
        o_ref, rest = rest[0], rest[1:]
        ride_out, rest = rest[:nride], rest[nride:]
        acc_ref, sems = rest[0], rest[1:]
        if not nride:
            core(*ins, o_ref, acc_ref)
            return
        step = (pl.program_id(0) * (n // tn) + pl.program_id(1)) * nk + pl.program_id(2)

        @pl.when(step == 0)
        def _():
            for cp in _scatter_copies(ride_in, ride_out, *sems):
                cp.start()

        core(*ins, o_ref, acc_ref)

        @pl.when(step == steps - 1)
        def _():
            for cp in _scatter_copies(ride_in, ride_out, *sems):
                cp.wait()

    def core(*refs):
        a_refs, b_ref, o_ref, acc_ref = refs[:npart], refs[npart], refs[npart + 1], refs[npart + 2]
        k = pl.program_id(2)

        @pl.when(k == 0)
        def _():
            acc_ref[...] = jnp.zeros_like(acc_ref)

        for s in range(npart):
            @pl.when(lax.rem(k, npart) == s)
            def _(s=s):
                acc_ref[...] += lax.dot_general(a_refs[s][...].astype(BF16), b_ref[...].astype(BF16), dn,
                                                preferred_element_type=F32)

        @pl.when(k == nk - 1)
        def _():
            o_ref[...] = acc_ref[...].astype(o_ref.dtype)

    a_specs = [pl.BlockSpec((tm, chunk), lambda i, j, k: (i, k // npart)) for _ in range(npart)]
    any_spec = pl.BlockSpec(memory_space=pl.ANY)
    ride_shape, ride_sems = _scatter_shapes(ride) if nride else ([], [])
    outs = pl.pallas_call(
        body, name=name, grid=(m // tm, n // tn, nk),
        in_specs=a_specs + [pl.BlockSpec((tn, chunk), lambda i, j, k: (j, k))] + [any_spec] * nride,
        out_specs=[pl.BlockSpec((tm, tn), lambda i, j, k: (i, j))] + [any_spec] * nride,
        out_shape=[jax.ShapeDtypeStruct((m, n), F32)] + ride_shape,
        scratch_shapes=[pltpu.VMEM((tm, tn), F32)] + ride_sems,
        compiler_params=_cparams(("arbitrary",) * 3 if nride else ("parallel", "parallel", "arbitrary")),
    )(*parts, b, *ride)
    return (outs[0], outs[1:]) if nride else outs[0]


def _matmul_nparts(a, parts, *, chunk, out_dtype, name, tm=MM_TILE, tk=MM_TILE):
    npart = len(parts)
    t, m = a.shape
    n = sum(p.shape[1] for p in parts)
    nj = n // chunk
    assert nj * chunk == n and nj % npart == 0
    tm, tk = _tile(m, tm), _tile(t, tk)
    nk = t // tk
    dn = (((0,), (0,)), ((), ()))

    def body(*refs):
        a_ref, b_refs, o_ref, acc_ref = refs[0], refs[1:1 + npart], refs[1 + npart], refs[2 + npart]
        j, k = pl.program_id(1), pl.program_id(2)

        @pl.when(k == 0)
        def _():
            acc_ref[...] = jnp.zeros_like(acc_ref)

        for s in range(npart):
            @pl.when(lax.rem(j, npart) == s)
            def _(s=s):
                acc_ref[...] += lax.dot_general(a_ref[...].astype(BF16), b_refs[s][...].astype(BF16), dn,
                                                preferred_element_type=F32)

        @pl.when(k == nk - 1)
        def _():
            o_ref[...] = acc_ref[...].astype(o_ref.dtype)

    def b_spec(s):
        return pl.BlockSpec((tk, chunk), lambda i, j, k: (jnp.where(lax.rem(j, npart) == s, k, 0), j // npart))

    return pl.pallas_call(
        body, name=name, grid=(m // tm, nj, nk),
        in_specs=[pl.BlockSpec((tk, tm), lambda i, j, k: (k, i))] + [b_spec(s) for s in range(npart)],
        out_specs=pl.BlockSpec((tm, chunk), lambda i, j, k: (i, j)),
        out_shape=jax.ShapeDtypeStruct((m, n), out_dtype),
        scratch_shapes=[pltpu.VMEM((tm, chunk), F32)],
        compiler_params=_cparams(("parallel", "parallel", "arbitrary")),
    )(a, *parts)


def _rmsnorm(x, g, *, name):
    t, d = x.shape
    tt = _tile(t, 512)

    def body(x_ref, g_ref, o_ref):
        xf = x_ref[...]
        rstd = lax.rsqrt(jnp.mean(xf * xf, axis=-1, keepdims=True) + EPS)
        o_ref[...] = (xf * rstd * g_ref[...]).astype(o_ref.dtype)

    return pl.pallas_call(
        body, name=name, grid=(t // tt,),
        in_specs=[pl.BlockSpec((tt, d), lambda i: (i, 0)), pl.BlockSpec((1, d), lambda i: (0, 0))],
        out_specs=pl.BlockSpec((tt, d), lambda i: (i, 0)),
        out_shape=jax.ShapeDtypeStruct((t, d), BF16),
        compiler_params=_cparams(("parallel",)),
    )(x, g.reshape(1, d))


def _rmsnorm_bwd(dh, x, g, dres, *, name, bf16_copy):
    t, d = x.shape
    tt = _tile(t, 512)
    nt = t // tt

    def body(dh_ref, x_ref, g_ref, dres_ref, dx_ref, *out_refs):
        dg_ref = out_refs[-1]
        i = pl.program_id(0)

        @pl.when(i == 0)
        def _():
            dg_ref[...] = jnp.zeros_like(dg_ref)

        xf = x_ref[...]
        rstd = lax.rsqrt(jnp.mean(xf * xf, axis=-1, keepdims=True) + EPS)
        xhat = xf * rstd
        dhf = dh_ref[...].astype(F32)
        dxhat = dhf * g_ref[...]
        mt = jnp.mean(dxhat * xhat, axis=-1, keepdims=True)
        dx = dres_ref[...] + rstd * (dxhat - xhat * mt)
        dx_ref[...] = dx
        if bf16_copy:
            out_refs[0][...] = dx.astype(BF16)
        dg_ref[...] += jnp.sum(dhf * xhat, axis=0, keepdims=True)

    blk = pl.BlockSpec((tt, d), lambda i: (i, 0))
    vec = pl.BlockSpec((1, d), lambda i: (0, 0))
    low = [jax.ShapeDtypeStruct((t, d), BF16)] if bf16_copy else []
    outs = pl.pallas_call(
        body, name=name, grid=(nt,),
        in_specs=[blk, blk, vec, blk], out_specs=[blk] + [blk] * len(low) + [vec],
        out_shape=[jax.ShapeDtypeStruct((t, d), F32)] + low + [jax.ShapeDtypeStruct((1, d), F32)],
        compiler_params=_cparams(("arbitrary",)),
    )(dh, x, g.reshape(1, d), dres)
    return outs[0], (outs[1] if bf16_copy else None), outs[-1]


def _final_loss(x2, tgt, g, *, name):
    t, d = x2.shape
    tt = _tile(t, 512)

    def body(x_ref, t_ref, g_ref, l_ref, dx_ref, dxb_ref, dg_ref):
        i = pl.program_id(0)

        @pl.when(i == 0)
        def _():
            dg_ref[...] = jnp.zeros_like(dg_ref)
            l_ref[...] = jnp.zeros_like(l_ref)

        xf = x_ref[...]
        gg = g_ref[...]
        rstd = lax.rsqrt(jnp.mean(xf * xf, axis=-1, keepdims=True) + EPS)
        xhat = xf * rstd
        err = xhat * gg - t_ref[...]
        l_ref[...] += jnp.sum(err * err, axis=0, keepdims=True)
        dy = err * (1.0 / d)
        dxhat = dy * gg
        mt = jnp.mean(dxhat * xhat, axis=-1, keepdims=True)
        dx = rstd * (dxhat - xhat * mt)
        dx_ref[...] = dx
        dxb_ref[...] = dx.astype(dxb_ref.dtype)
        dg_ref[...] += jnp.sum(dy * xhat, axis=0, keepdims=True)

    blk = pl.BlockSpec((tt, d), lambda i: (i, 0))
    vec = pl.BlockSpec((1, d), lambda i: (0, 0))
    return pl.pallas_call(
        body, name=name, grid=(t // tt,),
        in_specs=[blk, blk, vec], out_specs=[vec, blk, blk, vec],
        out_shape=[jax.ShapeDtypeStruct((1, d), F32), jax.ShapeDtypeStruct((t, d), F32),
                   jax.ShapeDtypeStruct((t, d), BF16), jax.ShapeDtypeStruct((1, d), F32)],
        compiler_params=_cparams(("arbitrary",)),
    )(x2, tgt, g.reshape(1, d))


def _shift_down(prev8, cur, s):
    ext = jnp.concatenate([prev8, cur], axis=0)
    if s == 0:
        return cur
    return pltpu.roll(ext, s, 0)[SUBLANES:, :]


def _shift_up(cur, next8, s):
    if s == 0:
        return cur
    n = cur.shape[0]
    ext = jnp.concatenate([cur, next8], axis=0)
    return pltpu.roll(ext, n + SUBLANES - s, 0)[:n, :]


def _lru_gates(xc, wa, ba, wx, bx, sp):
    xcb = xc.astype(BF16)
    r = _sigmoid(jnp.dot(xcb, wa, preferred_element_type=F32) + ba)
    ig = _sigmoid(jnp.dot(xcb, wx, preferred_element_type=F32) + bx)
    log_a = -LRU_C * r * sp
    a = jnp.exp(log_a)
    z = -jnp.tanh(log_a) * (a * a + 1.0)
    inv_mult = lax.rsqrt(jnp.maximum(z, TINY))
    return r, ig, a, z * inv_mult, inv_mult


def _lru_specs(tt, cg, n_groups, nt, reverse):
    ncol = cg // LANES
    if reverse:
        ti = lambda i: nt - 1 - i
    else:
        ti = lambda i: i
    hb = tt // SUBLANES
    cur = lambda col: pl.BlockSpec((tt, cg), lambda g, i: (ti(i), 2 * g + col))
    prev = lambda col: pl.BlockSpec((SUBLANES, cg), lambda g, i: (jnp.maximum(ti(i) * hb - 1, 0), 2 * g + col))
    chan = lambda rows: pl.BlockSpec((rows, cg), lambda g, i: (0, g))
    wblk = pl.BlockSpec((ncol, LRU_BLOCK_W, LRU_BLOCK_W), lambda g, i: (g, 0, 0))
    plain = pl.BlockSpec((tt, cg), lambda g, i: (ti(i), g))
    plain_prev = pl.BlockSpec((SUBLANES, cg), lambda g, i: (jnp.maximum(ti(i) * hb - 1, 0), g))
    return cur, prev, chan, wblk, plain, plain_prev


def _chip_gather_copies(ins, outs, send, recv, loc):
    n = len(ins)
    x, y, c = lax.axis_index("x"), lax.axis_index("y"), lax.axis_index("c")
    me = 2 * x + y
    copies = [pltpu.make_async_copy(ins[k], outs[k].at[me], loc.at[k]) for k in range(n)]
    for r, (rx, ry) in enumerate(((1, 0), (0, 1), (1, 1))):
        for k in range(n):
            copies.append(pltpu.make_async_remote_copy(
                src_ref=ins[k], dst_ref=outs[k].at[me], send_sem=send.at[r * n + k], recv_sem=recv.at[r * n + k],
                device_id=(_flip(x, rx), _flip(y, ry), c), device_id_type=MESH))
    return copies


def _lru_fwd(u, conv_w, conv_b, wa, ba, wx, bx, a_param, *, cg, name, ride=()):
    nride = len(ride)
    t, w2 = u.shape
    w = w2 // 2
    n_groups = w // cg
    ncol = cg // LANES
    tt = _tile(t, 256)
    nt = t // tt
    cur, prev, chan, wblk, plain, _ = _lru_specs(tt, cg, n_groups, nt, False)

    def body(*refs):
        n_in, n_out, n_scr = 10, 2, 3
        ins, rest = refs[:n_in], refs[n_in:]
        ride_in, rest = rest[:nride], rest[nride:]
        outs, rest = rest[:n_out], rest[n_out:]
        ride_out, rest = rest[:nride], rest[nride:]
        scr, sems = rest[:n_scr], rest[n_scr:]
        if not nride:
            core(*ins, *outs, *scr)
            return
        step = pl.program_id(0) * nt + pl.program_id(1)

        @pl.when(step == 0)
        def _():
            for cp in _chip_gather_copies(ride_in, ride_out, *sems):
                cp.start()

        core(*ins, *outs, *scr)

        @pl.when(step == n_groups * nt - 1)
        def _():
            for cp in _chip_gather_copies(ride_in, ride_out, *sems):
                cp.wait()

    def core(xb_ref, xp_ref, gate_ref, cw_ref, cb_ref, wa_ref, ba_ref, wx_ref, bx_ref, ap_ref,
             y_ref, hs_ref, h_ref, a_s, b_s):
        i = pl.program_id(1)

        @pl.when(i == 0)
        def _():
            h_ref[...] = jnp.zeros_like(h_ref)

        keep = (i > 0).astype(F32)
        for n in range(ncol):
            sl = slice(n * LANES, (n + 1) * LANES)
            xb = xb_ref[:, sl]
            xp = xp_ref[:, sl] * keep
            xc = cb_ref[:, sl] + cw_ref[3:4, sl] * xb
            for s in range(1, CONV_WIDTH):
                xc = xc + cw_ref[3 - s:4 - s, sl] * _shift_down(xp, xb, s)
            sp = _softplus(-ap_ref[:, sl])
            _, ig, a, mult, _ = _lru_gates(xc, wa_ref[n].astype(BF16), ba_ref[:, sl],
                                           wx_ref[n].astype(BF16), bx_ref[:, sl], sp)
            a_s[:, sl] = a
            b_s[:, sl] = mult * (ig * xc)

        def step(g, h):
            base = pl.multiple_of(g * SUBLANES, SUBLANES)
            for r in range(SUBLANES):
                h = a_s[pl.ds(base + r, 1), :] * h + b_s[pl.ds(base + r, 1), :]
                hs_ref[pl.ds(base + r, 1), :] = h
            return h

        h = lax.fori_loop(0, tt // SUBLANES, step, h_ref[0:1, :])
        h_ref[0:1, :] = h
        gate = gate_ref[...]
        y_ref[...] = (hs_ref[...] * (gate * _sigmoid(gate))).astype(y_ref.dtype)

    any_spec = pl.BlockSpec(memory_space=pl.ANY)
    ride_sems = [pltpu.SemaphoreType.DMA((3 * nride,)), pltpu.SemaphoreType.DMA((3 * nride,)),
                 pltpu.SemaphoreType.DMA((nride,))] if nride else []
    outs = pl.pallas_call(
        body, name=name, grid=(n_groups, nt),
        in_specs=[cur(0), prev(0), cur(1), chan(CONV_WIDTH), chan(1), wblk, chan(1), wblk, chan(1), chan(1)]
        + [any_spec] * nride,
        out_specs=[plain, plain] + [any_spec] * nride,
        out_shape=[jax.ShapeDtypeStruct((t, w), BF16), jax.ShapeDtypeStruct((t, w), F32)]
        + [jax.ShapeDtypeStruct((N_CHIPS,) + r.shape, r.dtype) for r in ride],
        scratch_shapes=[pltpu.VMEM((SUBLANES, cg), F32), pltpu.VMEM((tt, cg), F32), pltpu.VMEM((tt, cg), F32)]
        + ride_sems,
        compiler_params=_cparams(("arbitrary", "arbitrary")),
    )(u, u, u, conv_w, conv_b, wa, ba, wx, bx, a_param, *ride)
    return outs[0], outs[1], outs[2:]


def _lru_bwd(u, hs, dy, conv_w, conv_b, wa, ba, wx, bx, a_param, *, cg, name, ride=()):
    nride = len(ride)
    t, w2 = u.shape
    w = w2 // 2
    n_groups = w // cg
    ncol = cg // LANES
    tt = _tile(t, 256)
    nt = t // tt
    cur, prev, chan, wblk, plain, plain_prev = _lru_specs(tt, cg, n_groups, nt, True)
    tn_dims = (((0,), (0,)), ((), ()))
    nt_dims = (((1,), (1,)), ((), ()))

    def body(*refs):
        n_in, n_out, n_scr = 13, 9, 5
        ins, rest = refs[:n_in], refs[n_in:]
        ride_in, rest = rest[:nride], rest[nride:]
        outs, rest = rest[:n_out], rest[n_out:]
        ride_out, rest = rest[:nride], rest[nride:]
        scr, sems = rest[:n_scr], rest[n_scr:]
        if not nride:
            core(*ins, *outs, *scr)
            return
        step = pl.program_id(0) * nt + pl.program_id(1)

        @pl.when(step == 0)
        def _():
            for cp in _scatter_copies(ride_in, ride_out, *sems):
                cp.start()

        core(*ins, *outs, *scr)

        @pl.when(step == n_groups * nt - 1)
        def _():
            for cp in _scatter_copies(ride_in, ride_out, *sems):
                cp.wait()

    def core(xb_ref, xp_ref, gate_ref, hs_ref, hp_ref, dy_ref, cw_ref, cb_ref, wa_ref, ba_ref, wx_ref, bx_ref,
             ap_ref, dxb_ref, dgate_ref, dcw_ref, dcb_ref, dwa_ref, dba_ref, dwx_ref, dbx_ref, dsp_ref,
             c_ref, nx_ref, a_s, dhs_s, lam_s):
        i = pl.program_id(1)
        first_time_block = i == nt - 1

        @pl.when(i == 0)
        def _():
            c_ref[...] = jnp.zeros_like(c_ref)
            nx_ref[...] = jnp.zeros_like(nx_ref)
            for r in (dcw_ref, dcb_ref, dwa_ref, dba_ref, dwx_ref, dbx_ref, dsp_ref):
                r[...] = jnp.zeros_like(r)

        keep = jnp.where(first_time_block, 0.0, 1.0).astype(F32)
        gate = gate_ref[...]
        sg = _sigmoid(gate)
        dyv = dy_ref[...]
        hsv = hs_ref[...]
        dhs_s[...] = dyv * (gate * sg)
        dgate_ref[...] = (dyv * hsv * (sg * (1.0 + gate * (1.0 - sg)))).astype(dgate_ref.dtype)

        saved = []
        for n in range(ncol):
            sl = slice(n * LANES, (n + 1) * LANES)
            xb = xb_ref[:, sl]
            xp = xp_ref[:, sl] * keep
            shifted = [xb] + [_shift_down(xp, xb, s) for s in range(1, CONV_WIDTH)]
            xc = cb_ref[:, sl] + cw_ref[3:4, sl] * xb
            for s in range(1, CONV_WIDTH):
                xc = xc + cw_ref[3 - s:4 - s, sl] * shifted[s]
            sp = _softplus(-ap_ref[:, sl])
            wab = wa_ref[n].astype(BF16)
            wxb = wx_ref[n].astype(BF16)
            r, ig, a, mult, inv_mult = _lru_gates(xc, wab, ba_ref[:, sl], wxb, bx_ref[:, sl], sp)
            a_s[:, sl] = a
            saved.append((sl, shifted, xc, sp, wab, wxb, r, ig, a, mult, inv_mult))

        def step(g, c):
            base = pl.multiple_of(tt - SUBLANES - g * SUBLANES, SUBLANES)
            for r in range(SUBLANES - 1, -1, -1):
                lam = dhs_s[pl.ds(base + r, 1), :] + c
                lam_s[pl.ds(base + r, 1), :] = lam
                c = a_s[pl.ds(base + r, 1), :] * lam
            return c

        c_ref[0:1, :] = lax.fori_loop(0, tt // SUBLANES, step, c_ref[0:1, :])

        for n in range(ncol):
            sl, shifted, xc, sp, wab, wxb, r, ig, a, mult, inv_mult = saved[n]
            lam = lam_s[:, sl]
            hprev = _shift_down(hp_ref[:, sl] * keep, hs_ref[:, sl], 1)
            da = lam * hprev
            dmult = lam * (ig * xc)
            dlog_a = da * a - dmult * (a * a * inv_mult)
            di = lam * (mult * xc)
            dxc = lam * (mult * ig)
            dr = dlog_a * (-LRU_C * sp)
            dsp_ref[:, sl] += jnp.sum(dlog_a * (-LRU_C * r), axis=0, keepdims=True)
            dza = dr * (r * (1.0 - r))
            dzx = di * (ig * (1.0 - ig))
            dba_ref[:, sl] += jnp.sum(dza, axis=0, keepdims=True)
            dbx_ref[:, sl] += jnp.sum(dzx, axis=0, keepdims=True)
            xcb = xc.astype(BF16)
            dzab = dza.astype(BF16)
            dzxb = dzx.astype(BF16)
            dwa_ref[n] += lax.dot_general(xcb, dzab, tn_dims, preferred_element_type=F32)
            dwx_ref[n] += lax.dot_general(xcb, dzxb, tn_dims, preferred_element_type=F32)
            dxc = dxc + lax.dot_general(dzab, wab, nt_dims, preferred_element_type=F32)
            dxc = dxc + lax.dot_general(dzxb, wxb, nt_dims, preferred_element_type=F32)
            dcb_ref[:, sl] += jnp.sum(dxc, axis=0, keepdims=True)
            for s in range(CONV_WIDTH):
                dcw_ref[3 - s:4 - s, sl] += jnp.sum(dxc * shifted[s], axis=0, keepdims=True)
            nx = nx_ref[:, sl]
            dxb = cw_ref[3:4, sl] * dxc
            for s in range(1, CONV_WIDTH):
                dxb = dxb + cw_ref[3 - s:4 - s, sl] * _shift_up(dxc, nx, s)
            dxb_ref[:, sl] = dxb.astype(dxb_ref.dtype)
            nx_ref[:, sl] = dxc[0:SUBLANES, :]

        @pl.when(first_time_block)
        def _():
            dsp_ref[...] = dsp_ref[...] * (-_sigmoid(-ap_ref[...]))

    dxb_spec = pl.BlockSpec((tt, cg), lambda g, i: (nt - 1 - i, g))
    any_spec = pl.BlockSpec(memory_space=pl.ANY)
    ride_shape, ride_sems = _scatter_shapes(ride) if nride else ([], [])
    outs = pl.pallas_call(
        body, name=name, grid=(n_groups, nt),
        in_specs=[cur(0), prev(0), cur(1), plain, plain_prev, plain, chan(CONV_WIDTH), chan(1), wblk, chan(1), wblk,
                  chan(1), chan(1)] + [any_spec] * nride,
        out_specs=[dxb_spec, dxb_spec, chan(CONV_WIDTH), chan(1), wblk, chan(1), wblk, chan(1), chan(1)]
        + [any_spec] * nride,
        out_shape=[jax.ShapeDtypeStruct((t, w), BF16), jax.ShapeDtypeStruct((t, w), BF16),
                   jax.ShapeDtypeStruct(conv_w.shape, F32), jax.ShapeDtypeStruct(conv_b.shape, F32),
                   jax.ShapeDtypeStruct(wa.shape, F32), jax.ShapeDtypeStruct(ba.shape, F32),
                   jax.ShapeDtypeStruct(wx.shape, F32), jax.ShapeDtypeStruct(bx.shape, F32),
                   jax.ShapeDtypeStruct(a_param.shape, F32)] + ride_shape,
        scratch_shapes=[pltpu.VMEM((SUBLANES, cg), F32), pltpu.VMEM((SUBLANES, cg), F32),
                        pltpu.VMEM((tt, cg), F32), pltpu.VMEM((tt, cg), F32), pltpu.VMEM((tt, cg), F32)] + ride_sems,
        compiler_params=_cparams(("arbitrary", "arbitrary")),
    )(u, u, u, hs, hs, dy, conv_w, conv_b, wa, ba, wx, bx, a_param, *ride)
    return outs[:9], outs[9:]


def _fgate_fwd(f, b_f, *, name):
    t, n = f.shape
    tt = _tile(t, 256)
    width = FOX_HEADS * FOX_HEAD_DIM

    def body(f_ref, b_ref, cum_ref, wide_ref, carry_ref):
        i = pl.program_id(0)

        @pl.when(i == 0)
        def _():
            carry_ref[...] = jnp.zeros_like(carry_ref)

        z = f_ref[...] + b_ref[...]
        lf = jnp.minimum(z, 0.0) - _log1p(jnp.exp(-jnp.abs(z)))
        row = lax.broadcasted_iota(jnp.int32, (tt, tt), 0)
        col = lax.broadcasted_iota(jnp.int32, (tt, tt), 1)
        tri = (col <= row).astype(BF16)
        cum = _dot_01_left(tri, lf) + carry_ref[0:1, :]
        cum_ref[...] = cum
        carry_ref[0:1, :] = cum[tt - 1:tt, :]
        head = lax.broadcasted_iota(jnp.int32, (n, width), 0)
        chan = lax.broadcasted_iota(jnp.int32, (n, width), 1) // FOX_HEAD_DIM
        wide_ref[...] = _dot_01_right(cum, (head == chan).astype(BF16))

    return pl.pallas_call(
        body, name=name, grid=(t // tt,),
        in_specs=[pl.BlockSpec((tt, n), lambda i: (i, 0)), pl.BlockSpec((1, n), lambda i: (0, 0))],
        out_specs=[pl.BlockSpec((tt, n), lambda i: (i, 0)), pl.BlockSpec((tt, width), lambda i: (i, 0))],
        out_shape=[jax.ShapeDtypeStruct((t, n), F32), jax.ShapeDtypeStruct((t, width), F32)],
        scratch_shapes=[pltpu.VMEM((SUBLANES, n), F32)],
        compiler_params=_cparams(("arbitrary",)),
    )(f, b_f)


def _fgate_bwd(dcum, f, b_f, *, name):
    t, n = f.shape
    tt = _tile(t, 256)
    nt = t // tt

    def body(dc_ref, f_ref, b_ref, df_ref, db_ref, carry_ref):
        i = pl.program_id(0)

        @pl.when(i == 0)
        def _():
            carry_ref[...] = jnp.zeros_like(carry_ref)
            db_ref[...] = jnp.zeros_like(db_ref)

        row = lax.broadcasted_iota(jnp.int32, (tt, tt), 0)
        col = lax.broadcasted_iota(jnp.int32, (tt, tt), 1)
        triu = (col >= row).astype(BF16)
        dlf = _dot_01_left(triu, dc_ref[...]) + carry_ref[0:1, :]
        carry_ref[0:1, :] = dlf[0:1, :]
        z = f_ref[...] + b_ref[...]
        df = dlf * _sigmoid(-z)
        df_ref[...] = df
        db_ref[...] += jnp.sum(df, axis=0, keepdims=True)

    blk = pl.BlockSpec((tt, n), lambda i: (nt - 1 - i, 0))
    vec = pl.BlockSpec((1, n), lambda i: (0, 0))
    return pl.pallas_call(
        body, name=name, grid=(nt,),
        in_specs=[blk, blk, vec], out_specs=[blk, vec],
        out_shape=[jax.ShapeDtypeStruct((t, n), F32), jax.ShapeDtypeStruct((1, n), F32)],
        scratch_shapes=[pltpu.VMEM((SUBLANES, n), F32)],
        compiler_params=_cparams(("arbitrary",)),
    )(dcum, f, b_f)


def _attn_fwd(start, qkv, ckt, gate, *, name, tq, tk):
    t = qkv.shape[0]
    f = gate.shape[1]
    npair = f // LANES
    nq = t // tq
    ratio = tq // tk
    assert tq == ratio * tk and t == nq * tq
    scale = 1.0 / math.sqrt(FOX_HEAD_DIM)
    nt_dims = (((1,), (1,)), ((), ()))

    def body(start_ref, q_ref, k_ref, v_ref, ck_ref, g_ref, o_ref, y_ref, l_ref, acc_a, acc_b):
        accs = (acc_a, acc_b)
        i = pl.program_id(1)
        pair = pl.program_id(0)
        firsts = (start_ref[2 * pair, i], start_ref[2 * pair + 1, i])
        both = jnp.maximum(firsts[0], firsts[1])
        lane = lax.broadcasted_iota(jnp.int32, (tq, LANES), 1)
        lo = lane < FOX_HEAD_DIM
        q2 = q_ref[...] * scale
        qs = (jnp.where(lo, q2, 0).astype(BF16), jnp.where(lo, 0, q2).astype(BF16))
        row = lax.broadcasted_iota(jnp.int32, (tq, tk), 0)
        col = lax.broadcasted_iota(jnp.int32, (tq, tk), 1)

        def kv_step(j, carry, diag, heads=(0, 1)):
            off = pl.multiple_of(j * tk, tk)
            kj = k_ref[pl.ds(off, tk), :]
            vj = v_ref[pl.ds(off, tk), :]
            ck = ck_ref[:, pl.ds(off, tk)]
            new = list(carry)
            for h in heads:
                m, l = carry[h]
                s = lax.dot_general(qs[h], kj, nt_dims, preferred_element_type=F32) - ck[h:h + 1, :]
                if diag is not None:
                    s = jnp.where(col + diag * tk <= row, s, NEG_INF)
                m_new = jnp.maximum(m, jnp.max(s, axis=-1, keepdims=True))
                alpha = jnp.exp(m - m_new)
                p = jnp.exp(s - m_new)
                l = alpha * l + jnp.sum(p, axis=-1, keepdims=True)
                accs[h][...] = alpha * accs[h][...] + jnp.dot(p.astype(BF16), vj, preferred_element_type=F32)
                new[h] = (m_new, l)
            return tuple(new)

        carry = tuple((jnp.full((tq, 1), NEG_INF, F32), jnp.zeros((tq, 1), F32)) for _ in range(2))
        acc_a[...] = jnp.zeros_like(acc_a)
        acc_b[...] = jnp.zeros_like(acc_b)

        def run(lo_blk, hi_blk, carry, heads):
            twos = (hi_blk - lo_blk) // 2
            carry = lax.fori_loop(
                0, twos,
                lambda jj, c: kv_step(lo_blk + 2 * jj + 1, kv_step(lo_blk + 2 * jj, c, None, heads), None, heads),
                carry)
            return lax.fori_loop(lo_blk + 2 * twos, hi_blk, lambda j, c: kv_step(j, c, None, heads), carry)

        for h in range(2):
            carry = lax.fori_loop(firsts[h], both, lambda j, c, h=h: kv_step(j, c, None, (h,)), carry)
        carry = run(both, i * ratio, carry, (0, 1))
        for d in range(ratio):
            carry = kv_step(i * ratio + d, carry, d)
        (m0, l0), (m1, l1) = carry
        o = jnp.where(lo, acc_a[...] / l0, acc_b[...] / l1)
        o_ref[...] = o
        gate_v = g_ref[...]
        y_ref[...] = (o * (gate_v * _sigmoid(gate_v))).astype(y_ref.dtype)
        lse_t = jnp.transpose(jnp.where(lo, m0 + jnp.log(l0), m1 + jnp.log(l1)))
        l_ref[0:1, :] = lse_t[0:1, :]
        l_ref[1:2, :] = lse_t[FOX_HEAD_DIM:FOX_HEAD_DIM + 1, :]

    blk = lambda base: pl.BlockSpec((tq, LANES), lambda p, i, s: (i, base + p))
    full = lambda base: pl.BlockSpec((t, LANES), lambda p, i, s: (0, base + p))
    return pl.pallas_call(
        body, name=name,
        grid_spec=pltpu.PrefetchScalarGridSpec(
            num_scalar_prefetch=1, grid=(npair, nq),
            in_specs=[blk(0), full(npair), full(2 * npair), pl.BlockSpec((None, 2, t), lambda p, i, s: (p, 0, 0)),
                      blk(0)],
            out_specs=[blk(0), blk(0), pl.BlockSpec((None, 2, tq), lambda p, i, s: (p, 0, i))],
            scratch_shapes=[pltpu.VMEM((tq, LANES), F32), pltpu.VMEM((tq, LANES), F32)]),
        out_shape=[jax.ShapeDtypeStruct((t, f), F32), jax.ShapeDtypeStruct((t, f), BF16),
                   jax.ShapeDtypeStruct((npair, 2, t), F32)],
        compiler_params=_cparams(("parallel", "arbitrary")),
    )(start, qkv, qkv, qkv, ckt, gate)


def _attn_bwd(end, qkv, do, lt, dt, cke, *, name):
    t, f = do.shape
    npair = f // LANES
    tk = _tile(t, ATTN_TILE)
    nk = t // tk
    scale = 1.0 / math.sqrt(FOX_HEAD_DIM)
    nt_dims = (((1,), (1,)), ((), ()))
    tn_dims = (((0,), (0,)), ((), ()))

    def body(end_ref, k_ref, v_ref, q_ref, do_ref, l_ref, d_ref, ck_ref, dq_out_ref, dk_ref, dv_ref, dck_ref, dcq_ref,
             dq_ref, dk_s, dv_s, dck_s):
        j = pl.program_id(1)
        pair = pl.program_id(0)
        lasts = (end_ref[2 * pair, j], end_ref[2 * pair + 1, j])
        both = jnp.minimum(lasts[0], lasts[1])

        @pl.when(j == 0)
        def _():
            dq_ref[...] = jnp.zeros_like(dq_ref)
            dcq_ref[...] = jnp.zeros_like(dcq_ref)

        lane = lax.broadcasted_iota(jnp.int32, (tk, LANES), 1)
        lo = lane < FOX_HEAD_DIM
        sel = (lo, jnp.logical_not(lo))
        kj = k_ref[...]
        vj = v_ref[...]
        km = tuple(jnp.where(sel[h], kj, 0).astype(BF16) for h in range(2))
        ckv = ck_ref[...]
        ckh = (ckv[:, 0:1], ckv[:, FOX_HEAD_DIM:FOX_HEAD_DIM + 1])
        row = lax.broadcasted_iota(jnp.int32, (tk, tk), 0)
        col = lax.broadcasted_iota(jnp.int32, (tk, tk), 1)
        causal = row <= col

        def q_step(i, carry, masked, heads=(0, 1)):
            off = pl.multiple_of(i * tk, tk)
            qi = q_ref[pl.ds(off, tk), :]
            doi = do_ref[pl.ds(off, tk), :]
            lrow = l_ref[:, pl.ds(off, tk)]
            drow = d_ref[:, pl.ds(off, tk)]
            dq_add = jnp.zeros((tk, LANES), F32)
            for h in heads:
                qm = jnp.where(sel[h], qi, 0).astype(BF16)
                dom = jnp.where(sel[h], doi, 0).astype(BF16)
                st = lax.dot_general(kj, qm, nt_dims, preferred_element_type=F32) * scale
                st = st - ckh[h] - lrow[h:h + 1, :]
                if masked:
                    st = jnp.where(causal, st, NEG_INF)
                pt = jnp.exp(st)
                dpt = lax.dot_general(vj, dom, nt_dims, preferred_element_type=F32)
                dst = pt * (dpt - drow[h:h + 1, :])
                ptb = pt.astype(BF16)
                dstb = dst.astype(BF16)
                dv_s[...] += jnp.dot(ptb, dom, preferred_element_type=F32)
                dk_s[...] += jnp.dot(dstb, qm, preferred_element_type=F32)
                dq_add = dq_add + lax.dot_general(dstb, km[h], tn_dims, preferred_element_type=F32)
                dck_s[:, h:h + 1] -= jnp.sum(dst, axis=-1, keepdims=True)
                dcq_ref[h:h + 1, pl.ds(off, tk)] += jnp.sum(dst, axis=0, keepdims=True)
            dq_ref[pl.ds(off, tk), :] += dq_add * scale
            return carry

        dk_s[...] = jnp.zeros_like(dk_s)
        dv_s[...] = jnp.zeros_like(dv_s)
        dck_s[...] = jnp.zeros_like(dck_s)
        carry = 0
        carry = q_step(j, carry, True)
        carry = lax.fori_loop(j + 1, both, lambda i, c: q_step(i, c, False), carry)
        for h in range(2):
            carry = lax.fori_loop(both, lasts[h], lambda i, c, h=h: q_step(i, c, False, (h,)), carry)
        dk_acc, dv_acc = dk_s[...], dv_s[...]
        dck = (dck_s[:, 0:1], dck_s[:, 1:2])
        dk_ref[...] = (dk_acc * scale).astype(dk_ref.dtype)
        dv_ref[...] = dv_acc.astype(dv_ref.dtype)
        dck_t = jnp.transpose(jnp.where(lo, dck[0], dck[1]))
        dck_ref[0:1, :] = dck_t[0:1, :]
        dck_ref[1:2, :] = dck_t[FOX_HEAD_DIM:FOX_HEAD_DIM + 1, :]

        @pl.when(j == nk - 1)
        def _():
            dq_out_ref[...] = dq_ref[...].astype(dq_out_ref.dtype)

    blk = lambda base: pl.BlockSpec((tk, LANES), lambda p, j, e: (j, base + p))
    full = lambda base: pl.BlockSpec((t, LANES), lambda p, j, e: (0, base + p))
    rows = pl.BlockSpec((None, 2, t), lambda p, j, e: (p, 0, 0))
    return pl.pallas_call(
        body, name=name,
        grid_spec=pltpu.PrefetchScalarGridSpec(
            num_scalar_prefetch=1, grid=(npair, nk),
            in_specs=[blk(npair), blk(2 * npair), full(0), full(0), rows, rows, blk(0)],
            out_specs=[full(0), blk(0), blk(0), pl.BlockSpec((None, 2, tk), lambda p, j, e: (p, 0, j)), rows],
            scratch_shapes=[pltpu.VMEM((t, LANES), F32), pltpu.VMEM((tk, LANES), F32), pltpu.VMEM((tk, LANES), F32),
                            pltpu.VMEM((tk, LANES), F32)]),
        out_shape=[jax.ShapeDtypeStruct((t, f), BF16), jax.ShapeDtypeStruct((t, f), BF16),
                   jax.ShapeDtypeStruct((t, f), BF16), jax.ShapeDtypeStruct((npair, 2, t), F32),
                   jax.ShapeDtypeStruct((npair, 2, t), F32)],
        compiler_params=_cparams(("parallel", "arbitrary")),
    )(end, qkv, qkv, qkv, do, lt, dt, cke)


ATTN_TILE = 512
ATTN_FWD_QUERIES = 512
EXP_ZERO = -104.0
BOUND_SLACK = 1.02


def _attn_row_stats(qkv, *, name):
    t = qkv.shape[0]
    f = qkv.shape[1] // 3
    tt = _tile(t, 512)

    def body(q_ref, k_ref, s_ref):
        q = q_ref[...].astype(F32)
        k = k_ref[...].astype(F32)
        chan = lax.broadcasted_iota(jnp.int32, (f, LANES), 0) // FOX_HEAD_DIM
        lane = lax.broadcasted_iota(jnp.int32, (f, LANES), 1)
        acc = jnp.zeros((tt, LANES), F32)
        for off, val in ((0, q * q), (FOX_HEADS, q * k), (2 * FOX_HEADS, k * k)):
            pick = (chan == lane - off).astype(BF16)
            acc = acc + jnp.dot(val.astype(BF16), pick, preferred_element_type=F32)
        s_ref[...] = acc

    return pl.pallas_call(
        body, name=name, grid=(t // tt,),
        in_specs=[pl.BlockSpec((tt, f), lambda i: (i, 0)), pl.BlockSpec((tt, f), lambda i: (i, 1))],
        out_specs=pl.BlockSpec((tt, LANES), lambda i: (i, 0)),
        out_shape=jax.ShapeDtypeStruct((t, LANES), F32),
        compiler_params=_cparams(("parallel",)),
    )(qkv, qkv)


def _attn_skip_tables(stats, cum16, tile):
    t = stats.shape[0]
    nb = t // tile
    scale = 1.0 / math.sqrt(FOX_HEAD_DIM)
    qn = jnp.sqrt(stats[:, :FOX_HEADS]) * scale
    sii = stats[:, FOX_HEADS:2 * FOX_HEADS] * scale - cum16
    kmax = jnp.max(jnp.sqrt(stats[:, 2 * FOX_HEADS:3 * FOX_HEADS]), axis=0, keepdims=True)
    arow = qn * kmax * BOUND_SLACK - sii + 0.5 * BOUND_SLACK
    a_blk = jnp.max(arow.reshape(nb, tile, FOX_HEADS), axis=1)
    c_blk = -cum16.reshape(nb, tile, FOX_HEADS)[:, tile - 1, :]
    dead = (a_blk[:, None, :] + c_blk[None, :, :]) < EXP_ZERO
    start_h = jnp.sum(dead.astype(jnp.int32), axis=1)
    blk = jnp.arange(nb, dtype=jnp.int32)
    start = jnp.minimum(start_h, blk[:, None]).T
    needs = start[:, :, None] <= blk[None, None, :]
    end = jnp.max(jnp.where(needs, blk[None, :, None] + 1, 0), axis=1)
    return start, jnp.maximum(end, blk[None, :] + 1)


def _fox_post_bwd(dy, o, gate, *, name):
    t, f = dy.shape
    tt = _tile(t, 512)

    def body(dy_ref, o_ref, g_ref, do_ref, dg_ref, dl_ref):
        g = g_ref[...]
        sg = _sigmoid(g)
        dyv = dy_ref[...]
        ov = o_ref[...]
        do = dyv * (g * sg)
        do_ref[...] = do.astype(do_ref.dtype)
        dg_ref[...] = (dyv * ov * (sg * (1.0 + g * (1.0 - sg)))).astype(dg_ref.dtype)
        chan = lax.broadcasted_iota(jnp.int32, (f, LANES), 0)
        head = lax.broadcasted_iota(jnp.int32, (f, LANES), 1)
        pick = (chan // FOX_HEAD_DIM == head).astype(BF16)
        dl_ref[...] = _dot_01_right(do * ov, pick)

    blk = pl.BlockSpec((tt, f), lambda i: (i, 0))
    return pl.pallas_call(
        body, name=name, grid=(t // tt,),
        in_specs=[blk, blk, blk], out_specs=[blk, blk, pl.BlockSpec((tt, LANES), lambda i: (i, 0))],
        out_shape=[jax.ShapeDtypeStruct((t, f), BF16), jax.ShapeDtypeStruct((t, f), BF16),
                   jax.ShapeDtypeStruct((t, LANES), F32)],
        compiler_params=_cparams(("parallel",)),
    )(dy, o, gate)


def _adamw(w, g, m, v, *, name):
    _, r, c = w.shape
    tr = _tile(r, 256) if r % SUBLANES == 0 else r
    c1 = 1.0 - ADAM_B1 ** ADAM_STEP
    c2 = 1.0 - ADAM_B2 ** ADAM_STEP

    def body(w_ref, g_ref, m_ref, v_ref, go_ref, d_ref, mo_ref, vo_ref):
        gv = g_ref[...]
        go_ref[...] = gv
        mn = ADAM_B1 * m_ref[...] + (1.0 - ADAM_B1) * gv
        vn = ADAM_B2 * v_ref[...] + (1.0 - ADAM_B2) * (gv * gv)
        mo_ref[...] = mn
        vo_ref[...] = vn
        d_ref[...] = -ADAM_LR * ((mn / c1) / (jnp.sqrt(vn / c2) + ADAM_EPS) + ADAM_WD * w_ref[...])

    blk = pl.BlockSpec((None, tr, c), lambda i: (0, i, 0))
    return pl.pallas_call(
        body, name=name, grid=(r // tr,), in_specs=[blk] * 4, out_specs=[blk] * 4,
        out_shape=[jax.ShapeDtypeStruct((1, r, c), F32)] * 4,
        compiler_params=_cparams(("parallel",)),
    )(w, g, m, v)


def _sum_slots(land, *, name):
    ns, r, c = land.shape
    tr = _tile(r, 64) if r % SUBLANES == 0 else r

    def body(l_ref, o_ref):
        acc = l_ref[0].astype(F32)
        for s in range(1, ns):
            acc = acc + l_ref[s].astype(F32)
        o_ref[...] = acc

    return pl.pallas_call(
        body, name=name, grid=(r // tr,),
        in_specs=[pl.BlockSpec((ns, tr, c), lambda i: (0, i, 0))],
        out_specs=pl.BlockSpec((tr, c), lambda i: (i, 0)),
        out_shape=jax.ShapeDtypeStruct((r, c), F32),
        compiler_params=_cparams(("parallel",)),
    )(land)


ANY = pl.BlockSpec(memory_space=pl.ANY)


def _flip(v, bit):
    return 1 - v if bit else v


def _gather_chips(shards, small, *, name):
    n = len(shards)
    rels = ((1, 0), (0, 1), (1, 1))

    def body(*refs):
        ins, small_in = refs[:n], refs[n]
        outs, small_out = refs[n + 1:2 * n + 1], refs[2 * n + 1]
        send, recv, loc = refs[2 * n + 2:]
        x, y, c = lax.axis_index("x"), lax.axis_index("y"), lax.axis_index("c")
        me = 2 * x + y
        sibling = (x, y, 1 - c)
        local = [pltpu.make_async_copy(ins[k], outs[k].at[me], loc.at[k]) for k in range(n)]
        local.append(pltpu.make_async_copy(small_in, small_out.at[me], loc.at[n]))
        for cp in local:
            cp.start()

        def rows(k):
            half = ins[k].shape[0] // 2
            return pl.ds(pl.multiple_of(c * half, SUBLANES), half)

        sends = []
        for r, (rx, ry) in enumerate(rels):
            to = (_flip(x, rx), _flip(y, ry), c)
            for k in range(n):
                cp = pltpu.make_async_remote_copy(
                    src_ref=ins[k].at[rows(k), :], dst_ref=outs[k].at[me, rows(k), :],
                    send_sem=send.at[r * n + k], recv_sem=recv.at[r * n + k], device_id=to, device_id_type=MESH)
                cp.start()
                sends.append(cp)
            cp = pltpu.make_async_remote_copy(
                src_ref=small_in, dst_ref=small_out.at[me], send_sem=send.at[6 * n + r], recv_sem=recv.at[6 * n + r],
                device_id=to, device_id_type=MESH)
            cp.start()
            sends.append(cp)
        for r, (rx, ry) in enumerate(rels):
            src_chip = 2 * _flip(x, rx) + _flip(y, ry)
            for k in range(n):
                landed = outs[k].at[src_chip, rows(k), :]
                sends[r * (n + 1) + k].wait_recv()
                cp = pltpu.make_async_remote_copy(
                    src_ref=landed, dst_ref=landed, send_sem=send.at[3 * n + r * n + k],
                    recv_sem=recv.at[3 * n + r * n + k], device_id=sibling, device_id_type=MESH)
                cp.start()
                sends.append(cp)
            sends[r * (n + 1) + n].wait_recv()
        for cp in sends[:3 * (n + 1)]:
            cp.wait_send()
        for cp in sends[3 * (n + 1):]:
            cp.wait()
        for cp in local:
            cp.wait()

    vmem = pl.BlockSpec(memory_space=pltpu.VMEM)
    return pl.pallas_call(
        body, name=name, in_specs=[vmem] * (n + 1), out_specs=[vmem] * (n + 1),
        out_shape=[jax.ShapeDtypeStruct((N_CHIPS,) + s.shape, s.dtype) for s in list(shards) + [small]],
        scratch_shapes=[pltpu.SemaphoreType.DMA((6 * n + 3,)), pltpu.SemaphoreType.DMA((6 * n + 3,)),
                        pltpu.SemaphoreType.DMA((n + 1,))],
        compiler_params=pltpu.CompilerParams(has_side_effects=True, vmem_limit_bytes=VMEM_LIMIT),
    )(*shards, small)


_RELS7 = tuple((r >> 2 & 1, r >> 1 & 1, r & 1) for r in range(1, N_DEV))


def _scatter_copies(ins, outs, send, recv, loc):
    n = len(ins)
    x, y, c = lax.axis_index("x"), lax.axis_index("y"), lax.axis_index("c")
    me = 4 * x + 2 * y + c

    def piece(k, px, py, pc):
        half = ins[k].shape[1] // 2
        return ins[k].at[2 * px + py, pl.ds(pc * half, half), :]

    copies = [pltpu.make_async_copy(piece(k, x, y, c), outs[k].at[me], loc.at[k]) for k in range(n)]
    for r, (rx, ry, rc) in enumerate(_RELS7):
        tx, ty, tc = _flip(x, rx), _flip(y, ry), _flip(c, rc)
        for k in range(n):
            copies.append(pltpu.make_async_remote_copy(
                src_ref=piece(k, tx, ty, tc), dst_ref=outs[k].at[me], send_sem=send.at[r * n + k],
                recv_sem=recv.at[r * n + k], device_id=(tx, ty, tc), device_id_type=MESH))
    return copies


def _scatter_shapes(grads):
    n = len(grads)
    out_shape = [jax.ShapeDtypeStruct((N_DEV, g.shape[1] // 2, g.shape[2]), g.dtype) for g in grads]
    sems = [pltpu.SemaphoreType.DMA((7 * n,)), pltpu.SemaphoreType.DMA((7 * n,)), pltpu.SemaphoreType.DMA((n,))]
    return out_shape, sems


def _join_cores(halves, *, name):
    n = len(halves)

    def body(*refs):
        ins, outs = refs[:n], refs[n:2 * n]
        send, recv, loc = refs[2 * n:]
        x, y, c = lax.axis_index("x"), lax.axis_index("y"), lax.axis_index("c")
        copies = []
        for k in range(n):
            half = ins[k].shape[0]
            mine = outs[k].at[0, pl.ds(c * half, half), :]
            cp = pltpu.make_async_copy(ins[k], mine, loc.at[k])
            cp.start()
            copies.append(cp)
            cp = pltpu.make_async_remote_copy(
                src_ref=ins[k], dst_ref=mine, send_sem=send.at[k], recv_sem=recv.at[k],
                device_id=(x, y, 1 - c), device_id_type=MESH)
            cp.start()
            copies.append(cp)
        for cp in copies:
            cp.wait()

    in_vmem = pl.BlockSpec(memory_space=pltpu.VMEM)
    return pl.pallas_call(
        body, name=name, in_specs=[in_vmem] * n, out_specs=[in_vmem] * n,
        out_shape=[jax.ShapeDtypeStruct((1, 2 * h.shape[0], h.shape[1]), h.dtype) for h in halves],
        scratch_shapes=[pltpu.SemaphoreType.DMA((n,)), pltpu.SemaphoreType.DMA((n,)), pltpu.SemaphoreType.DMA((n,))],
        compiler_params=pltpu.CompilerParams(has_side_effects=True, vmem_limit_bytes=VMEM_LIMIT),
    )(*halves)


def _allreduce_small(buf, *, name):
    r, n = buf.shape
    half = r // 2
    rels = ((1, 0), (0, 1), (1, 1))

    def body(in_ref, out_ref, sib_ref, chips_ref, send, recv):
        x, y, c = lax.axis_index("x"), lax.axis_index("y"), lax.axis_index("c")
        sibling = (x, y, 1 - c)
        chip = 2 * x + y
        rows = pl.ds(pl.multiple_of(c * half, SUBLANES), half)

        swap = pltpu.make_async_remote_copy(src_ref=in_ref, dst_ref=sib_ref, send_sem=send.at[0], recv_sem=recv.at[0],
                                            device_id=sibling, device_id_type=MESH)
        swap.start()
        swap.wait()
        chips_ref[chip] = in_ref[rows, :] + sib_ref[rows, :]

        sends = []
        for k, (rx, ry) in enumerate(rels):
            cp = pltpu.make_async_remote_copy(
                src_ref=chips_ref.at[chip], dst_ref=chips_ref.at[chip], send_sem=send.at[1 + k],
                recv_sem=recv.at[1 + k], device_id=(_flip(x, rx), _flip(y, ry), c), device_id_type=MESH)
            cp.start()
            sends.append(cp)
        for cp in sends:
            cp.wait()
        total = chips_ref[0]
        for s in range(1, N_CHIPS):
            total = total + chips_ref[s]
        out_ref[rows, :] = total

        back = pltpu.make_async_remote_copy(src_ref=out_ref.at[rows, :], dst_ref=out_ref.at[rows, :],
                                            send_sem=send.at[4], recv_sem=recv.at[4],
                                            device_id=sibling, device_id_type=MESH)
        back.start()
        back.wait()

    vmem = pl.BlockSpec(memory_space=pltpu.VMEM)
    return pl.pallas_call(
        body, name=name, in_specs=[vmem], out_specs=vmem,
        out_shape=jax.ShapeDtypeStruct((r, n), F32),
        scratch_shapes=[pltpu.VMEM((r, n), F32), pltpu.VMEM((N_CHIPS, half, n), F32),
                        pltpu.SemaphoreType.DMA((5,)), pltpu.SemaphoreType.DMA((5,))],
        compiler_params=pltpu.CompilerParams(has_side_effects=True, vmem_limit_bytes=VMEM_LIMIT),
    )(buf)


def _pack(arrs):
    flat = []
    for a in arrs:
        v = a.reshape(-1)
        pad = (-v.shape[0]) % LANES
        if pad:
            v = jnp.pad(v, (0, pad))
        flat.append(v)
    v = jnp.concatenate(flat)
    pad = (-v.shape[0]) % (LANES * SUBLANES)
    if pad:
        v = jnp.pad(v, (0, pad))
    return v.reshape(-1, LANES)


def _unpack(buf, shapes):
    v = buf.reshape(-1)
    out, off = [], 0
    for s in shapes:
        n = math.prod(s)
        out.append(v[off:off + n].reshape(s))
        off += n + (-n) % LANES
    return out


def kernel(x, norm_g, final_g, lru_w_in, lru_conv_w, lru_conv_b, lru_wa, lru_ba, lru_wx, lru_bx, lru_a_param, lru_w_out, fox_w_in, fox_b_f, fox_w_out, loss_target, m_norm_g, m_final_g, m_lru_w_in, m_lru_conv_w, m_lru_conv_b, m_lru_wa, m_lru_ba, m_lru_wx, m_lru_bx, m_lru_a_param, m_lru_w_out, m_fox_w_in, m_fox_b_f, m_fox_w_out, v_norm_g, v_final_g, v_lru_w_in, v_lru_conv_w, v_lru_conv_b, v_lru_wa, v_lru_ba, v_lru_wx, v_lru_bx, v_lru_a_param, v_lru_w_out, v_fox_w_in, v_fox_b_f, v_fox_w_out):
    t, d = x.shape[1], x.shape[2]
    w = lru_wa.shape[1] * LRU_BLOCK_W
    f = FOX_HEADS * FOX_HEAD_DIM
    npair = f // LANES
    x0 = x.reshape(t, d)
    tgt = loss_target.reshape(t, d)
    chip = 2 * lax.axis_index("x") + lax.axis_index("y")

    g_lwi, g_lwo, g_cw = _gather_chips(
        [lru_w_in[0].astype(BF16), lru_w_out[0].astype(BF16)], lru_conv_w[0], name="gather_weights")
    cg = w // 2
    lwi = jnp.concatenate([g_lwi[0], g_lwi[2], g_lwi[1], g_lwi[3]], axis=1)
    lwo = g_lwo.reshape(w, d)
    conv_w = jnp.concatenate([g_cw[s] for s in range(N_CHIPS)], axis=1)
    conv_b, ba, bx, a_param = lru_conv_b, lru_ba, lru_bx, lru_a_param
    wa, wx = lru_wa[0], lru_wx[0]
    b_f = jnp.pad(fox_b_f, ((0, 0), (0, LANES - FOX_HEADS)))

    h0 = _rmsnorm(x0, norm_g[0], name="norm0")
    u = _matmul(h0, lwi, name="lru_in")
    y1, hs, (g_fwi, g_fwo) = _lru_fwd(u, conv_w, conv_b, wa, ba, wx, bx, a_param, cg=cg, name="lru_fwd",
                                      ride=[fox_w_in[0].astype(BF16), fox_w_out[0].astype(BF16)])
    fwi = jnp.concatenate([g_fwi[s] for s in range(N_CHIPS)], axis=1)
    w_qkv, w_g2 = fwi[:, :3 * f], fwi[:, 3 * f:4 * f]
    w_f = jnp.pad(fwi[:, 4 * f:], ((0, 0), (0, LANES - FOX_HEADS)))
    fwo = g_fwo.reshape(f, d)
    x1 = _matmul(y1, lwo, add=x0, name="lru_out")
    h1 = _rmsnorm(x1, norm_g[1], name="norm1")
    qkv = _matmul(h1, w_qkv, out_dtype=BF16, name="fox_qkv")
    gate2 = _matmul(h1, w_g2, name="fox_gate")
    flog = _matmul(h1, w_f, name="fox_f")
    cum, cke = _fgate_fwd(flog, b_f, name="fgate_fwd")
    cum16 = cum[:, :FOX_HEADS]
    ckt = cum16.T.reshape(npair, 2, t)
    a_tk, a_tq = _tile(t, ATTN_TILE), _tile(t, ATTN_FWD_QUERIES)
    a_start, a_end = _attn_skip_tables(_attn_row_stats(qkv, name="attn_row_stats"), cum16, a_tk)
    a_start_fwd = jnp.min(a_start.reshape(FOX_HEADS, t // a_tq, a_tq // a_tk), axis=2)
    o, y2, lse = _attn_fwd(a_start_fwd, qkv, ckt, gate2, name="attn_fwd", tq=a_tq, tk=a_tk)
    x2 = _matmul(y2, fwo, add=x1, name="fox_out")
    lsum, dx2, dx2_b, dgf = _final_loss(x2, tgt, final_g, name="final_loss")
    loss = lax.psum(0.5 * jnp.sum(lsum) / d, ("x", "y", "c"))

    d_fwo = _matmul(y2, dx2_b, ta=True, out_dtype=BF16, name="d_fox_w_out")
    dy2 = _matmul(dx2_b, fwo, tb=True, name="d_y2")
    do, dgate2, dl = _fox_post_bwd(dy2, o, gate2, name="fox_post_bwd")
    lt = lse
    dt = dl[:, :FOX_HEADS].T.reshape(npair, 2, t)
    dq, dk, dv, dck, dcq = _attn_bwd(a_end, qkv, do, lt, dt, cke, name="attn_bwd")
    dcum = jnp.pad((dck + dcq).reshape(FOX_HEADS, t).T, ((0, 0), (0, LANES - FOX_HEADS)))
    dflog, db_f = _fgate_bwd(dcum, flog, b_f, name="fgate_bwd")
    du2 = [dq, dk, dv, dgate2]
    dflog_b = dflog.astype(BF16)
    dh1 = _matmul_kparts(du2, fwi[:, :4 * f], chunk=f, name="d_h1_a")
    dh1 = _matmul(dflog_b, w_f, tb=True, add=dh1, name="d_h1_b")
    d_fwi_a = _matmul_nparts(h1, du2, chunk=f, out_dtype=BF16, name="d_fox_w_in_a")
    d_fwi_b = _matmul(h1, dflog_b, ta=True, out_dtype=BF16, name="d_fox_w_in_b")
    d_fwi = jnp.concatenate([d_fwi_a, d_fwi_b[:, :FOX_HEADS]], axis=1)
    dx1, dx1_b, dg1 = _rmsnorm_bwd(dh1, x1, norm_g[1], dx2, name="norm1_bwd", bf16_copy=True)

    d_lwo = _matmul(y1, dx1_b, ta=True, out_dtype=BF16, tm=w, name="d_lru_w_out")
    dy1 = _matmul(dx1_b, lwo, tb=True, tn=w, name="d_y1")
    n_fwi = fox_w_in.shape[2]
    g_fwi4 = jnp.stack([d_fwi[:, s * n_fwi:(s + 1) * n_fwi] for s in range(N_CHIPS)])
    g_fwo4 = d_fwo.reshape(N_CHIPS, f // N_CHIPS, d)
    g_lwo4 = d_lwo.reshape(N_CHIPS, w // N_CHIPS, d)
    (dxb, dgate, d_cw, d_cb, d_wa, d_ba, d_wx, d_bx, d_ap), lands_early = _lru_bwd(
        u, hs, dy1, conv_w, conv_b, wa, ba, wx, bx, a_param, cg=cg, name="lru_bwd", ride=[g_lwo4, g_fwi4, g_fwo4])
    d_lwi_p = _matmul_nparts(h0, [dxb, dgate], chunk=cg, out_dtype=BF16, name="d_lru_w_in")
    csz = cg
    g_lwi4 = jnp.stack([d_lwi_p[:, 0:csz], d_lwi_p[:, 2 * csz:3 * csz], d_lwi_p[:, csz:2 * csz],
                        d_lwi_p[:, 3 * csz:]])
    dh0, lands_last = _matmul_kparts([dxb, dgate], lwi, chunk=cg, name="d_h0", ride=[g_lwi4])
    dx0, _, dg0 = _rmsnorm_bwd(dh0, x0, norm_g[0], dx1, name="norm0_bwd", bf16_copy=False)
    lands = list(lands_last) + list(lands_early)
    halves = [_sum_slots(l, name="sum_" + nm) for l, nm in zip(lands, ("lru_w_in", "lru_w_out", "fox_w_in", "fox_w_out"))]
    big_g = _join_cores(halves, name="join_cores")

    small_g = [jnp.concatenate([dg0, dg1], axis=0), dgf.reshape(d), d_cw, d_cb, d_wa, d_ba, d_wx, d_bx, d_ap,
               db_f[:, :FOX_HEADS]]
    gsum = _allreduce_small(_pack(small_g), name="allreduce_small")
    zc = jnp.zeros((CONV_WIDTH, w), F32)
    pk_w = _pack([norm_g, final_g, zc, lru_conv_b, lru_wa, lru_ba, lru_wx, lru_bx, lru_a_param, fox_b_f])
    pk_m = _pack([m_norm_g, m_final_g, zc, m_lru_conv_b, m_lru_wa, m_lru_ba, m_lru_wx, m_lru_bx, m_lru_a_param,
                  m_fox_b_f])
    pk_v = _pack([v_norm_g, v_final_g, zc + 1.0, v_lru_conv_b, v_lru_wa, v_lru_ba, v_lru_wx, v_lru_bx,
                  v_lru_a_param, v_fox_b_f])
    s_g, s_delta, s_m, s_v = _adamw(pk_w[None], gsum[None], pk_m[None], pk_v[None], name="adamw_small")
    out_shapes = [norm_g.shape, final_g.shape, (CONV_WIDTH, w), lru_conv_b.shape, lru_wa.shape, lru_ba.shape,
                  lru_wx.shape, lru_bx.shape, lru_a_param.shape, fox_b_f.shape]
    sg = _unpack(s_g, out_shapes)
    sd = _unpack(s_delta, out_shapes)
    sm = _unpack(s_m, out_shapes)
    sv = _unpack(s_v, out_shapes)

    ncw = lru_conv_w.shape[2]
    g_cw_loc = lax.dynamic_slice_in_dim(sg[2], chip * ncw, ncw, axis=1)
    g_cw_loc, cw_d, cw_m, cw_v = _adamw(lru_conv_w, g_cw_loc[None], m_lru_conv_w, v_lru_conv_w, name="adamw_conv_w")

    big = []
    for nm, wt, g, mm, vv in (("lru_w_in", lru_w_in, big_g[0], m_lru_w_in, v_lru_w_in),
                              ("lru_w_out", lru_w_out, big_g[1], m_lru_w_out, v_lru_w_out),
                              ("fox_w_in", fox_w_in, big_g[2], m_fox_w_in, v_fox_w_in),
                              ("fox_w_out", fox_w_out, big_g[3], m_fox_w_out, v_fox_w_out)):
        big.append(tuple(_adamw(wt, g, mm, vv, name="adamw_" + nm)))

    def assemble(idx):
        small = (sg, sd, sm, sv)[idx]
        cw = (g_cw_loc, cw_d, cw_m, cw_v)[idx]
        return [small[0], small[1], big[0][idx], cw, small[3], small[4], small[5], small[6], small[7], small[8],
                big[1][idx], big[2][idx], small[9], big[3][idx]]

    grad_x = dx0.reshape(1, t, d)
    return (loss, grad_x, *assemble(0), *assemble(1), *assemble(2), *assemble(3))
```

```python
import math

import jax
import jax.numpy as jnp
from jax import lax
from jax.experimental import pallas as pl
from jax.experimental.pallas import tpu as pltpu

F32 = jnp.float32
BF16 = jnp.bfloat16

EPS = 1e-6
LRU_C = 8.0
LRU_BLOCK_W = 128
CONV_WIDTH = 4
FOX_HEADS = 16
FOX_HEAD_DIM = 64
NEG_INF = -1e30
ADAM_LR = 0.001
ADAM_B1 = 0.9
ADAM_B2 = 0.999
ADAM_EPS = 1e-08
ADAM_WD = 0.01
ADAM_STEP = 10

LANES = 128
SUBLANES = 8
VMEM_LIMIT = 56 * 1024 * 1024
TINY = 1e-30
N_CHIPS = 4
N_DEV = 8
MESH = pl.DeviceIdType.MESH


def _tile(n, pref):
    t = min(n, pref)
    while n % t:
        t //= 2
    return t


def _cparams(dims=None):
    return pltpu.CompilerParams(dimension_semantics=dims, vmem_limit_bytes=VMEM_LIMIT)


def _sigmoid(x):
    return 0.5 * jnp.tanh(0.5 * x) + 0.5


def _log1p(x):
    u = 1.0 + x
    return jnp.where(u == 1.0, x, jnp.log(u) * x / (u - 1.0))


def _bf16_pieces(x):
    hi = x.astype(BF16)
    rest = x - hi.astype(F32)
    mid = rest.astype(BF16)
    return hi, mid, (rest - mid.astype(F32)).astype(BF16)


def _dot_01_left(m01, x):
    return sum(jnp.dot(m01, p, preferred_element_type=F32) for p in _bf16_pieces(x))


def _dot_01_right(x, m01):
    return sum(jnp.dot(p, m01, preferred_element_type=F32) for p in _bf16_pieces(x))


def _softplus(x):
    return jnp.maximum(x, 0.0) + _log1p(jnp.exp(-jnp.abs(x)))


MM_TILE = 1024
MM_FULL_K = 1536


def _matmul(a, b, *, name, ta=False, tb=False, out_dtype=F32, add=None, tm=MM_TILE, tn=MM_TILE, tk=None):
    if ta:
        kdim, m = a.shape
    else:
        m, kdim = a.shape
    if tb:
        n, kb = b.shape
    else:
        kb, n = b.shape
    assert kdim == kb, (a.shape, b.shape, ta, tb)
    if tk is None:
        tk = kdim if kdim <= MM_FULL_K else MM_TILE
    tm, tn, tk = _tile(m, tm), _tile(n, tn), _tile(kdim, tk)
    nk = kdim // tk
    dn = (((0 if ta else 1,), (1 if tb else 0,)), ((), ()))
    has_add = add is not None

    def body(*refs):
        if has_add:
            a_ref, b_ref, add_ref, o_ref = refs[:4]
        else:
            a_ref, b_ref, o_ref = refs[:3]
        part = lax.dot_general(a_ref[...].astype(BF16), b_ref[...].astype(BF16), dn, preferred_element_type=F32)

        def finish(r):
            if has_add:
                r = r + add_ref[...].astype(F32)
            o_ref[...] = r.astype(o_ref.dtype)

        if nk == 1:
            finish(part)
            return
        acc_ref = refs[-1]
        k = pl.program_id(2)

        @pl.when(k == 0)
        def _():
            acc_ref[...] = part

        @pl.when(k > 0)
        def _():
            acc_ref[...] += part

        @pl.when(k == nk - 1)
        def _():
            finish(acc_ref[...])

    a_spec = pl.BlockSpec((tk, tm), lambda i, j, k: (k, i)) if ta else pl.BlockSpec((tm, tk), lambda i, j, k: (i, k))
    b_spec = pl.BlockSpec((tn, tk), lambda i, j, k: (j, k)) if tb else pl.BlockSpec((tk, tn), lambda i, j, k: (k, j))
    o_spec = pl.BlockSpec((tm, tn), lambda i, j, k: (i, j))
    in_specs = [a_spec, b_spec] + ([o_spec] if has_add else [])
    args = (a, b) + ((add,) if has_add else ())
    return pl.pallas_call(
        body, name=name, grid=(m // tm, n // tn, nk), in_specs=in_specs, out_specs=o_spec,
        out_shape=jax.ShapeDtypeStruct((m, n), out_dtype),
        scratch_shapes=[pltpu.VMEM((tm, tn), F32)] if nk > 1 else [],
        compiler_params=_cparams(("parallel", "parallel", "arbitrary")),
    )(*args)


def _matmul_kparts(parts, b, *, chunk, name, tm=MM_TILE, tn=MM_TILE, ride=()):
    npart = len(parts)
    nride = len(ride)
    m = parts[0].shape[0]
    n, kdim = b.shape
    nk = kdim // chunk
    assert nk * chunk == kdim and sum(p.shape[1] for p in parts) == kdim and nk % npart == 0
    tm, tn = _tile(m, tm), _tile(n, tn)
    dn = (((1,), (1,)), ((), ()))
    steps = (m // tm) * (n // tn) * nk

    def body(*refs):
        ins, rest = refs[:npart + 1], refs[npart + 1:]
        ride_in, rest = rest[:nride], rest[nride:]
        o_ref, rest = rest[0], rest[1:]
        ride_out, rest = rest[:nride], rest[nride:]
        acc_ref, sems = rest[0], rest[1:]
        if not nride:
            core(*ins, o_ref, acc_ref)
            return
        step = (pl.program_id(0) * (n // tn) + pl.program_id(1)) * nk + pl.program_id(2)

        @pl.when(step == 0)
        def _():
            for cp in _scatter_copies(ride_in, ride_out, *sems):
                cp.start()

        core(*ins, o_ref, acc_ref)

        @pl.when(step == steps - 1)
        def _():
            for cp in _scatter_copies(ride_in, ride_out, *sems):
                cp.wait()

    def core(*refs):
        a_refs, b_ref, o_ref, acc_ref = refs[:npart], refs[npart], refs[npart + 1], refs[npart + 2]
        k = pl.program_id(2)

        @pl.when(k == 0)
        def _():
            acc_ref[...] = jnp.zeros_like(acc_ref)

        for s in range(npart):
            @pl.when(lax.rem(k, npart) == s)
            def _(s=s):
                acc_ref[...] += lax.dot_general(a_refs[s][...].astype(BF16), b_ref[...].astype(BF16), dn,
                                                preferred_element_type=F32)

        @pl.when(k == nk - 1)
        def _():
            o_ref[...] = acc_ref[...].astype(o_ref.dtype)

    a_specs = [pl.BlockSpec((tm, chunk), lambda i, j, k: (i, k // npart)) for _ in range(npart)]
    any_spec = pl.BlockSpec(memory_space=pl.ANY)
    ride_shape, ride_sems = _scatter_shapes(ride) if nride else ([], [])
    outs = pl.pallas_call(
        body, name=name, grid=(m // tm, n // tn, nk),
        in_specs=a_specs + [pl.BlockSpec((tn, chunk), lambda i, j, k: (j, k))] + [any_spec] * nride,
        out_specs=[pl.BlockSpec((tm, tn), lambda i, j, k: (i, j))] + [any_spec] * nride,
        out_shape=[jax.ShapeDtypeStruct((m, n), F32)] + ride_shape,
        scratch_shapes=[pltpu.VMEM((tm, tn), F32)] + ride_sems,
        compiler_params=_cparams(("arbitrary",) * 3 if nride else ("parallel", "parallel", "arbitrary")),
    )(*parts, b, *ride)
    return (outs[0], outs[1:]) if nride else outs[0]


def _matmul_nparts(a, parts, *, chunk, out_dtype, name, tm=MM_TILE, tk=MM_TILE):
    npart = len(parts)
    t, m = a.shape
    n = sum(p.shape[1] for p in parts)
    nj = n // chunk
    assert nj * chunk == n and nj % npart == 0
    tm, tk = _tile(m, tm), _tile(t, tk)
    nk = t // tk
    dn = (((0,), (0,)), ((), ()))

    def body(*refs):
        a_ref, b_refs, o_ref, acc_ref = refs[0], refs[1:1 + npart], refs[1 + npart], refs[2 + npart]
        j, k = pl.program_id(1), pl.program_id(2)

        @pl.when(k == 0)
        def _():
            acc_ref[...] = jnp.zeros_like(acc_ref)

        for s in range(npart):
            @pl.when(lax.rem(j, npart) == s)
            def _(s=s):
                acc_ref[...] += lax.dot_general(a_ref[...].astype(BF16), b_refs[s][...].astype(BF16), dn,
                                                preferred_element_type=F32)

        @pl.when(k == nk - 1)
        def _():
            o_ref[...] = acc_ref[...].astype(o_ref.dtype)

    def b_spec(s):
        return pl.BlockSpec((tk, chunk), lambda i, j, k: (jnp.where(lax.rem(j, npart) == s, k, 0), j // npart))

    return pl.pallas_call(
        body, name=name, grid=(m // tm, nj, nk),
        in_specs=[pl.BlockSpec((tk, tm), lambda i, j, k: (k, i))] + [b_spec(s) for s in range(npart)],
        out_specs=pl.BlockSpec((tm, chunk), lambda i, j, k: (i, j)),
        out_shape=jax.ShapeDtypeStruct((m, n), out_dtype),
        scratch_shapes=[pltpu.VMEM((tm, chunk), F32)],
        compiler_params=_cparams(("parallel", "parallel", "arbitrary")),
    )(a, *parts)


def _rmsnorm(x, g, *, name):
    t, d = x.shape
    tt = _tile(t, 512)

    def body(x_ref, g_ref, o_ref):
        xf = x_ref[...]
        rstd = lax.rsqrt(jnp.mean(xf * xf, axis=-1, keepdims=True) + EPS)
        o_ref[...] = (xf * rstd * g_ref[...]).astype(o_ref.dtype)

    return pl.pallas_call(
        body, name=name, grid=(t // tt,),
        in_specs=[pl.BlockSpec((tt, d), lambda i: (i, 0)), pl.BlockSpec((1, d), lambda i: (0, 0))],
        out_specs=pl.BlockSpec((tt, d), lambda i: (i, 0)),
        out_shape=jax.ShapeDtypeStruct((t, d), BF16),
        compiler_params=_cparams(("parallel",)),
    )(x, g.reshape(1, d))


def _rmsnorm_bwd(dh, x, g, dres, *, name, bf16_copy):
    t, d = x.shape
    tt = _tile(t, 512)
    nt = t // tt

    def body(dh_ref, x_ref, g_ref, dres_ref, dx_ref, *out_refs):
        dg_ref = out_refs[-1]
        i = pl.program_id(0)

        @pl.when(i == 0)
        def _():
            dg_ref[...] = jnp.zeros_like(dg_ref)

        xf = x_ref[...]
        rstd = lax.rsqrt(jnp.mean(xf * xf, axis=-1, keepdims=True) + EPS)
        xhat = xf * rstd
        dhf = dh_ref[...].astype(F32)
        dxhat = dhf * g_ref[...]
        mt = jnp.mean(dxhat * xhat, axis=-1, keepdims=True)
        dx = dres_ref[...] + rstd * (dxhat - xhat * mt)
        dx_ref[...] = dx
        if bf16_copy:
            out_refs[0][...] = dx.astype(BF16)
        dg_ref[...] += jnp.sum(dhf * xhat, axis=0, keepdims=True)

    blk = pl.BlockSpec((tt, d), lambda i: (i, 0))
    vec = pl.BlockSpec((1, d), lambda i: (0, 0))
    low = [jax.ShapeDtypeStruct((t, d), BF16)] if bf16_copy else []
    outs = pl.pallas_call(
        body, name=name, grid=(nt,),
        in_specs=[blk, blk, vec, blk], out_specs=[blk] + [blk] * len(low) + [vec],
        out_shape=[jax.ShapeDtypeStruct((t, d), F32)] + low + [jax.ShapeDtypeStruct((1, d), F32)],
        compiler_params=_cparams(("arbitrary",)),
    )(dh, x, g.reshape(1, d), dres)
    return outs[0], (outs[1] if bf16_copy else None), outs[-1]


def _final_loss(x2, tgt, g, *, name):
    t, d = x2.shape
    tt = _tile(t, 512)

    def body(x_ref, t_ref, g_ref, l_ref, dx_ref, dxb_ref, dg_ref):
        i = pl.program_id(0)

        @pl.when(i == 0)
        def _():
            dg_ref[...] = jnp.zeros_like(dg_ref)
            l_ref[...] = jnp.zeros_like(l_ref)

        xf = x_ref[...]
        gg = g_ref[...]
        rstd = lax.rsqrt(jnp.mean(xf * xf, axis=-1, keepdims=True) + EPS)
        xhat = xf * rstd
        err = xhat * gg - t_ref[...]
        l_ref[...] += jnp.sum(err * err, axis=0, keepdims=True)
        dy = err * (1.0 / d)
        dxhat = dy * gg
        mt = jnp.mean(dxhat * xhat, axis=-1, keepdims=True)
        dx = rstd * (dxhat - xhat * mt)
        dx_ref[...] = dx
        dxb_ref[...] = dx.astype(dxb_ref.dtype)
        dg_ref[...] += jnp.sum(dy * xhat, axis=0, keepdims=True)

    blk = pl.BlockSpec((tt, d), lambda i: (i, 0))
    vec = pl.BlockSpec((1, d), lambda i: (0, 0))
    return pl.pallas_call(
        body, name=name, grid=(t // tt,),
        in_specs=[blk, blk, vec], out_specs=[vec, blk, blk, vec],
        out_shape=[jax.ShapeDtypeStruct((1, d), F32), jax.ShapeDtypeStruct((t, d), F32),
                   jax.ShapeDtypeStruct((t, d), BF16), jax.ShapeDtypeStruct((1, d), F32)],
        compiler_params=_cparams(("arbitrary",)),
    )(x2, tgt, g.reshape(1, d))


def _shift_down(prev8, cur, s):
    ext = jnp.concatenate([prev8, cur], axis=0)
    if s == 0:
        return cur
    return pltpu.roll(ext, s, 0)[SUBLANES:, :]


def _shift_up(cur, next8, s):
    if s == 0:
        return cur
    n = cur.shape[0]
    ext = jnp.concatenate([cur, next8], axis=0)
    return pltpu.roll(ext, n + SUBLANES - s, 0)[:n, :]


def _lru_gates(xc, wa, ba, wx, bx, sp):
    xcb = xc.astype(BF16)
    r = _sigmoid(jnp.dot(xcb, wa, preferred_element_type=F32) + ba)
    ig = _sigmoid(jnp.dot(xcb, wx, preferred_element_type=F32) + bx)
    log_a = -LRU_C * r * sp
    a = jnp.exp(log_a)
    z = -jnp.tanh(log_a) * (a * a + 1.0)
    inv_mult = lax.rsqrt(jnp.maximum(z, TINY))
    return r, ig, a, z * inv_mult, inv_mult


def _lru_specs(tt, cg, n_groups, nt, reverse):
    ncol = cg // LANES
    if reverse:
        ti = lambda i: nt - 1 - i
    else:
        ti = lambda i: i
    hb = tt // SUBLANES
    cur = lambda col: pl.BlockSpec((tt, cg), lambda g, i: (ti(i), 2 * g + col))
    prev = lambda col: pl.BlockSpec((SUBLANES, cg), lambda g, i: (jnp.maximum(ti(i) * hb - 1, 0), 2 * g + col))
    chan = lambda rows: pl.BlockSpec((rows, cg), lambda g, i: (0, g))
    wblk = pl.BlockSpec((ncol, LRU_BLOCK_W, LRU_BLOCK_W), lambda g, i: (g, 0, 0))
    plain = pl.BlockSpec((tt, cg), lambda g, i: (ti(i), g))
    plain_prev = pl.BlockSpec((SUBLANES, cg), lambda g, i: (jnp.maximum(ti(i) * hb - 1, 0), g))
    return cur, prev, chan, wblk, plain, plain_prev


def _chip_gather_copies(ins, outs, send, recv, loc):
    n = len(ins)
    x, y, c = lax.axis_index("x"), lax.axis_index("y"), lax.axis_index("c")
    me = 2 * x + y
    copies = [pltpu.make_async_copy(ins[k], outs[k].at[me], loc.at[k]) for k in range(n)]
    for r, (rx, ry) in enumerate(((1, 0), (0, 1), (1, 1))):
        for k in range(n):
            copies.append(pltpu.make_async_remote_copy(
                src_ref=ins[k], dst_ref=outs[k].at[me], send_sem=send.at[r * n + k], recv_sem=recv.at[r * n + k],
                device_id=(_flip(x, rx), _flip(y, ry), c), device_id_type=MESH))
    return copies


def _lru_fwd(u, conv_w, conv_b, wa, ba, wx, bx, a_param, *, cg, name, ride=()):
    nride = len(ride)
    t, w2 = u.shape
    w = w2 // 2
    n_groups = w // cg
    ncol = cg // LANES
    tt = _tile(t, 256)
    nt = t // tt
    cur, prev, chan, wblk, plain, _ = _lru_specs(tt, cg, n_groups, nt, False)

    def body(*refs):
        n_in, n_out, n_scr = 10, 2, 3
        ins, rest = refs[:n_in], refs[n_in:]
        ride_in, rest = rest[:nride], rest[nride:]
        outs, rest = rest[:n_out], rest[n_out:]
        ride_out, rest = rest[:nride], rest[nride:]
        scr, sems = rest[:n_scr], rest[n_scr:]
        if not nride:
            core(*ins, *outs, *scr)
            return
        step = pl.program_id(0) * nt + pl.program_id(1)

        @pl.when(step == 0)
        def _():
            for cp in _chip_gather_copies(ride_in, ride_out, *sems):
                cp.start()

        core(*ins, *outs, *scr)

        @pl.when(step == n_groups * nt - 1)
        def _():
            for cp in _chip_gather_copies(ride_in, ride_out, *sems):
                cp.wait()

    def core(xb_ref, xp_ref, gate_ref, cw_ref, cb_ref, wa_ref, ba_ref, wx_ref, bx_ref, ap_ref,
             y_ref, hs_ref, h_ref, a_s, b_s):
        i = pl.program_id(1)

        @pl.when(i == 0)
        def _():
            h_ref[...] = jnp.zeros_like(h_ref)

        keep = (i > 0).astype(F32)
        for n in range(ncol):
            sl = slice(n * LANES, (n + 1) * LANES)
            xb = xb_ref[:, sl]
            xp = xp_ref[:, sl] * keep
            xc = cb_ref[:, sl] + cw_ref[3:4, sl] * xb
            for s in range(1, CONV_WIDTH):
                xc = xc + cw_ref[3 - s:4 - s, sl] * _shift_down(xp, xb, s)
            sp = _softplus(-ap_ref[:, sl])
            _, ig, a, mult, _ = _lru_gates(xc, wa_ref[n].astype(BF16), ba_ref[:, sl],
                                           wx_ref[n].astype(BF16), bx_ref[:, sl], sp)
            a_s[:, sl] = a
            b_s[:, sl] = mult * (ig * xc)

        def step(g, h):
            base = pl.multiple_of(g * SUBLANES, SUBLANES)
            for r in range(SUBLANES):
                h = a_s[pl.ds(base + r, 1), :] * h + b_s[pl.ds(base + r, 1), :]
                hs_ref[pl.ds(base + r, 1), :] = h
            return h

        h = lax.fori_loop(0, tt // SUBLANES, step, h_ref[0:1, :])
        h_ref[0:1, :] = h
        gate = gate_ref[...]
        y_ref[...] = (hs_ref[...] * (gate * _sigmoid(gate))).astype(y_ref.dtype)

    any_spec = pl.BlockSpec(memory_space=pl.ANY)
    ride_sems = [pltpu.SemaphoreType.DMA((3 * nride,)), pltpu.SemaphoreType.DMA((3 * nride,)),
                 pltpu.SemaphoreType.DMA((nride,))] if nride else []
    outs = pl.pallas_call(
        body, name=name, grid=(n_groups, nt),
        in_specs=[cur(0), prev(0), cur(1), chan(CONV_WIDTH), chan(1), wblk, chan(1), wblk, chan(1), chan(1)]
        + [any_spec] * nride,
        out_specs=[plain, plain] + [any_spec] * nride,
        out_shape=[jax.ShapeDtypeStruct((t, w), BF16), jax.ShapeDtypeStruct((t, w), F32)]
        + [jax.ShapeDtypeStruct((N_CHIPS,) + r.shape, r.dtype) for r in ride],
        scratch_shapes=[pltpu.VMEM((SUBLANES, cg), F32), pltpu.VMEM((tt, cg), F32), pltpu.VMEM((tt, cg), F32)]
        + ride_sems,
        compiler_params=_cparams(("arbitrary", "arbitrary")),
    )(u, u, u, conv_w, conv_b, wa, ba, wx, bx, a_param, *ride)
    return outs[0], outs[1], outs[2:]


def _lru_bwd(u, hs, dy, conv_w, conv_b, wa, ba, wx, bx, a_param, *, cg, name, ride=()):
    nride = len(ride)
    t, w2 = u.shape
    w = w2 // 2
    n_groups = w // cg
    ncol = cg // LANES
    tt = _tile(t, 256)
    nt = t // tt
    cur, prev, chan, wblk, plain, plain_prev = _lru_specs(tt, cg, n_groups, nt, True)
    tn_dims = (((0,), (0,)), ((), ()))
    nt_dims = (((1,), (1,)), ((), ()))

    def body(*refs):
        n_in, n_out, n_scr = 13, 9, 5
        ins, rest = refs[:n_in], refs[n_in:]
        ride_in, rest = rest[:nride], rest[nride:]
        outs, rest = rest[:n_out], rest[n_out:]
        ride_out, rest = rest[:nride], rest[nride:]
        scr, sems = rest[:n_scr], rest[n_scr:]
        if not nride:
            core(*ins, *outs, *scr)
            return
        step = pl.program_id(0) * nt + pl.program_id(1)

        @pl.when(step == 0)
        def _():
            for cp in _scatter_copies(ride_in, ride_out, *sems):
                cp.start()

        core(*ins, *outs, *scr)

        @pl.when(step == n_groups * nt - 1)
        def _():
            for cp in _scatter_copies(ride_in, ride_out, *sems):
                cp.wait()

    def core(xb_ref, xp_ref, gate_ref, hs_ref, hp_ref, dy_ref, cw_ref, cb_ref, wa_ref, ba_ref, wx_ref, bx_ref,
             ap_ref, dxb_ref, dgate_ref, dcw_ref, dcb_ref, dwa_ref, dba_ref, dwx_ref, dbx_ref, dsp_ref,
             c_ref, nx_ref, a_s, dhs_s, lam_s):
        i = pl.program_id(1)
        first_time_block = i == nt - 1

        @pl.when(i == 0)
        def _():
            c_ref[...] = jnp.zeros_like(c_ref)
            nx_ref[...] = jnp.zeros_like(nx_ref)
            for r in (dcw_ref, dcb_ref, dwa_ref, dba_ref, dwx_ref, dbx_ref, dsp_ref):
                r[...] = jnp.zeros_like(r)

        keep = jnp.where(first_time_block, 0.0, 1.0).astype(F32)
        gate = gate_ref[...]
        sg = _sigmoid(gate)
        dyv = dy_ref[...]
        hsv = hs_ref[...]
        dhs_s[...] = dyv * (gate * sg)
        dgate_ref[...] = (dyv * hsv * (sg * (1.0 + gate * (1.0 - sg)))).astype(dgate_ref.dtype)

        saved = []
        for n in range(ncol):
            sl = slice(n * LANES, (n + 1) * LANES)
            xb = xb_ref[:, sl]
            xp = xp_ref[:, sl] * keep
            shifted = [xb] + [_shift_down(xp, xb, s) for s in range(1, CONV_WIDTH)]
            xc = cb_ref[:, sl] + cw_ref[3:4, sl] * xb
            for s in range(1, CONV_WIDTH):
                xc = xc + cw_ref[3 - s:4 - s, sl] * shifted[s]
            sp = _softplus(-ap_ref[:, sl])
            wab = wa_ref[n].astype(BF16)
            wxb = wx_ref[n].astype(BF16)
            r, ig, a, mult, inv_mult = _lru_gates(xc, wab, ba_ref[:, sl], wxb, bx_ref[:, sl], sp)
            a_s[:, sl] = a
            saved.append((sl, shifted, xc, sp, wab, wxb, r, ig, a, mult, inv_mult))

        def step(g, c):
            base = pl.multiple_of(tt - SUBLANES - g * SUBLANES, SUBLANES)
            for r in range(SUBLANES - 1, -1, -1):
                lam = dhs_s[pl.ds(base + r, 1), :] + c
                lam_s[pl.ds(base + r, 1), :] = lam
                c = a_s[pl.ds(base + r, 1), :] * lam
            return c

        c_ref[0:1, :] = lax.fori_loop(0, tt // SUBLANES, step, c_ref[0:1, :])

        for n in range(ncol):
            sl, shifted, xc, sp, wab, wxb, r, ig, a, mult, inv_mult = saved[n]
            lam = lam_s[:, sl]
            hprev = _shift_down(hp_ref[:, sl] * keep, hs_ref[:, sl], 1)
            da = lam * hprev
            dmult = lam * (ig * xc)
            dlog_a = da * a - dmult * (a * a * inv_mult)
            di = lam * (mult * xc)
            dxc = lam * (mult * ig)
            dr = dlog_a * (-LRU_C * sp)
            dsp_ref[:, sl] += jnp.sum(dlog_a * (-LRU_C * r), axis=0, keepdims=True)
            dza = dr * (r * (1.0 - r))
            dzx = di * (ig * (1.0 - ig))
            dba_ref[:, sl] += jnp.sum(dza, axis=0, keepdims=True)
            dbx_ref[:, sl] += jnp.sum(dzx, axis=0, keepdims=True)
            xcb = xc.astype(BF16)
            dzab = dza.astype(BF16)
            dzxb = dzx.astype(BF16)
            dwa_ref[n] += lax.dot_general(xcb, dzab, tn_dims, preferred_element_type=F32)
            dwx_ref[n] += lax.dot_general(xcb, dzxb, tn_dims, preferred_element_type=F32)
            dxc = dxc + lax.dot_general(dzab, wab, nt_dims, preferred_element_type=F32)
            dxc = dxc + lax.dot_general(dzxb, wxb, nt_dims, preferred_element_type=F32)
            dcb_ref[:, sl] += jnp.sum(dxc, axis=0, keepdims=True)
            for s in range(CONV_WIDTH):
                dcw_ref[3 - s:4 - s, sl] += jnp.sum(dxc * shifted[s], axis=0, keepdims=True)
            nx = nx_ref[:, sl]
            dxb = cw_ref[3:4, sl] * dxc
            for s in range(1, CONV_WIDTH):
                dxb = dxb + cw_ref[3 - s:4 - s, sl] * _shift_up(dxc, nx, s)
            dxb_ref[:, sl] = dxb.astype(dxb_ref.dtype)
            nx_ref[:, sl] = dxc[0:SUBLANES, :]

        @pl.when(first_time_block)
        def _():
            dsp_ref[...] = dsp_ref[...] * (-_sigmoid(-ap_ref[...]))

    dxb_spec = pl.BlockSpec((tt, cg), lambda g, i: (nt - 1 - i, g))
    any_spec = pl.BlockSpec(memory_space=pl.ANY)
    ride_shape, ride_sems = _scatter_shapes(ride) if nride else ([], [])
    outs = pl.pallas_call(
        body, name=name, grid=(n_groups, nt),
        in_specs=[cur(0), prev(0), cur(1), plain, plain_prev, plain, chan(CONV_WIDTH), chan(1), wblk, chan(1), wblk,
                  chan(1), chan(1)] + [any_spec] * nride,
        out_specs=[dxb_spec, dxb_spec, chan(CONV_WIDTH), chan(1), wblk, chan(1), wblk, chan(1), chan(1)]
        + [any_spec] * nride,
        out_shape=[jax.ShapeDtypeStruct((t, w), BF16), jax.ShapeDtypeStruct((t, w), BF16),
                   jax.ShapeDtypeStruct(conv_w.shape, F32), jax.ShapeDtypeStruct(conv_b.shape, F32),
                   jax.ShapeDtypeStruct(wa.shape, F32), jax.ShapeDtypeStruct(ba.shape, F32),
                   jax.ShapeDtypeStruct(wx.shape, F32), jax.ShapeDtypeStruct(bx.shape, F32),
                   jax.ShapeDtypeStruct(a_param.shape, F32)] + ride_shape,
        scratch_shapes=[pltpu.VMEM((SUBLANES, cg), F32), pltpu.VMEM((SUBLANES, cg), F32),
                        pltpu.VMEM((tt, cg), F32), pltpu.VMEM((tt, cg), F32), pltpu.VMEM((tt, cg), F32)] + ride_sems,
        compiler_params=_cparams(("arbitrary", "arbitrary")),
    )(u, u, u, hs, hs, dy, conv_w, conv_b, wa, ba, wx, bx, a_param, *ride)
    return outs[:9], outs[9:]


def _fgate_fwd(f, b_f, *, name):
    t, n = f.shape
    tt = _tile(t, 256)
    width = FOX_HEADS * FOX_HEAD_DIM

    def body(f_ref, b_ref, cum_ref, wide_ref, carry_ref):
        i = pl.program_id(0)

        @pl.when(i == 0)
        def _():
            carry_ref[...] = jnp.zeros_like(carry_ref)

        z = f_ref[...] + b_ref[...]
        lf = jnp.minimum(z, 0.0) - _log1p(jnp.exp(-jnp.abs(z)))
        row = lax.broadcasted_iota(jnp.int32, (tt, tt), 0)
        col = lax.broadcasted_iota(jnp.int32, (tt, tt), 1)
        tri = (col <= row).astype(BF16)
        cum = _dot_01_left(tri, lf) + carry_ref[0:1, :]
        cum_ref[...] = cum
        carry_ref[0:1, :] = cum[tt - 1:tt, :]
        head = lax.broadcasted_iota(jnp.int32, (n, width), 0)
        chan = lax.broadcasted_iota(jnp.int32, (n, width), 1) // FOX_HEAD_DIM
        wide_ref[...] = _dot_01_right(cum, (head == chan).astype(BF16))

    return pl.pallas_call(
        body, name=name, grid=(t // tt,),
        in_specs=[pl.BlockSpec((tt, n), lambda i: (i, 0)), pl.BlockSpec((1, n), lambda i: (0, 0))],
        out_specs=[pl.BlockSpec((tt, n), lambda i: (i, 0)), pl.BlockSpec((tt, width), lambda i: (i, 0))],
        out_shape=[jax.ShapeDtypeStruct((t, n), F32), jax.ShapeDtypeStruct((t, width), F32)],
        scratch_shapes=[pltpu.VMEM((SUBLANES, n), F32)],
        compiler_params=_cparams(("arbitrary",)),
    )(f, b_f)


def _fgate_bwd(dcum, f, b_f, *, name):
    t, n = f.shape
    tt = _tile(t, 256)
    nt = t // tt

    def body(dc_ref, f_ref, b_ref, df_ref, db_ref, carry_ref):
        i = pl.program_id(0)

        @pl.when(i == 0)
        def _():
            carry_ref[...] = jnp.zeros_like(carry_ref)
            db_ref[...] = jnp.zeros_like(db_ref)

        row = lax.broadcasted_iota(jnp.int32, (tt, tt), 0)
        col = lax.broadcasted_iota(jnp.int32, (tt, tt), 1)
        triu = (col >= row).astype(BF16)
        dlf = _dot_01_left(triu, dc_ref[...]) + carry_ref[0:1, :]
        carry_ref[0:1, :] = dlf[0:1, :]
        z = f_ref[...] + b_ref[...]
        df = dlf * _sigmoid(-z)
        df_ref[...] = df
        db_ref[...] += jnp.sum(df, axis=0, keepdims=True)

    blk = pl.BlockSpec((tt, n), lambda i: (nt - 1 - i, 0))
    vec = pl.BlockSpec((1, n), lambda i: (0, 0))
    return pl.pallas_call(
        body, name=name, grid=(nt,),
        in_specs=[blk, blk, vec], out_specs=[blk, vec],
        out_shape=[jax.ShapeDtypeStruct((t, n), F32), jax.ShapeDtypeStruct((1, n), F32)],
        scratch_shapes=[pltpu.VMEM((SUBLANES, n), F32)],
        compiler_params=_cparams(("arbitrary",)),
    )(dcum, f, b_f)


def _attn_fwd(start, qkv, ckt, gate, *, name, tq, tk):
    t = qkv.shape[0]
    f = gate.shape[1]
    npair = f // LANES
    nq = t // tq
    ratio = tq // tk
    assert tq == ratio * tk and t == nq * tq
    scale = 1.0 / math.sqrt(FOX_HEAD_DIM)
    nt_dims = (((1,), (1,)), ((), ()))

    def body(start_ref, q_ref, k_ref, v_ref, ck_ref, g_ref, o_ref, y_ref, l_ref, acc_a, acc_b):
        accs = (acc_a, acc_b)
        i = pl.program_id(1)
        pair = pl.program_id(0)
        firsts = (start_ref[2 * pair, i], start_ref[2 * pair + 1, i])
        both = jnp.maximum(firsts[0], firsts[1])
        lane = lax.broadcasted_iota(jnp.int32, (tq, LANES), 1)
        lo = lane < FOX_HEAD_DIM
        q2 = q_ref[...] * scale
        qs = (jnp.where(lo, q2, 0).astype(BF16), jnp.where(lo, 0, q2).astype(BF16))
        row = lax.broadcasted_iota(jnp.int32, (tq, tk), 0)
        col = lax.broadcasted_iota(jnp.int32, (tq, tk), 1)

        def kv_step(j, carry, diag, heads=(0, 1)):
            off = pl.multiple_of(j * tk, tk)
            kj = k_ref[pl.ds(off, tk), :]
            vj = v_ref[pl.ds(off, tk), :]
            ck = ck_ref[:, pl.ds(off, tk)]
            new = list(carry)
            for h in heads:
                m, l = carry[h]
                s = lax.dot_general(qs[h], kj, nt_dims, preferred_element_type=F32) - ck[h:h + 1, :]
                if diag is not None:
                    s = jnp.where(col + diag * tk <= row, s, NEG_INF)
                m_new = jnp.maximum(m, jnp.max(s, axis=-1, keepdims=True))
                alpha = jnp.exp(m - m_new)
                p = jnp.exp(s - m_new)
                l = alpha * l + jnp.sum(p, axis=-1, keepdims=True)
                accs[h][...] = alpha * accs[h][...] + jnp.dot(p.astype(BF16), vj, preferred_element_type=F32)
                new[h] = (m_new, l)
            return tuple(new)

        carry = tuple((jnp.full((tq, 1), NEG_INF, F32), jnp.zeros((tq, 1), F32)) for _ in range(2))
        acc_a[...] = jnp.zeros_like(acc_a)
        acc_b[...] = jnp.zeros_like(acc_b)

        def run(lo_blk, hi_blk, carry, heads):
            twos = (hi_blk - lo_blk) // 2
            carry = lax.fori_loop(
                0, twos,
                lambda jj, c: kv_step(lo_blk + 2 * jj + 1, kv_step(lo_blk + 2 * jj, c, None, heads), None, heads),
                carry)
            return lax.fori_loop(lo_blk + 2 * twos, hi_blk, lambda j, c: kv_step(j, c, None, heads), carry)

        for h in range(2):
            carry = lax.fori_loop(firsts[h], both, lambda j, c, h=h: kv_step(j, c, None, (h,)), carry)
        carry = run(both, i * ratio, carry, (0, 1))
        for d in range(ratio):
            carry = kv_step(i * ratio + d, carry, d)
        (m0, l0), (m1, l1) = carry
        o = jnp.where(lo, acc_a[...] / l0, acc_b[...] / l1)
        o_ref[...] = o
        gate_v = g_ref[...]
        y_ref[...] = (o * (gate_v * _sigmoid(gate_v))).astype(y_ref.dtype)
        lse_t = jnp.transpose(jnp.where(lo, m0 + jnp.log(l0), m1 + jnp.log(l1)))
        l_ref[0:1, :] = lse_t[0:1, :]
        l_ref[1:2, :] = lse_t[FOX_HEAD_DIM:FOX_HEAD_DIM + 1, :]

    blk = lambda base: pl.BlockSpec((tq, LANES), lambda p, i, s: (i, base + p))
    full = lambda base: pl.BlockSpec((t, LANES), lambda p, i, s: (0, base + p))
    return pl.pallas_call(
        body, name=name,
        grid_spec=pltpu.PrefetchScalarGridSpec(
            num_scalar_prefetch=1, grid=(npair, nq),
            in_specs=[blk(0), full(npair), full(2 * npair), pl.BlockSpec((None, 2, t), lambda p, i, s: (p, 0, 0)),
                      blk(0)],
            out_specs=[blk(0), blk(0), pl.BlockSpec((None, 2, tq), lambda p, i, s: (p, 0, i))],
            scratch_shapes=[pltpu.VMEM((tq, LANES), F32), pltpu.VMEM((tq, LANES), F32)]),
        out_shape=[jax.ShapeDtypeStruct((t, f), F32), jax.ShapeDtypeStruct((t, f), BF16),
                   jax.ShapeDtypeStruct((npair, 2, t), F32)],
        compiler_params=_cparams(("parallel", "arbitrary")),
    )(start, qkv, qkv, qkv, ckt, gate)


def _attn_bwd(end, qkv, do, lt, dt, cke, *, name):
    t, f = do.shape
    npair = f // LANES
    tk = _tile(t, ATTN_TILE)
    nk = t // tk
    scale = 1.0 / math.sqrt(FOX_HEAD_DIM)
    nt_dims = (((1,), (1,)), ((), ()))
    tn_dims = (((0,), (0,)), ((), ()))

    def body(end_ref, k_ref, v_ref, q_ref, do_ref, l_ref, d_ref, ck_ref, dq_out_ref, dk_ref, dv_ref, dck_ref, dcq_ref,
             dq_ref, dk_s, dv_s, dck_s):
        j = pl.program_id(1)
        pair = pl.program_id(0)
        lasts = (end_ref[2 * pair, j], end_ref[2 * pair + 1, j])
        both = jnp.minimum(lasts[0], lasts[1])

        @pl.when(j == 0)
        def _():
            dq_ref[...] = jnp.zeros_like(dq_ref)
            dcq_ref[...] = jnp.zeros_like(dcq_ref)

        lane = lax.broadcasted_iota(jnp.int32, (tk, LANES), 1)
        lo = lane < FOX_HEAD_DIM
        sel = (lo, jnp.logical_not(lo))
        kj = k_ref[...]
        vj = v_ref[...]
        km = tuple(jnp.where(sel[h], kj, 0).astype(BF16) for h in range(2))
        ckv = ck_ref[...]
        ckh = (ckv[:, 0:1], ckv[:, FOX_HEAD_DIM:FOX_HEAD_DIM + 1])
        row = lax.broadcasted_iota(jnp.int32, (tk, tk), 0)
        col = lax.broadcasted_iota(jnp.int32, (tk, tk), 1)
        causal = row <= col

        def q_step(i, carry, masked, heads=(0, 1)):
            off = pl.multiple_of(i * tk, tk)
            qi = q_ref[pl.ds(off, tk), :]
            doi = do_ref[pl.ds(off, tk), :]
            lrow = l_ref[:, pl.ds(off, tk)]
            drow = d_ref[:, pl.ds(off, tk)]
            dq_add = jnp.zeros((tk, LANES), F32)
            for h in heads:
                qm = jnp.where(sel[h], qi, 0).astype(BF16)
                dom = jnp.where(sel[h], doi, 0).astype(BF16)
                st = lax.dot_general(kj, qm, nt_dims, preferred_element_type=F32) * scale
                st = st - ckh[h] - lrow[h:h + 1, :]
                if masked:
                    st = jnp.where(causal, st, NEG_INF)
                pt = jnp.exp(st)
                dpt = lax.dot_general(vj, dom, nt_dims, preferred_element_type=F32)
                dst = pt * (dpt - drow[h:h + 1, :])
                ptb = pt.astype(BF16)
                dstb = dst.astype(BF16)
                dv_s[...] += jnp.dot(ptb, dom, preferred_element_type=F32)
                dk_s[...] += jnp.dot(dstb, qm, preferred_element_type=F32)
                dq_add = dq_add + lax.dot_general(dstb, km[h], tn_dims, preferred_element_type=F32)
                dck_s[:, h:h + 1] -= jnp.sum(dst, axis=-1, keepdims=True)
                dcq_ref[h:h + 1, pl.ds(off, tk)] += jnp.sum(dst, axis=0, keepdims=True)
            dq_ref[pl.ds(off, tk), :] += dq_add * scale
            return carry

        dk_s[...] = jnp.zeros_like(dk_s)
        dv_s[...] = jnp.zeros_like(dv_s)
        dck_s[...] = jnp.zeros_like(dck_s)
        carry = 0
        carry = q_step(j, carry, True)
        carry = lax.fori_loop(j + 1, both, lambda i, c: q_step(i, c, False), carry)
        for h in range(2):
            carry = lax.fori_loop(both, lasts[h], lambda i, c, h=h: q_step(i, c, False, (h,)), carry)
        dk_acc, dv_acc = dk_s[...], dv_s[...]
        dck = (dck_s[:, 0:1], dck_s[:, 1:2])
        dk_ref[...] = (dk_acc * scale).astype(dk_ref.dtype)
        dv_ref[...] = dv_acc.astype(dv_ref.dtype)
        dck_t = jnp.transpose(jnp.where(lo, dck[0], dck[1]))
        dck_ref[0:1, :] = dck_t[0:1, :]
        dck_ref[1:2, :] = dck_t[FOX_HEAD_DIM:FOX_HEAD_DIM + 1, :]

        @pl.when(j == nk - 1)
        def _():
            dq_out_ref[...] = dq_ref[...].astype(dq_out_ref.dtype)

    blk = lambda base: pl.BlockSpec((tk, LANES), lambda p, j, e: (j, base + p))
    full = lambda base: pl.BlockSpec((t, LANES), lambda p, j, e: (0, base + p))
    rows = pl.BlockSpec((None, 2, t), lambda p, j, e: (p, 0, 0))
    return pl.pallas_call(
        body, name=name,
        grid_spec=pltpu.PrefetchScalarGridSpec(
            num_scalar_prefetch=1, grid=(npair, nk),
            in_specs=[blk(npair), blk(2 * npair), full(0), full(0), rows, rows, blk(0)],
            out_specs=[full(0), blk(0), blk(0), pl.BlockSpec((None, 2, tk), lambda p, j, e: (p, 0, j)), rows],
            scratch_shapes=[pltpu.VMEM((t, LANES), F32), pltpu.VMEM((tk, LANES), F32), pltpu.VMEM((tk, LANES), F32),
                            pltpu.VMEM((tk, LANES), F32)]),
        out_shape=[jax.ShapeDtypeStruct((t, f), BF16), jax.ShapeDtypeStruct((t, f), BF16),
                   jax.ShapeDtypeStruct((t, f), BF16), jax.ShapeDtypeStruct((npair, 2, t), F32),
                   jax.ShapeDtypeStruct((npair, 2, t), F32)],
        compiler_params=_cparams(("parallel", "arbitrary")),
    )(end, qkv, qkv, qkv, do, lt, dt, cke)


ATTN_TILE = 512
ATTN_FWD_QUERIES = 512
EXP_ZERO = -104.0
BOUND_SLACK = 1.02


def _attn_row_stats(qkv, *, name):
    t = qkv.shape[0]
    f = qkv.shape[1] // 3
    tt = _tile(t, 512)

    def body(q_ref, k_ref, s_ref):
        q = q_ref[...].astype(F32)
        k = k_ref[...].astype(F32)
        chan = lax.broadcasted_iota(jnp.int32, (f, LANES), 0) // FOX_HEAD_DIM
        lane = lax.broadcasted_iota(jnp.int32, (f, LANES), 1)
        acc = jnp.zeros((tt, LANES), F32)
        for off, val in ((0, q * q), (FOX_HEADS, q * k), (2 * FOX_HEADS, k * k)):
            pick = (chan == lane - off).astype(BF16)
            acc = acc + jnp.dot(val.astype(BF16), pick, preferred_element_type=F32)
        s_ref[...] = acc

    return pl.pallas_call(
        body, name=name, grid=(t // tt,),
        in_specs=[pl.BlockSpec((tt, f), lambda i: (i, 0)), pl.BlockSpec((tt, f), lambda i: (i, 1))],
        out_specs=pl.BlockSpec((tt, LANES), lambda i: (i, 0)),
        out_shape=jax.ShapeDtypeStruct((t, LANES), F32),
        compiler_params=_cparams(("parallel",)),
    )(qkv, qkv)


def _attn_skip_tables(stats, cum16, tile):
    t = stats.shape[0]
    nb = t // tile
    scale = 1.0 / math.sqrt(FOX_HEAD_DIM)
    qn = jnp.sqrt(stats[:, :FOX_HEADS]) * scale
    sii = stats[:, FOX_HEADS:2 * FOX_HEADS] * scale - cum16
    kmax = jnp.max(jnp.sqrt(stats[:, 2 * FOX_HEADS:3 * FOX_HEADS]), axis=0, keepdims=True)
    arow = qn * kmax * BOUND_SLACK - sii + 0.5 * BOUND_SLACK
    a_blk = jnp.max(arow.reshape(nb, tile, FOX_HEADS), axis=1)
    c_blk = -cum16.reshape(nb, tile, FOX_HEADS)[:, tile - 1, :]
    dead = (a_blk[:, None, :] + c_blk[None, :, :]) < EXP_ZERO
    start_h = jnp.sum(dead.astype(jnp.int32), axis=1)
    blk = jnp.arange(nb, dtype=jnp.int32)
    start = jnp.minimum(start_h, blk[:, None]).T
    needs = start[:, :, None] <= blk[None, None, :]
    end = jnp.max(jnp.where(needs, blk[None, :, None] + 1, 0), axis=1)
    return start, jnp.maximum(end, blk[None, :] + 1)


def _fox_post_bwd(dy, o, gate, *, name):
    t, f = dy.shape
    tt = _tile(t, 512)

    def body(dy_ref, o_ref, g_ref, do_ref, dg_ref, dl_ref):
        g = g_ref[...]
        sg = _sigmoid(g)
        dyv = dy_ref[...]
        ov = o_ref[...]
        do = dyv * (g * sg)
        do_ref[...] = do.astype(do_ref.dtype)
        dg_ref[...] = (dyv * ov * (sg * (1.0 + g * (1.0 - sg)))).astype(dg_ref.dtype)
        chan = lax.broadcasted_iota(jnp.int32, (f, LANES), 0)
        head = lax.broadcasted_iota(jnp.int32, (f, LANES), 1)
        pick = (chan // FOX_HEAD_DIM == head).astype(BF16)
        dl_ref[...] = _dot_01_right(do * ov, pick)

    blk = pl.BlockSpec((tt, f), lambda i: (i, 0))
    return pl.pallas_call(
        body, name=name, grid=(t // tt,),
        in_specs=[blk, blk, blk], out_specs=[blk, blk, pl.BlockSpec((tt, LANES), lambda i: (i, 0))],
        out_shape=[jax.ShapeDtypeStruct((t, f), BF16), jax.ShapeDtypeStruct((t, f), BF16),
                   jax.ShapeDtypeStruct((t, LANES), F32)],
        compiler_params=_cparams(("parallel",)),
    )(dy, o, gate)


def _adamw(w, g, m, v, *, name):
    _, r, c = w.shape
    tr = _tile(r, 256) if r % SUBLANES == 0 else r
    c1 = 1.0 - ADAM_B1 ** ADAM_STEP
    c2 = 1.0 - ADAM_B2 ** ADAM_STEP

    def body(w_ref, g_ref, m_ref, v_ref, go_ref, d_ref, mo_ref, vo_ref):
        gv = g_ref[...]
        go_ref[...] = gv
        mn = ADAM_B1 * m_ref[...] + (1.0 - ADAM_B1) * gv
        vn = ADAM_B2 * v_ref[...] + (1.0 - ADAM_B2) * (gv * gv)
        mo_ref[...] = mn
        vo_ref[...] = vn
        d_ref[...] = -ADAM_LR * ((mn / c1) / (jnp.sqrt(vn / c2) + ADAM_EPS) + ADAM_WD * w_ref[...])

    blk = pl.BlockSpec((None, tr, c), lambda i: (0, i, 0))
    return pl.pallas_call(
        body, name=name, grid=(r // tr,), in_specs=[blk] * 4, out_specs=[blk] * 4,
        out_shape=[jax.ShapeDtypeStruct((1, r, c), F32)] * 4,
        compiler_params=_cparams(("parallel",)),
    )(w, g, m, v)


def _sum_slots(land, *, name):
    ns, r, c = land.shape
    tr = _tile(r, 64) if r % SUBLANES == 0 else r

    def body(l_ref, o_ref):
        acc = l_ref[0].astype(F32)
        for s in range(1, ns):
            acc = acc + l_ref[s].astype(F32)
        o_ref[...] = acc

    return pl.pallas_call(
        body, name=name, grid=(r // tr,),
        in_specs=[pl.BlockSpec((ns, tr, c), lambda i: (0, i, 0))],
        out_specs=pl.BlockSpec((tr, c), lambda i: (i, 0)),
        out_shape=jax.ShapeDtypeStruct((r, c), F32),
        compiler_params=_cparams(("parallel",)),
    )(land)


ANY = pl.BlockSpec(memory_space=pl.ANY)


def _flip(v, bit):
    return 1 - v if bit else v


def _gather_chips(shards, small, *, name):
    n = len(shards)
    rels = ((1, 0), (0, 1), (1, 1))

    def body(*refs):
        ins, small_in = refs[:n], refs[n]
        outs, small_out = refs[n + 1:2 * n + 1], refs[2 * n + 1]
        send, recv, loc = refs[2 * n + 2:]
        x, y, c = lax.axis_index("x"), lax.axis_index("y"), lax.axis_index("c")
        me = 2 * x + y
        sibling = (x, y, 1 - c)
        local = [pltpu.make_async_copy(ins[k], outs[k].at[me], loc.at[k]) for k in range(n)]
        local.append(pltpu.make_async_copy(small_in, small_out.at[me], loc.at[n]))
        for cp in local:
            cp.start()

        def rows(k):
            half = ins[k].shape[0] // 2
            return pl.ds(pl.multiple_of(c * half, SUBLANES), half)

        sends = []
        for r, (rx, ry) in enumerate(rels):
            to = (_flip(x, rx), _flip(y, ry), c)
            for k in range(n):
                cp = pltpu.make_async_remote_copy(
                    src_ref=ins[k].at[rows(k), :], dst_ref=outs[k].at[me, rows(k), :],
                    send_sem=send.at[r * n + k], recv_sem=recv.at[r * n + k], device_id=to, device_id_type=MESH)
                cp.start()
                sends.append(cp)
            cp = pltpu.make_async_remote_copy(
                src_ref=small_in, dst_ref=small_out.at[me], send_sem=send.at[6 * n + r], recv_sem=recv.at[6 * n + r],
                device_id=to, device_id_type=MESH)
            cp.start()
            sends.append(cp)
        for r, (rx, ry) in enumerate(rels):
            src_chip = 2 * _flip(x, rx) + _flip(y, ry)
            for k in range(n):
                landed = outs[k].at[src_chip, rows(k), :]
                sends[r * (n + 1) + k].wait_recv()
                cp = pltpu.make_async_remote_copy(
                    src_ref=landed, dst_ref=landed, send_sem=send.at[3 * n + r * n + k],
                    recv_sem=recv.at[3 * n + r * n + k], device_id=sibling, device_id_type=MESH)
                cp.start()
                sends.append(cp)
            sends[r * (n + 1) + n].wait_recv()
        for cp in sends[:3 * (n + 1)]:
            cp.wait_send()
        for cp in sends[3 * (n + 1):]:
            cp.wait()
        for cp in local:
            cp.wait()

    vmem = pl.BlockSpec(memory_space=pltpu.VMEM)
    return pl.pallas_call(
        body, name=name, in_specs=[vmem] * (n + 1), out_specs=[vmem] * (n + 1),
        out_shape=[jax.ShapeDtypeStruct((N_CHIPS,) + s.shape, s.dtype) for s in list(shards) + [small]],
        scratch_shapes=[pltpu.SemaphoreType.DMA((6 * n + 3,)), pltpu.SemaphoreType.DMA((6 * n + 3,)),
                        pltpu.SemaphoreType.DMA((n + 1,))],
        compiler_params=pltpu.CompilerParams(has_side_effects=True, vmem_limit_bytes=VMEM_LIMIT),
    )(*shards, small)


_RELS7 = tuple((r >> 2 & 1, r >> 1 & 1, r & 1) for r in range(1, N_DEV))


def _scatter_copies(ins, outs, send, recv, loc):
    n = len(ins)
    x, y, c = lax.axis_index("x"), lax.axis_index("y"), lax.axis_index("c")
    me = 4 * x + 2 * y + c

    def piece(k, px, py, pc):
        half = ins[k].shape[1] // 2
        return ins[k].at[2 * px + py, pl.ds(pc * half, half), :]

    copies = [pltpu.make_async_copy(piece(k, x, y, c), outs[k].at[me], loc.at[k]) for k in range(n)]
    for r, (rx, ry, rc) in enumerate(_RELS7):
        tx, ty, tc = _flip(x, rx), _flip(y, ry), _flip(c, rc)
        for k in range(n):
            copies.append(pltpu.make_async_remote_copy(
                src_ref=piece(k, tx, ty, tc), dst_ref=outs[k].at[me], send_sem=send.at[r * n + k],
                recv_sem=recv.at[r * n + k], device_id=(tx, ty, tc), device_id_type=MESH))
    return copies


def _scatter_shapes(grads):
    n = len(grads)
    out_shape = [jax.ShapeDtypeStruct((N_DEV, g.shape[1] // 2, g.shape[2]), g.dtype) for g in grads]
    sems = [pltpu.SemaphoreType.DMA((7 * n,)), pltpu.SemaphoreType.DMA((7 * n,)), pltpu.SemaphoreType.DMA((n,))]
    return out_shape, sems


def _join_cores(halves, *, name):
    n = len(halves)

    def body(*refs):
        ins, outs = refs[:n], refs[n:2 * n]
        send, recv, loc = refs[2 * n:]
        x, y, c = lax.axis_index("x"), lax.axis_index("y"), lax.axis_index("c")
        copies = []
        for k in range(n):
            half = ins[k].shape[0]
            mine = outs[k].at[0, pl.ds(c * half, half), :]
            cp = pltpu.make_async_copy(ins[k], mine, loc.at[k])
            cp.start()
            copies.append(cp)
            cp = pltpu.make_async_remote_copy(
                src_ref=ins[k], dst_ref=mine, send_sem=send.at[k], recv_sem=recv.at[k],
                device_id=(x, y, 1 - c), device_id_type=MESH)
            cp.start()
            copies.append(cp)
        for cp in copies:
            cp.wait()

    in_vmem = pl.BlockSpec(memory_space=pltpu.VMEM)
    return pl.pallas_call(
        body, name=name, in_specs=[in_vmem] * n, out_specs=[in_vmem] * n,
        out_shape=[jax.ShapeDtypeStruct((1, 2 * h.shape[0], h.shape[1]), h.dtype) for h in halves],
        scratch_shapes=[pltpu.SemaphoreType.DMA((n,)), pltpu.SemaphoreType.DMA((n,)), pltpu.SemaphoreType.DMA((n,))],
        compiler_params=pltpu.CompilerParams(has_side_effects=True, vmem_limit_bytes=VMEM_LIMIT),
    )(*halves)


def _allreduce_small(buf, *, name):
    r, n = buf.shape
    half = r // 2
    rels = ((1, 0), (0, 1), (1, 1))

    def body(in_ref, out_ref, sib_ref, chips_ref, send, recv):
        x, y, c = lax.axis_index("x"), lax.axis_index("y"), lax.axis_index("c")
        sibling = (x, y, 1 - c)
        chip = 2 * x + y
        rows = pl.ds(pl.multiple_of(c * half, SUBLANES), half)

        swap = pltpu.make_async_remote_copy(src_ref=in_ref, dst_ref=sib_ref, send_sem=send.at[0], recv_sem=recv.at[0],
                                            device_id=sibling, device_id_type=MESH)
        swap.start()
        swap.wait()
        chips_ref[chip] = in_ref[rows, :] + sib_ref[rows, :]

        sends = []
        for k, (rx, ry) in enumerate(rels):
            cp = pltpu.make_async_remote_copy(
                src_ref=chips_ref.at[chip], dst_ref=chips_ref.at[chip], send_sem=send.at[1 + k],
                recv_sem=recv.at[1 + k], device_id=(_flip(x, rx), _flip(y, ry), c), device_id_type=MESH)
            cp.start()
            sends.append(cp)
        for cp in sends:
            cp.wait()
        total = chips_ref[0]
        for s in range(1, N_CHIPS):
            total = total + chips_ref[s]
        out_ref[rows, :] = total

        back = pltpu.make_async_remote_copy(src_ref=out_ref.at[rows, :], dst_ref=out_ref.at[rows, :],
                                            send_sem=send.at[4], recv_sem=recv.at[4],
                                            device_id=sibling, device_id_type=MESH)
        back.start()
        back.wait()

    vmem = pl.BlockSpec(memory_space=pltpu.VMEM)
    return pl.pallas_call(
        body, name=name, in_specs=[vmem], out_specs=vmem,
        out_shape=jax.ShapeDtypeStruct((r, n), F32),
        scratch_shapes=[pltpu.VMEM((r, n), F32), pltpu.VMEM((N_CHIPS, half, n), F32),
                        pltpu.SemaphoreType.DMA((5,)), pltpu.SemaphoreType.DMA((5,))],
        compiler_params=pltpu.CompilerParams(has_side_effects=True, vmem_limit_bytes=VMEM_LIMIT),
    )(buf)


def _pack(arrs):
    flat = []
    for a in arrs:
        v = a.reshape(-1)
        pad = (-v.shape[0]) % LANES
        if pad:
            v = jnp.pad(v, (0, pad))
        flat.append(v)
    v = jnp.concatenate(flat)
    pad = (-v.shape[0]) % (LANES * SUBLANES)
    if pad:
        v = jnp.pad(v, (0, pad))
    return v.reshape(-1, LANES)


def _unpack(buf, shapes):
    v = buf.reshape(-1)
    out, off = [], 0
    for s in shapes:
        n = math.prod(s)
        out.append(v[off:off + n].reshape(s))
        off += n + (-n) % LANES
    return out


def kernel(x, norm_g, final_g, lru_w_in, lru_conv_w, lru_conv_b, lru_wa, lru_ba, lru_wx, lru_bx, lru_a_param, lru_w_out, fox_w_in, fox_b_f, fox_w_out, loss_target, m_norm_g, m_final_g, m_lru_w_in, m_lru_conv_w, m_lru_conv_b, m_lru_wa, m_lru_ba, m_lru_wx, m_lru_bx, m_lru_a_param, m_lru_w_out, m_fox_w_in, m_fox_b_f, m_fox_w_out, v_norm_g, v_final_g, v_lru_w_in, v_lru_conv_w, v_lru_conv_b, v_lru_wa, v_lru_ba, v_lru_wx, v_lru_bx, v_lru_a_param, v_lru_w_out, v_fox_w_in, v_fox_b_f, v_fox_w_out):
    t, d = x.shape[1], x.shape[2]
    w = lru_wa.shape[1] * LRU_BLOCK_W
    f = FOX_HEADS * FOX_HEAD_DIM
    npair = f // LANES
    x0 = x.reshape(t, d)
    tgt = loss_target.reshape(t, d)
    chip = 2 * lax.axis_index("x") + lax.axis_index("y")

    g_lwi, g_lwo, g_cw = _gather_chips(
        [lru_w_in[0].astype(BF16), lru_w_out[0].astype(BF16)], lru_conv_w[0], name="gather_weights")
    cg = w // 2
    lwi = jnp.concatenate([g_lwi[0], g_lwi[2], g_lwi[1], g_lwi[3]], axis=1)
    lwo = g_lwo.reshape(w, d)
    conv_w = jnp.concatenate([g_cw[s] for s in range(N_CHIPS)], axis=1)
    conv_b, ba, bx, a_param = lru_conv_b, lru_ba, lru_bx, lru_a_param
    wa, wx = lru_wa[0], lru_wx[0]
    b_f = jnp.pad(fox_b_f, ((0, 0), (0, LANES - FOX_HEADS)))

    h0 = _rmsnorm(x0, norm_g[0], name="norm0")
    u = _matmul(h0, lwi, tn=w, name="lru_in")
    y1, hs, (g_fwi, g_fwo) = _lru_fwd(u, conv_w, conv_b, wa, ba, wx, bx, a_param, cg=cg, name="lru_fwd",
                                      ride=[fox_w_in[0].astype(BF16), fox_w_out[0].astype(BF16)])
    fwi = jnp.concatenate([g_fwi[s] for s in range(N_CHIPS)], axis=1)
    w_qkv, w_g2 = fwi[:, :3 * f], fwi[:, 3 * f:4 * f]
    w_f = jnp.pad(fwi[:, 4 * f:], ((0, 0), (0, LANES - FOX_HEADS)))
    fwo = g_fwo.reshape(f, d)
    x1 = _matmul(y1, lwo, add=x0, name="lru_out")
    h1 = _rmsnorm(x1, norm_g[1], name="norm1")
    qkv = _matmul(h1, w_qkv, out_dtype=BF16, tn=3 * f // 2, name="fox_qkv")
    gate2 = _matmul(h1, w_g2, name="fox_gate")
    flog = _matmul(h1, w_f, name="fox_f")
    cum, cke = _fgate_fwd(flog, b_f, name="fgate_fwd")
    cum16 = cum[:, :FOX_HEADS]
    ckt = cum16.T.reshape(npair, 2, t)
    a_tk, a_tq = _tile(t, ATTN_TILE), _tile(t, ATTN_FWD_QUERIES)
    a_start, a_end = _attn_skip_tables(_attn_row_stats(qkv, name="attn_row_stats"), cum16, a_tk)
    a_start_fwd = jnp.min(a_start.reshape(FOX_HEADS, t // a_tq, a_tq // a_tk), axis=2)
    o, y2, lse = _attn_fwd(a_start_fwd, qkv, ckt, gate2, name="attn_fwd", tq=a_tq, tk=a_tk)
    x2 = _matmul(y2, fwo, add=x1, name="fox_out")
    lsum, dx2, dx2_b, dgf = _final_loss(x2, tgt, final_g, name="final_loss")
    loss = lax.psum(0.5 * jnp.sum(lsum) / d, ("x", "y", "c"))

    d_fwo = _matmul(y2, dx2_b, ta=True, out_dtype=BF16, name="d_fox_w_out")
    dy2 = _matmul(dx2_b, fwo, tb=True, name="d_y2")
    do, dgate2, dl = _fox_post_bwd(dy2, o, gate2, name="fox_post_bwd")
    lt = lse
    dt = dl[:, :FOX_HEADS].T.reshape(npair, 2, t)
    dq, dk, dv, dck, dcq = _attn_bwd(a_end, qkv, do, lt, dt, cke, name="attn_bwd")
    dcum = jnp.pad((dck + dcq).reshape(FOX_HEADS, t).T, ((0, 0), (0, LANES - FOX_HEADS)))
    dflog, db_f = _fgate_bwd(dcum, flog, b_f, name="fgate_bwd")
    du2 = [dq, dk, dv, dgate2]
    dflog_b = dflog.astype(BF16)
    dh1 = _matmul_kparts(du2, fwi[:, :4 * f], chunk=f, name="d_h1_a")
    dh1 = _matmul(dflog_b, w_f, tb=True, add=dh1, name="d_h1_b")
    d_fwi_a = _matmul_nparts(h1, du2, chunk=f, out_dtype=BF16, name="d_fox_w_in_a")
    d_fwi_b = _matmul(h1, dflog_b, ta=True, out_dtype=BF16, name="d_fox_w_in_b")
    d_fwi = jnp.concatenate([d_fwi_a, d_fwi_b[:, :FOX_HEADS]], axis=1)
    dx1, dx1_b, dg1 = _rmsnorm_bwd(dh1, x1, norm_g[1], dx2, name="norm1_bwd", bf16_copy=True)

    d_lwo = _matmul(y1, dx1_b, ta=True, out_dtype=BF16, tm=w, name="d_lru_w_out")
    dy1 = _matmul(dx1_b, lwo, tb=True, tn=w, name="d_y1")
    n_fwi = fox_w_in.shape[2]
    g_fwi4 = jnp.stack([d_fwi[:, s * n_fwi:(s + 1) * n_fwi] for s in range(N_CHIPS)])
    g_fwo4 = d_fwo.reshape(N_CHIPS, f // N_CHIPS, d)
    g_lwo4 = d_lwo.reshape(N_CHIPS, w // N_CHIPS, d)
    (dxb, dgate, d_cw, d_cb, d_wa, d_ba, d_wx, d_bx, d_ap), lands_early = _lru_bwd(
        u, hs, dy1, conv_w, conv_b, wa, ba, wx, bx, a_param, cg=cg, name="lru_bwd", ride=[g_lwo4, g_fwi4, g_fwo4])
    d_lwi_p = _matmul_nparts(h0, [dxb, dgate], chunk=cg, out_dtype=BF16, name="d_lru_w_in")
    csz = cg
    g_lwi4 = jnp.stack([d_lwi_p[:, 0:csz], d_lwi_p[:, 2 * csz:3 * csz], d_lwi_p[:, csz:2 * csz],
                        d_lwi_p[:, 3 * csz:]])
    dh0, lands_last = _matmul_kparts([dxb, dgate], lwi, chunk=cg, name="d_h0", ride=[g_lwi4])
    dx0, _, dg0 = _rmsnorm_bwd(dh0, x0, norm_g[0], dx1, name="norm0_bwd", bf16_copy=False)
    lands = list(lands_last) + list(lands_early)
    halves = [_sum_slots(l, name="sum_" + nm) for l, nm in zip(lands, ("lru_w_in", "lru_w_out", "fox_w_in", "fox_w_out"))]
    big_g = _join_cores(halves, name="join_cores")

    small_g = [jnp.concatenate([dg0, dg1], axis=0), dgf.reshape(d), d_cw, d_cb, d_wa, d_ba, d_wx, d_bx, d_ap,
               db_f[:, :FOX_HEADS]]
    gsum = _allreduce_small(_pack(small_g), name="allreduce_small")
    zc = jnp.zeros((CONV_WIDTH, w), F32)
    pk_w = _pack([norm_g, final_g, zc, lru_conv_b, lru_wa, lru_ba, lru_wx, lru_bx, lru_a_param, fox_b_f])
    pk_m = _pack([m_norm_g, m_final_g, zc, m_lru_conv_b, m_lru_wa, m_lru_ba, m_lru_wx, m_lru_bx, m_lru_a_param,
                  m_fox_b_f])
    pk_v = _pack([v_norm_g, v_final_g, zc + 1.0, v_lru_conv_b, v_lru_wa, v_lru_ba, v_lru_wx, v_lru_bx,
                  v_lru_a_param, v_fox_b_f])
    s_g, s_delta, s_m, s_v = _adamw(pk_w[None], gsum[None], pk_m[None], pk_v[None], name="adamw_small")
    out_shapes = [norm_g.shape, final_g.shape, (CONV_WIDTH, w), lru_conv_b.shape, lru_wa.shape, lru_ba.shape,
                  lru_wx.shape, lru_bx.shape, lru_a_param.shape, fox_b_f.shape]
    sg = _unpack(s_g, out_shapes)
    sd = _unpack(s_delta, out_shapes)
    sm = _unpack(s_m, out_shapes)
    sv = _unpack(s_v, out_shapes)

    ncw = lru_conv_w.shape[2]
    g_cw_loc = lax.dynamic_slice_in_dim(sg[2], chip * ncw, ncw, axis=1)
    g_cw_loc, cw_d, cw_m, cw_v = _adamw(lru_conv_w, g_cw_loc[None], m_lru_conv_w, v_lru_conv_w, name="adamw_conv_w")

    big = []
    for nm, wt, g, mm, vv in (("lru_w_in", lru_w_in, big_g[0], m_lru_w_in, v_lru_w_in),
                              ("lru_w_out", lru_w_out, big_g[1], m_lru_w_out, v_lru_w_out),
                              ("fox_w_in", fox_w_in, big_g[2], m_fox_w_in, v_fox_w_in),
                              ("fox_w_out", fox_w_out, big_g[3], m_fox_w_out, v_fox_w_out)):
        big.append(tuple(_adamw(wt, g, mm, vv, name="adamw_" + nm)))

    def assemble(idx):
        small = (sg, sd, sm, sv)[idx]
        cw = (g_cw_loc, cw_d, cw_m, cw_v)[idx]
        return [small[0], small[1], big[0][idx], cw, small[3], small[4], small[5], small[6], small[7], small[8],
                big[1][idx], big[2][idx], small[9], big[3][idx]]

    grad_x = dx0.reshape(1, t, d)
    return (loss, grad_x, *assemble(0), *assemble(1), *assemble(2), *assemble(3))
```
